```python
import jax
import jax.numpy as jnp
from jax import lax

D_MODEL = 1024
BATCH = 8
SEQ = 4096
DEPTH = 2

D_RNN = D_MODEL
D_POOL = D_MODEL
D_MIX = D_RNN + D_POOL
N_RNN_HEADS = 8
RNN_HEAD_DIM = D_RNN // N_RNN_HEADS
CONV_WIDTH = 4
LRU_C = 8.0
POOL_WINDOWS = (2, 4, 8, 16)
N_POOL_GROUPS = len(POOL_WINDOWS)
POOL_GROUP_DIM = D_POOL // N_POOL_GROUPS
NORM_EPS = 1e-6

kernel_name = "hybrid_rglru_multiscale_pool_parallel_heads"


def rmsnorm(x, g):
    xf = x.astype(jnp.float32)
    y = xf * lax.rsqrt(jnp.mean(xf * xf, axis=-1, keepdims=True) + NORM_EPS)
    return (y * g.astype(jnp.float32)).astype(x.dtype)


def causal_depthwise_conv(x, w, b):
    y = lax.conv_general_dilated(
        x, w[:, None, :].astype(x.dtype), window_strides=(1,),
        padding=[(CONV_WIDTH - 1, 0)],
        dimension_numbers=("NWC", "WIO", "NWC"),
        feature_group_count=x.shape[-1])
    return y + b


def rg_lru(x, w_a, b_a, w_x, b_x, lam):
    B, S, _ = x.shape
    xh = x.reshape(B, S, N_RNN_HEADS, RNN_HEAD_DIM)
    r = jax.nn.sigmoid(jnp.einsum("bshi,hij->bshj", xh, w_a) + b_a).reshape(B, S, D_RNN)
    i = jax.nn.sigmoid(jnp.einsum("bshi,hij->bshj", xh, w_x) + b_x).reshape(B, S, D_RNN)
    log_a = -LRU_C * r.astype(jnp.float32) * jax.nn.softplus(-lam.astype(jnp.float32))
    a = jnp.exp(log_a)
    mult = jnp.sqrt(-jnp.expm1(2.0 * log_a))
    u = mult * (i * x).astype(jnp.float32)

    def step(h, inp):
        a_t, u_t = inp
        h = a_t * h + u_t
        return h, h

    h0 = jnp.zeros((B, D_RNN), jnp.float32)
    _, hs = lax.scan(step, h0, (jnp.swapaxes(a, 0, 1), jnp.swapaxes(u, 0, 1)))
    return jnp.swapaxes(hs, 0, 1).astype(x.dtype)


def multi_scale_pool(x, w, b, scale):
    B, S, _ = x.shape
    xg = x.reshape(B, S, N_POOL_GROUPS, POOL_GROUP_DIM).astype(jnp.float32)
    cs = jnp.cumsum(xg, axis=1)
    t = jnp.arange(S)
    means = []
    for g, win in enumerate(POOL_WINDOWS):
        csg = cs[:, :, g]
        lagged = jnp.pad(csg, ((0, 0), (win, 0), (0, 0)))[:, :S]
        count = jnp.minimum(t + 1, win).astype(jnp.float32)[None, :, None]
        means.append((csg - lagged) / count)
    pooled = (jnp.stack(means, axis=2) - xg).astype(x.dtype)
    y = jnp.einsum("bsgi,gij->bsgj", pooled, w) + b
    return y.reshape(B, S, D_POOL) * scale


def _fwd_setup_inputs(seed: int = 0) -> dict:
    key = jax.random.key(seed)
    ks = jax.random.split(key, 24)
    f32 = jnp.float32
    nrm = lambda k, shape, s: jax.random.normal(k, shape, f32) * s
    L = DEPTH
    x = jax.random.normal(ks[0], (BATCH, SEQ, D_MODEL), f32)
    c = jax.random.normal(ks[1], (BATCH, D_MODEL), f32)
    ada_w = nrm(ks[2], (L, D_MODEL, 3 * D_MODEL), 0.5 * D_MODEL ** -0.5)
    ada_b = nrm(ks[3], (L, 3 * D_MODEL), 0.01)
    pre_norm_g = 1.0 + nrm(ks[4], (L, D_MODEL), 0.05)
    w_in = nrm(ks[5], (L, D_MODEL, 2 * D_MIX), D_MODEL ** -0.5)
    conv_w = nrm(ks[6], (L, CONV_WIDTH, D_RNN), CONV_WIDTH ** -0.5)
    conv_b = nrm(ks[7], (L, D_RNN), 0.01)
    gate_a_w = nrm(ks[8], (L, N_RNN_HEADS, RNN_HEAD_DIM, RNN_HEAD_DIM), RNN_HEAD_DIM ** -0.5)
    gate_a_b = nrm(ks[9], (L, N_RNN_HEADS, RNN_HEAD_DIM), 0.01)
    gate_x_w = nrm(ks[10], (L, N_RNN_HEADS, RNN_HEAD_DIM, RNN_HEAD_DIM), RNN_HEAD_DIM ** -0.5)
    gate_x_b = nrm(ks[11], (L, N_RNN_HEADS, RNN_HEAD_DIM), 0.01)
    a_c = jax.random.uniform(ks[12], (L, D_RNN), f32, 0.9, 0.999)
    a0 = a_c ** (1.0 / LRU_C)
    lru_lambda = jnp.log(a0) - jnp.log1p(-a0)
    pool_w = nrm(ks[13], (L, N_POOL_GROUPS, POOL_GROUP_DIM, POOL_GROUP_DIM), POOL_GROUP_DIM ** -0.5)
    pool_b = nrm(ks[14], (L, N_POOL_GROUPS, POOL_GROUP_DIM), 0.01)
    pool_scale = jax.random.uniform(ks[15], (L, D_POOL), f32, 0.5, 1.5)
    w_out = nrm(ks[16], (L, D_MIX, D_MODEL), D_MIX ** -0.5)
    post_norm_g = 1.0 + nrm(ks[17], (L, D_MODEL), 0.05)
    return {"x": x, "c": c, "ada_w": ada_w, "ada_b": ada_b, "pre_norm_g": pre_norm_g,
            "w_in": w_in, "conv_w": conv_w, "conv_b": conv_b,
            "gate_a_w": gate_a_w, "gate_a_b": gate_a_b, "gate_x_w": gate_x_w, "gate_x_b": gate_x_b,
            "lru_lambda": lru_lambda, "pool_w": pool_w, "pool_b": pool_b, "pool_scale": pool_scale,
            "w_out": w_out, "post_norm_g": post_norm_g}


def _fwd_reference(x, c, ada_w, ada_b, pre_norm_g, w_in, conv_w, conv_b,
              gate_a_w, gate_a_b, gate_x_w, gate_x_b, lru_lambda,
              pool_w, pool_b, pool_scale, w_out, post_norm_g):
    c_act = jax.nn.silu(c)
    for l in range(DEPTH):
        mod = c_act @ ada_w[l] + ada_b[l]
        shift, scale, gate = jnp.split(mod, 3, axis=-1)
        h = rmsnorm(x, pre_norm_g[l]) * (1.0 + scale[:, None, :]) + shift[:, None, :]
        proj = h @ w_in[l]
        x_rnn, g_rnn, x_pool, g_pool = jnp.split(
            proj, [D_RNN, 2 * D_RNN, 2 * D_RNN + D_POOL], axis=-1)
        u = causal_depthwise_conv(x_rnn, conv_w[l], conv_b[l])
        y_rnn = rg_lru(u, gate_a_w[l], gate_a_b[l], gate_x_w[l], gate_x_b[l],
                       lru_lambda[l]) * jax.nn.silu(g_rnn)
        y_pool = multi_scale_pool(x_pool, pool_w[l], pool_b[l], pool_scale[l]) * jax.nn.silu(g_pool)
        y = jnp.concatenate([y_rnn, y_pool], axis=-1) @ w_out[l]
        x = x + gate[:, None, :] * rmsnorm(y, post_norm_g[l])
    return x


import jax as _jax
import jax.numpy as _jnp

TWIN_FORMAT = 'train_step'
FWD_PARAMS = ['x', 'c', 'ada_w', 'ada_b', 'pre_norm_g', 'w_in', 'conv_w', 'conv_b', 'gate_a_w', 'gate_a_b', 'gate_x_w', 'gate_x_b', 'lru_lambda', 'pool_w', 'pool_b', 'pool_scale', 'w_out', 'post_norm_g']
TWIN_WEIGHTS = ['ada_w', 'ada_b', 'pre_norm_g', 'w_in', 'conv_w', 'conv_b', 'gate_a_w', 'gate_a_b', 'gate_x_w', 'gate_x_b', 'lru_lambda', 'pool_w', 'pool_b', 'pool_scale', 'w_out', 'post_norm_g']
TWIN_DIFF_INPUT = 'x'
TWIN_INPUTS = ['x', 'c', 'ada_w', 'ada_b', 'pre_norm_g', 'w_in', 'conv_w', 'conv_b', 'gate_a_w', 'gate_a_b', 'gate_x_w', 'gate_x_b', 'lru_lambda', 'pool_w', 'pool_b', 'pool_scale', 'w_out', 'post_norm_g', 'loss_target', 'm_ada_w', 'm_ada_b', 'm_pre_norm_g', 'm_w_in', 'm_conv_w', 'm_conv_b', 'm_gate_a_w', 'm_gate_a_b', 'm_gate_x_w', 'm_gate_x_b', 'm_lru_lambda', 'm_pool_w', 'm_pool_b', 'm_pool_scale', 'm_w_out', 'm_post_norm_g', 'v_ada_w', 'v_ada_b', 'v_pre_norm_g', 'v_w_in', 'v_conv_w', 'v_conv_b', 'v_gate_a_w', 'v_gate_a_b', 'v_gate_x_w', 'v_gate_x_b', 'v_lru_lambda', 'v_pool_w', 'v_pool_b', 'v_pool_scale', 'v_w_out', 'v_post_norm_g']
TWIN_OUTPUTS = ['loss', 'grad_x', 'grad_ada_w', 'grad_ada_b', 'grad_pre_norm_g', 'grad_w_in', 'grad_conv_w', 'grad_conv_b', 'grad_gate_a_w', 'grad_gate_a_b', 'grad_gate_x_w', 'grad_gate_x_b', 'grad_lru_lambda', 'grad_pool_w', 'grad_pool_b', 'grad_pool_scale', 'grad_w_out', 'grad_post_norm_g', 'delta_ada_w', 'delta_ada_b', 'delta_pre_norm_g', 'delta_w_in', 'delta_conv_w', 'delta_conv_b', 'delta_gate_a_w', 'delta_gate_a_b', 'delta_gate_x_w', 'delta_gate_x_b', 'delta_lru_lambda', 'delta_pool_w', 'delta_pool_b', 'delta_pool_scale', 'delta_w_out', 'delta_post_norm_g', 'new_m_ada_w', 'new_m_ada_b', 'new_m_pre_norm_g', 'new_m_w_in', 'new_m_conv_w', 'new_m_conv_b', 'new_m_gate_a_w', 'new_m_gate_a_b', 'new_m_gate_x_w', 'new_m_gate_x_b', 'new_m_lru_lambda', 'new_m_pool_w', 'new_m_pool_b', 'new_m_pool_scale', 'new_m_w_out', 'new_m_post_norm_g', 'new_v_ada_w', 'new_v_ada_b', 'new_v_pre_norm_g', 'new_v_w_in', 'new_v_conv_w', 'new_v_conv_b', 'new_v_gate_a_w', 'new_v_gate_a_b', 'new_v_gate_x_w', 'new_v_gate_x_b', 'new_v_lru_lambda', 'new_v_pool_w', 'new_v_pool_b', 'new_v_pool_scale', 'new_v_w_out', 'new_v_post_norm_g']
TWIN_LEAF_KINDS = {'loss': 'loss', 'grad_x': 'grad_x', 'grad_ada_w': 'grad_w', 'grad_ada_b': 'grad_w', 'grad_pre_norm_g': 'grad_w', 'grad_w_in': 'grad_w', 'grad_conv_w': 'grad_w', 'grad_conv_b': 'grad_w', 'grad_gate_a_w': 'grad_w', 'grad_gate_a_b': 'grad_w', 'grad_gate_x_w': 'grad_w', 'grad_gate_x_b': 'grad_w', 'grad_lru_lambda': 'grad_w', 'grad_pool_w': 'grad_w', 'grad_pool_b': 'grad_w', 'grad_pool_scale': 'grad_w', 'grad_w_out': 'grad_w', 'grad_post_norm_g': 'grad_w', 'delta_ada_w': 'delta_w', 'delta_ada_b': 'delta_w', 'delta_pre_norm_g': 'delta_w', 'delta_w_in': 'delta_w', 'delta_conv_w': 'delta_w', 'delta_conv_b': 'delta_w', 'delta_gate_a_w': 'delta_w', 'delta_gate_a_b': 'delta_w', 'delta_gate_x_w': 'delta_w', 'delta_gate_x_b': 'delta_w', 'delta_lru_lambda': 'delta_w', 'delta_pool_w': 'delta_w', 'delta_pool_b': 'delta_w', 'delta_pool_scale': 'delta_w', 'delta_w_out': 'delta_w', 'delta_post_norm_g': 'delta_w', 'new_m_ada_w': 'new_m', 'new_m_ada_b': 'new_m', 'new_m_pre_norm_g': 'new_m', 'new_m_w_in': 'new_m', 'new_m_conv_w': 'new_m', 'new_m_conv_b': 'new_m', 'new_m_gate_a_w': 'new_m', 'new_m_gate_a_b': 'new_m', 'new_m_gate_x_w': 'new_m', 'new_m_gate_x_b': 'new_m', 'new_m_lru_lambda': 'new_m', 'new_m_pool_w': 'new_m', 'new_m_pool_b': 'new_m', 'new_m_pool_scale': 'new_m', 'new_m_w_out': 'new_m', 'new_m_post_norm_g': 'new_m', 'new_v_ada_w': 'new_v', 'new_v_ada_b': 'new_v', 'new_v_pre_norm_g': 'new_v', 'new_v_w_in': 'new_v', 'new_v_conv_w': 'new_v', 'new_v_conv_b': 'new_v', 'new_v_gate_a_w': 'new_v', 'new_v_gate_a_b': 'new_v', 'new_v_gate_x_w': 'new_v', 'new_v_gate_x_b': 'new_v', 'new_v_lru_lambda': 'new_v', 'new_v_pool_w': 'new_v', 'new_v_pool_b': 'new_v', 'new_v_pool_scale': 'new_v', 'new_v_w_out': 'new_v', 'new_v_post_norm_g': 'new_v'}


def _forward(args):
    return _fwd_reference(*[args[k] for k in FWD_PARAMS])


def _output_shape():
    out = _jax.eval_shape(lambda: _forward(_fwd_setup_inputs(0)))
    return out.shape, out.dtype

N_MICROBATCH = 1
ADAM_LR = 0.001
ADAM_B1 = 0.9
ADAM_B2 = 0.999
ADAM_EPS = 1e-08
ADAM_WD = 0.01
ADAM_STEP = 10
PER_EXAMPLE_BATCH_AXIS = {'x': 0, 'c': 0, 'loss_target': 0}
SHARED_INPUTS = []
_WEIGHT_DTYPES = {'ada_w': _jnp.float32, 'ada_b': _jnp.float32, 'pre_norm_g': _jnp.float32, 'w_in': _jnp.float32, 'conv_w': _jnp.float32, 'conv_b': _jnp.float32, 'gate_a_w': _jnp.float32, 'gate_a_b': _jnp.float32, 'gate_x_w': _jnp.float32, 'gate_x_b': _jnp.float32, 'lru_lambda': _jnp.float32, 'pool_w': _jnp.float32, 'pool_b': _jnp.float32, 'pool_scale': _jnp.float32, 'w_out': _jnp.float32, 'post_norm_g': _jnp.float32}
MOMENT_SCALE = {'ada_w': 1.905964e+00, 'ada_b': 3.486385e+00, 'pre_norm_g': 1.403629e-01, 'w_in': 1.808005e-01, 'conv_w': 3.610719e-01, 'conv_b': 1.251969e+00, 'gate_a_w': 2.920910e-02, 'gate_a_b': 6.643655e-02, 'gate_x_w': 6.803002e-02, 'gate_x_b': 1.662005e-01, 'lru_lambda': 1.581211e-01, 'pool_w': 5.272712e-02, 'pool_b': 1.178630e-01, 'pool_scale': 5.177825e-02, 'w_out': 3.434316e-01, 'post_norm_g': 3.826512e+00}


def _to_microbatches(a, axis):
    t = _jnp.moveaxis(a, axis, 0)
    t = t.reshape((N_MICROBATCH, t.shape[0] // N_MICROBATCH) + t.shape[1:])
    return _jnp.moveaxis(t, 1, axis + 1)


def setup_inputs(seed: int = 0) -> dict:
    inp = _fwd_setup_inputs(seed)
    key = _jax.random.fold_in(_jax.random.key(seed), 7919)
    shape, _ = _output_shape()
    out = dict(inp)
    out["loss_target"] = _jax.random.normal(_jax.random.fold_in(key, 0), shape, _jnp.float32)
    for i, name in enumerate(TWIN_WEIGHTS):
        w = inp[name].astype(_jnp.float32)
        if MOMENT_SCALE is None:
            s = _jnp.sqrt(_jnp.mean(_jnp.square(w)) + 1e-30)
        else:
            s = MOMENT_SCALE[name]
        km, kv = _jax.random.split(_jax.random.fold_in(key, i + 1))
        out[name] = w
        out["m_" + name] = s * _jax.random.normal(km, w.shape, _jnp.float32)
        out["v_" + name] = (s * s) * _jax.random.uniform(kv, w.shape, _jnp.float32, 0.5, 1.5)
    if N_MICROBATCH > 1:
        for name, axis in PER_EXAMPLE_BATCH_AXIS.items():
            out[name] = _to_microbatches(out[name], axis)
    return {'x': out['x'], 'c': out['c'], 'ada_w': out['ada_w'], 'ada_b': out['ada_b'], 'pre_norm_g': out['pre_norm_g'], 'w_in': out['w_in'], 'conv_w': out['conv_w'], 'conv_b': out['conv_b'], 'gate_a_w': out['gate_a_w'], 'gate_a_b': out['gate_a_b'], 'gate_x_w': out['gate_x_w'], 'gate_x_b': out['gate_x_b'], 'lru_lambda': out['lru_lambda'], 'pool_w': out['pool_w'], 'pool_b': out['pool_b'], 'pool_scale': out['pool_scale'], 'w_out': out['w_out'], 'post_norm_g': out['post_norm_g'], 'loss_target': out['loss_target'], 'm_ada_w': out['m_ada_w'], 'm_ada_b': out['m_ada_b'], 'm_pre_norm_g': out['m_pre_norm_g'], 'm_w_in': out['m_w_in'], 'm_conv_w': out['m_conv_w'], 'm_conv_b': out['m_conv_b'], 'm_gate_a_w': out['m_gate_a_w'], 'm_gate_a_b': out['m_gate_a_b'], 'm_gate_x_w': out['m_gate_x_w'], 'm_gate_x_b': out['m_gate_x_b'], 'm_lru_lambda': out['m_lru_lambda'], 'm_pool_w': out['m_pool_w'], 'm_pool_b': out['m_pool_b'], 'm_pool_scale': out['m_pool_scale'], 'm_w_out': out['m_w_out'], 'm_post_norm_g': out['m_post_norm_g'], 'v_ada_w': out['v_ada_w'], 'v_ada_b': out['v_ada_b'], 'v_pre_norm_g': out['v_pre_norm_g'], 'v_w_in': out['v_w_in'], 'v_conv_w': out['v_conv_w'], 'v_conv_b': out['v_conv_b'], 'v_gate_a_w': out['v_gate_a_w'], 'v_gate_a_b': out['v_gate_a_b'], 'v_gate_x_w': out['v_gate_x_w'], 'v_gate_x_b': out['v_gate_x_b'], 'v_lru_lambda': out['v_lru_lambda'], 'v_pool_w': out['v_pool_w'], 'v_pool_b': out['v_pool_b'], 'v_pool_scale': out['v_pool_scale'], 'v_w_out': out['v_w_out'], 'v_post_norm_g': out['v_post_norm_g']}


def _loss(weights, diff, rest, loss_target):
    with _jax.named_scope("forward"):
        args = {**rest, TWIN_DIFF_INPUT: diff, **{k: w.astype(_WEIGHT_DTYPES[k]) for k, w in weights.items()}}
        y = _forward(args)
    with _jax.named_scope("loss_head"):
        err = _jnp.square(y.astype(_jnp.float32) - loss_target)
        return 0.5 * _jnp.sum(_jnp.mean(err, axis=-1)) if err.ndim else 0.5 * err


def _adamw(w, g, m, v):
    m = ADAM_B1 * m + (1.0 - ADAM_B1) * g
    v = ADAM_B2 * v + (1.0 - ADAM_B2) * _jnp.square(g)
    m_hat = m / (1.0 - ADAM_B1 ** ADAM_STEP)
    v_hat = v / (1.0 - ADAM_B2 ** ADAM_STEP)
    delta = -ADAM_LR * (m_hat / (_jnp.sqrt(v_hat) + ADAM_EPS) + ADAM_WD * w)
    return delta, m, v


def reference(x, c, ada_w, ada_b, pre_norm_g, w_in, conv_w, conv_b, gate_a_w, gate_a_b, gate_x_w, gate_x_b, lru_lambda, pool_w, pool_b, pool_scale, w_out, post_norm_g, loss_target, m_ada_w, m_ada_b, m_pre_norm_g, m_w_in, m_conv_w, m_conv_b, m_gate_a_w, m_gate_a_b, m_gate_x_w, m_gate_x_b, m_lru_lambda, m_pool_w, m_pool_b, m_pool_scale, m_w_out, m_post_norm_g, v_ada_w, v_ada_b, v_pre_norm_g, v_w_in, v_conv_w, v_conv_b, v_gate_a_w, v_gate_a_b, v_gate_x_w, v_gate_x_b, v_lru_lambda, v_pool_w, v_pool_b, v_pool_scale, v_w_out, v_post_norm_g):
    given = dict(x=x, c=c, ada_w=ada_w, ada_b=ada_b, pre_norm_g=pre_norm_g, w_in=w_in, conv_w=conv_w, conv_b=conv_b, gate_a_w=gate_a_w, gate_a_b=gate_a_b, gate_x_w=gate_x_w, gate_x_b=gate_x_b, lru_lambda=lru_lambda, pool_w=pool_w, pool_b=pool_b, pool_scale=pool_scale, w_out=w_out, post_norm_g=post_norm_g, loss_target=loss_target, m_ada_w=m_ada_w, m_ada_b=m_ada_b, m_pre_norm_g=m_pre_norm_g, m_w_in=m_w_in, m_conv_w=m_conv_w, m_conv_b=m_conv_b, m_gate_a_w=m_gate_a_w, m_gate_a_b=m_gate_a_b, m_gate_x_w=m_gate_x_w, m_gate_x_b=m_gate_x_b, m_lru_lambda=m_lru_lambda, m_pool_w=m_pool_w, m_pool_b=m_pool_b, m_pool_scale=m_pool_scale, m_w_out=m_w_out, m_post_norm_g=m_post_norm_g, v_ada_w=v_ada_w, v_ada_b=v_ada_b, v_pre_norm_g=v_pre_norm_g, v_w_in=v_w_in, v_conv_w=v_conv_w, v_conv_b=v_conv_b, v_gate_a_w=v_gate_a_w, v_gate_a_b=v_gate_a_b, v_gate_x_w=v_gate_x_w, v_gate_x_b=v_gate_x_b, v_lru_lambda=v_lru_lambda, v_pool_w=v_pool_w, v_pool_b=v_pool_b, v_pool_scale=v_pool_scale, v_w_out=v_w_out, v_post_norm_g=v_post_norm_g)
    weights = {n: given[n] for n in TWIN_WEIGHTS}
    shared = {n: given[n] for n in SHARED_INPUTS}
    per_example = {n: given[n] for n in ['x', 'c']}
    grad_fn = _jax.value_and_grad(_loss, argnums=(0, 1))

    def one_microbatch(ex, loss_target):
        ex = dict(ex)
        diff = ex.pop(TWIN_DIFF_INPUT)
        return grad_fn(weights, diff, {**shared, **ex}, loss_target)

    if N_MICROBATCH == 1:
        loss, (grad_w, grad_x) = one_microbatch(per_example, given["loss_target"])
    else:
        def body(carry, xs):
            loss_sum, grad_sum = carry
            l_k, (gw_k, gx_k) = one_microbatch(xs[0], xs[1])
            with _jax.named_scope("update"):
                return (loss_sum + l_k, _jax.tree.map(_jnp.add, grad_sum, gw_k)), gx_k

        init = (_jnp.zeros((), _jnp.float32), _jax.tree.map(_jnp.zeros_like, weights))
        (loss, grad_w), grad_x = _jax.lax.scan(body, init, (per_example, given["loss_target"]))
    with _jax.named_scope("update"):
        delta_w, new_m, new_v = {}, {}, {}
        for n in TWIN_WEIGHTS:
            delta_w[n], new_m[n], new_v[n] = _adamw(weights[n], grad_w[n], given["m_" + n], given["v_" + n])
    return (loss, grad_x, *[grad_w[n] for n in TWIN_WEIGHTS], *[delta_w[n] for n in TWIN_WEIGHTS],
            *[new_m[n] for n in TWIN_WEIGHTS], *[new_v[n] for n in TWIN_WEIGHTS])
```

```python
import jax
import jax.numpy as jnp
from jax import lax
from jax.experimental import pallas as pl
from jax.experimental.pallas import tpu as pltpu

F32, BF16 = jnp.float32, jnp.bfloat16
MESH = pl.DeviceIdType.MESH
HIGHEST = lax.Precision.HIGHEST

NDEV = 8
DEPTH = 2
D = 1024
NHEAD, HD = 8, 128
NGRP, GD = 4, 256
WINS = (2, 4, 8, 16)
CONV_K = 4
CONV_HALO = 8
POOL_HALO = 16
LRU_C = 8.0
NORM_EPS = 1e-6
ADAM_LR, ADAM_B1, ADAM_B2, ADAM_EPS, ADAM_WD, ADAM_STEP = 0.001, 0.9, 0.999, 1e-08, 0.01, 10
VMEM_LIMIT = 56 * 1024 * 1024
NQ = 4
SUB = 8
PACK_C = 256
PACK_ROWS = 272

WEIGHTS = ['ada_w', 'ada_b', 'pre_norm_g', 'w_in', 'conv_w', 'conv_b', 'gate_a_w', 'gate_a_b', 'gate_x_w',
           'gate_x_b', 'lru_lambda', 'pool_w', 'pool_b', 'pool_scale', 'w_out', 'post_norm_g']
REPLICATED = ['ada_b', 'pre_norm_g', 'conv_b', 'gate_a_w', 'gate_a_b', 'gate_x_w', 'gate_x_b', 'lru_lambda',
              'pool_scale', 'post_norm_g']


def _cparams(*sem):
    return pltpu.CompilerParams(dimension_semantics=sem, vmem_limit_bytes=VMEM_LIMIT)


def _vec(n=D):
    return pl.BlockSpec((1, n), lambda *_: (0, 0))


def _full(shape):
    nd = len(shape)
    return pl.BlockSpec(shape, lambda *_: (0,) * nd)


def _rowsum8(z):
    return z.reshape(z.shape[0] // SUB, SUB, z.shape[1]).sum(axis=0)


def _sum8(acc):
    return jnp.sum(acc, axis=0, keepdims=True)


def _silu_parts(g):
    sg = jax.nn.sigmoid(g)
    return g * sg, sg * (1.0 + g * (1.0 - sg))


def _one_minus_exp(z):
    p = 1.0 / 40320.0
    for k in (5040.0, 720.0, 120.0, 24.0, 6.0, 2.0, 1.0):
        p = p * z + 1.0 / k
    return jnp.where(z > -0.35, -(p * z), 1.0 - jnp.exp(z))


def _exchange(name, a2a=(), ag=()):
    a2a, ag = list(a2a), list(ag)
    n_a, n_t = len(a2a), len(a2a) + len(ag)
    out_shape = ([jax.ShapeDtypeStruct(a.shape, a.dtype) for a in a2a]
                 + [jax.ShapeDtypeStruct((NDEV,) + a.shape, a.dtype) for a in ag])

    def body(*refs):
        ins, outs = refs[:n_t], refs[n_t:2 * n_t]
        send_sems, recv_sems, local_sems = refs[2 * n_t:]
        x, y, c = lax.axis_index("x"), lax.axis_index("y"), lax.axis_index("c")
        me = 4 * x + 2 * y + c
        local = []
        for t in range(n_t):
            src = ins[t].at[me] if t < n_a else ins[t]
            cp = pltpu.make_async_copy(src, outs[t].at[me], local_sems.at[t])
            cp.start()
            local.append(cp)
        remote = []
        for r in range(1, NDEV):
            px = 1 - x if r & 4 else x
            py = 1 - y if r & 2 else y
            pc = 1 - c if r & 1 else c
            peer = 4 * px + 2 * py + pc
            for t in range(n_t):
                src = ins[t].at[peer] if t < n_a else ins[t]
                cp = pltpu.make_async_remote_copy(
                    src_ref=src, dst_ref=outs[t].at[me], send_sem=send_sems.at[t, r - 1],
                    recv_sem=recv_sems.at[t, r - 1], device_id=(px, py, pc), device_id_type=MESH)
                cp.start()
                remote.append(cp)
        for cp in remote:
            cp.wait()
        for cp in local:
            cp.wait()

    any_spec = pl.BlockSpec(memory_space=pl.ANY)
    outs = pl.pallas_call(
        body, name=name, out_shape=out_shape,
        in_specs=[any_spec] * n_t, out_specs=[any_spec] * n_t,
        scratch_shapes=[pltpu.SemaphoreType.DMA((n_t, NDEV - 1)), pltpu.SemaphoreType.DMA((n_t, NDEV - 1)),
                        pltpu.SemaphoreType.DMA((n_t,))],
    )(*a2a, *ag)
    return list(outs)


def _mod_cols(c_all, ada_w):
    nc = ada_w.shape[2]

    def body(c_ref, w_ref, o_ref):
        cv = c_ref[...]
        ca = cv * jax.nn.sigmoid(cv)
        for l in range(DEPTH):
            o_ref[:, l * nc:(l + 1) * nc] = jnp.dot(ca, w_ref[l], precision=HIGHEST, preferred_element_type=F32)

    return pl.pallas_call(body, name="mod_cols", out_shape=jax.ShapeDtypeStruct((NDEV, DEPTH * nc), F32),
                          compiler_params=_cparams())(c_all, ada_w)


def _pre_proj(x, gpre, scale1, shift, w_in_all, l):
    S = x.shape[0]
    TM = min(512, S)
    NB = w_in_all.shape[3]

    def body(x_ref, g_ref, sc_ref, sh_ref, w_ref, h_ref, p_ref):
        @pl.when(pl.program_id(1) == 0)
        def _():
            xv = x_ref[...]
            rstd = lax.rsqrt(jnp.mean(xv * xv, axis=-1, keepdims=True) + NORM_EPS)
            h_ref[...] = ((xv * rstd * g_ref[...]) * sc_ref[...] + sh_ref[...]).astype(BF16)
        p_ref[...] = jnp.dot(h_ref[...], w_ref[...], preferred_element_type=F32)

    return pl.pallas_call(
        body, name="pre_proj", grid=(S // TM, NDEV),
        in_specs=[pl.BlockSpec((TM, D), lambda i, j: (i, 0)), _vec(), _vec(), _vec(),
                  pl.BlockSpec((None, None, D, NB), lambda i, j: (j, l, 0, 0))],
        out_specs=[pl.BlockSpec((TM, D), lambda i, j: (i, 0)), pl.BlockSpec((TM, NB), lambda i, j: (i, j))],
        out_shape=[jax.ShapeDtypeStruct((S, D), BF16), jax.ShapeDtypeStruct((S, NDEV * NB), F32)],
        compiler_params=_cparams("arbitrary", "arbitrary"),
    )(x, gpre, scale1, shift, w_in_all)


def _conv(E, cw_ref, cb_ref):
    w = cw_ref[...]
    acc = E * w[3:4, :]
    for k in range(CONV_K - 1):
        acc = acc + pltpu.roll(E, CONV_K - 1 - k, axis=0) * w[k:k + 1, :]
    return acc[CONV_HALO:, :] + cb_ref[...]


def _gates(u, wa_ref, ba_ref, wx_ref, bx_ref, lam_ref):
    ub = u.astype(BF16)
    zr = jnp.concatenate([jnp.dot(ub[:, h * HD:(h + 1) * HD], wa_ref[h], preferred_element_type=F32)
                          for h in range(NHEAD)], axis=1)
    zi = jnp.concatenate([jnp.dot(ub[:, h * HD:(h + 1) * HD], wx_ref[h], preferred_element_type=F32)
                          for h in range(NHEAD)], axis=1)
    r = jax.nn.sigmoid(zr + ba_ref[...])
    ig = jax.nn.sigmoid(zi + bx_ref[...])
    sp = jax.nn.softplus(-lam_ref[...])
    log_a = (-LRU_C * r) * sp
    a = jnp.exp(log_a)
    mult = jnp.sqrt(_one_minus_exp(2.0 * log_a))
    return a, mult, r, ig, sp


def _scan(a, v, sa, sv, carry_ref, out_ref, reverse):
    T = a.shape[0]
    n8 = T // SUB
    A = a.reshape(n8, SUB, D)
    V = v.reshape(n8, SUB, D)
    row = lax.broadcasted_iota(jnp.int32, (n8, SUB, D), 1)
    for d in (1, 2, 4):
        sh = SUB - d if reverse else d
        keep = (row < SUB - d) if reverse else (row >= d)
        Ar = pltpu.roll(A, sh, axis=1)
        Vr = pltpu.roll(V, sh, axis=1)
        V = V + A * jnp.where(keep, Vr, 0.0)
        A = A * jnp.where(keep, Ar, 1.0)
    sa[...] = A.reshape(T, D)
    sv[...] = V.reshape(T, D)
    edge = 0 if reverse else SUB - 1

    def step(k, c):
        r0 = pl.multiple_of((n8 - 1 - k if reverse else k) * SUB, SUB)
        h = sv[pl.ds(r0, SUB), :] + sa[pl.ds(r0, SUB), :] * c
        out_ref[pl.ds(r0, SUB), :] = h
        return jnp.broadcast_to(h[edge:edge + 1, :], (SUB, D))

    carry_ref[...] = lax.fori_loop(0, n8, step, carry_ref[...])


def _rnn_fwd(proj, conv_w, conv_b, wa, ba, wx, bx, lam):
    S = proj.shape[0]
    TB = min(256, S)

    def body(xr_ref, g_ref, cw_ref, cb_ref, wa_ref, ba_ref, wx_ref, bx_ref, lam_ref, hs_ref, y_ref,
             xbuf, sa, sv, hc):
        @pl.when(pl.program_id(0) == 0)
        def _():
            xbuf[0:CONV_HALO, :] = jnp.zeros((CONV_HALO, D), F32)
            hc[...] = jnp.zeros((SUB, D), F32)
        xbuf[CONV_HALO:, :] = xr_ref[...]
        u = _conv(xbuf[...], cw_ref, cb_ref)
        xbuf[0:CONV_HALO, :] = xbuf[TB:TB + CONV_HALO, :]
        a, mult, _, ig, _ = _gates(u, wa_ref, ba_ref, wx_ref, bx_ref, lam_ref)
        _scan(a, mult * (ig * u), sa, sv, hc, hs_ref, reverse=False)
        silu, _ = _silu_parts(g_ref[...])
        y_ref[...] = (hs_ref[...] * silu).astype(BF16)

    return pl.pallas_call(
        body, name="rnn_fwd", grid=(S // TB,),
        in_specs=[pl.BlockSpec((TB, D), lambda i: (i, 0)), pl.BlockSpec((TB, D), lambda i: (i, 1)),
                  _full((CONV_K, D)), _vec(), _full((NHEAD, HD, HD)), _vec(), _full((NHEAD, HD, HD)), _vec(),
                  _vec()],
        out_specs=[pl.BlockSpec((TB, D), lambda i: (i, 0)), pl.BlockSpec((TB, D), lambda i: (i, 0))],
        out_shape=[jax.ShapeDtypeStruct((S, D), F32), jax.ShapeDtypeStruct((S, D), BF16)],
        scratch_shapes=[pltpu.VMEM((TB + CONV_HALO, D), F32), pltpu.VMEM((TB, D), F32),
                        pltpu.VMEM((TB, D), F32), pltpu.VMEM((SUB, D), F32)],
        compiler_params=_cparams("arbitrary"),
    )(proj, proj, conv_w, conv_b, wa, ba, wx, bx, lam)


def _pooled(ebuf, t0, TB):
    tt = t0 + lax.broadcasted_iota(jnp.int32, (TB, 1), 0)
    pooled, inv = [], []
    for g, win in enumerate(WINS):
        Eg = ebuf[:, g * GD:(g + 1) * GD]
        L = Eg
        for lev in range(g + 1):
            L = L + pltpu.roll(L, 1 << lev, axis=0)
        icnt = 1.0 / jnp.minimum(tt + 1, win).astype(F32)
        pooled.append(L[POOL_HALO:, :] * icnt - Eg[POOL_HALO:, :])
        inv.append(icnt)
    return pooled, inv


def _pool_fwd(proj, pw, pb, ps):
    S = proj.shape[0]
    TB = min(256, S)

    def body(xp_ref, g_ref, pw_ref, pb_ref, ps_ref, y_ref, ebuf):
        i = pl.program_id(0)

        @pl.when(i == 0)
        def _():
            ebuf[0:POOL_HALO, :] = jnp.zeros((POOL_HALO, D), F32)
        ebuf[POOL_HALO:, :] = xp_ref[...]
        pooled, _ = _pooled(ebuf, i * TB, TB)
        ebuf[0:POOL_HALO, :] = ebuf[TB:TB + POOL_HALO, :]
        yp = jnp.concatenate([jnp.dot(pooled[g].astype(BF16), pw_ref[g], preferred_element_type=F32)
                              for g in range(NGRP)], axis=1) + pb_ref[...]
        silu, _ = _silu_parts(g_ref[...])
        y_ref[...] = (yp * ps_ref[...] * silu).astype(BF16)

    return pl.pallas_call(
        body, name="pool_fwd", grid=(S // TB,),
        in_specs=[pl.BlockSpec((TB, D), lambda i: (i, 2)), pl.BlockSpec((TB, D), lambda i: (i, 3)),
                  _full((NGRP, GD, GD)), _vec(), _vec()],
        out_specs=pl.BlockSpec((TB, D), lambda i: (i, 0)),
        out_shape=jax.ShapeDtypeStruct((S, D), BF16),
        scratch_shapes=[pltpu.VMEM((TB + POOL_HALO, D), F32)],
        compiler_params=_cparams("arbitrary"),
    )(proj, proj, pw, pb, ps)


def _out_post(yr, yp, w_out_all, l, x, gate, gpost, target=None):
    S = x.shape[0]
    TM = min(512, S)
    KB = w_out_all.shape[2]
    last = target is not None

    def body(*refs):
        if last:
            yr_ref, yp_ref, w_ref, x_ref, gate_ref, gp_ref, t_ref, y_ref, xo_ref, loss_ref = refs
        else:
            yr_ref, yp_ref, w_ref, x_ref, gate_ref, gp_ref, y_ref, xo_ref = refs
        acc = jnp.zeros((TM, D), F32)
        for j in range(NDEV):
            src = yr_ref if j < NDEV // 2 else yp_ref
            k0 = (j % (NDEV // 2)) * KB
            acc = acc + jnp.dot(src[:, k0:k0 + KB], w_ref[j], preferred_element_type=F32)
        y_ref[...] = acc
        rstd = lax.rsqrt(jnp.mean(acc * acc, axis=-1, keepdims=True) + NORM_EPS)
        xn = x_ref[...] + gate_ref[...] * (acc * rstd * gp_ref[...])
        if last:
            err = xn - t_ref[...]
            xo_ref[...] = err * (1.0 / D)

            @pl.when(pl.program_id(0) == 0)
            def _():
                loss_ref[...] = jnp.zeros((SUB, D), F32)
            loss_ref[...] += _rowsum8(err * err)
        else:
            xo_ref[...] = xn

    row = pl.BlockSpec((TM, D), lambda i: (i, 0))
    in_specs = [row, row, pl.BlockSpec((NDEV, None, KB, D), lambda i: (0, l, 0, 0)), row, _vec(), _vec()]
    out_specs = [row, row]
    out_shape = [jax.ShapeDtypeStruct((S, D), F32), jax.ShapeDtypeStruct((S, D), F32)]
    args = [yr, yp, w_out_all, x, gate, gpost]
    if last:
        in_specs.append(row)
        out_specs.append(_full((SUB, D)))
        out_shape.append(jax.ShapeDtypeStruct((SUB, D), F32))
        args.append(target)
    return pl.pallas_call(body, name="out_post_loss" if last else "out_post", grid=(S // TM,),
                          in_specs=in_specs, out_specs=out_specs, out_shape=out_shape,
                          compiler_params=_cparams("arbitrary"))(*args)


def _out_bwd(dxo, y, yr, yp, w_out_all, l, gate, gpost):
    S = y.shape[0]
    TM = min(256, S)
    KB = w_out_all.shape[2]
    nsteps = S // TM

    def body(dxo_ref, y_ref, yr_ref, yp_ref, w_ref, gate_ref, gp_ref, dyr_ref, dyp_ref, gw_ref, dgate_ref,
             dgp_ref, gw_acc, vacc):
        i = pl.program_id(0)

        @pl.when(i == 0)
        def _():
            gw_acc[...] = jnp.zeros_like(gw_acc)
            vacc[...] = jnp.zeros_like(vacc)
        yv = y_ref[...]
        dxo_v = dxo_ref[...]
        rstd = lax.rsqrt(jnp.mean(yv * yv, axis=-1, keepdims=True) + NORM_EPS)
        n = yv * rstd
        gp = gp_ref[...]
        vacc[0] += _rowsum8(dxo_v * (n * gp))
        drn = dxo_v * gate_ref[...]
        vacc[1] += _rowsum8(drn * n)
        dn = drn * gp
        dy = (rstd * (dn - n * jnp.mean(dn * n, axis=-1, keepdims=True))).astype(BF16)
        for j in range(NDEV):
            dst = dyr_ref if j < NDEV // 2 else dyp_ref
            k0 = (j % (NDEV // 2)) * KB
            dst[:, k0:k0 + KB] = lax.dot_general(dy, w_ref[j], (((1,), (1,)), ((), ())),
                                                 preferred_element_type=F32)
        gw_acc[0:D, :] += lax.dot_general(yr_ref[...], dy, (((0,), (0,)), ((), ())), preferred_element_type=F32)
        gw_acc[D:2 * D, :] += lax.dot_general(yp_ref[...], dy, (((0,), (0,)), ((), ())),
                                              preferred_element_type=F32)

        @pl.when(i == nsteps - 1)
        def _():
            gw_ref[...] = gw_acc[...].astype(BF16)
            dgate_ref[...] = _sum8(vacc[0])
            dgp_ref[...] = _sum8(vacc[1])

    row = pl.BlockSpec((TM, D), lambda i: (i, 0))
    return pl.pallas_call(
        body, name="out_bwd", grid=(nsteps,),
        in_specs=[row, row, row, row, pl.BlockSpec((NDEV, None, KB, D), lambda i: (0, l, 0, 0)), _vec(), _vec()],
        out_specs=[row, row, _full((2 * D, D)), _vec(), _vec()],
        out_shape=[jax.ShapeDtypeStruct((S, D), F32), jax.ShapeDtypeStruct((S, D), F32),
                   jax.ShapeDtypeStruct((2 * D, D), BF16), jax.ShapeDtypeStruct((1, D), F32),
                   jax.ShapeDtypeStruct((1, D), F32)],
        scratch_shapes=[pltpu.VMEM((2 * D, D), F32), pltpu.VMEM((2, SUB, D), F32)],
        compiler_params=_cparams("arbitrary"),
    )(dxo, y, yr, yp, w_out_all, gate, gpost)


def _rnn_bwd(dyr, hs, proj, conv_w, conv_b, wa, ba, wx, bx, lam):
    S = proj.shape[0]
    TB = min(256, S)
    nb = S // TB
    TE = TB + CONV_HALO
    A_BA, A_BX, A_LAM, A_CB, A_CW = 0, 1, 2, 3, 4

    def blk(i):
        return nb - 1 - i

    def prev8(i):
        return jnp.maximum(blk(i) * (TB // SUB) - 1, 0)

    def body(dyr_ref, hs_ref, hprev_ref, xr_ref, xprev_ref, g_ref, cw_ref, cb_ref, wa_ref, ba_ref, wx_ref,
             bx_ref, lam_ref, dxr_ref, dg_ref, gcw_ref, gcb_ref, gwa_ref, gba_ref, gwx_ref, gbx_ref, glam_ref,
             xbuf, hbuf, abuf, dbuf, sa, sv, dh_ref, dhc, vacc):
        i = pl.program_id(0)
        first = blk(i) == 0

        @pl.when(i == 0)
        def _():
            abuf[TB:, :] = jnp.zeros((CONV_HALO, D), F32)
            dbuf[TB:, :] = jnp.zeros((CONV_HALO, D), F32)
            dhc[...] = jnp.zeros_like(dhc)
            vacc[...] = jnp.zeros_like(vacc)
            gwa_ref[...] = jnp.zeros_like(gwa_ref)
            gwx_ref[...] = jnp.zeros_like(gwx_ref)

        xbuf[0:CONV_HALO, :] = jnp.where(first, 0.0, xprev_ref[...])
        xbuf[CONV_HALO:, :] = xr_ref[...]
        E = xbuf[...]
        u = _conv(E, cw_ref, cb_ref)
        a, mult, r, ig, sp = _gates(u, wa_ref, ba_ref, wx_ref, bx_ref, lam_ref)
        hbuf[0:CONV_HALO, :] = jnp.where(first, 0.0, hprev_ref[...])
        hbuf[CONV_HALO:, :] = hs_ref[...]
        hprev = pltpu.roll(hbuf[...], 1, axis=0)[CONV_HALO:, :]

        silu, dsilu = _silu_parts(g_ref[...])
        dyv = dyr_ref[...]
        dg_ref[...] = (dyv * hs_ref[...] * dsilu).astype(BF16)

        abuf[0:TB, :] = a
        b = pltpu.roll(abuf[...], TE - 1, axis=0)[0:TB, :]
        _scan(b, dyv * silu, sa, sv, dhc, dh_ref, reverse=True)
        abuf[TB:, :] = a[0:CONV_HALO, :]
        dh = dh_ref[...]

        dlog_a = dh * hprev * a - (dh * ig * u) * (a * a) / mult
        vacc[A_LAM] += _rowsum8(dlog_a * r)
        dzr = dlog_a * (-LRU_C * sp) * r * (1.0 - r)
        dzi = (dh * mult * u) * ig * (1.0 - ig)
        vacc[A_BA] += _rowsum8(dzr)
        vacc[A_BX] += _rowsum8(dzi)
        ub, dzrb, dzib = u.astype(BF16), dzr.astype(BF16), dzi.astype(BF16)
        du_g = []
        for h in range(NHEAD):
            cs = slice(h * HD, (h + 1) * HD)
            gwa_ref[h] += lax.dot_general(ub[:, cs], dzrb[:, cs], (((0,), (0,)), ((), ())),
                                          preferred_element_type=F32)
            gwx_ref[h] += lax.dot_general(ub[:, cs], dzib[:, cs], (((0,), (0,)), ((), ())),
                                          preferred_element_type=F32)
            du_g.append(lax.dot_general(dzrb[:, cs], wa_ref[h], (((1,), (1,)), ((), ())),
                                        preferred_element_type=F32)
                        + lax.dot_general(dzib[:, cs], wx_ref[h], (((1,), (1,)), ((), ())),
                                          preferred_element_type=F32))
        du = dh * mult * ig + jnp.concatenate(du_g, axis=1)

        dbuf[0:TB, :] = du
        Dd = dbuf[...]
        w = cw_ref[...]
        dx = Dd * w[3:4, :]
        for k in range(CONV_K - 1):
            dx = dx + pltpu.roll(Dd, TE - (CONV_K - 1 - k), axis=0) * w[k:k + 1, :]
        dxr_ref[...] = dx[0:TB, :].astype(BF16)
        dbuf[TB:, :] = du[0:CONV_HALO, :]
        vacc[A_CB] += _rowsum8(du)
        vacc[A_CW + CONV_K - 1] += _rowsum8(du * E[CONV_HALO:, :])
        for k in range(CONV_K - 1):
            vacc[A_CW + k] += _rowsum8(du * pltpu.roll(E, CONV_K - 1 - k, axis=0)[CONV_HALO:, :])

        @pl.when(i == nb - 1)
        def _():
            gba_ref[...] = _sum8(vacc[A_BA])
            gbx_ref[...] = _sum8(vacc[A_BX])
            glam_ref[...] = _sum8(vacc[A_LAM]) * (LRU_C * jax.nn.sigmoid(-lam_ref[...]))
            gcb_ref[...] = _sum8(vacc[A_CB])
            for k in range(CONV_K):
                gcw_ref[k:k + 1, :] = _sum8(vacc[A_CW + k])

    rowb = pl.BlockSpec((TB, D), lambda i: (blk(i), 0))
    halo = pl.BlockSpec((SUB, D), lambda i: (prev8(i), 0))
    wspec = _full((NHEAD, HD, HD))
    return pl.pallas_call(
        body, name="rnn_bwd", grid=(nb,),
        in_specs=[rowb, rowb, halo, rowb, halo, pl.BlockSpec((TB, D), lambda i: (blk(i), 1)),
                  _full((CONV_K, D)), _vec(), wspec, _vec(), wspec, _vec(), _vec()],
        out_specs=[rowb, rowb, _full((CONV_K, D)), _vec(), wspec, _vec(), wspec, _vec(), _vec()],
        out_shape=[jax.ShapeDtypeStruct((S, D), BF16), jax.ShapeDtypeStruct((S, D), BF16),
                   jax.ShapeDtypeStruct((CONV_K, D), F32), jax.ShapeDtypeStruct((1, D), F32),
                   jax.ShapeDtypeStruct((NHEAD, HD, HD), F32), jax.ShapeDtypeStruct((1, D), F32),
                   jax.ShapeDtypeStruct((NHEAD, HD, HD), F32), jax.ShapeDtypeStruct((1, D), F32),
                   jax.ShapeDtypeStruct((1, D), F32)],
        scratch_shapes=[pltpu.VMEM((TE, D), F32), pltpu.VMEM((TE, D), F32), pltpu.VMEM((TE, D), F32),
                        pltpu.VMEM((TE, D), F32), pltpu.VMEM((TB, D), F32), pltpu.VMEM((TB, D), F32),
                        pltpu.VMEM((TB, D), F32), pltpu.VMEM((SUB, D), F32),
                        pltpu.VMEM((A_CW + CONV_K, SUB, D), F32)],
        compiler_params=_cparams("arbitrary"),
    )(dyr, hs, hs, proj, proj, proj, conv_w, conv_b, wa, ba, wx, bx, lam)


def _pool_bwd(dyp, proj, pw, pb, ps):
    S = proj.shape[0]
    TB = min(256, S)
    nb = S // TB
    TE = TB + POOL_HALO

    def blk(i):
        return nb - 1 - i

    def body(dy_ref, xp_ref, xprev_ref, g_ref, pw_ref, pb_ref, ps_ref, dxp_ref, dg_ref, gpw_ref, gpb_ref,
             gps_ref, ebuf, qbuf, vacc):
        i = pl.program_id(0)
        first = blk(i) == 0

        @pl.when(i == 0)
        def _():
            qbuf[TB:, :] = jnp.zeros((POOL_HALO, D), F32)
            vacc[...] = jnp.zeros_like(vacc)
            gpw_ref[...] = jnp.zeros_like(gpw_ref)

        ebuf[0:POOL_HALO, :] = jnp.where(first, 0.0, xprev_ref[...])
        ebuf[POOL_HALO:, :] = xp_ref[...]
        pooled, inv = _pooled(ebuf, blk(i) * TB, TB)
        pooled = [p.astype(BF16) for p in pooled]
        yp = jnp.concatenate([jnp.dot(pooled[g], pw_ref[g], preferred_element_type=F32)
                              for g in range(NGRP)], axis=1) + pb_ref[...]
        silu, dsilu = _silu_parts(g_ref[...])
        dy = dy_ref[...]
        ps = ps_ref[...]
        dyp_v = dy * ps * silu
        vacc[0] += _rowsum8(dy * yp * silu)
        vacc[1] += _rowsum8(dyp_v)
        dg_ref[...] = (dy * yp * ps * dsilu).astype(BF16)
        dypb = dyp_v.astype(BF16)
        for g in range(NGRP):
            cs = slice(g * GD, (g + 1) * GD)
            gpw_ref[g] += lax.dot_general(pooled[g], dypb[:, cs], (((0,), (0,)), ((), ())),
                                          preferred_element_type=F32)
            dpool = lax.dot_general(dypb[:, cs], pw_ref[g], (((1,), (1,)), ((), ())),
                                    preferred_element_type=F32)
            qbuf[0:TB, cs] = dpool * inv[g]
            L = qbuf[:, cs]
            for lev in range(g + 1):
                L = L + pltpu.roll(L, TE - (1 << lev), axis=0)
            dxp_ref[:, cs] = (L[0:TB, :] - dpool).astype(BF16)
        qbuf[TB:, :] = qbuf[0:POOL_HALO, :]

        @pl.when(i == nb - 1)
        def _():
            gps_ref[...] = _sum8(vacc[0])
            gpb_ref[...] = _sum8(vacc[1])

    rowb = pl.BlockSpec((TB, D), lambda i: (blk(i), 0))
    return pl.pallas_call(
        body, name="pool_bwd", grid=(nb,),
        in_specs=[rowb, pl.BlockSpec((TB, D), lambda i: (blk(i), 2)),
                  pl.BlockSpec((POOL_HALO, D), lambda i: (jnp.maximum(blk(i) * (TB // POOL_HALO) - 1, 0), 2)),
                  pl.BlockSpec((TB, D), lambda i: (blk(i), 3)), _full((NGRP, GD, GD)), _vec(), _vec()],
        out_specs=[rowb, rowb, _full((NGRP, GD, GD)), _vec(), _vec()],
        out_shape=[jax.ShapeDtypeStruct((S, D), BF16), jax.ShapeDtypeStruct((S, D), BF16),
                   jax.ShapeDtypeStruct((NGRP, GD, GD), F32), jax.ShapeDtypeStruct((1, D), F32),
                   jax.ShapeDtypeStruct((1, D), F32)],
        scratch_shapes=[pltpu.VMEM((TE, D), F32), pltpu.VMEM((TE, D), F32), pltpu.VMEM((2, SUB, D), F32)],
        compiler_params=_cparams("arbitrary"),
    )(dyp, proj, proj, proj, pw, pb, ps)


def _in_bwd(dq, w_in_all, l, x, dxo, gpre, scale1):
    S = x.shape[0]
    TM = min(256, S)
    NB = w_in_all.shape[3]
    nsteps = S // TM
    per_q = D // NB

    def body(d0, d1, d2, d3, w_ref, x_ref, dxo_ref, g_ref, sc_ref, dx_ref, dsh_ref, dsc_ref, dg_ref, vacc):
        i = pl.program_id(0)

        @pl.when(i == 0)
        def _():
            vacc[...] = jnp.zeros_like(vacc)
        dref = (d0, d1, d2, d3)
        dh = jnp.zeros((TM, D), F32)
        for j in range(NDEV):
            c0 = (j % per_q) * NB
            dh = dh + lax.dot_general(dref[j // per_q][:, c0:c0 + NB], w_ref[j], (((1,), (1,)), ((), ())),
                                      preferred_element_type=F32)
        xv = x_ref[...]
        rstd = lax.rsqrt(jnp.mean(xv * xv, axis=-1, keepdims=True) + NORM_EPS)
        xn = xv * rstd
        g, sc = g_ref[...], sc_ref[...]
        vacc[0] += _rowsum8(dh)
        vacc[1] += _rowsum8(dh * (xn * g))
        vacc[2] += _rowsum8(dh * sc * xn)
        dxn = dh * sc * g
        dx_ref[...] = dxo_ref[...] + rstd * (dxn - xn * jnp.mean(dxn * xn, axis=-1, keepdims=True))

        @pl.when(i == nsteps - 1)
        def _():
            dsh_ref[...] = _sum8(vacc[0])
            dsc_ref[...] = _sum8(vacc[1])
            dg_ref[...] = _sum8(vacc[2])

    row = pl.BlockSpec((TM, D), lambda i: (i, 0))
    return pl.pallas_call(
        body, name="in_bwd", grid=(nsteps,),
        in_specs=[row, row, row, row, pl.BlockSpec((NDEV, None, D, NB), lambda i: (0, l, 0, 0)), row, row,
                  _vec(), _vec()],
        out_specs=[row, _vec(), _vec(), _vec()],
        out_shape=[jax.ShapeDtypeStruct((S, D), F32)] + [jax.ShapeDtypeStruct((1, D), F32)] * 3,
        scratch_shapes=[pltpu.VMEM((3, SUB, D), F32)],
        compiler_params=_cparams("arbitrary"),
    )(*dq, w_in_all, x, dxo, gpre, scale1)


def _grad_w_in(h, d, NB):
    S = h.shape[0]
    TK = min(512, S)
    nk = S // TK

    def body(h_ref, d_ref, o_ref, acc):
        k = pl.program_id(1)

        @pl.when(k == 0)
        def _():
            acc[...] = jnp.zeros_like(acc)
        acc[...] += lax.dot_general(h_ref[...], d_ref[...], (((0,), (0,)), ((), ())), preferred_element_type=F32)

        @pl.when(k == nk - 1)
        def _():
            o_ref[...] = acc[...].astype(BF16)

    return pl.pallas_call(
        body, name="grad_w_in", grid=(D // NB, nk),
        in_specs=[pl.BlockSpec((TK, D), lambda j, k: (k, 0)), pl.BlockSpec((TK, NB), lambda j, k: (k, j))],
        out_specs=pl.BlockSpec((None, D, NB), lambda j, k: (j, 0, 0)),
        out_shape=jax.ShapeDtypeStruct((D // NB, D, NB), BF16),
        scratch_shapes=[pltpu.VMEM((D, NB), F32)],
        compiler_params=_cparams("arbitrary", "arbitrary"),
    )(h, d)


def _adamw_math(g, w, m, v):
    m2 = ADAM_B1 * m + (1.0 - ADAM_B1) * g
    v2 = ADAM_B2 * v + (1.0 - ADAM_B2) * (g * g)
    m_hat = m2 / (1.0 - ADAM_B1 ** ADAM_STEP)
    v_hat = v2 / (1.0 - ADAM_B2 ** ADAM_STEP)
    delta = -ADAM_LR * (m_hat / (jnp.sqrt(v_hat) + ADAM_EPS) + ADAM_WD * w)
    return delta, m2, v2


def _adamw(name, gs, w, m, v, TR):
    L = len(gs)
    n, R, C = gs[0].shape

    def body(*refs):
        g_refs = refs[:L]
        w_ref, m_ref, v_ref, go_ref, do_ref, mo_ref, vo_ref = refs[L:]
        lay = pl.program_id(0)
        for li in range(L):
            @pl.when(lay == li)
            def _(li=li):
                g = g_refs[li][0].astype(F32)
                for s in range(1, n):
                    g = g + g_refs[li][s].astype(F32)
                delta, m2, v2 = _adamw_math(g, w_ref[...], m_ref[...], v_ref[...])
                go_ref[...] = g
                do_ref[...] = delta
                mo_ref[...] = m2
                vo_ref[...] = v2

    lrc = pl.BlockSpec((None, TR, C), lambda lay, r: (lay, r, 0))
    g_specs = [pl.BlockSpec((n, TR, C), lambda lay, r, li=li: (0, jnp.where(lay == li, r, 0), 0))
               for li in range(L)]
    return pl.pallas_call(
        body, name=name, grid=(L, R // TR),
        in_specs=g_specs + [lrc, lrc, lrc], out_specs=[lrc] * 4,
        out_shape=[jax.ShapeDtypeStruct((L, R, C), F32)] * 4,
        compiler_params=_cparams("arbitrary", "arbitrary"),
    )(*gs, w, m, v)


def _ada_adamw(c_all_t, dm, w, m, v):
    L, _, nc = w.shape

    def body(c_ref, dm_ref, w_ref, m_ref, v_ref, go_ref, do_ref, mo_ref, vo_ref):
        cv = c_ref[...]
        ca = cv * jax.nn.sigmoid(cv)
        dmv = dm_ref[...]
        g = ca[:, 0:1] * dmv[0:1, :]
        for b in range(1, NDEV):
            g = g + ca[:, b:b + 1] * dmv[b:b + 1, :]
        delta, m2, v2 = _adamw_math(g, w_ref[...], m_ref[...], v_ref[...])
        go_ref[...] = g
        do_ref[...] = delta
        mo_ref[...] = m2
        vo_ref[...] = v2

    big = pl.BlockSpec((None, D, nc), lambda lay: (lay, 0, 0))
    return pl.pallas_call(
        body, name="ada_adamw", grid=(L,),
        in_specs=[_full((D, NDEV)), pl.BlockSpec((None, NDEV, nc), lambda lay: (lay, 0, 0)), big, big, big],
        out_specs=[big] * 4, out_shape=[jax.ShapeDtypeStruct((L, D, nc), F32)] * 4,
        compiler_params=_cparams("arbitrary"),
    )(c_all_t, dm, w, m, v)


def _sum_slots(recv):
    n, R, C = recv.shape

    def body(r_ref, o_ref):
        acc = r_ref[0]
        for s in range(1, n):
            acc = acc + r_ref[s]
        o_ref[...] = acc

    return pl.pallas_call(body, name="sum_slots", out_shape=jax.ShapeDtypeStruct((R, C), F32),
                          compiler_params=_cparams())(recv)


def _pad_rows(a, rows):
    return jnp.pad(a, ((0, rows - a.shape[0]), (0, 0)))


def _pack_sharded_block(pool_w, pool_b, conv_w):
    return jnp.concatenate([pool_w.reshape(-1, PACK_C), _pad_rows(pool_b.reshape(-1, PACK_C), SUB),
                            _pad_rows(conv_w.reshape(-1, PACK_C), SUB)], axis=0)


def _unpack_sharded_block(p):
    n_pw = DEPTH * NGRP * (GD // NDEV)
    pool_w = p[:n_pw].reshape(DEPTH, NGRP, GD // NDEV, GD)
    pool_b = p[n_pw].reshape(DEPTH, NGRP, GD // NDEV)
    conv_w = p[n_pw + SUB:n_pw + SUB + DEPTH * CONV_K * (D // NDEV) // PACK_C].reshape(DEPTH, CONV_K, D // NDEV)
    return pool_w, pool_b, conv_w


def _blocks_of_full(pool_w, pool_b, conv_w):
    pw = pool_w.reshape(DEPTH, NGRP, NDEV, GD // NDEV, GD).transpose(2, 0, 1, 3, 4).reshape(NDEV, -1, PACK_C)
    pb = pool_b.reshape(DEPTH, NGRP, NDEV, GD // NDEV).transpose(2, 0, 1, 3).reshape(NDEV, -1, PACK_C)
    cw = conv_w.reshape(DEPTH, CONV_K, NDEV, D // NDEV).transpose(2, 0, 1, 3).reshape(NDEV, -1, PACK_C)
    pad = lambda a: jnp.pad(a, ((0, 0), (0, SUB - a.shape[1]), (0, 0)))
    return jnp.concatenate([pw, pad(pb), pad(cw)], axis=1)


def _full_of_blocks(p):
    n_pw = DEPTH * NGRP * (GD // NDEV)
    pool_w = p[:, :n_pw].reshape(NDEV, DEPTH, NGRP, GD // NDEV, GD).transpose(1, 2, 0, 3, 4)
    pool_b = p[:, n_pw].reshape(NDEV, DEPTH, NGRP, GD // NDEV).transpose(1, 2, 0, 3)
    n_cw = DEPTH * CONV_K * (D // NDEV) // PACK_C
    conv_w = p[:, n_pw + SUB:n_pw + SUB + n_cw].reshape(NDEV, DEPTH, CONV_K, D // NDEV).transpose(1, 2, 0, 3)
    return (pool_w.reshape(DEPTH, NGRP, GD, GD), pool_b.reshape(DEPTH, NGRP, GD),
            conv_w.reshape(DEPTH, CONV_K, D))


def _pack_replicated(t):
    p = jnp.concatenate([t[k].reshape(-1, PACK_C) for k in REPLICATED], axis=0)
    return _pad_rows(p, NDEV * PACK_ROWS)


def _unpack_replicated(p, like):
    out, r0 = {}, 0
    for k in REPLICATED:
        rows = like[k].size // PACK_C
        out[k] = p[r0:r0 + rows].reshape(like[k].shape)
        r0 += rows
    return out


def kernel(x, c, ada_w, ada_b, pre_norm_g, w_in, conv_w, conv_b, gate_a_w, gate_a_b, gate_x_w, gate_x_b, lru_lambda, pool_w, pool_b, pool_scale, w_out, post_norm_g, loss_target, m_ada_w, m_ada_b, m_pre_norm_g, m_w_in, m_conv_w, m_conv_b, m_gate_a_w, m_gate_a_b, m_gate_x_w, m_gate_x_b, m_lru_lambda, m_pool_w, m_pool_b, m_pool_scale, m_w_out, m_post_norm_g, v_ada_w, v_ada_b, v_pre_norm_g, v_w_in, v_conv_w, v_conv_b, v_gate_a_w, v_gate_a_b, v_gate_x_w, v_gate_x_b, v_lru_lambda, v_pool_w, v_pool_b, v_pool_scale, v_w_out, v_post_norm_g):
    W = dict(ada_w=ada_w, ada_b=ada_b, pre_norm_g=pre_norm_g, w_in=w_in, conv_w=conv_w, conv_b=conv_b,
             gate_a_w=gate_a_w, gate_a_b=gate_a_b, gate_x_w=gate_x_w, gate_x_b=gate_x_b, lru_lambda=lru_lambda,
             pool_w=pool_w, pool_b=pool_b, pool_scale=pool_scale, w_out=w_out, post_norm_g=post_norm_g)
    M = dict(ada_w=m_ada_w, ada_b=m_ada_b, pre_norm_g=m_pre_norm_g, w_in=m_w_in, conv_w=m_conv_w,
             conv_b=m_conv_b, gate_a_w=m_gate_a_w, gate_a_b=m_gate_a_b, gate_x_w=m_gate_x_w,
             gate_x_b=m_gate_x_b, lru_lambda=m_lru_lambda, pool_w=m_pool_w, pool_b=m_pool_b,
             pool_scale=m_pool_scale, w_out=m_w_out, post_norm_g=m_post_norm_g)
    V = dict(ada_w=v_ada_w, ada_b=v_ada_b, pre_norm_g=v_pre_norm_g, w_in=v_w_in, conv_w=v_conv_w,
             conv_b=v_conv_b, gate_a_w=v_gate_a_w, gate_a_b=v_gate_a_b, gate_x_w=v_gate_x_w,
             gate_x_b=v_gate_x_b, lru_lambda=v_lru_lambda, pool_w=v_pool_w, pool_b=v_pool_b,
             pool_scale=v_pool_scale, w_out=v_w_out, post_norm_g=v_post_norm_g)
    S = x.shape[1]
    me = 4 * lax.axis_index("x") + 2 * lax.axis_index("y") + lax.axis_index("c")
    xs = x.reshape(S, D)
    tgt = loss_target.reshape(S, D)
    nc = ada_w.shape[2]

    (c_slots,) = _exchange("gather_c", ag=[jnp.broadcast_to(c, (SUB, D))])
    c_all = c_slots[:, 0, :]
    (mod_slots,) = _exchange("gather_mod", ag=[_mod_cols(c_all, ada_w)])
    mod = lax.dynamic_index_in_dim(mod_slots, me, axis=1, keepdims=False)
    mod = mod.reshape(NDEV, DEPTH, nc).transpose(1, 0, 2).reshape(DEPTH, 3 * D) + ada_b

    w_in_all, w_out_all, small_all = _exchange(
        "gather_weights", ag=[w_in.astype(BF16), w_out.astype(BF16), _pack_sharded_block(pool_w, pool_b, conv_w)])
    pool_w_f, pool_b_f, conv_w_f = _full_of_blocks(small_all)
    NB = w_in.shape[2]
    vec = lambda a: a.reshape(1, D)

    saved = []
    xl = xs
    for l in range(DEPTH):
        shift, scale1, gate = vec(mod[l, :D]), vec(1.0 + mod[l, D:2 * D]), vec(mod[l, 2 * D:])
        wa, wx = gate_a_w[l].astype(BF16), gate_x_w[l].astype(BF16)
        pw = pool_w_f[l].astype(BF16)
        h, proj = _pre_proj(xl, vec(pre_norm_g[l]), scale1, shift, w_in_all, l)
        hs, yr = _rnn_fwd(proj, conv_w_f[l], vec(conv_b[l]), wa, gate_a_b[l].reshape(1, D), wx,
                          gate_x_b[l].reshape(1, D), vec(lru_lambda[l]))
        yp = _pool_fwd(proj, pw, pool_b_f[l].reshape(1, D), vec(pool_scale[l]))
        if l == DEPTH - 1:
            y, x_next, loss_acc = _out_post(yr, yp, w_out_all, l, xl, gate, vec(post_norm_g[l]), tgt)
        else:
            y, x_next = _out_post(yr, yp, w_out_all, l, xl, gate, vec(post_norm_g[l]))
        saved.append((xl, h, proj, hs, yr, yp, y, scale1, gate, wa, wx, pw))
        xl = x_next
    loss = lax.psum((0.5 / D) * jnp.sum(loss_acc), ("x", "y", "c"))

    dxo = xl
    G = {k: [None] * DEPTH for k in WEIGHTS}
    dmod = [None] * DEPTH
    big_recv = [None] * DEPTH
    for l in reversed(range(DEPTH)):
        xin, h, proj, hs, yr, yp, y, scale1, gate, wa, wx, pw = saved[l]
        dyr, dyp, gw_out, dgate, G['post_norm_g'][l] = _out_bwd(dxo, y, yr, yp, w_out_all, l, gate,
                                                                vec(post_norm_g[l]))
        (dxr, dgr, G['conv_w'][l], G['conv_b'][l], G['gate_a_w'][l], G['gate_a_b'][l], G['gate_x_w'][l],
         G['gate_x_b'][l], G['lru_lambda'][l]) = _rnn_bwd(
            dyr, hs, proj, conv_w_f[l], vec(conv_b[l]), wa, gate_a_b[l].reshape(1, D), wx,
            gate_x_b[l].reshape(1, D), vec(lru_lambda[l]))
        dxp, dgp, G['pool_w'][l], G['pool_b'][l], G['pool_scale'][l] = _pool_bwd(
            dyp, proj, pw, pool_b_f[l].reshape(1, D), vec(pool_scale[l]))
        dq = (dxr, dgr, dxp, dgp)
        dxo, dshift, dscale, G['pre_norm_g'][l] = _in_bwd(dq, w_in_all, l, xin, dxo, vec(pre_norm_g[l]), scale1)
        gw_in = jnp.concatenate([_grad_w_in(h, d, NB) for d in dq], axis=0)
        dmod[l] = jnp.concatenate([dshift, dscale, dgate], axis=1)
        a2a = [gw_in, gw_out.reshape(NDEV, 2 * D // NDEV, D)]
        if l > 0:
            big_recv[l] = _exchange("grads_layer%d" % l, a2a=a2a)
        else:
            full = dict(conv_w=(CONV_K, D), pool_w=(NGRP, GD, GD), pool_b=(NGRP, GD))
            Gs = {k: jnp.stack([g.reshape(full.get(k, W[k].shape[1:])) for g in G[k]]) for k in WEIGHTS
                  if k not in ('ada_w', 'ada_b', 'w_in', 'w_out')}
            Gs['ada_b'] = jnp.concatenate(dmod, axis=0)
            small = jnp.concatenate([_blocks_of_full(Gs['pool_w'], Gs['pool_b'], Gs['conv_w']),
                                     _pack_replicated(Gs).reshape(NDEV, PACK_ROWS, PACK_C)], axis=1)
            recv_in, recv_out, small_recv, dmod_all = _exchange(
                "grads_layer0", a2a=a2a + [small], ag=[Gs['ada_b']])
            big_recv[l] = [recv_in, recv_out]
    grad_x = dxo.reshape(x.shape)

    out = {}
    out['w_in'] = _adamw("adamw_w_in", [big_recv[l][0] for l in range(DEPTH)], w_in, M['w_in'], V['w_in'], 256)
    out['w_out'] = _adamw("adamw_w_out", [big_recv[l][1] for l in range(DEPTH)], w_out, M['w_out'], V['w_out'],
                          256)
    dm = lax.dynamic_slice_in_dim(dmod_all.reshape(NDEV, DEPTH, 3 * D), me * nc, nc, axis=2)
    out['ada_w'] = _ada_adamw(c_all.T, dm.transpose(1, 0, 2), ada_w, M['ada_w'], V['ada_w'])

    small_sum = _sum_slots(small_recv)
    (rep_all,) = _exchange("gather_small", ag=[small_sum[PACK_ROWS:]])
    g_small = jnp.concatenate([small_sum[:PACK_ROWS], rep_all.reshape(NDEV * PACK_ROWS, PACK_C)], axis=0)

    def packs(T):
        return jnp.concatenate([_pack_sharded_block(T['pool_w'], T['pool_b'], T['conv_w']),
                                _pack_replicated(T)], axis=0)[None]
    res_small = _adamw("adamw_small", [g_small[None]], packs(W), packs(M), packs(V), PACK_ROWS)
    for idx in range(4):
        p = res_small[idx][0]
        pw_, pb_, cw_ = _unpack_sharded_block(p[:PACK_ROWS])
        rep = _unpack_replicated(p[PACK_ROWS:], W)
        rep.update(pool_w=pw_, pool_b=pb_, conv_w=cw_)
        for k, a in rep.items():
            out.setdefault(k, [None] * 4)[idx] = a
    for k in ('w_in', 'w_out', 'ada_w'):
        out[k] = [a.reshape(W[k].shape) for a in out[k]]

    return (loss, grad_x, *[out[k][0] for k in WEIGHTS], *[out[k][1] for k in WEIGHTS],
            *[out[k][2] for k in WEIGHTS], *[out[k][3] for k in WEIGHTS])
```

```python
import functools

import jax
import jax.numpy as jnp
from jax import lax
from jax.experimental import pallas as pl
from jax.experimental.pallas import tpu as pltpu

F32, BF16 = jnp.float32, jnp.bfloat16
MESH = pl.DeviceIdType.MESH
HIGHEST = lax.Precision.HIGHEST

NDEV = 8
DEPTH = 2
D = 1024
NHEAD, HD = 8, 128
NGRP, GD = 4, 256
WINS = (2, 4, 8, 16)
CONV_K = 4
CONV_HALO = 8
POOL_HALO = 16
LRU_C = 8.0
NORM_EPS = 1e-6
ADAM_LR, ADAM_B1, ADAM_B2, ADAM_EPS, ADAM_WD, ADAM_STEP = 0.001, 0.9, 0.999, 1e-08, 0.01, 10
VMEM_LIMIT = 56 * 1024 * 1024
NQ = 4
SUB = 8
PACK_C = 256
PACK_ROWS = 272

WEIGHTS = ['ada_w', 'ada_b', 'pre_norm_g', 'w_in', 'conv_w', 'conv_b', 'gate_a_w', 'gate_a_b', 'gate_x_w',
           'gate_x_b', 'lru_lambda', 'pool_w', 'pool_b', 'pool_scale', 'w_out', 'post_norm_g']
REPLICATED = ['ada_b', 'pre_norm_g', 'conv_b', 'gate_a_w', 'gate_a_b', 'gate_x_w', 'gate_x_b', 'lru_lambda',
              'pool_scale', 'post_norm_g']


def _cparams(*sem):
    return pltpu.CompilerParams(dimension_semantics=sem, vmem_limit_bytes=VMEM_LIMIT)


def _vec(n=D):
    return pl.BlockSpec((1, n), lambda *_: (0, 0))


def _full(shape):
    nd = len(shape)
    return pl.BlockSpec(shape, lambda *_: (0,) * nd)


def _rowsum8(z):
    return z.reshape(z.shape[0] // SUB, SUB, z.shape[1]).sum(axis=0)


def _sum8(acc):
    return jnp.sum(acc, axis=0, keepdims=True)


def _silu_parts(g):
    sg = jax.nn.sigmoid(g)
    return g * sg, sg * (1.0 + g * (1.0 - sg))


def _one_minus_exp(z):
    p = 1.0 / 40320.0
    for k in (5040.0, 720.0, 120.0, 24.0, 6.0, 2.0, 1.0):
        p = p * z + 1.0 / k
    return jnp.where(z > -0.35, -(p * z), 1.0 - jnp.exp(z))


def _place():
    x, y, c = lax.axis_index("x"), lax.axis_index("y"), lax.axis_index("c")
    return x, y, c, 4 * x + 2 * y + c


class _Direct:
    def __init__(self, a2a=(), ag=()):
        self.arrays = list(a2a) + list(ag)
        self.n_a, self.n = len(a2a), len(self.arrays)
        self.out_shape = ([jax.ShapeDtypeStruct(a.shape, a.dtype) for a in a2a]
                          + [jax.ShapeDtypeStruct((NDEV,) + a.shape, a.dtype) for a in ag])
        self.scratch = [pltpu.SemaphoreType.DMA((self.n, NDEV - 1)), pltpu.SemaphoreType.DMA((self.n, NDEV - 1)),
                        pltpu.SemaphoreType.DMA((self.n,))]

    def _copies(self, ins, outs, sems):
        send_sems, recv_sems, local_sems = sems
        x, y, c, me = _place()
        local, remote = [], []
        for t in range(self.n):
            src = ins[t].at[me] if t < self.n_a else ins[t]
            local.append(pltpu.make_async_copy(src, outs[t].at[me], local_sems.at[t]))
        for r in range(1, NDEV):
            px = 1 - x if r & 4 else x
            py = 1 - y if r & 2 else y
            pc = 1 - c if r & 1 else c
            for t in range(self.n):
                src = ins[t].at[4 * px + 2 * py + pc] if t < self.n_a else ins[t]
                remote.append(pltpu.make_async_remote_copy(
                    src_ref=src, dst_ref=outs[t].at[me], send_sem=send_sems.at[t, r - 1],
                    recv_sem=recv_sems.at[t, r - 1], device_id=(px, py, pc), device_id_type=MESH))
        return local, remote

    def start(self, ins, outs, sems):
        local, remote = self._copies(ins, outs, sems)
        for cp in local + remote:
            cp.start()

    def finish(self, ins, outs, sems):
        local, remote = self._copies(ins, outs, sems)
        for cp in remote + local:
            cp.wait()


class _AllGather2:
    def __init__(self, arrays):
        self.arrays = list(arrays)
        self.n = len(self.arrays)
        self.out_shape = [jax.ShapeDtypeStruct((NDEV,) + a.shape, a.dtype) for a in self.arrays]
        self.scratch = [pltpu.SemaphoreType.DMA((self.n, NDEV - 1)), pltpu.SemaphoreType.DMA((self.n, NDEV - 1)),
                        pltpu.SemaphoreType.DMA((self.n,))]

    @staticmethod
    def _chips(x, y):
        return [(1 - x, y), (x, 1 - y), (1 - x, 1 - y)]

    def _copy(self, t, k, src, dst, to, sems):
        return pltpu.make_async_remote_copy(src_ref=src, dst_ref=dst, send_sem=sems[0].at[t, k],
                                            recv_sem=sems[1].at[t, k], device_id=to, device_id_type=MESH)

    def start(self, ins, outs, sems):
        x, y, c, me = _place()
        for t in range(self.n):
            pltpu.make_async_copy(ins[t], outs[t].at[me], sems[2].at[t]).start()
            self._copy(t, 0, ins[t], outs[t].at[me], (x, y, 1 - c), sems).start()
            for j, (px, py) in enumerate(self._chips(x, y)):
                self._copy(t, 1 + j, ins[t], outs[t].at[me], (px, py, c), sems).start()

    def finish(self, ins, outs, sems):
        x, y, c, me = _place()
        sib = (x, y, 1 - c)
        for j, (px, py) in enumerate(self._chips(x, y)):
            slot = 4 * px + 2 * py + c
            for t in range(self.n):
                self._copy(t, 1 + j, ins[t], outs[t].at[slot], sib, sems).wait_recv()
                self._copy(t, 4 + j, outs[t].at[slot], outs[t].at[slot], sib, sems).start()
        for t in range(self.n):
            for k in (0, 4, 5, 6):
                self._copy(t, k, ins[t], outs[t].at[me], sib, sems).wait_recv()
        for t in range(self.n):
            for k in range(NDEV - 1):
                self._copy(t, k, ins[t], outs[t].at[me], sib, sems).wait_send()
            pltpu.make_async_copy(ins[t], outs[t].at[me], sems[2].at[t]).wait()


def _exchange(name, rider):
    n = rider.n

    def body(*refs):
        rider.start(refs[:n], refs[n:2 * n], refs[2 * n:])
        rider.finish(refs[:n], refs[n:2 * n], refs[2 * n:])

    any_spec = pl.BlockSpec(memory_space=pl.ANY)
    return list(pl.pallas_call(body, name=name, out_shape=rider.out_shape, in_specs=[any_spec] * n,
                               out_specs=[any_spec] * n, scratch_shapes=rider.scratch)(*rider.arrays))


def _pcall(body, *, name, grid, in_specs, out_specs, out_shape, args, scratch_shapes=(), rider=None):
    params = _cparams(*(("arbitrary",) * len(grid)))
    if rider is None:
        res = pl.pallas_call(body, name=name, grid=grid, in_specs=in_specs, out_specs=out_specs,
                             out_shape=out_shape, scratch_shapes=list(scratch_shapes),
                             compiler_params=params)(*args)
        return list(res), []
    n_in, n_out, n_scr, rn = len(in_specs), len(out_specs), len(scratch_shapes), rider.n

    def wrapped(*refs):
        cuts = [n_in, rn, n_out, rn, n_scr]
        parts, p = [], 0
        for n in cuts:
            parts.append(refs[p:p + n])
            p += n
        ins, r_in, outs, r_out, scr = parts
        sems = refs[p:]
        ids = [pl.program_id(a) for a in range(len(grid))]
        first = functools.reduce(jnp.logical_and, [i == 0 for i in ids])
        last = functools.reduce(jnp.logical_and, [i == g - 1 for i, g in zip(ids, grid)])

        @pl.when(first)
        def _():
            rider.start(r_in, r_out, sems)
        body(*ins, *outs, *scr)

        @pl.when(last)
        def _():
            rider.finish(r_in, r_out, sems)

    any_spec = pl.BlockSpec(memory_space=pl.ANY)
    res = pl.pallas_call(
        wrapped, name=name, grid=grid, in_specs=list(in_specs) + [any_spec] * rn,
        out_specs=list(out_specs) + [any_spec] * rn, out_shape=list(out_shape) + rider.out_shape,
        scratch_shapes=list(scratch_shapes) + rider.scratch, compiler_params=params)(*args, *rider.arrays)
    return list(res[:n_out]), list(res[n_out:])


def _mod_cols(c_all, ada_w):
    nc = ada_w.shape[2]

    def body(c_ref, w_ref, o_ref):
        cv = c_ref[...]
        ca = cv * jax.nn.sigmoid(cv)
        for l in range(DEPTH):
            o_ref[:, l * nc:(l + 1) * nc] = jnp.dot(ca, w_ref[l], precision=HIGHEST, preferred_element_type=F32)

    return pl.pallas_call(body, name="mod_cols", out_shape=jax.ShapeDtypeStruct((NDEV, DEPTH * nc), F32),
                          compiler_params=_cparams())(c_all, ada_w)


def _pre_proj(x, gpre, scale1, shift, w_in_l, rider=None):
    S = x.shape[0]
    TM = min(512, S)
    NB = w_in_l.shape[2]

    def body(x_ref, g_ref, sc_ref, sh_ref, w_ref, h_ref, p_ref):
        @pl.when(pl.program_id(1) == 0)
        def _():
            xv = x_ref[...]
            rstd = lax.rsqrt(jnp.mean(xv * xv, axis=-1, keepdims=True) + NORM_EPS)
            h_ref[...] = ((xv * rstd * g_ref[...]) * sc_ref[...] + sh_ref[...]).astype(BF16)
        p_ref[...] = jnp.dot(h_ref[...], w_ref[...], preferred_element_type=F32)

    return _pcall(
        body, name="pre_proj", grid=(S // TM, NDEV),
        in_specs=[pl.BlockSpec((TM, D), lambda i, j: (i, 0)), _vec(), _vec(), _vec(),
                  pl.BlockSpec((None, D, NB), lambda i, j: (j, 0, 0))],
        out_specs=[pl.BlockSpec((TM, D), lambda i, j: (i, 0)), pl.BlockSpec((TM, NB), lambda i, j: (i, j))],
        out_shape=[jax.ShapeDtypeStruct((S, D), BF16), jax.ShapeDtypeStruct((S, NDEV * NB), F32)],
        args=(x, gpre, scale1, shift, w_in_l), rider=rider)


def _conv(E, cw_ref, cb_ref):
    w = cw_ref[...]
    acc = E * w[3:4, :]
    for k in range(CONV_K - 1):
        acc = acc + pltpu.roll(E, CONV_K - 1 - k, axis=0) * w[k:k + 1, :]
    return acc[CONV_HALO:, :] + cb_ref[...]


def _gates(u, wa_ref, ba_ref, wx_ref, bx_ref, lam_ref):
    ub = u.astype(BF16)
    zr = jnp.concatenate([jnp.dot(ub[:, h * HD:(h + 1) * HD], wa_ref[h], preferred_element_type=F32)
                          for h in range(NHEAD)], axis=1)
    zi = jnp.concatenate([jnp.dot(ub[:, h * HD:(h + 1) * HD], wx_ref[h], preferred_element_type=F32)
                          for h in range(NHEAD)], axis=1)
    r = jax.nn.sigmoid(zr + ba_ref[...])
    ig = jax.nn.sigmoid(zi + bx_ref[...])
    sp = jax.nn.softplus(-lam_ref[...])
    log_a = (-LRU_C * r) * sp
    a = jnp.exp(log_a)
    mult = jnp.sqrt(_one_minus_exp(2.0 * log_a))
    return a, mult, r, ig, sp


def _scan(a, v, sa, sv, carry_ref, out_ref, reverse):
    T = a.shape[0]
    n8 = T // SUB
    A = a.reshape(n8, SUB, D)
    V = v.reshape(n8, SUB, D)
    row = lax.broadcasted_iota(jnp.int32, (n8, SUB, D), 1)
    for d in (1, 2, 4):
        sh = SUB - d if reverse else d
        keep = (row < SUB - d) if reverse else (row >= d)
        Ar = pltpu.roll(A, sh, axis=1)
        Vr = pltpu.roll(V, sh, axis=1)
        V = V + A * jnp.where(keep, Vr, 0.0)
        A = A * jnp.where(keep, Ar, 1.0)
    sa[...] = A.reshape(T, D)
    sv[...] = V.reshape(T, D)
    edge = 0 if reverse else SUB - 1

    def step(k, c):
        r0 = pl.multiple_of((n8 - 1 - k if reverse else k) * SUB, SUB)
        h = sv[pl.ds(r0, SUB), :] + sa[pl.ds(r0, SUB), :] * c
        out_ref[pl.ds(r0, SUB), :] = h
        return jnp.broadcast_to(h[edge:edge + 1, :], (SUB, D))

    carry_ref[...] = lax.fori_loop(0, n8, step, carry_ref[...])


def _rnn_fwd(proj, conv_w, conv_b, wa, ba, wx, bx, lam, rider=None):
    S = proj.shape[0]
    TB = min(256, S)

    def body(xr_ref, g_ref, cw_ref, cb_ref, wa_ref, ba_ref, wx_ref, bx_ref, lam_ref, hs_ref, y_ref,
             xbuf, sa, sv, hc):
        @pl.when(pl.program_id(0) == 0)
        def _():
            xbuf[0:CONV_HALO, :] = jnp.zeros((CONV_HALO, D), F32)
            hc[...] = jnp.zeros((SUB, D), F32)
        xbuf[CONV_HALO:, :] = xr_ref[...]
        u = _conv(xbuf[...], cw_ref, cb_ref)
        xbuf[0:CONV_HALO, :] = xbuf[TB:TB + CONV_HALO, :]
        a, mult, _, ig, _ = _gates(u, wa_ref, ba_ref, wx_ref, bx_ref, lam_ref)
        _scan(a, mult * (ig * u), sa, sv, hc, hs_ref, reverse=False)
        silu, _ = _silu_parts(g_ref[...])
        y_ref[...] = (hs_ref[...] * silu).astype(BF16)

    return _pcall(
        body, name="rnn_fwd", grid=(S // TB,),
        in_specs=[pl.BlockSpec((TB, D), lambda i: (i, 0)), pl.BlockSpec((TB, D), lambda i: (i, 1)),
                  _full((CONV_K, D)), _vec(), _full((NHEAD, HD, HD)), _vec(), _full((NHEAD, HD, HD)), _vec(),
                  _vec()],
        out_specs=[pl.BlockSpec((TB, D), lambda i: (i, 0)), pl.BlockSpec((TB, D), lambda i: (i, 0))],
        out_shape=[jax.ShapeDtypeStruct((S, D), F32), jax.ShapeDtypeStruct((S, D), BF16)],
        scratch_shapes=[pltpu.VMEM((TB + CONV_HALO, D), F32), pltpu.VMEM((TB, D), F32),
                        pltpu.VMEM((TB, D), F32), pltpu.VMEM((SUB, D), F32)],
        args=(proj, proj, conv_w, conv_b, wa, ba, wx, bx, lam), rider=rider)


def _pooled(ebuf, t0, TB):
    tt = t0 + lax.broadcasted_iota(jnp.int32, (TB, 1), 0)
    pooled, inv = [], []
    for g, win in enumerate(WINS):
        Eg = ebuf[:, g * GD:(g + 1) * GD]
        L = Eg
        for lev in range(g + 1):
            L = L + pltpu.roll(L, 1 << lev, axis=0)
        icnt = 1.0 / jnp.minimum(tt + 1, win).astype(F32)
        pooled.append(L[POOL_HALO:, :] * icnt - Eg[POOL_HALO:, :])
        inv.append(icnt)
    return pooled, inv


def _pool_fwd(proj, pw, pb, ps, rider=None):
    S = proj.shape[0]
    TB = min(256, S)

    def body(xp_ref, g_ref, pw_ref, pb_ref, ps_ref, y_ref, ebuf):
        i = pl.program_id(0)

        @pl.when(i == 0)
        def _():
            ebuf[0:POOL_HALO, :] = jnp.zeros((POOL_HALO, D), F32)
        ebuf[POOL_HALO:, :] = xp_ref[...]
        pooled, _ = _pooled(ebuf, i * TB, TB)
        ebuf[0:POOL_HALO, :] = ebuf[TB:TB + POOL_HALO, :]
        yp = jnp.concatenate([jnp.dot(pooled[g].astype(BF16), pw_ref[g], preferred_element_type=F32)
                              for g in range(NGRP)], axis=1) + pb_ref[...]
        silu, _ = _silu_parts(g_ref[...])
        y_ref[...] = (yp * ps_ref[...] * silu).astype(BF16)

    return _pcall(
        body, name="pool_fwd", grid=(S // TB,),
        in_specs=[pl.BlockSpec((TB, D), lambda i: (i, 2)), pl.BlockSpec((TB, D), lambda i: (i, 3)),
                  _full((NGRP, GD, GD)), _vec(), _vec()],
        out_specs=[pl.BlockSpec((TB, D), lambda i: (i, 0))],
        out_shape=[jax.ShapeDtypeStruct((S, D), BF16)],
        scratch_shapes=[pltpu.VMEM((TB + POOL_HALO, D), F32)],
        args=(proj, proj, pw, pb, ps), rider=rider)


def _out_post(yr, yp, w_out_l, x, gate, gpost, target=None, rider=None):
    S = x.shape[0]
    TM = min(512, S)
    KB = w_out_l.shape[1]
    last = target is not None

    def body(*refs):
        if last:
            yr_ref, yp_ref, w_ref, x_ref, gate_ref, gp_ref, t_ref, y_ref, xo_ref, loss_ref = refs
        else:
            yr_ref, yp_ref, w_ref, x_ref, gate_ref, gp_ref, y_ref, xo_ref = refs
        acc = jnp.zeros((TM, D), F32)
        for j in range(NDEV):
            src = yr_ref if j < NDEV // 2 else yp_ref
            k0 = (j % (NDEV // 2)) * KB
            acc = acc + jnp.dot(src[:, k0:k0 + KB], w_ref[j], preferred_element_type=F32)
        y_ref[...] = acc
        rstd = lax.rsqrt(jnp.mean(acc * acc, axis=-1, keepdims=True) + NORM_EPS)
        xn = x_ref[...] + gate_ref[...] * (acc * rstd * gp_ref[...])
        if last:
            err = xn - t_ref[...]
            xo_ref[...] = err * (1.0 / D)

            @pl.when(pl.program_id(0) == 0)
            def _():
                loss_ref[...] = jnp.zeros((SUB, D), F32)
            loss_ref[...] += _rowsum8(err * err)
        else:
            xo_ref[...] = xn

    row = pl.BlockSpec((TM, D), lambda i: (i, 0))
    in_specs = [row, row, _full((NDEV, KB, D)), row, _vec(), _vec()]
    out_specs = [row, row]
    out_shape = [jax.ShapeDtypeStruct((S, D), F32), jax.ShapeDtypeStruct((S, D), F32)]
    args = [yr, yp, w_out_l, x, gate, gpost]
    if last:
        in_specs.append(row)
        out_specs.append(_full((SUB, D)))
        out_shape.append(jax.ShapeDtypeStruct((SUB, D), F32))
        args.append(target)
    return _pcall(body, name="out_post_loss" if last else "out_post", grid=(S // TM,), in_specs=in_specs,
                  out_specs=out_specs, out_shape=out_shape, args=args, rider=rider)


def _out_bwd(dxo, y, yr, yp, w_out_l, gate, gpost, rider=None):
    S = y.shape[0]
    TM = min(256, S)
    KB = w_out_l.shape[1]
    nsteps = S // TM

    def body(dxo_ref, y_ref, yr_ref, yp_ref, w_ref, gate_ref, gp_ref, dyr_ref, dyp_ref, gw_ref, dgate_ref,
             dgp_ref, gw_acc, vacc):
        i = pl.program_id(0)

        @pl.when(i == 0)
        def _():
            gw_acc[...] = jnp.zeros_like(gw_acc)
            vacc[...] = jnp.zeros_like(vacc)
        yv = y_ref[...]
        dxo_v = dxo_ref[...]
        rstd = lax.rsqrt(jnp.mean(yv * yv, axis=-1, keepdims=True) + NORM_EPS)
        n = yv * rstd
        gp = gp_ref[...]
        vacc[0] += _rowsum8(dxo_v * (n * gp))
        drn = dxo_v * gate_ref[...]
        vacc[1] += _rowsum8(drn * n)
        dn = drn * gp
        dy = (rstd * (dn - n * jnp.mean(dn * n, axis=-1, keepdims=True))).astype(BF16)
        for j in range(NDEV):
            dst = dyr_ref if j < NDEV // 2 else dyp_ref
            k0 = (j % (NDEV // 2)) * KB
            dst[:, k0:k0 + KB] = lax.dot_general(dy, w_ref[j], (((1,), (1,)), ((), ())),
                                                 preferred_element_type=F32)
        gw_acc[0:D, :] += lax.dot_general(yr_ref[...], dy, (((0,), (0,)), ((), ())), preferred_element_type=F32)
        gw_acc[D:2 * D, :] += lax.dot_general(yp_ref[...], dy, (((0,), (0,)), ((), ())),
                                              preferred_element_type=F32)

        @pl.when(i == nsteps - 1)
        def _():
            gw_ref[...] = gw_acc[...].astype(BF16)
            dgate_ref[...] = _sum8(vacc[0])
            dgp_ref[...] = _sum8(vacc[1])

    row = pl.BlockSpec((TM, D), lambda i: (i, 0))
    return _pcall(
        body, name="out_bwd", grid=(nsteps,),
        in_specs=[row, row, row, row, _full((NDEV, KB, D)), _vec(), _vec()],
        out_specs=[row, row, _full((2 * D, D)), _vec(), _vec()],
        out_shape=[jax.ShapeDtypeStruct((S, D), F32), jax.ShapeDtypeStruct((S, D), F32),
                   jax.ShapeDtypeStruct((2 * D, D), BF16), jax.ShapeDtypeStruct((1, D), F32),
                   jax.ShapeDtypeStruct((1, D), F32)],
        scratch_shapes=[pltpu.VMEM((2 * D, D), F32), pltpu.VMEM((2, SUB, D), F32)],
        args=(dxo, y, yr, yp, w_out_l, gate, gpost), rider=rider)


def _rnn_bwd(dyr, hs, proj, conv_w, conv_b, wa, ba, wx, bx, lam, rider=None):
    S = proj.shape[0]
    TB = min(256, S)
    nb = S // TB
    TE = TB + CONV_HALO
    A_BA, A_BX, A_LAM, A_CB, A_CW = 0, 1, 2, 3, 4

    def blk(i):
        return nb - 1 - i

    def prev8(i):
        return jnp.maximum(blk(i) * (TB // SUB) - 1, 0)

    def body(dyr_ref, hs_ref, hprev_ref, xr_ref, xprev_ref, g_ref, cw_ref, cb_ref, wa_ref, ba_ref, wx_ref,
             bx_ref, lam_ref, dxr_ref, dg_ref, gcw_ref, gcb_ref, gwa_ref, gba_ref, gwx_ref, gbx_ref, glam_ref,
             xbuf, hbuf, abuf, dbuf, sa, sv, dh_ref, dhc, vacc):
        i = pl.program_id(0)
        first = blk(i) == 0

        @pl.when(i == 0)
        def _():
            abuf[TB:, :] = jnp.zeros((CONV_HALO, D), F32)
            dbuf[TB:, :] = jnp.zeros((CONV_HALO, D), F32)
            dhc[...] = jnp.zeros_like(dhc)
            vacc[...] = jnp.zeros_like(vacc)
            gwa_ref[...] = jnp.zeros_like(gwa_ref)
            gwx_ref[...] = jnp.zeros_like(gwx_ref)

        xbuf[0:CONV_HALO, :] = jnp.where(first, 0.0, xprev_ref[...])
        xbuf[CONV_HALO:, :] = xr_ref[...]
        E = xbuf[...]
        u = _conv(E, cw_ref, cb_ref)
        a, mult, r, ig, sp = _gates(u, wa_ref, ba_ref, wx_ref, bx_ref, lam_ref)
        hbuf[0:CONV_HALO, :] = jnp.where(first, 0.0, hprev_ref[...])
        hbuf[CONV_HALO:, :] = hs_ref[...]
        hprev = pltpu.roll(hbuf[...], 1, axis=0)[CONV_HALO:, :]

        silu, dsilu = _silu_parts(g_ref[...])
        dyv = dyr_ref[...]
        dg_ref[...] = (dyv * hs_ref[...] * dsilu).astype(BF16)

        abuf[0:TB, :] = a
        b = pltpu.roll(abuf[...], TE - 1, axis=0)[0:TB, :]
        _scan(b, dyv * silu, sa, sv, dhc, dh_ref, reverse=True)
        abuf[TB:, :] = a[0:CONV_HALO, :]
        dh = dh_ref[...]

        dlog_a = dh * hprev * a - (dh * ig * u) * (a * a) / mult
        vacc[A_LAM] += _rowsum8(dlog_a * r)
        dzr = dlog_a * (-LRU_C * sp) * r * (1.0 - r)
        dzi = (dh * mult * u) * ig * (1.0 - ig)
        vacc[A_BA] += _rowsum8(dzr)
        vacc[A_BX] += _rowsum8(dzi)
        ub, dzrb, dzib = u.astype(BF16), dzr.astype(BF16), dzi.astype(BF16)
        du_g = []
        for h in range(NHEAD):
            cs = slice(h * HD, (h + 1) * HD)
            gwa_ref[h] += lax.dot_general(ub[:, cs], dzrb[:, cs], (((0,), (0,)), ((), ())),
                                          preferred_element_type=F32)
            gwx_ref[h] += lax.dot_general(ub[:, cs], dzib[:, cs], (((0,), (0,)), ((), ())),
                                          preferred_element_type=F32)
            du_g.append(lax.dot_general(dzrb[:, cs], wa_ref[h], (((1,), (1,)), ((), ())),
                                        preferred_element_type=F32)
                        + lax.dot_general(dzib[:, cs], wx_ref[h], (((1,), (1,)), ((), ())),
                                          preferred_element_type=F32))
        du = dh * mult * ig + jnp.concatenate(du_g, axis=1)

        dbuf[0:TB, :] = du
        Dd = dbuf[...]
        w = cw_ref[...]
        dx = Dd * w[3:4, :]
        for k in range(CONV_K - 1):
            dx = dx + pltpu.roll(Dd, TE - (CONV_K - 1 - k), axis=0) * w[k:k + 1, :]
        dxr_ref[...] = dx[0:TB, :].astype(BF16)
        dbuf[TB:, :] = du[0:CONV_HALO, :]
        vacc[A_CB] += _rowsum8(du)
        vacc[A_CW + CONV_K - 1] += _rowsum8(du * E[CONV_HALO:, :])
        for k in range(CONV_K - 1):
            vacc[A_CW + k] += _rowsum8(du * pltpu.roll(E, CONV_K - 1 - k, axis=0)[CONV_HALO:, :])

        @pl.when(i == nb - 1)
        def _():
            gba_ref[...] = _sum8(vacc[A_BA])
            gbx_ref[...] = _sum8(vacc[A_BX])
            glam_ref[...] = _sum8(vacc[A_LAM]) * (LRU_C * jax.nn.sigmoid(-lam_ref[...]))
            gcb_ref[...] = _sum8(vacc[A_CB])
            for k in range(CONV_K):
                gcw_ref[k:k + 1, :] = _sum8(vacc[A_CW + k])

    rowb = pl.BlockSpec((TB, D), lambda i: (blk(i), 0))
    halo = pl.BlockSpec((SUB, D), lambda i: (prev8(i), 0))
    wspec = _full((NHEAD, HD, HD))
    return _pcall(
        body, name="rnn_bwd", grid=(nb,),
        in_specs=[rowb, rowb, halo, rowb, halo, pl.BlockSpec((TB, D), lambda i: (blk(i), 1)),
                  _full((CONV_K, D)), _vec(), wspec, _vec(), wspec, _vec(), _vec()],
        out_specs=[rowb, rowb, _full((CONV_K, D)), _vec(), wspec, _vec(), wspec, _vec(), _vec()],
        out_shape=[jax.ShapeDtypeStruct((S, D), BF16), jax.ShapeDtypeStruct((S, D), BF16),
                   jax.ShapeDtypeStruct((CONV_K, D), F32), jax.ShapeDtypeStruct((1, D), F32),
                   jax.ShapeDtypeStruct((NHEAD, HD, HD), F32), jax.ShapeDtypeStruct((1, D), F32),
                   jax.ShapeDtypeStruct((NHEAD, HD, HD), F32), jax.ShapeDtypeStruct((1, D), F32),
                   jax.ShapeDtypeStruct((1, D), F32)],
        scratch_shapes=[pltpu.VMEM((TE, D), F32), pltpu.VMEM((TE, D), F32), pltpu.VMEM((TE, D), F32),
                        pltpu.VMEM((TE, D), F32), pltpu.VMEM((TB, D), F32), pltpu.VMEM((TB, D), F32),
                        pltpu.VMEM((TB, D), F32), pltpu.VMEM((SUB, D), F32),
                        pltpu.VMEM((A_CW + CONV_K, SUB, D), F32)],
        args=(dyr, hs, hs, proj, proj, proj, conv_w, conv_b, wa, ba, wx, bx, lam), rider=rider)


def _pool_bwd(dyp, proj, pw, pb, ps, rider=None):
    S = proj.shape[0]
    TB = min(256, S)
    nb = S // TB
    TE = TB + POOL_HALO

    def blk(i):
        return nb - 1 - i

    def body(dy_ref, xp_ref, xprev_ref, g_ref, pw_ref, pb_ref, ps_ref, dxp_ref, dg_ref, gpw_ref, gpb_ref,
             gps_ref, ebuf, qbuf, vacc):
        i = pl.program_id(0)
        first = blk(i) == 0

        @pl.when(i == 0)
        def _():
            qbuf[TB:, :] = jnp.zeros((POOL_HALO, D), F32)
            vacc[...] = jnp.zeros_like(vacc)
            gpw_ref[...] = jnp.zeros_like(gpw_ref)

        ebuf[0:POOL_HALO, :] = jnp.where(first, 0.0, xprev_ref[...])
        ebuf[POOL_HALO:, :] = xp_ref[...]
        pooled, inv = _pooled(ebuf, blk(i) * TB, TB)
        pooled = [p.astype(BF16) for p in pooled]
        yp = jnp.concatenate([jnp.dot(pooled[g], pw_ref[g], preferred_element_type=F32)
                              for g in range(NGRP)], axis=1) + pb_ref[...]
        silu, dsilu = _silu_parts(g_ref[...])
        dy = dy_ref[...]
        ps = ps_ref[...]
        dyp_v = dy * ps * silu
        vacc[0] += _rowsum8(dy * yp * silu)
        vacc[1] += _rowsum8(dyp_v)
        dg_ref[...] = (dy * yp * ps * dsilu).astype(BF16)
        dypb = dyp_v.astype(BF16)
        for g in range(NGRP):
            cs = slice(g * GD, (g + 1) * GD)
            gpw_ref[g] += lax.dot_general(pooled[g], dypb[:, cs], (((0,), (0,)), ((), ())),
                                          preferred_element_type=F32)
            dpool = lax.dot_general(dypb[:, cs], pw_ref[g], (((1,), (1,)), ((), ())),
                                    preferred_element_type=F32)
            qbuf[0:TB, cs] = dpool * inv[g]
            L = qbuf[:, cs]
            for lev in range(g + 1):
                L = L + pltpu.roll(L, TE - (1 << lev), axis=0)
            dxp_ref[:, cs] = (L[0:TB, :] - dpool).astype(BF16)
        qbuf[TB:, :] = qbuf[0:POOL_HALO, :]

        @pl.when(i == nb - 1)
        def _():
            gps_ref[...] = _sum8(vacc[0])
            gpb_ref[...] = _sum8(vacc[1])

    rowb = pl.BlockSpec((TB, D), lambda i: (blk(i), 0))
    return _pcall(
        body, name="pool_bwd", grid=(nb,),
        in_specs=[rowb, pl.BlockSpec((TB, D), lambda i: (blk(i), 2)),
                  pl.BlockSpec((POOL_HALO, D), lambda i: (jnp.maximum(blk(i) * (TB // POOL_HALO) - 1, 0), 2)),
                  pl.BlockSpec((TB, D), lambda i: (blk(i), 3)), _full((NGRP, GD, GD)), _vec(), _vec()],
        out_specs=[rowb, rowb, _full((NGRP, GD, GD)), _vec(), _vec()],
        out_shape=[jax.ShapeDtypeStruct((S, D), BF16), jax.ShapeDtypeStruct((S, D), BF16),
                   jax.ShapeDtypeStruct((NGRP, GD, GD), F32), jax.ShapeDtypeStruct((1, D), F32),
                   jax.ShapeDtypeStruct((1, D), F32)],
        scratch_shapes=[pltpu.VMEM((TE, D), F32), pltpu.VMEM((TE, D), F32), pltpu.VMEM((2, SUB, D), F32)],
        args=(dyp, proj, proj, proj, pw, pb, ps), rider=rider)


def _in_bwd(dq, w_in_l, x, dxo, gpre, scale1, rider=None):
    S = x.shape[0]
    TM = min(256, S)
    NB = w_in_l.shape[2]
    nsteps = S // TM
    per_q = D // NB

    def body(d0, d1, d2, d3, w_ref, x_ref, dxo_ref, g_ref, sc_ref, dx_ref, dsh_ref, dsc_ref, dg_ref, vacc):
        i = pl.program_id(0)

        @pl.when(i == 0)
        def _():
            vacc[...] = jnp.zeros_like(vacc)
        dref = (d0, d1, d2, d3)
        dh = jnp.zeros((TM, D), F32)
        for j in range(NDEV):
            c0 = (j % per_q) * NB
            dh = dh + lax.dot_general(dref[j // per_q][:, c0:c0 + NB], w_ref[j], (((1,), (1,)), ((), ())),
                                      preferred_element_type=F32)
        xv = x_ref[...]
        rstd = lax.rsqrt(jnp.mean(xv * xv, axis=-1, keepdims=True) + NORM_EPS)
        xn = xv * rstd
        g, sc = g_ref[...], sc_ref[...]
        vacc[0] += _rowsum8(dh)
        vacc[1] += _rowsum8(dh * (xn * g))
        vacc[2] += _rowsum8(dh * sc * xn)
        dxn = dh * sc * g
        dx_ref[...] = dxo_ref[...] + rstd * (dxn - xn * jnp.mean(dxn * xn, axis=-1, keepdims=True))

        @pl.when(i == nsteps - 1)
        def _():
            dsh_ref[...] = _sum8(vacc[0])
            dsc_ref[...] = _sum8(vacc[1])
            dg_ref[...] = _sum8(vacc[2])

    row = pl.BlockSpec((TM, D), lambda i: (i, 0))
    return _pcall(
        body, name="in_bwd", grid=(nsteps,),
        in_specs=[row, row, row, row, _full((NDEV, D, NB)), row, row, _vec(), _vec()],
        out_specs=[row, _vec(), _vec(), _vec()],
        out_shape=[jax.ShapeDtypeStruct((S, D), F32)] + [jax.ShapeDtypeStruct((1, D), F32)] * 3,
        scratch_shapes=[pltpu.VMEM((3, SUB, D), F32)],
        args=(*dq, w_in_l, x, dxo, gpre, scale1), rider=rider)


def _grad_w_in(h, dq, NB, rider=None):
    S = h.shape[0]
    TK = min(512, S)
    nk = S // TK
    per_q = D // NB

    def body(h_ref, d0, d1, d2, d3, o_ref, acc):
        j, k = pl.program_id(0), pl.program_id(1)

        @pl.when(k == 0)
        def _():
            acc[...] = jnp.zeros_like(acc)
        for q, d_ref in enumerate((d0, d1, d2, d3)):
            @pl.when(j // per_q == q)
            def _(d_ref=d_ref):
                acc[...] += lax.dot_general(h_ref[...], d_ref[...], (((0,), (0,)), ((), ())),
                                            preferred_element_type=F32)

        @pl.when(k == nk - 1)
        def _():
            o_ref[...] = acc[...].astype(BF16)

    d_specs = [pl.BlockSpec((TK, NB), lambda j, k, q=q: (jnp.where(j // per_q == q, k, 0),
                                                          jnp.where(j // per_q == q, j % per_q, 0)))
               for q in range(NQ)]
    return _pcall(
        body, name="grad_w_in", grid=(NDEV, nk),
        in_specs=[pl.BlockSpec((TK, D), lambda j, k: (k, 0))] + d_specs,
        out_specs=[pl.BlockSpec((None, D, NB), lambda j, k: (j, 0, 0))],
        out_shape=[jax.ShapeDtypeStruct((NDEV, D, NB), BF16)],
        scratch_shapes=[pltpu.VMEM((D, NB), F32)],
        args=(h, *dq), rider=rider)


def _adamw_math(g, w, m, v):
    m2 = ADAM_B1 * m + (1.0 - ADAM_B1) * g
    v2 = ADAM_B2 * v + (1.0 - ADAM_B2) * (g * g)
    m_hat = m2 / (1.0 - ADAM_B1 ** ADAM_STEP)
    v_hat = v2 / (1.0 - ADAM_B2 ** ADAM_STEP)
    delta = -ADAM_LR * (m_hat / (jnp.sqrt(v_hat) + ADAM_EPS) + ADAM_WD * w)
    return delta, m2, v2


def _adamw(name, gs, w, m, v, TR, rider=None):
    L = len(gs)
    n, R, C = gs[0].shape

    def body(*refs):
        g_refs = refs[:L]
        w_ref, m_ref, v_ref, go_ref, do_ref, mo_ref, vo_ref = refs[L:]
        lay = pl.program_id(0)
        for li in range(L):
            @pl.when(lay == li)
            def _(li=li):
                g = g_refs[li][0].astype(F32)
                for s in range(1, n):
                    g = g + g_refs[li][s].astype(F32)
                delta, m2, v2 = _adamw_math(g, w_ref[...], m_ref[...], v_ref[...])
                go_ref[...] = g
                do_ref[...] = delta
                mo_ref[...] = m2
                vo_ref[...] = v2

    lrc = pl.BlockSpec((None, TR, C), lambda lay, r: (lay, r, 0))
    g_specs = [pl.BlockSpec((n, TR, C), lambda lay, r, li=li: (0, jnp.where(lay == li, r, 0), 0))
               for li in range(L)]
    return _pcall(
        body, name=name, grid=(L, R // TR),
        in_specs=g_specs + [lrc, lrc, lrc], out_specs=[lrc] * 4,
        out_shape=[jax.ShapeDtypeStruct((L, R, C), F32)] * 4,
        args=(*gs, w, m, v), rider=rider)


def _ada_adamw(c_all_t, dm, w, m, v, rider=None):
    L, _, nc = w.shape

    def body(c_ref, dm_ref, w_ref, m_ref, v_ref, go_ref, do_ref, mo_ref, vo_ref):
        cv = c_ref[...]
        ca = cv * jax.nn.sigmoid(cv)
        dmv = dm_ref[...]
        g = ca[:, 0:1] * dmv[0:1, :]
        for b in range(1, NDEV):
            g = g + ca[:, b:b + 1] * dmv[b:b + 1, :]
        delta, m2, v2 = _adamw_math(g, w_ref[...], m_ref[...], v_ref[...])
        go_ref[...] = g
        do_ref[...] = delta
        mo_ref[...] = m2
        vo_ref[...] = v2

    big = pl.BlockSpec((None, D, nc), lambda lay: (lay, 0, 0))
    return _pcall(
        body, name="ada_adamw", grid=(L,),
        in_specs=[_full((D, NDEV)), pl.BlockSpec((None, NDEV, nc), lambda lay: (lay, 0, 0)), big, big, big],
        out_specs=[big] * 4, out_shape=[jax.ShapeDtypeStruct((L, D, nc), F32)] * 4,
        args=(c_all_t, dm, w, m, v), rider=rider)


def _sum_slots(recv):
    n, R, C = recv.shape

    def body(r_ref, o_ref):
        acc = r_ref[0]
        for s in range(1, n):
            acc = acc + r_ref[s]
        o_ref[...] = acc

    return pl.pallas_call(body, name="sum_slots", out_shape=jax.ShapeDtypeStruct((R, C), F32),
                          compiler_params=_cparams())(recv)


def _pad_rows(a, rows):
    return jnp.pad(a, ((0, rows - a.shape[0]), (0, 0)))


def _pack_sharded_block(pool_w, pool_b, conv_w):
    return jnp.concatenate([pool_w.reshape(-1, PACK_C), _pad_rows(pool_b.reshape(-1, PACK_C), SUB),
                            _pad_rows(conv_w.reshape(-1, PACK_C), SUB)], axis=0)


def _unpack_sharded_block(p):
    n_pw = DEPTH * NGRP * (GD // NDEV)
    pool_w = p[:n_pw].reshape(DEPTH, NGRP, GD // NDEV, GD)
    pool_b = p[n_pw].reshape(DEPTH, NGRP, GD // NDEV)
    conv_w = p[n_pw + SUB:n_pw + SUB + DEPTH * CONV_K * (D // NDEV) // PACK_C].reshape(DEPTH, CONV_K, D // NDEV)
    return pool_w, pool_b, conv_w


def _blocks_of_full(pool_w, pool_b, conv_w):
    pw = pool_w.reshape(DEPTH, NGRP, NDEV, GD // NDEV, GD).transpose(2, 0, 1, 3, 4).reshape(NDEV, -1, PACK_C)
    pb = pool_b.reshape(DEPTH, NGRP, NDEV, GD // NDEV).transpose(2, 0, 1, 3).reshape(NDEV, -1, PACK_C)
    cw = conv_w.reshape(DEPTH, CONV_K, NDEV, D // NDEV).transpose(2, 0, 1, 3).reshape(NDEV, -1, PACK_C)
    pad = lambda a: jnp.pad(a, ((0, 0), (0, SUB - a.shape[1]), (0, 0)))
    return jnp.concatenate([pw, pad(pb), pad(cw)], axis=1)


def _full_of_blocks(p):
    n_pw = DEPTH * NGRP * (GD // NDEV)
    pool_w = p[:, :n_pw].reshape(NDEV, DEPTH, NGRP, GD // NDEV, GD).transpose(1, 2, 0, 3, 4)
    pool_b = p[:, n_pw].reshape(NDEV, DEPTH, NGRP, GD // NDEV).transpose(1, 2, 0, 3)
    n_cw = DEPTH * CONV_K * (D // NDEV) // PACK_C
    conv_w = p[:, n_pw + SUB:n_pw + SUB + n_cw].reshape(NDEV, DEPTH, CONV_K, D // NDEV).transpose(1, 2, 0, 3)
    return (pool_w.reshape(DEPTH, NGRP, GD, GD), pool_b.reshape(DEPTH, NGRP, GD),
            conv_w.reshape(DEPTH, CONV_K, D))


def _pack_replicated(t):
    p = jnp.concatenate([t[k].reshape(-1, PACK_C) for k in REPLICATED], axis=0)
    return _pad_rows(p, NDEV * PACK_ROWS)


def _unpack_replicated(p, like):
    out, r0 = {}, 0
    for k in REPLICATED:
        rows = like[k].size // PACK_C
        out[k] = p[r0:r0 + rows].reshape(like[k].shape)
        r0 += rows
    return out


def kernel(x, c, ada_w, ada_b, pre_norm_g, w_in, conv_w, conv_b, gate_a_w, gate_a_b, gate_x_w, gate_x_b, lru_lambda, pool_w, pool_b, pool_scale, w_out, post_norm_g, loss_target, m_ada_w, m_ada_b, m_pre_norm_g, m_w_in, m_conv_w, m_conv_b, m_gate_a_w, m_gate_a_b, m_gate_x_w, m_gate_x_b, m_lru_lambda, m_pool_w, m_pool_b, m_pool_scale, m_w_out, m_post_norm_g, v_ada_w, v_ada_b, v_pre_norm_g, v_w_in, v_conv_w, v_conv_b, v_gate_a_w, v_gate_a_b, v_gate_x_w, v_gate_x_b, v_lru_lambda, v_pool_w, v_pool_b, v_pool_scale, v_w_out, v_post_norm_g):
    W = dict(ada_w=ada_w, ada_b=ada_b, pre_norm_g=pre_norm_g, w_in=w_in, conv_w=conv_w, conv_b=conv_b,
             gate_a_w=gate_a_w, gate_a_b=gate_a_b, gate_x_w=gate_x_w, gate_x_b=gate_x_b, lru_lambda=lru_lambda,
             pool_w=pool_w, pool_b=pool_b, pool_scale=pool_scale, w_out=w_out, post_norm_g=post_norm_g)
    M = dict(ada_w=m_ada_w, ada_b=m_ada_b, pre_norm_g=m_pre_norm_g, w_in=m_w_in, conv_w=m_conv_w,
             conv_b=m_conv_b, gate_a_w=m_gate_a_w, gate_a_b=m_gate_a_b, gate_x_w=m_gate_x_w,
             gate_x_b=m_gate_x_b, lru_lambda=m_lru_lambda, pool_w=m_pool_w, pool_b=m_pool_b,
             pool_scale=m_pool_scale, w_out=m_w_out, post_norm_g=m_post_norm_g)
    V = dict(ada_w=v_ada_w, ada_b=v_ada_b, pre_norm_g=v_pre_norm_g, w_in=v_w_in, conv_w=v_conv_w,
             conv_b=v_conv_b, gate_a_w=v_gate_a_w, gate_a_b=v_gate_a_b, gate_x_w=v_gate_x_w,
             gate_x_b=v_gate_x_b, lru_lambda=v_lru_lambda, pool_w=v_pool_w, pool_b=v_pool_b,
             pool_scale=v_pool_scale, w_out=v_w_out, post_norm_g=v_post_norm_g)
    S = x.shape[1]
    me = 4 * lax.axis_index("x") + 2 * lax.axis_index("y") + lax.axis_index("c")
    xs = x.reshape(S, D)
    tgt = loss_target.reshape(S, D)
    nc = ada_w.shape[2]
    NB = w_in.shape[2]
    vec = lambda a: a.reshape(1, D)
    w_in_b, w_out_b = w_in.astype(BF16), w_out.astype(BF16)

    c_slots, w_in0 = _exchange("gather_c_w_in0", _AllGather2([jnp.broadcast_to(c, (SUB, D)), w_in_b[0]]))
    c_all = c_slots[:, 0, :]
    (mod_slots,) = _exchange("gather_mod", _Direct(ag=[_mod_cols(c_all, ada_w)]))
    mod = lax.dynamic_index_in_dim(mod_slots, me, axis=1, keepdims=False)
    mod = mod.reshape(NDEV, DEPTH, nc).transpose(1, 0, 2).reshape(DEPTH, 3 * D) + ada_b
    mods = [(vec(mod[l, :D]), vec(1.0 + mod[l, D:2 * D]), vec(mod[l, 2 * D:])) for l in range(DEPTH)]

    w_in_all, w_out_all = [w_in0, None], [None, None]
    saved = []
    xl = xs
    for l in range(DEPTH):
        shift, scale1, gate = mods[l]
        wa, wx = gate_a_w[l].astype(BF16), gate_x_w[l].astype(BF16)
        rider = _AllGather2([_pack_sharded_block(pool_w, pool_b, conv_w), w_out_b[0], w_out_b[1]]) if l == 0 else None
        (h, proj), got = _pre_proj(xl, vec(pre_norm_g[l]), scale1, shift, w_in_all[l], rider=rider)
        if l == 0:
            pool_w_f, pool_b_f, conv_w_f = _full_of_blocks(got[0])
            w_out_all = [got[1], got[2]]
        pw = pool_w_f[l].astype(BF16)
        rider = _AllGather2([w_in_b[1]]) if l == 0 else None
        (hs, yr), got = _rnn_fwd(proj, conv_w_f[l], vec(conv_b[l]), wa, gate_a_b[l].reshape(1, D), wx,
                                 gate_x_b[l].reshape(1, D), vec(lru_lambda[l]), rider=rider)
        if l == 0:
            w_in_all[1] = got[0]
        (yp,), _ = _pool_fwd(proj, pw, pool_b_f[l].reshape(1, D), vec(pool_scale[l]))
        if l == DEPTH - 1:
            (y, x_next, loss_acc), _ = _out_post(yr, yp, w_out_all[l], xl, gate, vec(post_norm_g[l]), tgt)
        else:
            (y, x_next), _ = _out_post(yr, yp, w_out_all[l], xl, gate, vec(post_norm_g[l]))
        saved.append((xl, h, proj, hs, yr, yp, y, wa, wx, pw))
        xl = x_next
    loss = lax.psum((0.5 / D) * jnp.sum(loss_acc), ("x", "y", "c"))

    dxo = xl
    G = {k: [None] * DEPTH for k in WEIGHTS}
    dmod = [None] * DEPTH
    recv_in, recv_out = [None] * DEPTH, [None] * DEPTH
    for l in reversed(range(DEPTH)):
        xin, h, proj, hs, yr, yp, y, wa, wx, pw = saved[l]
        shift, scale1, gate = mods[l]
        (dyr, dyp, gw_out, dgate, G['post_norm_g'][l]), _ = _out_bwd(dxo, y, yr, yp, w_out_all[l], gate,
                                                                     vec(post_norm_g[l]))
        ((dxr, dgr, G['conv_w'][l], G['conv_b'][l], G['gate_a_w'][l], G['gate_a_b'][l], G['gate_x_w'][l],
          G['gate_x_b'][l], G['lru_lambda'][l]), (recv_out[l],)) = _rnn_bwd(
            dyr, hs, proj, conv_w_f[l], vec(conv_b[l]), wa, gate_a_b[l].reshape(1, D), wx,
            gate_x_b[l].reshape(1, D), vec(lru_lambda[l]),
            rider=_Direct(a2a=[gw_out.reshape(NDEV, 2 * D // NDEV, D)]))
        (dxp, dgp, G['pool_w'][l], G['pool_b'][l], G['pool_scale'][l]), _ = _pool_bwd(
            dyp, proj, pw, pool_b_f[l].reshape(1, D), vec(pool_scale[l]))
        dq = (dxr, dgr, dxp, dgp)
        (gw_in,), _ = _grad_w_in(h, dq, NB)
        (dxo, dshift, dscale, G['pre_norm_g'][l]), (recv_in[l],) = _in_bwd(
            dq, w_in_all[l], xin, dxo, vec(pre_norm_g[l]), scale1, rider=_Direct(a2a=[gw_in]))
        dmod[l] = jnp.concatenate([dshift, dscale, dgate], axis=1)
    grad_x = dxo.reshape(x.shape)

    full = dict(conv_w=(CONV_K, D), pool_w=(NGRP, GD, GD), pool_b=(NGRP, GD))
    Gs = {k: jnp.stack([g.reshape(full.get(k, W[k].shape[1:])) for g in G[k]]) for k in WEIGHTS
          if k not in ('ada_w', 'ada_b', 'w_in', 'w_out')}
    Gs['ada_b'] = jnp.concatenate(dmod, axis=0)
    small = jnp.concatenate([_blocks_of_full(Gs['pool_w'], Gs['pool_b'], Gs['conv_w']),
                             _pack_replicated(Gs).reshape(NDEV, PACK_ROWS, PACK_C)], axis=1)

    out = {}
    out['w_in'], (small_recv, dmod_all) = _adamw("adamw_w_in", recv_in, w_in, M['w_in'], V['w_in'], 256,
                                                 rider=_Direct(a2a=[small], ag=[Gs['ada_b']]))
    small_sum = _sum_slots(small_recv)
    out['w_out'], (rep_all,) = _adamw("adamw_w_out", recv_out, w_out, M['w_out'], V['w_out'], 256,
                                      rider=_AllGather2([small_sum[PACK_ROWS:]]))
    dm = lax.dynamic_slice_in_dim(dmod_all.reshape(NDEV, DEPTH, 3 * D), me * nc, nc, axis=2)
    out['ada_w'], _ = _ada_adamw(c_all.T, dm.transpose(1, 0, 2), ada_w, M['ada_w'], V['ada_w'])
    g_small = jnp.concatenate([small_sum[:PACK_ROWS], rep_all.reshape(NDEV * PACK_ROWS, PACK_C)], axis=0)

    def packs(T):
        return jnp.concatenate([_pack_sharded_block(T['pool_w'], T['pool_b'], T['conv_w']),
                                _pack_replicated(T)], axis=0)[None]
    res_small, _ = _adamw("adamw_small", [g_small[None]], packs(W), packs(M), packs(V), PACK_ROWS)
    for idx in range(4):
        p = res_small[idx][0]
        pw_, pb_, cw_ = _unpack_sharded_block(p[:PACK_ROWS])
        rep = _unpack_replicated(p[PACK_ROWS:], W)
        rep.update(pool_w=pw_, pool_b=pb_, conv_w=cw_)
        for k, a in rep.items():
            out.setdefault(k, [None] * 4)[idx] = a
    for k in ('w_in', 'w_out', 'ada_w'):
        out[k] = [a.reshape(W[k].shape) for a in out[k]]

    return (loss, grad_x, *[out[k][0] for k in WEIGHTS], *[out[k][1] for k in WEIGHTS],
            *[out[k][2] for k in WEIGHTS], *[out[k][3] for k in WEIGHTS])
```

```python
import functools

import jax
import jax.numpy as jnp
from jax import lax
from jax.experimental import pallas as pl
from jax.experimental.pallas import tpu as pltpu

F32, BF16 = jnp.float32, jnp.bfloat16
MESH = pl.DeviceIdType.MESH
HIGHEST = lax.Precision.HIGHEST

NDEV = 8
DEPTH = 2
D = 1024
NHEAD, HD = 8, 128
NGRP, GD = 4, 256
WINS = (2, 4, 8, 16)
CONV_K = 4
CONV_HALO = 8
POOL_HALO = 16
LRU_C = 8.0
NORM_EPS = 1e-6
ADAM_LR, ADAM_B1, ADAM_B2, ADAM_EPS, ADAM_WD, ADAM_STEP = 0.001, 0.9, 0.999, 1e-08, 0.01, 10
VMEM_LIMIT = 56 * 1024 * 1024
NQ = 4
SUB = 8
PACK_C = 256
PACK_ROWS = 272

WEIGHTS = ['ada_w', 'ada_b', 'pre_norm_g', 'w_in', 'conv_w', 'conv_b', 'gate_a_w', 'gate_a_b', 'gate_x_w',
           'gate_x_b', 'lru_lambda', 'pool_w', 'pool_b', 'pool_scale', 'w_out', 'post_norm_g']
REPLICATED = ['ada_b', 'pre_norm_g', 'conv_b', 'gate_a_w', 'gate_a_b', 'gate_x_w', 'gate_x_b', 'lru_lambda',
              'pool_scale', 'post_norm_g']


def _cparams(*sem):
    return pltpu.CompilerParams(dimension_semantics=sem, vmem_limit_bytes=VMEM_LIMIT)


def _vec(n=D):
    return pl.BlockSpec((1, n), lambda *_: (0, 0))


def _full(shape):
    nd = len(shape)
    return pl.BlockSpec(shape, lambda *_: (0,) * nd)


def _rowsum8(z):
    return z.reshape(z.shape[0] // SUB, SUB, z.shape[1]).sum(axis=0)


def _sum8(acc):
    return jnp.sum(acc, axis=0, keepdims=True)


def _silu_parts(g):
    sg = jax.nn.sigmoid(g)
    return g * sg, sg * (1.0 + g * (1.0 - sg))


def _one_minus_sq(a, log_a):
    z = 2.0 * log_a
    p = 1.0 / 24.0
    for k in (6.0, 2.0, 1.0):
        p = p * z + 1.0 / k
    return jnp.where(z > -0.03, -(p * z), 1.0 - a * a)


def _place():
    x, y, c = lax.axis_index("x"), lax.axis_index("y"), lax.axis_index("c")
    return x, y, c, 4 * x + 2 * y + c


class _Direct:
    def __init__(self, a2a=(), ag=()):
        self.arrays = list(a2a) + list(ag)
        self.n_a, self.n = len(a2a), len(self.arrays)
        self.out_shape = ([jax.ShapeDtypeStruct(a.shape, a.dtype) for a in a2a]
                          + [jax.ShapeDtypeStruct((NDEV,) + a.shape, a.dtype) for a in ag])
        self.scratch = [pltpu.SemaphoreType.DMA((self.n, NDEV - 1)), pltpu.SemaphoreType.DMA((self.n, NDEV - 1)),
                        pltpu.SemaphoreType.DMA((self.n,))]

    def _copies(self, ins, outs, sems):
        send_sems, recv_sems, local_sems = sems
        x, y, c, me = _place()
        local, remote = [], []
        for t in range(self.n):
            src = ins[t].at[me] if t < self.n_a else ins[t]
            local.append(pltpu.make_async_copy(src, outs[t].at[me], local_sems.at[t]))
        for r in range(1, NDEV):
            px = 1 - x if r & 4 else x
            py = 1 - y if r & 2 else y
            pc = 1 - c if r & 1 else c
            for t in range(self.n):
                src = ins[t].at[4 * px + 2 * py + pc] if t < self.n_a else ins[t]
                remote.append(pltpu.make_async_remote_copy(
                    src_ref=src, dst_ref=outs[t].at[me], send_sem=send_sems.at[t, r - 1],
                    recv_sem=recv_sems.at[t, r - 1], device_id=(px, py, pc), device_id_type=MESH))
        return local, remote

    def start(self, ins, outs, sems):
        local, remote = self._copies(ins, outs, sems)
        for cp in local + remote:
            cp.start()

    def finish(self, ins, outs, sems):
        local, remote = self._copies(ins, outs, sems)
        for cp in remote + local:
            cp.wait()


class _AllGather2:
    def __init__(self, arrays):
        self.arrays = list(arrays)
        self.n = len(self.arrays)
        self.out_shape = [jax.ShapeDtypeStruct((NDEV,) + a.shape, a.dtype) for a in self.arrays]
        self.scratch = [pltpu.SemaphoreType.DMA((self.n, NDEV - 1)), pltpu.SemaphoreType.DMA((self.n, NDEV - 1)),
                        pltpu.SemaphoreType.DMA((self.n,))]

    @staticmethod
    def _chips(x, y):
        return [(1 - x, y), (x, 1 - y), (1 - x, 1 - y)]

    def _copy(self, t, k, src, dst, to, sems):
        return pltpu.make_async_remote_copy(src_ref=src, dst_ref=dst, send_sem=sems[0].at[t, k],
                                            recv_sem=sems[1].at[t, k], device_id=to, device_id_type=MESH)

    def start(self, ins, outs, sems):
        x, y, c, me = _place()
        for t in range(self.n):
            pltpu.make_async_copy(ins[t], outs[t].at[me], sems[2].at[t]).start()
            self._copy(t, 0, ins[t], outs[t].at[me], (x, y, 1 - c), sems).start()
            for j, (px, py) in enumerate(self._chips(x, y)):
                self._copy(t, 1 + j, ins[t], outs[t].at[me], (px, py, c), sems).start()

    def finish(self, ins, outs, sems):
        x, y, c, me = _place()
        sib = (x, y, 1 - c)
        for j, (px, py) in enumerate(self._chips(x, y)):
            slot = 4 * px + 2 * py + c
            for t in range(self.n):
                self._copy(t, 1 + j, ins[t], outs[t].at[slot], sib, sems).wait_recv()
                self._copy(t, 4 + j, outs[t].at[slot], outs[t].at[slot], sib, sems).start()
        for t in range(self.n):
            for k in (0, 4, 5, 6):
                self._copy(t, k, ins[t], outs[t].at[me], sib, sems).wait_recv()
        for t in range(self.n):
            for k in range(NDEV - 1):
                self._copy(t, k, ins[t], outs[t].at[me], sib, sems).wait_send()
            pltpu.make_async_copy(ins[t], outs[t].at[me], sems[2].at[t]).wait()


def _exchange(name, rider):
    n = rider.n

    def body(*refs):
        rider.start(refs[:n], refs[n:2 * n], refs[2 * n:])
        rider.finish(refs[:n], refs[n:2 * n], refs[2 * n:])

    any_spec = pl.BlockSpec(memory_space=pl.ANY)
    return list(pl.pallas_call(body, name=name, out_shape=rider.out_shape, in_specs=[any_spec] * n,
                               out_specs=[any_spec] * n, scratch_shapes=rider.scratch)(*rider.arrays))


def _pcall(body, *, name, grid, in_specs, out_specs, out_shape, args, scratch_shapes=(), rider=None):
    params = _cparams(*(("arbitrary",) * len(grid)))
    if rider is None:
        res = pl.pallas_call(body, name=name, grid=grid, in_specs=in_specs, out_specs=out_specs,
                             out_shape=out_shape, scratch_shapes=list(scratch_shapes),
                             compiler_params=params)(*args)
        return list(res), []
    n_in, n_out, n_scr, rn = len(in_specs), len(out_specs), len(scratch_shapes), rider.n

    def wrapped(*refs):
        cuts = [n_in, rn, n_out, rn, n_scr]
        parts, p = [], 0
        for n in cuts:
            parts.append(refs[p:p + n])
            p += n
        ins, r_in, outs, r_out, scr = parts
        sems = refs[p:]
        ids = [pl.program_id(a) for a in range(len(grid))]
        first = functools.reduce(jnp.logical_and, [i == 0 for i in ids])
        last = functools.reduce(jnp.logical_and, [i == g - 1 for i, g in zip(ids, grid)])

        @pl.when(first)
        def _():
            rider.start(r_in, r_out, sems)
        body(*ins, *outs, *scr)

        @pl.when(last)
        def _():
            rider.finish(r_in, r_out, sems)

    any_spec = pl.BlockSpec(memory_space=pl.ANY)
    res = pl.pallas_call(
        wrapped, name=name, grid=grid, in_specs=list(in_specs) + [any_spec] * rn,
        out_specs=list(out_specs) + [any_spec] * rn, out_shape=list(out_shape) + rider.out_shape,
        scratch_shapes=list(scratch_shapes) + rider.scratch, compiler_params=params)(*args, *rider.arrays)
    return list(res[:n_out]), list(res[n_out:])


def _mod_cols(c_all, ada_w):
    nc = ada_w.shape[2]

    def body(c_ref, w_ref, o_ref):
        cv = c_ref[...]
        ca = cv * jax.nn.sigmoid(cv)
        for l in range(DEPTH):
            o_ref[:, l * nc:(l + 1) * nc] = jnp.dot(ca, w_ref[l], precision=HIGHEST, preferred_element_type=F32)

    return pl.pallas_call(body, name="mod_cols", out_shape=jax.ShapeDtypeStruct((NDEV, DEPTH * nc), F32),
                          compiler_params=_cparams())(c_all, ada_w)


def _pre_proj(x, gpre, scale1, shift, w_in_l, rider=None):
    S = x.shape[0]
    TM = min(512, S)
    NB = w_in_l.shape[2]

    def body(x_ref, g_ref, sc_ref, sh_ref, w_ref, h_ref, p_ref):
        xv = x_ref[...]
        rstd = lax.rsqrt(jnp.mean(xv * xv, axis=-1, keepdims=True) + NORM_EPS)
        h = ((xv * rstd * g_ref[...]) * sc_ref[...] + sh_ref[...]).astype(BF16)
        h_ref[...] = h
        for j in range(NDEV):
            p_ref[:, j * NB:(j + 1) * NB] = jnp.dot(h, w_ref[j], preferred_element_type=F32)

    row = pl.BlockSpec((TM, D), lambda i: (i, 0))
    return _pcall(
        body, name="pre_proj", grid=(S // TM,),
        in_specs=[row, _vec(), _vec(), _vec(), _full((NDEV, D, NB))],
        out_specs=[row, pl.BlockSpec((TM, NDEV * NB), lambda i: (i, 0))],
        out_shape=[jax.ShapeDtypeStruct((S, D), BF16), jax.ShapeDtypeStruct((S, NDEV * NB), F32)],
        args=(x, gpre, scale1, shift, w_in_l), rider=rider)


def _conv(E, cw_ref, cb_ref):
    w = cw_ref[...]
    taps = [pltpu.roll(E, CONV_K - 1 - k, axis=0)[CONV_HALO:, :] for k in range(CONV_K - 1)]
    taps.append(E[CONV_HALO:, :])
    acc = cb_ref[...] + taps[0] * w[0:1, :]
    for k in range(1, CONV_K):
        acc = acc + taps[k] * w[k:k + 1, :]
    return acc, taps


def _gates(u, wa_ref, ba_ref, wx_ref, bx_ref, lam_ref):
    ub = u.astype(BF16)
    zr = jnp.concatenate([jnp.dot(ub[:, h * HD:(h + 1) * HD], wa_ref[h], preferred_element_type=F32)
                          for h in range(NHEAD)], axis=1)
    zi = jnp.concatenate([jnp.dot(ub[:, h * HD:(h + 1) * HD], wx_ref[h], preferred_element_type=F32)
                          for h in range(NHEAD)], axis=1)
    r = jax.nn.sigmoid(zr + ba_ref[...])
    ig = jax.nn.sigmoid(zi + bx_ref[...])
    sp = jax.nn.softplus(-lam_ref[...])
    log_a = (-LRU_C * r) * sp
    a = jnp.exp(log_a)
    mult = jnp.sqrt(_one_minus_sq(a, log_a))
    return a, mult, r, ig, sp


def _scan(a, v, sa, sv, carry_ref, out_ref, reverse):
    T = a.shape[0]
    n8 = T // SUB
    A = a.reshape(n8, SUB, D)
    V = v.reshape(n8, SUB, D)
    row = lax.broadcasted_iota(jnp.int32, (n8, SUB, D), 1)
    for d in (1, 2, 4):
        sh = SUB - d if reverse else d
        keep = (row < SUB - d) if reverse else (row >= d)
        Ar = pltpu.roll(A, sh, axis=1)
        Vr = pltpu.roll(V, sh, axis=1)
        V = V + A * jnp.where(keep, Vr, 0.0)
        A = A * jnp.where(keep, Ar, 1.0)
    sa[...] = A.reshape(T, D)
    sv[...] = V.reshape(T, D)
    edge = 0 if reverse else SUB - 1

    def step(k, c):
        r0 = pl.multiple_of((n8 - 1 - k if reverse else k) * SUB, SUB)
        h = sv[pl.ds(r0, SUB), :] + sa[pl.ds(r0, SUB), :] * c
        out_ref[pl.ds(r0, SUB), :] = h
        return jnp.broadcast_to(h[edge:edge + 1, :], (SUB, D))

    carry_ref[...] = lax.fori_loop(0, n8, step, carry_ref[...])


def _rnn_fwd(proj, conv_w, conv_b, wa, ba, wx, bx, lam, rider=None):
    S = proj.shape[0]
    TB = min(256, S)

    def body(xr_ref, g_ref, cw_ref, cb_ref, wa_ref, ba_ref, wx_ref, bx_ref, lam_ref, hs_ref, y_ref,
             xbuf, sa, sv, hc):
        @pl.when(pl.program_id(0) == 0)
        def _():
            xbuf[0:CONV_HALO, :] = jnp.zeros((CONV_HALO, D), F32)
            hc[...] = jnp.zeros((SUB, D), F32)
        xbuf[CONV_HALO:, :] = xr_ref[...]
        u, _ = _conv(xbuf[...], cw_ref, cb_ref)
        xbuf[0:CONV_HALO, :] = xbuf[TB:TB + CONV_HALO, :]
        a, mult, _, ig, _ = _gates(u, wa_ref, ba_ref, wx_ref, bx_ref, lam_ref)
        _scan(a, mult * (ig * u), sa, sv, hc, hs_ref, reverse=False)
        silu, _ = _silu_parts(g_ref[...])
        y_ref[...] = (hs_ref[...] * silu).astype(BF16)

    return _pcall(
        body, name="rnn_fwd", grid=(S // TB,),
        in_specs=[pl.BlockSpec((TB, D), lambda i: (i, 0)), pl.BlockSpec((TB, D), lambda i: (i, 1)),
                  _full((CONV_K, D)), _vec(), _full((NHEAD, HD, HD)), _vec(), _full((NHEAD, HD, HD)), _vec(),
                  _vec()],
        out_specs=[pl.BlockSpec((TB, D), lambda i: (i, 0)), pl.BlockSpec((TB, D), lambda i: (i, 0))],
        out_shape=[jax.ShapeDtypeStruct((S, D), F32), jax.ShapeDtypeStruct((S, D), BF16)],
        scratch_shapes=[pltpu.VMEM((TB + CONV_HALO, D), F32), pltpu.VMEM((TB, D), F32),
                        pltpu.VMEM((TB, D), F32), pltpu.VMEM((SUB, D), F32)],
        args=(proj, proj, conv_w, conv_b, wa, ba, wx, bx, lam), rider=rider)


def _pooled(ebuf, t0, TB):
    tt = t0 + lax.broadcasted_iota(jnp.int32, (TB, 1), 0)
    pooled, inv = [], []
    for g, win in enumerate(WINS):
        Eg = ebuf[:, g * GD:(g + 1) * GD]
        L = Eg
        for lev in range(g + 1):
            L = L + pltpu.roll(L, 1 << lev, axis=0)
        icnt = 1.0 / jnp.minimum(tt + 1, win).astype(F32)
        pooled.append(L[POOL_HALO:, :] * icnt - Eg[POOL_HALO:, :])
        inv.append(icnt)
    return pooled, inv


def _pool_fwd(proj, pw, pb, ps, rider=None):
    S = proj.shape[0]
    TB = min(256, S)

    def body(xp_ref, g_ref, pw_ref, pb_ref, ps_ref, y_ref, ebuf):
        i = pl.program_id(0)

        @pl.when(i == 0)
        def _():
            ebuf[0:POOL_HALO, :] = jnp.zeros((POOL_HALO, D), F32)
        ebuf[POOL_HALO:, :] = xp_ref[...]
        pooled, _ = _pooled(ebuf, i * TB, TB)
        ebuf[0:POOL_HALO, :] = ebuf[TB:TB + POOL_HALO, :]
        yp = jnp.concatenate([jnp.dot(pooled[g].astype(BF16), pw_ref[g], preferred_element_type=F32)
                              for g in range(NGRP)], axis=1) + pb_ref[...]
        silu, _ = _silu_parts(g_ref[...])
        y_ref[...] = (yp * ps_ref[...] * silu).astype(BF16)

    return _pcall(
        body, name="pool_fwd", grid=(S // TB,),
        in_specs=[pl.BlockSpec((TB, D), lambda i: (i, 2)), pl.BlockSpec((TB, D), lambda i: (i, 3)),
                  _full((NGRP, GD, GD)), _vec(), _vec()],
        out_specs=[pl.BlockSpec((TB, D), lambda i: (i, 0))],
        out_shape=[jax.ShapeDtypeStruct((S, D), BF16)],
        scratch_shapes=[pltpu.VMEM((TB + POOL_HALO, D), F32)],
        args=(proj, proj, pw, pb, ps), rider=rider)


def _out_post(yr, yp, w_out_l, x, gate, gpost, target=None, rider=None):
    S = x.shape[0]
    TM = min(512, S)
    KB = w_out_l.shape[1]
    last = target is not None

    def body(*refs):
        if last:
            yr_ref, yp_ref, w_ref, x_ref, gate_ref, gp_ref, t_ref, y_ref, xo_ref, loss_ref = refs
        else:
            yr_ref, yp_ref, w_ref, x_ref, gate_ref, gp_ref, y_ref, xo_ref = refs
        acc = jnp.zeros((TM, D), F32)
        for j in range(NDEV):
            src = yr_ref if j < NDEV // 2 else yp_ref
            k0 = (j % (NDEV // 2)) * KB
            acc = acc + jnp.dot(src[:, k0:k0 + KB], w_ref[j], preferred_element_type=F32)
        y_ref[...] = acc
        rstd = lax.rsqrt(jnp.mean(acc * acc, axis=-1, keepdims=True) + NORM_EPS)
        xn = x_ref[...] + gate_ref[...] * (acc * rstd * gp_ref[...])
        if last:
            err = xn - t_ref[...]
            xo_ref[...] = err * (1.0 / D)

            @pl.when(pl.program_id(0) == 0)
            def _():
                loss_ref[...] = jnp.zeros((SUB, D), F32)
            loss_ref[...] += _rowsum8(err * err)
        else:
            xo_ref[...] = xn

    row = pl.BlockSpec((TM, D), lambda i: (i, 0))
    in_specs = [row, row, _full((NDEV, KB, D)), row, _vec(), _vec()]
    out_specs = [row, row]
    out_shape = [jax.ShapeDtypeStruct((S, D), F32), jax.ShapeDtypeStruct((S, D), F32)]
    args = [yr, yp, w_out_l, x, gate, gpost]
    if last:
        in_specs.append(row)
        out_specs.append(_full((SUB, D)))
        out_shape.append(jax.ShapeDtypeStruct((SUB, D), F32))
        args.append(target)
    return _pcall(body, name="out_post_loss" if last else "out_post", grid=(S // TM,), in_specs=in_specs,
                  out_specs=out_specs, out_shape=out_shape, args=args, rider=rider)


def _out_bwd(dxo, y, yr, yp, w_out_l, gate, gpost, rider=None):
    S = y.shape[0]
    TM = min(256, S)
    KB = w_out_l.shape[1]
    nsteps = S // TM

    def body(dxo_ref, y_ref, yr_ref, yp_ref, w_ref, gate_ref, gp_ref, dyr_ref, dyp_ref, gw_ref, dgate_ref,
             dgp_ref, gw_acc, vacc):
        i = pl.program_id(0)

        @pl.when(i == 0)
        def _():
            gw_acc[...] = jnp.zeros_like(gw_acc)
            vacc[...] = jnp.zeros_like(vacc)
        yv = y_ref[...]
        dxo_v = dxo_ref[...]
        rstd = lax.rsqrt(jnp.mean(yv * yv, axis=-1, keepdims=True) + NORM_EPS)
        n = yv * rstd
        gp = gp_ref[...]
        vacc[0] += _rowsum8(dxo_v * (n * gp))
        drn = dxo_v * gate_ref[...]
        vacc[1] += _rowsum8(drn * n)
        dn = drn * gp
        dy = (rstd * (dn - n * jnp.mean(dn * n, axis=-1, keepdims=True))).astype(BF16)
        for j in range(NDEV):
            dst = dyr_ref if j < NDEV // 2 else dyp_ref
            k0 = (j % (NDEV // 2)) * KB
            dst[:, k0:k0 + KB] = lax.dot_general(dy, w_ref[j], (((1,), (1,)), ((), ())),
                                                 preferred_element_type=F32)
        gw_acc[0:D, :] += lax.dot_general(yr_ref[...], dy, (((0,), (0,)), ((), ())), preferred_element_type=F32)
        gw_acc[D:2 * D, :] += lax.dot_general(yp_ref[...], dy, (((0,), (0,)), ((), ())),
                                              preferred_element_type=F32)

        @pl.when(i == nsteps - 1)
        def _():
            gw_ref[...] = gw_acc[...].astype(BF16)
            dgate_ref[...] = _sum8(vacc[0])
            dgp_ref[...] = _sum8(vacc[1])

    row = pl.BlockSpec((TM, D), lambda i: (i, 0))
    return _pcall(
        body, name="out_bwd", grid=(nsteps,),
        in_specs=[row, row, row, row, _full((NDEV, KB, D)), _vec(), _vec()],
        out_specs=[row, row, _full((2 * D, D)), _vec(), _vec()],
        out_shape=[jax.ShapeDtypeStruct((S, D), F32), jax.ShapeDtypeStruct((S, D), F32),
                   jax.ShapeDtypeStruct((2 * D, D), BF16), jax.ShapeDtypeStruct((1, D), F32),
                   jax.ShapeDtypeStruct((1, D), F32)],
        scratch_shapes=[pltpu.VMEM((2 * D, D), F32), pltpu.VMEM((2, SUB, D), F32)],
        args=(dxo, y, yr, yp, w_out_l, gate, gpost), rider=rider)


def _rnn_bwd(dyr, hs, proj, conv_w, conv_b, wa, ba, wx, bx, lam, rider=None):
    S = proj.shape[0]
    TB = min(256, S)
    nb = S // TB
    TE = TB + CONV_HALO
    A_BA, A_BX, A_LAM, A_CB, A_CW = 0, 1, 2, 3, 4

    def blk(i):
        return nb - 1 - i

    def prev8(i):
        return jnp.maximum(blk(i) * (TB // SUB) - 1, 0)

    def body(dyr_ref, hs_ref, hprev_ref, xr_ref, xprev_ref, g_ref, cw_ref, cb_ref, wa_ref, ba_ref, wx_ref,
             bx_ref, lam_ref, dxr_ref, dg_ref, gcw_ref, gcb_ref, gwa_ref, gba_ref, gwx_ref, gbx_ref, glam_ref,
             xbuf, hbuf, abuf, dbuf, sa, sv, dh_ref, dhc, vacc):
        i = pl.program_id(0)
        first = blk(i) == 0

        @pl.when(i == 0)
        def _():
            abuf[TB:, :] = jnp.zeros((CONV_HALO, D), F32)
            dbuf[TB:, :] = jnp.zeros((CONV_HALO, D), F32)
            dhc[...] = jnp.zeros_like(dhc)
            vacc[...] = jnp.zeros_like(vacc)
            gwa_ref[...] = jnp.zeros_like(gwa_ref)
            gwx_ref[...] = jnp.zeros_like(gwx_ref)

        xbuf[0:CONV_HALO, :] = jnp.where(first, 0.0, xprev_ref[...])
        xbuf[CONV_HALO:, :] = xr_ref[...]
        u, taps = _conv(xbuf[...], cw_ref, cb_ref)
        a, mult, r, ig, sp = _gates(u, wa_ref, ba_ref, wx_ref, bx_ref, lam_ref)
        hbuf[0:CONV_HALO, :] = jnp.where(first, 0.0, hprev_ref[...])
        hbuf[CONV_HALO:, :] = hs_ref[...]
        hprev = pltpu.roll(hbuf[...], 1, axis=0)[CONV_HALO:, :]

        silu, dsilu = _silu_parts(g_ref[...])
        dyv = dyr_ref[...]
        dg_ref[...] = (dyv * hs_ref[...] * dsilu).astype(BF16)

        abuf[0:TB, :] = a
        b = pltpu.roll(abuf[...], TE - 1, axis=0)[0:TB, :]
        _scan(b, dyv * silu, sa, sv, dhc, dh_ref, reverse=True)
        abuf[TB:, :] = a[0:CONV_HALO, :]
        dh = dh_ref[...]

        dlog_a = dh * hprev * a - (dh * ig * u) * (a * a) / mult
        vacc[A_LAM] += _rowsum8(dlog_a * r)
        dzr = dlog_a * (-LRU_C * sp) * r * (1.0 - r)
        dzi = (dh * mult * u) * ig * (1.0 - ig)
        vacc[A_BA] += _rowsum8(dzr)
        vacc[A_BX] += _rowsum8(dzi)
        ub, dzrb, dzib = u.astype(BF16), dzr.astype(BF16), dzi.astype(BF16)
        du_g = []
        for h in range(NHEAD):
            cs = slice(h * HD, (h + 1) * HD)
            gwa_ref[h] += lax.dot_general(ub[:, cs], dzrb[:, cs], (((0,), (0,)), ((), ())),
                                          preferred_element_type=F32)
            gwx_ref[h] += lax.dot_general(ub[:, cs], dzib[:, cs], (((0,), (0,)), ((), ())),
                                          preferred_element_type=F32)
            du_g.append(lax.dot_general(dzrb[:, cs], wa_ref[h], (((1,), (1,)), ((), ())),
                                        preferred_element_type=F32)
                        + lax.dot_general(dzib[:, cs], wx_ref[h], (((1,), (1,)), ((), ())),
                                          preferred_element_type=F32))
        du = dh * mult * ig + jnp.concatenate(du_g, axis=1)

        dbuf[0:TB, :] = du
        Dd = dbuf[...]
        w = cw_ref[...]
        dx = Dd * w[3:4, :]
        for k in range(CONV_K - 1):
            dx = dx + pltpu.roll(Dd, TE - (CONV_K - 1 - k), axis=0) * w[k:k + 1, :]
        dxr_ref[...] = dx[0:TB, :].astype(BF16)
        dbuf[TB:, :] = du[0:CONV_HALO, :]
        vacc[A_CB] += _rowsum8(du)
        for k in range(CONV_K):
            vacc[A_CW + k] += _rowsum8(du * taps[k])

        @pl.when(i == nb - 1)
        def _():
            gba_ref[...] = _sum8(vacc[A_BA])
            gbx_ref[...] = _sum8(vacc[A_BX])
            glam_ref[...] = _sum8(vacc[A_LAM]) * (LRU_C * jax.nn.sigmoid(-lam_ref[...]))
            gcb_ref[...] = _sum8(vacc[A_CB])
            for k in range(CONV_K):
                gcw_ref[k:k + 1, :] = _sum8(vacc[A_CW + k])

    rowb = pl.BlockSpec((TB, D), lambda i: (blk(i), 0))
    halo = pl.BlockSpec((SUB, D), lambda i: (prev8(i), 0))
    wspec = _full((NHEAD, HD, HD))
    return _pcall(
        body, name="rnn_bwd", grid=(nb,),
        in_specs=[rowb, rowb, halo, rowb, halo, pl.BlockSpec((TB, D), lambda i: (blk(i), 1)),
                  _full((CONV_K, D)), _vec(), wspec, _vec(), wspec, _vec(), _vec()],
        out_specs=[rowb, rowb, _full((CONV_K, D)), _vec(), wspec, _vec(), wspec, _vec(), _vec()],
        out_shape=[jax.ShapeDtypeStruct((S, D), BF16), jax.ShapeDtypeStruct((S, D), BF16),
                   jax.ShapeDtypeStruct((CONV_K, D), F32), jax.ShapeDtypeStruct((1, D), F32),
                   jax.ShapeDtypeStruct((NHEAD, HD, HD), F32), jax.ShapeDtypeStruct((1, D), F32),
                   jax.ShapeDtypeStruct((NHEAD, HD, HD), F32), jax.ShapeDtypeStruct((1, D), F32),
                   jax.ShapeDtypeStruct((1, D), F32)],
        scratch_shapes=[pltpu.VMEM((TE, D), F32), pltpu.VMEM((TE, D), F32), pltpu.VMEM((TE, D), F32),
                        pltpu.VMEM((TE, D), F32), pltpu.VMEM((TB, D), F32), pltpu.VMEM((TB, D), F32),
                        pltpu.VMEM((TB, D), F32), pltpu.VMEM((SUB, D), F32),
                        pltpu.VMEM((A_CW + CONV_K, SUB, D), F32)],
        args=(dyr, hs, hs, proj, proj, proj, conv_w, conv_b, wa, ba, wx, bx, lam), rider=rider)


def _pool_bwd(dyp, proj, pw, pb, ps, rider=None):
    S = proj.shape[0]
    TB = min(256, S)
    nb = S // TB
    TE = TB + POOL_HALO

    def blk(i):
        return nb - 1 - i

    def body(dy_ref, xp_ref, xprev_ref, g_ref, pw_ref, pb_ref, ps_ref, dxp_ref, dg_ref, gpw_ref, gpb_ref,
             gps_ref, ebuf, qbuf, vacc):
        i = pl.program_id(0)
        first = blk(i) == 0

        @pl.when(i == 0)
        def _():
            qbuf[TB:, :] = jnp.zeros((POOL_HALO, D), F32)
            vacc[...] = jnp.zeros_like(vacc)
            gpw_ref[...] = jnp.zeros_like(gpw_ref)

        ebuf[0:POOL_HALO, :] = jnp.where(first, 0.0, xprev_ref[...])
        ebuf[POOL_HALO:, :] = xp_ref[...]
        pooled, inv = _pooled(ebuf, blk(i) * TB, TB)
        pooled = [p.astype(BF16) for p in pooled]
        yp = jnp.concatenate([jnp.dot(pooled[g], pw_ref[g], preferred_element_type=F32)
                              for g in range(NGRP)], axis=1) + pb_ref[...]
        silu, dsilu = _silu_parts(g_ref[...])
        dy = dy_ref[...]
        ps = ps_ref[...]
        dyp_v = dy * ps * silu
        vacc[0] += _rowsum8(dy * yp * silu)
        vacc[1] += _rowsum8(dyp_v)
        dg_ref[...] = (dy * yp * ps * dsilu).astype(BF16)
        dypb = dyp_v.astype(BF16)
        for g in range(NGRP):
            cs = slice(g * GD, (g + 1) * GD)
            gpw_ref[g] += lax.dot_general(pooled[g], dypb[:, cs], (((0,), (0,)), ((), ())),
                                          preferred_element_type=F32)
            dpool = lax.dot_general(dypb[:, cs], pw_ref[g], (((1,), (1,)), ((), ())),
                                    preferred_element_type=F32)
            qbuf[0:TB, cs] = dpool * inv[g]
            L = qbuf[:, cs]
            for lev in range(g + 1):
                L = L + pltpu.roll(L, TE - (1 << lev), axis=0)
            dxp_ref[:, cs] = (L[0:TB, :] - dpool).astype(BF16)
        qbuf[TB:, :] = qbuf[0:POOL_HALO, :]

        @pl.when(i == nb - 1)
        def _():
            gps_ref[...] = _sum8(vacc[0])
            gpb_ref[...] = _sum8(vacc[1])

    rowb = pl.BlockSpec((TB, D), lambda i: (blk(i), 0))
    return _pcall(
        body, name="pool_bwd", grid=(nb,),
        in_specs=[rowb, pl.BlockSpec((TB, D), lambda i: (blk(i), 2)),
                  pl.BlockSpec((POOL_HALO, D), lambda i: (jnp.maximum(blk(i) * (TB // POOL_HALO) - 1, 0), 2)),
                  pl.BlockSpec((TB, D), lambda i: (blk(i), 3)), _full((NGRP, GD, GD)), _vec(), _vec()],
        out_specs=[rowb, rowb, _full((NGRP, GD, GD)), _vec(), _vec()],
        out_shape=[jax.ShapeDtypeStruct((S, D), BF16), jax.ShapeDtypeStruct((S, D), BF16),
                   jax.ShapeDtypeStruct((NGRP, GD, GD), F32), jax.ShapeDtypeStruct((1, D), F32),
                   jax.ShapeDtypeStruct((1, D), F32)],
        scratch_shapes=[pltpu.VMEM((TE, D), F32), pltpu.VMEM((TE, D), F32), pltpu.VMEM((2, SUB, D), F32)],
        args=(dyp, proj, proj, proj, pw, pb, ps), rider=rider)


def _in_bwd(dq, w_in_l, x, dxo, gpre, scale1, rider=None):
    S = x.shape[0]
    TM = min(256, S)
    NB = w_in_l.shape[2]
    nsteps = S // TM
    per_q = D // NB

    def body(d0, d1, d2, d3, w_ref, x_ref, dxo_ref, g_ref, sc_ref, dx_ref, dsh_ref, dsc_ref, dg_ref, vacc):
        i = pl.program_id(0)

        @pl.when(i == 0)
        def _():
            vacc[...] = jnp.zeros_like(vacc)
        dref = (d0, d1, d2, d3)
        dh = jnp.zeros((TM, D), F32)
        for j in range(NDEV):
            c0 = (j % per_q) * NB
            dh = dh + lax.dot_general(dref[j // per_q][:, c0:c0 + NB], w_ref[j], (((1,), (1,)), ((), ())),
                                      preferred_element_type=F32)
        xv = x_ref[...]
        rstd = lax.rsqrt(jnp.mean(xv * xv, axis=-1, keepdims=True) + NORM_EPS)
        xn = xv * rstd
        g, sc = g_ref[...], sc_ref[...]
        vacc[0] += _rowsum8(dh)
        vacc[1] += _rowsum8(dh * (xn * g))
        vacc[2] += _rowsum8(dh * sc * xn)
        dxn = dh * sc * g
        dx_ref[...] = dxo_ref[...] + rstd * (dxn - xn * jnp.mean(dxn * xn, axis=-1, keepdims=True))

        @pl.when(i == nsteps - 1)
        def _():
            dsh_ref[...] = _sum8(vacc[0])
            dsc_ref[...] = _sum8(vacc[1])
            dg_ref[...] = _sum8(vacc[2])

    row = pl.BlockSpec((TM, D), lambda i: (i, 0))
    return _pcall(
        body, name="in_bwd", grid=(nsteps,),
        in_specs=[row, row, row, row, _full((NDEV, D, NB)), row, row, _vec(), _vec()],
        out_specs=[row, _vec(), _vec(), _vec()],
        out_shape=[jax.ShapeDtypeStruct((S, D), F32)] + [jax.ShapeDtypeStruct((1, D), F32)] * 3,
        scratch_shapes=[pltpu.VMEM((3, SUB, D), F32)],
        args=(*dq, w_in_l, x, dxo, gpre, scale1), rider=rider)


def _grad_w_in(h, dq, NB, part, rider=None):
    S = h.shape[0]
    TK = min(512, S)
    nk = S // TK
    RH = D // 2

    def body(h_ref, d0, d1, d2, d3, o_ref, acc):
        k = pl.program_id(0)

        @pl.when(k == 0)
        def _():
            acc[...] = jnp.zeros_like(acc)
        hv = h_ref[...]
        for q, d_ref in enumerate((d0, d1, d2, d3)):
            acc[:, q * D:(q + 1) * D] += lax.dot_general(hv, d_ref[...], (((0,), (0,)), ((), ())),
                                                         preferred_element_type=F32)

        @pl.when(k == nk - 1)
        def _():
            for j in range(NDEV):
                o_ref[j] = acc[:, j * NB:(j + 1) * NB].astype(BF16)

    row = pl.BlockSpec((TK, D), lambda k: (k, 0))
    return _pcall(
        body, name="grad_w_in", grid=(nk,),
        in_specs=[pl.BlockSpec((TK, RH), lambda k: (k, part)), row, row, row, row],
        out_specs=[_full((NDEV, RH, NB))],
        out_shape=[jax.ShapeDtypeStruct((NDEV, RH, NB), BF16)],
        scratch_shapes=[pltpu.VMEM((RH, NQ * D), F32)],
        args=(h, *dq), rider=rider)


def _adamw_math(g, w, m, v):
    m2 = ADAM_B1 * m + (1.0 - ADAM_B1) * g
    v2 = ADAM_B2 * v + (1.0 - ADAM_B2) * (g * g)
    m_hat = m2 / (1.0 - ADAM_B1 ** ADAM_STEP)
    v_hat = v2 / (1.0 - ADAM_B2 ** ADAM_STEP)
    delta = -ADAM_LR * (m_hat / (jnp.sqrt(v_hat) + ADAM_EPS) + ADAM_WD * w)
    return delta, m2, v2


def _adamw(name, gs, w, m, v, TR, rider=None):
    L = len(gs)
    n, R, C = gs[0].shape

    def body(*refs):
        g_refs = refs[:L]
        w_ref, m_ref, v_ref, go_ref, do_ref, mo_ref, vo_ref = refs[L:]
        lay = pl.program_id(0)
        for li in range(L):
            @pl.when(lay == li)
            def _(li=li):
                g = g_refs[li][0].astype(F32)
                for s in range(1, n):
                    g = g + g_refs[li][s].astype(F32)
                delta, m2, v2 = _adamw_math(g, w_ref[...], m_ref[...], v_ref[...])
                go_ref[...] = g
                do_ref[...] = delta
                mo_ref[...] = m2
                vo_ref[...] = v2

    lrc = pl.BlockSpec((None, TR, C), lambda lay, r: (lay, r, 0))
    g_specs = [pl.BlockSpec((n, TR, C), lambda lay, r, li=li: (0, jnp.where(lay == li, r, 0), 0))
               for li in range(L)]
    return _pcall(
        body, name=name, grid=(L, R // TR),
        in_specs=g_specs + [lrc, lrc, lrc], out_specs=[lrc] * 4,
        out_shape=[jax.ShapeDtypeStruct((L, R, C), F32)] * 4,
        args=(*gs, w, m, v), rider=rider)


def _ada_adamw(c_all_t, dm, w, m, v, rider=None):
    L, _, nc = w.shape

    def body(c_ref, dm_ref, w_ref, m_ref, v_ref, go_ref, do_ref, mo_ref, vo_ref):
        cv = c_ref[...]
        ca = cv * jax.nn.sigmoid(cv)
        dmv = dm_ref[...]
        g = ca[:, 0:1] * dmv[0:1, :]
        for b in range(1, NDEV):
            g = g + ca[:, b:b + 1] * dmv[b:b + 1, :]
        delta, m2, v2 = _adamw_math(g, w_ref[...], m_ref[...], v_ref[...])
        go_ref[...] = g
        do_ref[...] = delta
        mo_ref[...] = m2
        vo_ref[...] = v2

    big = pl.BlockSpec((None, D, nc), lambda lay: (lay, 0, 0))
    return _pcall(
        body, name="ada_adamw", grid=(L,),
        in_specs=[_full((D, NDEV)), pl.BlockSpec((None, NDEV, nc), lambda lay: (lay, 0, 0)), big, big, big],
        out_specs=[big] * 4, out_shape=[jax.ShapeDtypeStruct((L, D, nc), F32)] * 4,
        args=(c_all_t, dm, w, m, v), rider=rider)


def _sum_slots(recv):
    n, R, C = recv.shape

    def body(r_ref, o_ref):
        acc = r_ref[0]
        for s in range(1, n):
            acc = acc + r_ref[s]
        o_ref[...] = acc

    return pl.pallas_call(body, name="sum_slots", out_shape=jax.ShapeDtypeStruct((R, C), F32),
                          compiler_params=_cparams())(recv)


def _pad_rows(a, rows):
    return jnp.pad(a, ((0, rows - a.shape[0]), (0, 0)))


def _pack_sharded_block(pool_w, pool_b, conv_w):
    return jnp.concatenate([pool_w.reshape(-1, PACK_C), _pad_rows(pool_b.reshape(-1, PACK_C), SUB),
                            _pad_rows(conv_w.reshape(-1, PACK_C), SUB)], axis=0)


def _unpack_sharded_block(p):
    n_pw = DEPTH * NGRP * (GD // NDEV)
    pool_w = p[:n_pw].reshape(DEPTH, NGRP, GD // NDEV, GD)
    pool_b = p[n_pw].reshape(DEPTH, NGRP, GD // NDEV)
    conv_w = p[n_pw + SUB:n_pw + SUB + DEPTH * CONV_K * (D // NDEV) // PACK_C].reshape(DEPTH, CONV_K, D // NDEV)
    return pool_w, pool_b, conv_w


def _blocks_of_full(pool_w, pool_b, conv_w):
    pw = pool_w.reshape(DEPTH, NGRP, NDEV, GD // NDEV, GD).transpose(2, 0, 1, 3, 4).reshape(NDEV, -1, PACK_C)
    pb = pool_b.reshape(DEPTH, NGRP, NDEV, GD // NDEV).transpose(2, 0, 1, 3).reshape(NDEV, -1, PACK_C)
    cw = conv_w.reshape(DEPTH, CONV_K, NDEV, D // NDEV).transpose(2, 0, 1, 3).reshape(NDEV, -1, PACK_C)
    pad = lambda a: jnp.pad(a, ((0, 0), (0, SUB - a.shape[1]), (0, 0)))
    return jnp.concatenate([pw, pad(pb), pad(cw)], axis=1)


def _full_of_blocks(p):
    n_pw = DEPTH * NGRP * (GD // NDEV)
    pool_w = p[:, :n_pw].reshape(NDEV, DEPTH, NGRP, GD // NDEV, GD).transpose(1, 2, 0, 3, 4)
    pool_b = p[:, n_pw].reshape(NDEV, DEPTH, NGRP, GD // NDEV).transpose(1, 2, 0, 3)
    n_cw = DEPTH * CONV_K * (D // NDEV) // PACK_C
    conv_w = p[:, n_pw + SUB:n_pw + SUB + n_cw].reshape(NDEV, DEPTH, CONV_K, D // NDEV).transpose(1, 2, 0, 3)
    return (pool_w.reshape(DEPTH, NGRP, GD, GD), pool_b.reshape(DEPTH, NGRP, GD),
            conv_w.reshape(DEPTH, CONV_K, D))


def _pack_replicated(t):
    p = jnp.concatenate([t[k].reshape(-1, PACK_C) for k in REPLICATED], axis=0)
    return _pad_rows(p, NDEV * PACK_ROWS)


def _unpack_replicated(p, like):
    out, r0 = {}, 0
    for k in REPLICATED:
        rows = like[k].size // PACK_C
        out[k] = p[r0:r0 + rows].reshape(like[k].shape)
        r0 += rows
    return out


def kernel(x, c, ada_w, ada_b, pre_norm_g, w_in, conv_w, conv_b, gate_a_w, gate_a_b, gate_x_w, gate_x_b, lru_lambda, pool_w, pool_b, pool_scale, w_out, post_norm_g, loss_target, m_ada_w, m_ada_b, m_pre_norm_g, m_w_in, m_conv_w, m_conv_b, m_gate_a_w, m_gate_a_b, m_gate_x_w, m_gate_x_b, m_lru_lambda, m_pool_w, m_pool_b, m_pool_scale, m_w_out, m_post_norm_g, v_ada_w, v_ada_b, v_pre_norm_g, v_w_in, v_conv_w, v_conv_b, v_gate_a_w, v_gate_a_b, v_gate_x_w, v_gate_x_b, v_lru_lambda, v_pool_w, v_pool_b, v_pool_scale, v_w_out, v_post_norm_g):
    W = dict(ada_w=ada_w, ada_b=ada_b, pre_norm_g=pre_norm_g, w_in=w_in, conv_w=conv_w, conv_b=conv_b,
             gate_a_w=gate_a_w, gate_a_b=gate_a_b, gate_x_w=gate_x_w, gate_x_b=gate_x_b, lru_lambda=lru_lambda,
             pool_w=pool_w, pool_b=pool_b, pool_scale=pool_scale, w_out=w_out, post_norm_g=post_norm_g)
    M = dict(ada_w=m_ada_w, ada_b=m_ada_b, pre_norm_g=m_pre_norm_g, w_in=m_w_in, conv_w=m_conv_w,
             conv_b=m_conv_b, gate_a_w=m_gate_a_w, gate_a_b=m_gate_a_b, gate_x_w=m_gate_x_w,
             gate_x_b=m_gate_x_b, lru_lambda=m_lru_lambda, pool_w=m_pool_w, pool_b=m_pool_b,
             pool_scale=m_pool_scale, w_out=m_w_out, post_norm_g=m_post_norm_g)
    V = dict(ada_w=v_ada_w, ada_b=v_ada_b, pre_norm_g=v_pre_norm_g, w_in=v_w_in, conv_w=v_conv_w,
             conv_b=v_conv_b, gate_a_w=v_gate_a_w, gate_a_b=v_gate_a_b, gate_x_w=v_gate_x_w,
             gate_x_b=v_gate_x_b, lru_lambda=v_lru_lambda, pool_w=v_pool_w, pool_b=v_pool_b,
             pool_scale=v_pool_scale, w_out=v_w_out, post_norm_g=v_post_norm_g)
    S = x.shape[1]
    me = 4 * lax.axis_index("x") + 2 * lax.axis_index("y") + lax.axis_index("c")
    xs = x.reshape(S, D)
    tgt = loss_target.reshape(S, D)
    nc = ada_w.shape[2]
    NB = w_in.shape[2]
    vec = lambda a: a.reshape(1, D)
    w_in_b, w_out_b = w_in.astype(BF16), w_out.astype(BF16)

    c_slots, w_in0 = _exchange("gather_c_w_in0", _AllGather2([jnp.broadcast_to(c, (SUB, D)), w_in_b[0]]))
    c_all = c_slots[:, 0, :]
    (mod_slots,) = _exchange("gather_mod", _Direct(ag=[_mod_cols(c_all, ada_w)]))
    mod = lax.dynamic_index_in_dim(mod_slots, me, axis=1, keepdims=False)
    mod = mod.reshape(NDEV, DEPTH, nc).transpose(1, 0, 2).reshape(DEPTH, 3 * D) + ada_b
    mods = [(vec(mod[l, :D]), vec(1.0 + mod[l, D:2 * D]), vec(mod[l, 2 * D:])) for l in range(DEPTH)]

    w_in_all, w_out_all = [w_in0, None], [None, None]
    saved = []
    xl = xs
    for l in range(DEPTH):
        shift, scale1, gate = mods[l]
        wa, wx = gate_a_w[l].astype(BF16), gate_x_w[l].astype(BF16)
        rider = _AllGather2([_pack_sharded_block(pool_w, pool_b, conv_w), w_out_b[0], w_out_b[1]]) if l == 0 else None
        (h, proj), got = _pre_proj(xl, vec(pre_norm_g[l]), scale1, shift, w_in_all[l], rider=rider)
        if l == 0:
            pool_w_f, pool_b_f, conv_w_f = _full_of_blocks(got[0])
            w_out_all = [got[1], got[2]]
        pw = pool_w_f[l].astype(BF16)
        rider = _AllGather2([w_in_b[1]]) if l == 0 else None
        (hs, yr), got = _rnn_fwd(proj, conv_w_f[l], vec(conv_b[l]), wa, gate_a_b[l].reshape(1, D), wx,
                                 gate_x_b[l].reshape(1, D), vec(lru_lambda[l]), rider=rider)
        if l == 0:
            w_in_all[1] = got[0]
        (yp,), _ = _pool_fwd(proj, pw, pool_b_f[l].reshape(1, D), vec(pool_scale[l]))
        if l == DEPTH - 1:
            (y, x_next, loss_acc), _ = _out_post(yr, yp, w_out_all[l], xl, gate, vec(post_norm_g[l]), tgt)
        else:
            (y, x_next), _ = _out_post(yr, yp, w_out_all[l], xl, gate, vec(post_norm_g[l]))
        saved.append((xl, h, proj, hs, yr, yp, y, wa, wx, pw))
        xl = x_next
    loss = lax.psum((0.5 / D) * jnp.sum(loss_acc), ("x", "y", "c"))

    dxo = xl
    G = {k: [None] * DEPTH for k in WEIGHTS}
    dmod = [None] * DEPTH
    recv_in, recv_out = [None] * DEPTH, [None] * DEPTH
    for l in reversed(range(DEPTH)):
        xin, h, proj, hs, yr, yp, y, wa, wx, pw = saved[l]
        shift, scale1, gate = mods[l]
        (dyr, dyp, gw_out, dgate, G['post_norm_g'][l]), _ = _out_bwd(dxo, y, yr, yp, w_out_all[l], gate,
                                                                     vec(post_norm_g[l]))
        ((dxr, dgr, G['conv_w'][l], G['conv_b'][l], G['gate_a_w'][l], G['gate_a_b'][l], G['gate_x_w'][l],
          G['gate_x_b'][l], G['lru_lambda'][l]), (recv_out[l],)) = _rnn_bwd(
            dyr, hs, proj, conv_w_f[l], vec(conv_b[l]), wa, gate_a_b[l].reshape(1, D), wx,
            gate_x_b[l].reshape(1, D), vec(lru_lambda[l]),
            rider=_Direct(a2a=[gw_out.reshape(NDEV, 2 * D // NDEV, D)]))
        (dxp, dgp, G['pool_w'][l], G['pool_b'][l], G['pool_scale'][l]), _ = _pool_bwd(
            dyp, proj, pw, pool_b_f[l].reshape(1, D), vec(pool_scale[l]))
        dq = (dxr, dgr, dxp, dgp)
        (gw_top,), _ = _grad_w_in(h, dq, NB, 0)
        (gw_bot,), (recv_top,) = _grad_w_in(h, dq, NB, 1, rider=_Direct(a2a=[gw_top]))
        (dxo, dshift, dscale, G['pre_norm_g'][l]), (recv_bot,) = _in_bwd(
            dq, w_in_all[l], xin, dxo, vec(pre_norm_g[l]), scale1, rider=_Direct(a2a=[gw_bot]))
        recv_in[l] = [recv_top, recv_bot]
        dmod[l] = jnp.concatenate([dshift, dscale, dgate], axis=1)
    grad_x = dxo.reshape(x.shape)

    full = dict(conv_w=(CONV_K, D), pool_w=(NGRP, GD, GD), pool_b=(NGRP, GD))
    Gs = {k: jnp.stack([g.reshape(full.get(k, W[k].shape[1:])) for g in G[k]]) for k in WEIGHTS
          if k not in ('ada_w', 'ada_b', 'w_in', 'w_out')}
    Gs['ada_b'] = jnp.concatenate(dmod, axis=0)
    small = jnp.concatenate([_blocks_of_full(Gs['pool_w'], Gs['pool_b'], Gs['conv_w']),
                             _pack_replicated(Gs).reshape(NDEV, PACK_ROWS, PACK_C)], axis=1)

    out = {}
    halves = lambda a: a.reshape(2 * DEPTH, D // 2, NB)
    out['w_in'], (small_recv, dmod_all) = _adamw(
        "adamw_w_in", [r for l in range(DEPTH) for r in recv_in[l]], halves(w_in), halves(M['w_in']),
        halves(V['w_in']), 256, rider=_Direct(a2a=[small], ag=[Gs['ada_b']]))
    small_sum = _sum_slots(small_recv)
    out['w_out'], (rep_all,) = _adamw("adamw_w_out", recv_out, w_out, M['w_out'], V['w_out'], 256,
                                      rider=_AllGather2([small_sum[PACK_ROWS:]]))
    dm = lax.dynamic_slice_in_dim(dmod_all.reshape(NDEV, DEPTH, 3 * D), me * nc, nc, axis=2)
    out['ada_w'], _ = _ada_adamw(c_all.T, dm.transpose(1, 0, 2), ada_w, M['ada_w'], V['ada_w'])
    g_small = jnp.concatenate([small_sum[:PACK_ROWS], rep_all.reshape(NDEV * PACK_ROWS, PACK_C)], axis=0)

    def packs(T):
        return jnp.concatenate([_pack_sharded_block(T['pool_w'], T['pool_b'], T['conv_w']),
                                _pack_replicated(T)], axis=0)[None]
    res_small, _ = _adamw("adamw_small", [g_small[None]], packs(W), packs(M), packs(V), PACK_ROWS)
    for idx in range(4):
        p = res_small[idx][0]
        pw_, pb_, cw_ = _unpack_sharded_block(p[:PACK_ROWS])
        rep = _unpack_replicated(p[PACK_ROWS:], W)
        rep.update(pool_w=pw_, pool_b=pb_, conv_w=cw_)
        for k, a in rep.items():
            out.setdefault(k, [None] * 4)[idx] = a
    for k in ('w_in', 'w_out', 'ada_w'):
        out[k] = [a.reshape(W[k].shape) for a in out[k]]

    return (loss, grad_x, *[out[k][0] for k in WEIGHTS], *[out[k][1] for k in WEIGHTS],
            *[out[k][2] for k in WEIGHTS], *[out[k][3] for k in WEIGHTS])
```

```python
import functools

import jax
import jax.numpy as jnp
from jax import lax
from jax.experimental import pallas as pl
from jax.experimental.pallas import tpu as pltpu

F32, BF16 = jnp.float32, jnp.bfloat16
MESH = pl.DeviceIdType.MESH
HIGHEST = lax.Precision.HIGHEST

NDEV = 8
DEPTH = 2
D = 1024
NHEAD, HD = 8, 128
NGRP, GD = 4, 256
WINS = (2, 4, 8, 16)
CONV_K = 4
CONV_HALO = 8
POOL_HALO = 16
LRU_C = 8.0
NORM_EPS = 1e-6
ADAM_LR, ADAM_B1, ADAM_B2, ADAM_EPS, ADAM_WD, ADAM_STEP = 0.001, 0.9, 0.999, 1e-08, 0.01, 10
VMEM_LIMIT = 56 * 1024 * 1024
NQ = 4
SUB = 8
PACK_C = 256
PACK_ROWS = 272

WEIGHTS = ['ada_w', 'ada_b', 'pre_norm_g', 'w_in', 'conv_w', 'conv_b', 'gate_a_w', 'gate_a_b', 'gate_x_w',
           'gate_x_b', 'lru_lambda', 'pool_w', 'pool_b', 'pool_scale', 'w_out', 'post_norm_g']
REP_EARLY = ['conv_b', 'gate_a_w', 'gate_a_b', 'gate_x_w', 'gate_x_b', 'lru_lambda', 'pool_scale', 'post_norm_g']
REP_LATE = ['ada_b', 'pre_norm_g']
LATE_ROWS = 8


def _cparams(*sem):
    return pltpu.CompilerParams(dimension_semantics=sem, vmem_limit_bytes=VMEM_LIMIT)


def _vec(n=D):
    return pl.BlockSpec((1, n), lambda *_: (0, 0))


def _full(shape):
    nd = len(shape)
    return pl.BlockSpec(shape, lambda *_: (0,) * nd)


def _rowsum8(z):
    return z.reshape(z.shape[0] // SUB, SUB, z.shape[1]).sum(axis=0)


def _sum8(acc):
    return jnp.sum(acc, axis=0, keepdims=True)


def _sigmoid(z):
    return 0.5 * jnp.tanh(0.5 * z) + 0.5


def _silu_parts(g):
    sg = _sigmoid(g)
    return g * sg, sg * (1.0 + g * (1.0 - sg))


def _one_minus_sq(a, log_a):
    z = 2.0 * log_a
    p = 1.0 / 24.0
    for k in (6.0, 2.0, 1.0):
        p = p * z + 1.0 / k
    return jnp.where(z > -0.03, -(p * z), 1.0 - a * a)


def _place():
    x, y, c = lax.axis_index("x"), lax.axis_index("y"), lax.axis_index("c")
    return x, y, c, 4 * x + 2 * y + c


class _Direct:
    def __init__(self, a2a=(), ag=()):
        self.arrays = list(a2a) + list(ag)
        self.n_a, self.n = len(a2a), len(self.arrays)
        self.out_shape = ([jax.ShapeDtypeStruct(a.shape, a.dtype) for a in a2a]
                          + [jax.ShapeDtypeStruct((NDEV,) + a.shape, a.dtype) for a in ag])
        self.scratch = [pltpu.SemaphoreType.DMA((self.n, NDEV - 1)), pltpu.SemaphoreType.DMA((self.n, NDEV - 1)),
                        pltpu.SemaphoreType.DMA((self.n,))]

    def _copies(self, ins, outs, sems):
        send_sems, recv_sems, local_sems = sems
        x, y, c, me = _place()
        local, remote = [], []
        for t in range(self.n):
            src = ins[t].at[me] if t < self.n_a else ins[t]
            local.append(pltpu.make_async_copy(src, outs[t].at[me], local_sems.at[t]))
        for r in range(1, NDEV):
            px = 1 - x if r & 4 else x
            py = 1 - y if r & 2 else y
            pc = 1 - c if r & 1 else c
            for t in range(self.n):
                src = ins[t].at[4 * px + 2 * py + pc] if t < self.n_a else ins[t]
                remote.append(pltpu.make_async_remote_copy(
                    src_ref=src, dst_ref=outs[t].at[me], send_sem=send_sems.at[t, r - 1],
                    recv_sem=recv_sems.at[t, r - 1], device_id=(px, py, pc), device_id_type=MESH))
        return local, remote

    def start(self, ins, outs, sems):
        local, remote = self._copies(ins, outs, sems)
        for cp in local + remote:
            cp.start()

    def finish(self, ins, outs, sems):
        local, remote = self._copies(ins, outs, sems)
        for cp in remote + local:
            cp.wait()


class _AllGather2:
    def __init__(self, arrays):
        self.arrays = list(arrays)
        self.n = len(self.arrays)
        self.out_shape = [jax.ShapeDtypeStruct((NDEV,) + a.shape, a.dtype) for a in self.arrays]
        self.scratch = [pltpu.SemaphoreType.DMA((self.n, NDEV - 1)), pltpu.SemaphoreType.DMA((self.n, NDEV - 1)),
                        pltpu.SemaphoreType.DMA((self.n,))]

    @staticmethod
    def _chips(x, y):
        return [(1 - x, y), (x, 1 - y), (1 - x, 1 - y)]

    def _copy(self, t, k, src, dst, to, sems):
        return pltpu.make_async_remote_copy(src_ref=src, dst_ref=dst, send_sem=sems[0].at[t, k],
                                            recv_sem=sems[1].at[t, k], device_id=to, device_id_type=MESH)

    def start(self, ins, outs, sems):
        x, y, c, me = _place()
        for t in range(self.n):
            pltpu.make_async_copy(ins[t], outs[t].at[me], sems[2].at[t]).start()
            self._copy(t, 0, ins[t], outs[t].at[me], (x, y, 1 - c), sems).start()
            for j, (px, py) in enumerate(self._chips(x, y)):
                self._copy(t, 1 + j, ins[t], outs[t].at[me], (px, py, c), sems).start()

    def finish(self, ins, outs, sems):
        x, y, c, me = _place()
        sib = (x, y, 1 - c)
        for j, (px, py) in enumerate(self._chips(x, y)):
            slot = 4 * px + 2 * py + c
            for t in range(self.n):
                self._copy(t, 1 + j, ins[t], outs[t].at[slot], sib, sems).wait_recv()
                self._copy(t, 4 + j, outs[t].at[slot], outs[t].at[slot], sib, sems).start()
        for t in range(self.n):
            for k in (0, 4, 5, 6):
                self._copy(t, k, ins[t], outs[t].at[me], sib, sems).wait_recv()
        for t in range(self.n):
            for k in range(NDEV - 1):
                self._copy(t, k, ins[t], outs[t].at[me], sib, sems).wait_send()
            pltpu.make_async_copy(ins[t], outs[t].at[me], sems[2].at[t]).wait()


class _Both:
    def __init__(self, *riders):
        self.riders = riders
        self.arrays = [a for r in riders for a in r.arrays]
        self.n = len(self.arrays)
        self.out_shape = [o for r in riders for o in r.out_shape]
        self.scratch = [s for r in riders for s in r.scratch]

    def _parts(self, ins, outs, sems):
        p, q = 0, 0
        for r in self.riders:
            yield r, ins[p:p + r.n], outs[p:p + r.n], sems[q:q + len(r.scratch)]
            p, q = p + r.n, q + len(r.scratch)

    def start(self, ins, outs, sems):
        for r, i, o, s in self._parts(ins, outs, sems):
            r.start(i, o, s)

    def finish(self, ins, outs, sems):
        for r, i, o, s in self._parts(ins, outs, sems):
            r.finish(i, o, s)


def _exchange(name, rider):
    n = rider.n

    def body(*refs):
        rider.start(refs[:n], refs[n:2 * n], refs[2 * n:])
        rider.finish(refs[:n], refs[n:2 * n], refs[2 * n:])

    any_spec = pl.BlockSpec(memory_space=pl.ANY)
    return list(pl.pallas_call(body, name=name, out_shape=rider.out_shape, in_specs=[any_spec] * n,
                               out_specs=[any_spec] * n, scratch_shapes=rider.scratch)(*rider.arrays))


def _pcall(body, *, name, grid, in_specs, out_specs, out_shape, args, scratch_shapes=(), rider=None):
    params = _cparams(*(("arbitrary",) * len(grid)))
    if rider is None:
        res = pl.pallas_call(body, name=name, grid=grid, in_specs=in_specs, out_specs=out_specs,
                             out_shape=out_shape, scratch_shapes=list(scratch_shapes),
                             compiler_params=params)(*args)
        return list(res), []
    n_in, n_out, n_scr, rn = len(in_specs), len(out_specs), len(scratch_shapes), rider.n

    def wrapped(*refs):
        cuts = [n_in, rn, n_out, rn, n_scr]
        parts, p = [], 0
        for n in cuts:
            parts.append(refs[p:p + n])
            p += n
        ins, r_in, outs, r_out, scr = parts
        sems = refs[p:]
        ids = [pl.program_id(a) for a in range(len(grid))]
        first = functools.reduce(jnp.logical_and, [i == 0 for i in ids])
        last = functools.reduce(jnp.logical_and, [i == g - 1 for i, g in zip(ids, grid)])

        @pl.when(first)
        def _():
            rider.start(r_in, r_out, sems)
        body(*ins, *outs, *scr)

        @pl.when(last)
        def _():
            rider.finish(r_in, r_out, sems)

    any_spec = pl.BlockSpec(memory_space=pl.ANY)
    res = pl.pallas_call(
        wrapped, name=name, grid=grid, in_specs=list(in_specs) + [any_spec] * rn,
        out_specs=list(out_specs) + [any_spec] * rn, out_shape=list(out_shape) + rider.out_shape,
        scratch_shapes=list(scratch_shapes) + rider.scratch, compiler_params=params)(*args, *rider.arrays)
    return list(res[:n_out]), list(res[n_out:])


def _mod_cols(c_all, ada_w):
    nc = ada_w.shape[2]

    def body(c_ref, w_ref, o_ref):
        cv = c_ref[...]
        ca = cv * jax.nn.sigmoid(cv)
        for l in range(DEPTH):
            o_ref[:, l * nc:(l + 1) * nc] = jnp.dot(ca, w_ref[l], precision=HIGHEST, preferred_element_type=F32)

    return pl.pallas_call(body, name="mod_cols", out_shape=jax.ShapeDtypeStruct((NDEV, DEPTH * nc), F32),
                          compiler_params=_cparams())(c_all, ada_w)


def _pre_proj(x, gpre, scale1, shift, w_in_l, rider=None):
    S = x.shape[0]
    TM = min(512, S)
    NB = w_in_l.shape[2]

    def body(x_ref, g_ref, sc_ref, sh_ref, w_ref, h_ref, p_ref):
        xv = x_ref[...]
        rstd = lax.rsqrt(jnp.mean(xv * xv, axis=-1, keepdims=True) + NORM_EPS)
        h = ((xv * rstd * g_ref[...]) * sc_ref[...] + sh_ref[...]).astype(BF16)
        h_ref[...] = h
        for j in range(NDEV):
            p_ref[:, j * NB:(j + 1) * NB] = jnp.dot(h, w_ref[j], preferred_element_type=F32)

    row = pl.BlockSpec((TM, D), lambda i: (i, 0))
    return _pcall(
        body, name="pre_proj", grid=(S // TM,),
        in_specs=[row, _vec(), _vec(), _vec(), _full((NDEV, D, NB))],
        out_specs=[row, pl.BlockSpec((TM, NDEV * NB), lambda i: (i, 0))],
        out_shape=[jax.ShapeDtypeStruct((S, D), BF16), jax.ShapeDtypeStruct((S, NDEV * NB), F32)],
        args=(x, gpre, scale1, shift, w_in_l), rider=rider)


def _conv(E, cw_ref, cb_ref):
    w = cw_ref[...]
    taps = [pltpu.roll(E, CONV_K - 1 - k, axis=0)[CONV_HALO:, :] for k in range(CONV_K - 1)]
    taps.append(E[CONV_HALO:, :])
    acc = cb_ref[...] + taps[0] * w[0:1, :]
    for k in range(1, CONV_K):
        acc = acc + taps[k] * w[k:k + 1, :]
    return acc, taps


def _gates(u, wa_ref, ba_ref, wx_ref, bx_ref, lam_ref):
    ub = u.astype(BF16)
    zr = jnp.concatenate([jnp.dot(ub[:, h * HD:(h + 1) * HD], wa_ref[h], preferred_element_type=F32)
                          for h in range(NHEAD)], axis=1)
    zi = jnp.concatenate([jnp.dot(ub[:, h * HD:(h + 1) * HD], wx_ref[h], preferred_element_type=F32)
                          for h in range(NHEAD)], axis=1)
    r = _sigmoid(zr + ba_ref[...])
    ig = _sigmoid(zi + bx_ref[...])
    sp = jax.nn.softplus(-lam_ref[...])
    log_a = (-LRU_C * r) * sp
    a = jnp.exp(log_a)
    mult = jnp.sqrt(_one_minus_sq(a, log_a))
    return a, mult, r, ig, sp


def _scan(a, v, sa, sv, carry_ref, out_ref, reverse):
    T = a.shape[0]
    n8 = T // SUB
    A = a.reshape(n8, SUB, D)
    V = v.reshape(n8, SUB, D)
    row = lax.broadcasted_iota(jnp.int32, (n8, SUB, D), 1)
    for d in (1, 2, 4):
        sh = SUB - d if reverse else d
        keep = (row < SUB - d) if reverse else (row >= d)
        Ar = pltpu.roll(A, sh, axis=1)
        Vr = pltpu.roll(V, sh, axis=1)
        V = V + A * jnp.where(keep, Vr, 0.0)
        A = A * jnp.where(keep, Ar, 1.0)
    sa[...] = A.reshape(T, D)
    sv[...] = V.reshape(T, D)
    edge = 0 if reverse else SUB - 1

    def step(k, c):
        r0 = pl.multiple_of((n8 - 1 - k if reverse else k) * SUB, SUB)
        h = sv[pl.ds(r0, SUB), :] + sa[pl.ds(r0, SUB), :] * c
        out_ref[pl.ds(r0, SUB), :] = h
        return jnp.broadcast_to(h[edge:edge + 1, :], (SUB, D))

    carry_ref[...] = lax.fori_loop(0, n8, step, carry_ref[...])


def _rnn_fwd(proj, conv_w, conv_b, wa, ba, wx, bx, lam, rider=None):
    S = proj.shape[0]
    TB = min(256, S)

    def body(xr_ref, g_ref, cw_ref, cb_ref, wa_ref, ba_ref, wx_ref, bx_ref, lam_ref, hs_ref, y_ref,
             xbuf, sa, sv, hc):
        @pl.when(pl.program_id(0) == 0)
        def _():
            xbuf[0:CONV_HALO, :] = jnp.zeros((CONV_HALO, D), F32)
            hc[...] = jnp.zeros((SUB, D), F32)
        xbuf[CONV_HALO:, :] = xr_ref[...]
        u, _ = _conv(xbuf[...], cw_ref, cb_ref)
        xbuf[0:CONV_HALO, :] = xbuf[TB:TB + CONV_HALO, :]
        a, mult, _, ig, _ = _gates(u, wa_ref, ba_ref, wx_ref, bx_ref, lam_ref)
        _scan(a, mult * (ig * u), sa, sv, hc, hs_ref, reverse=False)
        silu, _ = _silu_parts(g_ref[...])
        y_ref[...] = (hs_ref[...] * silu).astype(BF16)

    return _pcall(
        body, name="rnn_fwd", grid=(S // TB,),
        in_specs=[pl.BlockSpec((TB, D), lambda i: (i, 0)), pl.BlockSpec((TB, D), lambda i: (i, 1)),
                  _full((CONV_K, D)), _vec(), _full((NHEAD, HD, HD)), _vec(), _full((NHEAD, HD, HD)), _vec(),
                  _vec()],
        out_specs=[pl.BlockSpec((TB, D), lambda i: (i, 0)), pl.BlockSpec((TB, D), lambda i: (i, 0))],
        out_shape=[jax.ShapeDtypeStruct((S, D), F32), jax.ShapeDtypeStruct((S, D), BF16)],
        scratch_shapes=[pltpu.VMEM((TB + CONV_HALO, D), F32), pltpu.VMEM((TB, D), F32),
                        pltpu.VMEM((TB, D), F32), pltpu.VMEM((SUB, D), F32)],
        args=(proj, proj, conv_w, conv_b, wa, ba, wx, bx, lam), rider=rider)


def _pooled(ebuf, t0, TB):
    tt = t0 + lax.broadcasted_iota(jnp.int32, (TB, 1), 0)
    pooled, inv = [], []
    for g, win in enumerate(WINS):
        Eg = ebuf[:, g * GD:(g + 1) * GD]
        L = Eg
        for lev in range(g + 1):
            L = L + pltpu.roll(L, 1 << lev, axis=0)
        icnt = 1.0 / jnp.minimum(tt + 1, win).astype(F32)
        pooled.append(L[POOL_HALO:, :] * icnt - Eg[POOL_HALO:, :])
        inv.append(icnt)
    return pooled, inv


def _pool_fwd(proj, pw, pb, ps, rider=None):
    S = proj.shape[0]
    TB = min(256, S)

    def body(xp_ref, g_ref, pw_ref, pb_ref, ps_ref, y_ref, ebuf):
        i = pl.program_id(0)

        @pl.when(i == 0)
        def _():
            ebuf[0:POOL_HALO, :] = jnp.zeros((POOL_HALO, D), F32)
        ebuf[POOL_HALO:, :] = xp_ref[...]
        pooled, _ = _pooled(ebuf, i * TB, TB)
        ebuf[0:POOL_HALO, :] = ebuf[TB:TB + POOL_HALO, :]
        yp = jnp.concatenate([jnp.dot(pooled[g].astype(BF16), pw_ref[g], preferred_element_type=F32)
                              for g in range(NGRP)], axis=1) + pb_ref[...]
        silu, _ = _silu_parts(g_ref[...])
        y_ref[...] = (yp * ps_ref[...] * silu).astype(BF16)

    return _pcall(
        body, name="pool_fwd", grid=(S // TB,),
        in_specs=[pl.BlockSpec((TB, D), lambda i: (i, 2)), pl.BlockSpec((TB, D), lambda i: (i, 3)),
                  _full((NGRP, GD, GD)), _vec(), _vec()],
        out_specs=[pl.BlockSpec((TB, D), lambda i: (i, 0))],
        out_shape=[jax.ShapeDtypeStruct((S, D), BF16)],
        scratch_shapes=[pltpu.VMEM((TB + POOL_HALO, D), F32)],
        args=(proj, proj, pw, pb, ps), rider=rider)


def _out_post(yr, yp, w_out_l, x, gate, gpost, target=None, rider=None):
    S = x.shape[0]
    TM = min(512, S)
    KB = w_out_l.shape[1]
    last = target is not None

    def body(*refs):
        if last:
            yr_ref, yp_ref, w_ref, x_ref, gate_ref, gp_ref, t_ref, y_ref, xo_ref, loss_ref = refs
        else:
            yr_ref, yp_ref, w_ref, x_ref, gate_ref, gp_ref, y_ref, xo_ref = refs
        acc = jnp.zeros((TM, D), F32)
        for j in range(NDEV):
            src = yr_ref if j < NDEV // 2 else yp_ref
            k0 = (j % (NDEV // 2)) * KB
            acc = acc + jnp.dot(src[:, k0:k0 + KB], w_ref[j], preferred_element_type=F32)
        y_ref[...] = acc
        rstd = lax.rsqrt(jnp.mean(acc * acc, axis=-1, keepdims=True) + NORM_EPS)
        xn = x_ref[...] + gate_ref[...] * (acc * rstd * gp_ref[...])
        if last:
            err = xn - t_ref[...]
            xo_ref[...] = err * (1.0 / D)

            @pl.when(pl.program_id(0) == 0)
            def _():
                loss_ref[...] = jnp.zeros((SUB, D), F32)
            loss_ref[...] += _rowsum8(err * err)
        else:
            xo_ref[...] = xn

    row = pl.BlockSpec((TM, D), lambda i: (i, 0))
    in_specs = [row, row, _full((NDEV, KB, D)), row, _vec(), _vec()]
    out_specs = [row, row]
    out_shape = [jax.ShapeDtypeStruct((S, D), F32), jax.ShapeDtypeStruct((S, D), F32)]
    args = [yr, yp, w_out_l, x, gate, gpost]
    if last:
        in_specs.append(row)
        out_specs.append(_full((SUB, D)))
        out_shape.append(jax.ShapeDtypeStruct((SUB, D), F32))
        args.append(target)
    return _pcall(body, name="out_post_loss" if last else "out_post", grid=(S // TM,), in_specs=in_specs,
                  out_specs=out_specs, out_shape=out_shape, args=args, rider=rider)


def _out_bwd(dxo, y, yr, yp, w_out_l, gate, gpost, rider=None):
    S = y.shape[0]
    TM = min(256, S)
    KB = w_out_l.shape[1]
    nsteps = S // TM

    def body(dxo_ref, y_ref, yr_ref, yp_ref, w_ref, gate_ref, gp_ref, dyr_ref, dyp_ref, gw_ref, dgate_ref,
             dgp_ref, gw_acc, vacc):
        i = pl.program_id(0)

        @pl.when(i == 0)
        def _():
            gw_acc[...] = jnp.zeros_like(gw_acc)
            vacc[...] = jnp.zeros_like(vacc)
        yv = y_ref[...]
        dxo_v = dxo_ref[...]
        rstd = lax.rsqrt(jnp.mean(yv * yv, axis=-1, keepdims=True) + NORM_EPS)
        n = yv * rstd
        gp = gp_ref[...]
        vacc[0] += _rowsum8(dxo_v * (n * gp))
        drn = dxo_v * gate_ref[...]
        vacc[1] += _rowsum8(drn * n)
        dn = drn * gp
        dy = (rstd * (dn - n * jnp.mean(dn * n, axis=-1, keepdims=True))).astype(BF16)
        for j in range(NDEV):
            dst = dyr_ref if j < NDEV // 2 else dyp_ref
            k0 = (j % (NDEV // 2)) * KB
            dst[:, k0:k0 + KB] = lax.dot_general(dy, w_ref[j], (((1,), (1,)), ((), ())),
                                                 preferred_element_type=F32)
        gw_acc[0:D, :] += lax.dot_general(yr_ref[...], dy, (((0,), (0,)), ((), ())), preferred_element_type=F32)
        gw_acc[D:2 * D, :] += lax.dot_general(yp_ref[...], dy, (((0,), (0,)), ((), ())),
                                              preferred_element_type=F32)

        @pl.when(i == nsteps - 1)
        def _():
            gw_ref[...] = gw_acc[...].astype(BF16)
            dgate_ref[...] = _sum8(vacc[0])
            dgp_ref[...] = _sum8(vacc[1])

    row = pl.BlockSpec((TM, D), lambda i: (i, 0))
    return _pcall(
        body, name="out_bwd", grid=(nsteps,),
        in_specs=[row, row, row, row, _full((NDEV, KB, D)), _vec(), _vec()],
        out_specs=[row, row, _full((2 * D, D)), _vec(), _vec()],
        out_shape=[jax.ShapeDtypeStruct((S, D), F32), jax.ShapeDtypeStruct((S, D), F32),
                   jax.ShapeDtypeStruct((2 * D, D), BF16), jax.ShapeDtypeStruct((1, D), F32),
                   jax.ShapeDtypeStruct((1, D), F32)],
        scratch_shapes=[pltpu.VMEM((2 * D, D), F32), pltpu.VMEM((2, SUB, D), F32)],
        args=(dxo, y, yr, yp, w_out_l, gate, gpost), rider=rider)


def _rnn_bwd(dyr, hs, proj, conv_w, conv_b, wa, ba, wx, bx, lam, rider=None):
    S = proj.shape[0]
    TB = min(256, S)
    nb = S // TB
    TE = TB + CONV_HALO
    A_BA, A_BX, A_LAM, A_CB, A_CW = 0, 1, 2, 3, 4

    def blk(i):
        return nb - 1 - i

    def prev8(i):
        return jnp.maximum(blk(i) * (TB // SUB) - 1, 0)

    def body(dyr_ref, hs_ref, hprev_ref, xr_ref, xprev_ref, g_ref, cw_ref, cb_ref, wa_ref, ba_ref, wx_ref,
             bx_ref, lam_ref, dxr_ref, dg_ref, gcw_ref, gcb_ref, gwa_ref, gba_ref, gwx_ref, gbx_ref, glam_ref,
             xbuf, hbuf, abuf, dbuf, sa, sv, dh_ref, dhc, vacc):
        i = pl.program_id(0)
        first = blk(i) == 0

        @pl.when(i == 0)
        def _():
            abuf[TB:, :] = jnp.zeros((CONV_HALO, D), F32)
            dbuf[TB:, :] = jnp.zeros((CONV_HALO, D), F32)
            dhc[...] = jnp.zeros_like(dhc)
            vacc[...] = jnp.zeros_like(vacc)
            gwa_ref[...] = jnp.zeros_like(gwa_ref)
            gwx_ref[...] = jnp.zeros_like(gwx_ref)

        xbuf[0:CONV_HALO, :] = jnp.where(first, 0.0, xprev_ref[...])
        xbuf[CONV_HALO:, :] = xr_ref[...]
        u, taps = _conv(xbuf[...], cw_ref, cb_ref)
        a, mult, r, ig, sp = _gates(u, wa_ref, ba_ref, wx_ref, bx_ref, lam_ref)
        hbuf[0:CONV_HALO, :] = jnp.where(first, 0.0, hprev_ref[...])
        hbuf[CONV_HALO:, :] = hs_ref[...]
        hprev = pltpu.roll(hbuf[...], 1, axis=0)[CONV_HALO:, :]

        silu, dsilu = _silu_parts(g_ref[...])
        dyv = dyr_ref[...]
        dg_ref[...] = (dyv * hs_ref[...] * dsilu).astype(BF16)

        abuf[0:TB, :] = a
        b = pltpu.roll(abuf[...], TE - 1, axis=0)[0:TB, :]
        _scan(b, dyv * silu, sa, sv, dhc, dh_ref, reverse=True)
        abuf[TB:, :] = a[0:CONV_HALO, :]
        dh = dh_ref[...]

        dlog_a = dh * hprev * a - (dh * ig * u) * (a * a) / mult
        vacc[A_LAM] += _rowsum8(dlog_a * r)
        dzr = dlog_a * (-LRU_C * sp) * r * (1.0 - r)
        dzi = (dh * mult * u) * ig * (1.0 - ig)
        vacc[A_BA] += _rowsum8(dzr)
        vacc[A_BX] += _rowsum8(dzi)
        ub, dzrb, dzib = u.astype(BF16), dzr.astype(BF16), dzi.astype(BF16)
        du_g = []
        for h in range(NHEAD):
            cs = slice(h * HD, (h + 1) * HD)
            gwa_ref[h] += lax.dot_general(ub[:, cs], dzrb[:, cs], (((0,), (0,)), ((), ())),
                                          preferred_element_type=F32)
            gwx_ref[h] += lax.dot_general(ub[:, cs], dzib[:, cs], (((0,), (0,)), ((), ())),
                                          preferred_element_type=F32)
            du_g.append(lax.dot_general(dzrb[:, cs], wa_ref[h], (((1,), (1,)), ((), ())),
                                        preferred_element_type=F32)
                        + lax.dot_general(dzib[:, cs], wx_ref[h], (((1,), (1,)), ((), ())),
                                          preferred_element_type=F32))
        du = dh * mult * ig + jnp.concatenate(du_g, axis=1)

        dbuf[0:TB, :] = du
        Dd = dbuf[...]
        w = cw_ref[...]
        dx = Dd * w[3:4, :]
        for k in range(CONV_K - 1):
            dx = dx + pltpu.roll(Dd, TE - (CONV_K - 1 - k), axis=0) * w[k:k + 1, :]
        dxr_ref[...] = dx[0:TB, :].astype(BF16)
        dbuf[TB:, :] = du[0:CONV_HALO, :]
        vacc[A_CB] += _rowsum8(du)
        for k in range(CONV_K):
            vacc[A_CW + k] += _rowsum8(du * taps[k])

        @pl.when(i == nb - 1)
        def _():
            gba_ref[...] = _sum8(vacc[A_BA])
            gbx_ref[...] = _sum8(vacc[A_BX])
            glam_ref[...] = _sum8(vacc[A_LAM]) * (LRU_C * jax.nn.sigmoid(-lam_ref[...]))
            gcb_ref[...] = _sum8(vacc[A_CB])
            for k in range(CONV_K):
                gcw_ref[k:k + 1, :] = _sum8(vacc[A_CW + k])

    rowb = pl.BlockSpec((TB, D), lambda i: (blk(i), 0))
    halo = pl.BlockSpec((SUB, D), lambda i: (prev8(i), 0))
    wspec = _full((NHEAD, HD, HD))
    return _pcall(
        body, name="rnn_bwd", grid=(nb,),
        in_specs=[rowb, rowb, halo, rowb, halo, pl.BlockSpec((TB, D), lambda i: (blk(i), 1)),
                  _full((CONV_K, D)), _vec(), wspec, _vec(), wspec, _vec(), _vec()],
        out_specs=[rowb, rowb, _full((CONV_K, D)), _vec(), wspec, _vec(), wspec, _vec(), _vec()],
        out_shape=[jax.ShapeDtypeStruct((S, D), BF16), jax.ShapeDtypeStruct((S, D), BF16),
                   jax.ShapeDtypeStruct((CONV_K, D), F32), jax.ShapeDtypeStruct((1, D), F32),
                   jax.ShapeDtypeStruct((NHEAD, HD, HD), F32), jax.ShapeDtypeStruct((1, D), F32),
                   jax.ShapeDtypeStruct((NHEAD, HD, HD), F32), jax.ShapeDtypeStruct((1, D), F32),
                   jax.ShapeDtypeStruct((1, D), F32)],
        scratch_shapes=[pltpu.VMEM((TE, D), F32), pltpu.VMEM((TE, D), F32), pltpu.VMEM((TE, D), F32),
                        pltpu.VMEM((TE, D), F32), pltpu.VMEM((TB, D), F32), pltpu.VMEM((TB, D), F32),
                        pltpu.VMEM((TB, D), F32), pltpu.VMEM((SUB, D), F32),
                        pltpu.VMEM((A_CW + CONV_K, SUB, D), F32)],
        args=(dyr, hs, hs, proj, proj, proj, conv_w, conv_b, wa, ba, wx, bx, lam), rider=rider)


def _pool_bwd(dyp, proj, pw, pb, ps, rider=None):
    S = proj.shape[0]
    TB = min(256, S)
    nb = S // TB
    TE = TB + POOL_HALO

    def blk(i):
        return nb - 1 - i

    def body(dy_ref, xp_ref, xprev_ref, g_ref, pw_ref, pb_ref, ps_ref, dxp_ref, dg_ref, gpw_ref, gpb_ref,
             gps_ref, ebuf, qbuf, vacc):
        i = pl.program_id(0)
        first = blk(i) == 0

        @pl.when(i == 0)
        def _():
            qbuf[TB:, :] = jnp.zeros((POOL_HALO, D), F32)
            vacc[...] = jnp.zeros_like(vacc)
            gpw_ref[...] = jnp.zeros_like(gpw_ref)

        ebuf[0:POOL_HALO, :] = jnp.where(first, 0.0, xprev_ref[...])
        ebuf[POOL_HALO:, :] = xp_ref[...]
        pooled, inv = _pooled(ebuf, blk(i) * TB, TB)
        pooled = [p.astype(BF16) for p in pooled]
        yp = jnp.concatenate([jnp.dot(pooled[g], pw_ref[g], preferred_element_type=F32)
                              for g in range(NGRP)], axis=1) + pb_ref[...]
        silu, dsilu = _silu_parts(g_ref[...])
        dy = dy_ref[...]
        ps = ps_ref[...]
        dyp_v = dy * ps * silu
        vacc[0] += _rowsum8(dy * yp * silu)
        vacc[1] += _rowsum8(dyp_v)
        dg_ref[...] = (dy * yp * ps * dsilu).astype(BF16)
        dypb = dyp_v.astype(BF16)
        for g in range(NGRP):
            cs = slice(g * GD, (g + 1) * GD)
            gpw_ref[g] += lax.dot_general(pooled[g], dypb[:, cs], (((0,), (0,)), ((), ())),
                                          preferred_element_type=F32)
            dpool = lax.dot_general(dypb[:, cs], pw_ref[g], (((1,), (1,)), ((), ())),
                                    preferred_element_type=F32)
            qbuf[0:TB, cs] = dpool * inv[g]
            L = qbuf[:, cs]
            for lev in range(g + 1):
                L = L + pltpu.roll(L, TE - (1 << lev), axis=0)
            dxp_ref[:, cs] = (L[0:TB, :] - dpool).astype(BF16)
        qbuf[TB:, :] = qbuf[0:POOL_HALO, :]

        @pl.when(i == nb - 1)
        def _():
            gps_ref[...] = _sum8(vacc[0])
            gpb_ref[...] = _sum8(vacc[1])

    rowb = pl.BlockSpec((TB, D), lambda i: (blk(i), 0))
    return _pcall(
        body, name="pool_bwd", grid=(nb,),
        in_specs=[rowb, pl.BlockSpec((TB, D), lambda i: (blk(i), 2)),
                  pl.BlockSpec((POOL_HALO, D), lambda i: (jnp.maximum(blk(i) * (TB // POOL_HALO) - 1, 0), 2)),
                  pl.BlockSpec((TB, D), lambda i: (blk(i), 3)), _full((NGRP, GD, GD)), _vec(), _vec()],
        out_specs=[rowb, rowb, _full((NGRP, GD, GD)), _vec(), _vec()],
        out_shape=[jax.ShapeDtypeStruct((S, D), BF16), jax.ShapeDtypeStruct((S, D), BF16),
                   jax.ShapeDtypeStruct((NGRP, GD, GD), F32), jax.ShapeDtypeStruct((1, D), F32),
                   jax.ShapeDtypeStruct((1, D), F32)],
        scratch_shapes=[pltpu.VMEM((TE, D), F32), pltpu.VMEM((TE, D), F32), pltpu.VMEM((2, SUB, D), F32)],
        args=(dyp, proj, proj, proj, pw, pb, ps), rider=rider)


def _in_bwd(dq, w_in_l, x, dxo, gpre, scale1, rider=None):
    S = x.shape[0]
    TM = min(256, S)
    NB = w_in_l.shape[2]
    nsteps = S // TM
    per_q = D // NB

    def body(d0, d1, d2, d3, w_ref, x_ref, dxo_ref, g_ref, sc_ref, dx_ref, dsh_ref, dsc_ref, dg_ref, vacc):
        i = pl.program_id(0)

        @pl.when(i == 0)
        def _():
            vacc[...] = jnp.zeros_like(vacc)
        dref = (d0, d1, d2, d3)
        dh = jnp.zeros((TM, D), F32)
        for j in range(NDEV):
            c0 = (j % per_q) * NB
            dh = dh + lax.dot_general(dref[j // per_q][:, c0:c0 + NB], w_ref[j], (((1,), (1,)), ((), ())),
                                      preferred_element_type=F32)
        xv = x_ref[...]
        rstd = lax.rsqrt(jnp.mean(xv * xv, axis=-1, keepdims=True) + NORM_EPS)
        xn = xv * rstd
        g, sc = g_ref[...], sc_ref[...]
        vacc[0] += _rowsum8(dh)
        vacc[1] += _rowsum8(dh * (xn * g))
        vacc[2] += _rowsum8(dh * sc * xn)
        dxn = dh * sc * g
        dx_ref[...] = dxo_ref[...] + rstd * (dxn - xn * jnp.mean(dxn * xn, axis=-1, keepdims=True))

        @pl.when(i == nsteps - 1)
        def _():
            dsh_ref[...] = _sum8(vacc[0])
            dsc_ref[...] = _sum8(vacc[1])
            dg_ref[...] = _sum8(vacc[2])

    row = pl.BlockSpec((TM, D), lambda i: (i, 0))
    return _pcall(
        body, name="in_bwd", grid=(nsteps,),
        in_specs=[row, row, row, row, _full((NDEV, D, NB)), row, row, _vec(), _vec()],
        out_specs=[row, _vec(), _vec(), _vec()],
        out_shape=[jax.ShapeDtypeStruct((S, D), F32)] + [jax.ShapeDtypeStruct((1, D), F32)] * 3,
        scratch_shapes=[pltpu.VMEM((3, SUB, D), F32)],
        args=(*dq, w_in_l, x, dxo, gpre, scale1), rider=rider)


def _grad_w_in(h, dq, NB, part, rider=None):
    S = h.shape[0]
    TK = min(512, S)
    nk = S // TK
    RH = D // 2

    def body(h_ref, d0, d1, d2, d3, o_ref, acc):
        k = pl.program_id(0)

        @pl.when(k == 0)
        def _():
            acc[...] = jnp.zeros_like(acc)
        hv = h_ref[...]
        for q, d_ref in enumerate((d0, d1, d2, d3)):
            acc[:, q * D:(q + 1) * D] += lax.dot_general(hv, d_ref[...], (((0,), (0,)), ((), ())),
                                                         preferred_element_type=F32)

        @pl.when(k == nk - 1)
        def _():
            for j in range(NDEV):
                o_ref[j] = acc[:, j * NB:(j + 1) * NB].astype(BF16)

    row = pl.BlockSpec((TK, D), lambda k: (k, 0))
    return _pcall(
        body, name="grad_w_in", grid=(nk,),
        in_specs=[pl.BlockSpec((TK, RH), lambda k: (k, part)), row, row, row, row],
        out_specs=[_full((NDEV, RH, NB))],
        out_shape=[jax.ShapeDtypeStruct((NDEV, RH, NB), BF16)],
        scratch_shapes=[pltpu.VMEM((RH, NQ * D), F32)],
        args=(h, *dq), rider=rider)


def _adamw_math(g, w, m, v):
    m2 = ADAM_B1 * m + (1.0 - ADAM_B1) * g
    v2 = ADAM_B2 * v + (1.0 - ADAM_B2) * (g * g)
    m_hat = m2 / (1.0 - ADAM_B1 ** ADAM_STEP)
    v_hat = v2 / (1.0 - ADAM_B2 ** ADAM_STEP)
    delta = -ADAM_LR * (m_hat / (jnp.sqrt(v_hat) + ADAM_EPS) + ADAM_WD * w)
    return delta, m2, v2


def _adamw(name, gs, w, m, v, TR, rider=None):
    L = len(gs)
    n, R, C = gs[0].shape

    def body(*refs):
        g_refs = refs[:L]
        w_ref, m_ref, v_ref, go_ref, do_ref, mo_ref, vo_ref = refs[L:]
        lay = pl.program_id(0)
        for li in range(L):
            @pl.when(lay == li)
            def _(li=li):
                g = g_refs[li][0].astype(F32)
                for s in range(1, n):
                    g = g + g_refs[li][s].astype(F32)
                delta, m2, v2 = _adamw_math(g, w_ref[...], m_ref[...], v_ref[...])
                go_ref[...] = g
                do_ref[...] = delta
                mo_ref[...] = m2
                vo_ref[...] = v2

    lrc = pl.BlockSpec((None, TR, C), lambda lay, r: (lay, r, 0))
    g_specs = [pl.BlockSpec((n, TR, C), lambda lay, r, li=li: (0, jnp.where(lay == li, r, 0), 0))
               for li in range(L)]
    return _pcall(
        body, name=name, grid=(L, R // TR),
        in_specs=g_specs + [lrc, lrc, lrc], out_specs=[lrc] * 4,
        out_shape=[jax.ShapeDtypeStruct((L, R, C), F32)] * 4,
        args=(*gs, w, m, v), rider=rider)


def _ada_adamw(c_all_t, dm, w, m, v, rider=None):
    L, _, nc = w.shape

    def body(c_ref, dm_ref, w_ref, m_ref, v_ref, go_ref, do_ref, mo_ref, vo_ref):
        cv = c_ref[...]
        ca = cv * jax.nn.sigmoid(cv)
        dmv = dm_ref[...]
        g = ca[:, 0:1] * dmv[0:1, :]
        for b in range(1, NDEV):
            g = g + ca[:, b:b + 1] * dmv[b:b + 1, :]
        delta, m2, v2 = _adamw_math(g, w_ref[...], m_ref[...], v_ref[...])
        go_ref[...] = g
        do_ref[...] = delta
        mo_ref[...] = m2
        vo_ref[...] = v2

    big = pl.BlockSpec((None, D, nc), lambda lay: (lay, 0, 0))
    return _pcall(
        body, name="ada_adamw", grid=(L,),
        in_specs=[_full((D, NDEV)), pl.BlockSpec((None, NDEV, nc), lambda lay: (lay, 0, 0)), big, big, big],
        out_specs=[big] * 4, out_shape=[jax.ShapeDtypeStruct((L, D, nc), F32)] * 4,
        args=(c_all_t, dm, w, m, v), rider=rider)


def _sum_slots(recv):
    n, R, C = recv.shape

    def body(r_ref, o_ref):
        acc = r_ref[0].astype(F32)
        for s in range(1, n):
            acc = acc + r_ref[s].astype(F32)
        o_ref[...] = acc

    return pl.pallas_call(body, name="sum_slots", out_shape=jax.ShapeDtypeStruct((R, C), F32),
                          compiler_params=_cparams())(recv)


def _pad_rows(a, rows):
    return jnp.pad(a, ((0, rows - a.shape[0]), (0, 0)))


def _pack_sharded_block(pool_w, pool_b, conv_w):
    return jnp.concatenate([pool_w.reshape(-1, PACK_C), _pad_rows(pool_b.reshape(-1, PACK_C), SUB),
                            _pad_rows(conv_w.reshape(-1, PACK_C), SUB)], axis=0)


def _unpack_sharded_block(p):
    n_pw = DEPTH * NGRP * (GD // NDEV)
    pool_w = p[:n_pw].reshape(DEPTH, NGRP, GD // NDEV, GD)
    pool_b = p[n_pw].reshape(DEPTH, NGRP, GD // NDEV)
    conv_w = p[n_pw + SUB:n_pw + SUB + DEPTH * CONV_K * (D // NDEV) // PACK_C].reshape(DEPTH, CONV_K, D // NDEV)
    return pool_w, pool_b, conv_w


def _blocks_of_full(pool_w, pool_b, conv_w):
    pw = pool_w.reshape(DEPTH, NGRP, NDEV, GD // NDEV, GD).transpose(2, 0, 1, 3, 4).reshape(NDEV, -1, PACK_C)
    pb = pool_b.reshape(DEPTH, NGRP, NDEV, GD // NDEV).transpose(2, 0, 1, 3).reshape(NDEV, -1, PACK_C)
    cw = conv_w.reshape(DEPTH, CONV_K, NDEV, D // NDEV).transpose(2, 0, 1, 3).reshape(NDEV, -1, PACK_C)
    pad = lambda a: jnp.pad(a, ((0, 0), (0, SUB - a.shape[1]), (0, 0)))
    return jnp.concatenate([pw, pad(pb), pad(cw)], axis=1)


def _full_of_blocks(p):
    n_pw = DEPTH * NGRP * (GD // NDEV)
    pool_w = p[:, :n_pw].reshape(NDEV, DEPTH, NGRP, GD // NDEV, GD).transpose(1, 2, 0, 3, 4)
    pool_b = p[:, n_pw].reshape(NDEV, DEPTH, NGRP, GD // NDEV).transpose(1, 2, 0, 3)
    n_cw = DEPTH * CONV_K * (D // NDEV) // PACK_C
    conv_w = p[:, n_pw + SUB:n_pw + SUB + n_cw].reshape(NDEV, DEPTH, CONV_K, D // NDEV).transpose(1, 2, 0, 3)
    return (pool_w.reshape(DEPTH, NGRP, GD, GD), pool_b.reshape(DEPTH, NGRP, GD),
            conv_w.reshape(DEPTH, CONV_K, D))


def _pack_replicated(t, keys, chunk_rows):
    p = jnp.concatenate([t[k].reshape(-1, PACK_C) for k in keys], axis=0)
    return _pad_rows(p, NDEV * chunk_rows)


def _unpack_replicated(p, like, keys):
    out, r0 = {}, 0
    for k in keys:
        rows = like[k].size // PACK_C
        out[k] = p[r0:r0 + rows].reshape(like[k].shape)
        r0 += rows
    return out


def kernel(x, c, ada_w, ada_b, pre_norm_g, w_in, conv_w, conv_b, gate_a_w, gate_a_b, gate_x_w, gate_x_b, lru_lambda, pool_w, pool_b, pool_scale, w_out, post_norm_g, loss_target, m_ada_w, m_ada_b, m_pre_norm_g, m_w_in, m_conv_w, m_conv_b, m_gate_a_w, m_gate_a_b, m_gate_x_w, m_gate_x_b, m_lru_lambda, m_pool_w, m_pool_b, m_pool_scale, m_w_out, m_post_norm_g, v_ada_w, v_ada_b, v_pre_norm_g, v_w_in, v_conv_w, v_conv_b, v_gate_a_w, v_gate_a_b, v_gate_x_w, v_gate_x_b, v_lru_lambda, v_pool_w, v_pool_b, v_pool_scale, v_w_out, v_post_norm_g):
    W = dict(ada_w=ada_w, ada_b=ada_b, pre_norm_g=pre_norm_g, w_in=w_in, conv_w=conv_w, conv_b=conv_b,
             gate_a_w=gate_a_w, gate_a_b=gate_a_b, gate_x_w=gate_x_w, gate_x_b=gate_x_b, lru_lambda=lru_lambda,
             pool_w=pool_w, pool_b=pool_b, pool_scale=pool_scale, w_out=w_out, post_norm_g=post_norm_g)
    M = dict(ada_w=m_ada_w, ada_b=m_ada_b, pre_norm_g=m_pre_norm_g, w_in=m_w_in, conv_w=m_conv_w,
             conv_b=m_conv_b, gate_a_w=m_gate_a_w, gate_a_b=m_gate_a_b, gate_x_w=m_gate_x_w,
             gate_x_b=m_gate_x_b, lru_lambda=m_lru_lambda, pool_w=m_pool_w, pool_b=m_pool_b,
             pool_scale=m_pool_scale, w_out=m_w_out, post_norm_g=m_post_norm_g)
    V = dict(ada_w=v_ada_w, ada_b=v_ada_b, pre_norm_g=v_pre_norm_g, w_in=v_w_in, conv_w=v_conv_w,
             conv_b=v_conv_b, gate_a_w=v_gate_a_w, gate_a_b=v_gate_a_b, gate_x_w=v_gate_x_w,
             gate_x_b=v_gate_x_b, lru_lambda=v_lru_lambda, pool_w=v_pool_w, pool_b=v_pool_b,
             pool_scale=v_pool_scale, w_out=v_w_out, post_norm_g=v_post_norm_g)
    S = x.shape[1]
    me = 4 * lax.axis_index("x") + 2 * lax.axis_index("y") + lax.axis_index("c")
    xs = x.reshape(S, D)
    tgt = loss_target.reshape(S, D)
    nc = ada_w.shape[2]
    NB = w_in.shape[2]
    vec = lambda a: a.reshape(1, D)
    w_in_b, w_out_b = w_in.astype(BF16), w_out.astype(BF16)

    c_slots, w_in0 = _exchange("gather_c_w_in0", _AllGather2([jnp.broadcast_to(c, (SUB, D)), w_in_b[0]]))
    c_all = c_slots[:, 0, :]
    (mod_slots,) = _exchange("gather_mod", _Direct(ag=[_mod_cols(c_all, ada_w)]))
    mod = lax.dynamic_index_in_dim(mod_slots, me, axis=1, keepdims=False)
    mod = mod.reshape(NDEV, DEPTH, nc).transpose(1, 0, 2).reshape(DEPTH, 3 * D) + ada_b
    mods = [(vec(mod[l, :D]), vec(1.0 + mod[l, D:2 * D]), vec(mod[l, 2 * D:])) for l in range(DEPTH)]

    w_in_all, w_out_all = [w_in0, None], [None, None]
    saved = []
    xl = xs
    for l in range(DEPTH):
        shift, scale1, gate = mods[l]
        wa, wx = gate_a_w[l].astype(BF16), gate_x_w[l].astype(BF16)
        rider = _AllGather2([_pack_sharded_block(pool_w, pool_b, conv_w), w_out_b[0]]) if l == 0 else None
        (h, proj), got = _pre_proj(xl, vec(pre_norm_g[l]), scale1, shift, w_in_all[l], rider=rider)
        if l == 0:
            pool_w_f, pool_b_f, conv_w_f = _full_of_blocks(got[0])
            w_out_all[0] = got[1]
        pw = pool_w_f[l].astype(BF16)
        rider = _AllGather2([w_in_b[1]]) if l == 0 else None
        (hs, yr), got = _rnn_fwd(proj, conv_w_f[l], vec(conv_b[l]), wa, gate_a_b[l].reshape(1, D), wx,
                                 gate_x_b[l].reshape(1, D), vec(lru_lambda[l]), rider=rider)
        if l == 0:
            w_in_all[1] = got[0]
        (yp,), _ = _pool_fwd(proj, pw, pool_b_f[l].reshape(1, D), vec(pool_scale[l]))
        if l == DEPTH - 1:
            (y, x_next, loss_acc), _ = _out_post(yr, yp, w_out_all[l], xl, gate, vec(post_norm_g[l]), tgt)
        else:
            (y, x_next), (w_out_all[1],) = _out_post(yr, yp, w_out_all[l], xl, gate, vec(post_norm_g[l]),
                                                     rider=_AllGather2([w_out_b[1]]))
        saved.append((xl, h, proj, hs, yr, yp, y, wa, wx, pw))
        xl = x_next
    loss = lax.psum((0.5 / D) * jnp.sum(loss_acc), ("x", "y", "c"))

    dxo = xl
    G = {k: [None] * DEPTH for k in WEIGHTS}
    dmod = [None] * DEPTH
    recv_in, recv_out = [[None, None] for _ in range(DEPTH)], [None] * DEPTH
    full = dict(conv_w=(CONV_K, D), pool_w=(NGRP, GD, GD), pool_b=(NGRP, GD))
    stack = lambda k: jnp.stack([g.reshape(full.get(k, W[k].shape[1:])) for g in G[k]])
    gw_bot_prev = None
    for l in reversed(range(DEPTH)):
        xin, h, proj, hs, yr, yp, y, wa, wx, pw = saved[l]
        shift, scale1, gate = mods[l]
        rider = None if gw_bot_prev is None else _Direct(a2a=[gw_bot_prev])
        (dyr, dyp, gw_out, dgate, G['post_norm_g'][l]), got = _out_bwd(dxo, y, yr, yp, w_out_all[l], gate,
                                                                       vec(post_norm_g[l]), rider=rider)
        if got:
            recv_in[l + 1][1] = got[0]
        ((dxr, dgr, G['conv_w'][l], G['conv_b'][l], G['gate_a_w'][l], G['gate_a_b'][l], G['gate_x_w'][l],
          G['gate_x_b'][l], G['lru_lambda'][l]), (recv_out[l],)) = _rnn_bwd(
            dyr, hs, proj, conv_w_f[l], vec(conv_b[l]), wa, gate_a_b[l].reshape(1, D), wx,
            gate_x_b[l].reshape(1, D), vec(lru_lambda[l]),
            rider=_Direct(a2a=[gw_out.reshape(NDEV, 2 * D // NDEV, D)]))
        (dxp, dgp, G['pool_w'][l], G['pool_b'][l], G['pool_scale'][l]), _ = _pool_bwd(
            dyp, proj, pw, pool_b_f[l].reshape(1, D), vec(pool_scale[l]))
        dq = (dxr, dgr, dxp, dgp)
        if l > 0:
            (gw_top,), _ = _grad_w_in(h, dq, NB, 0)
            (dxo, dshift, dscale, G['pre_norm_g'][l]), (recv_in[l][0],) = _in_bwd(
                dq, w_in_all[l], xin, dxo, vec(pre_norm_g[l]), scale1, rider=_Direct(a2a=[gw_top]))
            (gw_bot_prev,), _ = _grad_w_in(h, dq, NB, 1)
        else:
            Ge = {k: stack(k) for k in REP_EARLY + ['pool_w', 'pool_b', 'conv_w']}
            early = jnp.concatenate([_blocks_of_full(Ge['pool_w'], Ge['pool_b'], Ge['conv_w']),
                                     _pack_replicated(Ge, REP_EARLY, PACK_ROWS).reshape(NDEV, PACK_ROWS, PACK_C)],
                                    axis=1).astype(BF16)
            (gw_top,), (early_recv,) = _grad_w_in(h, dq, NB, 0, rider=_Direct(a2a=[early]))
            early_sum = _sum_slots(early_recv)
            (dxo, dshift, dscale, G['pre_norm_g'][l]), (recv_in[l][0], early_all) = _in_bwd(
                dq, w_in_all[l], xin, dxo, vec(pre_norm_g[l]), scale1,
                rider=_Both(_Direct(a2a=[gw_top]), _AllGather2([early_sum[PACK_ROWS:]])))
        dmod[l] = jnp.concatenate([dshift, dscale, dgate], axis=1)
    grad_x = dxo.reshape(x.shape)

    Gl = dict(ada_b=jnp.concatenate(dmod, axis=0), pre_norm_g=stack('pre_norm_g'))
    late = _pack_replicated(Gl, REP_LATE, LATE_ROWS).reshape(NDEV, LATE_ROWS, PACK_C)
    (gw_bot,), (late_recv, dmod_all) = _grad_w_in(h, dq, NB, 1, rider=_Direct(a2a=[late], ag=[Gl['ada_b']]))
    late_sum = _sum_slots(late_recv)
    recv_in[0][1], late_all = _exchange("grads_tail", _Both(_Direct(a2a=[gw_bot]), _AllGather2([late_sum])))

    out = {}
    halves = lambda a: a.reshape(2 * DEPTH, D // 2, NB)
    out['w_in'], _ = _adamw("adamw_w_in", [r for l in range(DEPTH) for r in recv_in[l]], halves(w_in),
                            halves(M['w_in']), halves(V['w_in']), 256)
    out['w_out'], _ = _adamw("adamw_w_out", recv_out, w_out, M['w_out'], V['w_out'], 256)
    dm = lax.dynamic_slice_in_dim(dmod_all.reshape(NDEV, DEPTH, 3 * D), me * nc, nc, axis=2)
    out['ada_w'], _ = _ada_adamw(c_all.T, dm.transpose(1, 0, 2), ada_w, M['ada_w'], V['ada_w'])
    g_small = jnp.concatenate([early_sum[:PACK_ROWS], early_all.reshape(NDEV * PACK_ROWS, PACK_C),
                               late_all.reshape(NDEV * LATE_ROWS, PACK_C)], axis=0)

    def packs(T):
        return jnp.concatenate([_pack_sharded_block(T['pool_w'], T['pool_b'], T['conv_w']),
                                _pack_replicated(T, REP_EARLY, PACK_ROWS),
                                _pack_replicated(T, REP_LATE, LATE_ROWS)], axis=0)[None]
    res_small, _ = _adamw("adamw_small", [g_small[None]], packs(W), packs(M), packs(V), g_small.shape[0] // 2)
    n_early = (1 + NDEV) * PACK_ROWS
    for idx in range(4):
        p = res_small[idx][0]
        pw_, pb_, cw_ = _unpack_sharded_block(p[:PACK_ROWS])
        rep = _unpack_replicated(p[PACK_ROWS:n_early], W, REP_EARLY)
        rep.update(_unpack_replicated(p[n_early:], W, REP_LATE))
        rep.update(pool_w=pw_, pool_b=pb_, conv_w=cw_)
        for k, a in rep.items():
            out.setdefault(k, [None] * 4)[idx] = a
    for k in ('w_in', 'w_out', 'ada_w'):
        out[k] = [a.reshape(W[k].shape) for a in out[k]]

    return (loss, grad_x, *[out[k][0] for k in WEIGHTS], *[out[k][1] for k in WEIGHTS],
            *[out[k][2] for k in WEIGHTS], *[out[k][3] for k in WEIGHTS])
```

```python
import functools

import jax
import jax.numpy as jnp
from jax import lax
from jax.experimental import pallas as pl
from jax.experimental.pallas import tpu as pltpu

F32, BF16 = jnp.float32, jnp.bfloat16
MESH = pl.DeviceIdType.MESH
HIGHEST = lax.Precision.HIGHEST

NDEV = 8
DEPTH = 2
D = 1024
NHEAD, HD = 8, 128
NGRP, GD = 4, 256
WINS = (2, 4, 8, 16)
CONV_K = 4
CONV_HALO = 8
POOL_HALO = 16
LRU_C = 8.0
NORM_EPS = 1e-6
ADAM_LR, ADAM_B1, ADAM_B2, ADAM_EPS, ADAM_WD, ADAM_STEP = 0.001, 0.9, 0.999, 1e-08, 0.01, 10
VMEM_LIMIT = 56 * 1024 * 1024
NQ = 4
SUB = 8
LANE = 128
PACK_C = 256
PACK_ROWS = 272

WEIGHTS = ['ada_w', 'ada_b', 'pre_norm_g', 'w_in', 'conv_w', 'conv_b', 'gate_a_w', 'gate_a_b', 'gate_x_w',
           'gate_x_b', 'lru_lambda', 'pool_w', 'pool_b', 'pool_scale', 'w_out', 'post_norm_g']
REP_EARLY = ['conv_b', 'gate_a_w', 'gate_a_b', 'gate_x_w', 'gate_x_b', 'lru_lambda', 'pool_scale', 'post_norm_g']
REP_LATE = ['ada_b', 'pre_norm_g']
LATE_ROWS = 8


def _cparams(*sem):
    return pltpu.CompilerParams(dimension_semantics=sem, vmem_limit_bytes=VMEM_LIMIT)


def _vec(n=D):
    return pl.BlockSpec((1, n), lambda *_: (0, 0))


def _full(shape):
    nd = len(shape)
    return pl.BlockSpec(shape, lambda *_: (0,) * nd)


def _rowsum8(z):
    return z.reshape(z.shape[0] // SUB, SUB, z.shape[1]).sum(axis=0)


def _sum8(acc):
    return jnp.sum(acc, axis=0, keepdims=True)


def _sigmoid(z):
    return 0.5 * jnp.tanh(0.5 * z) + 0.5


def _silu_parts(g):
    sg = _sigmoid(g)
    return g * sg, sg * (1.0 + g * (1.0 - sg))


def _one_minus_sq(a, log_a):
    z = 2.0 * log_a
    p = 1.0 / 24.0
    for k in (6.0, 2.0, 1.0):
        p = p * z + 1.0 / k
    return jnp.where(z > -0.03, -(p * z), 1.0 - a * a)


def _place():
    x, y, c = lax.axis_index("x"), lax.axis_index("y"), lax.axis_index("c")
    return x, y, c, 4 * x + 2 * y + c


class _Direct:
    def __init__(self, a2a=(), ag=()):
        self.arrays = list(a2a) + list(ag)
        self.n_a, self.n = len(a2a), len(self.arrays)
        self.out_shape = ([jax.ShapeDtypeStruct(a.shape, a.dtype) for a in a2a]
                          + [jax.ShapeDtypeStruct((NDEV,) + a.shape, a.dtype) for a in ag])
        self.scratch = [pltpu.SemaphoreType.DMA((self.n, NDEV - 1)), pltpu.SemaphoreType.DMA((self.n, NDEV - 1)),
                        pltpu.SemaphoreType.DMA((self.n,))]

    def _copies(self, ins, outs, sems):
        send_sems, recv_sems, local_sems = sems
        x, y, c, me = _place()
        local, remote = [], []
        for t in range(self.n):
            src = ins[t].at[me] if t < self.n_a else ins[t]
            local.append(pltpu.make_async_copy(src, outs[t].at[me], local_sems.at[t]))
        for r in range(1, NDEV):
            px = 1 - x if r & 4 else x
            py = 1 - y if r & 2 else y
            pc = 1 - c if r & 1 else c
            for t in range(self.n):
                src = ins[t].at[4 * px + 2 * py + pc] if t < self.n_a else ins[t]
                remote.append(pltpu.make_async_remote_copy(
                    src_ref=src, dst_ref=outs[t].at[me], send_sem=send_sems.at[t, r - 1],
                    recv_sem=recv_sems.at[t, r - 1], device_id=(px, py, pc), device_id_type=MESH))
        return local, remote

    def start(self, ins, outs, sems):
        local, remote = self._copies(ins, outs, sems)
        for cp in local + remote:
            cp.start()

    def finish(self, ins, outs, sems):
        local, remote = self._copies(ins, outs, sems)
        for cp in remote + local:
            cp.wait()


class _AllGather2:
    def __init__(self, arrays):
        self.arrays = list(arrays)
        self.n = len(self.arrays)
        self.out_shape = [jax.ShapeDtypeStruct((NDEV,) + a.shape, a.dtype) for a in self.arrays]
        self.scratch = [pltpu.SemaphoreType.DMA((self.n, NDEV - 1)), pltpu.SemaphoreType.DMA((self.n, NDEV - 1)),
                        pltpu.SemaphoreType.DMA((self.n,))]

    @staticmethod
    def _chips(x, y):
        return [(1 - x, y), (x, 1 - y), (1 - x, 1 - y)]

    def _copy(self, t, k, src, dst, to, sems):
        return pltpu.make_async_remote_copy(src_ref=src, dst_ref=dst, send_sem=sems[0].at[t, k],
                                            recv_sem=sems[1].at[t, k], device_id=to, device_id_type=MESH)

    def start(self, ins, outs, sems):
        x, y, c, me = _place()
        for t in range(self.n):
            pltpu.make_async_copy(ins[t], outs[t].at[me], sems[2].at[t]).start()
            self._copy(t, 0, ins[t], outs[t].at[me], (x, y, 1 - c), sems).start()
            for j, (px, py) in enumerate(self._chips(x, y)):
                self._copy(t, 1 + j, ins[t], outs[t].at[me], (px, py, c), sems).start()

    def finish(self, ins, outs, sems):
        x, y, c, me = _place()
        sib = (x, y, 1 - c)
        for j, (px, py) in enumerate(self._chips(x, y)):
            slot = 4 * px + 2 * py + c
            for t in range(self.n):
                self._copy(t, 1 + j, ins[t], outs[t].at[slot], sib, sems).wait_recv()
                self._copy(t, 4 + j, outs[t].at[slot], outs[t].at[slot], sib, sems).start()
        for t in range(self.n):
            for k in (0, 4, 5, 6):
                self._copy(t, k, ins[t], outs[t].at[me], sib, sems).wait_recv()
        for t in range(self.n):
            for k in range(NDEV - 1):
                self._copy(t, k, ins[t], outs[t].at[me], sib, sems).wait_send()
            pltpu.make_async_copy(ins[t], outs[t].at[me], sems[2].at[t]).wait()


class _Both:
    def __init__(self, *riders):
        self.riders = riders
        self.arrays = [a for r in riders for a in r.arrays]
        self.n = len(self.arrays)
        self.out_shape = [o for r in riders for o in r.out_shape]
        self.scratch = [s for r in riders for s in r.scratch]

    def _parts(self, ins, outs, sems):
        p, q = 0, 0
        for r in self.riders:
            yield r, ins[p:p + r.n], outs[p:p + r.n], sems[q:q + len(r.scratch)]
            p, q = p + r.n, q + len(r.scratch)

    def start(self, ins, outs, sems):
        for r, i, o, s in self._parts(ins, outs, sems):
            r.start(i, o, s)

    def finish(self, ins, outs, sems):
        for r, i, o, s in self._parts(ins, outs, sems):
            r.finish(i, o, s)


def _exchange(name, rider):
    n = rider.n

    def body(*refs):
        rider.start(refs[:n], refs[n:2 * n], refs[2 * n:])
        rider.finish(refs[:n], refs[n:2 * n], refs[2 * n:])

    any_spec = pl.BlockSpec(memory_space=pl.ANY)
    return list(pl.pallas_call(body, name=name, out_shape=rider.out_shape, in_specs=[any_spec] * n,
                               out_specs=[any_spec] * n, scratch_shapes=rider.scratch)(*rider.arrays))


def _pcall(body, *, name, grid, in_specs, out_specs, out_shape, args, scratch_shapes=(), rider=None):
    params = _cparams(*(("arbitrary",) * len(grid)))
    if rider is None:
        res = pl.pallas_call(body, name=name, grid=grid, in_specs=in_specs, out_specs=out_specs,
                             out_shape=out_shape, scratch_shapes=list(scratch_shapes),
                             compiler_params=params)(*args)
        return list(res), []
    n_in, n_out, n_scr, rn = len(in_specs), len(out_specs), len(scratch_shapes), rider.n

    def wrapped(*refs):
        cuts = [n_in, rn, n_out, rn, n_scr]
        parts, p = [], 0
        for n in cuts:
            parts.append(refs[p:p + n])
            p += n
        ins, r_in, outs, r_out, scr = parts
        sems = refs[p:]
        ids = [pl.program_id(a) for a in range(len(grid))]
        first = functools.reduce(jnp.logical_and, [i == 0 for i in ids])
        last = functools.reduce(jnp.logical_and, [i == g - 1 for i, g in zip(ids, grid)])

        @pl.when(first)
        def _():
            rider.start(r_in, r_out, sems)
        body(*ins, *outs, *scr)

        @pl.when(last)
        def _():
            rider.finish(r_in, r_out, sems)

    any_spec = pl.BlockSpec(memory_space=pl.ANY)
    res = pl.pallas_call(
        wrapped, name=name, grid=grid, in_specs=list(in_specs) + [any_spec] * rn,
        out_specs=list(out_specs) + [any_spec] * rn, out_shape=list(out_shape) + rider.out_shape,
        scratch_shapes=list(scratch_shapes) + rider.scratch, compiler_params=params)(*args, *rider.arrays)
    return list(res[:n_out]), list(res[n_out:])


def _mod_cols(c_all, ada_w):
    nc = ada_w.shape[2]

    def body(c_ref, w_ref, o_ref):
        cv = c_ref[...]
        ca = cv * jax.nn.sigmoid(cv)
        for l in range(DEPTH):
            o_ref[:, l * nc:(l + 1) * nc] = jnp.dot(ca, w_ref[l], precision=HIGHEST, preferred_element_type=F32)

    return pl.pallas_call(body, name="mod_cols", out_shape=jax.ShapeDtypeStruct((NDEV, DEPTH * nc), F32),
                          compiler_params=_cparams())(c_all, ada_w)


def _pre_proj(x, gpre, scale1, shift, w_in_l, rider=None):
    S = x.shape[0]
    TM = min(512, S)
    NB = w_in_l.shape[2]

    def body(x_ref, g_ref, sc_ref, sh_ref, w_ref, h_ref, p_ref):
        xv = x_ref[...]
        rstd = lax.rsqrt(jnp.mean(xv * xv, axis=-1, keepdims=True) + NORM_EPS)
        h = ((xv * rstd * g_ref[...]) * sc_ref[...] + sh_ref[...]).astype(BF16)
        h_ref[...] = h
        for j in range(NDEV):
            p_ref[:, j * NB:(j + 1) * NB] = jnp.dot(h, w_ref[j], preferred_element_type=F32)

    row = pl.BlockSpec((TM, D), lambda i: (i, 0))
    return _pcall(
        body, name="pre_proj", grid=(S // TM,),
        in_specs=[row, _vec(), _vec(), _vec(), _full((NDEV, D, NB))],
        out_specs=[row, pl.BlockSpec((TM, NDEV * NB), lambda i: (i, 0))],
        out_shape=[jax.ShapeDtypeStruct((S, D), BF16), jax.ShapeDtypeStruct((S, NDEV * NB), F32)],
        args=(x, gpre, scale1, shift, w_in_l), rider=rider)


def _taps(E):
    return [pltpu.roll(E, CONV_K - 1 - k, axis=0)[CONV_HALO:, :] for k in range(CONV_K - 1)] + [E[CONV_HALO:, :]]


def _conv(E, cw_ref, cb_ref):
    w = cw_ref[...]
    taps = _taps(E)
    acc = cb_ref[...] + taps[0] * w[0:1, :]
    for k in range(1, CONV_K):
        acc = acc + taps[k] * w[k:k + 1, :]
    return acc


def _gates(u, wa_ref, ba_ref, wx_ref, bx_ref, lam_ref):
    ub = u.astype(BF16)
    zr = jnp.concatenate([jnp.dot(ub[:, h * HD:(h + 1) * HD], wa_ref[h], preferred_element_type=F32)
                          for h in range(NHEAD)], axis=1)
    zi = jnp.concatenate([jnp.dot(ub[:, h * HD:(h + 1) * HD], wx_ref[h], preferred_element_type=F32)
                          for h in range(NHEAD)], axis=1)
    r = _sigmoid(zr + ba_ref[...])
    ig = _sigmoid(zi + bx_ref[...])
    sp = jax.nn.softplus(-lam_ref[...])
    log_a = (-LRU_C * r) * sp
    a = jnp.exp(log_a)
    mult = jnp.sqrt(_one_minus_sq(a, log_a))
    return a, mult, r, ig, sp


def _scan(a, v, sa, sv, carry_ref, out_ref, reverse):
    T = a.shape[0]
    n8 = T // SUB
    NC = D // LANE
    rows = range(SUB - 2, -1, -1) if reverse else range(1, SUB)
    for cb in range(NC):
        sa[cb] = a[:, cb * LANE:(cb + 1) * LANE]
        sv[cb] = v[:, cb * LANE:(cb + 1) * LANE]
    for cb in range(NC):
        r_in = SUB - 1 if reverse else 0
        Ap = sa[cb, pl.ds(r_in, n8, stride=SUB), :]
        Vp = sv[cb, pl.ds(r_in, n8, stride=SUB), :]
        for r in rows:
            Ar = sa[cb, pl.ds(r, n8, stride=SUB), :]
            Vp = sv[cb, pl.ds(r, n8, stride=SUB), :] + Ar * Vp
            Ap = Ar * Ap
            sa[cb, pl.ds(r, n8, stride=SUB), :] = Ap
            sv[cb, pl.ds(r, n8, stride=SUB), :] = Vp
    edge = 0 if reverse else SUB - 1

    def step(k, c):
        r0 = pl.multiple_of((n8 - 1 - k if reverse else k) * SUB, SUB)
        h = jnp.concatenate([sv[cb, pl.ds(r0, SUB), :] + sa[cb, pl.ds(r0, SUB), :] * c[:, cb * LANE:(cb + 1) * LANE]
                             for cb in range(NC)], axis=1)
        out_ref[pl.ds(r0, SUB), :] = h
        return jnp.broadcast_to(h[edge:edge + 1, :], (SUB, D))

    carry_ref[...] = lax.fori_loop(0, n8, step, carry_ref[...])


def _rnn_fwd(proj, conv_w, conv_b, wa, ba, wx, bx, lam, rider=None):
    S = proj.shape[0]
    TB = min(256, S)

    def body(xr_ref, g_ref, cw_ref, cb_ref, wa_ref, ba_ref, wx_ref, bx_ref, lam_ref, hs_ref, y_ref,
             u_ref, r_ref, i_ref, a_ref, m_ref, xbuf, sa, sv, hc):
        @pl.when(pl.program_id(0) == 0)
        def _():
            xbuf[0:CONV_HALO, :] = jnp.zeros((CONV_HALO, D), F32)
            hc[...] = jnp.zeros((SUB, D), F32)
        xbuf[CONV_HALO:, :] = xr_ref[...]
        u = _conv(xbuf[...], cw_ref, cb_ref)
        xbuf[0:CONV_HALO, :] = xbuf[TB:TB + CONV_HALO, :]
        a, mult, r, ig, _ = _gates(u, wa_ref, ba_ref, wx_ref, bx_ref, lam_ref)
        u_ref[...] = u
        r_ref[...] = r
        i_ref[...] = ig
        a_ref[...] = a
        m_ref[...] = mult
        _scan(a, mult * (ig * u), sa, sv, hc, hs_ref, reverse=False)
        silu, _ = _silu_parts(g_ref[...])
        y_ref[...] = (hs_ref[...] * silu).astype(BF16)

    rowb = pl.BlockSpec((TB, D), lambda i: (i, 0))
    return _pcall(
        body, name="rnn_fwd", grid=(S // TB,),
        in_specs=[rowb, pl.BlockSpec((TB, D), lambda i: (i, 1)),
                  _full((CONV_K, D)), _vec(), _full((NHEAD, HD, HD)), _vec(), _full((NHEAD, HD, HD)), _vec(),
                  _vec()],
        out_specs=[rowb] * 7,
        out_shape=[jax.ShapeDtypeStruct((S, D), F32), jax.ShapeDtypeStruct((S, D), BF16)]
        + [jax.ShapeDtypeStruct((S, D), F32)] * 5,
        scratch_shapes=[pltpu.VMEM((TB + CONV_HALO, D), F32), pltpu.VMEM((D // LANE, TB, LANE), F32),
                        pltpu.VMEM((D // LANE, TB, LANE), F32), pltpu.VMEM((SUB, D), F32)],
        args=(proj, proj, conv_w, conv_b, wa, ba, wx, bx, lam), rider=rider)


def _pooled(ebuf, t0, TB):
    tt = t0 + lax.broadcasted_iota(jnp.int32, (TB, 1), 0)
    pooled, inv = [], []
    for g, win in enumerate(WINS):
        Eg = ebuf[:, g * GD:(g + 1) * GD]
        L = Eg
        for lev in range(g + 1):
            L = L + pltpu.roll(L, 1 << lev, axis=0)
        icnt = 1.0 / jnp.minimum(tt + 1, win).astype(F32)
        pooled.append(L[POOL_HALO:, :] * icnt - Eg[POOL_HALO:, :])
        inv.append(icnt)
    return pooled, inv


def _pool_fwd(proj, pw, pb, ps, rider=None):
    S = proj.shape[0]
    TB = min(256, S)

    def body(xp_ref, g_ref, pw_ref, pb_ref, ps_ref, y_ref, ebuf):
        i = pl.program_id(0)

        @pl.when(i == 0)
        def _():
            ebuf[0:POOL_HALO, :] = jnp.zeros((POOL_HALO, D), F32)
        ebuf[POOL_HALO:, :] = xp_ref[...]
        pooled, _ = _pooled(ebuf, i * TB, TB)
        ebuf[0:POOL_HALO, :] = ebuf[TB:TB + POOL_HALO, :]
        yp = jnp.concatenate([jnp.dot(pooled[g].astype(BF16), pw_ref[g], preferred_element_type=F32)
                              for g in range(NGRP)], axis=1) + pb_ref[...]
        silu, _ = _silu_parts(g_ref[...])
        y_ref[...] = (yp * ps_ref[...] * silu).astype(BF16)

    return _pcall(
        body, name="pool_fwd", grid=(S // TB,),
        in_specs=[pl.BlockSpec((TB, D), lambda i: (i, 2)), pl.BlockSpec((TB, D), lambda i: (i, 3)),
                  _full((NGRP, GD, GD)), _vec(), _vec()],
        out_specs=[pl.BlockSpec((TB, D), lambda i: (i, 0))],
        out_shape=[jax.ShapeDtypeStruct((S, D), BF16)],
        scratch_shapes=[pltpu.VMEM((TB + POOL_HALO, D), F32)],
        args=(proj, proj, pw, pb, ps), rider=rider)


def _out_post(yr, yp, w_out_l, x, gate, gpost, target=None, rider=None):
    S = x.shape[0]
    TM = min(512, S)
    KB = w_out_l.shape[1]
    last = target is not None

    def body(*refs):
        if last:
            yr_ref, yp_ref, w_ref, x_ref, gate_ref, gp_ref, t_ref, y_ref, xo_ref, loss_ref = refs
        else:
            yr_ref, yp_ref, w_ref, x_ref, gate_ref, gp_ref, y_ref, xo_ref = refs
        acc = jnp.zeros((TM, D), F32)
        for j in range(NDEV):
            src = yr_ref if j < NDEV // 2 else yp_ref
            k0 = (j % (NDEV // 2)) * KB
            acc = acc + jnp.dot(src[:, k0:k0 + KB], w_ref[j], preferred_element_type=F32)
        y_ref[...] = acc
        rstd = lax.rsqrt(jnp.mean(acc * acc, axis=-1, keepdims=True) + NORM_EPS)
        xn = x_ref[...] + gate_ref[...] * (acc * rstd * gp_ref[...])
        if last:
            err = xn - t_ref[...]
            xo_ref[...] = err * (1.0 / D)

            @pl.when(pl.program_id(0) == 0)
            def _():
                loss_ref[...] = jnp.zeros((SUB, D), F32)
            loss_ref[...] += _rowsum8(err * err)
        else:
            xo_ref[...] = xn

    row = pl.BlockSpec((TM, D), lambda i: (i, 0))
    in_specs = [row, row, _full((NDEV, KB, D)), row, _vec(), _vec()]
    out_specs = [row, row]
    out_shape = [jax.ShapeDtypeStruct((S, D), F32), jax.ShapeDtypeStruct((S, D), F32)]
    args = [yr, yp, w_out_l, x, gate, gpost]
    if last:
        in_specs.append(row)
        out_specs.append(_full((SUB, D)))
        out_shape.append(jax.ShapeDtypeStruct((SUB, D), F32))
        args.append(target)
    return _pcall(body, name="out_post_loss" if last else "out_post", grid=(S // TM,), in_specs=in_specs,
                  out_specs=out_specs, out_shape=out_shape, args=args, rider=rider)


def _out_bwd(dxo, y, yr, yp, w_out_l, gate, gpost, rider=None):
    S = y.shape[0]
    TM = min(256, S)
    KB = w_out_l.shape[1]
    nsteps = S // TM

    def body(dxo_ref, y_ref, yr_ref, yp_ref, w_ref, gate_ref, gp_ref, dyr_ref, dyp_ref, gw_ref, dgate_ref,
             dgp_ref, gw_acc, vacc):
        i = pl.program_id(0)

        @pl.when(i == 0)
        def _():
            gw_acc[...] = jnp.zeros_like(gw_acc)
            vacc[...] = jnp.zeros_like(vacc)
        yv = y_ref[...]
        dxo_v = dxo_ref[...]
        rstd = lax.rsqrt(jnp.mean(yv * yv, axis=-1, keepdims=True) + NORM_EPS)
        n = yv * rstd
        gp = gp_ref[...]
        vacc[0] += _rowsum8(dxo_v * (n * gp))
        drn = dxo_v * gate_ref[...]
        vacc[1] += _rowsum8(drn * n)
        dn = drn * gp
        dy = (rstd * (dn - n * jnp.mean(dn * n, axis=-1, keepdims=True))).astype(BF16)
        for j in range(NDEV):
            dst = dyr_ref if j < NDEV // 2 else dyp_ref
            k0 = (j % (NDEV // 2)) * KB
            dst[:, k0:k0 + KB] = lax.dot_general(dy, w_ref[j], (((1,), (1,)), ((), ())),
                                                 preferred_element_type=F32)
        gw_acc[0:D, :] += lax.dot_general(yr_ref[...], dy, (((0,), (0,)), ((), ())), preferred_element_type=F32)
        gw_acc[D:2 * D, :] += lax.dot_general(yp_ref[...], dy, (((0,), (0,)), ((), ())),
                                              preferred_element_type=F32)

        @pl.when(i == nsteps - 1)
        def _():
            gw_ref[...] = gw_acc[...].astype(BF16)
            dgate_ref[...] = _sum8(vacc[0])
            dgp_ref[...] = _sum8(vacc[1])

    row = pl.BlockSpec((TM, D), lambda i: (i, 0))
    return _pcall(
        body, name="out_bwd", grid=(nsteps,),
        in_specs=[row, row, row, row, _full((NDEV, KB, D)), _vec(), _vec()],
        out_specs=[row, row, _full((2 * D, D)), _vec(), _vec()],
        out_shape=[jax.ShapeDtypeStruct((S, D), F32), jax.ShapeDtypeStruct((S, D), F32),
                   jax.ShapeDtypeStruct((2 * D, D), BF16), jax.ShapeDtypeStruct((1, D), F32),
                   jax.ShapeDtypeStruct((1, D), F32)],
        scratch_shapes=[pltpu.VMEM((2 * D, D), F32), pltpu.VMEM((2, SUB, D), F32)],
        args=(dxo, y, yr, yp, w_out_l, gate, gpost), rider=rider)


def _rnn_bwd(dyr, hs, fwd, proj, conv_w, wa, wx, lam, rider=None):
    S = proj.shape[0]
    TB = min(256, S)
    nb = S // TB
    TE = TB + CONV_HALO
    A_BA, A_BX, A_LAM, A_CB, A_CW = 0, 1, 2, 3, 4

    def blk(i):
        return nb - 1 - i

    def prev8(i):
        return jnp.maximum(blk(i) * (TB // SUB) - 1, 0)

    def body(dyr_ref, hs_ref, hprev_ref, u_ref, r_ref, i_ref, a_ref, m_ref, xr_ref, xprev_ref, g_ref, cw_ref,
             wa_ref, wx_ref, lam_ref, dxr_ref, dg_ref, gcw_ref, gcb_ref, gwa_ref, gba_ref, gwx_ref, gbx_ref, glam_ref,
             xbuf, hbuf, abuf, dbuf, sa, sv, dh_ref, dhc, vacc):
        i = pl.program_id(0)
        first = blk(i) == 0

        @pl.when(i == 0)
        def _():
            abuf[TB:, :] = jnp.zeros((CONV_HALO, D), F32)
            dbuf[TB:, :] = jnp.zeros((CONV_HALO, D), F32)
            dhc[...] = jnp.zeros_like(dhc)
            vacc[...] = jnp.zeros_like(vacc)
            gwa_ref[...] = jnp.zeros_like(gwa_ref)
            gwx_ref[...] = jnp.zeros_like(gwx_ref)

        xbuf[0:CONV_HALO, :] = jnp.where(first, 0.0, xprev_ref[...])
        xbuf[CONV_HALO:, :] = xr_ref[...]
        taps = _taps(xbuf[...])
        u, r, ig, a, mult = u_ref[...], r_ref[...], i_ref[...], a_ref[...], m_ref[...]
        sp = jax.nn.softplus(-lam_ref[...])
        hbuf[0:CONV_HALO, :] = jnp.where(first, 0.0, hprev_ref[...])
        hbuf[CONV_HALO:, :] = hs_ref[...]
        hprev = pltpu.roll(hbuf[...], 1, axis=0)[CONV_HALO:, :]

        silu, dsilu = _silu_parts(g_ref[...])
        dyv = dyr_ref[...]
        dg_ref[...] = (dyv * hs_ref[...] * dsilu).astype(BF16)

        abuf[0:TB, :] = a
        b = pltpu.roll(abuf[...], TE - 1, axis=0)[0:TB, :]
        _scan(b, dyv * silu, sa, sv, dhc, dh_ref, reverse=True)
        abuf[TB:, :] = a[0:CONV_HALO, :]
        dh = dh_ref[...]

        dlog_a = dh * hprev * a - (dh * ig * u) * (a * a) / mult
        vacc[A_LAM] += _rowsum8(dlog_a * r)
        dzr = dlog_a * (-LRU_C * sp) * r * (1.0 - r)
        dzi = (dh * mult * u) * ig * (1.0 - ig)
        vacc[A_BA] += _rowsum8(dzr)
        vacc[A_BX] += _rowsum8(dzi)
        ub, dzrb, dzib = u.astype(BF16), dzr.astype(BF16), dzi.astype(BF16)
        du_g = []
        for h in range(NHEAD):
            cs = slice(h * HD, (h + 1) * HD)
            gwa_ref[h] += lax.dot_general(ub[:, cs], dzrb[:, cs], (((0,), (0,)), ((), ())),
                                          preferred_element_type=F32)
            gwx_ref[h] += lax.dot_general(ub[:, cs], dzib[:, cs], (((0,), (0,)), ((), ())),
                                          preferred_element_type=F32)
            du_g.append(lax.dot_general(dzrb[:, cs], wa_ref[h], (((1,), (1,)), ((), ())),
                                        preferred_element_type=F32)
                        + lax.dot_general(dzib[:, cs], wx_ref[h], (((1,), (1,)), ((), ())),
                                          preferred_element_type=F32))
        du = dh * mult * ig + jnp.concatenate(du_g, axis=1)

        dbuf[0:TB, :] = du
        Dd = dbuf[...]
        w = cw_ref[...]
        dx = Dd * w[3:4, :]
        for k in range(CONV_K - 1):
            dx = dx + pltpu.roll(Dd, TE - (CONV_K - 1 - k), axis=0) * w[k:k + 1, :]
        dxr_ref[...] = dx[0:TB, :].astype(BF16)
        dbuf[TB:, :] = du[0:CONV_HALO, :]
        vacc[A_CB] += _rowsum8(du)
        for k in range(CONV_K):
            vacc[A_CW + k] += _rowsum8(du * taps[k])

        @pl.when(i == nb - 1)
        def _():
            gba_ref[...] = _sum8(vacc[A_BA])
            gbx_ref[...] = _sum8(vacc[A_BX])
            glam_ref[...] = _sum8(vacc[A_LAM]) * (LRU_C * jax.nn.sigmoid(-lam_ref[...]))
            gcb_ref[...] = _sum8(vacc[A_CB])
            for k in range(CONV_K):
                gcw_ref[k:k + 1, :] = _sum8(vacc[A_CW + k])

    rowb = pl.BlockSpec((TB, D), lambda i: (blk(i), 0))
    halo = pl.BlockSpec((SUB, D), lambda i: (prev8(i), 0))
    wspec = _full((NHEAD, HD, HD))
    return _pcall(
        body, name="rnn_bwd", grid=(nb,),
        in_specs=[rowb, rowb, halo] + [rowb] * 5 + [rowb, halo, pl.BlockSpec((TB, D), lambda i: (blk(i), 1)),
                                                    _full((CONV_K, D)), wspec, wspec, _vec()],
        out_specs=[rowb, rowb, _full((CONV_K, D)), _vec(), wspec, _vec(), wspec, _vec(), _vec()],
        out_shape=[jax.ShapeDtypeStruct((S, D), BF16), jax.ShapeDtypeStruct((S, D), BF16),
                   jax.ShapeDtypeStruct((CONV_K, D), F32), jax.ShapeDtypeStruct((1, D), F32),
                   jax.ShapeDtypeStruct((NHEAD, HD, HD), F32), jax.ShapeDtypeStruct((1, D), F32),
                   jax.ShapeDtypeStruct((NHEAD, HD, HD), F32), jax.ShapeDtypeStruct((1, D), F32),
                   jax.ShapeDtypeStruct((1, D), F32)],
        scratch_shapes=[pltpu.VMEM((TE, D), F32), pltpu.VMEM((TE, D), F32), pltpu.VMEM((TE, D), F32),
                        pltpu.VMEM((TE, D), F32), pltpu.VMEM((D // LANE, TB, LANE), F32),
                        pltpu.VMEM((D // LANE, TB, LANE), F32), pltpu.VMEM((TB, D), F32), pltpu.VMEM((SUB, D), F32),
                        pltpu.VMEM((A_CW + CONV_K, SUB, D), F32)],
        args=(dyr, hs, hs, *fwd, proj, proj, proj, conv_w, wa, wx, lam), rider=rider)


def _pool_bwd(dyp, proj, pw, pb, ps, rider=None):
    S = proj.shape[0]
    TB = min(256, S)
    nb = S // TB
    TE = TB + POOL_HALO

    def blk(i):
        return nb - 1 - i

    def body(dy_ref, xp_ref, xprev_ref, g_ref, pw_ref, pb_ref, ps_ref, dxp_ref, dg_ref, gpw_ref, gpb_ref,
             gps_ref, ebuf, qbuf, vacc):
        i = pl.program_id(0)
        first = blk(i) == 0

        @pl.when(i == 0)
        def _():
            qbuf[TB:, :] = jnp.zeros((POOL_HALO, D), F32)
            vacc[...] = jnp.zeros_like(vacc)
            gpw_ref[...] = jnp.zeros_like(gpw_ref)

        ebuf[0:POOL_HALO, :] = jnp.where(first, 0.0, xprev_ref[...])
        ebuf[POOL_HALO:, :] = xp_ref[...]
        pooled, inv = _pooled(ebuf, blk(i) * TB, TB)
        pooled = [p.astype(BF16) for p in pooled]
        yp = jnp.concatenate([jnp.dot(pooled[g], pw_ref[g], preferred_element_type=F32)
                              for g in range(NGRP)], axis=1) + pb_ref[...]
        silu, dsilu = _silu_parts(g_ref[...])
        dy = dy_ref[...]
        ps = ps_ref[...]
        dyp_v = dy * ps * silu
        vacc[0] += _rowsum8(dy * yp * silu)
        vacc[1] += _rowsum8(dyp_v)
        dg_ref[...] = (dy * yp * ps * dsilu).astype(BF16)
        dypb = dyp_v.astype(BF16)
        for g in range(NGRP):
            cs = slice(g * GD, (g + 1) * GD)
            gpw_ref[g] += lax.dot_general(pooled[g], dypb[:, cs], (((0,), (0,)), ((), ())),
                                          preferred_element_type=F32)
            dpool = lax.dot_general(dypb[:, cs], pw_ref[g], (((1,), (1,)), ((), ())),
                                    preferred_element_type=F32)
            qbuf[0:TB, cs] = dpool * inv[g]
            L = qbuf[:, cs]
            for lev in range(g + 1):
                L = L + pltpu.roll(L, TE - (1 << lev), axis=0)
            dxp_ref[:, cs] = (L[0:TB, :] - dpool).astype(BF16)
        qbuf[TB:, :] = qbuf[0:POOL_HALO, :]

        @pl.when(i == nb - 1)
        def _():
            gps_ref[...] = _sum8(vacc[0])
            gpb_ref[...] = _sum8(vacc[1])

    rowb = pl.BlockSpec((TB, D), lambda i: (blk(i), 0))
    return _pcall(
        body, name="pool_bwd", grid=(nb,),
        in_specs=[rowb, pl.BlockSpec((TB, D), lambda i: (blk(i), 2)),
                  pl.BlockSpec((POOL_HALO, D), lambda i: (jnp.maximum(blk(i) * (TB // POOL_HALO) - 1, 0), 2)),
                  pl.BlockSpec((TB, D), lambda i: (blk(i), 3)), _full((NGRP, GD, GD)), _vec(), _vec()],
        out_specs=[rowb, rowb, _full((NGRP, GD, GD)), _vec(), _vec()],
        out_shape=[jax.ShapeDtypeStruct((S, D), BF16), jax.ShapeDtypeStruct((S, D), BF16),
                   jax.ShapeDtypeStruct((NGRP, GD, GD), F32), jax.ShapeDtypeStruct((1, D), F32),
                   jax.ShapeDtypeStruct((1, D), F32)],
        scratch_shapes=[pltpu.VMEM((TE, D), F32), pltpu.VMEM((TE, D), F32), pltpu.VMEM((2, SUB, D), F32)],
        args=(dyp, proj, proj, proj, pw, pb, ps), rider=rider)


def _in_bwd(dq, w_in_l, x, dxo, gpre, scale1, rider=None):
    S = x.shape[0]
    TM = min(256, S)
    NB = w_in_l.shape[2]
    nsteps = S // TM
    per_q = D // NB

    def body(d0, d1, d2, d3, w_ref, x_ref, dxo_ref, g_ref, sc_ref, dx_ref, dsh_ref, dsc_ref, dg_ref, vacc):
        i = pl.program_id(0)

        @pl.when(i == 0)
        def _():
            vacc[...] = jnp.zeros_like(vacc)
        dref = (d0, d1, d2, d3)
        dh = jnp.zeros((TM, D), F32)
        for j in range(NDEV):
            c0 = (j % per_q) * NB
            dh = dh + lax.dot_general(dref[j // per_q][:, c0:c0 + NB], w_ref[j], (((1,), (1,)), ((), ())),
                                      preferred_element_type=F32)
        xv = x_ref[...]
        rstd = lax.rsqrt(jnp.mean(xv * xv, axis=-1, keepdims=True) + NORM_EPS)
        xn = xv * rstd
        g, sc = g_ref[...], sc_ref[...]
        vacc[0] += _rowsum8(dh)
        vacc[1] += _rowsum8(dh * (xn * g))
        vacc[2] += _rowsum8(dh * sc * xn)
        dxn = dh * sc * g
        dx_ref[...] = dxo_ref[...] + rstd * (dxn - xn * jnp.mean(dxn * xn, axis=-1, keepdims=True))

        @pl.when(i == nsteps - 1)
        def _():
            dsh_ref[...] = _sum8(vacc[0])
            dsc_ref[...] = _sum8(vacc[1])
            dg_ref[...] = _sum8(vacc[2])

    row = pl.BlockSpec((TM, D), lambda i: (i, 0))
    return _pcall(
        body, name="in_bwd", grid=(nsteps,),
        in_specs=[row, row, row, row, _full((NDEV, D, NB)), row, row, _vec(), _vec()],
        out_specs=[row, _vec(), _vec(), _vec()],
        out_shape=[jax.ShapeDtypeStruct((S, D), F32)] + [jax.ShapeDtypeStruct((1, D), F32)] * 3,
        scratch_shapes=[pltpu.VMEM((3, SUB, D), F32)],
        args=(*dq, w_in_l, x, dxo, gpre, scale1), rider=rider)


def _grad_w_in(h, dq, NB, part, rider=None):
    S = h.shape[0]
    TK = min(512, S)
    nk = S // TK
    RH = D // 2

    def body(h_ref, d0, d1, d2, d3, o_ref, acc):
        k = pl.program_id(0)

        @pl.when(k == 0)
        def _():
            acc[...] = jnp.zeros_like(acc)
        hv = h_ref[...]
        for q, d_ref in enumerate((d0, d1, d2, d3)):
            acc[:, q * D:(q + 1) * D] += lax.dot_general(hv, d_ref[...], (((0,), (0,)), ((), ())),
                                                         preferred_element_type=F32)

        @pl.when(k == nk - 1)
        def _():
            for j in range(NDEV):
                o_ref[j] = acc[:, j * NB:(j + 1) * NB].astype(BF16)

    row = pl.BlockSpec((TK, D), lambda k: (k, 0))
    return _pcall(
        body, name="grad_w_in", grid=(nk,),
        in_specs=[pl.BlockSpec((TK, RH), lambda k: (k, part)), row, row, row, row],
        out_specs=[_full((NDEV, RH, NB))],
        out_shape=[jax.ShapeDtypeStruct((NDEV, RH, NB), BF16)],
        scratch_shapes=[pltpu.VMEM((RH, NQ * D), F32)],
        args=(h, *dq), rider=rider)


def _adamw_math(g, w, m, v):
    m2 = ADAM_B1 * m + (1.0 - ADAM_B1) * g
    v2 = ADAM_B2 * v + (1.0 - ADAM_B2) * (g * g)
    m_hat = m2 / (1.0 - ADAM_B1 ** ADAM_STEP)
    v_hat = v2 / (1.0 - ADAM_B2 ** ADAM_STEP)
    delta = -ADAM_LR * (m_hat / (jnp.sqrt(v_hat) + ADAM_EPS) + ADAM_WD * w)
    return delta, m2, v2


def _adamw(name, gs, w, m, v, TR, rider=None):
    L = len(gs)
    n, R, C = gs[0].shape

    def body(*refs):
        g_refs = refs[:L]
        w_ref, m_ref, v_ref, go_ref, do_ref, mo_ref, vo_ref = refs[L:]
        lay = pl.program_id(0)
        for li in range(L):
            @pl.when(lay == li)
            def _(li=li):
                g = g_refs[li][0].astype(F32)
                for s in range(1, n):
                    g = g + g_refs[li][s].astype(F32)
                delta, m2, v2 = _adamw_math(g, w_ref[...], m_ref[...], v_ref[...])
                go_ref[...] = g
                do_ref[...] = delta
                mo_ref[...] = m2
                vo_ref[...] = v2

    lrc = pl.BlockSpec((None, TR, C), lambda lay, r: (lay, r, 0))
    g_specs = [pl.BlockSpec((n, TR, C), lambda lay, r, li=li: (0, jnp.where(lay == li, r, 0), 0))
               for li in range(L)]
    return _pcall(
        body, name=name, grid=(L, R // TR),
        in_specs=g_specs + [lrc, lrc, lrc], out_specs=[lrc] * 4,
        out_shape=[jax.ShapeDtypeStruct((L, R, C), F32)] * 4,
        args=(*gs, w, m, v), rider=rider)


def _ada_adamw(c_all_t, dm, w, m, v, rider=None):
    L, _, nc = w.shape

    def body(c_ref, dm_ref, w_ref, m_ref, v_ref, go_ref, do_ref, mo_ref, vo_ref):
        cv = c_ref[...]
        ca = cv * jax.nn.sigmoid(cv)
        dmv = dm_ref[...]
        g = ca[:, 0:1] * dmv[0:1, :]
        for b in range(1, NDEV):
            g = g + ca[:, b:b + 1] * dmv[b:b + 1, :]
        delta, m2, v2 = _adamw_math(g, w_ref[...], m_ref[...], v_ref[...])
        go_ref[...] = g
        do_ref[...] = delta
        mo_ref[...] = m2
        vo_ref[...] = v2

    big = pl.BlockSpec((None, D, nc), lambda lay: (lay, 0, 0))
    return _pcall(
        body, name="ada_adamw", grid=(L,),
        in_specs=[_full((D, NDEV)), pl.BlockSpec((None, NDEV, nc), lambda lay: (lay, 0, 0)), big, big, big],
        out_specs=[big] * 4, out_shape=[jax.ShapeDtypeStruct((L, D, nc), F32)] * 4,
        args=(c_all_t, dm, w, m, v), rider=rider)


def _sum_slots(recv):
    n, R, C = recv.shape

    def body(r_ref, o_ref):
        acc = r_ref[0].astype(F32)
        for s in range(1, n):
            acc = acc + r_ref[s].astype(F32)
        o_ref[...] = acc

    return pl.pallas_call(body, name="sum_slots", out_shape=jax.ShapeDtypeStruct((R, C), F32),
                          compiler_params=_cparams())(recv)


def _pad_rows(a, rows):
    return jnp.pad(a, ((0, rows - a.shape[0]), (0, 0)))


def _pack_sharded_block(pool_w, pool_b, conv_w):
    return jnp.concatenate([pool_w.reshape(-1, PACK_C), _pad_rows(pool_b.reshape(-1, PACK_C), SUB),
                            _pad_rows(conv_w.reshape(-1, PACK_C), SUB)], axis=0)


def _unpack_sharded_block(p):
    n_pw = DEPTH * NGRP * (GD // NDEV)
    pool_w = p[:n_pw].reshape(DEPTH, NGRP, GD // NDEV, GD)
    pool_b = p[n_pw].reshape(DEPTH, NGRP, GD // NDEV)
    conv_w = p[n_pw + SUB:n_pw + SUB + DEPTH * CONV_K * (D // NDEV) // PACK_C].reshape(DEPTH, CONV_K, D // NDEV)
    return pool_w, pool_b, conv_w


def _blocks_of_full(pool_w, pool_b, conv_w):
    pw = pool_w.reshape(DEPTH, NGRP, NDEV, GD // NDEV, GD).transpose(2, 0, 1, 3, 4).reshape(NDEV, -1, PACK_C)
    pb = pool_b.reshape(DEPTH, NGRP, NDEV, GD // NDEV).transpose(2, 0, 1, 3).reshape(NDEV, -1, PACK_C)
    cw = conv_w.reshape(DEPTH, CONV_K, NDEV, D // NDEV).transpose(2, 0, 1, 3).reshape(NDEV, -1, PACK_C)
    pad = lambda a: jnp.pad(a, ((0, 0), (0, SUB - a.shape[1]), (0, 0)))
    return jnp.concatenate([pw, pad(pb), pad(cw)], axis=1)


def _full_of_blocks(p):
    n_pw = DEPTH * NGRP * (GD // NDEV)
    pool_w = p[:, :n_pw].reshape(NDEV, DEPTH, NGRP, GD // NDEV, GD).transpose(1, 2, 0, 3, 4)
    pool_b = p[:, n_pw].reshape(NDEV, DEPTH, NGRP, GD // NDEV).transpose(1, 2, 0, 3)
    n_cw = DEPTH * CONV_K * (D // NDEV) // PACK_C
    conv_w = p[:, n_pw + SUB:n_pw + SUB + n_cw].reshape(NDEV, DEPTH, CONV_K, D // NDEV).transpose(1, 2, 0, 3)
    return (pool_w.reshape(DEPTH, NGRP, GD, GD), pool_b.reshape(DEPTH, NGRP, GD),
            conv_w.reshape(DEPTH, CONV_K, D))


def _pack_replicated(t, keys, chunk_rows):
    p = jnp.concatenate([t[k].reshape(-1, PACK_C) for k in keys], axis=0)
    return _pad_rows(p, NDEV * chunk_rows)


def _unpack_replicated(p, like, keys):
    out, r0 = {}, 0
    for k in keys:
        rows = like[k].size // PACK_C
        out[k] = p[r0:r0 + rows].reshape(like[k].shape)
        r0 += rows
    return out


def kernel(x, c, ada_w, ada_b, pre_norm_g, w_in, conv_w, conv_b, gate_a_w, gate_a_b, gate_x_w, gate_x_b, lru_lambda, pool_w, pool_b, pool_scale, w_out, post_norm_g, loss_target, m_ada_w, m_ada_b, m_pre_norm_g, m_w_in, m_conv_w, m_conv_b, m_gate_a_w, m_gate_a_b, m_gate_x_w, m_gate_x_b, m_lru_lambda, m_pool_w, m_pool_b, m_pool_scale, m_w_out, m_post_norm_g, v_ada_w, v_ada_b, v_pre_norm_g, v_w_in, v_conv_w, v_conv_b, v_gate_a_w, v_gate_a_b, v_gate_x_w, v_gate_x_b, v_lru_lambda, v_pool_w, v_pool_b, v_pool_scale, v_w_out, v_post_norm_g):
    W = dict(ada_w=ada_w, ada_b=ada_b, pre_norm_g=pre_norm_g, w_in=w_in, conv_w=conv_w, conv_b=conv_b,
             gate_a_w=gate_a_w, gate_a_b=gate_a_b, gate_x_w=gate_x_w, gate_x_b=gate_x_b, lru_lambda=lru_lambda,
             pool_w=pool_w, pool_b=pool_b, pool_scale=pool_scale, w_out=w_out, post_norm_g=post_norm_g)
    M = dict(ada_w=m_ada_w, ada_b=m_ada_b, pre_norm_g=m_pre_norm_g, w_in=m_w_in, conv_w=m_conv_w,
             conv_b=m_conv_b, gate_a_w=m_gate_a_w, gate_a_b=m_gate_a_b, gate_x_w=m_gate_x_w,
             gate_x_b=m_gate_x_b, lru_lambda=m_lru_lambda, pool_w=m_pool_w, pool_b=m_pool_b,
             pool_scale=m_pool_scale, w_out=m_w_out, post_norm_g=m_post_norm_g)
    V = dict(ada_w=v_ada_w, ada_b=v_ada_b, pre_norm_g=v_pre_norm_g, w_in=v_w_in, conv_w=v_conv_w,
             conv_b=v_conv_b, gate_a_w=v_gate_a_w, gate_a_b=v_gate_a_b, gate_x_w=v_gate_x_w,
             gate_x_b=v_gate_x_b, lru_lambda=v_lru_lambda, pool_w=v_pool_w, pool_b=v_pool_b,
             pool_scale=v_pool_scale, w_out=v_w_out, post_norm_g=v_post_norm_g)
    S = x.shape[1]
    me = 4 * lax.axis_index("x") + 2 * lax.axis_index("y") + lax.axis_index("c")
    xs = x.reshape(S, D)
    tgt = loss_target.reshape(S, D)
    nc = ada_w.shape[2]
    NB = w_in.shape[2]
    vec = lambda a: a.reshape(1, D)
    w_in_b, w_out_b = w_in.astype(BF16), w_out.astype(BF16)

    c_slots, w_in0 = _exchange("gather_c_w_in0", _AllGather2([jnp.broadcast_to(c, (SUB, D)), w_in_b[0]]))
    c_all = c_slots[:, 0, :]
    (mod_slots,) = _exchange("gather_mod", _Direct(ag=[_mod_cols(c_all, ada_w)]))
    mod = lax.dynamic_index_in_dim(mod_slots, me, axis=1, keepdims=False)
    mod = mod.reshape(NDEV, DEPTH, nc).transpose(1, 0, 2).reshape(DEPTH, 3 * D) + ada_b
    mods = [(vec(mod[l, :D]), vec(1.0 + mod[l, D:2 * D]), vec(mod[l, 2 * D:])) for l in range(DEPTH)]

    w_in_all, w_out_all = [w_in0, None], [None, None]
    saved = []
    xl = xs
    for l in range(DEPTH):
        shift, scale1, gate = mods[l]
        wa, wx = gate_a_w[l].astype(BF16), gate_x_w[l].astype(BF16)
        rider = _AllGather2([_pack_sharded_block(pool_w, pool_b, conv_w), w_out_b[0]]) if l == 0 else None
        (h, proj), got = _pre_proj(xl, vec(pre_norm_g[l]), scale1, shift, w_in_all[l], rider=rider)
        if l == 0:
            pool_w_f, pool_b_f, conv_w_f = _full_of_blocks(got[0])
            w_out_all[0] = got[1]
        pw = pool_w_f[l].astype(BF16)
        rider = _AllGather2([w_in_b[1]]) if l == 0 else None
        (hs, yr, *fwd), got = _rnn_fwd(proj, conv_w_f[l], vec(conv_b[l]), wa, gate_a_b[l].reshape(1, D), wx,
                                 gate_x_b[l].reshape(1, D), vec(lru_lambda[l]), rider=rider)
        if l == 0:
            w_in_all[1] = got[0]
        (yp,), _ = _pool_fwd(proj, pw, pool_b_f[l].reshape(1, D), vec(pool_scale[l]))
        if l == DEPTH - 1:
            (y, x_next, loss_acc), _ = _out_post(yr, yp, w_out_all[l], xl, gate, vec(post_norm_g[l]), tgt)
        else:
            (y, x_next), (w_out_all[1],) = _out_post(yr, yp, w_out_all[l], xl, gate, vec(post_norm_g[l]),
                                                     rider=_AllGather2([w_out_b[1]]))
        saved.append((xl, h, proj, hs, fwd, yr, yp, y, wa, wx, pw))
        xl = x_next
    loss = lax.psum((0.5 / D) * jnp.sum(loss_acc), ("x", "y", "c"))

    dxo = xl
    G = {k: [None] * DEPTH for k in WEIGHTS}
    dmod = [None] * DEPTH
    recv_in, recv_out = [[None, None] for _ in range(DEPTH)], [None] * DEPTH
    full = dict(conv_w=(CONV_K, D), pool_w=(NGRP, GD, GD), pool_b=(NGRP, GD))
    stack = lambda k: jnp.stack([g.reshape(full.get(k, W[k].shape[1:])) for g in G[k]])
    gw_bot_prev = None
    for l in reversed(range(DEPTH)):
        xin, h, proj, hs, fwd, yr, yp, y, wa, wx, pw = saved[l]
        shift, scale1, gate = mods[l]
        rider = None if gw_bot_prev is None else _Direct(a2a=[gw_bot_prev])
        (dyr, dyp, gw_out, dgate, G['post_norm_g'][l]), got = _out_bwd(dxo, y, yr, yp, w_out_all[l], gate,
                                                                       vec(post_norm_g[l]), rider=rider)
        if got:
            recv_in[l + 1][1] = got[0]
        ((dxr, dgr, G['conv_w'][l], G['conv_b'][l], G['gate_a_w'][l], G['gate_a_b'][l], G['gate_x_w'][l],
          G['gate_x_b'][l], G['lru_lambda'][l]), (recv_out[l],)) = _rnn_bwd(
            dyr, hs, fwd, proj, conv_w_f[l], wa, wx, vec(lru_lambda[l]),
            rider=_Direct(a2a=[gw_out.reshape(NDEV, 2 * D // NDEV, D)]))
        (dxp, dgp, G['pool_w'][l], G['pool_b'][l], G['pool_scale'][l]), _ = _pool_bwd(
            dyp, proj, pw, pool_b_f[l].reshape(1, D), vec(pool_scale[l]))
        dq = (dxr, dgr, dxp, dgp)
        if l > 0:
            (gw_top,), _ = _grad_w_in(h, dq, NB, 0)
            (dxo, dshift, dscale, G['pre_norm_g'][l]), (recv_in[l][0],) = _in_bwd(
                dq, w_in_all[l], xin, dxo, vec(pre_norm_g[l]), scale1, rider=_Direct(a2a=[gw_top]))
            (gw_bot_prev,), _ = _grad_w_in(h, dq, NB, 1)
        else:
            Ge = {k: stack(k) for k in REP_EARLY + ['pool_w', 'pool_b', 'conv_w']}
            early = jnp.concatenate([_blocks_of_full(Ge['pool_w'], Ge['pool_b'], Ge['conv_w']),
                                     _pack_replicated(Ge, REP_EARLY, PACK_ROWS).reshape(NDEV, PACK_ROWS, PACK_C)],
                                    axis=1).astype(BF16)
            (gw_top,), (early_recv,) = _grad_w_in(h, dq, NB, 0, rider=_Direct(a2a=[early]))
            early_sum = _sum_slots(early_recv)
            (dxo, dshift, dscale, G['pre_norm_g'][l]), (recv_in[l][0], early_all) = _in_bwd(
                dq, w_in_all[l], xin, dxo, vec(pre_norm_g[l]), scale1,
                rider=_Both(_Direct(a2a=[gw_top]), _AllGather2([early_sum[PACK_ROWS:]])))
        dmod[l] = jnp.concatenate([dshift, dscale, dgate], axis=1)
    grad_x = dxo.reshape(x.shape)

    Gl = dict(ada_b=jnp.concatenate(dmod, axis=0), pre_norm_g=stack('pre_norm_g'))
    late = _pack_replicated(Gl, REP_LATE, LATE_ROWS).reshape(NDEV, LATE_ROWS, PACK_C)
    (gw_bot,), (late_recv, dmod_all) = _grad_w_in(h, dq, NB, 1, rider=_Direct(a2a=[late], ag=[Gl['ada_b']]))
    late_sum = _sum_slots(late_recv)
    recv_in[0][1], late_all = _exchange("grads_tail", _Both(_Direct(a2a=[gw_bot]), _AllGather2([late_sum])))

    out = {}
    halves = lambda a: a.reshape(2 * DEPTH, D // 2, NB)
    out['w_in'], _ = _adamw("adamw_w_in", [r for l in range(DEPTH) for r in recv_in[l]], halves(w_in),
                            halves(M['w_in']), halves(V['w_in']), 256)
    out['w_out'], _ = _adamw("adamw_w_out", recv_out, w_out, M['w_out'], V['w_out'], 256)
    dm = lax.dynamic_slice_in_dim(dmod_all.reshape(NDEV, DEPTH, 3 * D), me * nc, nc, axis=2)
    out['ada_w'], _ = _ada_adamw(c_all.T, dm.transpose(1, 0, 2), ada_w, M['ada_w'], V['ada_w'])
    g_small = jnp.concatenate([early_sum[:PACK_ROWS], early_all.reshape(NDEV * PACK_ROWS, PACK_C),
                               late_all.reshape(NDEV * LATE_ROWS, PACK_C)], axis=0)

    def packs(T):
        return jnp.concatenate([_pack_sharded_block(T['pool_w'], T['pool_b'], T['conv_w']),
                                _pack_replicated(T, REP_EARLY, PACK_ROWS),
                                _pack_replicated(T, REP_LATE, LATE_ROWS)], axis=0)[None]
    res_small, _ = _adamw("adamw_small", [g_small[None]], packs(W), packs(M), packs(V), g_small.shape[0] // 2)
    n_early = (1 + NDEV) * PACK_ROWS
    for idx in range(4):
        p = res_small[idx][0]
        pw_, pb_, cw_ = _unpack_sharded_block(p[:PACK_ROWS])
        rep = _unpack_replicated(p[PACK_ROWS:n_early], W, REP_EARLY)
        rep.update(_unpack_replicated(p[n_early:], W, REP_LATE))
        rep.update(pool_w=pw_, pool_b=pb_, conv_w=cw_)
        for k, a in rep.items():
            out.setdefault(k, [None] * 4)[idx] = a
    for k in ('w_in', 'w_out', 'ada_w'):
        out[k] = [a.reshape(W[k].shape) for a in out[k]]

    return (loss, grad_x, *[out[k][0] for k in WEIGHTS], *[out[k][1] for k in WEIGHTS],
            *[out[k][2] for k in WEIGHTS], *[out[k][3] for k in WEIGHTS])
```

```python
import functools

import jax
import jax.numpy as jnp
from jax import lax
from jax.experimental import pallas as pl
from jax.experimental.pallas import tpu as pltpu

F32, BF16 = jnp.float32, jnp.bfloat16
MESH = pl.DeviceIdType.MESH
HIGHEST = lax.Precision.HIGHEST

NDEV = 8
DEPTH = 2
D = 1024
NHEAD, HD = 8, 128
NGRP, GD = 4, 256
WINS = (2, 4, 8, 16)
CONV_K = 4
CONV_HALO = 8
POOL_HALO = 16
LRU_C = 8.0
NORM_EPS = 1e-6
ADAM_LR, ADAM_B1, ADAM_B2, ADAM_EPS, ADAM_WD, ADAM_STEP = 0.001, 0.9, 0.999, 1e-08, 0.01, 10
VMEM_LIMIT = 56 * 1024 * 1024
NQ = 4
SUB = 8
LANE = 128
PACK_C = 256
PACK_ROWS = 272

WEIGHTS = ['ada_w', 'ada_b', 'pre_norm_g', 'w_in', 'conv_w', 'conv_b', 'gate_a_w', 'gate_a_b', 'gate_x_w',
           'gate_x_b', 'lru_lambda', 'pool_w', 'pool_b', 'pool_scale', 'w_out', 'post_norm_g']
REP_EARLY = ['conv_b', 'gate_a_w', 'gate_a_b', 'gate_x_w', 'gate_x_b', 'lru_lambda', 'pool_scale', 'post_norm_g']
REP_LATE = ['ada_b', 'pre_norm_g']
LATE_PACK_ROWS = 32


def _cparams(*sem):
    return pltpu.CompilerParams(dimension_semantics=sem, vmem_limit_bytes=VMEM_LIMIT)


def _vec(n=D):
    return pl.BlockSpec((1, n), lambda *_: (0, 0))


def _full(shape):
    nd = len(shape)
    return pl.BlockSpec(shape, lambda *_: (0,) * nd)


def _rowsum8(z):
    return z.reshape(z.shape[0] // SUB, SUB, z.shape[1]).sum(axis=0)


def _sum8(acc):
    return jnp.sum(acc, axis=0, keepdims=True)


def _sigmoid(z):
    return 0.5 * jnp.tanh(0.5 * z) + 0.5


def _silu_parts(g):
    sg = _sigmoid(g)
    return g * sg, sg * (1.0 + g * (1.0 - sg))


def _one_minus_sq(a, log_a):
    z = 2.0 * log_a
    p = 1.0 / 24.0
    for k in (6.0, 2.0, 1.0):
        p = p * z + 1.0 / k
    return jnp.where(z > -0.03, -(p * z), 1.0 - a * a)


def _place():
    x, y, c = lax.axis_index("x"), lax.axis_index("y"), lax.axis_index("c")
    return x, y, c, 4 * x + 2 * y + c


class _Direct:
    def __init__(self, a2a=(), ag=()):
        self.arrays = list(a2a) + list(ag)
        self.n_a, self.n = len(a2a), len(self.arrays)
        self.out_shape = ([jax.ShapeDtypeStruct(a.shape, a.dtype) for a in a2a]
                          + [jax.ShapeDtypeStruct((NDEV,) + a.shape, a.dtype) for a in ag])
        self.scratch = [pltpu.SemaphoreType.DMA((self.n, NDEV - 1)), pltpu.SemaphoreType.DMA((self.n, NDEV - 1)),
                        pltpu.SemaphoreType.DMA((self.n,))]

    def _copies(self, ins, outs, sems):
        send_sems, recv_sems, local_sems = sems
        x, y, c, me = _place()
        local, remote = [], []
        for t in range(self.n):
            src = ins[t].at[me] if t < self.n_a else ins[t]
            local.append(pltpu.make_async_copy(src, outs[t].at[me], local_sems.at[t]))
        for r in range(1, NDEV):
            px = 1 - x if r & 4 else x
            py = 1 - y if r & 2 else y
            pc = 1 - c if r & 1 else c
            for t in range(self.n):
                src = ins[t].at[4 * px + 2 * py + pc] if t < self.n_a else ins[t]
                remote.append(pltpu.make_async_remote_copy(
                    src_ref=src, dst_ref=outs[t].at[me], send_sem=send_sems.at[t, r - 1],
                    recv_sem=recv_sems.at[t, r - 1], device_id=(px, py, pc), device_id_type=MESH))
        return local, remote

    def start(self, ins, outs, sems):
        local, remote = self._copies(ins, outs, sems)
        for cp in local + remote:
            cp.start()

    def finish(self, ins, outs, sems):
        local, remote = self._copies(ins, outs, sems)
        for cp in remote + local:
            cp.wait()


class _AllGather2:
    def __init__(self, arrays):
        self.arrays = list(arrays)
        self.n = len(self.arrays)
        self.out_shape = [jax.ShapeDtypeStruct((NDEV,) + a.shape, a.dtype) for a in self.arrays]
        self.scratch = [pltpu.SemaphoreType.DMA((self.n, NDEV - 1)), pltpu.SemaphoreType.DMA((self.n, NDEV - 1)),
                        pltpu.SemaphoreType.DMA((self.n,))]

    @staticmethod
    def _chips(x, y):
        return [(1 - x, y), (x, 1 - y), (1 - x, 1 - y)]

    def _copy(self, t, k, src, dst, to, sems):
        return pltpu.make_async_remote_copy(src_ref=src, dst_ref=dst, send_sem=sems[0].at[t, k],
                                            recv_sem=sems[1].at[t, k], device_id=to, device_id_type=MESH)

    def start(self, ins, outs, sems):
        x, y, c, me = _place()
        for t in range(self.n):
            pltpu.make_async_copy(ins[t], outs[t].at[me], sems[2].at[t]).start()
            self._copy(t, 0, ins[t], outs[t].at[me], (x, y, 1 - c), sems).start()
            for j, (px, py) in enumerate(self._chips(x, y)):
                self._copy(t, 1 + j, ins[t], outs[t].at[me], (px, py, c), sems).start()

    def finish(self, ins, outs, sems):
        x, y, c, me = _place()
        sib = (x, y, 1 - c)
        for j, (px, py) in enumerate(self._chips(x, y)):
            slot = 4 * px + 2 * py + c
            for t in range(self.n):
                self._copy(t, 1 + j, ins[t], outs[t].at[slot], sib, sems).wait_recv()
                self._copy(t, 4 + j, outs[t].at[slot], outs[t].at[slot], sib, sems).start()
        for t in range(self.n):
            for k in (0, 4, 5, 6):
                self._copy(t, k, ins[t], outs[t].at[me], sib, sems).wait_recv()
        for t in range(self.n):
            for k in range(NDEV - 1):
                self._copy(t, k, ins[t], outs[t].at[me], sib, sems).wait_send()
            pltpu.make_async_copy(ins[t], outs[t].at[me], sems[2].at[t]).wait()


class _Both:
    def __init__(self, *riders):
        self.riders = riders
        self.arrays = [a for r in riders for a in r.arrays]
        self.n = len(self.arrays)
        self.out_shape = [o for r in riders for o in r.out_shape]
        self.scratch = [s for r in riders for s in r.scratch]

    def _parts(self, ins, outs, sems):
        p, q = 0, 0
        for r in self.riders:
            yield r, ins[p:p + r.n], outs[p:p + r.n], sems[q:q + len(r.scratch)]
            p, q = p + r.n, q + len(r.scratch)

    def start(self, ins, outs, sems):
        for r, i, o, s in self._parts(ins, outs, sems):
            r.start(i, o, s)

    def finish(self, ins, outs, sems):
        for r, i, o, s in self._parts(ins, outs, sems):
            r.finish(i, o, s)


def _exchange(name, rider):
    n = rider.n

    def body(*refs):
        rider.start(refs[:n], refs[n:2 * n], refs[2 * n:])
        rider.finish(refs[:n], refs[n:2 * n], refs[2 * n:])

    any_spec = pl.BlockSpec(memory_space=pl.ANY)
    return list(pl.pallas_call(body, name=name, out_shape=rider.out_shape, in_specs=[any_spec] * n,
                               out_specs=[any_spec] * n, scratch_shapes=rider.scratch)(*rider.arrays))


def _pcall(body, *, name, grid, in_specs, out_specs, out_shape, args, scratch_shapes=(), rider=None):
    params = _cparams(*(("arbitrary",) * len(grid)))
    if rider is None:
        res = pl.pallas_call(body, name=name, grid=grid, in_specs=in_specs, out_specs=out_specs,
                             out_shape=out_shape, scratch_shapes=list(scratch_shapes),
                             compiler_params=params)(*args)
        return list(res), []
    n_in, n_out, n_scr, rn = len(in_specs), len(out_specs), len(scratch_shapes), rider.n

    def wrapped(*refs):
        cuts = [n_in, rn, n_out, rn, n_scr]
        parts, p = [], 0
        for n in cuts:
            parts.append(refs[p:p + n])
            p += n
        ins, r_in, outs, r_out, scr = parts
        sems = refs[p:]
        ids = [pl.program_id(a) for a in range(len(grid))]
        first = functools.reduce(jnp.logical_and, [i == 0 for i in ids])
        last = functools.reduce(jnp.logical_and, [i == g - 1 for i, g in zip(ids, grid)])

        @pl.when(first)
        def _():
            rider.start(r_in, r_out, sems)
        body(*ins, *outs, *scr)

        @pl.when(last)
        def _():
            rider.finish(r_in, r_out, sems)

    any_spec = pl.BlockSpec(memory_space=pl.ANY)
    res = pl.pallas_call(
        wrapped, name=name, grid=grid, in_specs=list(in_specs) + [any_spec] * rn,
        out_specs=list(out_specs) + [any_spec] * rn, out_shape=list(out_shape) + rider.out_shape,
        scratch_shapes=list(scratch_shapes) + rider.scratch, compiler_params=params)(*args, *rider.arrays)
    return list(res[:n_out]), list(res[n_out:])


def _mod_cols(c_all, ada_w):
    nc = ada_w.shape[2]

    def body(c_ref, w_ref, o_ref):
        cv = c_ref[...]
        ca = cv * jax.nn.sigmoid(cv)
        for l in range(DEPTH):
            o_ref[:, l * nc:(l + 1) * nc] = jnp.dot(ca, w_ref[l], precision=HIGHEST, preferred_element_type=F32)

    return pl.pallas_call(body, name="mod_cols", out_shape=jax.ShapeDtypeStruct((NDEV, DEPTH * nc), F32),
                          compiler_params=_cparams())(c_all, ada_w)


def _pre_proj(x, gpre, scale1, shift, w_in_l, rider=None):
    S = x.shape[0]
    TM = min(512, S)
    NB = w_in_l.shape[2]

    def body(x_ref, g_ref, sc_ref, sh_ref, w_ref, h_ref, p_ref):
        xv = x_ref[...]
        rstd = lax.rsqrt(jnp.mean(xv * xv, axis=-1, keepdims=True) + NORM_EPS)
        h = ((xv * rstd * g_ref[...]) * sc_ref[...] + sh_ref[...]).astype(BF16)
        h_ref[...] = h
        for j in range(NDEV):
            p_ref[:, j * NB:(j + 1) * NB] = jnp.dot(h, w_ref[j], preferred_element_type=F32)

    row = pl.BlockSpec((TM, D), lambda i: (i, 0))
    return _pcall(
        body, name="pre_proj", grid=(S // TM,),
        in_specs=[row, _vec(), _vec(), _vec(), _full((NDEV, D, NB))],
        out_specs=[row, pl.BlockSpec((TM, NDEV * NB), lambda i: (i, 0))],
        out_shape=[jax.ShapeDtypeStruct((S, D), BF16), jax.ShapeDtypeStruct((S, NDEV * NB), F32)],
        args=(x, gpre, scale1, shift, w_in_l), rider=rider)


def _taps(E):
    return [pltpu.roll(E, CONV_K - 1 - k, axis=0)[CONV_HALO:, :] for k in range(CONV_K - 1)] + [E[CONV_HALO:, :]]


def _conv(E, cw_ref, cb_ref):
    w = cw_ref[...]
    taps = _taps(E)
    acc = cb_ref[...] + taps[0] * w[0:1, :]
    for k in range(1, CONV_K):
        acc = acc + taps[k] * w[k:k + 1, :]
    return acc


def _gates(u, wa_ref, ba_ref, wx_ref, bx_ref, lam_ref):
    ub = u.astype(BF16)
    zr = jnp.concatenate([jnp.dot(ub[:, h * HD:(h + 1) * HD], wa_ref[h], preferred_element_type=F32)
                          for h in range(NHEAD)], axis=1)
    zi = jnp.concatenate([jnp.dot(ub[:, h * HD:(h + 1) * HD], wx_ref[h], preferred_element_type=F32)
                          for h in range(NHEAD)], axis=1)
    r = _sigmoid(zr + ba_ref[...])
    ig = _sigmoid(zi + bx_ref[...])
    sp = jax.nn.softplus(-lam_ref[...])
    log_a = (-LRU_C * r) * sp
    a = jnp.exp(log_a)
    mult = jnp.sqrt(_one_minus_sq(a, log_a))
    return a, mult, r, ig, sp


def _scan(a, v, sa, sv, carry_ref, out_ref, reverse):
    T = a.shape[0]
    n8 = T // SUB
    NC = D // LANE
    rows = range(SUB - 2, -1, -1) if reverse else range(1, SUB)
    for cb in range(NC):
        sa[cb] = a[:, cb * LANE:(cb + 1) * LANE]
        sv[cb] = v[:, cb * LANE:(cb + 1) * LANE]
    for cb in range(NC):
        r_in = SUB - 1 if reverse else 0
        Ap = sa[cb, pl.ds(r_in, n8, stride=SUB), :]
        Vp = sv[cb, pl.ds(r_in, n8, stride=SUB), :]
        for r in rows:
            Ar = sa[cb, pl.ds(r, n8, stride=SUB), :]
            Vp = sv[cb, pl.ds(r, n8, stride=SUB), :] + Ar * Vp
            Ap = Ar * Ap
            sa[cb, pl.ds(r, n8, stride=SUB), :] = Ap
            sv[cb, pl.ds(r, n8, stride=SUB), :] = Vp
    edge = 0 if reverse else SUB - 1

    def step(k, c):
        r0 = pl.multiple_of((n8 - 1 - k if reverse else k) * SUB, SUB)
        h = jnp.concatenate([sv[cb, pl.ds(r0, SUB), :] + sa[cb, pl.ds(r0, SUB), :] * c[:, cb * LANE:(cb + 1) * LANE]
                             for cb in range(NC)], axis=1)
        out_ref[pl.ds(r0, SUB), :] = h
        return jnp.broadcast_to(h[edge:edge + 1, :], (SUB, D))

    carry_ref[...] = lax.fori_loop(0, n8, step, carry_ref[...])


def _rnn_fwd(proj, conv_w, conv_b, wa, ba, wx, bx, lam, rider=None):
    S = proj.shape[0]
    TB = min(256, S)

    def body(xr_ref, g_ref, cw_ref, cb_ref, wa_ref, ba_ref, wx_ref, bx_ref, lam_ref, hs_ref, y_ref,
             u_ref, r_ref, i_ref, a_ref, m_ref, xbuf, sa, sv, hc):
        @pl.when(pl.program_id(0) == 0)
        def _():
            xbuf[0:CONV_HALO, :] = jnp.zeros((CONV_HALO, D), F32)
            hc[...] = jnp.zeros((SUB, D), F32)
        xbuf[CONV_HALO:, :] = xr_ref[...]
        u = _conv(xbuf[...], cw_ref, cb_ref)
        xbuf[0:CONV_HALO, :] = xbuf[TB:TB + CONV_HALO, :]
        a, mult, r, ig, _ = _gates(u, wa_ref, ba_ref, wx_ref, bx_ref, lam_ref)
        u_ref[...] = u
        r_ref[...] = r
        i_ref[...] = ig
        a_ref[...] = a
        m_ref[...] = mult
        _scan(a, mult * (ig * u), sa, sv, hc, hs_ref, reverse=False)
        silu, _ = _silu_parts(g_ref[...])
        y_ref[...] = (hs_ref[...] * silu).astype(BF16)

    rowb = pl.BlockSpec((TB, D), lambda i: (i, 0))
    return _pcall(
        body, name="rnn_fwd", grid=(S // TB,),
        in_specs=[rowb, pl.BlockSpec((TB, D), lambda i: (i, 1)),
                  _full((CONV_K, D)), _vec(), _full((NHEAD, HD, HD)), _vec(), _full((NHEAD, HD, HD)), _vec(),
                  _vec()],
        out_specs=[rowb] * 7,
        out_shape=[jax.ShapeDtypeStruct((S, D), F32), jax.ShapeDtypeStruct((S, D), BF16)]
        + [jax.ShapeDtypeStruct((S, D), F32)] * 5,
        scratch_shapes=[pltpu.VMEM((TB + CONV_HALO, D), F32), pltpu.VMEM((D // LANE, TB, LANE), F32),
                        pltpu.VMEM((D // LANE, TB, LANE), F32), pltpu.VMEM((SUB, D), F32)],
        args=(proj, proj, conv_w, conv_b, wa, ba, wx, bx, lam), rider=rider)


def _pooled(ebuf, t0, TB):
    tt = t0 + lax.broadcasted_iota(jnp.int32, (TB, 1), 0)
    pooled, inv = [], []
    for g, win in enumerate(WINS):
        Eg = ebuf[:, g * GD:(g + 1) * GD]
        L = Eg
        for lev in range(g + 1):
            L = L + pltpu.roll(L, 1 << lev, axis=0)
        icnt = 1.0 / jnp.minimum(tt + 1, win).astype(F32)
        pooled.append(L[POOL_HALO:, :] * icnt - Eg[POOL_HALO:, :])
        inv.append(icnt)
    return pooled, inv


def _pool_fwd(proj, pw, pb, ps, rider=None):
    S = proj.shape[0]
    TB = min(256, S)

    def body(xp_ref, g_ref, pw_ref, pb_ref, ps_ref, y_ref, ebuf):
        i = pl.program_id(0)

        @pl.when(i == 0)
        def _():
            ebuf[0:POOL_HALO, :] = jnp.zeros((POOL_HALO, D), F32)
        ebuf[POOL_HALO:, :] = xp_ref[...]
        pooled, _ = _pooled(ebuf, i * TB, TB)
        ebuf[0:POOL_HALO, :] = ebuf[TB:TB + POOL_HALO, :]
        yp = jnp.concatenate([jnp.dot(pooled[g].astype(BF16), pw_ref[g], preferred_element_type=F32)
                              for g in range(NGRP)], axis=1) + pb_ref[...]
        silu, _ = _silu_parts(g_ref[...])
        y_ref[...] = (yp * ps_ref[...] * silu).astype(BF16)

    return _pcall(
        body, name="pool_fwd", grid=(S // TB,),
        in_specs=[pl.BlockSpec((TB, D), lambda i: (i, 2)), pl.BlockSpec((TB, D), lambda i: (i, 3)),
                  _full((NGRP, GD, GD)), _vec(), _vec()],
        out_specs=[pl.BlockSpec((TB, D), lambda i: (i, 0))],
        out_shape=[jax.ShapeDtypeStruct((S, D), BF16)],
        scratch_shapes=[pltpu.VMEM((TB + POOL_HALO, D), F32)],
        args=(proj, proj, pw, pb, ps), rider=rider)


def _out_post(yr, yp, w_out_l, x, gate, gpost, target=None, rider=None):
    S = x.shape[0]
    TM = min(512, S)
    KB = w_out_l.shape[1]
    last = target is not None

    def body(*refs):
        if last:
            yr_ref, yp_ref, w_ref, x_ref, gate_ref, gp_ref, t_ref, y_ref, xo_ref, loss_ref = refs
        else:
            yr_ref, yp_ref, w_ref, x_ref, gate_ref, gp_ref, y_ref, xo_ref = refs
        acc = jnp.zeros((TM, D), F32)
        for j in range(NDEV):
            src = yr_ref if j < NDEV // 2 else yp_ref
            k0 = (j % (NDEV // 2)) * KB
            acc = acc + jnp.dot(src[:, k0:k0 + KB], w_ref[j], preferred_element_type=F32)
        y_ref[...] = acc
        rstd = lax.rsqrt(jnp.mean(acc * acc, axis=-1, keepdims=True) + NORM_EPS)
        xn = x_ref[...] + gate_ref[...] * (acc * rstd * gp_ref[...])
        if last:
            err = xn - t_ref[...]
            xo_ref[...] = err * (1.0 / D)

            @pl.when(pl.program_id(0) == 0)
            def _():
                loss_ref[...] = jnp.zeros((SUB, D), F32)
            loss_ref[...] += _rowsum8(err * err)
        else:
            xo_ref[...] = xn

    row = pl.BlockSpec((TM, D), lambda i: (i, 0))
    in_specs = [row, row, _full((NDEV, KB, D)), row, _vec(), _vec()]
    out_specs = [row, row]
    out_shape = [jax.ShapeDtypeStruct((S, D), F32), jax.ShapeDtypeStruct((S, D), F32)]
    args = [yr, yp, w_out_l, x, gate, gpost]
    if last:
        in_specs.append(row)
        out_specs.append(_full((SUB, D)))
        out_shape.append(jax.ShapeDtypeStruct((SUB, D), F32))
        args.append(target)
    return _pcall(body, name="out_post_loss" if last else "out_post", grid=(S // TM,), in_specs=in_specs,
                  out_specs=out_specs, out_shape=out_shape, args=args, rider=rider)


def _out_bwd(dxo, y, yr, yp, w_out_l, gate, gpost, rider=None):
    S = y.shape[0]
    TM = min(256, S)
    KB = w_out_l.shape[1]
    nsteps = S // TM

    def body(dxo_ref, y_ref, yr_ref, yp_ref, w_ref, gate_ref, gp_ref, dyr_ref, dyp_ref, gw_ref, dgate_ref,
             dgp_ref, gw_acc, vacc):
        i = pl.program_id(0)

        @pl.when(i == 0)
        def _():
            gw_acc[...] = jnp.zeros_like(gw_acc)
            vacc[...] = jnp.zeros_like(vacc)
        yv = y_ref[...]
        dxo_v = dxo_ref[...]
        rstd = lax.rsqrt(jnp.mean(yv * yv, axis=-1, keepdims=True) + NORM_EPS)
        n = yv * rstd
        gp = gp_ref[...]
        vacc[0] += _rowsum8(dxo_v * (n * gp))
        drn = dxo_v * gate_ref[...]
        vacc[1] += _rowsum8(drn * n)
        dn = drn * gp
        dy = (rstd * (dn - n * jnp.mean(dn * n, axis=-1, keepdims=True))).astype(BF16)
        for j in range(NDEV):
            dst = dyr_ref if j < NDEV // 2 else dyp_ref
            k0 = (j % (NDEV // 2)) * KB
            dst[:, k0:k0 + KB] = lax.dot_general(dy, w_ref[j], (((1,), (1,)), ((), ())),
                                                 preferred_element_type=F32)
        gw_acc[0:D, :] += lax.dot_general(yr_ref[...], dy, (((0,), (0,)), ((), ())), preferred_element_type=F32)
        gw_acc[D:2 * D, :] += lax.dot_general(yp_ref[...], dy, (((0,), (0,)), ((), ())),
                                              preferred_element_type=F32)

        @pl.when(i == nsteps - 1)
        def _():
            gw_ref[...] = gw_acc[...].astype(BF16)
            dgate_ref[...] = _sum8(vacc[0])
            dgp_ref[...] = _sum8(vacc[1])

    row = pl.BlockSpec((TM, D), lambda i: (i, 0))
    return _pcall(
        body, name="out_bwd", grid=(nsteps,),
        in_specs=[row, row, row, row, _full((NDEV, KB, D)), _vec(), _vec()],
        out_specs=[row, row, _full((2 * D, D)), _vec(), _vec()],
        out_shape=[jax.ShapeDtypeStruct((S, D), F32), jax.ShapeDtypeStruct((S, D), F32),
                   jax.ShapeDtypeStruct((2 * D, D), BF16), jax.ShapeDtypeStruct((1, D), F32),
                   jax.ShapeDtypeStruct((1, D), F32)],
        scratch_shapes=[pltpu.VMEM((2 * D, D), F32), pltpu.VMEM((2, SUB, D), F32)],
        args=(dxo, y, yr, yp, w_out_l, gate, gpost), rider=rider)


def _rnn_bwd(dyr, hs, fwd, proj, conv_w, wa, wx, lam, rider=None):
    S = proj.shape[0]
    TB = min(256, S)
    nb = S // TB
    TE = TB + CONV_HALO
    A_BA, A_BX, A_LAM, A_CB, A_CW = 0, 1, 2, 3, 4

    def blk(i):
        return nb - 1 - i

    def prev8(i):
        return jnp.maximum(blk(i) * (TB // SUB) - 1, 0)

    def body(dyr_ref, hs_ref, hprev_ref, u_ref, r_ref, i_ref, a_ref, m_ref, xr_ref, xprev_ref, g_ref, cw_ref,
             wa_ref, wx_ref, lam_ref, dxr_ref, dg_ref, gcw_ref, gcb_ref, gwa_ref, gba_ref, gwx_ref, gbx_ref, glam_ref,
             xbuf, hbuf, abuf, dbuf, sa, sv, dh_ref, dhc, vacc):
        i = pl.program_id(0)
        first = blk(i) == 0

        @pl.when(i == 0)
        def _():
            abuf[TB:, :] = jnp.zeros((CONV_HALO, D), F32)
            dbuf[TB:, :] = jnp.zeros((CONV_HALO, D), F32)
            dhc[...] = jnp.zeros_like(dhc)
            vacc[...] = jnp.zeros_like(vacc)
            gwa_ref[...] = jnp.zeros_like(gwa_ref)
            gwx_ref[...] = jnp.zeros_like(gwx_ref)

        xbuf[0:CONV_HALO, :] = jnp.where(first, 0.0, xprev_ref[...])
        xbuf[CONV_HALO:, :] = xr_ref[...]
        taps = _taps(xbuf[...])
        u, r, ig, a, mult = u_ref[...], r_ref[...], i_ref[...], a_ref[...], m_ref[...]
        sp = jax.nn.softplus(-lam_ref[...])
        hbuf[0:CONV_HALO, :] = jnp.where(first, 0.0, hprev_ref[...])
        hbuf[CONV_HALO:, :] = hs_ref[...]
        hprev = pltpu.roll(hbuf[...], 1, axis=0)[CONV_HALO:, :]

        silu, dsilu = _silu_parts(g_ref[...])
        dyv = dyr_ref[...]
        dg_ref[...] = (dyv * hs_ref[...] * dsilu).astype(BF16)

        abuf[0:TB, :] = a
        b = pltpu.roll(abuf[...], TE - 1, axis=0)[0:TB, :]
        _scan(b, dyv * silu, sa, sv, dhc, dh_ref, reverse=True)
        abuf[TB:, :] = a[0:CONV_HALO, :]
        dh = dh_ref[...]

        dlog_a = dh * hprev * a - (dh * ig * u) * (a * a) / mult
        vacc[A_LAM] += _rowsum8(dlog_a * r)
        dzr = dlog_a * (-LRU_C * sp) * r * (1.0 - r)
        dzi = (dh * mult * u) * ig * (1.0 - ig)
        vacc[A_BA] += _rowsum8(dzr)
        vacc[A_BX] += _rowsum8(dzi)
        ub, dzrb, dzib = u.astype(BF16), dzr.astype(BF16), dzi.astype(BF16)
        du_g = []
        for h in range(NHEAD):
            cs = slice(h * HD, (h + 1) * HD)
            gwa_ref[h] += lax.dot_general(ub[:, cs], dzrb[:, cs], (((0,), (0,)), ((), ())),
                                          preferred_element_type=F32)
            gwx_ref[h] += lax.dot_general(ub[:, cs], dzib[:, cs], (((0,), (0,)), ((), ())),
                                          preferred_element_type=F32)
            du_g.append(lax.dot_general(dzrb[:, cs], wa_ref[h], (((1,), (1,)), ((), ())),
                                        preferred_element_type=F32)
                        + lax.dot_general(dzib[:, cs], wx_ref[h], (((1,), (1,)), ((), ())),
                                          preferred_element_type=F32))
        du = dh * mult * ig + jnp.concatenate(du_g, axis=1)

        dbuf[0:TB, :] = du
        Dd = dbuf[...]
        w = cw_ref[...]
        dx = Dd * w[3:4, :]
        for k in range(CONV_K - 1):
            dx = dx + pltpu.roll(Dd, TE - (CONV_K - 1 - k), axis=0) * w[k:k + 1, :]
        dxr_ref[...] = dx[0:TB, :].astype(BF16)
        dbuf[TB:, :] = du[0:CONV_HALO, :]
        vacc[A_CB] += _rowsum8(du)
        for k in range(CONV_K):
            vacc[A_CW + k] += _rowsum8(du * taps[k])

        @pl.when(i == nb - 1)
        def _():
            gba_ref[...] = _sum8(vacc[A_BA])
            gbx_ref[...] = _sum8(vacc[A_BX])
            glam_ref[...] = _sum8(vacc[A_LAM]) * (LRU_C * jax.nn.sigmoid(-lam_ref[...]))
            gcb_ref[...] = _sum8(vacc[A_CB])
            for k in range(CONV_K):
                gcw_ref[k:k + 1, :] = _sum8(vacc[A_CW + k])

    rowb = pl.BlockSpec((TB, D), lambda i: (blk(i), 0))
    halo = pl.BlockSpec((SUB, D), lambda i: (prev8(i), 0))
    wspec = _full((NHEAD, HD, HD))
    return _pcall(
        body, name="rnn_bwd", grid=(nb,),
        in_specs=[rowb, rowb, halo] + [rowb] * 5 + [rowb, halo, pl.BlockSpec((TB, D), lambda i: (blk(i), 1)),
                                                    _full((CONV_K, D)), wspec, wspec, _vec()],
        out_specs=[rowb, rowb, _full((CONV_K, D)), _vec(), wspec, _vec(), wspec, _vec(), _vec()],
        out_shape=[jax.ShapeDtypeStruct((S, D), BF16), jax.ShapeDtypeStruct((S, D), BF16),
                   jax.ShapeDtypeStruct((CONV_K, D), F32), jax.ShapeDtypeStruct((1, D), F32),
                   jax.ShapeDtypeStruct((NHEAD, HD, HD), F32), jax.ShapeDtypeStruct((1, D), F32),
                   jax.ShapeDtypeStruct((NHEAD, HD, HD), F32), jax.ShapeDtypeStruct((1, D), F32),
                   jax.ShapeDtypeStruct((1, D), F32)],
        scratch_shapes=[pltpu.VMEM((TE, D), F32), pltpu.VMEM((TE, D), F32), pltpu.VMEM((TE, D), F32),
                        pltpu.VMEM((TE, D), F32), pltpu.VMEM((D // LANE, TB, LANE), F32),
                        pltpu.VMEM((D // LANE, TB, LANE), F32), pltpu.VMEM((TB, D), F32), pltpu.VMEM((SUB, D), F32),
                        pltpu.VMEM((A_CW + CONV_K, SUB, D), F32)],
        args=(dyr, hs, hs, *fwd, proj, proj, proj, conv_w, wa, wx, lam), rider=rider)


def _pool_bwd(dyp, proj, pw, pb, ps, rider=None):
    S = proj.shape[0]
    TB = min(256, S)
    nb = S // TB
    TE = TB + POOL_HALO

    def blk(i):
        return nb - 1 - i

    def body(dy_ref, xp_ref, xprev_ref, g_ref, pw_ref, pb_ref, ps_ref, dxp_ref, dg_ref, gpw_ref, gpb_ref,
             gps_ref, ebuf, qbuf, vacc):
        i = pl.program_id(0)
        first = blk(i) == 0

        @pl.when(i == 0)
        def _():
            qbuf[TB:, :] = jnp.zeros((POOL_HALO, D), F32)
            vacc[...] = jnp.zeros_like(vacc)
            gpw_ref[...] = jnp.zeros_like(gpw_ref)

        ebuf[0:POOL_HALO, :] = jnp.where(first, 0.0, xprev_ref[...])
        ebuf[POOL_HALO:, :] = xp_ref[...]
        pooled, inv = _pooled(ebuf, blk(i) * TB, TB)
        pooled = [p.astype(BF16) for p in pooled]
        yp = jnp.concatenate([jnp.dot(pooled[g], pw_ref[g], preferred_element_type=F32)
                              for g in range(NGRP)], axis=1) + pb_ref[...]
        silu, dsilu = _silu_parts(g_ref[...])
        dy = dy_ref[...]
        ps = ps_ref[...]
        dyp_v = dy * ps * silu
        vacc[0] += _rowsum8(dy * yp * silu)
        vacc[1] += _rowsum8(dyp_v)
        dg_ref[...] = (dy * yp * ps * dsilu).astype(BF16)
        dypb = dyp_v.astype(BF16)
        for g in range(NGRP):
            cs = slice(g * GD, (g + 1) * GD)
            gpw_ref[g] += lax.dot_general(pooled[g], dypb[:, cs], (((0,), (0,)), ((), ())),
                                          preferred_element_type=F32)
            dpool = lax.dot_general(dypb[:, cs], pw_ref[g], (((1,), (1,)), ((), ())),
                                    preferred_element_type=F32)
            qbuf[0:TB, cs] = dpool * inv[g]
            L = qbuf[:, cs]
            for lev in range(g + 1):
                L = L + pltpu.roll(L, TE - (1 << lev), axis=0)
            dxp_ref[:, cs] = (L[0:TB, :] - dpool).astype(BF16)
        qbuf[TB:, :] = qbuf[0:POOL_HALO, :]

        @pl.when(i == nb - 1)
        def _():
            gps_ref[...] = _sum8(vacc[0])
            gpb_ref[...] = _sum8(vacc[1])

    rowb = pl.BlockSpec((TB, D), lambda i: (blk(i), 0))
    return _pcall(
        body, name="pool_bwd", grid=(nb,),
        in_specs=[rowb, pl.BlockSpec((TB, D), lambda i: (blk(i), 2)),
                  pl.BlockSpec((POOL_HALO, D), lambda i: (jnp.maximum(blk(i) * (TB // POOL_HALO) - 1, 0), 2)),
                  pl.BlockSpec((TB, D), lambda i: (blk(i), 3)), _full((NGRP, GD, GD)), _vec(), _vec()],
        out_specs=[rowb, rowb, _full((NGRP, GD, GD)), _vec(), _vec()],
        out_shape=[jax.ShapeDtypeStruct((S, D), BF16), jax.ShapeDtypeStruct((S, D), BF16),
                   jax.ShapeDtypeStruct((NGRP, GD, GD), F32), jax.ShapeDtypeStruct((1, D), F32),
                   jax.ShapeDtypeStruct((1, D), F32)],
        scratch_shapes=[pltpu.VMEM((TE, D), F32), pltpu.VMEM((TE, D), F32), pltpu.VMEM((2, SUB, D), F32)],
        args=(dyp, proj, proj, proj, pw, pb, ps), rider=rider)


def _in_bwd(dq, w_in_l, x, dxo, gpre, scale1, rider=None):
    S = x.shape[0]
    TM = min(256, S)
    NB = w_in_l.shape[2]
    nsteps = S // TM
    per_q = D // NB

    def body(d0, d1, d2, d3, w_ref, x_ref, dxo_ref, g_ref, sc_ref, dx_ref, dsh_ref, dsc_ref, dg_ref, vacc):
        i = pl.program_id(0)

        @pl.when(i == 0)
        def _():
            vacc[...] = jnp.zeros_like(vacc)
        dref = (d0, d1, d2, d3)
        dh = jnp.zeros((TM, D), F32)
        for j in range(NDEV):
            c0 = (j % per_q) * NB
            dh = dh + lax.dot_general(dref[j // per_q][:, c0:c0 + NB], w_ref[j], (((1,), (1,)), ((), ())),
                                      preferred_element_type=F32)
        xv = x_ref[...]
        rstd = lax.rsqrt(jnp.mean(xv * xv, axis=-1, keepdims=True) + NORM_EPS)
        xn = xv * rstd
        g, sc = g_ref[...], sc_ref[...]
        vacc[0] += _rowsum8(dh)
        vacc[1] += _rowsum8(dh * (xn * g))
        vacc[2] += _rowsum8(dh * sc * xn)
        dxn = dh * sc * g
        dx_ref[...] = dxo_ref[...] + rstd * (dxn - xn * jnp.mean(dxn * xn, axis=-1, keepdims=True))

        @pl.when(i == nsteps - 1)
        def _():
            dsh_ref[...] = _sum8(vacc[0])
            dsc_ref[...] = _sum8(vacc[1])
            dg_ref[...] = _sum8(vacc[2])

    row = pl.BlockSpec((TM, D), lambda i: (i, 0))
    return _pcall(
        body, name="in_bwd", grid=(nsteps,),
        in_specs=[row, row, row, row, _full((NDEV, D, NB)), row, row, _vec(), _vec()],
        out_specs=[row, _vec(), _vec(), _vec()],
        out_shape=[jax.ShapeDtypeStruct((S, D), F32)] + [jax.ShapeDtypeStruct((1, D), F32)] * 3,
        scratch_shapes=[pltpu.VMEM((3, SUB, D), F32)],
        args=(*dq, w_in_l, x, dxo, gpre, scale1), rider=rider)


def _grad_w_in(h, dq, NB, part, rider=None):
    S = h.shape[0]
    TK = min(512, S)
    nk = S // TK
    RH = D // 2

    def body(h_ref, d0, d1, d2, d3, o_ref, acc):
        k = pl.program_id(0)

        @pl.when(k == 0)
        def _():
            acc[...] = jnp.zeros_like(acc)
        hv = h_ref[...]
        for q, d_ref in enumerate((d0, d1, d2, d3)):
            acc[:, q * D:(q + 1) * D] += lax.dot_general(hv, d_ref[...], (((0,), (0,)), ((), ())),
                                                         preferred_element_type=F32)

        @pl.when(k == nk - 1)
        def _():
            for j in range(NDEV):
                o_ref[j] = acc[:, j * NB:(j + 1) * NB].astype(BF16)

    row = pl.BlockSpec((TK, D), lambda k: (k, 0))
    return _pcall(
        body, name="grad_w_in", grid=(nk,),
        in_specs=[pl.BlockSpec((TK, RH), lambda k: (k, part)), row, row, row, row],
        out_specs=[_full((NDEV, RH, NB))],
        out_shape=[jax.ShapeDtypeStruct((NDEV, RH, NB), BF16)],
        scratch_shapes=[pltpu.VMEM((RH, NQ * D), F32)],
        args=(h, *dq), rider=rider)


def _adamw_math(g, w, m, v):
    m2 = ADAM_B1 * m + (1.0 - ADAM_B1) * g
    v2 = ADAM_B2 * v + (1.0 - ADAM_B2) * (g * g)
    m_hat = m2 / (1.0 - ADAM_B1 ** ADAM_STEP)
    v_hat = v2 / (1.0 - ADAM_B2 ** ADAM_STEP)
    delta = -ADAM_LR * (m_hat / (jnp.sqrt(v_hat) + ADAM_EPS) + ADAM_WD * w)
    return delta, m2, v2


def _adamw(name, gs, w, m, v, TR, rider=None):
    L = len(gs)
    n, R, C = gs[0].shape

    def body(*refs):
        g_refs = refs[:L]
        w_ref, m_ref, v_ref, go_ref, do_ref, mo_ref, vo_ref = refs[L:]
        lay = pl.program_id(0)
        for li in range(L):
            @pl.when(lay == li)
            def _(li=li):
                g = g_refs[li][0].astype(F32)
                for s in range(1, n):
                    g = g + g_refs[li][s].astype(F32)
                delta, m2, v2 = _adamw_math(g, w_ref[...], m_ref[...], v_ref[...])
                go_ref[...] = g
                do_ref[...] = delta
                mo_ref[...] = m2
                vo_ref[...] = v2

    lrc = pl.BlockSpec((None, TR, C), lambda lay, r: (lay, r, 0))
    g_specs = [pl.BlockSpec((n, TR, C), lambda lay, r, li=li: (0, jnp.where(lay == li, r, 0), 0))
               for li in range(L)]
    return _pcall(
        body, name=name, grid=(L, R // TR),
        in_specs=g_specs + [lrc, lrc, lrc], out_specs=[lrc] * 4,
        out_shape=[jax.ShapeDtypeStruct((L, R, C), F32)] * 4,
        args=(*gs, w, m, v), rider=rider)


def _ada_adamw(c_all_t, dm, w, m, v, rider=None):
    L, _, nc = w.shape

    def body(c_ref, dm_ref, w_ref, m_ref, v_ref, go_ref, do_ref, mo_ref, vo_ref):
        cv = c_ref[...]
        ca = cv * jax.nn.sigmoid(cv)
        dmv = dm_ref[...]
        g = ca[:, 0:1] * dmv[0:1, :]
        for b in range(1, NDEV):
            g = g + ca[:, b:b + 1] * dmv[b:b + 1, :]
        delta, m2, v2 = _adamw_math(g, w_ref[...], m_ref[...], v_ref[...])
        go_ref[...] = g
        do_ref[...] = delta
        mo_ref[...] = m2
        vo_ref[...] = v2

    big = pl.BlockSpec((None, D, nc), lambda lay: (lay, 0, 0))
    return _pcall(
        body, name="ada_adamw", grid=(L,),
        in_specs=[_full((D, NDEV)), pl.BlockSpec((None, NDEV, nc), lambda lay: (lay, 0, 0)), big, big, big],
        out_specs=[big] * 4, out_shape=[jax.ShapeDtypeStruct((L, D, nc), F32)] * 4,
        args=(c_all_t, dm, w, m, v), rider=rider)


def _sum_slots(recv):
    n, R, C = recv.shape

    def body(r_ref, o_ref):
        acc = r_ref[0].astype(F32)
        for s in range(1, n):
            acc = acc + r_ref[s].astype(F32)
        o_ref[...] = acc

    return pl.pallas_call(body, name="sum_slots", out_shape=jax.ShapeDtypeStruct((R, C), F32),
                          compiler_params=_cparams())(recv)


def _pad_rows(a, rows):
    return jnp.pad(a, ((0, rows - a.shape[0]), (0, 0)))


def _pack_sharded_block(pool_w, pool_b, conv_w):
    return jnp.concatenate([pool_w.reshape(-1, PACK_C), _pad_rows(pool_b.reshape(-1, PACK_C), SUB),
                            _pad_rows(conv_w.reshape(-1, PACK_C), SUB)], axis=0)


def _unpack_sharded_block(p):
    n_pw = DEPTH * NGRP * (GD // NDEV)
    pool_w = p[:n_pw].reshape(DEPTH, NGRP, GD // NDEV, GD)
    pool_b = p[n_pw].reshape(DEPTH, NGRP, GD // NDEV)
    conv_w = p[n_pw + SUB:n_pw + SUB + DEPTH * CONV_K * (D // NDEV) // PACK_C].reshape(DEPTH, CONV_K, D // NDEV)
    return pool_w, pool_b, conv_w


def _blocks_of_full(pool_w, pool_b, conv_w):
    pw = pool_w.reshape(DEPTH, NGRP, NDEV, GD // NDEV, GD).transpose(2, 0, 1, 3, 4).reshape(NDEV, -1, PACK_C)
    pb = pool_b.reshape(DEPTH, NGRP, NDEV, GD // NDEV).transpose(2, 0, 1, 3).reshape(NDEV, -1, PACK_C)
    cw = conv_w.reshape(DEPTH, CONV_K, NDEV, D // NDEV).transpose(2, 0, 1, 3).reshape(NDEV, -1, PACK_C)
    pad = lambda a: jnp.pad(a, ((0, 0), (0, SUB - a.shape[1]), (0, 0)))
    return jnp.concatenate([pw, pad(pb), pad(cw)], axis=1)


def _full_of_blocks(p):
    n_pw = DEPTH * NGRP * (GD // NDEV)
    pool_w = p[:, :n_pw].reshape(NDEV, DEPTH, NGRP, GD // NDEV, GD).transpose(1, 2, 0, 3, 4)
    pool_b = p[:, n_pw].reshape(NDEV, DEPTH, NGRP, GD // NDEV).transpose(1, 2, 0, 3)
    n_cw = DEPTH * CONV_K * (D // NDEV) // PACK_C
    conv_w = p[:, n_pw + SUB:n_pw + SUB + n_cw].reshape(NDEV, DEPTH, CONV_K, D // NDEV).transpose(1, 2, 0, 3)
    return (pool_w.reshape(DEPTH, NGRP, GD, GD), pool_b.reshape(DEPTH, NGRP, GD),
            conv_w.reshape(DEPTH, CONV_K, D))


def _pack_replicated(t, keys, rows):
    p = jnp.concatenate([t[k].reshape(-1, PACK_C) for k in keys], axis=0)
    return _pad_rows(p, rows)


def _unpack_replicated(p, like, keys):
    out, r0 = {}, 0
    for k in keys:
        rows = like[k].size // PACK_C
        out[k] = p[r0:r0 + rows].reshape(like[k].shape)
        r0 += rows
    return out


def kernel(x, c, ada_w, ada_b, pre_norm_g, w_in, conv_w, conv_b, gate_a_w, gate_a_b, gate_x_w, gate_x_b, lru_lambda, pool_w, pool_b, pool_scale, w_out, post_norm_g, loss_target, m_ada_w, m_ada_b, m_pre_norm_g, m_w_in, m_conv_w, m_conv_b, m_gate_a_w, m_gate_a_b, m_gate_x_w, m_gate_x_b, m_lru_lambda, m_pool_w, m_pool_b, m_pool_scale, m_w_out, m_post_norm_g, v_ada_w, v_ada_b, v_pre_norm_g, v_w_in, v_conv_w, v_conv_b, v_gate_a_w, v_gate_a_b, v_gate_x_w, v_gate_x_b, v_lru_lambda, v_pool_w, v_pool_b, v_pool_scale, v_w_out, v_post_norm_g):
    W = dict(ada_w=ada_w, ada_b=ada_b, pre_norm_g=pre_norm_g, w_in=w_in, conv_w=conv_w, conv_b=conv_b,
             gate_a_w=gate_a_w, gate_a_b=gate_a_b, gate_x_w=gate_x_w, gate_x_b=gate_x_b, lru_lambda=lru_lambda,
             pool_w=pool_w, pool_b=pool_b, pool_scale=pool_scale, w_out=w_out, post_norm_g=post_norm_g)
    M = dict(ada_w=m_ada_w, ada_b=m_ada_b, pre_norm_g=m_pre_norm_g, w_in=m_w_in, conv_w=m_conv_w,
             conv_b=m_conv_b, gate_a_w=m_gate_a_w, gate_a_b=m_gate_a_b, gate_x_w=m_gate_x_w,
             gate_x_b=m_gate_x_b, lru_lambda=m_lru_lambda, pool_w=m_pool_w, pool_b=m_pool_b,
             pool_scale=m_pool_scale, w_out=m_w_out, post_norm_g=m_post_norm_g)
    V = dict(ada_w=v_ada_w, ada_b=v_ada_b, pre_norm_g=v_pre_norm_g, w_in=v_w_in, conv_w=v_conv_w,
             conv_b=v_conv_b, gate_a_w=v_gate_a_w, gate_a_b=v_gate_a_b, gate_x_w=v_gate_x_w,
             gate_x_b=v_gate_x_b, lru_lambda=v_lru_lambda, pool_w=v_pool_w, pool_b=v_pool_b,
             pool_scale=v_pool_scale, w_out=v_w_out, post_norm_g=v_post_norm_g)
    S = x.shape[1]
    me = 4 * lax.axis_index("x") + 2 * lax.axis_index("y") + lax.axis_index("c")
    xs = x.reshape(S, D)
    tgt = loss_target.reshape(S, D)
    nc = ada_w.shape[2]
    NB = w_in.shape[2]
    vec = lambda a: a.reshape(1, D)
    w_in_b, w_out_b = w_in.astype(BF16), w_out.astype(BF16)

    c_slots, w_in0 = _exchange("gather_c_w_in0", _AllGather2([jnp.broadcast_to(c, (SUB, D)), w_in_b[0]]))
    c_all = c_slots[:, 0, :]
    (mod_slots,) = _exchange("gather_mod", _Direct(ag=[_mod_cols(c_all, ada_w)]))
    mod = lax.dynamic_index_in_dim(mod_slots, me, axis=1, keepdims=False)
    mod = mod.reshape(NDEV, DEPTH, nc).transpose(1, 0, 2).reshape(DEPTH, 3 * D) + ada_b
    mods = [(vec(mod[l, :D]), vec(1.0 + mod[l, D:2 * D]), vec(mod[l, 2 * D:])) for l in range(DEPTH)]

    w_in_all, w_out_all = [w_in0, None], [None, None]
    saved = []
    xl = xs
    for l in range(DEPTH):
        shift, scale1, gate = mods[l]
        wa, wx = gate_a_w[l].astype(BF16), gate_x_w[l].astype(BF16)
        rider = _AllGather2([_pack_sharded_block(pool_w, pool_b, conv_w), w_out_b[0]]) if l == 0 else None
        (h, proj), got = _pre_proj(xl, vec(pre_norm_g[l]), scale1, shift, w_in_all[l], rider=rider)
        if l == 0:
            pool_w_f, pool_b_f, conv_w_f = _full_of_blocks(got[0])
            w_out_all[0] = got[1]
        pw = pool_w_f[l].astype(BF16)
        rider = _AllGather2([w_in_b[1]]) if l == 0 else None
        (hs, yr, *fwd), got = _rnn_fwd(proj, conv_w_f[l], vec(conv_b[l]), wa, gate_a_b[l].reshape(1, D), wx,
                                 gate_x_b[l].reshape(1, D), vec(lru_lambda[l]), rider=rider)
        if l == 0:
            w_in_all[1] = got[0]
        (yp,), _ = _pool_fwd(proj, pw, pool_b_f[l].reshape(1, D), vec(pool_scale[l]))
        if l == DEPTH - 1:
            (y, x_next, loss_acc), _ = _out_post(yr, yp, w_out_all[l], xl, gate, vec(post_norm_g[l]), tgt)
        else:
            (y, x_next), (w_out_all[1],) = _out_post(yr, yp, w_out_all[l], xl, gate, vec(post_norm_g[l]),
                                                     rider=_AllGather2([w_out_b[1]]))
        saved.append((xl, h, proj, hs, fwd, yr, yp, y, wa, wx, pw))
        xl = x_next

    dxo = xl
    G = {k: [None] * DEPTH for k in WEIGHTS}
    dmod = [None] * DEPTH
    recv_in, recv_out = [[None, None] for _ in range(DEPTH)], [None] * DEPTH
    full = dict(conv_w=(CONV_K, D), pool_w=(NGRP, GD, GD), pool_b=(NGRP, GD))
    stack = lambda k: jnp.stack([g.reshape(full.get(k, W[k].shape[1:])) for g in G[k]])
    gw_bot_prev = None
    for l in reversed(range(DEPTH)):
        xin, h, proj, hs, fwd, yr, yp, y, wa, wx, pw = saved[l]
        shift, scale1, gate = mods[l]
        rider = _AllGather2([loss_acc]) if gw_bot_prev is None else _Direct(a2a=[gw_bot_prev])
        (dyr, dyp, gw_out, dgate, G['post_norm_g'][l]), got = _out_bwd(dxo, y, yr, yp, w_out_all[l], gate,
                                                                       vec(post_norm_g[l]), rider=rider)
        if gw_bot_prev is None:
            loss = (0.5 / D) * jnp.sum(got[0])
        else:
            recv_in[l + 1][1] = got[0]
        ((dxr, dgr, G['conv_w'][l], G['conv_b'][l], G['gate_a_w'][l], G['gate_a_b'][l], G['gate_x_w'][l],
          G['gate_x_b'][l], G['lru_lambda'][l]), (recv_out[l],)) = _rnn_bwd(
            dyr, hs, fwd, proj, conv_w_f[l], wa, wx, vec(lru_lambda[l]),
            rider=_Direct(a2a=[gw_out.reshape(NDEV, 2 * D // NDEV, D)]))
        (dxp, dgp, G['pool_w'][l], G['pool_b'][l], G['pool_scale'][l]), _ = _pool_bwd(
            dyp, proj, pw, pool_b_f[l].reshape(1, D), vec(pool_scale[l]))
        dq = (dxr, dgr, dxp, dgp)
        if l > 0:
            (gw_top,), _ = _grad_w_in(h, dq, NB, 0)
            (dxo, dshift, dscale, G['pre_norm_g'][l]), (recv_in[l][0],) = _in_bwd(
                dq, w_in_all[l], xin, dxo, vec(pre_norm_g[l]), scale1, rider=_Direct(a2a=[gw_top]))
            (gw_bot_prev,), _ = _grad_w_in(h, dq, NB, 1)
        else:
            Ge = {k: stack(k) for k in REP_EARLY + ['pool_w', 'pool_b', 'conv_w']}
            early = jnp.concatenate([_blocks_of_full(Ge['pool_w'], Ge['pool_b'], Ge['conv_w']),
                                     _pack_replicated(Ge, REP_EARLY, NDEV * PACK_ROWS).reshape(NDEV, PACK_ROWS, PACK_C)],
                                    axis=1).astype(BF16)
            (gw_top,), (early_recv,) = _grad_w_in(h, dq, NB, 0, rider=_Direct(a2a=[early]))
            early_sum = _sum_slots(early_recv)
            (gw_bot,), (recv_in[l][0],) = _grad_w_in(h, dq, NB, 1, rider=_Direct(a2a=[gw_top]))
            (dxo, dshift, dscale, G['pre_norm_g'][l]), (recv_in[l][1], early_all) = _in_bwd(
                dq, w_in_all[l], xin, dxo, vec(pre_norm_g[l]), scale1,
                rider=_Both(_Direct(a2a=[gw_bot]), _AllGather2([early_sum[PACK_ROWS:]])))
        dmod[l] = jnp.concatenate([dshift, dscale, dgate], axis=1)
    grad_x = dxo.reshape(x.shape)

    Gl = dict(ada_b=jnp.concatenate(dmod, axis=0), pre_norm_g=stack('pre_norm_g'))
    (late_slots,) = _exchange("gather_late", _AllGather2([_pack_replicated(Gl, REP_LATE, LATE_PACK_ROWS)]))
    late_sum = _sum_slots(late_slots)
    dmod_all = late_slots[:, :DEPTH * 3 * D // PACK_C]

    out = {}
    halves = lambda a: a.reshape(2 * DEPTH, D // 2, NB)
    out['w_in'], _ = _adamw("adamw_w_in", [r for l in range(DEPTH) for r in recv_in[l]], halves(w_in),
                            halves(M['w_in']), halves(V['w_in']), 256)
    out['w_out'], _ = _adamw("adamw_w_out", recv_out, w_out, M['w_out'], V['w_out'], 256)
    dm = lax.dynamic_slice_in_dim(dmod_all.reshape(NDEV, DEPTH, 3 * D), me * nc, nc, axis=2)
    out['ada_w'], _ = _ada_adamw(c_all.T, dm.transpose(1, 0, 2), ada_w, M['ada_w'], V['ada_w'])
    g_small = jnp.concatenate([early_sum[:PACK_ROWS], early_all.reshape(NDEV * PACK_ROWS, PACK_C), late_sum],
                              axis=0)

    def packs(T):
        return jnp.concatenate([_pack_sharded_block(T['pool_w'], T['pool_b'], T['conv_w']),
                                _pack_replicated(T, REP_EARLY, NDEV * PACK_ROWS),
                                _pack_replicated(T, REP_LATE, LATE_PACK_ROWS)], axis=0)[None]
    res_small, _ = _adamw("adamw_small", [g_small[None]], packs(W), packs(M), packs(V), g_small.shape[0] // 2)
    n_early = (1 + NDEV) * PACK_ROWS
    for idx in range(4):
        p = res_small[idx][0]
        pw_, pb_, cw_ = _unpack_sharded_block(p[:PACK_ROWS])
        rep = _unpack_replicated(p[PACK_ROWS:n_early], W, REP_EARLY)
        rep.update(_unpack_replicated(p[n_early:], W, REP_LATE))
        rep.update(pool_w=pw_, pool_b=pb_, conv_w=cw_)
        for k, a in rep.items():
            out.setdefault(k, [None] * 4)[idx] = a
    for k in ('w_in', 'w_out', 'ada_w'):
        out[k] = [a.reshape(W[k].shape) for a in out[k]]

    return (loss, grad_x, *[out[k][0] for k in WEIGHTS], *[out[k][1] for k in WEIGHTS],
            *[out[k][2] for k in WEIGHTS], *[out[k][3] for k in WEIGHTS])
```

```python
import functools

import jax
import jax.numpy as jnp
from jax import lax
from jax.experimental import pallas as pl
from jax.experimental.pallas import tpu as pltpu

F32, BF16 = jnp.float32, jnp.bfloat16
MESH = pl.DeviceIdType.MESH
HIGHEST = lax.Precision.HIGHEST

NDEV = 8
DEPTH = 2
D = 1024
NHEAD, HD = 8, 128
NGRP, GD = 4, 256
WINS = (2, 4, 8, 16)
CONV_K = 4
CONV_HALO = 8
POOL_HALO = 16
LRU_C = 8.0
NORM_EPS = 1e-6
ADAM_LR, ADAM_B1, ADAM_B2, ADAM_EPS, ADAM_WD, ADAM_STEP = 0.001, 0.9, 0.999, 1e-08, 0.01, 10
VMEM_LIMIT = 56 * 1024 * 1024
NQ = 4
SUB = 8
LANE = 128
PACK_C = 256
PACK_ROWS = 272

WEIGHTS = ['ada_w', 'ada_b', 'pre_norm_g', 'w_in', 'conv_w', 'conv_b', 'gate_a_w', 'gate_a_b', 'gate_x_w',
           'gate_x_b', 'lru_lambda', 'pool_w', 'pool_b', 'pool_scale', 'w_out', 'post_norm_g']
REP_EARLY = ['conv_b', 'gate_a_w', 'gate_a_b', 'gate_x_w', 'gate_x_b', 'lru_lambda', 'pool_scale', 'post_norm_g']
REP_LATE = ['ada_b', 'pre_norm_g']
LATE_PACK_ROWS = 32


def _cparams(*sem):
    return pltpu.CompilerParams(dimension_semantics=sem, vmem_limit_bytes=VMEM_LIMIT)


def _vec(n=D):
    return pl.BlockSpec((1, n), lambda *_: (0, 0))


def _full(shape):
    nd = len(shape)
    return pl.BlockSpec(shape, lambda *_: (0,) * nd)


def _rowsum8(z):
    return z.reshape(z.shape[0] // SUB, SUB, z.shape[1]).sum(axis=0)


def _sum8(acc):
    return jnp.sum(acc, axis=0, keepdims=True)


def _sigmoid(z):
    return 0.5 * jnp.tanh(0.5 * z) + 0.5


def _silu_parts(g):
    sg = _sigmoid(g)
    return g * sg, sg * (1.0 + g * (1.0 - sg))


def _one_minus_sq(a, log_a):
    z = 2.0 * log_a
    p = 1.0 / 24.0
    for k in (6.0, 2.0, 1.0):
        p = p * z + 1.0 / k
    return jnp.where(z > -0.03, -(p * z), 1.0 - a * a)


def _place():
    x, y, c = lax.axis_index("x"), lax.axis_index("y"), lax.axis_index("c")
    return x, y, c, 4 * x + 2 * y + c


class _Direct:
    def __init__(self, a2a=(), ag=()):
        self.arrays = list(a2a) + list(ag)
        self.n_a, self.n = len(a2a), len(self.arrays)
        self.out_shape = ([jax.ShapeDtypeStruct(a.shape, a.dtype) for a in a2a]
                          + [jax.ShapeDtypeStruct((NDEV,) + a.shape, a.dtype) for a in ag])
        self.scratch = [pltpu.SemaphoreType.DMA((self.n, NDEV - 1)), pltpu.SemaphoreType.DMA((self.n, NDEV - 1)),
                        pltpu.SemaphoreType.DMA((self.n,))]

    def _copies(self, ins, outs, sems):
        send_sems, recv_sems, local_sems = sems
        x, y, c, me = _place()
        local, remote = [], []
        for t in range(self.n):
            src = ins[t].at[me] if t < self.n_a else ins[t]
            local.append(pltpu.make_async_copy(src, outs[t].at[me], local_sems.at[t]))
        for r in range(1, NDEV):
            px = 1 - x if r & 4 else x
            py = 1 - y if r & 2 else y
            pc = 1 - c if r & 1 else c
            for t in range(self.n):
                src = ins[t].at[4 * px + 2 * py + pc] if t < self.n_a else ins[t]
                remote.append(pltpu.make_async_remote_copy(
                    src_ref=src, dst_ref=outs[t].at[me], send_sem=send_sems.at[t, r - 1],
                    recv_sem=recv_sems.at[t, r - 1], device_id=(px, py, pc), device_id_type=MESH))
        return local, remote

    def start(self, ins, outs, sems):
        local, remote = self._copies(ins, outs, sems)
        for cp in local + remote:
            cp.start()

    def finish(self, ins, outs, sems):
        local, remote = self._copies(ins, outs, sems)
        for cp in remote + local:
            cp.wait()


class _AllGather2:
    def __init__(self, arrays):
        self.arrays = list(arrays)
        self.n = len(self.arrays)
        self.out_shape = [jax.ShapeDtypeStruct((NDEV,) + a.shape, a.dtype) for a in self.arrays]
        self.scratch = [pltpu.SemaphoreType.DMA((self.n, NDEV - 1)), pltpu.SemaphoreType.DMA((self.n, NDEV - 1)),
                        pltpu.SemaphoreType.DMA((self.n,))]

    @staticmethod
    def _chips(x, y):
        return [(1 - x, y), (x, 1 - y), (1 - x, 1 - y)]

    def _copy(self, t, k, src, dst, to, sems):
        return pltpu.make_async_remote_copy(src_ref=src, dst_ref=dst, send_sem=sems[0].at[t, k],
                                            recv_sem=sems[1].at[t, k], device_id=to, device_id_type=MESH)

    def start(self, ins, outs, sems):
        x, y, c, me = _place()
        for t in range(self.n):
            pltpu.make_async_copy(ins[t], outs[t].at[me], sems[2].at[t]).start()
            self._copy(t, 0, ins[t], outs[t].at[me], (x, y, 1 - c), sems).start()
            for j, (px, py) in enumerate(self._chips(x, y)):
                self._copy(t, 1 + j, ins[t], outs[t].at[me], (px, py, c), sems).start()

    def finish(self, ins, outs, sems):
        x, y, c, me = _place()
        sib = (x, y, 1 - c)
        for j, (px, py) in enumerate(self._chips(x, y)):
            slot = 4 * px + 2 * py + c
            for t in range(self.n):
                self._copy(t, 1 + j, ins[t], outs[t].at[slot], sib, sems).wait_recv()
                self._copy(t, 4 + j, outs[t].at[slot], outs[t].at[slot], sib, sems).start()
        for t in range(self.n):
            for k in (0, 4, 5, 6):
                self._copy(t, k, ins[t], outs[t].at[me], sib, sems).wait_recv()
        for t in range(self.n):
            for k in range(NDEV - 1):
                self._copy(t, k, ins[t], outs[t].at[me], sib, sems).wait_send()
            pltpu.make_async_copy(ins[t], outs[t].at[me], sems[2].at[t]).wait()


class _Both:
    def __init__(self, *riders):
        self.riders = riders
        self.arrays = [a for r in riders for a in r.arrays]
        self.n = len(self.arrays)
        self.out_shape = [o for r in riders for o in r.out_shape]
        self.scratch = [s for r in riders for s in r.scratch]

    def _parts(self, ins, outs, sems):
        p, q = 0, 0
        for r in self.riders:
            yield r, ins[p:p + r.n], outs[p:p + r.n], sems[q:q + len(r.scratch)]
            p, q = p + r.n, q + len(r.scratch)

    def start(self, ins, outs, sems):
        for r, i, o, s in self._parts(ins, outs, sems):
            r.start(i, o, s)

    def finish(self, ins, outs, sems):
        for r, i, o, s in self._parts(ins, outs, sems):
            r.finish(i, o, s)


def _exchange(name, rider):
    n = rider.n

    def body(*refs):
        rider.start(refs[:n], refs[n:2 * n], refs[2 * n:])
        rider.finish(refs[:n], refs[n:2 * n], refs[2 * n:])

    any_spec = pl.BlockSpec(memory_space=pl.ANY)
    return list(pl.pallas_call(body, name=name, out_shape=rider.out_shape, in_specs=[any_spec] * n,
                               out_specs=[any_spec] * n, scratch_shapes=rider.scratch)(*rider.arrays))


def _pcall(body, *, name, grid, in_specs, out_specs, out_shape, args, scratch_shapes=(), rider=None):
    params = _cparams(*(("arbitrary",) * len(grid)))
    if rider is None:
        res = pl.pallas_call(body, name=name, grid=grid, in_specs=in_specs, out_specs=out_specs,
                             out_shape=out_shape, scratch_shapes=list(scratch_shapes),
                             compiler_params=params)(*args)
        return list(res), []
    n_in, n_out, n_scr, rn = len(in_specs), len(out_specs), len(scratch_shapes), rider.n

    def wrapped(*refs):
        cuts = [n_in, rn, n_out, rn, n_scr]
        parts, p = [], 0
        for n in cuts:
            parts.append(refs[p:p + n])
            p += n
        ins, r_in, outs, r_out, scr = parts
        sems = refs[p:]
        ids = [pl.program_id(a) for a in range(len(grid))]
        first = functools.reduce(jnp.logical_and, [i == 0 for i in ids])
        last = functools.reduce(jnp.logical_and, [i == g - 1 for i, g in zip(ids, grid)])

        @pl.when(first)
        def _():
            rider.start(r_in, r_out, sems)
        body(*ins, *outs, *scr)

        @pl.when(last)
        def _():
            rider.finish(r_in, r_out, sems)

    any_spec = pl.BlockSpec(memory_space=pl.ANY)
    res = pl.pallas_call(
        wrapped, name=name, grid=grid, in_specs=list(in_specs) + [any_spec] * rn,
        out_specs=list(out_specs) + [any_spec] * rn, out_shape=list(out_shape) + rider.out_shape,
        scratch_shapes=list(scratch_shapes) + rider.scratch, compiler_params=params)(*args, *rider.arrays)
    return list(res[:n_out]), list(res[n_out:])


def _mod_cols(c_all, ada_w):
    nc = ada_w.shape[2]

    def body(c_ref, w_ref, o_ref):
        cv = c_ref[...]
        ca = cv * jax.nn.sigmoid(cv)
        for l in range(DEPTH):
            o_ref[:, l * nc:(l + 1) * nc] = jnp.dot(ca, w_ref[l], precision=HIGHEST, preferred_element_type=F32)

    return pl.pallas_call(body, name="mod_cols", out_shape=jax.ShapeDtypeStruct((NDEV, DEPTH * nc), F32),
                          compiler_params=_cparams())(c_all, ada_w)


def _pre_proj(x, gpre, scale1, shift, w_in_l, rider=None):
    S = x.shape[0]
    TM = min(512, S)
    NB = w_in_l.shape[2]

    def body(x_ref, g_ref, sc_ref, sh_ref, w_ref, h_ref, p_ref):
        xv = x_ref[...]
        rstd = lax.rsqrt(jnp.mean(xv * xv, axis=-1, keepdims=True) + NORM_EPS)
        h = ((xv * rstd * g_ref[...]) * sc_ref[...] + sh_ref[...]).astype(BF16)
        h_ref[...] = h
        for j in range(NDEV):
            p_ref[:, j * NB:(j + 1) * NB] = jnp.dot(h, w_ref[j], preferred_element_type=F32)

    row = pl.BlockSpec((TM, D), lambda i: (i, 0))
    return _pcall(
        body, name="pre_proj", grid=(S // TM,),
        in_specs=[row, _vec(), _vec(), _vec(), _full((NDEV, D, NB))],
        out_specs=[row, pl.BlockSpec((TM, NDEV * NB), lambda i: (i, 0))],
        out_shape=[jax.ShapeDtypeStruct((S, D), BF16), jax.ShapeDtypeStruct((S, NDEV * NB), F32)],
        args=(x, gpre, scale1, shift, w_in_l), rider=rider)


def _taps(E):
    return [pltpu.roll(E, CONV_K - 1 - k, axis=0)[CONV_HALO:, :] for k in range(CONV_K - 1)] + [E[CONV_HALO:, :]]


def _conv(E, cw_ref, cb_ref):
    w = cw_ref[...]
    taps = _taps(E)
    acc = cb_ref[...] + taps[0] * w[0:1, :]
    for k in range(1, CONV_K):
        acc = acc + taps[k] * w[k:k + 1, :]
    return acc


CH_R, CH_C = 16, 512


def _chunks(T, fn):
    def step(c, carry):
        rows = pl.ds(pl.multiple_of(c * CH_R, CH_R), CH_R)
        for hf in range(D // CH_C):
            fn(rows, slice(hf * CH_C, (hf + 1) * CH_C), hf)
        return carry
    lax.fori_loop(0, T // CH_R, step, 0)


def _to_scan(ref, rows, hf, val):
    for q in range(CH_C // LANE):
        ref[hf * (CH_C // LANE) + q, rows, :] = val[:, q * LANE:(q + 1) * LANE]


def _scan(sa, sv, carry_ref, out_ref, reverse):
    NC, T, _ = sa.shape
    n8 = T // SUB
    rows = range(SUB - 2, -1, -1) if reverse else range(1, SUB)
    for cb in range(NC):
        r_in = SUB - 1 if reverse else 0
        Ap = sa[cb, pl.ds(r_in, n8, stride=SUB), :]
        Vp = sv[cb, pl.ds(r_in, n8, stride=SUB), :]
        for r in rows:
            Ar = sa[cb, pl.ds(r, n8, stride=SUB), :]
            Vp = sv[cb, pl.ds(r, n8, stride=SUB), :] + Ar * Vp
            Ap = Ar * Ap
            sa[cb, pl.ds(r, n8, stride=SUB), :] = Ap
            sv[cb, pl.ds(r, n8, stride=SUB), :] = Vp
    edge = 0 if reverse else SUB - 1

    def step(k, c):
        r0 = pl.multiple_of((n8 - 1 - k if reverse else k) * SUB, SUB)
        h = jnp.concatenate([sv[cb, pl.ds(r0, SUB), :] + sa[cb, pl.ds(r0, SUB), :] * c[:, cb * LANE:(cb + 1) * LANE]
                             for cb in range(NC)], axis=1)
        out_ref[pl.ds(r0, SUB), :] = h
        return jnp.broadcast_to(h[edge:edge + 1, :], (SUB, D))

    carry_ref[...] = lax.fori_loop(0, n8, step, carry_ref[...])


def _rnn_fwd(proj, conv_w, conv_b, wa, ba, wx, bx, lam, rider=None):
    S = proj.shape[0]
    TB = min(256, S)

    def body(xr_ref, g_ref, cw_ref, cb_ref, wa_ref, ba_ref, wx_ref, bx_ref, lam_ref, hs_ref, y_ref,
             u_ref, r_ref, i_ref, a_ref, m_ref, xbuf, sa, sv, hc):
        @pl.when(pl.program_id(0) == 0)
        def _():
            xbuf[0:CONV_HALO, :] = jnp.zeros((CONV_HALO, D), F32)
            hc[...] = jnp.zeros((SUB, D), F32)
        xbuf[CONV_HALO:, :] = xr_ref[...]
        u = _conv(xbuf[...], cw_ref, cb_ref)
        xbuf[0:CONV_HALO, :] = xbuf[TB:TB + CONV_HALO, :]
        ub = u.astype(BF16)
        zr = jnp.concatenate([jnp.dot(ub[:, h * HD:(h + 1) * HD], wa_ref[h], preferred_element_type=F32)
                              for h in range(NHEAD)], axis=1)
        zi = jnp.concatenate([jnp.dot(ub[:, h * HD:(h + 1) * HD], wx_ref[h], preferred_element_type=F32)
                              for h in range(NHEAD)], axis=1)
        r = _sigmoid(zr + ba_ref[...])
        ig = _sigmoid(zi + bx_ref[...])
        log_a = r * (-LRU_C * jax.nn.softplus(-lam_ref[...]))
        a = jnp.exp(log_a)
        mult = jnp.sqrt(_one_minus_sq(a, log_a))
        v = mult * (ig * u)
        u_ref[...] = u
        r_ref[...] = r
        i_ref[...] = ig
        a_ref[...] = a
        m_ref[...] = mult
        for cb in range(D // LANE):
            sa[cb] = a[:, cb * LANE:(cb + 1) * LANE]
            sv[cb] = v[:, cb * LANE:(cb + 1) * LANE]
        _scan(sa, sv, hc, hs_ref, reverse=False)
        silu, _ = _silu_parts(g_ref[...])
        y_ref[...] = (hs_ref[...] * silu).astype(BF16)

    rowb = pl.BlockSpec((TB, D), lambda i: (i, 0))
    return _pcall(
        body, name="rnn_fwd", grid=(S // TB,),
        in_specs=[rowb, pl.BlockSpec((TB, D), lambda i: (i, 1)),
                  _full((CONV_K, D)), _vec(), _full((NHEAD, HD, HD)), _vec(), _full((NHEAD, HD, HD)), _vec(),
                  _vec()],
        out_specs=[rowb] * 7,
        out_shape=[jax.ShapeDtypeStruct((S, D), F32), jax.ShapeDtypeStruct((S, D), BF16)]
        + [jax.ShapeDtypeStruct((S, D), F32)] * 5,
        scratch_shapes=[pltpu.VMEM((TB + CONV_HALO, D), F32), pltpu.VMEM((D // LANE, TB, LANE), F32),
                        pltpu.VMEM((D // LANE, TB, LANE), F32), pltpu.VMEM((SUB, D), F32)],
        args=(proj, proj, conv_w, conv_b, wa, ba, wx, bx, lam), rider=rider)


def _pooled(ebuf, t0, TB):
    tt = t0 + lax.broadcasted_iota(jnp.int32, (TB, 1), 0)
    pooled, inv = [], []
    for g, win in enumerate(WINS):
        Eg = ebuf[:, g * GD:(g + 1) * GD]
        L = Eg
        for lev in range(g + 1):
            L = L + pltpu.roll(L, 1 << lev, axis=0)
        icnt = 1.0 / jnp.minimum(tt + 1, win).astype(F32)
        pooled.append(L[POOL_HALO:, :] * icnt - Eg[POOL_HALO:, :])
        inv.append(icnt)
    return pooled, inv


def _pool_fwd(proj, pw, pb, ps, rider=None):
    S = proj.shape[0]
    TB = min(256, S)

    def body(xp_ref, g_ref, pw_ref, pb_ref, ps_ref, y_ref, ebuf):
        i = pl.program_id(0)

        @pl.when(i == 0)
        def _():
            ebuf[0:POOL_HALO, :] = jnp.zeros((POOL_HALO, D), F32)
        ebuf[POOL_HALO:, :] = xp_ref[...]
        pooled, _ = _pooled(ebuf, i * TB, TB)
        ebuf[0:POOL_HALO, :] = ebuf[TB:TB + POOL_HALO, :]
        yp = jnp.concatenate([jnp.dot(pooled[g].astype(BF16), pw_ref[g], preferred_element_type=F32)
                              for g in range(NGRP)], axis=1) + pb_ref[...]
        silu, _ = _silu_parts(g_ref[...])
        y_ref[...] = (yp * ps_ref[...] * silu).astype(BF16)

    return _pcall(
        body, name="pool_fwd", grid=(S // TB,),
        in_specs=[pl.BlockSpec((TB, D), lambda i: (i, 2)), pl.BlockSpec((TB, D), lambda i: (i, 3)),
                  _full((NGRP, GD, GD)), _vec(), _vec()],
        out_specs=[pl.BlockSpec((TB, D), lambda i: (i, 0))],
        out_shape=[jax.ShapeDtypeStruct((S, D), BF16)],
        scratch_shapes=[pltpu.VMEM((TB + POOL_HALO, D), F32)],
        args=(proj, proj, pw, pb, ps), rider=rider)


def _out_post(yr, yp, w_out_l, x, gate, gpost, target=None, rider=None):
    S = x.shape[0]
    TM = min(512, S)
    KB = w_out_l.shape[1]
    last = target is not None

    def body(*refs):
        if last:
            yr_ref, yp_ref, w_ref, x_ref, gate_ref, gp_ref, t_ref, y_ref, xo_ref, loss_ref = refs
        else:
            yr_ref, yp_ref, w_ref, x_ref, gate_ref, gp_ref, y_ref, xo_ref = refs
        acc = jnp.zeros((TM, D), F32)
        for j in range(NDEV):
            src = yr_ref if j < NDEV // 2 else yp_ref
            k0 = (j % (NDEV // 2)) * KB
            acc = acc + jnp.dot(src[:, k0:k0 + KB], w_ref[j], preferred_element_type=F32)
        y_ref[...] = acc
        rstd = lax.rsqrt(jnp.mean(acc * acc, axis=-1, keepdims=True) + NORM_EPS)
        xn = x_ref[...] + gate_ref[...] * (acc * rstd * gp_ref[...])
        if last:
            err = xn - t_ref[...]
            xo_ref[...] = err * (1.0 / D)

            @pl.when(pl.program_id(0) == 0)
            def _():
                loss_ref[...] = jnp.zeros((SUB, D), F32)
            loss_ref[...] += _rowsum8(err * err)
        else:
            xo_ref[...] = xn

    row = pl.BlockSpec((TM, D), lambda i: (i, 0))
    in_specs = [row, row, _full((NDEV, KB, D)), row, _vec(), _vec()]
    out_specs = [row, row]
    out_shape = [jax.ShapeDtypeStruct((S, D), F32), jax.ShapeDtypeStruct((S, D), F32)]
    args = [yr, yp, w_out_l, x, gate, gpost]
    if last:
        in_specs.append(row)
        out_specs.append(_full((SUB, D)))
        out_shape.append(jax.ShapeDtypeStruct((SUB, D), F32))
        args.append(target)
    return _pcall(body, name="out_post_loss" if last else "out_post", grid=(S // TM,), in_specs=in_specs,
                  out_specs=out_specs, out_shape=out_shape, args=args, rider=rider)


def _out_bwd(dxo, y, yr, yp, w_out_l, gate, gpost, rider=None):
    S = y.shape[0]
    TM = min(256, S)
    KB = w_out_l.shape[1]
    nsteps = S // TM

    def body(dxo_ref, y_ref, yr_ref, yp_ref, w_ref, gate_ref, gp_ref, dyr_ref, dyp_ref, gw_ref, dgate_ref,
             dgp_ref, gw_acc, vacc):
        i = pl.program_id(0)

        @pl.when(i == 0)
        def _():
            gw_acc[...] = jnp.zeros_like(gw_acc)
            vacc[...] = jnp.zeros_like(vacc)
        yv = y_ref[...]
        dxo_v = dxo_ref[...]
        rstd = lax.rsqrt(jnp.mean(yv * yv, axis=-1, keepdims=True) + NORM_EPS)
        n = yv * rstd
        gp = gp_ref[...]
        vacc[0] += _rowsum8(dxo_v * (n * gp))
        drn = dxo_v * gate_ref[...]
        vacc[1] += _rowsum8(drn * n)
        dn = drn * gp
        dy = (rstd * (dn - n * jnp.mean(dn * n, axis=-1, keepdims=True))).astype(BF16)
        for j in range(NDEV):
            dst = dyr_ref if j < NDEV // 2 else dyp_ref
            k0 = (j % (NDEV // 2)) * KB
            dst[:, k0:k0 + KB] = lax.dot_general(dy, w_ref[j], (((1,), (1,)), ((), ())),
                                                 preferred_element_type=F32)
        gw_acc[0:D, :] += lax.dot_general(yr_ref[...], dy, (((0,), (0,)), ((), ())), preferred_element_type=F32)
        gw_acc[D:2 * D, :] += lax.dot_general(yp_ref[...], dy, (((0,), (0,)), ((), ())),
                                              preferred_element_type=F32)

        @pl.when(i == nsteps - 1)
        def _():
            gw_ref[...] = gw_acc[...].astype(BF16)
            dgate_ref[...] = _sum8(vacc[0])
            dgp_ref[...] = _sum8(vacc[1])

    row = pl.BlockSpec((TM, D), lambda i: (i, 0))
    return _pcall(
        body, name="out_bwd", grid=(nsteps,),
        in_specs=[row, row, row, row, _full((NDEV, KB, D)), _vec(), _vec()],
        out_specs=[row, row, _full((2 * D, D)), _vec(), _vec()],
        out_shape=[jax.ShapeDtypeStruct((S, D), F32), jax.ShapeDtypeStruct((S, D), F32),
                   jax.ShapeDtypeStruct((2 * D, D), BF16), jax.ShapeDtypeStruct((1, D), F32),
                   jax.ShapeDtypeStruct((1, D), F32)],
        scratch_shapes=[pltpu.VMEM((2 * D, D), F32), pltpu.VMEM((2, SUB, D), F32)],
        args=(dxo, y, yr, yp, w_out_l, gate, gpost), rider=rider)


def _rnn_bwd(dyr, hs, fwd, proj, conv_w, wa, wx, lam, rider=None):
    S = proj.shape[0]
    TB = min(256, S)
    nb = S // TB
    TE = TB + CONV_HALO
    A_BA, A_BX, A_LAM, A_CB, A_CW = 0, 1, 2, 3, 4

    def blk(i):
        return nb - 1 - i

    def body(dyr_ref, hs_ref, hprev_ref, u_ref, r_ref, i_ref, a_ref, m_ref, xr_ref, g_ref, cw_ref,
             wa_ref, wx_ref, lam_ref, dxr_ref, dg_ref, gcw_ref, gcb_ref, gwa_ref, gba_ref, gwx_ref, gbx_ref,
             glam_ref, hbuf, abuf, dbuf, sa, sv, dh_ref, hp_ref, dzr_ref, dzi_ref, dhc, vacc):
        i = pl.program_id(0)
        first = blk(i) == 0

        @pl.when(i == 0)
        def _():
            abuf[TB:, :] = jnp.zeros((CONV_HALO, D), F32)
            dbuf[TB:, :] = jnp.zeros((CONV_HALO, D), F32)
            dhc[...] = jnp.zeros_like(dhc)
            vacc[...] = jnp.zeros_like(vacc)
            gwa_ref[...] = jnp.zeros_like(gwa_ref)
            gwx_ref[...] = jnp.zeros_like(gwx_ref)

        hbuf[0:CONV_HALO, :] = jnp.where(first, 0.0, hprev_ref[...])
        hbuf[CONV_HALO:, :] = hs_ref[...]
        hp_ref[...] = pltpu.roll(hbuf[...], 1, axis=0)[CONV_HALO:, :]
        abuf[0:TB, :] = a_ref[...]
        b = pltpu.roll(abuf[...], TE - 1, axis=0)[0:TB, :]
        for cb in range(D // LANE):
            sa[cb] = b[:, cb * LANE:(cb + 1) * LANE]
        abuf[TB:, :] = a_ref[0:CONV_HALO, :]

        def gate_bwd(rows, cs, hf):
            silu, dsilu = _silu_parts(g_ref[rows, cs])
            dy = dyr_ref[rows, cs]
            dg_ref[rows, cs] = (dy * hs_ref[rows, cs] * dsilu).astype(BF16)
            _to_scan(sv, rows, hf, dy * silu)
        _chunks(TB, gate_bwd)

        _scan(sa, sv, dhc, dh_ref, reverse=True)

        csp = -LRU_C * jax.nn.softplus(-lam_ref[...])

        def lru_bwd(rows, cs, hf):
            dh, a, ig, u, mult, r = dh_ref[rows, cs], a_ref[rows, cs], i_ref[rows, cs], u_ref[rows, cs], \
                m_ref[rows, cs], r_ref[rows, cs]
            dlog_a = dh * hp_ref[rows, cs] * a - (dh * ig * u) * (a * a) / mult
            dzr = dlog_a * csp[:, cs] * r * (1.0 - r)
            dzi = (dh * mult * u) * ig * (1.0 - ig)
            dzr_ref[rows, cs] = dzr.astype(BF16)
            dzi_ref[rows, cs] = dzi.astype(BF16)
            dbuf[rows, cs] = dh * mult * ig
            vacc[A_LAM, :, cs] += _rowsum8(dlog_a * r)
            vacc[A_BA, :, cs] += _rowsum8(dzr)
            vacc[A_BX, :, cs] += _rowsum8(dzi)
        _chunks(TB, lru_bwd)

        ub, dzrb, dzib = u_ref[...].astype(BF16), dzr_ref[...], dzi_ref[...]
        du_g = []
        for h in range(NHEAD):
            cs = slice(h * HD, (h + 1) * HD)
            gwa_ref[h] += lax.dot_general(ub[:, cs], dzrb[:, cs], (((0,), (0,)), ((), ())),
                                          preferred_element_type=F32)
            gwx_ref[h] += lax.dot_general(ub[:, cs], dzib[:, cs], (((0,), (0,)), ((), ())),
                                          preferred_element_type=F32)
            du_g.append(lax.dot_general(dzrb[:, cs], wa_ref[h], (((1,), (1,)), ((), ())),
                                        preferred_element_type=F32)
                        + lax.dot_general(dzib[:, cs], wx_ref[h], (((1,), (1,)), ((), ())),
                                          preferred_element_type=F32))
        du = dbuf[0:TB, :] + jnp.concatenate(du_g, axis=1)
        dbuf[0:TB, :] = du

        Dd = dbuf[...]
        w = cw_ref[...]
        xv = xr_ref[...]
        dx = du * w[CONV_K - 1:CONV_K, :]
        vacc[A_CB] += _rowsum8(du)
        vacc[A_CW + CONV_K - 1] += _rowsum8(xv * du)
        for k in range(CONV_K - 1):
            ahead = pltpu.roll(Dd, TE - (CONV_K - 1 - k), axis=0)[0:TB, :]
            dx = dx + ahead * w[k:k + 1, :]
            vacc[A_CW + k] += _rowsum8(xv * ahead)
        dxr_ref[...] = dx.astype(BF16)
        dbuf[TB:, :] = du[0:CONV_HALO, :]

        @pl.when(i == nb - 1)
        def _():
            gba_ref[...] = _sum8(vacc[A_BA])
            gbx_ref[...] = _sum8(vacc[A_BX])
            glam_ref[...] = _sum8(vacc[A_LAM]) * (LRU_C * _sigmoid(-lam_ref[...]))
            gcb_ref[...] = _sum8(vacc[A_CB])
            for k in range(CONV_K):
                gcw_ref[k:k + 1, :] = _sum8(vacc[A_CW + k])

    rowb = pl.BlockSpec((TB, D), lambda i: (blk(i), 0))
    halo = pl.BlockSpec((SUB, D), lambda i: (jnp.maximum(blk(i) * (TB // SUB) - 1, 0), 0))
    wspec = _full((NHEAD, HD, HD))
    return _pcall(
        body, name="rnn_bwd", grid=(nb,),
        in_specs=[rowb, rowb, halo] + [rowb] * 5 + [rowb, pl.BlockSpec((TB, D), lambda i: (blk(i), 1)),
                                                    _full((CONV_K, D)), wspec, wspec, _vec()],
        out_specs=[rowb, rowb, _full((CONV_K, D)), _vec(), wspec, _vec(), wspec, _vec(), _vec()],
        out_shape=[jax.ShapeDtypeStruct((S, D), BF16), jax.ShapeDtypeStruct((S, D), BF16),
                   jax.ShapeDtypeStruct((CONV_K, D), F32), jax.ShapeDtypeStruct((1, D), F32),
                   jax.ShapeDtypeStruct((NHEAD, HD, HD), F32), jax.ShapeDtypeStruct((1, D), F32),
                   jax.ShapeDtypeStruct((NHEAD, HD, HD), F32), jax.ShapeDtypeStruct((1, D), F32),
                   jax.ShapeDtypeStruct((1, D), F32)],
        scratch_shapes=[pltpu.VMEM((TE, D), F32), pltpu.VMEM((TE, D), F32), pltpu.VMEM((TE, D), F32),
                        pltpu.VMEM((D // LANE, TB, LANE), F32), pltpu.VMEM((D // LANE, TB, LANE), F32),
                        pltpu.VMEM((TB, D), F32), pltpu.VMEM((TB, D), F32), pltpu.VMEM((TB, D), BF16),
                        pltpu.VMEM((TB, D), BF16), pltpu.VMEM((SUB, D), F32),
                        pltpu.VMEM((A_CW + CONV_K, SUB, D), F32)],
        args=(dyr, hs, hs, *fwd, proj, proj, conv_w, wa, wx, lam), rider=rider)


def _pool_bwd(dyp, proj, pw, pb, ps, rider=None):
    S = proj.shape[0]
    TB = min(256, S)
    nb = S // TB
    TE = TB + POOL_HALO

    def blk(i):
        return nb - 1 - i

    def body(dy_ref, xp_ref, xprev_ref, g_ref, pw_ref, pb_ref, ps_ref, dxp_ref, dg_ref, gpw_ref, gpb_ref,
             gps_ref, ebuf, qbuf, vacc):
        i = pl.program_id(0)
        first = blk(i) == 0

        @pl.when(i == 0)
        def _():
            qbuf[TB:, :] = jnp.zeros((POOL_HALO, D), F32)
            vacc[...] = jnp.zeros_like(vacc)
            gpw_ref[...] = jnp.zeros_like(gpw_ref)

        ebuf[0:POOL_HALO, :] = jnp.where(first, 0.0, xprev_ref[...])
        ebuf[POOL_HALO:, :] = xp_ref[...]
        pooled, inv = _pooled(ebuf, blk(i) * TB, TB)
        pooled = [p.astype(BF16) for p in pooled]
        yp = jnp.concatenate([jnp.dot(pooled[g], pw_ref[g], preferred_element_type=F32)
                              for g in range(NGRP)], axis=1) + pb_ref[...]
        silu, dsilu = _silu_parts(g_ref[...])
        dy = dy_ref[...]
        ps = ps_ref[...]
        dyp_v = dy * ps * silu
        vacc[0] += _rowsum8(dy * yp * silu)
        vacc[1] += _rowsum8(dyp_v)
        dg_ref[...] = (dy * yp * ps * dsilu).astype(BF16)
        dypb = dyp_v.astype(BF16)
        for g in range(NGRP):
            cs = slice(g * GD, (g + 1) * GD)
            gpw_ref[g] += lax.dot_general(pooled[g], dypb[:, cs], (((0,), (0,)), ((), ())),
                                          preferred_element_type=F32)
            dpool = lax.dot_general(dypb[:, cs], pw_ref[g], (((1,), (1,)), ((), ())),
                                    preferred_element_type=F32)
            qbuf[0:TB, cs] = dpool * inv[g]
            L = qbuf[:, cs]
            for lev in range(g + 1):
                L = L + pltpu.roll(L, TE - (1 << lev), axis=0)
            dxp_ref[:, cs] = (L[0:TB, :] - dpool).astype(BF16)
        qbuf[TB:, :] = qbuf[0:POOL_HALO, :]

        @pl.when(i == nb - 1)
        def _():
            gps_ref[...] = _sum8(vacc[0])
            gpb_ref[...] = _sum8(vacc[1])

    rowb = pl.BlockSpec((TB, D), lambda i: (blk(i), 0))
    return _pcall(
        body, name="pool_bwd", grid=(nb,),
        in_specs=[rowb, pl.BlockSpec((TB, D), lambda i: (blk(i), 2)),
                  pl.BlockSpec((POOL_HALO, D), lambda i: (jnp.maximum(blk(i) * (TB // POOL_HALO) - 1, 0), 2)),
                  pl.BlockSpec((TB, D), lambda i: (blk(i), 3)), _full((NGRP, GD, GD)), _vec(), _vec()],
        out_specs=[rowb, rowb, _full((NGRP, GD, GD)), _vec(), _vec()],
        out_shape=[jax.ShapeDtypeStruct((S, D), BF16), jax.ShapeDtypeStruct((S, D), BF16),
                   jax.ShapeDtypeStruct((NGRP, GD, GD), F32), jax.ShapeDtypeStruct((1, D), F32),
                   jax.ShapeDtypeStruct((1, D), F32)],
        scratch_shapes=[pltpu.VMEM((TE, D), F32), pltpu.VMEM((TE, D), F32), pltpu.VMEM((2, SUB, D), F32)],
        args=(dyp, proj, proj, proj, pw, pb, ps), rider=rider)


def _in_bwd(dq, w_in_l, x, dxo, gpre, scale1, rider=None):
    S = x.shape[0]
    TM = min(256, S)
    NB = w_in_l.shape[2]
    nsteps = S // TM
    per_q = D // NB

    def body(d0, d1, d2, d3, w_ref, x_ref, dxo_ref, g_ref, sc_ref, dx_ref, dsh_ref, dsc_ref, dg_ref, vacc):
        i = pl.program_id(0)

        @pl.when(i == 0)
        def _():
            vacc[...] = jnp.zeros_like(vacc)
        dref = (d0, d1, d2, d3)
        dh = jnp.zeros((TM, D), F32)
        for j in range(NDEV):
            c0 = (j % per_q) * NB
            dh = dh + lax.dot_general(dref[j // per_q][:, c0:c0 + NB], w_ref[j], (((1,), (1,)), ((), ())),
                                      preferred_element_type=F32)
        xv = x_ref[...]
        rstd = lax.rsqrt(jnp.mean(xv * xv, axis=-1, keepdims=True) + NORM_EPS)
        xn = xv * rstd
        g, sc = g_ref[...], sc_ref[...]
        vacc[0] += _rowsum8(dh)
        vacc[1] += _rowsum8(dh * (xn * g))
        vacc[2] += _rowsum8(dh * sc * xn)
        dxn = dh * sc * g
        dx_ref[...] = dxo_ref[...] + rstd * (dxn - xn * jnp.mean(dxn * xn, axis=-1, keepdims=True))

        @pl.when(i == nsteps - 1)
        def _():
            dsh_ref[...] = _sum8(vacc[0])
            dsc_ref[...] = _sum8(vacc[1])
            dg_ref[...] = _sum8(vacc[2])

    row = pl.BlockSpec((TM, D), lambda i: (i, 0))
    return _pcall(
        body, name="in_bwd", grid=(nsteps,),
        in_specs=[row, row, row, row, _full((NDEV, D, NB)), row, row, _vec(), _vec()],
        out_specs=[row, _vec(), _vec(), _vec()],
        out_shape=[jax.ShapeDtypeStruct((S, D), F32)] + [jax.ShapeDtypeStruct((1, D), F32)] * 3,
        scratch_shapes=[pltpu.VMEM((3, SUB, D), F32)],
        args=(*dq, w_in_l, x, dxo, gpre, scale1), rider=rider)


def _grad_w_in(h, dq, NB, part, rider=None):
    S = h.shape[0]
    TK = min(512, S)
    nk = S // TK
    RH = D // 2

    def body(h_ref, d0, d1, d2, d3, o_ref, acc):
        k = pl.program_id(0)

        @pl.when(k == 0)
        def _():
            acc[...] = jnp.zeros_like(acc)
        hv = h_ref[...]
        for q, d_ref in enumerate((d0, d1, d2, d3)):
            acc[:, q * D:(q + 1) * D] += lax.dot_general(hv, d_ref[...], (((0,), (0,)), ((), ())),
                                                         preferred_element_type=F32)

        @pl.when(k == nk - 1)
        def _():
            for j in range(NDEV):
                o_ref[j] = acc[:, j * NB:(j + 1) * NB].astype(BF16)

    row = pl.BlockSpec((TK, D), lambda k: (k, 0))
    return _pcall(
        body, name="grad_w_in", grid=(nk,),
        in_specs=[pl.BlockSpec((TK, RH), lambda k: (k, part)), row, row, row, row],
        out_specs=[_full((NDEV, RH, NB))],
        out_shape=[jax.ShapeDtypeStruct((NDEV, RH, NB), BF16)],
        scratch_shapes=[pltpu.VMEM((RH, NQ * D), F32)],
        args=(h, *dq), rider=rider)


def _adamw_math(g, w, m, v):
    m2 = ADAM_B1 * m + (1.0 - ADAM_B1) * g
    v2 = ADAM_B2 * v + (1.0 - ADAM_B2) * (g * g)
    m_hat = m2 / (1.0 - ADAM_B1 ** ADAM_STEP)
    v_hat = v2 / (1.0 - ADAM_B2 ** ADAM_STEP)
    delta = -ADAM_LR * (m_hat / (jnp.sqrt(v_hat) + ADAM_EPS) + ADAM_WD * w)
    return delta, m2, v2


def _adamw(name, gs, w, m, v, TR, rider=None):
    L = len(gs)
    n, R, C = gs[0].shape

    def body(*refs):
        g_refs = refs[:L]
        w_ref, m_ref, v_ref, go_ref, do_ref, mo_ref, vo_ref = refs[L:]
        lay = pl.program_id(0)
        for li in range(L):
            @pl.when(lay == li)
            def _(li=li):
                g = g_refs[li][0].astype(F32)
                for s in range(1, n):
                    g = g + g_refs[li][s].astype(F32)
                delta, m2, v2 = _adamw_math(g, w_ref[...], m_ref[...], v_ref[...])
                go_ref[...] = g
                do_ref[...] = delta
                mo_ref[...] = m2
                vo_ref[...] = v2

    lrc = pl.BlockSpec((None, TR, C), lambda lay, r: (lay, r, 0))
    g_specs = [pl.BlockSpec((n, TR, C), lambda lay, r, li=li: (0, jnp.where(lay == li, r, 0), 0))
               for li in range(L)]
    return _pcall(
        body, name=name, grid=(L, R // TR),
        in_specs=g_specs + [lrc, lrc, lrc], out_specs=[lrc] * 4,
        out_shape=[jax.ShapeDtypeStruct((L, R, C), F32)] * 4,
        args=(*gs, w, m, v), rider=rider)


def _ada_adamw(c_all_t, dm, w, m, v, rider=None):
    L, _, nc = w.shape

    def body(c_ref, dm_ref, w_ref, m_ref, v_ref, go_ref, do_ref, mo_ref, vo_ref):
        cv = c_ref[...]
        ca = cv * jax.nn.sigmoid(cv)
        dmv = dm_ref[...]
        g = ca[:, 0:1] * dmv[0:1, :]
        for b in range(1, NDEV):
            g = g + ca[:, b:b + 1] * dmv[b:b + 1, :]
        delta, m2, v2 = _adamw_math(g, w_ref[...], m_ref[...], v_ref[...])
        go_ref[...] = g
        do_ref[...] = delta
        mo_ref[...] = m2
        vo_ref[...] = v2

    big = pl.BlockSpec((None, D, nc), lambda lay: (lay, 0, 0))
    return _pcall(
        body, name="ada_adamw", grid=(L,),
        in_specs=[_full((D, NDEV)), pl.BlockSpec((None, NDEV, nc), lambda lay: (lay, 0, 0)), big, big, big],
        out_specs=[big] * 4, out_shape=[jax.ShapeDtypeStruct((L, D, nc), F32)] * 4,
        args=(c_all_t, dm, w, m, v), rider=rider)


def _sum_slots(recv):
    n, R, C = recv.shape

    def body(r_ref, o_ref):
        acc = r_ref[0].astype(F32)
        for s in range(1, n):
            acc = acc + r_ref[s].astype(F32)
        o_ref[...] = acc

    return pl.pallas_call(body, name="sum_slots", out_shape=jax.ShapeDtypeStruct((R, C), F32),
                          compiler_params=_cparams())(recv)


def _pad_rows(a, rows):
    return jnp.pad(a, ((0, rows - a.shape[0]), (0, 0)))


def _pack_sharded_block(pool_w, pool_b, conv_w):
    return jnp.concatenate([pool_w.reshape(-1, PACK_C), _pad_rows(pool_b.reshape(-1, PACK_C), SUB),
                            _pad_rows(conv_w.reshape(-1, PACK_C), SUB)], axis=0)


def _unpack_sharded_block(p):
    n_pw = DEPTH * NGRP * (GD // NDEV)
    pool_w = p[:n_pw].reshape(DEPTH, NGRP, GD // NDEV, GD)
    pool_b = p[n_pw].reshape(DEPTH, NGRP, GD // NDEV)
    conv_w = p[n_pw + SUB:n_pw + SUB + DEPTH * CONV_K * (D // NDEV) // PACK_C].reshape(DEPTH, CONV_K, D // NDEV)
    return pool_w, pool_b, conv_w


def _blocks_of_full(pool_w, pool_b, conv_w):
    pw = pool_w.reshape(DEPTH, NGRP, NDEV, GD // NDEV, GD).transpose(2, 0, 1, 3, 4).reshape(NDEV, -1, PACK_C)
    pb = pool_b.reshape(DEPTH, NGRP, NDEV, GD // NDEV).transpose(2, 0, 1, 3).reshape(NDEV, -1, PACK_C)
    cw = conv_w.reshape(DEPTH, CONV_K, NDEV, D // NDEV).transpose(2, 0, 1, 3).reshape(NDEV, -1, PACK_C)
    pad = lambda a: jnp.pad(a, ((0, 0), (0, SUB - a.shape[1]), (0, 0)))
    return jnp.concatenate([pw, pad(pb), pad(cw)], axis=1)


def _full_of_blocks(p):
    n_pw = DEPTH * NGRP * (GD // NDEV)
    pool_w = p[:, :n_pw].reshape(NDEV, DEPTH, NGRP, GD // NDEV, GD).transpose(1, 2, 0, 3, 4)
    pool_b = p[:, n_pw].reshape(NDEV, DEPTH, NGRP, GD // NDEV).transpose(1, 2, 0, 3)
    n_cw = DEPTH * CONV_K * (D // NDEV) // PACK_C
    conv_w = p[:, n_pw + SUB:n_pw + SUB + n_cw].reshape(NDEV, DEPTH, CONV_K, D // NDEV).transpose(1, 2, 0, 3)
    return (pool_w.reshape(DEPTH, NGRP, GD, GD), pool_b.reshape(DEPTH, NGRP, GD),
            conv_w.reshape(DEPTH, CONV_K, D))


def _pack_replicated(t, keys, rows):
    p = jnp.concatenate([t[k].reshape(-1, PACK_C) for k in keys], axis=0)
    return _pad_rows(p, rows)


def _unpack_replicated(p, like, keys):
    out, r0 = {}, 0
    for k in keys:
        rows = like[k].size // PACK_C
        out[k] = p[r0:r0 + rows].reshape(like[k].shape)
        r0 += rows
    return out


def kernel(x, c, ada_w, ada_b, pre_norm_g, w_in, conv_w, conv_b, gate_a_w, gate_a_b, gate_x_w, gate_x_b, lru_lambda, pool_w, pool_b, pool_scale, w_out, post_norm_g, loss_target, m_ada_w, m_ada_b, m_pre_norm_g, m_w_in, m_conv_w, m_conv_b, m_gate_a_w, m_gate_a_b, m_gate_x_w, m_gate_x_b, m_lru_lambda, m_pool_w, m_pool_b, m_pool_scale, m_w_out, m_post_norm_g, v_ada_w, v_ada_b, v_pre_norm_g, v_w_in, v_conv_w, v_conv_b, v_gate_a_w, v_gate_a_b, v_gate_x_w, v_gate_x_b, v_lru_lambda, v_pool_w, v_pool_b, v_pool_scale, v_w_out, v_post_norm_g):
    W = dict(ada_w=ada_w, ada_b=ada_b, pre_norm_g=pre_norm_g, w_in=w_in, conv_w=conv_w, conv_b=conv_b,
             gate_a_w=gate_a_w, gate_a_b=gate_a_b, gate_x_w=gate_x_w, gate_x_b=gate_x_b, lru_lambda=lru_lambda,
             pool_w=pool_w, pool_b=pool_b, pool_scale=pool_scale, w_out=w_out, post_norm_g=post_norm_g)
    M = dict(ada_w=m_ada_w, ada_b=m_ada_b, pre_norm_g=m_pre_norm_g, w_in=m_w_in, conv_w=m_conv_w,
             conv_b=m_conv_b, gate_a_w=m_gate_a_w, gate_a_b=m_gate_a_b, gate_x_w=m_gate_x_w,
             gate_x_b=m_gate_x_b, lru_lambda=m_lru_lambda, pool_w=m_pool_w, pool_b=m_pool_b,
             pool_scale=m_pool_scale, w_out=m_w_out, post_norm_g=m_post_norm_g)
    V = dict(ada_w=v_ada_w, ada_b=v_ada_b, pre_norm_g=v_pre_norm_g, w_in=v_w_in, conv_w=v_conv_w,
             conv_b=v_conv_b, gate_a_w=v_gate_a_w, gate_a_b=v_gate_a_b, gate_x_w=v_gate_x_w,
             gate_x_b=v_gate_x_b, lru_lambda=v_lru_lambda, pool_w=v_pool_w, pool_b=v_pool_b,
             pool_scale=v_pool_scale, w_out=v_w_out, post_norm_g=v_post_norm_g)
    S = x.shape[1]
    me = 4 * lax.axis_index("x") + 2 * lax.axis_index("y") + lax.axis_index("c")
    xs = x.reshape(S, D)
    tgt = loss_target.reshape(S, D)
    nc = ada_w.shape[2]
    NB = w_in.shape[2]
    vec = lambda a: a.reshape(1, D)
    w_in_b, w_out_b = w_in.astype(BF16), w_out.astype(BF16)

    c_slots, w_in0 = _exchange("gather_c_w_in0", _AllGather2([jnp.broadcast_to(c, (SUB, D)), w_in_b[0]]))
    c_all = c_slots[:, 0, :]
    (mod_slots,) = _exchange("gather_mod", _Direct(ag=[_mod_cols(c_all, ada_w)]))
    mod = lax.dynamic_index_in_dim(mod_slots, me, axis=1, keepdims=False)
    mod = mod.reshape(NDEV, DEPTH, nc).transpose(1, 0, 2).reshape(DEPTH, 3 * D) + ada_b
    mods = [(vec(mod[l, :D]), vec(1.0 + mod[l, D:2 * D]), vec(mod[l, 2 * D:])) for l in range(DEPTH)]

    w_in_all, w_out_all = [w_in0, None], [None, None]
    saved = []
    xl = xs
    for l in range(DEPTH):
        shift, scale1, gate = mods[l]
        wa, wx = gate_a_w[l].astype(BF16), gate_x_w[l].astype(BF16)
        rider = _AllGather2([_pack_sharded_block(pool_w, pool_b, conv_w), w_out_b[0]]) if l == 0 else None
        (h, proj), got = _pre_proj(xl, vec(pre_norm_g[l]), scale1, shift, w_in_all[l], rider=rider)
        if l == 0:
            pool_w_f, pool_b_f, conv_w_f = _full_of_blocks(got[0])
            w_out_all[0] = got[1]
        pw = pool_w_f[l].astype(BF16)
        rider = _AllGather2([w_in_b[1]]) if l == 0 else None
        (hs, yr, *fwd), got = _rnn_fwd(proj, conv_w_f[l], vec(conv_b[l]), wa, gate_a_b[l].reshape(1, D), wx,
                                 gate_x_b[l].reshape(1, D), vec(lru_lambda[l]), rider=rider)
        if l == 0:
            w_in_all[1] = got[0]
        (yp,), _ = _pool_fwd(proj, pw, pool_b_f[l].reshape(1, D), vec(pool_scale[l]))
        if l == DEPTH - 1:
            (y, x_next, loss_acc), _ = _out_post(yr, yp, w_out_all[l], xl, gate, vec(post_norm_g[l]), tgt)
        else:
            (y, x_next), (w_out_all[1],) = _out_post(yr, yp, w_out_all[l], xl, gate, vec(post_norm_g[l]),
                                                     rider=_AllGather2([w_out_b[1]]))
        saved.append((xl, h, proj, hs, fwd, yr, yp, y, wa, wx, pw))
        xl = x_next

    dxo = xl
    G = {k: [None] * DEPTH for k in WEIGHTS}
    dmod = [None] * DEPTH
    recv_in, recv_out = [[None, None] for _ in range(DEPTH)], [None] * DEPTH
    full = dict(conv_w=(CONV_K, D), pool_w=(NGRP, GD, GD), pool_b=(NGRP, GD))
    stack = lambda k: jnp.stack([g.reshape(full.get(k, W[k].shape[1:])) for g in G[k]])
    gw_bot_prev = None
    for l in reversed(range(DEPTH)):
        xin, h, proj, hs, fwd, yr, yp, y, wa, wx, pw = saved[l]
        shift, scale1, gate = mods[l]
        rider = _AllGather2([loss_acc]) if gw_bot_prev is None else _Direct(a2a=[gw_bot_prev])
        (dyr, dyp, gw_out, dgate, G['post_norm_g'][l]), got = _out_bwd(dxo, y, yr, yp, w_out_all[l], gate,
                                                                       vec(post_norm_g[l]), rider=rider)
        if gw_bot_prev is None:
            loss = (0.5 / D) * jnp.sum(got[0])
        else:
            recv_in[l + 1][1] = got[0]
        ((dxr, dgr, G['conv_w'][l], G['conv_b'][l], G['gate_a_w'][l], G['gate_a_b'][l], G['gate_x_w'][l],
          G['gate_x_b'][l], G['lru_lambda'][l]), (recv_out[l],)) = _rnn_bwd(
            dyr, hs, fwd, proj, conv_w_f[l], wa, wx, vec(lru_lambda[l]),
            rider=_Direct(a2a=[gw_out.reshape(NDEV, 2 * D // NDEV, D)]))
        (dxp, dgp, G['pool_w'][l], G['pool_b'][l], G['pool_scale'][l]), _ = _pool_bwd(
            dyp, proj, pw, pool_b_f[l].reshape(1, D), vec(pool_scale[l]))
        dq = (dxr, dgr, dxp, dgp)
        if l > 0:
            (gw_top,), _ = _grad_w_in(h, dq, NB, 0)
            (dxo, dshift, dscale, G['pre_norm_g'][l]), (recv_in[l][0],) = _in_bwd(
                dq, w_in_all[l], xin, dxo, vec(pre_norm_g[l]), scale1, rider=_Direct(a2a=[gw_top]))
            (gw_bot_prev,), _ = _grad_w_in(h, dq, NB, 1)
        else:
            Ge = {k: stack(k) for k in REP_EARLY + ['pool_w', 'pool_b', 'conv_w']}
            early = jnp.concatenate([_blocks_of_full(Ge['pool_w'], Ge['pool_b'], Ge['conv_w']),
                                     _pack_replicated(Ge, REP_EARLY, NDEV * PACK_ROWS).reshape(NDEV, PACK_ROWS, PACK_C)],
                                    axis=1).astype(BF16)
            (gw_top,), (early_recv,) = _grad_w_in(h, dq, NB, 0, rider=_Direct(a2a=[early]))
            early_sum = _sum_slots(early_recv)
            (gw_bot,), (recv_in[l][0],) = _grad_w_in(h, dq, NB, 1, rider=_Direct(a2a=[gw_top]))
            (dxo, dshift, dscale, G['pre_norm_g'][l]), (recv_in[l][1], early_all) = _in_bwd(
                dq, w_in_all[l], xin, dxo, vec(pre_norm_g[l]), scale1,
                rider=_Both(_Direct(a2a=[gw_bot]), _AllGather2([early_sum[PACK_ROWS:]])))
        dmod[l] = jnp.concatenate([dshift, dscale, dgate], axis=1)
    grad_x = dxo.reshape(x.shape)

    Gl = dict(ada_b=jnp.concatenate(dmod, axis=0), pre_norm_g=stack('pre_norm_g'))
    (late_slots,) = _exchange("gather_late", _AllGather2([_pack_replicated(Gl, REP_LATE, LATE_PACK_ROWS)]))
    late_sum = _sum_slots(late_slots)
    dmod_all = late_slots[:, :DEPTH * 3 * D // PACK_C]

    out = {}
    halves = lambda a: a.reshape(2 * DEPTH, D // 2, NB)
    out['w_in'], _ = _adamw("adamw_w_in", [r for l in range(DEPTH) for r in recv_in[l]], halves(w_in),
                            halves(M['w_in']), halves(V['w_in']), 256)
    out['w_out'], _ = _adamw("adamw_w_out", recv_out, w_out, M['w_out'], V['w_out'], 256)
    dm = lax.dynamic_slice_in_dim(dmod_all.reshape(NDEV, DEPTH, 3 * D), me * nc, nc, axis=2)
    out['ada_w'], _ = _ada_adamw(c_all.T, dm.transpose(1, 0, 2), ada_w, M['ada_w'], V['ada_w'])
    g_small = jnp.concatenate([early_sum[:PACK_ROWS], early_all.reshape(NDEV * PACK_ROWS, PACK_C), late_sum],
                              axis=0)

    def packs(T):
        return jnp.concatenate([_pack_sharded_block(T['pool_w'], T['pool_b'], T['conv_w']),
                                _pack_replicated(T, REP_EARLY, NDEV * PACK_ROWS),
                                _pack_replicated(T, REP_LATE, LATE_PACK_ROWS)], axis=0)[None]
    res_small, _ = _adamw("adamw_small", [g_small[None]], packs(W), packs(M), packs(V), g_small.shape[0] // 2)
    n_early = (1 + NDEV) * PACK_ROWS
    for idx in range(4):
        p = res_small[idx][0]
        pw_, pb_, cw_ = _unpack_sharded_block(p[:PACK_ROWS])
        rep = _unpack_replicated(p[PACK_ROWS:n_early], W, REP_EARLY)
        rep.update(_unpack_replicated(p[n_early:], W, REP_LATE))
        rep.update(pool_w=pw_, pool_b=pb_, conv_w=cw_)
        for k, a in rep.items():
            out.setdefault(k, [None] * 4)[idx] = a
    for k in ('w_in', 'w_out', 'ada_w'):
        out[k] = [a.reshape(W[k].shape) for a in out[k]]

    return (loss, grad_x, *[out[k][0] for k in WEIGHTS], *[out[k][1] for k in WEIGHTS],
            *[out[k][2] for k in WEIGHTS], *[out[k][3] for k in WEIGHTS])
```

```python
import functools

import jax
import jax.numpy as jnp
from jax import lax
from jax.experimental import pallas as pl
from jax.experimental.pallas import tpu as pltpu

F32, BF16 = jnp.float32, jnp.bfloat16
MESH = pl.DeviceIdType.MESH
HIGHEST = lax.Precision.HIGHEST

NDEV = 8
DEPTH = 2
D = 1024
NHEAD, HD = 8, 128
NGRP, GD = 4, 256
WINS = (2, 4, 8, 16)
CONV_K = 4
CONV_HALO = 8
POOL_HALO = 16
LRU_C = 8.0
NORM_EPS = 1e-6
ADAM_LR, ADAM_B1, ADAM_B2, ADAM_EPS, ADAM_WD, ADAM_STEP = 0.001, 0.9, 0.999, 1e-08, 0.01, 10
VMEM_LIMIT = 56 * 1024 * 1024
NQ = 4
SUB = 8
LANE = 128
PACK_C = 256
PACK_ROWS = 272

WEIGHTS = ['ada_w', 'ada_b', 'pre_norm_g', 'w_in', 'conv_w', 'conv_b', 'gate_a_w', 'gate_a_b', 'gate_x_w',
           'gate_x_b', 'lru_lambda', 'pool_w', 'pool_b', 'pool_scale', 'w_out', 'post_norm_g']
REP_EARLY = ['conv_b', 'gate_a_w', 'gate_a_b', 'gate_x_w', 'gate_x_b', 'lru_lambda', 'pool_scale', 'post_norm_g']
REP_LATE = ['ada_b', 'pre_norm_g']
LATE_PACK_ROWS = 32


def _cparams(*sem):
    return pltpu.CompilerParams(dimension_semantics=sem, vmem_limit_bytes=VMEM_LIMIT)


def _vec(l, k=None):
    if k is None:
        return pl.BlockSpec((None, 1, D), lambda *_: (l, 0, 0))
    return pl.BlockSpec((None, None, 1, D), lambda *_: (l, k, 0, 0))


def _layer(l, shape):
    nd = len(shape)
    return pl.BlockSpec((None,) + tuple(shape), lambda *_: (l,) + (0,) * nd)


def _full(shape):
    nd = len(shape)
    return pl.BlockSpec(shape, lambda *_: (0,) * nd)


def _rowsum8(z):
    return z.reshape(z.shape[0] // SUB, SUB, z.shape[1]).sum(axis=0)


def _sum8(acc):
    return jnp.sum(acc, axis=0, keepdims=True)


def _sigmoid(z):
    return 0.5 * jnp.tanh(0.5 * z) + 0.5


def _silu_parts(g):
    sg = _sigmoid(g)
    return g * sg, sg * (1.0 + g * (1.0 - sg))


def _one_minus_sq(a, log_a):
    z = 2.0 * log_a
    p = 1.0 / 24.0
    for k in (6.0, 2.0, 1.0):
        p = p * z + 1.0 / k
    return jnp.where(z > -0.03, -(p * z), 1.0 - a * a)


def _place():
    x, y, c = lax.axis_index("x"), lax.axis_index("y"), lax.axis_index("c")
    return x, y, c, 4 * x + 2 * y + c


class _Direct:
    def __init__(self, a2a=(), ag=()):
        self.arrays = list(a2a) + list(ag)
        self.n_a, self.n = len(a2a), len(self.arrays)
        self.out_shape = ([jax.ShapeDtypeStruct(a.shape, a.dtype) for a in a2a]
                          + [jax.ShapeDtypeStruct((NDEV,) + a.shape, a.dtype) for a in ag])
        self.scratch = [pltpu.SemaphoreType.DMA((self.n, NDEV - 1)), pltpu.SemaphoreType.DMA((self.n, NDEV - 1)),
                        pltpu.SemaphoreType.DMA((self.n,))]

    def _copies(self, ins, outs, sems):
        send_sems, recv_sems, local_sems = sems
        x, y, c, me = _place()
        local, remote = [], []
        for t in range(self.n):
            src = ins[t].at[me] if t < self.n_a else ins[t]
            local.append(pltpu.make_async_copy(src, outs[t].at[me], local_sems.at[t]))
        for r in range(1, NDEV):
            px = 1 - x if r & 4 else x
            py = 1 - y if r & 2 else y
            pc = 1 - c if r & 1 else c
            for t in range(self.n):
                src = ins[t].at[4 * px + 2 * py + pc] if t < self.n_a else ins[t]
                remote.append(pltpu.make_async_remote_copy(
                    src_ref=src, dst_ref=outs[t].at[me], send_sem=send_sems.at[t, r - 1],
                    recv_sem=recv_sems.at[t, r - 1], device_id=(px, py, pc), device_id_type=MESH))
        return local, remote

    def start(self, ins, outs, sems):
        local, remote = self._copies(ins, outs, sems)
        for cp in local + remote:
            cp.start()

    def finish(self, ins, outs, sems):
        local, remote = self._copies(ins, outs, sems)
        for cp in remote + local:
            cp.wait()


class _AllGather2:
    def __init__(self, arrays):
        self.arrays = list(arrays)
        self.n = len(self.arrays)
        self.out_shape = [jax.ShapeDtypeStruct((NDEV,) + a.shape, a.dtype) for a in self.arrays]
        self.scratch = [pltpu.SemaphoreType.DMA((self.n, NDEV - 1)), pltpu.SemaphoreType.DMA((self.n, NDEV - 1)),
                        pltpu.SemaphoreType.DMA((self.n,))]

    @staticmethod
    def _chips(x, y):
        return [(1 - x, y), (x, 1 - y), (1 - x, 1 - y)]

    def _copy(self, t, k, src, dst, to, sems):
        return pltpu.make_async_remote_copy(src_ref=src, dst_ref=dst, send_sem=sems[0].at[t, k],
                                            recv_sem=sems[1].at[t, k], device_id=to, device_id_type=MESH)

    def start(self, ins, outs, sems):
        x, y, c, me = _place()
        for t in range(self.n):
            pltpu.make_async_copy(ins[t], outs[t].at[me], sems[2].at[t]).start()
            self._copy(t, 0, ins[t], outs[t].at[me], (x, y, 1 - c), sems).start()
            for j, (px, py) in enumerate(self._chips(x, y)):
                self._copy(t, 1 + j, ins[t], outs[t].at[me], (px, py, c), sems).start()

    def finish(self, ins, outs, sems):
        x, y, c, me = _place()
        sib = (x, y, 1 - c)
        for j, (px, py) in enumerate(self._chips(x, y)):
            slot = 4 * px + 2 * py + c
            for t in range(self.n):
                self._copy(t, 1 + j, ins[t], outs[t].at[slot], sib, sems).wait_recv()
                self._copy(t, 4 + j, outs[t].at[slot], outs[t].at[slot], sib, sems).start()
        for t in range(self.n):
            for k in (0, 4, 5, 6):
                self._copy(t, k, ins[t], outs[t].at[me], sib, sems).wait_recv()
        for t in range(self.n):
            for k in range(NDEV - 1):
                self._copy(t, k, ins[t], outs[t].at[me], sib, sems).wait_send()
            pltpu.make_async_copy(ins[t], outs[t].at[me], sems[2].at[t]).wait()


class _Both:
    def __init__(self, *riders):
        self.riders = riders
        self.arrays = [a for r in riders for a in r.arrays]
        self.n = len(self.arrays)
        self.out_shape = [o for r in riders for o in r.out_shape]
        self.scratch = [s for r in riders for s in r.scratch]

    def _parts(self, ins, outs, sems):
        p, q = 0, 0
        for r in self.riders:
            yield r, ins[p:p + r.n], outs[p:p + r.n], sems[q:q + len(r.scratch)]
            p, q = p + r.n, q + len(r.scratch)

    def start(self, ins, outs, sems):
        for r, i, o, s in self._parts(ins, outs, sems):
            r.start(i, o, s)

    def finish(self, ins, outs, sems):
        for r, i, o, s in self._parts(ins, outs, sems):
            r.finish(i, o, s)


def _exchange(name, rider):
    n = rider.n

    def body(*refs):
        rider.start(refs[:n], refs[n:2 * n], refs[2 * n:])
        rider.finish(refs[:n], refs[n:2 * n], refs[2 * n:])

    any_spec = pl.BlockSpec(memory_space=pl.ANY)
    return list(pl.pallas_call(body, name=name, out_shape=rider.out_shape, in_specs=[any_spec] * n,
                               out_specs=[any_spec] * n, scratch_shapes=rider.scratch)(*rider.arrays))


def _pcall(body, *, name, grid, in_specs, out_specs, out_shape, args, scratch_shapes=(), rider=None):
    params = _cparams(*(("arbitrary",) * len(grid)))
    if rider is None:
        res = pl.pallas_call(body, name=name, grid=grid, in_specs=in_specs, out_specs=out_specs,
                             out_shape=out_shape, scratch_shapes=list(scratch_shapes),
                             compiler_params=params)(*args)
        return list(res), []
    n_in, n_out, n_scr, rn = len(in_specs), len(out_specs), len(scratch_shapes), rider.n

    def wrapped(*refs):
        cuts = [n_in, rn, n_out, rn, n_scr]
        parts, p = [], 0
        for n in cuts:
            parts.append(refs[p:p + n])
            p += n
        ins, r_in, outs, r_out, scr = parts
        sems = refs[p:]
        ids = [pl.program_id(a) for a in range(len(grid))]
        first = functools.reduce(jnp.logical_and, [i == 0 for i in ids])
        last = functools.reduce(jnp.logical_and, [i == g - 1 for i, g in zip(ids, grid)])

        @pl.when(first)
        def _():
            rider.start(r_in, r_out, sems)
        body(*ins, *outs, *scr)

        @pl.when(last)
        def _():
            rider.finish(r_in, r_out, sems)

    any_spec = pl.BlockSpec(memory_space=pl.ANY)
    res = pl.pallas_call(
        wrapped, name=name, grid=grid, in_specs=list(in_specs) + [any_spec] * rn,
        out_specs=list(out_specs) + [any_spec] * rn, out_shape=list(out_shape) + rider.out_shape,
        scratch_shapes=list(scratch_shapes) + rider.scratch, compiler_params=params)(*args, *rider.arrays)
    return list(res[:n_out]), list(res[n_out:])


def _mod_cols(c_all, ada_w):
    nc = ada_w.shape[2]

    def body(c_ref, w_ref, o_ref):
        cv = c_ref[...]
        ca = cv * jax.nn.sigmoid(cv)
        for l in range(DEPTH):
            o_ref[:, l * nc:(l + 1) * nc] = jnp.dot(ca, w_ref[l], precision=HIGHEST, preferred_element_type=F32)

    return pl.pallas_call(body, name="mod_cols", out_shape=jax.ShapeDtypeStruct((NDEV, DEPTH * nc), F32),
                          compiler_params=_cparams())(c_all, ada_w)


def _pre_proj(x, P, mod, w_in_l, l, rider=None):
    S = x.shape[0]
    TM = min(512, S)
    NB = w_in_l.shape[2]

    def body(x_ref, g_ref, sc_ref, sh_ref, w_ref, h_ref, p_ref):
        xv = x_ref[...]
        rstd = lax.rsqrt(jnp.mean(xv * xv, axis=-1, keepdims=True) + NORM_EPS)
        h = ((xv * rstd * g_ref[...]) * (1.0 + sc_ref[...]) + sh_ref[...]).astype(BF16)
        h_ref[...] = h
        for j in range(NDEV):
            p_ref[:, j * NB:(j + 1) * NB] = jnp.dot(h, w_ref[j], preferred_element_type=F32)

    row = pl.BlockSpec((TM, D), lambda i: (i, 0))
    return _pcall(
        body, name="pre_proj", grid=(S // TM,),
        in_specs=[row, _vec(l), _vec(l, 1), _vec(l, 0), _full((NDEV, D, NB))],
        out_specs=[row, pl.BlockSpec((TM, NDEV * NB), lambda i: (i, 0))],
        out_shape=[jax.ShapeDtypeStruct((S, D), BF16), jax.ShapeDtypeStruct((S, NDEV * NB), F32)],
        args=(x, P['pre_norm_g'], mod, mod, w_in_l), rider=rider)


def _taps(E):
    return [pltpu.roll(E, CONV_K - 1 - k, axis=0)[CONV_HALO:, :] for k in range(CONV_K - 1)] + [E[CONV_HALO:, :]]


def _conv(E, cw_ref, cb_ref):
    w = cw_ref[...]
    taps = _taps(E)
    acc = cb_ref[...] + taps[0] * w[0:1, :]
    for k in range(1, CONV_K):
        acc = acc + taps[k] * w[k:k + 1, :]
    return acc


CH_R, CH_C = 16, 512


def _chunks(T, fn):
    def step(c, carry):
        rows = pl.ds(pl.multiple_of(c * CH_R, CH_R), CH_R)
        for hf in range(D // CH_C):
            fn(rows, slice(hf * CH_C, (hf + 1) * CH_C), hf)
        return carry
    lax.fori_loop(0, T // CH_R, step, 0)


def _to_scan(ref, rows, hf, val):
    for q in range(CH_C // LANE):
        ref[hf * (CH_C // LANE) + q, rows, :] = val[:, q * LANE:(q + 1) * LANE]


def _scan(sa, sv, carry_ref, out_ref, reverse):
    NC, T, _ = sa.shape
    n8 = T // SUB
    rows = range(SUB - 2, -1, -1) if reverse else range(1, SUB)
    for cb in range(NC):
        r_in = SUB - 1 if reverse else 0
        Ap = sa[cb, pl.ds(r_in, n8, stride=SUB), :]
        Vp = sv[cb, pl.ds(r_in, n8, stride=SUB), :]
        for r in rows:
            Ar = sa[cb, pl.ds(r, n8, stride=SUB), :]
            Vp = sv[cb, pl.ds(r, n8, stride=SUB), :] + Ar * Vp
            Ap = Ar * Ap
            sa[cb, pl.ds(r, n8, stride=SUB), :] = Ap
            sv[cb, pl.ds(r, n8, stride=SUB), :] = Vp
    edge = 0 if reverse else SUB - 1

    def step(k, c):
        r0 = pl.multiple_of((n8 - 1 - k if reverse else k) * SUB, SUB)
        h = jnp.concatenate([sv[cb, pl.ds(r0, SUB), :] + sa[cb, pl.ds(r0, SUB), :] * c[:, cb * LANE:(cb + 1) * LANE]
                             for cb in range(NC)], axis=1)
        out_ref[pl.ds(r0, SUB), :] = h
        return jnp.broadcast_to(h[edge:edge + 1, :], (SUB, D))

    carry_ref[...] = lax.fori_loop(0, n8, step, carry_ref[...])


def _rnn_fwd(proj, P, l, rider=None):
    S = proj.shape[0]
    TB = min(256, S)

    def body(xr_ref, g_ref, cw_ref, cb_ref, wa_ref, ba_ref, wx_ref, bx_ref, lam_ref, hs_ref, y_ref,
             u_ref, r_ref, i_ref, a_ref, m_ref, xbuf, sa, sv, hc):
        @pl.when(pl.program_id(0) == 0)
        def _():
            xbuf[0:CONV_HALO, :] = jnp.zeros((CONV_HALO, D), F32)
            hc[...] = jnp.zeros((SUB, D), F32)
        xbuf[CONV_HALO:, :] = xr_ref[...]
        u = _conv(xbuf[...], cw_ref, cb_ref)
        xbuf[0:CONV_HALO, :] = xbuf[TB:TB + CONV_HALO, :]
        ub = u.astype(BF16)
        zr = jnp.concatenate([jnp.dot(ub[:, h * HD:(h + 1) * HD], wa_ref[h], preferred_element_type=F32)
                              for h in range(NHEAD)], axis=1)
        zi = jnp.concatenate([jnp.dot(ub[:, h * HD:(h + 1) * HD], wx_ref[h], preferred_element_type=F32)
                              for h in range(NHEAD)], axis=1)
        r = _sigmoid(zr + ba_ref[...])
        ig = _sigmoid(zi + bx_ref[...])
        log_a = r * (-LRU_C * jax.nn.softplus(-lam_ref[...]))
        a = jnp.exp(log_a)
        mult = jnp.sqrt(_one_minus_sq(a, log_a))
        v = mult * (ig * u)
        u_ref[...] = u
        r_ref[...] = r
        i_ref[...] = ig
        a_ref[...] = a
        m_ref[...] = mult
        for cb in range(D // LANE):
            sa[cb] = a[:, cb * LANE:(cb + 1) * LANE]
            sv[cb] = v[:, cb * LANE:(cb + 1) * LANE]
        _scan(sa, sv, hc, hs_ref, reverse=False)
        silu, _ = _silu_parts(g_ref[...])
        y_ref[...] = (hs_ref[...] * silu).astype(BF16)

    rowb = pl.BlockSpec((TB, D), lambda i: (i, 0))
    return _pcall(
        body, name="rnn_fwd", grid=(S // TB,),
        in_specs=[rowb, pl.BlockSpec((TB, D), lambda i: (i, 1)), _layer(l, (CONV_K, D)), _vec(l),
                  _layer(l, (NHEAD, HD, HD)), _vec(l), _layer(l, (NHEAD, HD, HD)), _vec(l), _vec(l)],
        out_specs=[rowb] * 7,
        out_shape=[jax.ShapeDtypeStruct((S, D), F32), jax.ShapeDtypeStruct((S, D), BF16)]
        + [jax.ShapeDtypeStruct((S, D), F32)] * 5,
        scratch_shapes=[pltpu.VMEM((TB + CONV_HALO, D), F32), pltpu.VMEM((D // LANE, TB, LANE), F32),
                        pltpu.VMEM((D // LANE, TB, LANE), F32), pltpu.VMEM((SUB, D), F32)],
        args=(proj, proj, P['conv_w'], P['conv_b'], P['gate_a_w'], P['gate_a_b'], P['gate_x_w'], P['gate_x_b'],
              P['lru_lambda']), rider=rider)


def _pooled(ebuf, t0, TB):
    tt = t0 + lax.broadcasted_iota(jnp.int32, (TB, 1), 0)
    pooled, inv = [], []
    for g, win in enumerate(WINS):
        Eg = ebuf[:, g * GD:(g + 1) * GD]
        L = Eg
        for lev in range(g + 1):
            L = L + pltpu.roll(L, 1 << lev, axis=0)
        icnt = 1.0 / jnp.minimum(tt + 1, win).astype(F32)
        pooled.append(L[POOL_HALO:, :] * icnt - Eg[POOL_HALO:, :])
        inv.append(icnt)
    return pooled, inv


def _pool_fwd(proj, P, l, rider=None):
    S = proj.shape[0]
    TB = min(256, S)

    def body(xp_ref, g_ref, pw_ref, pb_ref, ps_ref, y_ref, ebuf):
        i = pl.program_id(0)

        @pl.when(i == 0)
        def _():
            ebuf[0:POOL_HALO, :] = jnp.zeros((POOL_HALO, D), F32)
        ebuf[POOL_HALO:, :] = xp_ref[...]
        pooled, _ = _pooled(ebuf, i * TB, TB)
        ebuf[0:POOL_HALO, :] = ebuf[TB:TB + POOL_HALO, :]
        yp = jnp.concatenate([jnp.dot(pooled[g].astype(BF16), pw_ref[g], preferred_element_type=F32)
                              for g in range(NGRP)], axis=1) + pb_ref[...]
        silu, _ = _silu_parts(g_ref[...])
        y_ref[...] = (yp * ps_ref[...] * silu).astype(BF16)

    return _pcall(
        body, name="pool_fwd", grid=(S // TB,),
        in_specs=[pl.BlockSpec((TB, D), lambda i: (i, 2)), pl.BlockSpec((TB, D), lambda i: (i, 3)),
                  _layer(l, (NGRP, GD, GD)), _vec(l), _vec(l)],
        out_specs=[pl.BlockSpec((TB, D), lambda i: (i, 0))],
        out_shape=[jax.ShapeDtypeStruct((S, D), BF16)],
        scratch_shapes=[pltpu.VMEM((TB + POOL_HALO, D), F32)],
        args=(proj, proj, P['pool_w'], P['pool_b'], P['pool_scale']), rider=rider)


def _out_post(yr, yp, w_out_l, x, mod, P, l, target=None, rider=None):
    S = x.shape[0]
    TM = min(512, S)
    last = target is not None

    def body(*refs):
        if last:
            yr_ref, yp_ref, w_ref, x_ref, gate_ref, gp_ref, t_ref, y_ref, xo_ref, loss_ref = refs
        else:
            yr_ref, yp_ref, w_ref, x_ref, gate_ref, gp_ref, y_ref, xo_ref = refs
        acc = (jnp.dot(yr_ref[...], w_ref[0:D, :], preferred_element_type=F32)
               + jnp.dot(yp_ref[...], w_ref[D:2 * D, :], preferred_element_type=F32))
        y_ref[...] = acc
        rstd = lax.rsqrt(jnp.mean(acc * acc, axis=-1, keepdims=True) + NORM_EPS)
        xn = x_ref[...] + gate_ref[...] * (acc * rstd * gp_ref[...])
        if last:
            err = xn - t_ref[...]
            xo_ref[...] = err * (1.0 / D)

            @pl.when(pl.program_id(0) == 0)
            def _():
                loss_ref[...] = jnp.zeros((SUB, D), F32)
            loss_ref[...] += _rowsum8(err * err)
        else:
            xo_ref[...] = xn

    row = pl.BlockSpec((TM, D), lambda i: (i, 0))
    in_specs = [row, row, _full((2 * D, D)), row, _vec(l, 2), _vec(l)]
    out_specs = [row, row]
    out_shape = [jax.ShapeDtypeStruct((S, D), F32), jax.ShapeDtypeStruct((S, D), F32)]
    args = [yr, yp, w_out_l, x, mod, P['post_norm_g']]
    if last:
        in_specs.append(row)
        out_specs.append(_full((SUB, D)))
        out_shape.append(jax.ShapeDtypeStruct((SUB, D), F32))
        args.append(target)
    return _pcall(body, name="out_post_loss" if last else "out_post", grid=(S // TM,), in_specs=in_specs,
                  out_specs=out_specs, out_shape=out_shape, args=args, rider=rider)


def _out_bwd(dxo, y, yr, yp, w_out_l, mod, P, l, rider=None):
    S = y.shape[0]
    TM = min(256, S)
    nsteps = S // TM

    def body(dxo_ref, y_ref, yr_ref, yp_ref, w_ref, gate_ref, gp_ref, dyr_ref, dyp_ref, gw_ref, dgate_ref,
             dgp_ref, gw_acc, vacc):
        i = pl.program_id(0)

        @pl.when(i == 0)
        def _():
            gw_acc[...] = jnp.zeros_like(gw_acc)
            vacc[...] = jnp.zeros_like(vacc)
        yv = y_ref[...]
        dxo_v = dxo_ref[...]
        rstd = lax.rsqrt(jnp.mean(yv * yv, axis=-1, keepdims=True) + NORM_EPS)
        n = yv * rstd
        gp = gp_ref[...]
        vacc[0] += _rowsum8(dxo_v * (n * gp))
        drn = dxo_v * gate_ref[...]
        vacc[1] += _rowsum8(drn * n)
        dn = drn * gp
        dy = (rstd * (dn - n * jnp.mean(dn * n, axis=-1, keepdims=True))).astype(BF16)
        dyr_ref[...] = lax.dot_general(dy, w_ref[0:D, :], (((1,), (1,)), ((), ())), preferred_element_type=F32)
        dyp_ref[...] = lax.dot_general(dy, w_ref[D:2 * D, :], (((1,), (1,)), ((), ())),
                                       preferred_element_type=F32)
        gw_acc[0:D, :] += lax.dot_general(yr_ref[...], dy, (((0,), (0,)), ((), ())), preferred_element_type=F32)
        gw_acc[D:2 * D, :] += lax.dot_general(yp_ref[...], dy, (((0,), (0,)), ((), ())),
                                              preferred_element_type=F32)

        @pl.when(i == nsteps - 1)
        def _():
            gw_ref[...] = gw_acc[...].astype(BF16)
            dgate_ref[...] = _sum8(vacc[0])
            dgp_ref[...] = _sum8(vacc[1])

    row = pl.BlockSpec((TM, D), lambda i: (i, 0))
    return _pcall(
        body, name="out_bwd", grid=(nsteps,),
        in_specs=[row, row, row, row, _full((2 * D, D)), _vec(l, 2), _vec(l)],
        out_specs=[row, row, _full((2 * D, D)), _full((1, D)), _full((1, D))],
        out_shape=[jax.ShapeDtypeStruct((S, D), F32), jax.ShapeDtypeStruct((S, D), F32),
                   jax.ShapeDtypeStruct((2 * D, D), BF16), jax.ShapeDtypeStruct((1, D), F32),
                   jax.ShapeDtypeStruct((1, D), F32)],
        scratch_shapes=[pltpu.VMEM((2 * D, D), F32), pltpu.VMEM((2, SUB, D), F32)],
        args=(dxo, y, yr, yp, w_out_l, mod, P['post_norm_g']), rider=rider)


def _rnn_bwd(dyr, hs, fwd, proj, P, l, rider=None):
    S = proj.shape[0]
    TB = min(256, S)
    nb = S // TB
    TE = TB + CONV_HALO
    A_BA, A_BX, A_LAM, A_CB, A_CW = 0, 1, 2, 3, 4

    def blk(i):
        return nb - 1 - i

    def body(dyr_ref, hs_ref, hprev_ref, u_ref, r_ref, i_ref, a_ref, m_ref, xr_ref, g_ref, cw_ref,
             wa_ref, wx_ref, lam_ref, dxr_ref, dg_ref, gcw_ref, gcb_ref, gwa_ref, gba_ref, gwx_ref, gbx_ref,
             glam_ref, hbuf, abuf, dbuf, sa, sv, dh_ref, hp_ref, dzr_ref, dzi_ref, dhc, vacc):
        i = pl.program_id(0)
        first = blk(i) == 0

        @pl.when(i == 0)
        def _():
            abuf[TB:, :] = jnp.zeros((CONV_HALO, D), F32)
            dbuf[TB:, :] = jnp.zeros((CONV_HALO, D), F32)
            dhc[...] = jnp.zeros_like(dhc)
            vacc[...] = jnp.zeros_like(vacc)
            gwa_ref[...] = jnp.zeros_like(gwa_ref)
            gwx_ref[...] = jnp.zeros_like(gwx_ref)

        hbuf[0:CONV_HALO, :] = jnp.where(first, 0.0, hprev_ref[...])
        hbuf[CONV_HALO:, :] = hs_ref[...]
        hp_ref[...] = pltpu.roll(hbuf[...], 1, axis=0)[CONV_HALO:, :]
        abuf[0:TB, :] = a_ref[...]
        b = pltpu.roll(abuf[...], TE - 1, axis=0)[0:TB, :]
        for cb in range(D // LANE):
            sa[cb] = b[:, cb * LANE:(cb + 1) * LANE]
        abuf[TB:, :] = a_ref[0:CONV_HALO, :]

        def gate_bwd(rows, cs, hf):
            silu, dsilu = _silu_parts(g_ref[rows, cs])
            dy = dyr_ref[rows, cs]
            dg_ref[rows, cs] = (dy * hs_ref[rows, cs] * dsilu).astype(BF16)
            _to_scan(sv, rows, hf, dy * silu)
        _chunks(TB, gate_bwd)

        _scan(sa, sv, dhc, dh_ref, reverse=True)

        csp = -LRU_C * jax.nn.softplus(-lam_ref[...])

        def lru_bwd(rows, cs, hf):
            dh, a, ig, u, mult, r = dh_ref[rows, cs], a_ref[rows, cs], i_ref[rows, cs], u_ref[rows, cs], \
                m_ref[rows, cs], r_ref[rows, cs]
            dlog_a = dh * hp_ref[rows, cs] * a - (dh * ig * u) * (a * a) / mult
            dzr = dlog_a * csp[:, cs] * r * (1.0 - r)
            dzi = (dh * mult * u) * ig * (1.0 - ig)
            dzr_ref[rows, cs] = dzr.astype(BF16)
            dzi_ref[rows, cs] = dzi.astype(BF16)
            dbuf[rows, cs] = dh * mult * ig
            vacc[A_LAM, :, cs] += _rowsum8(dlog_a * r)
            vacc[A_BA, :, cs] += _rowsum8(dzr)
            vacc[A_BX, :, cs] += _rowsum8(dzi)
        _chunks(TB, lru_bwd)

        ub, dzrb, dzib = u_ref[...].astype(BF16), dzr_ref[...], dzi_ref[...]
        du_g = []
        for h in range(NHEAD):
            cs = slice(h * HD, (h + 1) * HD)
            gwa_ref[h] += lax.dot_general(ub[:, cs], dzrb[:, cs], (((0,), (0,)), ((), ())),
                                          preferred_element_type=F32)
            gwx_ref[h] += lax.dot_general(ub[:, cs], dzib[:, cs], (((0,), (0,)), ((), ())),
                                          preferred_element_type=F32)
            du_g.append(lax.dot_general(dzrb[:, cs], wa_ref[h], (((1,), (1,)), ((), ())),
                                        preferred_element_type=F32)
                        + lax.dot_general(dzib[:, cs], wx_ref[h], (((1,), (1,)), ((), ())),
                                          preferred_element_type=F32))
        du = dbuf[0:TB, :] + jnp.concatenate(du_g, axis=1)
        dbuf[0:TB, :] = du

        Dd = dbuf[...]
        w = cw_ref[...]
        xv = xr_ref[...]
        dx = du * w[CONV_K - 1:CONV_K, :]
        vacc[A_CB] += _rowsum8(du)
        vacc[A_CW + CONV_K - 1] += _rowsum8(xv * du)
        for k in range(CONV_K - 1):
            ahead = pltpu.roll(Dd, TE - (CONV_K - 1 - k), axis=0)[0:TB, :]
            dx = dx + ahead * w[k:k + 1, :]
            vacc[A_CW + k] += _rowsum8(xv * ahead)
        dxr_ref[...] = dx.astype(BF16)
        dbuf[TB:, :] = du[0:CONV_HALO, :]

        @pl.when(i == nb - 1)
        def _():
            gba_ref[...] = _sum8(vacc[A_BA])
            gbx_ref[...] = _sum8(vacc[A_BX])
            glam_ref[...] = _sum8(vacc[A_LAM]) * (LRU_C * _sigmoid(-lam_ref[...]))
            gcb_ref[...] = _sum8(vacc[A_CB])
            for k in range(CONV_K):
                gcw_ref[k:k + 1, :] = _sum8(vacc[A_CW + k])

    rowb = pl.BlockSpec((TB, D), lambda i: (blk(i), 0))
    halo = pl.BlockSpec((SUB, D), lambda i: (jnp.maximum(blk(i) * (TB // SUB) - 1, 0), 0))
    wspec = _full((NHEAD, HD, HD))
    wlay = _layer(l, (NHEAD, HD, HD))
    vec1 = _full((1, D))
    return _pcall(
        body, name="rnn_bwd", grid=(nb,),
        in_specs=[rowb, rowb, halo] + [rowb] * 5 + [rowb, pl.BlockSpec((TB, D), lambda i: (blk(i), 1)),
                                                    _layer(l, (CONV_K, D)), wlay, wlay, _vec(l)],
        out_specs=[rowb, rowb, _full((CONV_K, D)), vec1, wspec, vec1, wspec, vec1, vec1],
        out_shape=[jax.ShapeDtypeStruct((S, D), BF16), jax.ShapeDtypeStruct((S, D), BF16),
                   jax.ShapeDtypeStruct((CONV_K, D), F32), jax.ShapeDtypeStruct((1, D), F32),
                   jax.ShapeDtypeStruct((NHEAD, HD, HD), F32), jax.ShapeDtypeStruct((1, D), F32),
                   jax.ShapeDtypeStruct((NHEAD, HD, HD), F32), jax.ShapeDtypeStruct((1, D), F32),
                   jax.ShapeDtypeStruct((1, D), F32)],
        scratch_shapes=[pltpu.VMEM((TE, D), F32), pltpu.VMEM((TE, D), F32), pltpu.VMEM((TE, D), F32),
                        pltpu.VMEM((D // LANE, TB, LANE), F32), pltpu.VMEM((D // LANE, TB, LANE), F32),
                        pltpu.VMEM((TB, D), F32), pltpu.VMEM((TB, D), F32), pltpu.VMEM((TB, D), BF16),
                        pltpu.VMEM((TB, D), BF16), pltpu.VMEM((SUB, D), F32),
                        pltpu.VMEM((A_CW + CONV_K, SUB, D), F32)],
        args=(dyr, hs, hs, *fwd, proj, proj, P['conv_w'], P['gate_a_w'], P['gate_x_w'], P['lru_lambda']),
        rider=rider)


def _pool_bwd(dyp, proj, P, l, rider=None):
    S = proj.shape[0]
    TB = min(256, S)
    nb = S // TB
    TE = TB + POOL_HALO

    def blk(i):
        return nb - 1 - i

    def body(dy_ref, xp_ref, xprev_ref, g_ref, pw_ref, pb_ref, ps_ref, dxp_ref, dg_ref, gpw_ref, gpb_ref,
             gps_ref, ebuf, qbuf, vacc):
        i = pl.program_id(0)
        first = blk(i) == 0

        @pl.when(i == 0)
        def _():
            qbuf[TB:, :] = jnp.zeros((POOL_HALO, D), F32)
            vacc[...] = jnp.zeros_like(vacc)
            gpw_ref[...] = jnp.zeros_like(gpw_ref)

        ebuf[0:POOL_HALO, :] = jnp.where(first, 0.0, xprev_ref[...])
        ebuf[POOL_HALO:, :] = xp_ref[...]
        pooled, inv = _pooled(ebuf, blk(i) * TB, TB)
        pooled = [p.astype(BF16) for p in pooled]
        yp = jnp.concatenate([jnp.dot(pooled[g], pw_ref[g], preferred_element_type=F32)
                              for g in range(NGRP)], axis=1) + pb_ref[...]
        silu, dsilu = _silu_parts(g_ref[...])
        dy = dy_ref[...]
        ps = ps_ref[...]
        dyp_v = dy * ps * silu
        vacc[0] += _rowsum8(dy * yp * silu)
        vacc[1] += _rowsum8(dyp_v)
        dg_ref[...] = (dy * yp * ps * dsilu).astype(BF16)
        dypb = dyp_v.astype(BF16)
        for g in range(NGRP):
            cs = slice(g * GD, (g + 1) * GD)
            gpw_ref[g] += lax.dot_general(pooled[g], dypb[:, cs], (((0,), (0,)), ((), ())),
                                          preferred_element_type=F32)
            dpool = lax.dot_general(dypb[:, cs], pw_ref[g], (((1,), (1,)), ((), ())),
                                    preferred_element_type=F32)
            qbuf[0:TB, cs] = dpool * inv[g]
            L = qbuf[:, cs]
            for lev in range(g + 1):
                L = L + pltpu.roll(L, TE - (1 << lev), axis=0)
            dxp_ref[:, cs] = (L[0:TB, :] - dpool).astype(BF16)
        qbuf[TB:, :] = qbuf[0:POOL_HALO, :]

        @pl.when(i == nb - 1)
        def _():
            gps_ref[...] = _sum8(vacc[0])
            gpb_ref[...] = _sum8(vacc[1])

    rowb = pl.BlockSpec((TB, D), lambda i: (blk(i), 0))
    return _pcall(
        body, name="pool_bwd", grid=(nb,),
        in_specs=[rowb, pl.BlockSpec((TB, D), lambda i: (blk(i), 2)),
                  pl.BlockSpec((POOL_HALO, D), lambda i: (jnp.maximum(blk(i) * (TB // POOL_HALO) - 1, 0), 2)),
                  pl.BlockSpec((TB, D), lambda i: (blk(i), 3)), _layer(l, (NGRP, GD, GD)), _vec(l), _vec(l)],
        out_specs=[rowb, rowb, _full((NGRP, GD, GD)), _full((1, D)), _full((1, D))],
        out_shape=[jax.ShapeDtypeStruct((S, D), BF16), jax.ShapeDtypeStruct((S, D), BF16),
                   jax.ShapeDtypeStruct((NGRP, GD, GD), F32), jax.ShapeDtypeStruct((1, D), F32),
                   jax.ShapeDtypeStruct((1, D), F32)],
        scratch_shapes=[pltpu.VMEM((TE, D), F32), pltpu.VMEM((TE, D), F32), pltpu.VMEM((2, SUB, D), F32)],
        args=(dyp, proj, proj, proj, P['pool_w'], P['pool_b'], P['pool_scale']), rider=rider)


def _in_bwd(dq, w_in_l, x, dxo, P, mod, l, rider=None):
    S = x.shape[0]
    TM = min(256, S)
    NB = w_in_l.shape[2]
    nsteps = S // TM
    per_q = D // NB

    def body(d0, d1, d2, d3, w_ref, x_ref, dxo_ref, g_ref, sc_ref, dx_ref, dsh_ref, dsc_ref, dg_ref, vacc):
        i = pl.program_id(0)

        @pl.when(i == 0)
        def _():
            vacc[...] = jnp.zeros_like(vacc)
        dref = (d0, d1, d2, d3)
        dh = jnp.zeros((TM, D), F32)
        for j in range(NDEV):
            c0 = (j % per_q) * NB
            dh = dh + lax.dot_general(dref[j // per_q][:, c0:c0 + NB], w_ref[j], (((1,), (1,)), ((), ())),
                                      preferred_element_type=F32)
        xv = x_ref[...]
        rstd = lax.rsqrt(jnp.mean(xv * xv, axis=-1, keepdims=True) + NORM_EPS)
        xn = xv * rstd
        g, sc = g_ref[...], 1.0 + sc_ref[...]
        vacc[0] += _rowsum8(dh)
        vacc[1] += _rowsum8(dh * (xn * g))
        vacc[2] += _rowsum8(dh * sc * xn)
        dxn = dh * sc * g
        dx_ref[...] = dxo_ref[...] + rstd * (dxn - xn * jnp.mean(dxn * xn, axis=-1, keepdims=True))

        @pl.when(i == nsteps - 1)
        def _():
            dsh_ref[...] = _sum8(vacc[0])
            dsc_ref[...] = _sum8(vacc[1])
            dg_ref[...] = _sum8(vacc[2])

    row = pl.BlockSpec((TM, D), lambda i: (i, 0))
    return _pcall(
        body, name="in_bwd", grid=(nsteps,),
        in_specs=[row, row, row, row, _full((NDEV, D, NB)), row, row, _vec(l), _vec(l, 1)],
        out_specs=[row, _full((1, D)), _full((1, D)), _full((1, D))],
        out_shape=[jax.ShapeDtypeStruct((S, D), F32)] + [jax.ShapeDtypeStruct((1, D), F32)] * 3,
        scratch_shapes=[pltpu.VMEM((3, SUB, D), F32)],
        args=(*dq, w_in_l, x, dxo, P['pre_norm_g'], mod), rider=rider)


def _grad_w_in(h, dq, NB, part, rider=None):
    S = h.shape[0]
    TK = min(512, S)
    nk = S // TK
    RH = D // 2

    def body(h_ref, d0, d1, d2, d3, o_ref, acc):
        k = pl.program_id(0)

        @pl.when(k == 0)
        def _():
            acc[...] = jnp.zeros_like(acc)
        hv = h_ref[...]
        for q, d_ref in enumerate((d0, d1, d2, d3)):
            acc[:, q * D:(q + 1) * D] += lax.dot_general(hv, d_ref[...], (((0,), (0,)), ((), ())),
                                                         preferred_element_type=F32)

        @pl.when(k == nk - 1)
        def _():
            for j in range(NDEV):
                o_ref[j] = acc[:, j * NB:(j + 1) * NB].astype(BF16)

    row = pl.BlockSpec((TK, D), lambda k: (k, 0))
    return _pcall(
        body, name="grad_w_in", grid=(nk,),
        in_specs=[pl.BlockSpec((TK, RH), lambda k: (k, part)), row, row, row, row],
        out_specs=[_full((NDEV, RH, NB))],
        out_shape=[jax.ShapeDtypeStruct((NDEV, RH, NB), BF16)],
        scratch_shapes=[pltpu.VMEM((RH, NQ * D), F32)],
        args=(h, *dq), rider=rider)


def _adamw_math(g, w, m, v):
    m2 = ADAM_B1 * m + (1.0 - ADAM_B1) * g
    v2 = ADAM_B2 * v + (1.0 - ADAM_B2) * (g * g)
    m_hat = m2 / (1.0 - ADAM_B1 ** ADAM_STEP)
    v_hat = v2 / (1.0 - ADAM_B2 ** ADAM_STEP)
    delta = -ADAM_LR * (m_hat / (jnp.sqrt(v_hat) + ADAM_EPS) + ADAM_WD * w)
    return delta, m2, v2


def _adamw(name, gs, w, m, v, TR, rider=None):
    L = len(gs)
    n, R, C = gs[0].shape

    def body(*refs):
        g_refs = refs[:L]
        w_ref, m_ref, v_ref, go_ref, do_ref, mo_ref, vo_ref = refs[L:]
        lay = pl.program_id(0)
        for li in range(L):
            @pl.when(lay == li)
            def _(li=li):
                g = g_refs[li][0].astype(F32)
                for s in range(1, n):
                    g = g + g_refs[li][s].astype(F32)
                delta, m2, v2 = _adamw_math(g, w_ref[...], m_ref[...], v_ref[...])
                go_ref[...] = g
                do_ref[...] = delta
                mo_ref[...] = m2
                vo_ref[...] = v2

    lrc = pl.BlockSpec((None, TR, C), lambda lay, r: (lay, r, 0))
    g_specs = [pl.BlockSpec((n, TR, C), lambda lay, r, li=li: (0, jnp.where(lay == li, r, 0), 0))
               for li in range(L)]
    return _pcall(
        body, name=name, grid=(L, R // TR),
        in_specs=g_specs + [lrc, lrc, lrc], out_specs=[lrc] * 4,
        out_shape=[jax.ShapeDtypeStruct((L, R, C), F32)] * 4,
        args=(*gs, w, m, v), rider=rider)


def _ada_adamw(c_all_t, dm, w, m, v, rider=None):
    L, _, nc = w.shape

    def body(c_ref, dm_ref, w_ref, m_ref, v_ref, go_ref, do_ref, mo_ref, vo_ref):
        cv = c_ref[...]
        ca = cv * jax.nn.sigmoid(cv)
        dmv = dm_ref[...]
        g = ca[:, 0:1] * dmv[0:1, :]
        for b in range(1, NDEV):
            g = g + ca[:, b:b + 1] * dmv[b:b + 1, :]
        delta, m2, v2 = _adamw_math(g, w_ref[...], m_ref[...], v_ref[...])
        go_ref[...] = g
        do_ref[...] = delta
        mo_ref[...] = m2
        vo_ref[...] = v2

    big = pl.BlockSpec((None, D, nc), lambda lay: (lay, 0, 0))
    return _pcall(
        body, name="ada_adamw", grid=(L,),
        in_specs=[_full((D, NDEV)), pl.BlockSpec((None, NDEV, nc), lambda lay: (lay, 0, 0)), big, big, big],
        out_specs=[big] * 4, out_shape=[jax.ShapeDtypeStruct((L, D, nc), F32)] * 4,
        args=(c_all_t, dm, w, m, v), rider=rider)


def _sum_slots(recv):
    n, R, C = recv.shape

    def body(r_ref, o_ref):
        acc = r_ref[0].astype(F32)
        for s in range(1, n):
            acc = acc + r_ref[s].astype(F32)
        o_ref[...] = acc

    return pl.pallas_call(body, name="sum_slots", out_shape=jax.ShapeDtypeStruct((R, C), F32),
                          compiler_params=_cparams())(recv)


def _pad_rows(a, rows):
    return jnp.pad(a, ((0, rows - a.shape[0]), (0, 0)))


def _pack_sharded_block(pool_w, pool_b, conv_w):
    return jnp.concatenate([pool_w.reshape(-1, PACK_C), _pad_rows(pool_b.reshape(-1, PACK_C), SUB),
                            _pad_rows(conv_w.reshape(-1, PACK_C), SUB)], axis=0)


def _unpack_sharded_block(p):
    n_pw = DEPTH * NGRP * (GD // NDEV)
    pool_w = p[:n_pw].reshape(DEPTH, NGRP, GD // NDEV, GD)
    pool_b = p[n_pw].reshape(DEPTH, NGRP, GD // NDEV)
    conv_w = p[n_pw + SUB:n_pw + SUB + DEPTH * CONV_K * (D // NDEV) // PACK_C].reshape(DEPTH, CONV_K, D // NDEV)
    return pool_w, pool_b, conv_w


def _blocks_of_full(pool_w, pool_b, conv_w):
    pw = pool_w.reshape(DEPTH, NGRP, NDEV, GD // NDEV, GD).transpose(2, 0, 1, 3, 4).reshape(NDEV, -1, PACK_C)
    pb = pool_b.reshape(DEPTH, NGRP, NDEV, GD // NDEV).transpose(2, 0, 1, 3).reshape(NDEV, -1, PACK_C)
    cw = conv_w.reshape(DEPTH, CONV_K, NDEV, D // NDEV).transpose(2, 0, 1, 3).reshape(NDEV, -1, PACK_C)
    pad = lambda a: jnp.pad(a, ((0, 0), (0, SUB - a.shape[1]), (0, 0)))
    return jnp.concatenate([pw, pad(pb), pad(cw)], axis=1)


def _full_of_blocks(p):
    n_pw = DEPTH * NGRP * (GD // NDEV)
    pool_w = p[:, :n_pw].reshape(NDEV, DEPTH, NGRP, GD // NDEV, GD).transpose(1, 2, 0, 3, 4)
    pool_b = p[:, n_pw].reshape(NDEV, DEPTH, NGRP, GD // NDEV).transpose(1, 2, 0, 3)
    n_cw = DEPTH * CONV_K * (D // NDEV) // PACK_C
    conv_w = p[:, n_pw + SUB:n_pw + SUB + n_cw].reshape(NDEV, DEPTH, CONV_K, D // NDEV).transpose(1, 2, 0, 3)
    return (pool_w.reshape(DEPTH, NGRP, GD, GD), pool_b.reshape(DEPTH, NGRP, GD),
            conv_w.reshape(DEPTH, CONV_K, D))


def _pack_replicated(t, keys, rows):
    p = jnp.concatenate([t[k].reshape(-1, PACK_C) for k in keys], axis=0)
    return _pad_rows(p, rows)


def _unpack_replicated(p, like, keys):
    out, r0 = {}, 0
    for k in keys:
        rows = like[k].size // PACK_C
        out[k] = p[r0:r0 + rows].reshape(like[k].shape)
        r0 += rows
    return out


def kernel(x, c, ada_w, ada_b, pre_norm_g, w_in, conv_w, conv_b, gate_a_w, gate_a_b, gate_x_w, gate_x_b, lru_lambda, pool_w, pool_b, pool_scale, w_out, post_norm_g, loss_target, m_ada_w, m_ada_b, m_pre_norm_g, m_w_in, m_conv_w, m_conv_b, m_gate_a_w, m_gate_a_b, m_gate_x_w, m_gate_x_b, m_lru_lambda, m_pool_w, m_pool_b, m_pool_scale, m_w_out, m_post_norm_g, v_ada_w, v_ada_b, v_pre_norm_g, v_w_in, v_conv_w, v_conv_b, v_gate_a_w, v_gate_a_b, v_gate_x_w, v_gate_x_b, v_lru_lambda, v_pool_w, v_pool_b, v_pool_scale, v_w_out, v_post_norm_g):
    W = dict(ada_w=ada_w, ada_b=ada_b, pre_norm_g=pre_norm_g, w_in=w_in, conv_w=conv_w, conv_b=conv_b,
             gate_a_w=gate_a_w, gate_a_b=gate_a_b, gate_x_w=gate_x_w, gate_x_b=gate_x_b, lru_lambda=lru_lambda,
             pool_w=pool_w, pool_b=pool_b, pool_scale=pool_scale, w_out=w_out, post_norm_g=post_norm_g)
    M = dict(ada_w=m_ada_w, ada_b=m_ada_b, pre_norm_g=m_pre_norm_g, w_in=m_w_in, conv_w=m_conv_w,
             conv_b=m_conv_b, gate_a_w=m_gate_a_w, gate_a_b=m_gate_a_b, gate_x_w=m_gate_x_w,
             gate_x_b=m_gate_x_b, lru_lambda=m_lru_lambda, pool_w=m_pool_w, pool_b=m_pool_b,
             pool_scale=m_pool_scale, w_out=m_w_out, post_norm_g=m_post_norm_g)
    V = dict(ada_w=v_ada_w, ada_b=v_ada_b, pre_norm_g=v_pre_norm_g, w_in=v_w_in, conv_w=v_conv_w,
             conv_b=v_conv_b, gate_a_w=v_gate_a_w, gate_a_b=v_gate_a_b, gate_x_w=v_gate_x_w,
             gate_x_b=v_gate_x_b, lru_lambda=v_lru_lambda, pool_w=v_pool_w, pool_b=v_pool_b,
             pool_scale=v_pool_scale, w_out=v_w_out, post_norm_g=v_post_norm_g)
    S = x.shape[1]
    me = 4 * lax.axis_index("x") + 2 * lax.axis_index("y") + lax.axis_index("c")
    xs = x.reshape(S, D)
    tgt = loss_target.reshape(S, D)
    nc = ada_w.shape[2]
    NB = w_in.shape[2]
    w_in_b, w_out_b = w_in.astype(BF16), w_out.astype(BF16)
    rows = lambda a: a.reshape(DEPTH, 1, D)
    P = dict(pre_norm_g=rows(pre_norm_g), conv_b=rows(conv_b), gate_a_b=rows(gate_a_b), gate_x_b=rows(gate_x_b),
             lru_lambda=rows(lru_lambda), pool_scale=rows(pool_scale), post_norm_g=rows(post_norm_g),
             gate_a_w=gate_a_w.astype(BF16), gate_x_w=gate_x_w.astype(BF16))

    c_slots, w_in0 = _exchange("gather_c_w_in0", _AllGather2([jnp.broadcast_to(c, (SUB, D)), w_in_b[0]]))
    c_all = c_slots[:, 0, :]
    (mod_slots,) = _exchange("gather_mod", _Direct(ag=[_mod_cols(c_all, ada_w)]))
    mod = lax.dynamic_index_in_dim(mod_slots, me, axis=1, keepdims=False)
    mod = (mod.reshape(NDEV, DEPTH, nc).transpose(1, 0, 2).reshape(DEPTH, 3 * D) + ada_b).reshape(DEPTH, 3, 1, D)

    w_in_all, w_out_all = [w_in0, None], [None, None]
    saved = []
    xl = xs
    flat = lambda w_slots: w_slots.reshape(2 * D, D)
    for l in range(DEPTH):
        rider = _AllGather2([_pack_sharded_block(pool_w, pool_b, conv_w), w_out_b[0]]) if l == 0 else None
        (h, proj), got = _pre_proj(xl, P, mod, w_in_all[l], l, rider=rider)
        if l == 0:
            pool_w_f, pool_b_f, conv_w_f = _full_of_blocks(got[0])
            P.update(conv_w=conv_w_f, pool_w=pool_w_f.astype(BF16), pool_b=rows(pool_b_f))
            w_out_all[0] = flat(got[1])
        rider = _AllGather2([w_in_b[1]]) if l == 0 else None
        (hs, yr, *fwd), got = _rnn_fwd(proj, P, l, rider=rider)
        if l == 0:
            w_in_all[1] = got[0]
        (yp,), _ = _pool_fwd(proj, P, l)
        if l == DEPTH - 1:
            (y, x_next, loss_acc), _ = _out_post(yr, yp, w_out_all[l], xl, mod, P, l, tgt)
        else:
            (y, x_next), got = _out_post(yr, yp, w_out_all[l], xl, mod, P, l, rider=_AllGather2([w_out_b[1]]))
            w_out_all[1] = flat(got[0])
        saved.append((xl, h, proj, hs, fwd, yr, yp, y))
        xl = x_next

    dxo = xl
    G = {k: [None] * DEPTH for k in WEIGHTS}
    dmod = [None] * DEPTH
    recv_in, recv_out = [[None, None] for _ in range(DEPTH)], [None] * DEPTH
    full = dict(conv_w=(CONV_K, D), pool_w=(NGRP, GD, GD), pool_b=(NGRP, GD))
    stack = lambda k: jnp.stack([g.reshape(full.get(k, W[k].shape[1:])) for g in G[k]])
    gw_bot_prev = None
    for l in reversed(range(DEPTH)):
        xin, h, proj, hs, fwd, yr, yp, y = saved[l]
        rider = _AllGather2([loss_acc]) if gw_bot_prev is None else _Direct(a2a=[gw_bot_prev])
        (dyr, dyp, gw_out, dgate, G['post_norm_g'][l]), got = _out_bwd(dxo, y, yr, yp, w_out_all[l], mod, P, l,
                                                                       rider=rider)
        if gw_bot_prev is None:
            loss = (0.5 / D) * jnp.sum(got[0])
        else:
            recv_in[l + 1][1] = got[0]
        ((dxr, dgr, G['conv_w'][l], G['conv_b'][l], G['gate_a_w'][l], G['gate_a_b'][l], G['gate_x_w'][l],
          G['gate_x_b'][l], G['lru_lambda'][l]), (recv_out[l],)) = _rnn_bwd(
            dyr, hs, fwd, proj, P, l,
            rider=_Direct(a2a=[gw_out.reshape(NDEV, 2 * D // NDEV, D)]))
        (dxp, dgp, G['pool_w'][l], G['pool_b'][l], G['pool_scale'][l]), _ = _pool_bwd(dyp, proj, P, l)
        dq = (dxr, dgr, dxp, dgp)
        if l > 0:
            (gw_top,), _ = _grad_w_in(h, dq, NB, 0)
            (dxo, dshift, dscale, G['pre_norm_g'][l]), (recv_in[l][0],) = _in_bwd(
                dq, w_in_all[l], xin, dxo, P, mod, l, rider=_Direct(a2a=[gw_top]))
            (gw_bot_prev,), _ = _grad_w_in(h, dq, NB, 1)
        else:
            Ge = {k: stack(k) for k in REP_EARLY + ['pool_w', 'pool_b', 'conv_w']}
            early = jnp.concatenate([_blocks_of_full(Ge['pool_w'], Ge['pool_b'], Ge['conv_w']),
                                     _pack_replicated(Ge, REP_EARLY, NDEV * PACK_ROWS).reshape(NDEV, PACK_ROWS, PACK_C)],
                                    axis=1).astype(BF16)
            (gw_top,), (early_recv,) = _grad_w_in(h, dq, NB, 0, rider=_Direct(a2a=[early]))
            early_sum = _sum_slots(early_recv)
            (gw_bot,), (recv_in[l][0],) = _grad_w_in(h, dq, NB, 1, rider=_Direct(a2a=[gw_top]))
            (dxo, dshift, dscale, G['pre_norm_g'][l]), (recv_in[l][1], early_all) = _in_bwd(
                dq, w_in_all[l], xin, dxo, P, mod, l,
                rider=_Both(_Direct(a2a=[gw_bot]), _AllGather2([early_sum[PACK_ROWS:]])))
        dmod[l] = jnp.concatenate([dshift, dscale, dgate], axis=1)
    grad_x = dxo.reshape(x.shape)

    Gl = dict(ada_b=jnp.concatenate(dmod, axis=0), pre_norm_g=stack('pre_norm_g'))
    (late_slots,) = _exchange("gather_late", _AllGather2([_pack_replicated(Gl, REP_LATE, LATE_PACK_ROWS)]))
    late_sum = _sum_slots(late_slots)
    dmod_all = late_slots[:, :DEPTH * 3 * D // PACK_C]

    out = {}
    halves = lambda a: a.reshape(2 * DEPTH, D // 2, NB)
    out['w_in'], _ = _adamw("adamw_w_in", [r for l in range(DEPTH) for r in recv_in[l]], halves(w_in),
                            halves(M['w_in']), halves(V['w_in']), 256)
    out['w_out'], _ = _adamw("adamw_w_out", recv_out, w_out, M['w_out'], V['w_out'], 256)
    dm = lax.dynamic_slice_in_dim(dmod_all.reshape(NDEV, DEPTH, 3 * D), me * nc, nc, axis=2)
    out['ada_w'], _ = _ada_adamw(c_all.T, dm.transpose(1, 0, 2), ada_w, M['ada_w'], V['ada_w'])
    g_small = jnp.concatenate([early_sum[:PACK_ROWS], early_all.reshape(NDEV * PACK_ROWS, PACK_C), late_sum],
                              axis=0)

    def packs(T):
        return jnp.concatenate([_pack_sharded_block(T['pool_w'], T['pool_b'], T['conv_w']),
                                _pack_replicated(T, REP_EARLY, NDEV * PACK_ROWS),
                                _pack_replicated(T, REP_LATE, LATE_PACK_ROWS)], axis=0)[None]
    res_small, _ = _adamw("adamw_small", [g_small[None]], packs(W), packs(M), packs(V), g_small.shape[0] // 2)
    n_early = (1 + NDEV) * PACK_ROWS
    for idx in range(4):
        p = res_small[idx][0]
        pw_, pb_, cw_ = _unpack_sharded_block(p[:PACK_ROWS])
        rep = _unpack_replicated(p[PACK_ROWS:n_early], W, REP_EARLY)
        rep.update(_unpack_replicated(p[n_early:], W, REP_LATE))
        rep.update(pool_w=pw_, pool_b=pb_, conv_w=cw_)
        for k, a in rep.items():
            out.setdefault(k, [None] * 4)[idx] = a
    for k in ('w_in', 'w_out', 'ada_w'):
        out[k] = [a.reshape(W[k].shape) for a in out[k]]

    return (loss, grad_x, *[out[k][0] for k in WEIGHTS], *[out[k][1] for k in WEIGHTS],
            *[out[k][2] for k in WEIGHTS], *[out[k][3] for k in WEIGHTS])
```

```python
import functools

import jax
import jax.numpy as jnp
from jax import lax
from jax.experimental import pallas as pl
from jax.experimental.pallas import tpu as pltpu

F32, BF16 = jnp.float32, jnp.bfloat16
MESH = pl.DeviceIdType.MESH
HIGHEST = lax.Precision.HIGHEST

NDEV = 8
DEPTH = 2
D = 1024
NHEAD, HD = 8, 128
NGRP, GD = 4, 256
WINS = (2, 4, 8, 16)
CONV_K = 4
CONV_HALO = 8
POOL_HALO = 16
LRU_C = 8.0
NORM_EPS = 1e-6
ADAM_LR, ADAM_B1, ADAM_B2, ADAM_EPS, ADAM_WD, ADAM_STEP = 0.001, 0.9, 0.999, 1e-08, 0.01, 10
VMEM_LIMIT = 56 * 1024 * 1024
NQ = 4
SUB = 8
LANE = 128
PACK_C = 256
PACK_ROWS = 272

WEIGHTS = ['ada_w', 'ada_b', 'pre_norm_g', 'w_in', 'conv_w', 'conv_b', 'gate_a_w', 'gate_a_b', 'gate_x_w',
           'gate_x_b', 'lru_lambda', 'pool_w', 'pool_b', 'pool_scale', 'w_out', 'post_norm_g']
REP_EARLY = ['conv_b', 'gate_a_w', 'gate_a_b', 'gate_x_w', 'gate_x_b', 'lru_lambda', 'pool_scale', 'post_norm_g']
REP_LATE = ['ada_b', 'pre_norm_g']
GW_SPLIT = 384
LATE_PACK_ROWS = 32


def _cparams(*sem):
    return pltpu.CompilerParams(dimension_semantics=sem, vmem_limit_bytes=VMEM_LIMIT)


def _vec(l, k=None):
    if k is None:
        return pl.BlockSpec((None, 1, D), lambda *_: (l, 0, 0))
    return pl.BlockSpec((None, None, 1, D), lambda *_: (l, k, 0, 0))


def _layer(l, shape):
    nd = len(shape)
    return pl.BlockSpec((None,) + tuple(shape), lambda *_: (l,) + (0,) * nd)


def _full(shape):
    nd = len(shape)
    return pl.BlockSpec(shape, lambda *_: (0,) * nd)


def _rowsum8(z):
    return z.reshape(z.shape[0] // SUB, SUB, z.shape[1]).sum(axis=0)


def _sum8(acc):
    return jnp.sum(acc, axis=0, keepdims=True)


def _sigmoid(z):
    return 0.5 * jnp.tanh(0.5 * z) + 0.5


def _silu_parts(g):
    sg = _sigmoid(g)
    return g * sg, sg * (1.0 + g * (1.0 - sg))


def _one_minus_sq(a, log_a):
    z = 2.0 * log_a
    p = 1.0 / 24.0
    for k in (6.0, 2.0, 1.0):
        p = p * z + 1.0 / k
    return jnp.where(z > -0.03, -(p * z), 1.0 - a * a)


def _place():
    x, y, c = lax.axis_index("x"), lax.axis_index("y"), lax.axis_index("c")
    return x, y, c, 4 * x + 2 * y + c


class _Direct:
    def __init__(self, a2a=(), ag=()):
        self.arrays = list(a2a) + list(ag)
        self.n_a, self.n = len(a2a), len(self.arrays)
        self.out_shape = ([jax.ShapeDtypeStruct(a.shape, a.dtype) for a in a2a]
                          + [jax.ShapeDtypeStruct((NDEV,) + a.shape, a.dtype) for a in ag])
        self.scratch = [pltpu.SemaphoreType.DMA((self.n, NDEV - 1)), pltpu.SemaphoreType.DMA((self.n, NDEV - 1)),
                        pltpu.SemaphoreType.DMA((self.n,))]

    def _copies(self, ins, outs, sems):
        send_sems, recv_sems, local_sems = sems
        x, y, c, me = _place()
        local, remote = [], []
        for t in range(self.n):
            src = ins[t].at[me] if t < self.n_a else ins[t]
            local.append(pltpu.make_async_copy(src, outs[t].at[me], local_sems.at[t]))
        for r in range(1, NDEV):
            px = 1 - x if r & 4 else x
            py = 1 - y if r & 2 else y
            pc = 1 - c if r & 1 else c
            for t in range(self.n):
                src = ins[t].at[4 * px + 2 * py + pc] if t < self.n_a else ins[t]
                remote.append(pltpu.make_async_remote_copy(
                    src_ref=src, dst_ref=outs[t].at[me], send_sem=send_sems.at[t, r - 1],
                    recv_sem=recv_sems.at[t, r - 1], device_id=(px, py, pc), device_id_type=MESH))
        return local, remote

    def start(self, ins, outs, sems):
        local, remote = self._copies(ins, outs, sems)
        for cp in local + remote:
            cp.start()

    def finish(self, ins, outs, sems):
        local, remote = self._copies(ins, outs, sems)
        for cp in remote + local:
            cp.wait()


class _AllGather2:
    def __init__(self, arrays):
        self.arrays = list(arrays)
        self.n = len(self.arrays)
        self.out_shape = [jax.ShapeDtypeStruct((NDEV,) + a.shape, a.dtype) for a in self.arrays]
        self.scratch = [pltpu.SemaphoreType.DMA((self.n, NDEV - 1)), pltpu.SemaphoreType.DMA((self.n, NDEV - 1)),
                        pltpu.SemaphoreType.DMA((self.n,))]

    @staticmethod
    def _chips(x, y):
        return [(1 - x, y), (x, 1 - y), (1 - x, 1 - y)]

    def _copy(self, t, k, src, dst, to, sems):
        return pltpu.make_async_remote_copy(src_ref=src, dst_ref=dst, send_sem=sems[0].at[t, k],
                                            recv_sem=sems[1].at[t, k], device_id=to, device_id_type=MESH)

    def start(self, ins, outs, sems):
        x, y, c, me = _place()
        for t in range(self.n):
            pltpu.make_async_copy(ins[t], outs[t].at[me], sems[2].at[t]).start()
            self._copy(t, 0, ins[t], outs[t].at[me], (x, y, 1 - c), sems).start()
            for j, (px, py) in enumerate(self._chips(x, y)):
                self._copy(t, 1 + j, ins[t], outs[t].at[me], (px, py, c), sems).start()

    def finish(self, ins, outs, sems):
        x, y, c, me = _place()
        sib = (x, y, 1 - c)
        for j, (px, py) in enumerate(self._chips(x, y)):
            slot = 4 * px + 2 * py + c
            for t in range(self.n):
                self._copy(t, 1 + j, ins[t], outs[t].at[slot], sib, sems).wait_recv()
                self._copy(t, 4 + j, outs[t].at[slot], outs[t].at[slot], sib, sems).start()
        for t in range(self.n):
            for k in (0, 4, 5, 6):
                self._copy(t, k, ins[t], outs[t].at[me], sib, sems).wait_recv()
        for t in range(self.n):
            for k in range(NDEV - 1):
                self._copy(t, k, ins[t], outs[t].at[me], sib, sems).wait_send()
            pltpu.make_async_copy(ins[t], outs[t].at[me], sems[2].at[t]).wait()


class _Both:
    def __init__(self, *riders):
        self.riders = riders
        self.arrays = [a for r in riders for a in r.arrays]
        self.n = len(self.arrays)
        self.out_shape = [o for r in riders for o in r.out_shape]
        self.scratch = [s for r in riders for s in r.scratch]

    def _parts(self, ins, outs, sems):
        p, q = 0, 0
        for r in self.riders:
            yield r, ins[p:p + r.n], outs[p:p + r.n], sems[q:q + len(r.scratch)]
            p, q = p + r.n, q + len(r.scratch)

    def start(self, ins, outs, sems):
        for r, i, o, s in self._parts(ins, outs, sems):
            r.start(i, o, s)

    def finish(self, ins, outs, sems):
        for r, i, o, s in self._parts(ins, outs, sems):
            r.finish(i, o, s)


def _exchange(name, rider):
    n = rider.n

    def body(*refs):
        rider.start(refs[:n], refs[n:2 * n], refs[2 * n:])
        rider.finish(refs[:n], refs[n:2 * n], refs[2 * n:])

    any_spec = pl.BlockSpec(memory_space=pl.ANY)
    return list(pl.pallas_call(body, name=name, out_shape=rider.out_shape, in_specs=[any_spec] * n,
                               out_specs=[any_spec] * n, scratch_shapes=rider.scratch)(*rider.arrays))


def _pcall(body, *, name, grid, in_specs, out_specs, out_shape, args, scratch_shapes=(), rider=None, aliases=None):
    params = _cparams(*(("arbitrary",) * len(grid)))
    if rider is None:
        res = pl.pallas_call(body, name=name, grid=grid, in_specs=in_specs, out_specs=out_specs,
                             out_shape=out_shape, scratch_shapes=list(scratch_shapes),
                             input_output_aliases=aliases or {}, compiler_params=params)(*args)
        return list(res), []
    n_in, n_out, n_scr, rn = len(in_specs), len(out_specs), len(scratch_shapes), rider.n

    def wrapped(*refs):
        cuts = [n_in, rn, n_out, rn, n_scr]
        parts, p = [], 0
        for n in cuts:
            parts.append(refs[p:p + n])
            p += n
        ins, r_in, outs, r_out, scr = parts
        sems = refs[p:]
        ids = [pl.program_id(a) for a in range(len(grid))]
        first = functools.reduce(jnp.logical_and, [i == 0 for i in ids])
        last = functools.reduce(jnp.logical_and, [i == g - 1 for i, g in zip(ids, grid)])

        @pl.when(first)
        def _():
            rider.start(r_in, r_out, sems)
        body(*ins, *outs, *scr)

        @pl.when(last)
        def _():
            rider.finish(r_in, r_out, sems)

    any_spec = pl.BlockSpec(memory_space=pl.ANY)
    res = pl.pallas_call(
        wrapped, name=name, grid=grid, in_specs=list(in_specs) + [any_spec] * rn,
        out_specs=list(out_specs) + [any_spec] * rn, out_shape=list(out_shape) + rider.out_shape,
        scratch_shapes=list(scratch_shapes) + rider.scratch, compiler_params=params)(*args, *rider.arrays)
    return list(res[:n_out]), list(res[n_out:])


def _mod_cols(c_all, ada_w):
    nc = ada_w.shape[2]

    def body(c_ref, w_ref, o_ref):
        cv = c_ref[...]
        ca = cv * jax.nn.sigmoid(cv)
        for l in range(DEPTH):
            o_ref[:, l * nc:(l + 1) * nc] = jnp.dot(ca, w_ref[l], precision=HIGHEST, preferred_element_type=F32)

    return pl.pallas_call(body, name="mod_cols", out_shape=jax.ShapeDtypeStruct((NDEV, DEPTH * nc), F32),
                          compiler_params=_cparams())(c_all, ada_w)


def _pre_proj(x, P, mod, w_in_l, l, rider=None):
    S = x.shape[0]
    TM = min(512, S)
    NB = w_in_l.shape[2]

    def body(x_ref, g_ref, sc_ref, sh_ref, w_ref, ha_ref, hb_ref, p_ref):
        xv = x_ref[...]
        rstd = lax.rsqrt(jnp.mean(xv * xv, axis=-1, keepdims=True) + NORM_EPS)
        h = ((xv * rstd * g_ref[...]) * (1.0 + sc_ref[...]) + sh_ref[...]).astype(BF16)
        ha_ref[...] = h[:, :GW_SPLIT]
        hb_ref[...] = h[:, GW_SPLIT:]
        for j in range(NDEV):
            p_ref[:, j * NB:(j + 1) * NB] = jnp.dot(h, w_ref[j], preferred_element_type=F32)

    row = pl.BlockSpec((TM, D), lambda i: (i, 0))
    return _pcall(
        body, name="pre_proj", grid=(S // TM,),
        in_specs=[row, _vec(l), _vec(l, 1), _vec(l, 0), _full((NDEV, D, NB))],
        out_specs=[pl.BlockSpec((TM, GW_SPLIT), lambda i: (i, 0)), pl.BlockSpec((TM, D - GW_SPLIT), lambda i: (i, 0)),
                   pl.BlockSpec((TM, NDEV * NB), lambda i: (i, 0))],
        out_shape=[jax.ShapeDtypeStruct((S, GW_SPLIT), BF16), jax.ShapeDtypeStruct((S, D - GW_SPLIT), BF16),
                   jax.ShapeDtypeStruct((S, NDEV * NB), F32)],
        args=(x, P['pre_norm_g'], mod, mod, w_in_l), rider=rider)


def _taps(E):
    return [pltpu.roll(E, CONV_K - 1 - k, axis=0)[CONV_HALO:, :] for k in range(CONV_K - 1)] + [E[CONV_HALO:, :]]


def _conv(E, cw_ref, cb_ref):
    w = cw_ref[...]
    taps = _taps(E)
    acc = cb_ref[...] + taps[0] * w[0:1, :]
    for k in range(1, CONV_K):
        acc = acc + taps[k] * w[k:k + 1, :]
    return acc


CH_R, CH_C = 16, 512


def _chunks(T, fn):
    def step(c, carry):
        rows = pl.ds(pl.multiple_of(c * CH_R, CH_R), CH_R)
        for hf in range(D // CH_C):
            fn(rows, slice(hf * CH_C, (hf + 1) * CH_C), hf)
        return carry
    lax.fori_loop(0, T // CH_R, step, 0)


def _to_scan(ref, rows, hf, val):
    for q in range(CH_C // LANE):
        ref[hf * (CH_C // LANE) + q, rows, :] = val[:, q * LANE:(q + 1) * LANE]


def _scan(sa, sv, carry_ref, out_ref, reverse):
    NC, T, _ = sa.shape
    n8 = T // SUB
    rows = range(SUB - 2, -1, -1) if reverse else range(1, SUB)
    for cb in range(NC):
        r_in = SUB - 1 if reverse else 0
        Ap = sa[cb, pl.ds(r_in, n8, stride=SUB), :]
        Vp = sv[cb, pl.ds(r_in, n8, stride=SUB), :]
        for r in rows:
            Ar = sa[cb, pl.ds(r, n8, stride=SUB), :]
            Vp = sv[cb, pl.ds(r, n8, stride=SUB), :] + Ar * Vp
            Ap = Ar * Ap
            sa[cb, pl.ds(r, n8, stride=SUB), :] = Ap
            sv[cb, pl.ds(r, n8, stride=SUB), :] = Vp
    edge = 0 if reverse else SUB - 1

    def step(k, c):
        r0 = pl.multiple_of((n8 - 1 - k if reverse else k) * SUB, SUB)
        h = jnp.concatenate([sv[cb, pl.ds(r0, SUB), :] + sa[cb, pl.ds(r0, SUB), :] * c[:, cb * LANE:(cb + 1) * LANE]
                             for cb in range(NC)], axis=1)
        out_ref[pl.ds(r0, SUB), :] = h
        return jnp.broadcast_to(h[edge:edge + 1, :], (SUB, D))

    carry_ref[...] = lax.fori_loop(0, n8, step, carry_ref[...])


def _rnn_fwd(proj, P, l, rider=None):
    S = proj.shape[0]
    TB = min(256, S)

    def body(xr_ref, g_ref, cw_ref, cb_ref, wa_ref, ba_ref, wx_ref, bx_ref, lam_ref, hs_ref, y_ref,
             u_ref, r_ref, i_ref, a_ref, m_ref, xbuf, sa, sv, hc):
        @pl.when(pl.program_id(0) == 0)
        def _():
            xbuf[0:CONV_HALO, :] = jnp.zeros((CONV_HALO, D), F32)
            hc[...] = jnp.zeros((SUB, D), F32)
        xbuf[CONV_HALO:, :] = xr_ref[...]
        u = _conv(xbuf[...], cw_ref, cb_ref)
        xbuf[0:CONV_HALO, :] = xbuf[TB:TB + CONV_HALO, :]
        ub = u.astype(BF16)
        zr = jnp.concatenate([jnp.dot(ub[:, h * HD:(h + 1) * HD], wa_ref[h], preferred_element_type=F32)
                              for h in range(NHEAD)], axis=1)
        zi = jnp.concatenate([jnp.dot(ub[:, h * HD:(h + 1) * HD], wx_ref[h], preferred_element_type=F32)
                              for h in range(NHEAD)], axis=1)
        r = _sigmoid(zr + ba_ref[...])
        ig = _sigmoid(zi + bx_ref[...])
        log_a = r * (-LRU_C * jax.nn.softplus(-lam_ref[...]))
        a = jnp.exp(log_a)
        mult = jnp.sqrt(_one_minus_sq(a, log_a))
        v = mult * (ig * u)
        u_ref[...] = u
        r_ref[...] = r
        i_ref[...] = ig
        a_ref[...] = a
        m_ref[...] = mult
        for cb in range(D // LANE):
            sa[cb] = a[:, cb * LANE:(cb + 1) * LANE]
            sv[cb] = v[:, cb * LANE:(cb + 1) * LANE]
        _scan(sa, sv, hc, hs_ref, reverse=False)
        silu, _ = _silu_parts(g_ref[...])
        y_ref[...] = (hs_ref[...] * silu).astype(BF16)

    rowb = pl.BlockSpec((TB, D), lambda i: (i, 0))
    return _pcall(
        body, name="rnn_fwd", grid=(S // TB,),
        in_specs=[rowb, pl.BlockSpec((TB, D), lambda i: (i, 1)), _layer(l, (CONV_K, D)), _vec(l),
                  _layer(l, (NHEAD, HD, HD)), _vec(l), _layer(l, (NHEAD, HD, HD)), _vec(l), _vec(l)],
        out_specs=[rowb] * 7,
        out_shape=[jax.ShapeDtypeStruct((S, D), F32), jax.ShapeDtypeStruct((S, D), BF16)]
        + [jax.ShapeDtypeStruct((S, D), F32)] * 5,
        scratch_shapes=[pltpu.VMEM((TB + CONV_HALO, D), F32), pltpu.VMEM((D // LANE, TB, LANE), F32),
                        pltpu.VMEM((D // LANE, TB, LANE), F32), pltpu.VMEM((SUB, D), F32)],
        args=(proj, proj, P['conv_w'], P['conv_b'], P['gate_a_w'], P['gate_a_b'], P['gate_x_w'], P['gate_x_b'],
              P['lru_lambda']), rider=rider)


def _pooled(ebuf, t0, TB):
    tt = t0 + lax.broadcasted_iota(jnp.int32, (TB, 1), 0)
    pooled, inv = [], []
    for g, win in enumerate(WINS):
        Eg = ebuf[:, g * GD:(g + 1) * GD]
        L = Eg
        for lev in range(g + 1):
            L = L + pltpu.roll(L, 1 << lev, axis=0)
        icnt = 1.0 / jnp.minimum(tt + 1, win).astype(F32)
        pooled.append(L[POOL_HALO:, :] * icnt - Eg[POOL_HALO:, :])
        inv.append(icnt)
    return pooled, inv


def _pool_fwd(proj, P, l, rider=None):
    S = proj.shape[0]
    TB = min(256, S)

    def body(xp_ref, g_ref, pw_ref, pb_ref, ps_ref, y_ref, ebuf):
        i = pl.program_id(0)

        @pl.when(i == 0)
        def _():
            ebuf[0:POOL_HALO, :] = jnp.zeros((POOL_HALO, D), F32)
        ebuf[POOL_HALO:, :] = xp_ref[...]
        pooled, _ = _pooled(ebuf, i * TB, TB)
        ebuf[0:POOL_HALO, :] = ebuf[TB:TB + POOL_HALO, :]
        yp = jnp.concatenate([jnp.dot(pooled[g].astype(BF16), pw_ref[g], preferred_element_type=F32)
                              for g in range(NGRP)], axis=1) + pb_ref[...]
        silu, _ = _silu_parts(g_ref[...])
        y_ref[...] = (yp * ps_ref[...] * silu).astype(BF16)

    return _pcall(
        body, name="pool_fwd", grid=(S // TB,),
        in_specs=[pl.BlockSpec((TB, D), lambda i: (i, 2)), pl.BlockSpec((TB, D), lambda i: (i, 3)),
                  _layer(l, (NGRP, GD, GD)), _vec(l), _vec(l)],
        out_specs=[pl.BlockSpec((TB, D), lambda i: (i, 0))],
        out_shape=[jax.ShapeDtypeStruct((S, D), BF16)],
        scratch_shapes=[pltpu.VMEM((TB + POOL_HALO, D), F32)],
        args=(proj, proj, P['pool_w'], P['pool_b'], P['pool_scale']), rider=rider)


def _out_post(yr, yp, w_out_l, x, mod, P, l, target=None, rider=None):
    S = x.shape[0]
    TM = min(512, S)
    last = target is not None

    def body(*refs):
        if last:
            yr_ref, yp_ref, w_ref, x_ref, gate_ref, gp_ref, t_ref, y_ref, xo_ref, loss_ref = refs
        else:
            yr_ref, yp_ref, w_ref, x_ref, gate_ref, gp_ref, y_ref, xo_ref = refs
        acc = (jnp.dot(yr_ref[...], w_ref[0:D, :], preferred_element_type=F32)
               + jnp.dot(yp_ref[...], w_ref[D:2 * D, :], preferred_element_type=F32))
        y_ref[...] = acc
        rstd = lax.rsqrt(jnp.mean(acc * acc, axis=-1, keepdims=True) + NORM_EPS)
        xn = x_ref[...] + gate_ref[...] * (acc * rstd * gp_ref[...])
        if last:
            err = xn - t_ref[...]
            xo_ref[...] = err * (1.0 / D)

            @pl.when(pl.program_id(0) == 0)
            def _():
                loss_ref[...] = jnp.zeros((SUB, D), F32)
            loss_ref[...] += _rowsum8(err * err)
        else:
            xo_ref[...] = xn

    row = pl.BlockSpec((TM, D), lambda i: (i, 0))
    in_specs = [row, row, _full((2 * D, D)), row, _vec(l, 2), _vec(l)]
    out_specs = [row, row]
    out_shape = [jax.ShapeDtypeStruct((S, D), F32), jax.ShapeDtypeStruct((S, D), F32)]
    args = [yr, yp, w_out_l, x, mod, P['post_norm_g']]
    if last:
        in_specs.append(row)
        out_specs.append(_full((SUB, D)))
        out_shape.append(jax.ShapeDtypeStruct((SUB, D), F32))
        args.append(target)
    return _pcall(body, name="out_post_loss" if last else "out_post", grid=(S // TM,), in_specs=in_specs,
                  out_specs=out_specs, out_shape=out_shape, args=args, rider=rider)


def _out_bwd(dxo, y, yr, yp, w_out_l, mod, P, l, rider=None):
    S = y.shape[0]
    TM = min(256, S)
    nsteps = S // TM

    def body(dxo_ref, y_ref, yr_ref, yp_ref, w_ref, gate_ref, gp_ref, dyr_ref, dyp_ref, gw_ref, dgate_ref,
             dgp_ref, gw_acc, vacc):
        i = pl.program_id(0)

        @pl.when(i == 0)
        def _():
            gw_acc[...] = jnp.zeros_like(gw_acc)
            vacc[...] = jnp.zeros_like(vacc)
        yv = y_ref[...]
        dxo_v = dxo_ref[...]
        rstd = lax.rsqrt(jnp.mean(yv * yv, axis=-1, keepdims=True) + NORM_EPS)
        n = yv * rstd
        gp = gp_ref[...]
        vacc[0] += _rowsum8(dxo_v * (n * gp))
        drn = dxo_v * gate_ref[...]
        vacc[1] += _rowsum8(drn * n)
        dn = drn * gp
        dy = (rstd * (dn - n * jnp.mean(dn * n, axis=-1, keepdims=True))).astype(BF16)
        dyr_ref[...] = lax.dot_general(dy, w_ref[0:D, :], (((1,), (1,)), ((), ())), preferred_element_type=F32)
        dyp_ref[...] = lax.dot_general(dy, w_ref[D:2 * D, :], (((1,), (1,)), ((), ())),
                                       preferred_element_type=F32)
        gw_acc[0:D, :] += lax.dot_general(yr_ref[...], dy, (((0,), (0,)), ((), ())), preferred_element_type=F32)
        gw_acc[D:2 * D, :] += lax.dot_general(yp_ref[...], dy, (((0,), (0,)), ((), ())),
                                              preferred_element_type=F32)

        @pl.when(i == nsteps - 1)
        def _():
            gw_ref[...] = gw_acc[...].astype(BF16)
            dgate_ref[...] = _sum8(vacc[0])
            dgp_ref[...] = _sum8(vacc[1])

    row = pl.BlockSpec((TM, D), lambda i: (i, 0))
    return _pcall(
        body, name="out_bwd", grid=(nsteps,),
        in_specs=[row, row, row, row, _full((2 * D, D)), _vec(l, 2), _vec(l)],
        out_specs=[row, row, _full((2 * D, D)), _full((1, D)), _full((1, D))],
        out_shape=[jax.ShapeDtypeStruct((S, D), F32), jax.ShapeDtypeStruct((S, D), F32),
                   jax.ShapeDtypeStruct((2 * D, D), BF16), jax.ShapeDtypeStruct((1, D), F32),
                   jax.ShapeDtypeStruct((1, D), F32)],
        scratch_shapes=[pltpu.VMEM((2 * D, D), F32), pltpu.VMEM((2, SUB, D), F32)],
        args=(dxo, y, yr, yp, w_out_l, mod, P['post_norm_g']), rider=rider)


def _rnn_bwd(dyr, hs, fwd, proj, P, l, rider=None):
    S = proj.shape[0]
    TB = min(256, S)
    nb = S // TB
    TE = TB + CONV_HALO
    A_BA, A_BX, A_LAM, A_CB, A_CW = 0, 1, 2, 3, 4

    def blk(i):
        return nb - 1 - i

    def body(dyr_ref, hs_ref, hprev_ref, u_ref, r_ref, i_ref, a_ref, m_ref, xr_ref, g_ref, cw_ref,
             wa_ref, wx_ref, lam_ref, dxr_ref, dg_ref, gcw_ref, gcb_ref, gwa_ref, gba_ref, gwx_ref, gbx_ref,
             glam_ref, hbuf, abuf, dbuf, sa, sv, dh_ref, hp_ref, dzr_ref, dzi_ref, dhc, vacc):
        i = pl.program_id(0)
        first = blk(i) == 0

        @pl.when(i == 0)
        def _():
            abuf[TB:, :] = jnp.zeros((CONV_HALO, D), F32)
            dbuf[TB:, :] = jnp.zeros((CONV_HALO, D), F32)
            dhc[...] = jnp.zeros_like(dhc)
            vacc[...] = jnp.zeros_like(vacc)
            gwa_ref[...] = jnp.zeros_like(gwa_ref)
            gwx_ref[...] = jnp.zeros_like(gwx_ref)

        hbuf[0:CONV_HALO, :] = jnp.where(first, 0.0, hprev_ref[...])
        hbuf[CONV_HALO:, :] = hs_ref[...]
        hp_ref[...] = pltpu.roll(hbuf[...], 1, axis=0)[CONV_HALO:, :]
        abuf[0:TB, :] = a_ref[...]
        b = pltpu.roll(abuf[...], TE - 1, axis=0)[0:TB, :]
        for cb in range(D // LANE):
            sa[cb] = b[:, cb * LANE:(cb + 1) * LANE]
        abuf[TB:, :] = a_ref[0:CONV_HALO, :]

        def gate_bwd(rows, cs, hf):
            silu, dsilu = _silu_parts(g_ref[rows, cs])
            dy = dyr_ref[rows, cs]
            dg_ref[rows, cs] = (dy * hs_ref[rows, cs] * dsilu).astype(BF16)
            _to_scan(sv, rows, hf, dy * silu)
        _chunks(TB, gate_bwd)

        _scan(sa, sv, dhc, dh_ref, reverse=True)

        csp = -LRU_C * jax.nn.softplus(-lam_ref[...])

        def lru_bwd(rows, cs, hf):
            dh, a, ig, u, mult, r = dh_ref[rows, cs], a_ref[rows, cs], i_ref[rows, cs], u_ref[rows, cs], \
                m_ref[rows, cs], r_ref[rows, cs]
            dlog_a = dh * hp_ref[rows, cs] * a - (dh * ig * u) * (a * a) / mult
            dzr = dlog_a * csp[:, cs] * r * (1.0 - r)
            dzi = (dh * mult * u) * ig * (1.0 - ig)
            dzr_ref[rows, cs] = dzr.astype(BF16)
            dzi_ref[rows, cs] = dzi.astype(BF16)
            dbuf[rows, cs] = dh * mult * ig
            vacc[A_LAM, :, cs] += _rowsum8(dlog_a * r)
            vacc[A_BA, :, cs] += _rowsum8(dzr)
            vacc[A_BX, :, cs] += _rowsum8(dzi)
        _chunks(TB, lru_bwd)

        ub, dzrb, dzib = u_ref[...].astype(BF16), dzr_ref[...], dzi_ref[...]
        du_g = []
        for h in range(NHEAD):
            cs = slice(h * HD, (h + 1) * HD)
            gwa_ref[h] += lax.dot_general(ub[:, cs], dzrb[:, cs], (((0,), (0,)), ((), ())),
                                          preferred_element_type=F32)
            gwx_ref[h] += lax.dot_general(ub[:, cs], dzib[:, cs], (((0,), (0,)), ((), ())),
                                          preferred_element_type=F32)
            du_g.append(lax.dot_general(dzrb[:, cs], wa_ref[h], (((1,), (1,)), ((), ())),
                                        preferred_element_type=F32)
                        + lax.dot_general(dzib[:, cs], wx_ref[h], (((1,), (1,)), ((), ())),
                                          preferred_element_type=F32))
        du = dbuf[0:TB, :] + jnp.concatenate(du_g, axis=1)
        dbuf[0:TB, :] = du

        Dd = dbuf[...]
        w = cw_ref[...]
        xv = xr_ref[...]
        dx = du * w[CONV_K - 1:CONV_K, :]
        vacc[A_CB] += _rowsum8(du)
        vacc[A_CW + CONV_K - 1] += _rowsum8(xv * du)
        for k in range(CONV_K - 1):
            ahead = pltpu.roll(Dd, TE - (CONV_K - 1 - k), axis=0)[0:TB, :]
            dx = dx + ahead * w[k:k + 1, :]
            vacc[A_CW + k] += _rowsum8(xv * ahead)
        dxr_ref[...] = dx.astype(BF16)
        dbuf[TB:, :] = du[0:CONV_HALO, :]

        @pl.when(i == nb - 1)
        def _():
            gba_ref[...] = _sum8(vacc[A_BA])
            gbx_ref[...] = _sum8(vacc[A_BX])
            glam_ref[...] = _sum8(vacc[A_LAM]) * (LRU_C * _sigmoid(-lam_ref[...]))
            gcb_ref[...] = _sum8(vacc[A_CB])
            for k in range(CONV_K):
                gcw_ref[k:k + 1, :] = _sum8(vacc[A_CW + k])

    rowb = pl.BlockSpec((TB, D), lambda i: (blk(i), 0))
    halo = pl.BlockSpec((SUB, D), lambda i: (jnp.maximum(blk(i) * (TB // SUB) - 1, 0), 0))
    wspec = _full((NHEAD, HD, HD))
    wlay = _layer(l, (NHEAD, HD, HD))
    vec1 = _full((1, D))
    return _pcall(
        body, name="rnn_bwd", grid=(nb,),
        in_specs=[rowb, rowb, halo] + [rowb] * 5 + [rowb, pl.BlockSpec((TB, D), lambda i: (blk(i), 1)),
                                                    _layer(l, (CONV_K, D)), wlay, wlay, _vec(l)],
        out_specs=[rowb, rowb, _full((CONV_K, D)), vec1, wspec, vec1, wspec, vec1, vec1],
        out_shape=[jax.ShapeDtypeStruct((S, D), BF16), jax.ShapeDtypeStruct((S, D), BF16),
                   jax.ShapeDtypeStruct((CONV_K, D), F32), jax.ShapeDtypeStruct((1, D), F32),
                   jax.ShapeDtypeStruct((NHEAD, HD, HD), F32), jax.ShapeDtypeStruct((1, D), F32),
                   jax.ShapeDtypeStruct((NHEAD, HD, HD), F32), jax.ShapeDtypeStruct((1, D), F32),
                   jax.ShapeDtypeStruct((1, D), F32)],
        scratch_shapes=[pltpu.VMEM((TE, D), F32), pltpu.VMEM((TE, D), F32), pltpu.VMEM((TE, D), F32),
                        pltpu.VMEM((D // LANE, TB, LANE), F32), pltpu.VMEM((D // LANE, TB, LANE), F32),
                        pltpu.VMEM((TB, D), F32), pltpu.VMEM((TB, D), F32), pltpu.VMEM((TB, D), BF16),
                        pltpu.VMEM((TB, D), BF16), pltpu.VMEM((SUB, D), F32),
                        pltpu.VMEM((A_CW + CONV_K, SUB, D), F32)],
        args=(dyr, hs, hs, *fwd, proj, proj, P['conv_w'], P['gate_a_w'], P['gate_x_w'], P['lru_lambda']),
        rider=rider)


def _pool_bwd(dyp, proj, P, l, rider=None):
    S = proj.shape[0]
    TB = min(256, S)
    nb = S // TB
    TE = TB + POOL_HALO

    def blk(i):
        return nb - 1 - i

    def body(dy_ref, xp_ref, xprev_ref, g_ref, pw_ref, pb_ref, ps_ref, dxp_ref, dg_ref, gpw_ref, gpb_ref,
             gps_ref, ebuf, qbuf, vacc):
        i = pl.program_id(0)
        first = blk(i) == 0

        @pl.when(i == 0)
        def _():
            qbuf[TB:, :] = jnp.zeros((POOL_HALO, D), F32)
            vacc[...] = jnp.zeros_like(vacc)
            gpw_ref[...] = jnp.zeros_like(gpw_ref)

        ebuf[0:POOL_HALO, :] = jnp.where(first, 0.0, xprev_ref[...])
        ebuf[POOL_HALO:, :] = xp_ref[...]
        pooled, inv = _pooled(ebuf, blk(i) * TB, TB)
        pooled = [p.astype(BF16) for p in pooled]
        yp = jnp.concatenate([jnp.dot(pooled[g], pw_ref[g], preferred_element_type=F32)
                              for g in range(NGRP)], axis=1) + pb_ref[...]
        silu, dsilu = _silu_parts(g_ref[...])
        dy = dy_ref[...]
        ps = ps_ref[...]
        dyp_v = dy * ps * silu
        vacc[0] += _rowsum8(dy * yp * silu)
        vacc[1] += _rowsum8(dyp_v)
        dg_ref[...] = (dy * yp * ps * dsilu).astype(BF16)
        dypb = dyp_v.astype(BF16)
        for g in range(NGRP):
            cs = slice(g * GD, (g + 1) * GD)
            gpw_ref[g] += lax.dot_general(pooled[g], dypb[:, cs], (((0,), (0,)), ((), ())),
                                          preferred_element_type=F32)
            dpool = lax.dot_general(dypb[:, cs], pw_ref[g], (((1,), (1,)), ((), ())),
                                    preferred_element_type=F32)
            qbuf[0:TB, cs] = dpool * inv[g]
            L = qbuf[:, cs]
            for lev in range(g + 1):
                L = L + pltpu.roll(L, TE - (1 << lev), axis=0)
            dxp_ref[:, cs] = (L[0:TB, :] - dpool).astype(BF16)
        qbuf[TB:, :] = qbuf[0:POOL_HALO, :]

        @pl.when(i == nb - 1)
        def _():
            gps_ref[...] = _sum8(vacc[0])
            gpb_ref[...] = _sum8(vacc[1])

    rowb = pl.BlockSpec((TB, D), lambda i: (blk(i), 0))
    return _pcall(
        body, name="pool_bwd", grid=(nb,),
        in_specs=[rowb, pl.BlockSpec((TB, D), lambda i: (blk(i), 2)),
                  pl.BlockSpec((POOL_HALO, D), lambda i: (jnp.maximum(blk(i) * (TB // POOL_HALO) - 1, 0), 2)),
                  pl.BlockSpec((TB, D), lambda i: (blk(i), 3)), _layer(l, (NGRP, GD, GD)), _vec(l), _vec(l)],
        out_specs=[rowb, rowb, _full((NGRP, GD, GD)), _full((1, D)), _full((1, D))],
        out_shape=[jax.ShapeDtypeStruct((S, D), BF16), jax.ShapeDtypeStruct((S, D), BF16),
                   jax.ShapeDtypeStruct((NGRP, GD, GD), F32), jax.ShapeDtypeStruct((1, D), F32),
                   jax.ShapeDtypeStruct((1, D), F32)],
        scratch_shapes=[pltpu.VMEM((TE, D), F32), pltpu.VMEM((TE, D), F32), pltpu.VMEM((2, SUB, D), F32)],
        args=(dyp, proj, proj, proj, P['pool_w'], P['pool_b'], P['pool_scale']), rider=rider)


def _in_bwd(dq, w_in_l, x, dxo, P, mod, l, rider=None):
    S = x.shape[0]
    TM = min(256, S)
    NB = w_in_l.shape[2]
    nsteps = S // TM
    per_q = D // NB

    def body(d0, d1, d2, d3, w_ref, x_ref, dxo_ref, g_ref, sc_ref, dx_ref, dsh_ref, dsc_ref, dg_ref, vacc):
        i = pl.program_id(0)

        @pl.when(i == 0)
        def _():
            vacc[...] = jnp.zeros_like(vacc)
        dref = (d0, d1, d2, d3)
        dh = jnp.zeros((TM, D), F32)
        for j in range(NDEV):
            c0 = (j % per_q) * NB
            dh = dh + lax.dot_general(dref[j // per_q][:, c0:c0 + NB], w_ref[j], (((1,), (1,)), ((), ())),
                                      preferred_element_type=F32)
        xv = x_ref[...]
        rstd = lax.rsqrt(jnp.mean(xv * xv, axis=-1, keepdims=True) + NORM_EPS)
        xn = xv * rstd
        g, sc = g_ref[...], 1.0 + sc_ref[...]
        vacc[0] += _rowsum8(dh)
        vacc[1] += _rowsum8(dh * (xn * g))
        vacc[2] += _rowsum8(dh * sc * xn)
        dxn = dh * sc * g
        dx_ref[...] = dxo_ref[...] + rstd * (dxn - xn * jnp.mean(dxn * xn, axis=-1, keepdims=True))

        @pl.when(i == nsteps - 1)
        def _():
            dsh_ref[...] = _sum8(vacc[0])
            dsc_ref[...] = _sum8(vacc[1])
            dg_ref[...] = _sum8(vacc[2])

    row = pl.BlockSpec((TM, D), lambda i: (i, 0))
    return _pcall(
        body, name="in_bwd", grid=(nsteps,),
        in_specs=[row, row, row, row, _full((NDEV, D, NB)), row, row, _vec(l), _vec(l, 1)],
        out_specs=[row, _full((1, D)), _full((1, D)), _full((1, D))],
        out_shape=[jax.ShapeDtypeStruct((S, D), F32)] + [jax.ShapeDtypeStruct((1, D), F32)] * 3,
        scratch_shapes=[pltpu.VMEM((3, SUB, D), F32)],
        args=(*dq, w_in_l, x, dxo, P['pre_norm_g'], mod), rider=rider)


def _grad_w_in(h, dq, NB, rider=None):
    S, RH = h.shape
    TK = min(512, S)
    nk = S // TK

    def body(h_ref, d0, d1, d2, d3, o_ref, acc):
        k = pl.program_id(0)

        @pl.when(k == 0)
        def _():
            acc[...] = jnp.zeros_like(acc)
        hv = h_ref[...]
        for q, d_ref in enumerate((d0, d1, d2, d3)):
            acc[:, q * D:(q + 1) * D] += lax.dot_general(hv, d_ref[...], (((0,), (0,)), ((), ())),
                                                         preferred_element_type=F32)

        @pl.when(k == nk - 1)
        def _():
            for j in range(NDEV):
                o_ref[j] = acc[:, j * NB:(j + 1) * NB].astype(BF16)

    row = pl.BlockSpec((TK, D), lambda k: (k, 0))
    return _pcall(
        body, name="grad_w_in", grid=(nk,),
        in_specs=[pl.BlockSpec((TK, RH), lambda k: (k, 0)), row, row, row, row],
        out_specs=[_full((NDEV, RH, NB))],
        out_shape=[jax.ShapeDtypeStruct((NDEV, RH, NB), BF16)],
        scratch_shapes=[pltpu.VMEM((RH, NQ * D), F32)],
        args=(h, *dq), rider=rider)


def _adamw_math(g, w, m, v):
    m2 = ADAM_B1 * m + (1.0 - ADAM_B1) * g
    v2 = ADAM_B2 * v + (1.0 - ADAM_B2) * (g * g)
    m_hat = m2 / (1.0 - ADAM_B1 ** ADAM_STEP)
    v_hat = v2 / (1.0 - ADAM_B2 ** ADAM_STEP)
    delta = -ADAM_LR * (m_hat / (jnp.sqrt(v_hat) + ADAM_EPS) + ADAM_WD * w)
    return delta, m2, v2


def _adamw(name, gs, w, m, v, TR, row0=0, into=None):
    L = len(gs)
    n, Rp, C = gs[0].shape
    R = w.shape[1]
    b0 = row0 // TR

    def body(*refs):
        g_refs = refs[:L]
        w_ref, m_ref, v_ref = refs[L:L + 3]
        go_ref, do_ref, mo_ref, vo_ref = refs[-4:]
        lay = pl.program_id(0)
        for li in range(L):
            @pl.when(lay == li)
            def _(li=li):
                g = g_refs[li][0].astype(F32)
                for s in range(1, n):
                    g = g + g_refs[li][s].astype(F32)
                delta, m2, v2 = _adamw_math(g, w_ref[...], m_ref[...], v_ref[...])
                go_ref[...] = g
                do_ref[...] = delta
                mo_ref[...] = m2
                vo_ref[...] = v2

    lrc = pl.BlockSpec((None, TR, C), lambda lay, r: (lay, r + b0, 0))
    g_specs = [pl.BlockSpec((n, TR, C), lambda lay, r, li=li: (0, jnp.where(lay == li, r, 0), 0))
               for li in range(L)]
    in_specs, args, aliases = g_specs + [lrc, lrc, lrc], [*gs, w, m, v], None
    if into is not None:
        aliases = {len(args) + k: k for k in range(4)}
        in_specs = in_specs + [pl.BlockSpec(memory_space=pl.ANY)] * 4
        args = args + list(into)
    return _pcall(
        body, name=name, grid=(L, Rp // TR), in_specs=in_specs, out_specs=[lrc] * 4,
        out_shape=[jax.ShapeDtypeStruct((L, R, C), F32)] * 4, args=args, aliases=aliases)


def _ada_adamw(c_all_t, dm, w, m, v, rider=None):
    L, _, nc = w.shape

    def body(c_ref, dm_ref, w_ref, m_ref, v_ref, go_ref, do_ref, mo_ref, vo_ref):
        cv = c_ref[...]
        ca = cv * jax.nn.sigmoid(cv)
        dmv = dm_ref[...]
        g = ca[:, 0:1] * dmv[0:1, :]
        for b in range(1, NDEV):
            g = g + ca[:, b:b + 1] * dmv[b:b + 1, :]
        delta, m2, v2 = _adamw_math(g, w_ref[...], m_ref[...], v_ref[...])
        go_ref[...] = g
        do_ref[...] = delta
        mo_ref[...] = m2
        vo_ref[...] = v2

    big = pl.BlockSpec((None, D, nc), lambda lay: (lay, 0, 0))
    return _pcall(
        body, name="ada_adamw", grid=(L,),
        in_specs=[_full((D, NDEV)), pl.BlockSpec((None, NDEV, nc), lambda lay: (lay, 0, 0)), big, big, big],
        out_specs=[big] * 4, out_shape=[jax.ShapeDtypeStruct((L, D, nc), F32)] * 4,
        args=(c_all_t, dm, w, m, v), rider=rider)


def _sum_slots(recv):
    n, R, C = recv.shape

    def body(r_ref, o_ref):
        acc = r_ref[0].astype(F32)
        for s in range(1, n):
            acc = acc + r_ref[s].astype(F32)
        o_ref[...] = acc

    return pl.pallas_call(body, name="sum_slots", out_shape=jax.ShapeDtypeStruct((R, C), F32),
                          compiler_params=_cparams())(recv)


def _pad_rows(a, rows):
    return jnp.pad(a, ((0, rows - a.shape[0]), (0, 0)))


def _pack_sharded_block(pool_w, pool_b, conv_w):
    return jnp.concatenate([pool_w.reshape(-1, PACK_C), _pad_rows(pool_b.reshape(-1, PACK_C), SUB),
                            _pad_rows(conv_w.reshape(-1, PACK_C), SUB)], axis=0)


def _unpack_sharded_block(p):
    n_pw = DEPTH * NGRP * (GD // NDEV)
    pool_w = p[:n_pw].reshape(DEPTH, NGRP, GD // NDEV, GD)
    pool_b = p[n_pw].reshape(DEPTH, NGRP, GD // NDEV)
    conv_w = p[n_pw + SUB:n_pw + SUB + DEPTH * CONV_K * (D // NDEV) // PACK_C].reshape(DEPTH, CONV_K, D // NDEV)
    return pool_w, pool_b, conv_w


def _blocks_of_full(pool_w, pool_b, conv_w):
    pw = pool_w.reshape(DEPTH, NGRP, NDEV, GD // NDEV, GD).transpose(2, 0, 1, 3, 4).reshape(NDEV, -1, PACK_C)
    pb = pool_b.reshape(DEPTH, NGRP, NDEV, GD // NDEV).transpose(2, 0, 1, 3).reshape(NDEV, -1, PACK_C)
    cw = conv_w.reshape(DEPTH, CONV_K, NDEV, D // NDEV).transpose(2, 0, 1, 3).reshape(NDEV, -1, PACK_C)
    pad = lambda a: jnp.pad(a, ((0, 0), (0, SUB - a.shape[1]), (0, 0)))
    return jnp.concatenate([pw, pad(pb), pad(cw)], axis=1)


def _full_of_blocks(p):
    n_pw = DEPTH * NGRP * (GD // NDEV)
    pool_w = p[:, :n_pw].reshape(NDEV, DEPTH, NGRP, GD // NDEV, GD).transpose(1, 2, 0, 3, 4)
    pool_b = p[:, n_pw].reshape(NDEV, DEPTH, NGRP, GD // NDEV).transpose(1, 2, 0, 3)
    n_cw = DEPTH * CONV_K * (D // NDEV) // PACK_C
    conv_w = p[:, n_pw + SUB:n_pw + SUB + n_cw].reshape(NDEV, DEPTH, CONV_K, D // NDEV).transpose(1, 2, 0, 3)
    return (pool_w.reshape(DEPTH, NGRP, GD, GD), pool_b.reshape(DEPTH, NGRP, GD),
            conv_w.reshape(DEPTH, CONV_K, D))


def _pack_replicated(t, keys, rows):
    p = jnp.concatenate([t[k].reshape(-1, PACK_C) for k in keys], axis=0)
    return _pad_rows(p, rows)


def _unpack_replicated(p, like, keys):
    out, r0 = {}, 0
    for k in keys:
        rows = like[k].size // PACK_C
        out[k] = p[r0:r0 + rows].reshape(like[k].shape)
        r0 += rows
    return out


def kernel(x, c, ada_w, ada_b, pre_norm_g, w_in, conv_w, conv_b, gate_a_w, gate_a_b, gate_x_w, gate_x_b, lru_lambda, pool_w, pool_b, pool_scale, w_out, post_norm_g, loss_target, m_ada_w, m_ada_b, m_pre_norm_g, m_w_in, m_conv_w, m_conv_b, m_gate_a_w, m_gate_a_b, m_gate_x_w, m_gate_x_b, m_lru_lambda, m_pool_w, m_pool_b, m_pool_scale, m_w_out, m_post_norm_g, v_ada_w, v_ada_b, v_pre_norm_g, v_w_in, v_conv_w, v_conv_b, v_gate_a_w, v_gate_a_b, v_gate_x_w, v_gate_x_b, v_lru_lambda, v_pool_w, v_pool_b, v_pool_scale, v_w_out, v_post_norm_g):
    W = dict(ada_w=ada_w, ada_b=ada_b, pre_norm_g=pre_norm_g, w_in=w_in, conv_w=conv_w, conv_b=conv_b,
             gate_a_w=gate_a_w, gate_a_b=gate_a_b, gate_x_w=gate_x_w, gate_x_b=gate_x_b, lru_lambda=lru_lambda,
             pool_w=pool_w, pool_b=pool_b, pool_scale=pool_scale, w_out=w_out, post_norm_g=post_norm_g)
    M = dict(ada_w=m_ada_w, ada_b=m_ada_b, pre_norm_g=m_pre_norm_g, w_in=m_w_in, conv_w=m_conv_w,
             conv_b=m_conv_b, gate_a_w=m_gate_a_w, gate_a_b=m_gate_a_b, gate_x_w=m_gate_x_w,
             gate_x_b=m_gate_x_b, lru_lambda=m_lru_lambda, pool_w=m_pool_w, pool_b=m_pool_b,
             pool_scale=m_pool_scale, w_out=m_w_out, post_norm_g=m_post_norm_g)
    V = dict(ada_w=v_ada_w, ada_b=v_ada_b, pre_norm_g=v_pre_norm_g, w_in=v_w_in, conv_w=v_conv_w,
             conv_b=v_conv_b, gate_a_w=v_gate_a_w, gate_a_b=v_gate_a_b, gate_x_w=v_gate_x_w,
             gate_x_b=v_gate_x_b, lru_lambda=v_lru_lambda, pool_w=v_pool_w, pool_b=v_pool_b,
             pool_scale=v_pool_scale, w_out=v_w_out, post_norm_g=v_post_norm_g)
    S = x.shape[1]
    me = 4 * lax.axis_index("x") + 2 * lax.axis_index("y") + lax.axis_index("c")
    xs = x.reshape(S, D)
    tgt = loss_target.reshape(S, D)
    nc = ada_w.shape[2]
    NB = w_in.shape[2]
    w_in_b, w_out_b = w_in.astype(BF16), w_out.astype(BF16)
    rows = lambda a: a.reshape(DEPTH, 1, D)
    P = dict(pre_norm_g=rows(pre_norm_g), conv_b=rows(conv_b), gate_a_b=rows(gate_a_b), gate_x_b=rows(gate_x_b),
             lru_lambda=rows(lru_lambda), pool_scale=rows(pool_scale), post_norm_g=rows(post_norm_g),
             gate_a_w=gate_a_w.astype(BF16), gate_x_w=gate_x_w.astype(BF16))

    c_slots, w_in0 = _exchange("gather_c_w_in0", _AllGather2([jnp.broadcast_to(c, (SUB, D)), w_in_b[0]]))
    c_all = c_slots[:, 0, :]
    (mod_slots,) = _exchange("gather_mod", _Direct(ag=[_mod_cols(c_all, ada_w)]))
    mod = lax.dynamic_index_in_dim(mod_slots, me, axis=1, keepdims=False)
    mod = (mod.reshape(NDEV, DEPTH, nc).transpose(1, 0, 2).reshape(DEPTH, 3 * D) + ada_b).reshape(DEPTH, 3, 1, D)

    w_in_all, w_out_all = [w_in0, None], [None, None]
    saved = []
    xl = xs
    flat = lambda w_slots: w_slots.reshape(2 * D, D)
    for l in range(DEPTH):
        rider = _AllGather2([_pack_sharded_block(pool_w, pool_b, conv_w), w_out_b[0]]) if l == 0 else None
        (h_a, h_b, proj), got = _pre_proj(xl, P, mod, w_in_all[l], l, rider=rider)
        if l == 0:
            pool_w_f, pool_b_f, conv_w_f = _full_of_blocks(got[0])
            P.update(conv_w=conv_w_f, pool_w=pool_w_f.astype(BF16), pool_b=rows(pool_b_f))
            w_out_all[0] = flat(got[1])
        rider = _AllGather2([w_in_b[1]]) if l == 0 else None
        (hs, yr, *fwd), got = _rnn_fwd(proj, P, l, rider=rider)
        if l == 0:
            w_in_all[1] = got[0]
        (yp,), _ = _pool_fwd(proj, P, l)
        if l == DEPTH - 1:
            (y, x_next, loss_acc), _ = _out_post(yr, yp, w_out_all[l], xl, mod, P, l, tgt)
        else:
            (y, x_next), got = _out_post(yr, yp, w_out_all[l], xl, mod, P, l, rider=_AllGather2([w_out_b[1]]))
            w_out_all[1] = flat(got[0])
        saved.append((xl, h_a, h_b, proj, hs, fwd, yr, yp, y))
        xl = x_next

    dxo = xl
    G = {k: [None] * DEPTH for k in WEIGHTS}
    dmod = [None] * DEPTH
    recv_in, recv_out = [[None, None] for _ in range(DEPTH)], [None] * DEPTH
    full = dict(conv_w=(CONV_K, D), pool_w=(NGRP, GD, GD), pool_b=(NGRP, GD))
    stack = lambda k: jnp.stack([g.reshape(full.get(k, W[k].shape[1:])) for g in G[k]])
    gw_bot_prev = None
    for l in reversed(range(DEPTH)):
        xin, h_a, h_b, proj, hs, fwd, yr, yp, y = saved[l]
        rider = _AllGather2([loss_acc]) if gw_bot_prev is None else _Direct(a2a=[gw_bot_prev])
        (dyr, dyp, gw_out, dgate, G['post_norm_g'][l]), got = _out_bwd(dxo, y, yr, yp, w_out_all[l], mod, P, l,
                                                                       rider=rider)
        if gw_bot_prev is None:
            loss = (0.5 / D) * jnp.sum(got[0])
        else:
            recv_in[l + 1][1] = got[0]
        ((dxr, dgr, G['conv_w'][l], G['conv_b'][l], G['gate_a_w'][l], G['gate_a_b'][l], G['gate_x_w'][l],
          G['gate_x_b'][l], G['lru_lambda'][l]), (recv_out[l],)) = _rnn_bwd(
            dyr, hs, fwd, proj, P, l,
            rider=_Direct(a2a=[gw_out.reshape(NDEV, 2 * D // NDEV, D)]))
        (dxp, dgp, G['pool_w'][l], G['pool_b'][l], G['pool_scale'][l]), _ = _pool_bwd(dyp, proj, P, l)
        dq = (dxr, dgr, dxp, dgp)
        if l > 0:
            (gw_top,), _ = _grad_w_in(h_a, dq, NB)
            (dxo, dshift, dscale, G['pre_norm_g'][l]), (recv_in[l][0],) = _in_bwd(
                dq, w_in_all[l], xin, dxo, P, mod, l, rider=_Direct(a2a=[gw_top]))
            (gw_bot_prev,), _ = _grad_w_in(h_b, dq, NB)
        else:
            Ge = {k: stack(k) for k in REP_EARLY + ['pool_w', 'pool_b', 'conv_w']}
            early = jnp.concatenate([_blocks_of_full(Ge['pool_w'], Ge['pool_b'], Ge['conv_w']),
                                     _pack_replicated(Ge, REP_EARLY, NDEV * PACK_ROWS).reshape(NDEV, PACK_ROWS, PACK_C)],
                                    axis=1).astype(BF16)
            (gw_top,), (early_recv,) = _grad_w_in(h_a, dq, NB, rider=_Direct(a2a=[early]))
            early_sum = _sum_slots(early_recv)
            (gw_bot,), (recv_in[l][0],) = _grad_w_in(h_b, dq, NB, rider=_Direct(a2a=[gw_top]))
            (dxo, dshift, dscale, G['pre_norm_g'][l]), (recv_in[l][1], early_all) = _in_bwd(
                dq, w_in_all[l], xin, dxo, P, mod, l,
                rider=_Both(_Direct(a2a=[gw_bot]), _AllGather2([early_sum[PACK_ROWS:]])))
        dmod[l] = jnp.concatenate([dshift, dscale, dgate], axis=1)
    grad_x = dxo.reshape(x.shape)

    Gl = dict(ada_b=jnp.concatenate(dmod, axis=0), pre_norm_g=stack('pre_norm_g'))
    (late_slots,) = _exchange("gather_late", _AllGather2([_pack_replicated(Gl, REP_LATE, LATE_PACK_ROWS)]))
    late_sum = _sum_slots(late_slots)
    dmod_all = late_slots[:, :DEPTH * 3 * D // PACK_C]

    out = {}
    first, _ = _adamw("adamw_w_in_a", [recv_in[l][0] for l in range(DEPTH)], w_in, M['w_in'], V['w_in'], LANE)
    out['w_in'], _ = _adamw("adamw_w_in_b", [recv_in[l][1] for l in range(DEPTH)], w_in, M['w_in'], V['w_in'], LANE,
                            row0=GW_SPLIT, into=first)
    out['w_out'], _ = _adamw("adamw_w_out", recv_out, w_out, M['w_out'], V['w_out'], 256)
    dm = lax.dynamic_slice_in_dim(dmod_all.reshape(NDEV, DEPTH, 3 * D), me * nc, nc, axis=2)
    out['ada_w'], _ = _ada_adamw(c_all.T, dm.transpose(1, 0, 2), ada_w, M['ada_w'], V['ada_w'])
    g_small = jnp.concatenate([early_sum[:PACK_ROWS], early_all.reshape(NDEV * PACK_ROWS, PACK_C), late_sum],
                              axis=0)

    def packs(T):
        return jnp.concatenate([_pack_sharded_block(T['pool_w'], T['pool_b'], T['conv_w']),
                                _pack_replicated(T, REP_EARLY, NDEV * PACK_ROWS),
                                _pack_replicated(T, REP_LATE, LATE_PACK_ROWS)], axis=0)[None]
    res_small, _ = _adamw("adamw_small", [g_small[None]], packs(W), packs(M), packs(V), g_small.shape[0] // 2)
    n_early = (1 + NDEV) * PACK_ROWS
    for idx in range(4):
        p = res_small[idx][0]
        pw_, pb_, cw_ = _unpack_sharded_block(p[:PACK_ROWS])
        rep = _unpack_replicated(p[PACK_ROWS:n_early], W, REP_EARLY)
        rep.update(_unpack_replicated(p[n_early:], W, REP_LATE))
        rep.update(pool_w=pw_, pool_b=pb_, conv_w=cw_)
        for k, a in rep.items():
            out.setdefault(k, [None] * 4)[idx] = a
    for k in ('w_in', 'w_out', 'ada_w'):
        out[k] = [a.reshape(W[k].shape) for a in out[k]]

    return (loss, grad_x, *[out[k][0] for k in WEIGHTS], *[out[k][1] for k in WEIGHTS],
            *[out[k][2] for k in WEIGHTS], *[out[k][3] for k in WEIGHTS])
```

```python
import functools

import jax
import jax.numpy as jnp
from jax import lax
from jax.experimental import pallas as pl
from jax.experimental.pallas import tpu as pltpu

F32, BF16 = jnp.float32, jnp.bfloat16
MESH = pl.DeviceIdType.MESH
HIGHEST = lax.Precision.HIGHEST

NDEV = 8
DEPTH = 2
D = 1024
NHEAD, HD = 8, 128
NGRP, GD = 4, 256
WINS = (2, 4, 8, 16)
CONV_K = 4
CONV_HALO = 8
POOL_HALO = 16
LRU_C = 8.0
NORM_EPS = 1e-6
ADAM_LR, ADAM_B1, ADAM_B2, ADAM_EPS, ADAM_WD, ADAM_STEP = 0.001, 0.9, 0.999, 1e-08, 0.01, 10
VMEM_LIMIT = 56 * 1024 * 1024
NQ = 4
SUB = 8
LANE = 128
PACK_C = 256
PACK_ROWS = 272

WEIGHTS = ['ada_w', 'ada_b', 'pre_norm_g', 'w_in', 'conv_w', 'conv_b', 'gate_a_w', 'gate_a_b', 'gate_x_w',
           'gate_x_b', 'lru_lambda', 'pool_w', 'pool_b', 'pool_scale', 'w_out', 'post_norm_g']
REP_EARLY = ['conv_b', 'gate_a_w', 'gate_a_b', 'gate_x_w', 'gate_x_b', 'lru_lambda', 'pool_scale', 'post_norm_g']
REP_LATE = ['ada_b', 'pre_norm_g']
GW_SPLIT = 512
LATE_PACK_ROWS = 32


def _cparams(*sem):
    return pltpu.CompilerParams(dimension_semantics=sem, vmem_limit_bytes=VMEM_LIMIT)


def _vec(l, k=None):
    if k is None:
        return pl.BlockSpec((None, 1, D), lambda *_: (l, 0, 0))
    return pl.BlockSpec((None, None, 1, D), lambda *_: (l, k, 0, 0))


def _layer(l, shape):
    nd = len(shape)
    return pl.BlockSpec((None,) + tuple(shape), lambda *_: (l,) + (0,) * nd)


def _full(shape):
    nd = len(shape)
    return pl.BlockSpec(shape, lambda *_: (0,) * nd)


def _rowsum8(z):
    return z.reshape(z.shape[0] // SUB, SUB, z.shape[1]).sum(axis=0)


def _sum8(acc):
    return jnp.sum(acc, axis=0, keepdims=True)


def _sigmoid(z):
    return 0.5 * jnp.tanh(0.5 * z) + 0.5


def _silu_parts(g):
    sg = _sigmoid(g)
    return g * sg, sg * (1.0 + g * (1.0 - sg))


def _one_minus_sq(a, log_a):
    z = 2.0 * log_a
    p = 1.0 / 24.0
    for k in (6.0, 2.0, 1.0):
        p = p * z + 1.0 / k
    return jnp.where(z > -0.03, -(p * z), 1.0 - a * a)


def _place():
    x, y, c = lax.axis_index("x"), lax.axis_index("y"), lax.axis_index("c")
    return x, y, c, 4 * x + 2 * y + c


class _Direct:
    def __init__(self, a2a=(), ag=()):
        self.arrays = list(a2a) + list(ag)
        self.n_a, self.n = len(a2a), len(self.arrays)
        self.out_shape = ([jax.ShapeDtypeStruct(a.shape, a.dtype) for a in a2a]
                          + [jax.ShapeDtypeStruct((NDEV,) + a.shape, a.dtype) for a in ag])
        self.scratch = [pltpu.SemaphoreType.DMA((self.n, NDEV - 1)), pltpu.SemaphoreType.DMA((self.n, NDEV - 1)),
                        pltpu.SemaphoreType.DMA((self.n,))]

    def _copies(self, ins, outs, sems):
        send_sems, recv_sems, local_sems = sems
        x, y, c, me = _place()
        local, remote = [], []
        for t in range(self.n):
            src = ins[t].at[me] if t < self.n_a else ins[t]
            local.append(pltpu.make_async_copy(src, outs[t].at[me], local_sems.at[t]))
        for r in range(1, NDEV):
            px = 1 - x if r & 4 else x
            py = 1 - y if r & 2 else y
            pc = 1 - c if r & 1 else c
            for t in range(self.n):
                src = ins[t].at[4 * px + 2 * py + pc] if t < self.n_a else ins[t]
                remote.append(pltpu.make_async_remote_copy(
                    src_ref=src, dst_ref=outs[t].at[me], send_sem=send_sems.at[t, r - 1],
                    recv_sem=recv_sems.at[t, r - 1], device_id=(px, py, pc), device_id_type=MESH))
        return local, remote

    def start(self, ins, outs, sems):
        local, remote = self._copies(ins, outs, sems)
        for cp in local + remote:
            cp.start()

    def finish(self, ins, outs, sems):
        local, remote = self._copies(ins, outs, sems)
        for cp in remote + local:
            cp.wait()


class _AllGather2:
    def __init__(self, arrays):
        self.arrays = list(arrays)
        self.n = len(self.arrays)
        self.out_shape = [jax.ShapeDtypeStruct((NDEV,) + a.shape, a.dtype) for a in self.arrays]
        self.scratch = [pltpu.SemaphoreType.DMA((self.n, NDEV - 1)), pltpu.SemaphoreType.DMA((self.n, NDEV - 1)),
                        pltpu.SemaphoreType.DMA((self.n,))]

    @staticmethod
    def _chips(x, y):
        return [(1 - x, y), (x, 1 - y), (1 - x, 1 - y)]

    def _copy(self, t, k, src, dst, to, sems):
        return pltpu.make_async_remote_copy(src_ref=src, dst_ref=dst, send_sem=sems[0].at[t, k],
                                            recv_sem=sems[1].at[t, k], device_id=to, device_id_type=MESH)

    def start(self, ins, outs, sems):
        x, y, c, me = _place()
        for t in range(self.n):
            pltpu.make_async_copy(ins[t], outs[t].at[me], sems[2].at[t]).start()
            self._copy(t, 0, ins[t], outs[t].at[me], (x, y, 1 - c), sems).start()
            for j, (px, py) in enumerate(self._chips(x, y)):
                self._copy(t, 1 + j, ins[t], outs[t].at[me], (px, py, c), sems).start()

    def finish(self, ins, outs, sems):
        x, y, c, me = _place()
        sib = (x, y, 1 - c)
        for j, (px, py) in enumerate(self._chips(x, y)):
            slot = 4 * px + 2 * py + c
            for t in range(self.n):
                self._copy(t, 1 + j, ins[t], outs[t].at[slot], sib, sems).wait_recv()
                self._copy(t, 4 + j, outs[t].at[slot], outs[t].at[slot], sib, sems).start()
        for t in range(self.n):
            for k in (0, 4, 5, 6):
                self._copy(t, k, ins[t], outs[t].at[me], sib, sems).wait_recv()
        for t in range(self.n):
            for k in range(NDEV - 1):
                self._copy(t, k, ins[t], outs[t].at[me], sib, sems).wait_send()
            pltpu.make_async_copy(ins[t], outs[t].at[me], sems[2].at[t]).wait()


class _Both:
    def __init__(self, *riders):
        self.riders = riders
        self.arrays = [a for r in riders for a in r.arrays]
        self.n = len(self.arrays)
        self.out_shape = [o for r in riders for o in r.out_shape]
        self.scratch = [s for r in riders for s in r.scratch]

    def _parts(self, ins, outs, sems):
        p, q = 0, 0
        for r in self.riders:
            yield r, ins[p:p + r.n], outs[p:p + r.n], sems[q:q + len(r.scratch)]
            p, q = p + r.n, q + len(r.scratch)

    def start(self, ins, outs, sems):
        for r, i, o, s in self._parts(ins, outs, sems):
            r.start(i, o, s)

    def finish(self, ins, outs, sems):
        for r, i, o, s in self._parts(ins, outs, sems):
            r.finish(i, o, s)


def _exchange(name, rider):
    n = rider.n

    def body(*refs):
        rider.start(refs[:n], refs[n:2 * n], refs[2 * n:])
        rider.finish(refs[:n], refs[n:2 * n], refs[2 * n:])

    any_spec = pl.BlockSpec(memory_space=pl.ANY)
    return list(pl.pallas_call(body, name=name, out_shape=rider.out_shape, in_specs=[any_spec] * n,
                               out_specs=[any_spec] * n, scratch_shapes=rider.scratch)(*rider.arrays))


def _pcall(body, *, name, grid, in_specs, out_specs, out_shape, args, scratch_shapes=(), rider=None, aliases=None):
    params = _cparams(*(("arbitrary",) * len(grid)))
    if rider is None:
        res = pl.pallas_call(body, name=name, grid=grid, in_specs=in_specs, out_specs=out_specs,
                             out_shape=out_shape, scratch_shapes=list(scratch_shapes),
                             input_output_aliases=aliases or {}, compiler_params=params)(*args)
        return list(res), []
    n_in, n_out, n_scr, rn = len(in_specs), len(out_specs), len(scratch_shapes), rider.n

    def wrapped(*refs):
        cuts = [n_in, rn, n_out, rn, n_scr]
        parts, p = [], 0
        for n in cuts:
            parts.append(refs[p:p + n])
            p += n
        ins, r_in, outs, r_out, scr = parts
        sems = refs[p:]
        ids = [pl.program_id(a) for a in range(len(grid))]
        first = functools.reduce(jnp.logical_and, [i == 0 for i in ids])
        last = functools.reduce(jnp.logical_and, [i == g - 1 for i, g in zip(ids, grid)])

        @pl.when(first)
        def _():
            rider.start(r_in, r_out, sems)
        body(*ins, *outs, *scr)

        @pl.when(last)
        def _():
            rider.finish(r_in, r_out, sems)

    any_spec = pl.BlockSpec(memory_space=pl.ANY)
    res = pl.pallas_call(
        wrapped, name=name, grid=grid, in_specs=list(in_specs) + [any_spec] * rn,
        out_specs=list(out_specs) + [any_spec] * rn, out_shape=list(out_shape) + rider.out_shape,
        scratch_shapes=list(scratch_shapes) + rider.scratch, compiler_params=params)(*args, *rider.arrays)
    return list(res[:n_out]), list(res[n_out:])


def _mod_cols(c_all, ada_w):
    nc = ada_w.shape[2]

    def body(c_ref, w_ref, o_ref):
        cv = c_ref[...]
        ca = cv * jax.nn.sigmoid(cv)
        for l in range(DEPTH):
            o_ref[:, l * nc:(l + 1) * nc] = jnp.dot(ca, w_ref[l], precision=HIGHEST, preferred_element_type=F32)

    return pl.pallas_call(body, name="mod_cols", out_shape=jax.ShapeDtypeStruct((NDEV, DEPTH * nc), F32),
                          compiler_params=_cparams())(c_all, ada_w)


def _pre_proj(x, P, mod, w_in_l, l, rider=None):
    S = x.shape[0]
    TM = min(512, S)
    NB = w_in_l.shape[2]

    def body(x_ref, g_ref, sc_ref, sh_ref, w_ref, ha_ref, hb_ref, p_ref):
        xv = x_ref[...]
        rstd = lax.rsqrt(jnp.mean(xv * xv, axis=-1, keepdims=True) + NORM_EPS)
        h = ((xv * rstd * g_ref[...]) * (1.0 + sc_ref[...]) + sh_ref[...]).astype(BF16)
        ha_ref[...] = h[:, :GW_SPLIT]
        hb_ref[...] = h[:, GW_SPLIT:]
        for j in range(NDEV):
            p_ref[:, j * NB:(j + 1) * NB] = jnp.dot(h, w_ref[j], preferred_element_type=F32)

    row = pl.BlockSpec((TM, D), lambda i: (i, 0))
    return _pcall(
        body, name="pre_proj", grid=(S // TM,),
        in_specs=[row, _vec(l), _vec(l, 1), _vec(l, 0), _full((NDEV, D, NB))],
        out_specs=[pl.BlockSpec((TM, GW_SPLIT), lambda i: (i, 0)), pl.BlockSpec((TM, D - GW_SPLIT), lambda i: (i, 0)),
                   pl.BlockSpec((TM, NDEV * NB), lambda i: (i, 0))],
        out_shape=[jax.ShapeDtypeStruct((S, GW_SPLIT), BF16), jax.ShapeDtypeStruct((S, D - GW_SPLIT), BF16),
                   jax.ShapeDtypeStruct((S, NDEV * NB), F32)],
        args=(x, P['pre_norm_g'], mod, mod, w_in_l), rider=rider)


def _taps(E):
    return [pltpu.roll(E, CONV_K - 1 - k, axis=0)[CONV_HALO:, :] for k in range(CONV_K - 1)] + [E[CONV_HALO:, :]]


def _conv(E, cw_ref, cb_ref):
    w = cw_ref[...]
    taps = _taps(E)
    acc = cb_ref[...] + taps[0] * w[0:1, :]
    for k in range(1, CONV_K):
        acc = acc + taps[k] * w[k:k + 1, :]
    return acc


CH_R, CH_C = 16, 512


def _chunks(T, fn):
    def step(c, carry):
        rows = pl.ds(pl.multiple_of(c * CH_R, CH_R), CH_R)
        for hf in range(D // CH_C):
            fn(rows, slice(hf * CH_C, (hf + 1) * CH_C), hf)
        return carry
    lax.fori_loop(0, T // CH_R, step, 0)


def _to_scan(ref, rows, hf, val):
    for q in range(CH_C // LANE):
        ref[hf * (CH_C // LANE) + q, rows, :] = val[:, q * LANE:(q + 1) * LANE]


def _scan(sa, sv, carry_ref, out_ref, reverse):
    NC, T, _ = sa.shape
    n8 = T // SUB
    rows = range(SUB - 2, -1, -1) if reverse else range(1, SUB)
    for cb in range(NC):
        r_in = SUB - 1 if reverse else 0
        Ap = sa[cb, pl.ds(r_in, n8, stride=SUB), :]
        Vp = sv[cb, pl.ds(r_in, n8, stride=SUB), :]
        for r in rows:
            Ar = sa[cb, pl.ds(r, n8, stride=SUB), :]
            Vp = sv[cb, pl.ds(r, n8, stride=SUB), :] + Ar * Vp
            Ap = Ar * Ap
            sa[cb, pl.ds(r, n8, stride=SUB), :] = Ap
            sv[cb, pl.ds(r, n8, stride=SUB), :] = Vp
    edge = 0 if reverse else SUB - 1

    def step(k, c):
        r0 = pl.multiple_of((n8 - 1 - k if reverse else k) * SUB, SUB)
        h = jnp.concatenate([sv[cb, pl.ds(r0, SUB), :] + sa[cb, pl.ds(r0, SUB), :] * c[:, cb * LANE:(cb + 1) * LANE]
                             for cb in range(NC)], axis=1)
        out_ref[pl.ds(r0, SUB), :] = h
        return jnp.broadcast_to(h[edge:edge + 1, :], (SUB, D))

    carry_ref[...] = lax.fori_loop(0, n8, step, carry_ref[...])


def _rnn_fwd(proj, P, l, rider=None):
    S = proj.shape[0]
    TB = min(256, S)

    def body(xr_ref, g_ref, cw_ref, cb_ref, wa_ref, ba_ref, wx_ref, bx_ref, lam_ref, hs_ref, y_ref,
             u_ref, r_ref, i_ref, a_ref, m_ref, xbuf, sa, sv, hc):
        @pl.when(pl.program_id(0) == 0)
        def _():
            xbuf[0:CONV_HALO, :] = jnp.zeros((CONV_HALO, D), F32)
            hc[...] = jnp.zeros((SUB, D), F32)
        xbuf[CONV_HALO:, :] = xr_ref[...]
        u = _conv(xbuf[...], cw_ref, cb_ref)
        xbuf[0:CONV_HALO, :] = xbuf[TB:TB + CONV_HALO, :]
        ub = u.astype(BF16)
        zr = jnp.concatenate([jnp.dot(ub[:, h * HD:(h + 1) * HD], wa_ref[h], preferred_element_type=F32)
                              for h in range(NHEAD)], axis=1)
        zi = jnp.concatenate([jnp.dot(ub[:, h * HD:(h + 1) * HD], wx_ref[h], preferred_element_type=F32)
                              for h in range(NHEAD)], axis=1)
        r = _sigmoid(zr + ba_ref[...])
        ig = _sigmoid(zi + bx_ref[...])
        log_a = r * (-LRU_C * jax.nn.softplus(-lam_ref[...]))
        a = jnp.exp(log_a)
        mult = jnp.sqrt(_one_minus_sq(a, log_a))
        v = mult * (ig * u)
        u_ref[...] = u
        r_ref[...] = r
        i_ref[...] = ig
        a_ref[...] = a
        m_ref[...] = mult
        for cb in range(D // LANE):
            sa[cb] = a[:, cb * LANE:(cb + 1) * LANE]
            sv[cb] = v[:, cb * LANE:(cb + 1) * LANE]
        _scan(sa, sv, hc, hs_ref, reverse=False)
        silu, _ = _silu_parts(g_ref[...])
        y_ref[...] = (hs_ref[...] * silu).astype(BF16)

    rowb = pl.BlockSpec((TB, D), lambda i: (i, 0))
    return _pcall(
        body, name="rnn_fwd", grid=(S // TB,),
        in_specs=[rowb, pl.BlockSpec((TB, D), lambda i: (i, 1)), _layer(l, (CONV_K, D)), _vec(l),
                  _layer(l, (NHEAD, HD, HD)), _vec(l), _layer(l, (NHEAD, HD, HD)), _vec(l), _vec(l)],
        out_specs=[rowb] * 7,
        out_shape=[jax.ShapeDtypeStruct((S, D), F32), jax.ShapeDtypeStruct((S, D), BF16)]
        + [jax.ShapeDtypeStruct((S, D), F32)] * 5,
        scratch_shapes=[pltpu.VMEM((TB + CONV_HALO, D), F32), pltpu.VMEM((D // LANE, TB, LANE), F32),
                        pltpu.VMEM((D // LANE, TB, LANE), F32), pltpu.VMEM((SUB, D), F32)],
        args=(proj, proj, P['conv_w'], P['conv_b'], P['gate_a_w'], P['gate_a_b'], P['gate_x_w'], P['gate_x_b'],
              P['lru_lambda']), rider=rider)


def _pooled(ebuf, t0, TB):
    tt = t0 + lax.broadcasted_iota(jnp.int32, (TB, 1), 0)
    pooled, inv = [], []
    for g, win in enumerate(WINS):
        Eg = ebuf[:, g * GD:(g + 1) * GD]
        L = Eg
        for lev in range(g + 1):
            L = L + pltpu.roll(L, 1 << lev, axis=0)
        icnt = 1.0 / jnp.minimum(tt + 1, win).astype(F32)
        pooled.append(L[POOL_HALO:, :] * icnt - Eg[POOL_HALO:, :])
        inv.append(icnt)
    return pooled, inv


def _pool_fwd(proj, P, l, rider=None):
    S = proj.shape[0]
    TB = min(256, S)

    def body(xp_ref, g_ref, pw_ref, pb_ref, ps_ref, y_ref, ebuf):
        i = pl.program_id(0)

        @pl.when(i == 0)
        def _():
            ebuf[0:POOL_HALO, :] = jnp.zeros((POOL_HALO, D), F32)
        ebuf[POOL_HALO:, :] = xp_ref[...]
        pooled, _ = _pooled(ebuf, i * TB, TB)
        ebuf[0:POOL_HALO, :] = ebuf[TB:TB + POOL_HALO, :]
        yp = jnp.concatenate([jnp.dot(pooled[g].astype(BF16), pw_ref[g], preferred_element_type=F32)
                              for g in range(NGRP)], axis=1) + pb_ref[...]
        silu, _ = _silu_parts(g_ref[...])
        y_ref[...] = (yp * ps_ref[...] * silu).astype(BF16)

    return _pcall(
        body, name="pool_fwd", grid=(S // TB,),
        in_specs=[pl.BlockSpec((TB, D), lambda i: (i, 2)), pl.BlockSpec((TB, D), lambda i: (i, 3)),
                  _layer(l, (NGRP, GD, GD)), _vec(l), _vec(l)],
        out_specs=[pl.BlockSpec((TB, D), lambda i: (i, 0))],
        out_shape=[jax.ShapeDtypeStruct((S, D), BF16)],
        scratch_shapes=[pltpu.VMEM((TB + POOL_HALO, D), F32)],
        args=(proj, proj, P['pool_w'], P['pool_b'], P['pool_scale']), rider=rider)


def _out_post(yr, yp, w_out_l, x, mod, P, l, target=None, rider=None):
    S = x.shape[0]
    TM = min(512, S)
    last = target is not None

    def body(*refs):
        if last:
            yr_ref, yp_ref, w_ref, x_ref, gate_ref, gp_ref, t_ref, y_ref, xo_ref, loss_ref = refs
        else:
            yr_ref, yp_ref, w_ref, x_ref, gate_ref, gp_ref, y_ref, xo_ref = refs
        acc = (jnp.dot(yr_ref[...], w_ref[0:D, :], preferred_element_type=F32)
               + jnp.dot(yp_ref[...], w_ref[D:2 * D, :], preferred_element_type=F32))
        y_ref[...] = acc
        rstd = lax.rsqrt(jnp.mean(acc * acc, axis=-1, keepdims=True) + NORM_EPS)
        xn = x_ref[...] + gate_ref[...] * (acc * rstd * gp_ref[...])
        if last:
            err = xn - t_ref[...]
            xo_ref[...] = err * (1.0 / D)

            @pl.when(pl.program_id(0) == 0)
            def _():
                loss_ref[...] = jnp.zeros((SUB, D), F32)
            loss_ref[...] += _rowsum8(err * err)
        else:
            xo_ref[...] = xn

    row = pl.BlockSpec((TM, D), lambda i: (i, 0))
    in_specs = [row, row, _full((2 * D, D)), row, _vec(l, 2), _vec(l)]
    out_specs = [row, row]
    out_shape = [jax.ShapeDtypeStruct((S, D), F32), jax.ShapeDtypeStruct((S, D), F32)]
    args = [yr, yp, w_out_l, x, mod, P['post_norm_g']]
    if last:
        in_specs.append(row)
        out_specs.append(_full((SUB, D)))
        out_shape.append(jax.ShapeDtypeStruct((SUB, D), F32))
        args.append(target)
    return _pcall(body, name="out_post_loss" if last else "out_post", grid=(S // TM,), in_specs=in_specs,
                  out_specs=out_specs, out_shape=out_shape, args=args, rider=rider)


def _out_bwd(dxo, y, yr, yp, w_out_l, mod, P, l, rider=None):
    S = y.shape[0]
    TM = min(512, S)
    nsteps = S // TM

    def body(dxo_ref, y_ref, yr_ref, yp_ref, w_ref, gate_ref, gp_ref, dyr_ref, dyp_ref, gw_ref, dgate_ref,
             dgp_ref, gw_acc, vacc):
        i = pl.program_id(0)

        @pl.when(i == 0)
        def _():
            gw_acc[...] = jnp.zeros_like(gw_acc)
            vacc[...] = jnp.zeros_like(vacc)
        yv = y_ref[...]
        dxo_v = dxo_ref[...]
        rstd = lax.rsqrt(jnp.mean(yv * yv, axis=-1, keepdims=True) + NORM_EPS)
        n = yv * rstd
        gp = gp_ref[...]
        vacc[0] += _rowsum8(dxo_v * (n * gp))
        drn = dxo_v * gate_ref[...]
        vacc[1] += _rowsum8(drn * n)
        dn = drn * gp
        dy = (rstd * (dn - n * jnp.mean(dn * n, axis=-1, keepdims=True))).astype(BF16)
        dyr_ref[...] = lax.dot_general(dy, w_ref[0:D, :], (((1,), (1,)), ((), ())), preferred_element_type=F32)
        dyp_ref[...] = lax.dot_general(dy, w_ref[D:2 * D, :], (((1,), (1,)), ((), ())),
                                       preferred_element_type=F32)
        gw_acc[0:D, :] += lax.dot_general(yr_ref[...], dy, (((0,), (0,)), ((), ())), preferred_element_type=F32)
        gw_acc[D:2 * D, :] += lax.dot_general(yp_ref[...], dy, (((0,), (0,)), ((), ())),
                                              preferred_element_type=F32)

        @pl.when(i == nsteps - 1)
        def _():
            gw_ref[...] = gw_acc[...].astype(BF16)
            dgate_ref[...] = _sum8(vacc[0])
            dgp_ref[...] = _sum8(vacc[1])

    row = pl.BlockSpec((TM, D), lambda i: (i, 0))
    return _pcall(
        body, name="out_bwd", grid=(nsteps,),
        in_specs=[row, row, row, row, _full((2 * D, D)), _vec(l, 2), _vec(l)],
        out_specs=[row, row, _full((2 * D, D)), _full((1, D)), _full((1, D))],
        out_shape=[jax.ShapeDtypeStruct((S, D), F32), jax.ShapeDtypeStruct((S, D), F32),
                   jax.ShapeDtypeStruct((2 * D, D), BF16), jax.ShapeDtypeStruct((1, D), F32),
                   jax.ShapeDtypeStruct((1, D), F32)],
        scratch_shapes=[pltpu.VMEM((2 * D, D), F32), pltpu.VMEM((2, SUB, D), F32)],
        args=(dxo, y, yr, yp, w_out_l, mod, P['post_norm_g']), rider=rider)


def _rnn_bwd(dyr, hs, fwd, proj, P, l, rider=None):
    S = proj.shape[0]
    TB = min(256, S)
    nb = S // TB
    TE = TB + CONV_HALO
    A_BA, A_BX, A_LAM, A_CB, A_CW = 0, 1, 2, 3, 4

    def blk(i):
        return nb - 1 - i

    def body(dyr_ref, hs_ref, hprev_ref, u_ref, r_ref, i_ref, a_ref, m_ref, xr_ref, g_ref, cw_ref,
             wa_ref, wx_ref, lam_ref, dxr_ref, dg_ref, gcw_ref, gcb_ref, gwa_ref, gba_ref, gwx_ref, gbx_ref,
             glam_ref, hbuf, abuf, dbuf, sa, sv, dh_ref, hp_ref, dzr_ref, dzi_ref, dhc, vacc):
        i = pl.program_id(0)
        first = blk(i) == 0

        @pl.when(i == 0)
        def _():
            abuf[TB:, :] = jnp.zeros((CONV_HALO, D), F32)
            dbuf[TB:, :] = jnp.zeros((CONV_HALO, D), F32)
            dhc[...] = jnp.zeros_like(dhc)
            vacc[...] = jnp.zeros_like(vacc)
            gwa_ref[...] = jnp.zeros_like(gwa_ref)
            gwx_ref[...] = jnp.zeros_like(gwx_ref)

        hbuf[0:CONV_HALO, :] = jnp.where(first, 0.0, hprev_ref[...])
        hbuf[CONV_HALO:, :] = hs_ref[...]
        hp_ref[...] = pltpu.roll(hbuf[...], 1, axis=0)[CONV_HALO:, :]
        abuf[0:TB, :] = a_ref[...]
        b = pltpu.roll(abuf[...], TE - 1, axis=0)[0:TB, :]
        for cb in range(D // LANE):
            sa[cb] = b[:, cb * LANE:(cb + 1) * LANE]
        abuf[TB:, :] = a_ref[0:CONV_HALO, :]

        def gate_bwd(rows, cs, hf):
            silu, dsilu = _silu_parts(g_ref[rows, cs])
            dy = dyr_ref[rows, cs]
            dg_ref[rows, cs] = (dy * hs_ref[rows, cs] * dsilu).astype(BF16)
            _to_scan(sv, rows, hf, dy * silu)
        _chunks(TB, gate_bwd)

        _scan(sa, sv, dhc, dh_ref, reverse=True)

        csp = -LRU_C * jax.nn.softplus(-lam_ref[...])

        def lru_bwd(rows, cs, hf):
            dh, a, ig, u, mult, r = dh_ref[rows, cs], a_ref[rows, cs], i_ref[rows, cs], u_ref[rows, cs], \
                m_ref[rows, cs], r_ref[rows, cs]
            dlog_a = dh * hp_ref[rows, cs] * a - (dh * ig * u) * (a * a) / mult
            dzr = dlog_a * csp[:, cs] * r * (1.0 - r)
            dzi = (dh * mult * u) * ig * (1.0 - ig)
            dzr_ref[rows, cs] = dzr.astype(BF16)
            dzi_ref[rows, cs] = dzi.astype(BF16)
            dbuf[rows, cs] = dh * mult * ig
            vacc[A_LAM, :, cs] += _rowsum8(dlog_a * r)
            vacc[A_BA, :, cs] += _rowsum8(dzr)
            vacc[A_BX, :, cs] += _rowsum8(dzi)
        _chunks(TB, lru_bwd)

        ub, dzrb, dzib = u_ref[...].astype(BF16), dzr_ref[...], dzi_ref[...]
        du_g = []
        for h in range(NHEAD):
            cs = slice(h * HD, (h + 1) * HD)
            gwa_ref[h] += lax.dot_general(ub[:, cs], dzrb[:, cs], (((0,), (0,)), ((), ())),
                                          preferred_element_type=F32)
            gwx_ref[h] += lax.dot_general(ub[:, cs], dzib[:, cs], (((0,), (0,)), ((), ())),
                                          preferred_element_type=F32)
            du_g.append(lax.dot_general(dzrb[:, cs], wa_ref[h], (((1,), (1,)), ((), ())),
                                        preferred_element_type=F32)
                        + lax.dot_general(dzib[:, cs], wx_ref[h], (((1,), (1,)), ((), ())),
                                          preferred_element_type=F32))
        du = dbuf[0:TB, :] + jnp.concatenate(du_g, axis=1)
        dbuf[0:TB, :] = du

        Dd = dbuf[...]
        w = cw_ref[...]
        xv = xr_ref[...]
        dx = du * w[CONV_K - 1:CONV_K, :]
        vacc[A_CB] += _rowsum8(du)
        vacc[A_CW + CONV_K - 1] += _rowsum8(xv * du)
        for k in range(CONV_K - 1):
            ahead = pltpu.roll(Dd, TE - (CONV_K - 1 - k), axis=0)[0:TB, :]
            dx = dx + ahead * w[k:k + 1, :]
            vacc[A_CW + k] += _rowsum8(xv * ahead)
        dxr_ref[...] = dx.astype(BF16)
        dbuf[TB:, :] = du[0:CONV_HALO, :]

        @pl.when(i == nb - 1)
        def _():
            gba_ref[...] = _sum8(vacc[A_BA])
            gbx_ref[...] = _sum8(vacc[A_BX])
            glam_ref[...] = _sum8(vacc[A_LAM]) * (LRU_C * _sigmoid(-lam_ref[...]))
            gcb_ref[...] = _sum8(vacc[A_CB])
            for k in range(CONV_K):
                gcw_ref[k:k + 1, :] = _sum8(vacc[A_CW + k])

    rowb = pl.BlockSpec((TB, D), lambda i: (blk(i), 0))
    halo = pl.BlockSpec((SUB, D), lambda i: (jnp.maximum(blk(i) * (TB // SUB) - 1, 0), 0))
    wspec = _full((NHEAD, HD, HD))
    wlay = _layer(l, (NHEAD, HD, HD))
    vec1 = _full((1, D))
    return _pcall(
        body, name="rnn_bwd", grid=(nb,),
        in_specs=[rowb, rowb, halo] + [rowb] * 5 + [rowb, pl.BlockSpec((TB, D), lambda i: (blk(i), 1)),
                                                    _layer(l, (CONV_K, D)), wlay, wlay, _vec(l)],
        out_specs=[rowb, rowb, _full((CONV_K, D)), vec1, wspec, vec1, wspec, vec1, vec1],
        out_shape=[jax.ShapeDtypeStruct((S, D), BF16), jax.ShapeDtypeStruct((S, D), BF16),
                   jax.ShapeDtypeStruct((CONV_K, D), F32), jax.ShapeDtypeStruct((1, D), F32),
                   jax.ShapeDtypeStruct((NHEAD, HD, HD), F32), jax.ShapeDtypeStruct((1, D), F32),
                   jax.ShapeDtypeStruct((NHEAD, HD, HD), F32), jax.ShapeDtypeStruct((1, D), F32),
                   jax.ShapeDtypeStruct((1, D), F32)],
        scratch_shapes=[pltpu.VMEM((TE, D), F32), pltpu.VMEM((TE, D), F32), pltpu.VMEM((TE, D), F32),
                        pltpu.VMEM((D // LANE, TB, LANE), F32), pltpu.VMEM((D // LANE, TB, LANE), F32),
                        pltpu.VMEM((TB, D), F32), pltpu.VMEM((TB, D), F32), pltpu.VMEM((TB, D), BF16),
                        pltpu.VMEM((TB, D), BF16), pltpu.VMEM((SUB, D), F32),
                        pltpu.VMEM((A_CW + CONV_K, SUB, D), F32)],
        args=(dyr, hs, hs, *fwd, proj, proj, P['conv_w'], P['gate_a_w'], P['gate_x_w'], P['lru_lambda']),
        rider=rider)


def _pool_bwd(dyp, proj, P, l, rider=None):
    S = proj.shape[0]
    TB = min(256, S)
    nb = S // TB
    TE = TB + POOL_HALO

    def blk(i):
        return nb - 1 - i

    def body(dy_ref, xp_ref, xprev_ref, g_ref, pw_ref, pb_ref, ps_ref, dxp_ref, dg_ref, gpw_ref, gpb_ref,
             gps_ref, ebuf, qbuf, vacc):
        i = pl.program_id(0)
        first = blk(i) == 0

        @pl.when(i == 0)
        def _():
            qbuf[TB:, :] = jnp.zeros((POOL_HALO, D), F32)
            vacc[...] = jnp.zeros_like(vacc)
            gpw_ref[...] = jnp.zeros_like(gpw_ref)

        ebuf[0:POOL_HALO, :] = jnp.where(first, 0.0, xprev_ref[...])
        ebuf[POOL_HALO:, :] = xp_ref[...]
        pooled, inv = _pooled(ebuf, blk(i) * TB, TB)
        pooled = [p.astype(BF16) for p in pooled]
        yp = jnp.concatenate([jnp.dot(pooled[g], pw_ref[g], preferred_element_type=F32)
                              for g in range(NGRP)], axis=1) + pb_ref[...]
        silu, dsilu = _silu_parts(g_ref[...])
        dy = dy_ref[...]
        ps = ps_ref[...]
        dyp_v = dy * ps * silu
        vacc[0] += _rowsum8(dy * yp * silu)
        vacc[1] += _rowsum8(dyp_v)
        dg_ref[...] = (dy * yp * ps * dsilu).astype(BF16)
        dypb = dyp_v.astype(BF16)
        for g in range(NGRP):
            cs = slice(g * GD, (g + 1) * GD)
            gpw_ref[g] += lax.dot_general(pooled[g], dypb[:, cs], (((0,), (0,)), ((), ())),
                                          preferred_element_type=F32)
            dpool = lax.dot_general(dypb[:, cs], pw_ref[g], (((1,), (1,)), ((), ())),
                                    preferred_element_type=F32)
            qbuf[0:TB, cs] = dpool * inv[g]
            L = qbuf[:, cs]
            for lev in range(g + 1):
                L = L + pltpu.roll(L, TE - (1 << lev), axis=0)
            dxp_ref[:, cs] = (L[0:TB, :] - dpool).astype(BF16)
        qbuf[TB:, :] = qbuf[0:POOL_HALO, :]

        @pl.when(i == nb - 1)
        def _():
            gps_ref[...] = _sum8(vacc[0])
            gpb_ref[...] = _sum8(vacc[1])

    rowb = pl.BlockSpec((TB, D), lambda i: (blk(i), 0))
    return _pcall(
        body, name="pool_bwd", grid=(nb,),
        in_specs=[rowb, pl.BlockSpec((TB, D), lambda i: (blk(i), 2)),
                  pl.BlockSpec((POOL_HALO, D), lambda i: (jnp.maximum(blk(i) * (TB // POOL_HALO) - 1, 0), 2)),
                  pl.BlockSpec((TB, D), lambda i: (blk(i), 3)), _layer(l, (NGRP, GD, GD)), _vec(l), _vec(l)],
        out_specs=[rowb, rowb, _full((NGRP, GD, GD)), _full((1, D)), _full((1, D))],
        out_shape=[jax.ShapeDtypeStruct((S, D), BF16), jax.ShapeDtypeStruct((S, D), BF16),
                   jax.ShapeDtypeStruct((NGRP, GD, GD), F32), jax.ShapeDtypeStruct((1, D), F32),
                   jax.ShapeDtypeStruct((1, D), F32)],
        scratch_shapes=[pltpu.VMEM((TE, D), F32), pltpu.VMEM((TE, D), F32), pltpu.VMEM((2, SUB, D), F32)],
        args=(dyp, proj, proj, proj, P['pool_w'], P['pool_b'], P['pool_scale']), rider=rider)


def _in_bwd(dq, w_in_l, x, dxo, P, mod, l, rider=None):
    S = x.shape[0]
    TM = min(256, S)
    NB = w_in_l.shape[2]
    nsteps = S // TM
    per_q = D // NB

    def body(d0, d1, d2, d3, w_ref, x_ref, dxo_ref, g_ref, sc_ref, dx_ref, dsh_ref, dsc_ref, dg_ref, vacc):
        i = pl.program_id(0)

        @pl.when(i == 0)
        def _():
            vacc[...] = jnp.zeros_like(vacc)
        dref = (d0, d1, d2, d3)
        dh = jnp.zeros((TM, D), F32)
        for j in range(NDEV):
            c0 = (j % per_q) * NB
            dh = dh + lax.dot_general(dref[j // per_q][:, c0:c0 + NB], w_ref[j], (((1,), (1,)), ((), ())),
                                      preferred_element_type=F32)
        xv = x_ref[...]
        rstd = lax.rsqrt(jnp.mean(xv * xv, axis=-1, keepdims=True) + NORM_EPS)
        xn = xv * rstd
        g, sc = g_ref[...], 1.0 + sc_ref[...]
        vacc[0] += _rowsum8(dh)
        vacc[1] += _rowsum8(dh * (xn * g))
        vacc[2] += _rowsum8(dh * sc * xn)
        dxn = dh * sc * g
        dx_ref[...] = dxo_ref[...] + rstd * (dxn - xn * jnp.mean(dxn * xn, axis=-1, keepdims=True))

        @pl.when(i == nsteps - 1)
        def _():
            dsh_ref[...] = _sum8(vacc[0])
            dsc_ref[...] = _sum8(vacc[1])
            dg_ref[...] = _sum8(vacc[2])

    row = pl.BlockSpec((TM, D), lambda i: (i, 0))
    return _pcall(
        body, name="in_bwd", grid=(nsteps,),
        in_specs=[row, row, row, row, _full((NDEV, D, NB)), row, row, _vec(l), _vec(l, 1)],
        out_specs=[row, _full((1, D)), _full((1, D)), _full((1, D))],
        out_shape=[jax.ShapeDtypeStruct((S, D), F32)] + [jax.ShapeDtypeStruct((1, D), F32)] * 3,
        scratch_shapes=[pltpu.VMEM((3, SUB, D), F32)],
        args=(*dq, w_in_l, x, dxo, P['pre_norm_g'], mod), rider=rider)


def _grad_w_in(h, dq, NB, rider=None):
    S, RH = h.shape
    TK = min(1024, S)
    nk = S // TK

    def body(h_ref, d0, d1, d2, d3, o_ref, acc):
        k = pl.program_id(0)

        @pl.when(k == 0)
        def _():
            acc[...] = jnp.zeros_like(acc)
        hv = h_ref[...]
        for q, d_ref in enumerate((d0, d1, d2, d3)):
            acc[:, q * D:(q + 1) * D] += lax.dot_general(hv, d_ref[...], (((0,), (0,)), ((), ())),
                                                         preferred_element_type=F32)

        @pl.when(k == nk - 1)
        def _():
            for j in range(NDEV):
                o_ref[j] = acc[:, j * NB:(j + 1) * NB].astype(BF16)

    row = pl.BlockSpec((TK, D), lambda k: (k, 0))
    return _pcall(
        body, name="grad_w_in", grid=(nk,),
        in_specs=[pl.BlockSpec((TK, RH), lambda k: (k, 0)), row, row, row, row],
        out_specs=[_full((NDEV, RH, NB))],
        out_shape=[jax.ShapeDtypeStruct((NDEV, RH, NB), BF16)],
        scratch_shapes=[pltpu.VMEM((RH, NQ * D), F32)],
        args=(h, *dq), rider=rider)


def _adamw_math(g, w, m, v):
    m2 = ADAM_B1 * m + (1.0 - ADAM_B1) * g
    v2 = ADAM_B2 * v + (1.0 - ADAM_B2) * (g * g)
    m_hat = m2 / (1.0 - ADAM_B1 ** ADAM_STEP)
    v_hat = v2 / (1.0 - ADAM_B2 ** ADAM_STEP)
    delta = -ADAM_LR * (m_hat / (jnp.sqrt(v_hat) + ADAM_EPS) + ADAM_WD * w)
    return delta, m2, v2


def _adamw(name, gs, w, m, v, TR, row0=0, into=None):
    L = len(gs)
    n, Rp, C = gs[0].shape
    R = w.shape[1]
    b0 = row0 // TR

    def body(*refs):
        g_refs = refs[:L]
        w_ref, m_ref, v_ref = refs[L:L + 3]
        go_ref, do_ref, mo_ref, vo_ref = refs[-4:]
        lay = pl.program_id(0)
        for li in range(L):
            @pl.when(lay == li)
            def _(li=li):
                g = g_refs[li][0].astype(F32)
                for s in range(1, n):
                    g = g + g_refs[li][s].astype(F32)
                delta, m2, v2 = _adamw_math(g, w_ref[...], m_ref[...], v_ref[...])
                go_ref[...] = g
                do_ref[...] = delta
                mo_ref[...] = m2
                vo_ref[...] = v2

    lrc = pl.BlockSpec((None, TR, C), lambda lay, r: (lay, r + b0, 0))
    g_specs = [pl.BlockSpec((n, TR, C), lambda lay, r, li=li: (0, jnp.where(lay == li, r, 0), 0))
               for li in range(L)]
    in_specs, args, aliases = g_specs + [lrc, lrc, lrc], [*gs, w, m, v], None
    if into is not None:
        aliases = {len(args) + k: k for k in range(4)}
        in_specs = in_specs + [pl.BlockSpec(memory_space=pl.ANY)] * 4
        args = args + list(into)
    return _pcall(
        body, name=name, grid=(L, Rp // TR), in_specs=in_specs, out_specs=[lrc] * 4,
        out_shape=[jax.ShapeDtypeStruct((L, R, C), F32)] * 4, args=args, aliases=aliases)


def _ada_adamw(c_all_t, dm, w, m, v, rider=None):
    L, _, nc = w.shape

    def body(c_ref, dm_ref, w_ref, m_ref, v_ref, go_ref, do_ref, mo_ref, vo_ref):
        cv = c_ref[...]
        ca = cv * jax.nn.sigmoid(cv)
        dmv = dm_ref[...]
        g = ca[:, 0:1] * dmv[0:1, :]
        for b in range(1, NDEV):
            g = g + ca[:, b:b + 1] * dmv[b:b + 1, :]
        delta, m2, v2 = _adamw_math(g, w_ref[...], m_ref[...], v_ref[...])
        go_ref[...] = g
        do_ref[...] = delta
        mo_ref[...] = m2
        vo_ref[...] = v2

    big = pl.BlockSpec((None, D, nc), lambda lay: (lay, 0, 0))
    return _pcall(
        body, name="ada_adamw", grid=(L,),
        in_specs=[_full((D, NDEV)), pl.BlockSpec((None, NDEV, nc), lambda lay: (lay, 0, 0)), big, big, big],
        out_specs=[big] * 4, out_shape=[jax.ShapeDtypeStruct((L, D, nc), F32)] * 4,
        args=(c_all_t, dm, w, m, v), rider=rider)


def _sum_slots(recv):
    n, R, C = recv.shape

    def body(r_ref, o_ref):
        acc = r_ref[0].astype(F32)
        for s in range(1, n):
            acc = acc + r_ref[s].astype(F32)
        o_ref[...] = acc

    return pl.pallas_call(body, name="sum_slots", out_shape=jax.ShapeDtypeStruct((R, C), F32),
                          compiler_params=_cparams())(recv)


def _pad_rows(a, rows):
    return jnp.pad(a, ((0, rows - a.shape[0]), (0, 0)))


def _pack_sharded_block(pool_w, pool_b, conv_w):
    return jnp.concatenate([pool_w.reshape(-1, PACK_C), _pad_rows(pool_b.reshape(-1, PACK_C), SUB),
                            _pad_rows(conv_w.reshape(-1, PACK_C), SUB)], axis=0)


def _unpack_sharded_block(p):
    n_pw = DEPTH * NGRP * (GD // NDEV)
    pool_w = p[:n_pw].reshape(DEPTH, NGRP, GD // NDEV, GD)
    pool_b = p[n_pw].reshape(DEPTH, NGRP, GD // NDEV)
    conv_w = p[n_pw + SUB:n_pw + SUB + DEPTH * CONV_K * (D // NDEV) // PACK_C].reshape(DEPTH, CONV_K, D // NDEV)
    return pool_w, pool_b, conv_w


def _blocks_of_full(pool_w, pool_b, conv_w):
    pw = pool_w.reshape(DEPTH, NGRP, NDEV, GD // NDEV, GD).transpose(2, 0, 1, 3, 4).reshape(NDEV, -1, PACK_C)
    pb = pool_b.reshape(DEPTH, NGRP, NDEV, GD // NDEV).transpose(2, 0, 1, 3).reshape(NDEV, -1, PACK_C)
    cw = conv_w.reshape(DEPTH, CONV_K, NDEV, D // NDEV).transpose(2, 0, 1, 3).reshape(NDEV, -1, PACK_C)
    pad = lambda a: jnp.pad(a, ((0, 0), (0, SUB - a.shape[1]), (0, 0)))
    return jnp.concatenate([pw, pad(pb), pad(cw)], axis=1)


def _full_of_blocks(p):
    n_pw = DEPTH * NGRP * (GD // NDEV)
    pool_w = p[:, :n_pw].reshape(NDEV, DEPTH, NGRP, GD // NDEV, GD).transpose(1, 2, 0, 3, 4)
    pool_b = p[:, n_pw].reshape(NDEV, DEPTH, NGRP, GD // NDEV).transpose(1, 2, 0, 3)
    n_cw = DEPTH * CONV_K * (D // NDEV) // PACK_C
    conv_w = p[:, n_pw + SUB:n_pw + SUB + n_cw].reshape(NDEV, DEPTH, CONV_K, D // NDEV).transpose(1, 2, 0, 3)
    return (pool_w.reshape(DEPTH, NGRP, GD, GD), pool_b.reshape(DEPTH, NGRP, GD),
            conv_w.reshape(DEPTH, CONV_K, D))


def _pack_replicated(t, keys, rows):
    p = jnp.concatenate([t[k].reshape(-1, PACK_C) for k in keys], axis=0)
    return _pad_rows(p, rows)


def _unpack_replicated(p, like, keys):
    out, r0 = {}, 0
    for k in keys:
        rows = like[k].size // PACK_C
        out[k] = p[r0:r0 + rows].reshape(like[k].shape)
        r0 += rows
    return out


def kernel(x, c, ada_w, ada_b, pre_norm_g, w_in, conv_w, conv_b, gate_a_w, gate_a_b, gate_x_w, gate_x_b, lru_lambda, pool_w, pool_b, pool_scale, w_out, post_norm_g, loss_target, m_ada_w, m_ada_b, m_pre_norm_g, m_w_in, m_conv_w, m_conv_b, m_gate_a_w, m_gate_a_b, m_gate_x_w, m_gate_x_b, m_lru_lambda, m_pool_w, m_pool_b, m_pool_scale, m_w_out, m_post_norm_g, v_ada_w, v_ada_b, v_pre_norm_g, v_w_in, v_conv_w, v_conv_b, v_gate_a_w, v_gate_a_b, v_gate_x_w, v_gate_x_b, v_lru_lambda, v_pool_w, v_pool_b, v_pool_scale, v_w_out, v_post_norm_g):
    W = dict(ada_w=ada_w, ada_b=ada_b, pre_norm_g=pre_norm_g, w_in=w_in, conv_w=conv_w, conv_b=conv_b,
             gate_a_w=gate_a_w, gate_a_b=gate_a_b, gate_x_w=gate_x_w, gate_x_b=gate_x_b, lru_lambda=lru_lambda,
             pool_w=pool_w, pool_b=pool_b, pool_scale=pool_scale, w_out=w_out, post_norm_g=post_norm_g)
    M = dict(ada_w=m_ada_w, ada_b=m_ada_b, pre_norm_g=m_pre_norm_g, w_in=m_w_in, conv_w=m_conv_w,
             conv_b=m_conv_b, gate_a_w=m_gate_a_w, gate_a_b=m_gate_a_b, gate_x_w=m_gate_x_w,
             gate_x_b=m_gate_x_b, lru_lambda=m_lru_lambda, pool_w=m_pool_w, pool_b=m_pool_b,
             pool_scale=m_pool_scale, w_out=m_w_out, post_norm_g=m_post_norm_g)
    V = dict(ada_w=v_ada_w, ada_b=v_ada_b, pre_norm_g=v_pre_norm_g, w_in=v_w_in, conv_w=v_conv_w,
             conv_b=v_conv_b, gate_a_w=v_gate_a_w, gate_a_b=v_gate_a_b, gate_x_w=v_gate_x_w,
             gate_x_b=v_gate_x_b, lru_lambda=v_lru_lambda, pool_w=v_pool_w, pool_b=v_pool_b,
             pool_scale=v_pool_scale, w_out=v_w_out, post_norm_g=v_post_norm_g)
    S = x.shape[1]
    me = 4 * lax.axis_index("x") + 2 * lax.axis_index("y") + lax.axis_index("c")
    xs = x.reshape(S, D)
    tgt = loss_target.reshape(S, D)
    nc = ada_w.shape[2]
    NB = w_in.shape[2]
    w_in_b, w_out_b = w_in.astype(BF16), w_out.astype(BF16)
    rows = lambda a: a.reshape(DEPTH, 1, D)
    P = dict(pre_norm_g=rows(pre_norm_g), conv_b=rows(conv_b), gate_a_b=rows(gate_a_b), gate_x_b=rows(gate_x_b),
             lru_lambda=rows(lru_lambda), pool_scale=rows(pool_scale), post_norm_g=rows(post_norm_g),
             gate_a_w=gate_a_w.astype(BF16), gate_x_w=gate_x_w.astype(BF16))

    c_slots, w_in0 = _exchange("gather_c_w_in0", _AllGather2([jnp.broadcast_to(c, (SUB, D)), w_in_b[0]]))
    c_all = c_slots[:, 0, :]
    (mod_slots,) = _exchange("gather_mod", _Direct(ag=[_mod_cols(c_all, ada_w)]))
    mod = lax.dynamic_index_in_dim(mod_slots, me, axis=1, keepdims=False)
    mod = (mod.reshape(NDEV, DEPTH, nc).transpose(1, 0, 2).reshape(DEPTH, 3 * D) + ada_b).reshape(DEPTH, 3, 1, D)

    w_in_all, w_out_all = [w_in0, None], [None, None]
    saved = []
    xl = xs
    flat = lambda w_slots: w_slots.reshape(2 * D, D)
    for l in range(DEPTH):
        rider = _AllGather2([_pack_sharded_block(pool_w, pool_b, conv_w), w_out_b[0]]) if l == 0 else None
        (h_a, h_b, proj), got = _pre_proj(xl, P, mod, w_in_all[l], l, rider=rider)
        if l == 0:
            pool_w_f, pool_b_f, conv_w_f = _full_of_blocks(got[0])
            P.update(conv_w=conv_w_f, pool_w=pool_w_f.astype(BF16), pool_b=rows(pool_b_f))
            w_out_all[0] = flat(got[1])
        rider = _AllGather2([w_in_b[1]]) if l == 0 else None
        (hs, yr, *fwd), got = _rnn_fwd(proj, P, l, rider=rider)
        if l == 0:
            w_in_all[1] = got[0]
        (yp,), _ = _pool_fwd(proj, P, l)
        if l == DEPTH - 1:
            (y, x_next, loss_acc), _ = _out_post(yr, yp, w_out_all[l], xl, mod, P, l, tgt)
        else:
            (y, x_next), got = _out_post(yr, yp, w_out_all[l], xl, mod, P, l, rider=_AllGather2([w_out_b[1]]))
            w_out_all[1] = flat(got[0])
        saved.append((xl, h_a, h_b, proj, hs, fwd, yr, yp, y))
        xl = x_next

    dxo = xl
    G = {k: [None] * DEPTH for k in WEIGHTS}
    dmod = [None] * DEPTH
    recv_in, recv_out = [[None, None] for _ in range(DEPTH)], [None] * DEPTH
    full = dict(conv_w=(CONV_K, D), pool_w=(NGRP, GD, GD), pool_b=(NGRP, GD))
    stack = lambda k: jnp.stack([g.reshape(full.get(k, W[k].shape[1:])) for g in G[k]])
    gw_bot_prev = None
    for l in reversed(range(DEPTH)):
        xin, h_a, h_b, proj, hs, fwd, yr, yp, y = saved[l]
        rider = _AllGather2([loss_acc]) if gw_bot_prev is None else _Direct(a2a=[gw_bot_prev])
        (dyr, dyp, gw_out, dgate, G['post_norm_g'][l]), got = _out_bwd(dxo, y, yr, yp, w_out_all[l], mod, P, l,
                                                                       rider=rider)
        if gw_bot_prev is None:
            loss = (0.5 / D) * jnp.sum(got[0])
        else:
            recv_in[l + 1][1] = got[0]
        ((dxr, dgr, G['conv_w'][l], G['conv_b'][l], G['gate_a_w'][l], G['gate_a_b'][l], G['gate_x_w'][l],
          G['gate_x_b'][l], G['lru_lambda'][l]), (recv_out[l],)) = _rnn_bwd(
            dyr, hs, fwd, proj, P, l,
            rider=_Direct(a2a=[gw_out.reshape(NDEV, 2 * D // NDEV, D)]))
        (dxp, dgp, G['pool_w'][l], G['pool_b'][l], G['pool_scale'][l]), _ = _pool_bwd(dyp, proj, P, l)
        dq = (dxr, dgr, dxp, dgp)
        if l > 0:
            (gw_top,), _ = _grad_w_in(h_a, dq, NB)
            (dxo, dshift, dscale, G['pre_norm_g'][l]), (recv_in[l][0],) = _in_bwd(
                dq, w_in_all[l], xin, dxo, P, mod, l, rider=_Direct(a2a=[gw_top]))
            (gw_bot_prev,), _ = _grad_w_in(h_b, dq, NB)
        else:
            Ge = {k: stack(k) for k in REP_EARLY + ['pool_w', 'pool_b', 'conv_w']}
            early = jnp.concatenate([_blocks_of_full(Ge['pool_w'], Ge['pool_b'], Ge['conv_w']),
                                     _pack_replicated(Ge, REP_EARLY, NDEV * PACK_ROWS).reshape(NDEV, PACK_ROWS, PACK_C)],
                                    axis=1).astype(BF16)
            (gw_top,), (early_recv,) = _grad_w_in(h_a, dq, NB, rider=_Direct(a2a=[early]))
            early_sum = _sum_slots(early_recv)
            (gw_bot,), (recv_in[l][0],) = _grad_w_in(h_b, dq, NB, rider=_Direct(a2a=[gw_top]))
            (dxo, dshift, dscale, G['pre_norm_g'][l]), (recv_in[l][1], early_all) = _in_bwd(
                dq, w_in_all[l], xin, dxo, P, mod, l,
                rider=_Both(_Direct(a2a=[gw_bot]), _AllGather2([early_sum[PACK_ROWS:]])))
        dmod[l] = jnp.concatenate([dshift, dscale, dgate], axis=1)
    grad_x = dxo.reshape(x.shape)

    Gl = dict(ada_b=jnp.concatenate(dmod, axis=0), pre_norm_g=stack('pre_norm_g'))
    (late_slots,) = _exchange("gather_late", _AllGather2([_pack_replicated(Gl, REP_LATE, LATE_PACK_ROWS)]))
    late_sum = _sum_slots(late_slots)
    dmod_all = late_slots[:, :DEPTH * 3 * D // PACK_C]

    out = {}
    first, _ = _adamw("adamw_w_in_a", [recv_in[l][0] for l in range(DEPTH)], w_in, M['w_in'], V['w_in'], LANE)
    out['w_in'], _ = _adamw("adamw_w_in_b", [recv_in[l][1] for l in range(DEPTH)], w_in, M['w_in'], V['w_in'], LANE,
                            row0=GW_SPLIT, into=first)
    out['w_out'], _ = _adamw("adamw_w_out", recv_out, w_out, M['w_out'], V['w_out'], 256)
    dm = lax.dynamic_slice_in_dim(dmod_all.reshape(NDEV, DEPTH, 3 * D), me * nc, nc, axis=2)
    out['ada_w'], _ = _ada_adamw(c_all.T, dm.transpose(1, 0, 2), ada_w, M['ada_w'], V['ada_w'])
    g_small = jnp.concatenate([early_sum[:PACK_ROWS], early_all.reshape(NDEV * PACK_ROWS, PACK_C), late_sum],
                              axis=0)

    def packs(T):
        return jnp.concatenate([_pack_sharded_block(T['pool_w'], T['pool_b'], T['conv_w']),
                                _pack_replicated(T, REP_EARLY, NDEV * PACK_ROWS),
                                _pack_replicated(T, REP_LATE, LATE_PACK_ROWS)], axis=0)[None]
    res_small, _ = _adamw("adamw_small", [g_small[None]], packs(W), packs(M), packs(V), g_small.shape[0] // 2)
    n_early = (1 + NDEV) * PACK_ROWS
    for idx in range(4):
        p = res_small[idx][0]
        pw_, pb_, cw_ = _unpack_sharded_block(p[:PACK_ROWS])
        rep = _unpack_replicated(p[PACK_ROWS:n_early], W, REP_EARLY)
        rep.update(_unpack_replicated(p[n_early:], W, REP_LATE))
        rep.update(pool_w=pw_, pool_b=pb_, conv_w=cw_)
        for k, a in rep.items():
            out.setdefault(k, [None] * 4)[idx] = a
    for k in ('w_in', 'w_out', 'ada_w'):
        out[k] = [a.reshape(W[k].shape) for a in out[k]]

    return (loss, grad_x, *[out[k][0] for k in WEIGHTS], *[out[k][1] for k in WEIGHTS],
            *[out[k][2] for k in WEIGHTS], *[out[k][3] for k in WEIGHTS])
```

```python
import functools

import jax
import jax.numpy as jnp
from jax import lax
from jax.experimental import pallas as pl
from jax.experimental.pallas import tpu as pltpu

F32, BF16 = jnp.float32, jnp.bfloat16
MESH = pl.DeviceIdType.MESH
HIGHEST = lax.Precision.HIGHEST

NDEV = 8
DEPTH = 2
D = 1024
NHEAD, HD = 8, 128
NGRP, GD = 4, 256
WINS = (2, 4, 8, 16)
CONV_K = 4
CONV_HALO = 8
POOL_HALO = 16
LRU_C = 8.0
NORM_EPS = 1e-6
ADAM_LR, ADAM_B1, ADAM_B2, ADAM_EPS, ADAM_WD, ADAM_STEP = 0.001, 0.9, 0.999, 1e-08, 0.01, 10
VMEM_LIMIT = 56 * 1024 * 1024
NQ = 4
SUB = 8
LANE = 128
PACK_C = 256
PACK_ROWS = 272

WEIGHTS = ['ada_w', 'ada_b', 'pre_norm_g', 'w_in', 'conv_w', 'conv_b', 'gate_a_w', 'gate_a_b', 'gate_x_w',
           'gate_x_b', 'lru_lambda', 'pool_w', 'pool_b', 'pool_scale', 'w_out', 'post_norm_g']
REP_EARLY = ['conv_b', 'gate_a_w', 'gate_a_b', 'gate_x_w', 'gate_x_b', 'lru_lambda', 'pool_scale', 'post_norm_g']
REP_LATE = ['ada_b', 'pre_norm_g']
GW_SPLIT = 512
LATE_PACK_ROWS = 32


def _cparams(*sem):
    return pltpu.CompilerParams(dimension_semantics=sem, vmem_limit_bytes=VMEM_LIMIT)


def _vec(l, k=None):
    if k is None:
        return pl.BlockSpec((None, 1, D), lambda *_: (l, 0, 0))
    return pl.BlockSpec((None, None, 1, D), lambda *_: (l, k, 0, 0))


def _layer(l, shape):
    nd = len(shape)
    return pl.BlockSpec((None,) + tuple(shape), lambda *_: (l,) + (0,) * nd)


def _full(shape):
    nd = len(shape)
    return pl.BlockSpec(shape, lambda *_: (0,) * nd)


def _rowsum8(z):
    return z.reshape(z.shape[0] // SUB, SUB, z.shape[1]).sum(axis=0)


def _sum8(acc):
    return jnp.sum(acc, axis=0, keepdims=True)


def _sigmoid(z):
    return 0.5 * jnp.tanh(0.5 * z) + 0.5


def _silu_parts(g):
    sg = _sigmoid(g)
    return g * sg, sg * (1.0 + g * (1.0 - sg))


def _one_minus_sq(a, log_a):
    z = 2.0 * log_a
    p = 1.0 / 24.0
    for k in (6.0, 2.0, 1.0):
        p = p * z + 1.0 / k
    return jnp.where(z > -0.03, -(p * z), 1.0 - a * a)


def _place():
    x, y, c = lax.axis_index("x"), lax.axis_index("y"), lax.axis_index("c")
    return x, y, c, 4 * x + 2 * y + c


class _Direct:
    def __init__(self, a2a=(), ag=()):
        self.arrays = list(a2a) + list(ag)
        self.n_a, self.n = len(a2a), len(self.arrays)
        self.out_shape = ([jax.ShapeDtypeStruct(a.shape, a.dtype) for a in a2a]
                          + [jax.ShapeDtypeStruct((NDEV,) + a.shape, a.dtype) for a in ag])
        self.scratch = [pltpu.SemaphoreType.DMA((self.n, NDEV - 1)), pltpu.SemaphoreType.DMA((self.n, NDEV - 1)),
                        pltpu.SemaphoreType.DMA((self.n,))]

    def _copies(self, ins, outs, sems):
        send_sems, recv_sems, local_sems = sems
        x, y, c, me = _place()
        local, remote = [], []
        for t in range(self.n):
            src = ins[t].at[me] if t < self.n_a else ins[t]
            local.append(pltpu.make_async_copy(src, outs[t].at[me], local_sems.at[t]))
        for r in range(1, NDEV):
            px = 1 - x if r & 4 else x
            py = 1 - y if r & 2 else y
            pc = 1 - c if r & 1 else c
            for t in range(self.n):
                src = ins[t].at[4 * px + 2 * py + pc] if t < self.n_a else ins[t]
                remote.append(pltpu.make_async_remote_copy(
                    src_ref=src, dst_ref=outs[t].at[me], send_sem=send_sems.at[t, r - 1],
                    recv_sem=recv_sems.at[t, r - 1], device_id=(px, py, pc), device_id_type=MESH))
        return local, remote

    def start(self, ins, outs, sems):
        local, remote = self._copies(ins, outs, sems)
        for cp in local + remote:
            cp.start()

    def finish(self, ins, outs, sems):
        local, remote = self._copies(ins, outs, sems)
        for cp in remote + local:
            cp.wait()


class _AllGather2:
    def __init__(self, arrays):
        self.arrays = list(arrays)
        self.n = len(self.arrays)
        self.out_shape = [jax.ShapeDtypeStruct((NDEV,) + a.shape, a.dtype) for a in self.arrays]
        self.scratch = [pltpu.SemaphoreType.DMA((self.n, NDEV - 1)), pltpu.SemaphoreType.DMA((self.n, NDEV - 1)),
                        pltpu.SemaphoreType.DMA((self.n,))]

    @staticmethod
    def _chips(x, y):
        return [(1 - x, y), (x, 1 - y), (1 - x, 1 - y)]

    def _copy(self, t, k, src, dst, to, sems):
        return pltpu.make_async_remote_copy(src_ref=src, dst_ref=dst, send_sem=sems[0].at[t, k],
                                            recv_sem=sems[1].at[t, k], device_id=to, device_id_type=MESH)

    def start(self, ins, outs, sems):
        x, y, c, me = _place()
        for t in range(self.n):
            pltpu.make_async_copy(ins[t], outs[t].at[me], sems[2].at[t]).start()
            self._copy(t, 0, ins[t], outs[t].at[me], (x, y, 1 - c), sems).start()
            for j, (px, py) in enumerate(self._chips(x, y)):
                self._copy(t, 1 + j, ins[t], outs[t].at[me], (px, py, c), sems).start()

    def finish(self, ins, outs, sems):
        x, y, c, me = _place()
        sib = (x, y, 1 - c)
        for j, (px, py) in enumerate(self._chips(x, y)):
            slot = 4 * px + 2 * py + c
            for t in range(self.n):
                self._copy(t, 1 + j, ins[t], outs[t].at[slot], sib, sems).wait_recv()
                self._copy(t, 4 + j, outs[t].at[slot], outs[t].at[slot], sib, sems).start()
        for t in range(self.n):
            for k in (0, 4, 5, 6):
                self._copy(t, k, ins[t], outs[t].at[me], sib, sems).wait_recv()
        for t in range(self.n):
            for k in range(NDEV - 1):
                self._copy(t, k, ins[t], outs[t].at[me], sib, sems).wait_send()
            pltpu.make_async_copy(ins[t], outs[t].at[me], sems[2].at[t]).wait()


class _Both:
    def __init__(self, *riders):
        self.riders = riders
        self.arrays = [a for r in riders for a in r.arrays]
        self.n = len(self.arrays)
        self.out_shape = [o for r in riders for o in r.out_shape]
        self.scratch = [s for r in riders for s in r.scratch]

    def _parts(self, ins, outs, sems):
        p, q = 0, 0
        for r in self.riders:
            yield r, ins[p:p + r.n], outs[p:p + r.n], sems[q:q + len(r.scratch)]
            p, q = p + r.n, q + len(r.scratch)

    def start(self, ins, outs, sems):
        for r, i, o, s in self._parts(ins, outs, sems):
            r.start(i, o, s)

    def finish(self, ins, outs, sems):
        for r, i, o, s in self._parts(ins, outs, sems):
            r.finish(i, o, s)


def _exchange(name, rider):
    n = rider.n

    def body(*refs):
        rider.start(refs[:n], refs[n:2 * n], refs[2 * n:])
        rider.finish(refs[:n], refs[n:2 * n], refs[2 * n:])

    any_spec = pl.BlockSpec(memory_space=pl.ANY)
    return list(pl.pallas_call(body, name=name, out_shape=rider.out_shape, in_specs=[any_spec] * n,
                               out_specs=[any_spec] * n, scratch_shapes=rider.scratch)(*rider.arrays))


def _pcall(body, *, name, grid, in_specs, out_specs, out_shape, args, scratch_shapes=(), rider=None, aliases=None):
    params = _cparams(*(("arbitrary",) * len(grid)))
    if rider is None:
        res = pl.pallas_call(body, name=name, grid=grid, in_specs=in_specs, out_specs=out_specs,
                             out_shape=out_shape, scratch_shapes=list(scratch_shapes),
                             input_output_aliases=aliases or {}, compiler_params=params)(*args)
        return list(res), []
    n_in, n_out, n_scr, rn = len(in_specs), len(out_specs), len(scratch_shapes), rider.n

    def wrapped(*refs):
        cuts = [n_in, rn, n_out, rn, n_scr]
        parts, p = [], 0
        for n in cuts:
            parts.append(refs[p:p + n])
            p += n
        ins, r_in, outs, r_out, scr = parts
        sems = refs[p:]
        ids = [pl.program_id(a) for a in range(len(grid))]
        first = functools.reduce(jnp.logical_and, [i == 0 for i in ids])
        last = functools.reduce(jnp.logical_and, [i == g - 1 for i, g in zip(ids, grid)])

        @pl.when(first)
        def _():
            rider.start(r_in, r_out, sems)
        body(*ins, *outs, *scr)

        @pl.when(last)
        def _():
            rider.finish(r_in, r_out, sems)

    any_spec = pl.BlockSpec(memory_space=pl.ANY)
    res = pl.pallas_call(
        wrapped, name=name, grid=grid, in_specs=list(in_specs) + [any_spec] * rn,
        out_specs=list(out_specs) + [any_spec] * rn, out_shape=list(out_shape) + rider.out_shape,
        scratch_shapes=list(scratch_shapes) + rider.scratch, compiler_params=params)(*args, *rider.arrays)
    return list(res[:n_out]), list(res[n_out:])


def _mod_cols(c_all, ada_w):
    nc = ada_w.shape[2]

    def body(c_ref, w_ref, o_ref):
        cv = c_ref[...]
        ca = cv * jax.nn.sigmoid(cv)
        for l in range(DEPTH):
            o_ref[:, l * nc:(l + 1) * nc] = jnp.dot(ca, w_ref[l], precision=HIGHEST, preferred_element_type=F32)

    return pl.pallas_call(body, name="mod_cols", out_shape=jax.ShapeDtypeStruct((NDEV, DEPTH * nc), F32),
                          compiler_params=_cparams())(c_all, ada_w)


def _pre_proj(x, P, mod, w_in_l, l, rider=None):
    S = x.shape[0]
    TM = min(512, S)
    NB = w_in_l.shape[2]

    def body(x_ref, g_ref, sc_ref, sh_ref, w_ref, ha_ref, hb_ref, p_ref):
        xv = x_ref[...]
        rstd = lax.rsqrt(jnp.mean(xv * xv, axis=-1, keepdims=True) + NORM_EPS)
        h = ((xv * rstd * g_ref[...]) * (1.0 + sc_ref[...]) + sh_ref[...]).astype(BF16)
        ha_ref[...] = h[:, :GW_SPLIT]
        hb_ref[...] = h[:, GW_SPLIT:]
        for j in range(NDEV):
            p_ref[:, j * NB:(j + 1) * NB] = jnp.dot(h, w_ref[j], preferred_element_type=F32)

    row = pl.BlockSpec((TM, D), lambda i: (i, 0))
    return _pcall(
        body, name="pre_proj", grid=(S // TM,),
        in_specs=[row, _vec(l), _vec(l, 1), _vec(l, 0), _full((NDEV, D, NB))],
        out_specs=[pl.BlockSpec((TM, GW_SPLIT), lambda i: (i, 0)), pl.BlockSpec((TM, D - GW_SPLIT), lambda i: (i, 0)),
                   pl.BlockSpec((TM, NDEV * NB), lambda i: (i, 0))],
        out_shape=[jax.ShapeDtypeStruct((S, GW_SPLIT), BF16), jax.ShapeDtypeStruct((S, D - GW_SPLIT), BF16),
                   jax.ShapeDtypeStruct((S, NDEV * NB), F32)],
        args=(x, P['pre_norm_g'], mod, mod, w_in_l), rider=rider)


def _taps(E):
    return [pltpu.roll(E, CONV_K - 1 - k, axis=0)[CONV_HALO:, :] for k in range(CONV_K - 1)] + [E[CONV_HALO:, :]]


def _conv(E, cw_ref, cb_ref):
    w = cw_ref[...]
    taps = _taps(E)
    acc = cb_ref[...] + taps[0] * w[0:1, :]
    for k in range(1, CONV_K):
        acc = acc + taps[k] * w[k:k + 1, :]
    return acc


CH_R, CH_C = 16, 512


def _chunks(T, fn):
    def step(c, carry):
        rows = pl.ds(pl.multiple_of(c * CH_R, CH_R), CH_R)
        for hf in range(D // CH_C):
            fn(rows, slice(hf * CH_C, (hf + 1) * CH_C), hf)
        return carry
    lax.fori_loop(0, T // CH_R, step, 0)


def _to_scan(ref, rows, hf, val):
    for q in range(CH_C // LANE):
        ref[hf * (CH_C // LANE) + q, rows, :] = val[:, q * LANE:(q + 1) * LANE]


def _scan(sa, sv, carry_ref, out_ref, reverse):
    NC, T, _ = sa.shape
    n8 = T // SUB
    rows = range(SUB - 2, -1, -1) if reverse else range(1, SUB)
    for cb in range(NC):
        r_in = SUB - 1 if reverse else 0
        Ap = sa[cb, pl.ds(r_in, n8, stride=SUB), :]
        Vp = sv[cb, pl.ds(r_in, n8, stride=SUB), :]
        for r in rows:
            Ar = sa[cb, pl.ds(r, n8, stride=SUB), :]
            Vp = sv[cb, pl.ds(r, n8, stride=SUB), :] + Ar * Vp
            Ap = Ar * Ap
            sa[cb, pl.ds(r, n8, stride=SUB), :] = Ap
            sv[cb, pl.ds(r, n8, stride=SUB), :] = Vp
    edge = 0 if reverse else SUB - 1

    def step(k, c):
        r0 = pl.multiple_of((n8 - 1 - k if reverse else k) * SUB, SUB)
        h = jnp.concatenate([sv[cb, pl.ds(r0, SUB), :] + sa[cb, pl.ds(r0, SUB), :] * c[:, cb * LANE:(cb + 1) * LANE]
                             for cb in range(NC)], axis=1)
        out_ref[pl.ds(r0, SUB), :] = h
        return jnp.broadcast_to(h[edge:edge + 1, :], (SUB, D))

    carry_ref[...] = lax.fori_loop(0, n8, step, carry_ref[...])


def _rnn_fwd(proj, P, l, rider=None):
    S = proj.shape[0]
    TB = min(256, S)

    def body(xr_ref, g_ref, cw_ref, cb_ref, wa_ref, ba_ref, wx_ref, bx_ref, lam_ref, hs_ref, y_ref,
             u_ref, r_ref, i_ref, a_ref, m_ref, xbuf, sa, sv, hc):
        @pl.when(pl.program_id(0) == 0)
        def _():
            xbuf[0:CONV_HALO, :] = jnp.zeros((CONV_HALO, D), F32)
            hc[...] = jnp.zeros((SUB, D), F32)
        xbuf[CONV_HALO:, :] = xr_ref[...]
        u = _conv(xbuf[...], cw_ref, cb_ref)
        xbuf[0:CONV_HALO, :] = xbuf[TB:TB + CONV_HALO, :]
        ub = u.astype(BF16)
        zr = jnp.concatenate([jnp.dot(ub[:, h * HD:(h + 1) * HD], wa_ref[h], preferred_element_type=F32)
                              for h in range(NHEAD)], axis=1)
        zi = jnp.concatenate([jnp.dot(ub[:, h * HD:(h + 1) * HD], wx_ref[h], preferred_element_type=F32)
                              for h in range(NHEAD)], axis=1)
        r = _sigmoid(zr + ba_ref[...])
        ig = _sigmoid(zi + bx_ref[...])
        log_a = r * (-LRU_C * jax.nn.softplus(-lam_ref[...]))
        a = jnp.exp(log_a)
        mult = jnp.sqrt(_one_minus_sq(a, log_a))
        v = mult * (ig * u)
        u_ref[...] = u
        r_ref[...] = r
        i_ref[...] = ig
        a_ref[...] = a
        m_ref[...] = mult
        for cb in range(D // LANE):
            sa[cb] = a[:, cb * LANE:(cb + 1) * LANE]
            sv[cb] = v[:, cb * LANE:(cb + 1) * LANE]
        _scan(sa, sv, hc, hs_ref, reverse=False)
        silu, _ = _silu_parts(g_ref[...])
        y_ref[...] = (hs_ref[...] * silu).astype(BF16)

    rowb = pl.BlockSpec((TB, D), lambda i: (i, 0))
    return _pcall(
        body, name="rnn_fwd", grid=(S // TB,),
        in_specs=[rowb, pl.BlockSpec((TB, D), lambda i: (i, 1)), _layer(l, (CONV_K, D)), _vec(l),
                  _layer(l, (NHEAD, HD, HD)), _vec(l), _layer(l, (NHEAD, HD, HD)), _vec(l), _vec(l)],
        out_specs=[rowb] * 7,
        out_shape=[jax.ShapeDtypeStruct((S, D), F32), jax.ShapeDtypeStruct((S, D), BF16)]
        + [jax.ShapeDtypeStruct((S, D), F32)] * 5,
        scratch_shapes=[pltpu.VMEM((TB + CONV_HALO, D), F32), pltpu.VMEM((D // LANE, TB, LANE), F32),
                        pltpu.VMEM((D // LANE, TB, LANE), F32), pltpu.VMEM((SUB, D), F32)],
        args=(proj, proj, P['conv_w'], P['conv_b'], P['gate_a_w'], P['gate_a_b'], P['gate_x_w'], P['gate_x_b'],
              P['lru_lambda']), rider=rider)


def _pooled(ebuf, t0, TB):
    tt = t0 + lax.broadcasted_iota(jnp.int32, (TB, 1), 0)
    pooled, inv = [], []
    for g, win in enumerate(WINS):
        Eg = ebuf[:, g * GD:(g + 1) * GD]
        L = Eg
        for lev in range(g + 1):
            L = L + pltpu.roll(L, 1 << lev, axis=0)
        icnt = 1.0 / jnp.minimum(tt + 1, win).astype(F32)
        pooled.append(L[POOL_HALO:, :] * icnt - Eg[POOL_HALO:, :])
        inv.append(icnt)
    return pooled, inv


def _pool_fwd(proj, P, l, rider=None):
    S = proj.shape[0]
    TB = min(512, S)

    def body(xp_ref, g_ref, pw_ref, pb_ref, ps_ref, y_ref, ebuf):
        i = pl.program_id(0)

        @pl.when(i == 0)
        def _():
            ebuf[0:POOL_HALO, :] = jnp.zeros((POOL_HALO, D), F32)
        ebuf[POOL_HALO:, :] = xp_ref[...]
        pooled, _ = _pooled(ebuf, i * TB, TB)
        ebuf[0:POOL_HALO, :] = ebuf[TB:TB + POOL_HALO, :]
        yp = jnp.concatenate([jnp.dot(pooled[g].astype(BF16), pw_ref[g], preferred_element_type=F32)
                              for g in range(NGRP)], axis=1) + pb_ref[...]
        silu, _ = _silu_parts(g_ref[...])
        y_ref[...] = (yp * ps_ref[...] * silu).astype(BF16)

    return _pcall(
        body, name="pool_fwd", grid=(S // TB,),
        in_specs=[pl.BlockSpec((TB, D), lambda i: (i, 2)), pl.BlockSpec((TB, D), lambda i: (i, 3)),
                  _layer(l, (NGRP, GD, GD)), _vec(l), _vec(l)],
        out_specs=[pl.BlockSpec((TB, D), lambda i: (i, 0))],
        out_shape=[jax.ShapeDtypeStruct((S, D), BF16)],
        scratch_shapes=[pltpu.VMEM((TB + POOL_HALO, D), F32)],
        args=(proj, proj, P['pool_w'], P['pool_b'], P['pool_scale']), rider=rider)


def _out_post(yr, yp, w_out_l, x, mod, P, l, target=None, rider=None):
    S = x.shape[0]
    TM = min(512, S)
    last = target is not None

    def body(*refs):
        if last:
            yr_ref, yp_ref, w_ref, x_ref, gate_ref, gp_ref, t_ref, y_ref, xo_ref, loss_ref = refs
        else:
            yr_ref, yp_ref, w_ref, x_ref, gate_ref, gp_ref, y_ref, xo_ref = refs
        acc = (jnp.dot(yr_ref[...], w_ref[0:D, :], preferred_element_type=F32)
               + jnp.dot(yp_ref[...], w_ref[D:2 * D, :], preferred_element_type=F32))
        y_ref[...] = acc
        rstd = lax.rsqrt(jnp.mean(acc * acc, axis=-1, keepdims=True) + NORM_EPS)
        xn = x_ref[...] + gate_ref[...] * (acc * rstd * gp_ref[...])
        if last:
            err = xn - t_ref[...]
            xo_ref[...] = err * (1.0 / D)

            @pl.when(pl.program_id(0) == 0)
            def _():
                loss_ref[...] = jnp.zeros((SUB, D), F32)
            loss_ref[...] += _rowsum8(err * err)
        else:
            xo_ref[...] = xn

    row = pl.BlockSpec((TM, D), lambda i: (i, 0))
    in_specs = [row, row, _full((2 * D, D)), row, _vec(l, 2), _vec(l)]
    out_specs = [row, row]
    out_shape = [jax.ShapeDtypeStruct((S, D), F32), jax.ShapeDtypeStruct((S, D), F32)]
    args = [yr, yp, w_out_l, x, mod, P['post_norm_g']]
    if last:
        in_specs.append(row)
        out_specs.append(_full((SUB, D)))
        out_shape.append(jax.ShapeDtypeStruct((SUB, D), F32))
        args.append(target)
    return _pcall(body, name="out_post_loss" if last else "out_post", grid=(S // TM,), in_specs=in_specs,
                  out_specs=out_specs, out_shape=out_shape, args=args, rider=rider)


def _out_bwd(dxo, y, yr, yp, w_out_l, mod, P, l, rider=None):
    S = y.shape[0]
    TM = min(512, S)
    nsteps = S // TM

    def body(dxo_ref, y_ref, yr_ref, yp_ref, w_ref, gate_ref, gp_ref, dyr_ref, dyp_ref, gw_ref, dgate_ref,
             dgp_ref, gw_acc, vacc):
        i = pl.program_id(0)

        @pl.when(i == 0)
        def _():
            gw_acc[...] = jnp.zeros_like(gw_acc)
            vacc[...] = jnp.zeros_like(vacc)
        yv = y_ref[...]
        dxo_v = dxo_ref[...]
        rstd = lax.rsqrt(jnp.mean(yv * yv, axis=-1, keepdims=True) + NORM_EPS)
        n = yv * rstd
        gp = gp_ref[...]
        vacc[0] += _rowsum8(dxo_v * (n * gp))
        drn = dxo_v * gate_ref[...]
        vacc[1] += _rowsum8(drn * n)
        dn = drn * gp
        dy = (rstd * (dn - n * jnp.mean(dn * n, axis=-1, keepdims=True))).astype(BF16)
        dyr_ref[...] = lax.dot_general(dy, w_ref[0:D, :], (((1,), (1,)), ((), ())), preferred_element_type=F32)
        dyp_ref[...] = lax.dot_general(dy, w_ref[D:2 * D, :], (((1,), (1,)), ((), ())),
                                       preferred_element_type=F32)
        gw_acc[0:D, :] += lax.dot_general(yr_ref[...], dy, (((0,), (0,)), ((), ())), preferred_element_type=F32)
        gw_acc[D:2 * D, :] += lax.dot_general(yp_ref[...], dy, (((0,), (0,)), ((), ())),
                                              preferred_element_type=F32)

        @pl.when(i == nsteps - 1)
        def _():
            gw_ref[...] = gw_acc[...].astype(BF16)
            dgate_ref[...] = _sum8(vacc[0])
            dgp_ref[...] = _sum8(vacc[1])

    row = pl.BlockSpec((TM, D), lambda i: (i, 0))
    return _pcall(
        body, name="out_bwd", grid=(nsteps,),
        in_specs=[row, row, row, row, _full((2 * D, D)), _vec(l, 2), _vec(l)],
        out_specs=[row, row, _full((2 * D, D)), _full((1, D)), _full((1, D))],
        out_shape=[jax.ShapeDtypeStruct((S, D), F32), jax.ShapeDtypeStruct((S, D), F32),
                   jax.ShapeDtypeStruct((2 * D, D), BF16), jax.ShapeDtypeStruct((1, D), F32),
                   jax.ShapeDtypeStruct((1, D), F32)],
        scratch_shapes=[pltpu.VMEM((2 * D, D), F32), pltpu.VMEM((2, SUB, D), F32)],
        args=(dxo, y, yr, yp, w_out_l, mod, P['post_norm_g']), rider=rider)


def _rnn_bwd(dyr, hs, fwd, proj, P, l, rider=None):
    S = proj.shape[0]
    TB = min(256, S)
    nb = S // TB
    TE = TB + CONV_HALO
    A_BA, A_BX, A_LAM, A_CB, A_CW = 0, 1, 2, 3, 4

    def blk(i):
        return nb - 1 - i

    def body(dyr_ref, hs_ref, hprev_ref, u_ref, r_ref, i_ref, a_ref, m_ref, xr_ref, g_ref, cw_ref,
             wa_ref, wx_ref, lam_ref, dxr_ref, dg_ref, gcw_ref, gcb_ref, gwa_ref, gba_ref, gwx_ref, gbx_ref,
             glam_ref, hbuf, abuf, dbuf, sa, sv, dh_ref, hp_ref, dzr_ref, dzi_ref, dhc, vacc):
        i = pl.program_id(0)
        first = blk(i) == 0

        @pl.when(i == 0)
        def _():
            abuf[TB:, :] = jnp.zeros((CONV_HALO, D), F32)
            dbuf[TB:, :] = jnp.zeros((CONV_HALO, D), F32)
            dhc[...] = jnp.zeros_like(dhc)
            vacc[...] = jnp.zeros_like(vacc)
            gwa_ref[...] = jnp.zeros_like(gwa_ref)
            gwx_ref[...] = jnp.zeros_like(gwx_ref)

        hbuf[0:CONV_HALO, :] = jnp.where(first, 0.0, hprev_ref[...])
        hbuf[CONV_HALO:, :] = hs_ref[...]
        hp_ref[...] = pltpu.roll(hbuf[...], 1, axis=0)[CONV_HALO:, :]
        abuf[0:TB, :] = a_ref[...]
        b = pltpu.roll(abuf[...], TE - 1, axis=0)[0:TB, :]
        for cb in range(D // LANE):
            sa[cb] = b[:, cb * LANE:(cb + 1) * LANE]
        abuf[TB:, :] = a_ref[0:CONV_HALO, :]

        def gate_bwd(rows, cs, hf):
            silu, dsilu = _silu_parts(g_ref[rows, cs])
            dy = dyr_ref[rows, cs]
            dg_ref[rows, cs] = (dy * hs_ref[rows, cs] * dsilu).astype(BF16)
            _to_scan(sv, rows, hf, dy * silu)
        _chunks(TB, gate_bwd)

        _scan(sa, sv, dhc, dh_ref, reverse=True)

        csp = -LRU_C * jax.nn.softplus(-lam_ref[...])

        def lru_bwd(rows, cs, hf):
            dh, a, ig, u, mult, r = dh_ref[rows, cs], a_ref[rows, cs], i_ref[rows, cs], u_ref[rows, cs], \
                m_ref[rows, cs], r_ref[rows, cs]
            dlog_a = dh * hp_ref[rows, cs] * a - (dh * ig * u) * (a * a) / mult
            dzr = dlog_a * csp[:, cs] * r * (1.0 - r)
            dzi = (dh * mult * u) * ig * (1.0 - ig)
            dzr_ref[rows, cs] = dzr.astype(BF16)
            dzi_ref[rows, cs] = dzi.astype(BF16)
            dbuf[rows, cs] = dh * mult * ig
            vacc[A_LAM, :, cs] += _rowsum8(dlog_a * r)
            vacc[A_BA, :, cs] += _rowsum8(dzr)
            vacc[A_BX, :, cs] += _rowsum8(dzi)
        _chunks(TB, lru_bwd)

        ub, dzrb, dzib = u_ref[...].astype(BF16), dzr_ref[...], dzi_ref[...]
        du_g = []
        for h in range(NHEAD):
            cs = slice(h * HD, (h + 1) * HD)
            gwa_ref[h] += lax.dot_general(ub[:, cs], dzrb[:, cs], (((0,), (0,)), ((), ())),
                                          preferred_element_type=F32)
            gwx_ref[h] += lax.dot_general(ub[:, cs], dzib[:, cs], (((0,), (0,)), ((), ())),
                                          preferred_element_type=F32)
            du_g.append(lax.dot_general(dzrb[:, cs], wa_ref[h], (((1,), (1,)), ((), ())),
                                        preferred_element_type=F32)
                        + lax.dot_general(dzib[:, cs], wx_ref[h], (((1,), (1,)), ((), ())),
                                          preferred_element_type=F32))
        du = dbuf[0:TB, :] + jnp.concatenate(du_g, axis=1)
        dbuf[0:TB, :] = du

        Dd = dbuf[...]
        w = cw_ref[...]
        xv = xr_ref[...]
        dx = du * w[CONV_K - 1:CONV_K, :]
        vacc[A_CB] += _rowsum8(du)
        vacc[A_CW + CONV_K - 1] += _rowsum8(xv * du)
        for k in range(CONV_K - 1):
            ahead = pltpu.roll(Dd, TE - (CONV_K - 1 - k), axis=0)[0:TB, :]
            dx = dx + ahead * w[k:k + 1, :]
            vacc[A_CW + k] += _rowsum8(xv * ahead)
        dxr_ref[...] = dx.astype(BF16)
        dbuf[TB:, :] = du[0:CONV_HALO, :]

        @pl.when(i == nb - 1)
        def _():
            gba_ref[...] = _sum8(vacc[A_BA])
            gbx_ref[...] = _sum8(vacc[A_BX])
            glam_ref[...] = _sum8(vacc[A_LAM]) * (LRU_C * _sigmoid(-lam_ref[...]))
            gcb_ref[...] = _sum8(vacc[A_CB])
            for k in range(CONV_K):
                gcw_ref[k:k + 1, :] = _sum8(vacc[A_CW + k])

    rowb = pl.BlockSpec((TB, D), lambda i: (blk(i), 0))
    halo = pl.BlockSpec((SUB, D), lambda i: (jnp.maximum(blk(i) * (TB // SUB) - 1, 0), 0))
    wspec = _full((NHEAD, HD, HD))
    wlay = _layer(l, (NHEAD, HD, HD))
    vec1 = _full((1, D))
    return _pcall(
        body, name="rnn_bwd", grid=(nb,),
        in_specs=[rowb, rowb, halo] + [rowb] * 5 + [rowb, pl.BlockSpec((TB, D), lambda i: (blk(i), 1)),
                                                    _layer(l, (CONV_K, D)), wlay, wlay, _vec(l)],
        out_specs=[rowb, rowb, _full((CONV_K, D)), vec1, wspec, vec1, wspec, vec1, vec1],
        out_shape=[jax.ShapeDtypeStruct((S, D), BF16), jax.ShapeDtypeStruct((S, D), BF16),
                   jax.ShapeDtypeStruct((CONV_K, D), F32), jax.ShapeDtypeStruct((1, D), F32),
                   jax.ShapeDtypeStruct((NHEAD, HD, HD), F32), jax.ShapeDtypeStruct((1, D), F32),
                   jax.ShapeDtypeStruct((NHEAD, HD, HD), F32), jax.ShapeDtypeStruct((1, D), F32),
                   jax.ShapeDtypeStruct((1, D), F32)],
        scratch_shapes=[pltpu.VMEM((TE, D), F32), pltpu.VMEM((TE, D), F32), pltpu.VMEM((TE, D), F32),
                        pltpu.VMEM((D // LANE, TB, LANE), F32), pltpu.VMEM((D // LANE, TB, LANE), F32),
                        pltpu.VMEM((TB, D), F32), pltpu.VMEM((TB, D), F32), pltpu.VMEM((TB, D), BF16),
                        pltpu.VMEM((TB, D), BF16), pltpu.VMEM((SUB, D), F32),
                        pltpu.VMEM((A_CW + CONV_K, SUB, D), F32)],
        args=(dyr, hs, hs, *fwd, proj, proj, P['conv_w'], P['gate_a_w'], P['gate_x_w'], P['lru_lambda']),
        rider=rider)


def _pool_bwd(dyp, proj, P, l, rider=None):
    S = proj.shape[0]
    TB = min(512, S)
    nb = S // TB
    TE = TB + POOL_HALO

    def blk(i):
        return nb - 1 - i

    def body(dy_ref, xp_ref, xprev_ref, g_ref, pw_ref, pb_ref, ps_ref, dxp_ref, dg_ref, gpw_ref, gpb_ref,
             gps_ref, ebuf, qbuf, vacc):
        i = pl.program_id(0)
        first = blk(i) == 0

        @pl.when(i == 0)
        def _():
            qbuf[TB:, :] = jnp.zeros((POOL_HALO, D), F32)
            vacc[...] = jnp.zeros_like(vacc)
            gpw_ref[...] = jnp.zeros_like(gpw_ref)

        ebuf[0:POOL_HALO, :] = jnp.where(first, 0.0, xprev_ref[...])
        ebuf[POOL_HALO:, :] = xp_ref[...]
        pooled, inv = _pooled(ebuf, blk(i) * TB, TB)
        pooled = [p.astype(BF16) for p in pooled]
        yp = jnp.concatenate([jnp.dot(pooled[g], pw_ref[g], preferred_element_type=F32)
                              for g in range(NGRP)], axis=1) + pb_ref[...]
        silu, dsilu = _silu_parts(g_ref[...])
        dy = dy_ref[...]
        ps = ps_ref[...]
        dyp_v = dy * ps * silu
        vacc[0] += _rowsum8(dy * yp * silu)
        vacc[1] += _rowsum8(dyp_v)
        dg_ref[...] = (dy * yp * ps * dsilu).astype(BF16)
        dypb = dyp_v.astype(BF16)
        for g in range(NGRP):
            cs = slice(g * GD, (g + 1) * GD)
            gpw_ref[g] += lax.dot_general(pooled[g], dypb[:, cs], (((0,), (0,)), ((), ())),
                                          preferred_element_type=F32)
            dpool = lax.dot_general(dypb[:, cs], pw_ref[g], (((1,), (1,)), ((), ())),
                                    preferred_element_type=F32)
            qbuf[0:TB, cs] = dpool * inv[g]
            L = qbuf[:, cs]
            for lev in range(g + 1):
                L = L + pltpu.roll(L, TE - (1 << lev), axis=0)
            dxp_ref[:, cs] = (L[0:TB, :] - dpool).astype(BF16)
        qbuf[TB:, :] = qbuf[0:POOL_HALO, :]

        @pl.when(i == nb - 1)
        def _():
            gps_ref[...] = _sum8(vacc[0])
            gpb_ref[...] = _sum8(vacc[1])

    rowb = pl.BlockSpec((TB, D), lambda i: (blk(i), 0))
    return _pcall(
        body, name="pool_bwd", grid=(nb,),
        in_specs=[rowb, pl.BlockSpec((TB, D), lambda i: (blk(i), 2)),
                  pl.BlockSpec((POOL_HALO, D), lambda i: (jnp.maximum(blk(i) * (TB // POOL_HALO) - 1, 0), 2)),
                  pl.BlockSpec((TB, D), lambda i: (blk(i), 3)), _layer(l, (NGRP, GD, GD)), _vec(l), _vec(l)],
        out_specs=[rowb, rowb, _full((NGRP, GD, GD)), _full((1, D)), _full((1, D))],
        out_shape=[jax.ShapeDtypeStruct((S, D), BF16), jax.ShapeDtypeStruct((S, D), BF16),
                   jax.ShapeDtypeStruct((NGRP, GD, GD), F32), jax.ShapeDtypeStruct((1, D), F32),
                   jax.ShapeDtypeStruct((1, D), F32)],
        scratch_shapes=[pltpu.VMEM((TE, D), F32), pltpu.VMEM((TE, D), F32), pltpu.VMEM((2, SUB, D), F32)],
        args=(dyp, proj, proj, proj, P['pool_w'], P['pool_b'], P['pool_scale']), rider=rider)


def _in_bwd(dq, w_in_l, x, dxo, P, mod, l, rider=None):
    S = x.shape[0]
    TM = min(512, S)
    NB = w_in_l.shape[2]
    nsteps = S // TM
    per_q = D // NB

    def body(d0, d1, d2, d3, w_ref, x_ref, dxo_ref, g_ref, sc_ref, dx_ref, dsh_ref, dsc_ref, dg_ref, vacc):
        i = pl.program_id(0)

        @pl.when(i == 0)
        def _():
            vacc[...] = jnp.zeros_like(vacc)
        dref = (d0, d1, d2, d3)
        dh = jnp.zeros((TM, D), F32)
        for j in range(NDEV):
            c0 = (j % per_q) * NB
            dh = dh + lax.dot_general(dref[j // per_q][:, c0:c0 + NB], w_ref[j], (((1,), (1,)), ((), ())),
                                      preferred_element_type=F32)
        xv = x_ref[...]
        rstd = lax.rsqrt(jnp.mean(xv * xv, axis=-1, keepdims=True) + NORM_EPS)
        xn = xv * rstd
        g, sc = g_ref[...], 1.0 + sc_ref[...]
        vacc[0] += _rowsum8(dh)
        vacc[1] += _rowsum8(dh * (xn * g))
        vacc[2] += _rowsum8(dh * sc * xn)
        dxn = dh * sc * g
        dx_ref[...] = dxo_ref[...] + rstd * (dxn - xn * jnp.mean(dxn * xn, axis=-1, keepdims=True))

        @pl.when(i == nsteps - 1)
        def _():
            dsh_ref[...] = _sum8(vacc[0])
            dsc_ref[...] = _sum8(vacc[1])
            dg_ref[...] = _sum8(vacc[2])

    row = pl.BlockSpec((TM, D), lambda i: (i, 0))
    return _pcall(
        body, name="in_bwd", grid=(nsteps,),
        in_specs=[row, row, row, row, _full((NDEV, D, NB)), row, row, _vec(l), _vec(l, 1)],
        out_specs=[row, _full((1, D)), _full((1, D)), _full((1, D))],
        out_shape=[jax.ShapeDtypeStruct((S, D), F32)] + [jax.ShapeDtypeStruct((1, D), F32)] * 3,
        scratch_shapes=[pltpu.VMEM((3, SUB, D), F32)],
        args=(*dq, w_in_l, x, dxo, P['pre_norm_g'], mod), rider=rider)


def _grad_w_in(h, dq, NB, rider=None):
    S, RH = h.shape
    TK = min(1024, S)
    nk = S // TK

    def body(h_ref, d0, d1, d2, d3, o_ref, acc):
        k = pl.program_id(0)

        @pl.when(k == 0)
        def _():
            acc[...] = jnp.zeros_like(acc)
        hv = h_ref[...]
        for q, d_ref in enumerate((d0, d1, d2, d3)):
            acc[:, q * D:(q + 1) * D] += lax.dot_general(hv, d_ref[...], (((0,), (0,)), ((), ())),
                                                         preferred_element_type=F32)

        @pl.when(k == nk - 1)
        def _():
            for j in range(NDEV):
                o_ref[j] = acc[:, j * NB:(j + 1) * NB].astype(BF16)

    row = pl.BlockSpec((TK, D), lambda k: (k, 0))
    return _pcall(
        body, name="grad_w_in", grid=(nk,),
        in_specs=[pl.BlockSpec((TK, RH), lambda k: (k, 0)), row, row, row, row],
        out_specs=[_full((NDEV, RH, NB))],
        out_shape=[jax.ShapeDtypeStruct((NDEV, RH, NB), BF16)],
        scratch_shapes=[pltpu.VMEM((RH, NQ * D), F32)],
        args=(h, *dq), rider=rider)


def _adamw_math(g, w, m, v):
    m2 = ADAM_B1 * m + (1.0 - ADAM_B1) * g
    v2 = ADAM_B2 * v + (1.0 - ADAM_B2) * (g * g)
    m_hat = m2 / (1.0 - ADAM_B1 ** ADAM_STEP)
    v_hat = v2 / (1.0 - ADAM_B2 ** ADAM_STEP)
    delta = -ADAM_LR * (m_hat / (jnp.sqrt(v_hat) + ADAM_EPS) + ADAM_WD * w)
    return delta, m2, v2


def _adamw(name, gs, w, m, v, TR, row0=0, into=None):
    L = len(gs)
    n, Rp, C = gs[0].shape
    R = w.shape[1]
    b0 = row0 // TR

    def body(*refs):
        g_refs = refs[:L]
        w_ref, m_ref, v_ref = refs[L:L + 3]
        go_ref, do_ref, mo_ref, vo_ref = refs[-4:]
        lay = pl.program_id(0)
        for li in range(L):
            @pl.when(lay == li)
            def _(li=li):
                g = g_refs[li][0].astype(F32)
                for s in range(1, n):
                    g = g + g_refs[li][s].astype(F32)
                delta, m2, v2 = _adamw_math(g, w_ref[...], m_ref[...], v_ref[...])
                go_ref[...] = g
                do_ref[...] = delta
                mo_ref[...] = m2
                vo_ref[...] = v2

    lrc = pl.BlockSpec((None, TR, C), lambda lay, r: (lay, r + b0, 0))
    g_specs = [pl.BlockSpec((n, TR, C), lambda lay, r, li=li: (0, jnp.where(lay == li, r, 0), 0))
               for li in range(L)]
    in_specs, args, aliases = g_specs + [lrc, lrc, lrc], [*gs, w, m, v], None
    if into is not None:
        aliases = {len(args) + k: k for k in range(4)}
        in_specs = in_specs + [pl.BlockSpec(memory_space=pl.ANY)] * 4
        args = args + list(into)
    return _pcall(
        body, name=name, grid=(L, Rp // TR), in_specs=in_specs, out_specs=[lrc] * 4,
        out_shape=[jax.ShapeDtypeStruct((L, R, C), F32)] * 4, args=args, aliases=aliases)


def _ada_adamw(c_all_t, dm, w, m, v, rider=None):
    L, _, nc = w.shape

    def body(c_ref, dm_ref, w_ref, m_ref, v_ref, go_ref, do_ref, mo_ref, vo_ref):
        cv = c_ref[...]
        ca = cv * jax.nn.sigmoid(cv)
        dmv = dm_ref[...]
        g = ca[:, 0:1] * dmv[0:1, :]
        for b in range(1, NDEV):
            g = g + ca[:, b:b + 1] * dmv[b:b + 1, :]
        delta, m2, v2 = _adamw_math(g, w_ref[...], m_ref[...], v_ref[...])
        go_ref[...] = g
        do_ref[...] = delta
        mo_ref[...] = m2
        vo_ref[...] = v2

    big = pl.BlockSpec((None, D, nc), lambda lay: (lay, 0, 0))
    return _pcall(
        body, name="ada_adamw", grid=(L,),
        in_specs=[_full((D, NDEV)), pl.BlockSpec((None, NDEV, nc), lambda lay: (lay, 0, 0)), big, big, big],
        out_specs=[big] * 4, out_shape=[jax.ShapeDtypeStruct((L, D, nc), F32)] * 4,
        args=(c_all_t, dm, w, m, v), rider=rider)


def _sum_slots(recv):
    n, R, C = recv.shape

    def body(r_ref, o_ref):
        acc = r_ref[0].astype(F32)
        for s in range(1, n):
            acc = acc + r_ref[s].astype(F32)
        o_ref[...] = acc

    return pl.pallas_call(body, name="sum_slots", out_shape=jax.ShapeDtypeStruct((R, C), F32),
                          compiler_params=_cparams())(recv)


def _pad_rows(a, rows):
    return jnp.pad(a, ((0, rows - a.shape[0]), (0, 0)))


def _pack_sharded_block(pool_w, pool_b, conv_w):
    return jnp.concatenate([pool_w.reshape(-1, PACK_C), _pad_rows(pool_b.reshape(-1, PACK_C), SUB),
                            _pad_rows(conv_w.reshape(-1, PACK_C), SUB)], axis=0)


def _unpack_sharded_block(p):
    n_pw = DEPTH * NGRP * (GD // NDEV)
    pool_w = p[:n_pw].reshape(DEPTH, NGRP, GD // NDEV, GD)
    pool_b = p[n_pw].reshape(DEPTH, NGRP, GD // NDEV)
    conv_w = p[n_pw + SUB:n_pw + SUB + DEPTH * CONV_K * (D // NDEV) // PACK_C].reshape(DEPTH, CONV_K, D // NDEV)
    return pool_w, pool_b, conv_w


def _blocks_of_full(pool_w, pool_b, conv_w):
    pw = pool_w.reshape(DEPTH, NGRP, NDEV, GD // NDEV, GD).transpose(2, 0, 1, 3, 4).reshape(NDEV, -1, PACK_C)
    pb = pool_b.reshape(DEPTH, NGRP, NDEV, GD // NDEV).transpose(2, 0, 1, 3).reshape(NDEV, -1, PACK_C)
    cw = conv_w.reshape(DEPTH, CONV_K, NDEV, D // NDEV).transpose(2, 0, 1, 3).reshape(NDEV, -1, PACK_C)
    pad = lambda a: jnp.pad(a, ((0, 0), (0, SUB - a.shape[1]), (0, 0)))
    return jnp.concatenate([pw, pad(pb), pad(cw)], axis=1)


def _full_of_blocks(p):
    n_pw = DEPTH * NGRP * (GD // NDEV)
    pool_w = p[:, :n_pw].reshape(NDEV, DEPTH, NGRP, GD // NDEV, GD).transpose(1, 2, 0, 3, 4)
    pool_b = p[:, n_pw].reshape(NDEV, DEPTH, NGRP, GD // NDEV).transpose(1, 2, 0, 3)
    n_cw = DEPTH * CONV_K * (D // NDEV) // PACK_C
    conv_w = p[:, n_pw + SUB:n_pw + SUB + n_cw].reshape(NDEV, DEPTH, CONV_K, D // NDEV).transpose(1, 2, 0, 3)
    return (pool_w.reshape(DEPTH, NGRP, GD, GD), pool_b.reshape(DEPTH, NGRP, GD),
            conv_w.reshape(DEPTH, CONV_K, D))


def _pack_replicated(t, keys, rows):
    p = jnp.concatenate([t[k].reshape(-1, PACK_C) for k in keys], axis=0)
    return _pad_rows(p, rows)


def _unpack_replicated(p, like, keys):
    out, r0 = {}, 0
    for k in keys:
        rows = like[k].size // PACK_C
        out[k] = p[r0:r0 + rows].reshape(like[k].shape)
        r0 += rows
    return out


def kernel(x, c, ada_w, ada_b, pre_norm_g, w_in, conv_w, conv_b, gate_a_w, gate_a_b, gate_x_w, gate_x_b, lru_lambda, pool_w, pool_b, pool_scale, w_out, post_norm_g, loss_target, m_ada_w, m_ada_b, m_pre_norm_g, m_w_in, m_conv_w, m_conv_b, m_gate_a_w, m_gate_a_b, m_gate_x_w, m_gate_x_b, m_lru_lambda, m_pool_w, m_pool_b, m_pool_scale, m_w_out, m_post_norm_g, v_ada_w, v_ada_b, v_pre_norm_g, v_w_in, v_conv_w, v_conv_b, v_gate_a_w, v_gate_a_b, v_gate_x_w, v_gate_x_b, v_lru_lambda, v_pool_w, v_pool_b, v_pool_scale, v_w_out, v_post_norm_g):
    W = dict(ada_w=ada_w, ada_b=ada_b, pre_norm_g=pre_norm_g, w_in=w_in, conv_w=conv_w, conv_b=conv_b,
             gate_a_w=gate_a_w, gate_a_b=gate_a_b, gate_x_w=gate_x_w, gate_x_b=gate_x_b, lru_lambda=lru_lambda,
             pool_w=pool_w, pool_b=pool_b, pool_scale=pool_scale, w_out=w_out, post_norm_g=post_norm_g)
    M = dict(ada_w=m_ada_w, ada_b=m_ada_b, pre_norm_g=m_pre_norm_g, w_in=m_w_in, conv_w=m_conv_w,
             conv_b=m_conv_b, gate_a_w=m_gate_a_w, gate_a_b=m_gate_a_b, gate_x_w=m_gate_x_w,
             gate_x_b=m_gate_x_b, lru_lambda=m_lru_lambda, pool_w=m_pool_w, pool_b=m_pool_b,
             pool_scale=m_pool_scale, w_out=m_w_out, post_norm_g=m_post_norm_g)
    V = dict(ada_w=v_ada_w, ada_b=v_ada_b, pre_norm_g=v_pre_norm_g, w_in=v_w_in, conv_w=v_conv_w,
             conv_b=v_conv_b, gate_a_w=v_gate_a_w, gate_a_b=v_gate_a_b, gate_x_w=v_gate_x_w,
             gate_x_b=v_gate_x_b, lru_lambda=v_lru_lambda, pool_w=v_pool_w, pool_b=v_pool_b,
             pool_scale=v_pool_scale, w_out=v_w_out, post_norm_g=v_post_norm_g)
    S = x.shape[1]
    me = 4 * lax.axis_index("x") + 2 * lax.axis_index("y") + lax.axis_index("c")
    xs = x.reshape(S, D)
    tgt = loss_target.reshape(S, D)
    nc = ada_w.shape[2]
    NB = w_in.shape[2]
    w_in_b, w_out_b = w_in.astype(BF16), w_out.astype(BF16)
    rows = lambda a: a.reshape(DEPTH, 1, D)
    P = dict(pre_norm_g=rows(pre_norm_g), conv_b=rows(conv_b), gate_a_b=rows(gate_a_b), gate_x_b=rows(gate_x_b),
             lru_lambda=rows(lru_lambda), pool_scale=rows(pool_scale), post_norm_g=rows(post_norm_g),
             gate_a_w=gate_a_w.astype(BF16), gate_x_w=gate_x_w.astype(BF16))

    c_slots, w_in0 = _exchange("gather_c_w_in0", _AllGather2([jnp.broadcast_to(c, (SUB, D)), w_in_b[0]]))
    c_all = c_slots[:, 0, :]
    (mod_slots,) = _exchange("gather_mod", _Direct(ag=[_mod_cols(c_all, ada_w)]))
    mod = lax.dynamic_index_in_dim(mod_slots, me, axis=1, keepdims=False)
    mod = (mod.reshape(NDEV, DEPTH, nc).transpose(1, 0, 2).reshape(DEPTH, 3 * D) + ada_b).reshape(DEPTH, 3, 1, D)

    w_in_all, w_out_all = [w_in0, None], [None, None]
    saved = []
    xl = xs
    flat = lambda w_slots: w_slots.reshape(2 * D, D)
    for l in range(DEPTH):
        rider = _AllGather2([_pack_sharded_block(pool_w, pool_b, conv_w), w_out_b[0]]) if l == 0 else None
        (h_a, h_b, proj), got = _pre_proj(xl, P, mod, w_in_all[l], l, rider=rider)
        if l == 0:
            pool_w_f, pool_b_f, conv_w_f = _full_of_blocks(got[0])
            P.update(conv_w=conv_w_f, pool_w=pool_w_f.astype(BF16), pool_b=rows(pool_b_f))
            w_out_all[0] = flat(got[1])
        rider = _AllGather2([w_in_b[1]]) if l == 0 else None
        (hs, yr, *fwd), got = _rnn_fwd(proj, P, l, rider=rider)
        if l == 0:
            w_in_all[1] = got[0]
        (yp,), _ = _pool_fwd(proj, P, l)
        if l == DEPTH - 1:
            (y, x_next, loss_acc), _ = _out_post(yr, yp, w_out_all[l], xl, mod, P, l, tgt)
        else:
            (y, x_next), got = _out_post(yr, yp, w_out_all[l], xl, mod, P, l, rider=_AllGather2([w_out_b[1]]))
            w_out_all[1] = flat(got[0])
        saved.append((xl, h_a, h_b, proj, hs, fwd, yr, yp, y))
        xl = x_next

    dxo = xl
    G = {k: [None] * DEPTH for k in WEIGHTS}
    dmod = [None] * DEPTH
    recv_in, recv_out = [[None, None] for _ in range(DEPTH)], [None] * DEPTH
    full = dict(conv_w=(CONV_K, D), pool_w=(NGRP, GD, GD), pool_b=(NGRP, GD))
    stack = lambda k: jnp.stack([g.reshape(full.get(k, W[k].shape[1:])) for g in G[k]])
    gw_bot_prev = None
    for l in reversed(range(DEPTH)):
        xin, h_a, h_b, proj, hs, fwd, yr, yp, y = saved[l]
        rider = _AllGather2([loss_acc]) if gw_bot_prev is None else _Direct(a2a=[gw_bot_prev])
        (dyr, dyp, gw_out, dgate, G['post_norm_g'][l]), got = _out_bwd(dxo, y, yr, yp, w_out_all[l], mod, P, l,
                                                                       rider=rider)
        if gw_bot_prev is None:
            loss = (0.5 / D) * jnp.sum(got[0])
        else:
            recv_in[l + 1][1] = got[0]
        ((dxr, dgr, G['conv_w'][l], G['conv_b'][l], G['gate_a_w'][l], G['gate_a_b'][l], G['gate_x_w'][l],
          G['gate_x_b'][l], G['lru_lambda'][l]), (recv_out[l],)) = _rnn_bwd(
            dyr, hs, fwd, proj, P, l,
            rider=_Direct(a2a=[gw_out.reshape(NDEV, 2 * D // NDEV, D)]))
        (dxp, dgp, G['pool_w'][l], G['pool_b'][l], G['pool_scale'][l]), _ = _pool_bwd(dyp, proj, P, l)
        dq = (dxr, dgr, dxp, dgp)
        if l > 0:
            (gw_top,), _ = _grad_w_in(h_a, dq, NB)
            (dxo, dshift, dscale, G['pre_norm_g'][l]), (recv_in[l][0],) = _in_bwd(
                dq, w_in_all[l], xin, dxo, P, mod, l, rider=_Direct(a2a=[gw_top]))
            (gw_bot_prev,), _ = _grad_w_in(h_b, dq, NB)
        else:
            Ge = {k: stack(k) for k in REP_EARLY + ['pool_w', 'pool_b', 'conv_w']}
            early = jnp.concatenate([_blocks_of_full(Ge['pool_w'], Ge['pool_b'], Ge['conv_w']),
                                     _pack_replicated(Ge, REP_EARLY, NDEV * PACK_ROWS).reshape(NDEV, PACK_ROWS, PACK_C)],
                                    axis=1).astype(BF16)
            (gw_top,), (early_recv,) = _grad_w_in(h_a, dq, NB, rider=_Direct(a2a=[early]))
            early_sum = _sum_slots(early_recv)
            (gw_bot,), (recv_in[l][0],) = _grad_w_in(h_b, dq, NB, rider=_Direct(a2a=[gw_top]))
            (dxo, dshift, dscale, G['pre_norm_g'][l]), (recv_in[l][1], early_all) = _in_bwd(
                dq, w_in_all[l], xin, dxo, P, mod, l,
                rider=_Both(_Direct(a2a=[gw_bot]), _AllGather2([early_sum[PACK_ROWS:]])))
        dmod[l] = jnp.concatenate([dshift, dscale, dgate], axis=1)
    grad_x = dxo.reshape(x.shape)

    Gl = dict(ada_b=jnp.concatenate(dmod, axis=0), pre_norm_g=stack('pre_norm_g'))
    (late_slots,) = _exchange("gather_late", _AllGather2([_pack_replicated(Gl, REP_LATE, LATE_PACK_ROWS)]))
    late_sum = _sum_slots(late_slots)
    dmod_all = late_slots[:, :DEPTH * 3 * D // PACK_C]

    out = {}
    first, _ = _adamw("adamw_w_in_a", [recv_in[l][0] for l in range(DEPTH)], w_in, M['w_in'], V['w_in'], LANE)
    out['w_in'], _ = _adamw("adamw_w_in_b", [recv_in[l][1] for l in range(DEPTH)], w_in, M['w_in'], V['w_in'], LANE,
                            row0=GW_SPLIT, into=first)
    out['w_out'], _ = _adamw("adamw_w_out", recv_out, w_out, M['w_out'], V['w_out'], 256)
    dm = lax.dynamic_slice_in_dim(dmod_all.reshape(NDEV, DEPTH, 3 * D), me * nc, nc, axis=2)
    out['ada_w'], _ = _ada_adamw(c_all.T, dm.transpose(1, 0, 2), ada_w, M['ada_w'], V['ada_w'])
    g_small = jnp.concatenate([early_sum[:PACK_ROWS], early_all.reshape(NDEV * PACK_ROWS, PACK_C), late_sum],
                              axis=0)

    def packs(T):
        return jnp.concatenate([_pack_sharded_block(T['pool_w'], T['pool_b'], T['conv_w']),
                                _pack_replicated(T, REP_EARLY, NDEV * PACK_ROWS),
                                _pack_replicated(T, REP_LATE, LATE_PACK_ROWS)], axis=0)[None]
    res_small, _ = _adamw("adamw_small", [g_small[None]], packs(W), packs(M), packs(V), g_small.shape[0] // 2)
    n_early = (1 + NDEV) * PACK_ROWS
    for idx in range(4):
        p = res_small[idx][0]
        pw_, pb_, cw_ = _unpack_sharded_block(p[:PACK_ROWS])
        rep = _unpack_replicated(p[PACK_ROWS:n_early], W, REP_EARLY)
        rep.update(_unpack_replicated(p[n_early:], W, REP_LATE))
        rep.update(pool_w=pw_, pool_b=pb_, conv_w=cw_)
        for k, a in rep.items():
            out.setdefault(k, [None] * 4)[idx] = a
    for k in ('w_in', 'w_out', 'ada_w'):
        out[k] = [a.reshape(W[k].shape) for a in out[k]]

    return (loss, grad_x, *[out[k][0] for k in WEIGHTS], *[out[k][1] for k in WEIGHTS],
            *[out[k][2] for k in WEIGHTS], *[out[k][3] for k in WEIGHTS])
```

```python
import functools

import jax
import jax.numpy as jnp
from jax import lax
from jax.experimental import pallas as pl
from jax.experimental.pallas import tpu as pltpu

F32, BF16 = jnp.float32, jnp.bfloat16
MESH = pl.DeviceIdType.MESH
HIGHEST = lax.Precision.HIGHEST

NDEV = 8
DEPTH = 2
D = 1024
NHEAD, HD = 8, 128
NGRP, GD = 4, 256
WINS = (2, 4, 8, 16)
CONV_K = 4
CONV_HALO = 8
POOL_HALO = 16
LRU_C = 8.0
NORM_EPS = 1e-6
ADAM_LR, ADAM_B1, ADAM_B2, ADAM_EPS, ADAM_WD, ADAM_STEP = 0.001, 0.9, 0.999, 1e-08, 0.01, 10
VMEM_LIMIT = 56 * 1024 * 1024
NQ = 4
SUB = 8
LANE = 128
PACK_C = 256
PACK_ROWS = 272

WEIGHTS = ['ada_w', 'ada_b', 'pre_norm_g', 'w_in', 'conv_w', 'conv_b', 'gate_a_w', 'gate_a_b', 'gate_x_w',
           'gate_x_b', 'lru_lambda', 'pool_w', 'pool_b', 'pool_scale', 'w_out', 'post_norm_g']
REP_EARLY = ['conv_b', 'gate_a_w', 'gate_a_b', 'gate_x_w', 'gate_x_b', 'lru_lambda', 'pool_scale', 'post_norm_g']
REP_LATE = ['ada_b', 'pre_norm_g']
GW_SPLIT = 512
LATE_PACK_ROWS = 32


def _cparams(*sem):
    return pltpu.CompilerParams(dimension_semantics=sem, vmem_limit_bytes=VMEM_LIMIT)


def _vec(l, k=None):
    if k is None:
        return pl.BlockSpec((None, 1, D), lambda *_: (l, 0, 0))
    return pl.BlockSpec((None, None, 1, D), lambda *_: (l, k, 0, 0))


def _layer(l, shape):
    nd = len(shape)
    return pl.BlockSpec((None,) + tuple(shape), lambda *_: (l,) + (0,) * nd)


def _full(shape):
    nd = len(shape)
    return pl.BlockSpec(shape, lambda *_: (0,) * nd)


def _rowsum8(z):
    return z.reshape(z.shape[0] // SUB, SUB, z.shape[1]).sum(axis=0)


def _sum8(acc):
    return jnp.sum(acc, axis=0, keepdims=True)


def _sigmoid(z):
    return 0.5 * jnp.tanh(0.5 * z) + 0.5


def _silu_parts(g):
    sg = _sigmoid(g)
    return g * sg, sg * (1.0 + g * (1.0 - sg))


def _one_minus_sq(a, log_a):
    z = 2.0 * log_a
    p = 1.0 / 24.0
    for k in (6.0, 2.0, 1.0):
        p = p * z + 1.0 / k
    return jnp.where(z > -0.03, -(p * z), 1.0 - a * a)


def _place():
    x, y, c = lax.axis_index("x"), lax.axis_index("y"), lax.axis_index("c")
    return x, y, c, 4 * x + 2 * y + c


class _Direct:
    def __init__(self, a2a=(), ag=()):
        self.arrays = list(a2a) + list(ag)
        self.n_a, self.n = len(a2a), len(self.arrays)
        self.out_shape = ([jax.ShapeDtypeStruct(a.shape, a.dtype) for a in a2a]
                          + [jax.ShapeDtypeStruct((NDEV,) + a.shape, a.dtype) for a in ag])
        self.scratch = [pltpu.SemaphoreType.DMA((self.n, NDEV - 1)), pltpu.SemaphoreType.DMA((self.n, NDEV - 1)),
                        pltpu.SemaphoreType.DMA((self.n,))]

    def _copies(self, ins, outs, sems):
        send_sems, recv_sems, local_sems = sems
        x, y, c, me = _place()
        local, remote = [], []
        for t in range(self.n):
            src = ins[t].at[me] if t < self.n_a else ins[t]
            local.append(pltpu.make_async_copy(src, outs[t].at[me], local_sems.at[t]))
        for r in range(1, NDEV):
            px = 1 - x if r & 4 else x
            py = 1 - y if r & 2 else y
            pc = 1 - c if r & 1 else c
            for t in range(self.n):
                src = ins[t].at[4 * px + 2 * py + pc] if t < self.n_a else ins[t]
                remote.append(pltpu.make_async_remote_copy(
                    src_ref=src, dst_ref=outs[t].at[me], send_sem=send_sems.at[t, r - 1],
                    recv_sem=recv_sems.at[t, r - 1], device_id=(px, py, pc), device_id_type=MESH))
        return local, remote

    def start(self, ins, outs, sems):
        local, remote = self._copies(ins, outs, sems)
        for cp in local + remote:
            cp.start()

    def finish(self, ins, outs, sems):
        local, remote = self._copies(ins, outs, sems)
        for cp in remote + local:
            cp.wait()


class _AllGather2:
    def __init__(self, arrays):
        self.arrays = list(arrays)
        self.n = len(self.arrays)
        self.out_shape = [jax.ShapeDtypeStruct((NDEV,) + a.shape, a.dtype) for a in self.arrays]
        self.scratch = [pltpu.SemaphoreType.DMA((self.n, NDEV - 1)), pltpu.SemaphoreType.DMA((self.n, NDEV - 1)),
                        pltpu.SemaphoreType.DMA((self.n,))]

    def _copy(self, t, k, src, dst, to, sems):
        return pltpu.make_async_remote_copy(src_ref=src, dst_ref=dst, send_sem=sems[0].at[t, k],
                                            recv_sem=sems[1].at[t, k], device_id=to, device_id_type=MESH)

    def start(self, ins, outs, sems):
        x, y, c, me = _place()
        for t in range(self.n):
            pltpu.make_async_copy(ins[t], outs[t].at[me], sems[2].at[t]).start()
            self._copy(t, 0, ins[t], outs[t].at[me], (x, y, 1 - c), sems).start()
            self._copy(t, 1, ins[t], outs[t].at[me], (1 - x, y, c), sems).start()
            self._copy(t, 2, ins[t], outs[t].at[me], (x, 1 - y, c), sems).start()

    def finish(self, ins, outs, sems):
        x, y, c, me = _place()
        sib = (x, y, 1 - c)
        slot_x, slot_y, slot_d = 4 * (1 - x) + 2 * y + c, 4 * x + 2 * (1 - y) + c, 4 * (1 - x) + 2 * (1 - y) + c
        slot_on = c * slot_x + (1 - c) * slot_y
        to_on = (c * x + (1 - c) * (1 - x), c * (1 - y) + (1 - c) * y, c)
        for t in range(self.n):
            self._copy(t, 1, ins[t], outs[t].at[slot_x], sib, sems).wait_recv()
            self._copy(t, 2, ins[t], outs[t].at[slot_y], sib, sems).wait_recv()
            self._copy(t, 3, outs[t].at[slot_on], outs[t].at[slot_on], to_on, sems).start()
            self._copy(t, 4, outs[t].at[slot_x], outs[t].at[slot_x], sib, sems).start()
            self._copy(t, 5, outs[t].at[slot_y], outs[t].at[slot_y], sib, sems).start()
        for t in range(self.n):
            self._copy(t, 3, ins[t], outs[t].at[slot_d], sib, sems).wait_recv()
            self._copy(t, 6, outs[t].at[slot_d], outs[t].at[slot_d], sib, sems).start()
        for t in range(self.n):
            for k in (0, 4, 5, 6):
                self._copy(t, k, ins[t], outs[t].at[me], sib, sems).wait_recv()
        for t in range(self.n):
            for k in range(NDEV - 1):
                self._copy(t, k, ins[t], outs[t].at[me], sib, sems).wait_send()
            pltpu.make_async_copy(ins[t], outs[t].at[me], sems[2].at[t]).wait()


class _Both:
    def __init__(self, *riders):
        self.riders = riders
        self.arrays = [a for r in riders for a in r.arrays]
        self.n = len(self.arrays)
        self.out_shape = [o for r in riders for o in r.out_shape]
        self.scratch = [s for r in riders for s in r.scratch]

    def _parts(self, ins, outs, sems):
        p, q = 0, 0
        for r in self.riders:
            yield r, ins[p:p + r.n], outs[p:p + r.n], sems[q:q + len(r.scratch)]
            p, q = p + r.n, q + len(r.scratch)

    def start(self, ins, outs, sems):
        for r, i, o, s in self._parts(ins, outs, sems):
            r.start(i, o, s)

    def finish(self, ins, outs, sems):
        for r, i, o, s in self._parts(ins, outs, sems):
            r.finish(i, o, s)


def _exchange(name, rider):
    n = rider.n

    def body(*refs):
        rider.start(refs[:n], refs[n:2 * n], refs[2 * n:])
        rider.finish(refs[:n], refs[n:2 * n], refs[2 * n:])

    any_spec = pl.BlockSpec(memory_space=pl.ANY)
    return list(pl.pallas_call(body, name=name, out_shape=rider.out_shape, in_specs=[any_spec] * n,
                               out_specs=[any_spec] * n, scratch_shapes=rider.scratch)(*rider.arrays))


def _pcall(body, *, name, grid, in_specs, out_specs, out_shape, args, scratch_shapes=(), rider=None, aliases=None):
    params = _cparams(*(("arbitrary",) * len(grid)))
    if rider is None:
        res = pl.pallas_call(body, name=name, grid=grid, in_specs=in_specs, out_specs=out_specs,
                             out_shape=out_shape, scratch_shapes=list(scratch_shapes),
                             input_output_aliases=aliases or {}, compiler_params=params)(*args)
        return list(res), []
    n_in, n_out, n_scr, rn = len(in_specs), len(out_specs), len(scratch_shapes), rider.n

    def wrapped(*refs):
        cuts = [n_in, rn, n_out, rn, n_scr]
        parts, p = [], 0
        for n in cuts:
            parts.append(refs[p:p + n])
            p += n
        ins, r_in, outs, r_out, scr = parts
        sems = refs[p:]
        ids = [pl.program_id(a) for a in range(len(grid))]
        first = functools.reduce(jnp.logical_and, [i == 0 for i in ids])
        last = functools.reduce(jnp.logical_and, [i == g - 1 for i, g in zip(ids, grid)])

        @pl.when(first)
        def _():
            rider.start(r_in, r_out, sems)
        body(*ins, *outs, *scr)

        @pl.when(last)
        def _():
            rider.finish(r_in, r_out, sems)

    any_spec = pl.BlockSpec(memory_space=pl.ANY)
    res = pl.pallas_call(
        wrapped, name=name, grid=grid, in_specs=list(in_specs) + [any_spec] * rn,
        out_specs=list(out_specs) + [any_spec] * rn, out_shape=list(out_shape) + rider.out_shape,
        scratch_shapes=list(scratch_shapes) + rider.scratch, compiler_params=params)(*args, *rider.arrays)
    return list(res[:n_out]), list(res[n_out:])


def _mod_cols(c_all, ada_w):
    nc = ada_w.shape[2]

    def body(c_ref, w_ref, o_ref):
        cv = c_ref[...]
        ca = cv * jax.nn.sigmoid(cv)
        for l in range(DEPTH):
            o_ref[:, l * nc:(l + 1) * nc] = jnp.dot(ca, w_ref[l], precision=HIGHEST, preferred_element_type=F32)

    return pl.pallas_call(body, name="mod_cols", out_shape=jax.ShapeDtypeStruct((NDEV, DEPTH * nc), F32),
                          compiler_params=_cparams())(c_all, ada_w)


def _pre_proj(x, P, mod, w_in_l, l, rider=None):
    S = x.shape[0]
    TM = min(512, S)
    NB = w_in_l.shape[2]

    def body(x_ref, g_ref, sc_ref, sh_ref, w_ref, ha_ref, hb_ref, p_ref):
        xv = x_ref[...]
        rstd = lax.rsqrt(jnp.mean(xv * xv, axis=-1, keepdims=True) + NORM_EPS)
        h = ((xv * rstd * g_ref[...]) * (1.0 + sc_ref[...]) + sh_ref[...]).astype(BF16)
        ha_ref[...] = h[:, :GW_SPLIT]
        hb_ref[...] = h[:, GW_SPLIT:]
        for j in range(NDEV):
            p_ref[:, j * NB:(j + 1) * NB] = jnp.dot(h, w_ref[j], preferred_element_type=F32)

    row = pl.BlockSpec((TM, D), lambda i: (i, 0))
    return _pcall(
        body, name="pre_proj", grid=(S // TM,),
        in_specs=[row, _vec(l), _vec(l, 1), _vec(l, 0), _full((NDEV, D, NB))],
        out_specs=[pl.BlockSpec((TM, GW_SPLIT), lambda i: (i, 0)), pl.BlockSpec((TM, D - GW_SPLIT), lambda i: (i, 0)),
                   pl.BlockSpec((TM, NDEV * NB), lambda i: (i, 0))],
        out_shape=[jax.ShapeDtypeStruct((S, GW_SPLIT), BF16), jax.ShapeDtypeStruct((S, D - GW_SPLIT), BF16),
                   jax.ShapeDtypeStruct((S, NDEV * NB), F32)],
        args=(x, P['pre_norm_g'], mod, mod, w_in_l), rider=rider)


def _taps(E):
    return [pltpu.roll(E, CONV_K - 1 - k, axis=0)[CONV_HALO:, :] for k in range(CONV_K - 1)] + [E[CONV_HALO:, :]]


def _conv(E, cw_ref, cb_ref):
    w = cw_ref[...]
    taps = _taps(E)
    acc = cb_ref[...] + taps[0] * w[0:1, :]
    for k in range(1, CONV_K):
        acc = acc + taps[k] * w[k:k + 1, :]
    return acc


CH_R, CH_C = 16, 512


def _chunks(T, fn):
    def step(c, carry):
        rows = pl.ds(pl.multiple_of(c * CH_R, CH_R), CH_R)
        for hf in range(D // CH_C):
            fn(rows, slice(hf * CH_C, (hf + 1) * CH_C), hf)
        return carry
    lax.fori_loop(0, T // CH_R, step, 0)


def _to_scan(ref, rows, hf, val):
    for q in range(CH_C // LANE):
        ref[hf * (CH_C // LANE) + q, rows, :] = val[:, q * LANE:(q + 1) * LANE]


def _scan(sa, sv, carry_ref, out_ref, reverse):
    NC, T, _ = sa.shape
    n8 = T // SUB
    rows = range(SUB - 2, -1, -1) if reverse else range(1, SUB)
    for cb in range(NC):
        r_in = SUB - 1 if reverse else 0
        Ap = sa[cb, pl.ds(r_in, n8, stride=SUB), :]
        Vp = sv[cb, pl.ds(r_in, n8, stride=SUB), :]
        for r in rows:
            Ar = sa[cb, pl.ds(r, n8, stride=SUB), :]
            Vp = sv[cb, pl.ds(r, n8, stride=SUB), :] + Ar * Vp
            Ap = Ar * Ap
            sa[cb, pl.ds(r, n8, stride=SUB), :] = Ap
            sv[cb, pl.ds(r, n8, stride=SUB), :] = Vp
    edge = 0 if reverse else SUB - 1

    def step(k, c):
        r0 = pl.multiple_of((n8 - 1 - k if reverse else k) * SUB, SUB)
        h = jnp.concatenate([sv[cb, pl.ds(r0, SUB), :] + sa[cb, pl.ds(r0, SUB), :] * c[:, cb * LANE:(cb + 1) * LANE]
                             for cb in range(NC)], axis=1)
        out_ref[pl.ds(r0, SUB), :] = h
        return jnp.broadcast_to(h[edge:edge + 1, :], (SUB, D))

    carry_ref[...] = lax.fori_loop(0, n8, step, carry_ref[...])


def _rnn_fwd(proj, P, l, rider=None):
    S = proj.shape[0]
    TB = min(512, S)

    def body(xr_ref, g_ref, cw_ref, cb_ref, wa_ref, ba_ref, wx_ref, bx_ref, lam_ref, hs_ref, y_ref,
             u_ref, r_ref, i_ref, a_ref, m_ref, xbuf, sa, sv, hc):
        @pl.when(pl.program_id(0) == 0)
        def _():
            xbuf[0:CONV_HALO, :] = jnp.zeros((CONV_HALO, D), F32)
            hc[...] = jnp.zeros((SUB, D), F32)
        xbuf[CONV_HALO:, :] = xr_ref[...]
        u = _conv(xbuf[...], cw_ref, cb_ref)
        xbuf[0:CONV_HALO, :] = xbuf[TB:TB + CONV_HALO, :]
        ub = u.astype(BF16)
        zr = jnp.concatenate([jnp.dot(ub[:, h * HD:(h + 1) * HD], wa_ref[h], preferred_element_type=F32)
                              for h in range(NHEAD)], axis=1)
        zi = jnp.concatenate([jnp.dot(ub[:, h * HD:(h + 1) * HD], wx_ref[h], preferred_element_type=F32)
                              for h in range(NHEAD)], axis=1)
        r = _sigmoid(zr + ba_ref[...])
        ig = _sigmoid(zi + bx_ref[...])
        log_a = r * (-LRU_C * jax.nn.softplus(-lam_ref[...]))
        a = jnp.exp(log_a)
        mult = jnp.sqrt(_one_minus_sq(a, log_a))
        v = mult * (ig * u)
        u_ref[...] = u
        r_ref[...] = r
        i_ref[...] = ig
        a_ref[...] = a
        m_ref[...] = mult
        for cb in range(D // LANE):
            sa[cb] = a[:, cb * LANE:(cb + 1) * LANE]
            sv[cb] = v[:, cb * LANE:(cb + 1) * LANE]
        _scan(sa, sv, hc, hs_ref, reverse=False)
        silu, _ = _silu_parts(g_ref[...])
        y_ref[...] = (hs_ref[...] * silu).astype(BF16)

    rowb = pl.BlockSpec((TB, D), lambda i: (i, 0))
    return _pcall(
        body, name="rnn_fwd", grid=(S // TB,),
        in_specs=[rowb, pl.BlockSpec((TB, D), lambda i: (i, 1)), _layer(l, (CONV_K, D)), _vec(l),
                  _layer(l, (NHEAD, HD, HD)), _vec(l), _layer(l, (NHEAD, HD, HD)), _vec(l), _vec(l)],
        out_specs=[rowb] * 7,
        out_shape=[jax.ShapeDtypeStruct((S, D), F32), jax.ShapeDtypeStruct((S, D), BF16)]
        + [jax.ShapeDtypeStruct((S, D), F32)] * 5,
        scratch_shapes=[pltpu.VMEM((TB + CONV_HALO, D), F32), pltpu.VMEM((D // LANE, TB, LANE), F32),
                        pltpu.VMEM((D // LANE, TB, LANE), F32), pltpu.VMEM((SUB, D), F32)],
        args=(proj, proj, P['conv_w'], P['conv_b'], P['gate_a_w'], P['gate_a_b'], P['gate_x_w'], P['gate_x_b'],
              P['lru_lambda']), rider=rider)


def _pooled(ebuf, t0, TB):
    tt = t0 + lax.broadcasted_iota(jnp.int32, (TB, 1), 0)
    pooled, inv = [], []
    for g, win in enumerate(WINS):
        Eg = ebuf[:, g * GD:(g + 1) * GD]
        L = Eg
        for lev in range(g + 1):
            L = L + pltpu.roll(L, 1 << lev, axis=0)
        icnt = 1.0 / jnp.minimum(tt + 1, win).astype(F32)
        pooled.append(L[POOL_HALO:, :] * icnt - Eg[POOL_HALO:, :])
        inv.append(icnt)
    return pooled, inv


def _pool_fwd(proj, P, l, rider=None):
    S = proj.shape[0]
    TB = min(512, S)

    def body(xp_ref, g_ref, pw_ref, pb_ref, ps_ref, y_ref, ebuf):
        i = pl.program_id(0)

        @pl.when(i == 0)
        def _():
            ebuf[0:POOL_HALO, :] = jnp.zeros((POOL_HALO, D), F32)
        ebuf[POOL_HALO:, :] = xp_ref[...]
        pooled, _ = _pooled(ebuf, i * TB, TB)
        ebuf[0:POOL_HALO, :] = ebuf[TB:TB + POOL_HALO, :]
        yp = jnp.concatenate([jnp.dot(pooled[g].astype(BF16), pw_ref[g], preferred_element_type=F32)
                              for g in range(NGRP)], axis=1) + pb_ref[...]
        silu, _ = _silu_parts(g_ref[...])
        y_ref[...] = (yp * ps_ref[...] * silu).astype(BF16)

    return _pcall(
        body, name="pool_fwd", grid=(S // TB,),
        in_specs=[pl.BlockSpec((TB, D), lambda i: (i, 2)), pl.BlockSpec((TB, D), lambda i: (i, 3)),
                  _layer(l, (NGRP, GD, GD)), _vec(l), _vec(l)],
        out_specs=[pl.BlockSpec((TB, D), lambda i: (i, 0))],
        out_shape=[jax.ShapeDtypeStruct((S, D), BF16)],
        scratch_shapes=[pltpu.VMEM((TB + POOL_HALO, D), F32)],
        args=(proj, proj, P['pool_w'], P['pool_b'], P['pool_scale']), rider=rider)


def _out_post(yr, yp, w_out_l, x, mod, P, l, target=None, rider=None):
    S = x.shape[0]
    TM = min(512, S)
    last = target is not None

    def body(*refs):
        if last:
            yr_ref, yp_ref, w_ref, x_ref, gate_ref, gp_ref, t_ref, y_ref, xo_ref, loss_ref = refs
        else:
            yr_ref, yp_ref, w_ref, x_ref, gate_ref, gp_ref, y_ref, xo_ref = refs
        acc = (jnp.dot(yr_ref[...], w_ref[0:D, :], preferred_element_type=F32)
               + jnp.dot(yp_ref[...], w_ref[D:2 * D, :], preferred_element_type=F32))
        y_ref[...] = acc
        rstd = lax.rsqrt(jnp.mean(acc * acc, axis=-1, keepdims=True) + NORM_EPS)
        xn = x_ref[...] + gate_ref[...] * (acc * rstd * gp_ref[...])
        if last:
            err = xn - t_ref[...]
            xo_ref[...] = err * (1.0 / D)

            @pl.when(pl.program_id(0) == 0)
            def _():
                loss_ref[...] = jnp.zeros((SUB, D), F32)
            loss_ref[...] += _rowsum8(err * err)
        else:
            xo_ref[...] = xn

    row = pl.BlockSpec((TM, D), lambda i: (i, 0))
    in_specs = [row, row, _full((2 * D, D)), row, _vec(l, 2), _vec(l)]
    out_specs = [row, row]
    out_shape = [jax.ShapeDtypeStruct((S, D), F32), jax.ShapeDtypeStruct((S, D), F32)]
    args = [yr, yp, w_out_l, x, mod, P['post_norm_g']]
    if last:
        in_specs.append(row)
        out_specs.append(_full((SUB, D)))
        out_shape.append(jax.ShapeDtypeStruct((SUB, D), F32))
        args.append(target)
    return _pcall(body, name="out_post_loss" if last else "out_post", grid=(S // TM,), in_specs=in_specs,
                  out_specs=out_specs, out_shape=out_shape, args=args, rider=rider)


def _out_bwd(dxo, y, yr, yp, w_out_l, mod, P, l, rider=None):
    S = y.shape[0]
    TM = min(512, S)
    nsteps = S // TM

    def body(dxo_ref, y_ref, yr_ref, yp_ref, w_ref, gate_ref, gp_ref, dyr_ref, dyp_ref, gw_ref, dgate_ref,
             dgp_ref, gw_acc, vacc):
        i = pl.program_id(0)

        @pl.when(i == 0)
        def _():
            gw_acc[...] = jnp.zeros_like(gw_acc)
            vacc[...] = jnp.zeros_like(vacc)
        yv = y_ref[...]
        dxo_v = dxo_ref[...]
        rstd = lax.rsqrt(jnp.mean(yv * yv, axis=-1, keepdims=True) + NORM_EPS)
        n = yv * rstd
        gp = gp_ref[...]
        vacc[0] += _rowsum8(dxo_v * (n * gp))
        drn = dxo_v * gate_ref[...]
        vacc[1] += _rowsum8(drn * n)
        dn = drn * gp
        dy = (rstd * (dn - n * jnp.mean(dn * n, axis=-1, keepdims=True))).astype(BF16)
        dyr_ref[...] = lax.dot_general(dy, w_ref[0:D, :], (((1,), (1,)), ((), ())), preferred_element_type=F32)
        dyp_ref[...] = lax.dot_general(dy, w_ref[D:2 * D, :], (((1,), (1,)), ((), ())),
                                       preferred_element_type=F32)
        gw_acc[0:D, :] += lax.dot_general(yr_ref[...], dy, (((0,), (0,)), ((), ())), preferred_element_type=F32)
        gw_acc[D:2 * D, :] += lax.dot_general(yp_ref[...], dy, (((0,), (0,)), ((), ())),
                                              preferred_element_type=F32)

        @pl.when(i == nsteps - 1)
        def _():
            gw_ref[...] = gw_acc[...].astype(BF16)
            dgate_ref[...] = _sum8(vacc[0])
            dgp_ref[...] = _sum8(vacc[1])

    row = pl.BlockSpec((TM, D), lambda i: (i, 0))
    return _pcall(
        body, name="out_bwd", grid=(nsteps,),
        in_specs=[row, row, row, row, _full((2 * D, D)), _vec(l, 2), _vec(l)],
        out_specs=[row, row, _full((2 * D, D)), _full((1, D)), _full((1, D))],
        out_shape=[jax.ShapeDtypeStruct((S, D), F32), jax.ShapeDtypeStruct((S, D), F32),
                   jax.ShapeDtypeStruct((2 * D, D), BF16), jax.ShapeDtypeStruct((1, D), F32),
                   jax.ShapeDtypeStruct((1, D), F32)],
        scratch_shapes=[pltpu.VMEM((2 * D, D), F32), pltpu.VMEM((2, SUB, D), F32)],
        args=(dxo, y, yr, yp, w_out_l, mod, P['post_norm_g']), rider=rider)


def _rnn_bwd(dyr, hs, fwd, proj, P, l, rider=None):
    S = proj.shape[0]
    TB = min(256, S)
    nb = S // TB
    TE = TB + CONV_HALO
    A_BA, A_BX, A_LAM, A_CB, A_CW = 0, 1, 2, 3, 4

    def blk(i):
        return nb - 1 - i

    def body(dyr_ref, hs_ref, hprev_ref, u_ref, r_ref, i_ref, a_ref, m_ref, xr_ref, g_ref, cw_ref,
             wa_ref, wx_ref, lam_ref, dxr_ref, dg_ref, gcw_ref, gcb_ref, gwa_ref, gba_ref, gwx_ref, gbx_ref,
             glam_ref, hbuf, abuf, dbuf, sa, sv, dh_ref, hp_ref, dzr_ref, dzi_ref, dhc, vacc):
        i = pl.program_id(0)
        first = blk(i) == 0

        @pl.when(i == 0)
        def _():
            abuf[TB:, :] = jnp.zeros((CONV_HALO, D), F32)
            dbuf[TB:, :] = jnp.zeros((CONV_HALO, D), F32)
            dhc[...] = jnp.zeros_like(dhc)
            vacc[...] = jnp.zeros_like(vacc)
            gwa_ref[...] = jnp.zeros_like(gwa_ref)
            gwx_ref[...] = jnp.zeros_like(gwx_ref)

        hbuf[0:CONV_HALO, :] = jnp.where(first, 0.0, hprev_ref[...])
        hbuf[CONV_HALO:, :] = hs_ref[...]
        hp_ref[...] = pltpu.roll(hbuf[...], 1, axis=0)[CONV_HALO:, :]
        abuf[0:TB, :] = a_ref[...]
        b = pltpu.roll(abuf[...], TE - 1, axis=0)[0:TB, :]
        for cb in range(D // LANE):
            sa[cb] = b[:, cb * LANE:(cb + 1) * LANE]
        abuf[TB:, :] = a_ref[0:CONV_HALO, :]

        def gate_bwd(rows, cs, hf):
            silu, dsilu = _silu_parts(g_ref[rows, cs])
            dy = dyr_ref[rows, cs]
            dg_ref[rows, cs] = (dy * hs_ref[rows, cs] * dsilu).astype(BF16)
            _to_scan(sv, rows, hf, dy * silu)
        _chunks(TB, gate_bwd)

        _scan(sa, sv, dhc, dh_ref, reverse=True)

        csp = -LRU_C * jax.nn.softplus(-lam_ref[...])

        def lru_bwd(rows, cs, hf):
            dh, a, ig, u, mult, r = dh_ref[rows, cs], a_ref[rows, cs], i_ref[rows, cs], u_ref[rows, cs], \
                m_ref[rows, cs], r_ref[rows, cs]
            dlog_a = dh * hp_ref[rows, cs] * a - (dh * ig * u) * (a * a) / mult
            dzr = dlog_a * csp[:, cs] * r * (1.0 - r)
            dzi = (dh * mult * u) * ig * (1.0 - ig)
            dzr_ref[rows, cs] = dzr.astype(BF16)
            dzi_ref[rows, cs] = dzi.astype(BF16)
            dbuf[rows, cs] = dh * mult * ig
            vacc[A_LAM, :, cs] += _rowsum8(dlog_a * r)
            vacc[A_BA, :, cs] += _rowsum8(dzr)
            vacc[A_BX, :, cs] += _rowsum8(dzi)
        _chunks(TB, lru_bwd)

        ub, dzrb, dzib = u_ref[...].astype(BF16), dzr_ref[...], dzi_ref[...]
        du_g = []
        for h in range(NHEAD):
            cs = slice(h * HD, (h + 1) * HD)
            gwa_ref[h] += lax.dot_general(ub[:, cs], dzrb[:, cs], (((0,), (0,)), ((), ())),
                                          preferred_element_type=F32)
            gwx_ref[h] += lax.dot_general(ub[:, cs], dzib[:, cs], (((0,), (0,)), ((), ())),
                                          preferred_element_type=F32)
            du_g.append(lax.dot_general(dzrb[:, cs], wa_ref[h], (((1,), (1,)), ((), ())),
                                        preferred_element_type=F32)
                        + lax.dot_general(dzib[:, cs], wx_ref[h], (((1,), (1,)), ((), ())),
                                          preferred_element_type=F32))
        du = dbuf[0:TB, :] + jnp.concatenate(du_g, axis=1)
        dbuf[0:TB, :] = du

        Dd = dbuf[...]
        w = cw_ref[...]
        xv = xr_ref[...]
        dx = du * w[CONV_K - 1:CONV_K, :]
        vacc[A_CB] += _rowsum8(du)
        vacc[A_CW + CONV_K - 1] += _rowsum8(xv * du)
        for k in range(CONV_K - 1):
            ahead = pltpu.roll(Dd, TE - (CONV_K - 1 - k), axis=0)[0:TB, :]
            dx = dx + ahead * w[k:k + 1, :]
            vacc[A_CW + k] += _rowsum8(xv * ahead)
        dxr_ref[...] = dx.astype(BF16)
        dbuf[TB:, :] = du[0:CONV_HALO, :]

        @pl.when(i == nb - 1)
        def _():
            gba_ref[...] = _sum8(vacc[A_BA])
            gbx_ref[...] = _sum8(vacc[A_BX])
            glam_ref[...] = _sum8(vacc[A_LAM]) * (LRU_C * _sigmoid(-lam_ref[...]))
            gcb_ref[...] = _sum8(vacc[A_CB])
            for k in range(CONV_K):
                gcw_ref[k:k + 1, :] = _sum8(vacc[A_CW + k])

    rowb = pl.BlockSpec((TB, D), lambda i: (blk(i), 0))
    halo = pl.BlockSpec((SUB, D), lambda i: (jnp.maximum(blk(i) * (TB // SUB) - 1, 0), 0))
    wspec = _full((NHEAD, HD, HD))
    wlay = _layer(l, (NHEAD, HD, HD))
    vec1 = _full((1, D))
    return _pcall(
        body, name="rnn_bwd", grid=(nb,),
        in_specs=[rowb, rowb, halo] + [rowb] * 5 + [rowb, pl.BlockSpec((TB, D), lambda i: (blk(i), 1)),
                                                    _layer(l, (CONV_K, D)), wlay, wlay, _vec(l)],
        out_specs=[rowb, rowb, _full((CONV_K, D)), vec1, wspec, vec1, wspec, vec1, vec1],
        out_shape=[jax.ShapeDtypeStruct((S, D), BF16), jax.ShapeDtypeStruct((S, D), BF16),
                   jax.ShapeDtypeStruct((CONV_K, D), F32), jax.ShapeDtypeStruct((1, D), F32),
                   jax.ShapeDtypeStruct((NHEAD, HD, HD), F32), jax.ShapeDtypeStruct((1, D), F32),
                   jax.ShapeDtypeStruct((NHEAD, HD, HD), F32), jax.ShapeDtypeStruct((1, D), F32),
                   jax.ShapeDtypeStruct((1, D), F32)],
        scratch_shapes=[pltpu.VMEM((TE, D), F32), pltpu.VMEM((TE, D), F32), pltpu.VMEM((TE, D), F32),
                        pltpu.VMEM((D // LANE, TB, LANE), F32), pltpu.VMEM((D // LANE, TB, LANE), F32),
                        pltpu.VMEM((TB, D), F32), pltpu.VMEM((TB, D), F32), pltpu.VMEM((TB, D), BF16),
                        pltpu.VMEM((TB, D), BF16), pltpu.VMEM((SUB, D), F32),
                        pltpu.VMEM((A_CW + CONV_K, SUB, D), F32)],
        args=(dyr, hs, hs, *fwd, proj, proj, P['conv_w'], P['gate_a_w'], P['gate_x_w'], P['lru_lambda']),
        rider=rider)


def _pool_bwd(dyp, proj, P, l, rider=None):
    S = proj.shape[0]
    TB = min(512, S)
    nb = S // TB
    TE = TB + POOL_HALO

    def blk(i):
        return nb - 1 - i

    def body(dy_ref, xp_ref, xprev_ref, g_ref, pw_ref, pb_ref, ps_ref, dxp_ref, dg_ref, gpw_ref, gpb_ref,
             gps_ref, ebuf, qbuf, vacc):
        i = pl.program_id(0)
        first = blk(i) == 0

        @pl.when(i == 0)
        def _():
            qbuf[TB:, :] = jnp.zeros((POOL_HALO, D), F32)
            vacc[...] = jnp.zeros_like(vacc)
            gpw_ref[...] = jnp.zeros_like(gpw_ref)

        ebuf[0:POOL_HALO, :] = jnp.where(first, 0.0, xprev_ref[...])
        ebuf[POOL_HALO:, :] = xp_ref[...]
        pooled, inv = _pooled(ebuf, blk(i) * TB, TB)
        pooled = [p.astype(BF16) for p in pooled]
        yp = jnp.concatenate([jnp.dot(pooled[g], pw_ref[g], preferred_element_type=F32)
                              for g in range(NGRP)], axis=1) + pb_ref[...]
        silu, dsilu = _silu_parts(g_ref[...])
        dy = dy_ref[...]
        ps = ps_ref[...]
        dyp_v = dy * ps * silu
        vacc[0] += _rowsum8(dy * yp * silu)
        vacc[1] += _rowsum8(dyp_v)
        dg_ref[...] = (dy * yp * ps * dsilu).astype(BF16)
        dypb = dyp_v.astype(BF16)
        for g in range(NGRP):
            cs = slice(g * GD, (g + 1) * GD)
            gpw_ref[g] += lax.dot_general(pooled[g], dypb[:, cs], (((0,), (0,)), ((), ())),
                                          preferred_element_type=F32)
            dpool = lax.dot_general(dypb[:, cs], pw_ref[g], (((1,), (1,)), ((), ())),
                                    preferred_element_type=F32)
            qbuf[0:TB, cs] = dpool * inv[g]
            L = qbuf[:, cs]
            for lev in range(g + 1):
                L = L + pltpu.roll(L, TE - (1 << lev), axis=0)
            dxp_ref[:, cs] = (L[0:TB, :] - dpool).astype(BF16)
        qbuf[TB:, :] = qbuf[0:POOL_HALO, :]

        @pl.when(i == nb - 1)
        def _():
            gps_ref[...] = _sum8(vacc[0])
            gpb_ref[...] = _sum8(vacc[1])

    rowb = pl.BlockSpec((TB, D), lambda i: (blk(i), 0))
    return _pcall(
        body, name="pool_bwd", grid=(nb,),
        in_specs=[rowb, pl.BlockSpec((TB, D), lambda i: (blk(i), 2)),
                  pl.BlockSpec((POOL_HALO, D), lambda i: (jnp.maximum(blk(i) * (TB // POOL_HALO) - 1, 0), 2)),
                  pl.BlockSpec((TB, D), lambda i: (blk(i), 3)), _layer(l, (NGRP, GD, GD)), _vec(l), _vec(l)],
        out_specs=[rowb, rowb, _full((NGRP, GD, GD)), _full((1, D)), _full((1, D))],
        out_shape=[jax.ShapeDtypeStruct((S, D), BF16), jax.ShapeDtypeStruct((S, D), BF16),
                   jax.ShapeDtypeStruct((NGRP, GD, GD), F32), jax.ShapeDtypeStruct((1, D), F32),
                   jax.ShapeDtypeStruct((1, D), F32)],
        scratch_shapes=[pltpu.VMEM((TE, D), F32), pltpu.VMEM((TE, D), F32), pltpu.VMEM((2, SUB, D), F32)],
        args=(dyp, proj, proj, proj, P['pool_w'], P['pool_b'], P['pool_scale']), rider=rider)


def _in_bwd(dq, w_in_l, x, dxo, P, mod, l, rider=None):
    S = x.shape[0]
    TM = min(256, S)
    NB = w_in_l.shape[2]
    nsteps = S // TM
    per_q = D // NB

    def body(d0, d1, d2, d3, w_ref, x_ref, dxo_ref, g_ref, sc_ref, dx_ref, dsh_ref, dsc_ref, dg_ref, vacc):
        i = pl.program_id(0)

        @pl.when(i == 0)
        def _():
            vacc[...] = jnp.zeros_like(vacc)
        dref = (d0, d1, d2, d3)
        dh = jnp.zeros((TM, D), F32)
        for j in range(NDEV):
            c0 = (j % per_q) * NB
            dh = dh + lax.dot_general(dref[j // per_q][:, c0:c0 + NB], w_ref[j], (((1,), (1,)), ((), ())),
                                      preferred_element_type=F32)
        xv = x_ref[...]
        rstd = lax.rsqrt(jnp.mean(xv * xv, axis=-1, keepdims=True) + NORM_EPS)
        xn = xv * rstd
        g, sc = g_ref[...], 1.0 + sc_ref[...]
        vacc[0] += _rowsum8(dh)
        vacc[1] += _rowsum8(dh * (xn * g))
        vacc[2] += _rowsum8(dh * sc * xn)
        dxn = dh * sc * g
        dx_ref[...] = dxo_ref[...] + rstd * (dxn - xn * jnp.mean(dxn * xn, axis=-1, keepdims=True))

        @pl.when(i == nsteps - 1)
        def _():
            dsh_ref[...] = _sum8(vacc[0])
            dsc_ref[...] = _sum8(vacc[1])
            dg_ref[...] = _sum8(vacc[2])

    row = pl.BlockSpec((TM, D), lambda i: (i, 0))
    return _pcall(
        body, name="in_bwd", grid=(nsteps,),
        in_specs=[row, row, row, row, _full((NDEV, D, NB)), row, row, _vec(l), _vec(l, 1)],
        out_specs=[row, _full((1, D)), _full((1, D)), _full((1, D))],
        out_shape=[jax.ShapeDtypeStruct((S, D), F32)] + [jax.ShapeDtypeStruct((1, D), F32)] * 3,
        scratch_shapes=[pltpu.VMEM((3, SUB, D), F32)],
        args=(*dq, w_in_l, x, dxo, P['pre_norm_g'], mod), rider=rider)


def _grad_w_in(h, dq, NB, rider=None):
    S, RH = h.shape
    TK = min(1024, S)
    nk = S // TK

    def body(h_ref, d0, d1, d2, d3, o_ref, acc):
        k = pl.program_id(0)

        @pl.when(k == 0)
        def _():
            acc[...] = jnp.zeros_like(acc)
        hv = h_ref[...]
        for q, d_ref in enumerate((d0, d1, d2, d3)):
            acc[:, q * D:(q + 1) * D] += lax.dot_general(hv, d_ref[...], (((0,), (0,)), ((), ())),
                                                         preferred_element_type=F32)

        @pl.when(k == nk - 1)
        def _():
            for j in range(NDEV):
                o_ref[j] = acc[:, j * NB:(j + 1) * NB].astype(BF16)

    row = pl.BlockSpec((TK, D), lambda k: (k, 0))
    return _pcall(
        body, name="grad_w_in", grid=(nk,),
        in_specs=[pl.BlockSpec((TK, RH), lambda k: (k, 0)), row, row, row, row],
        out_specs=[_full((NDEV, RH, NB))],
        out_shape=[jax.ShapeDtypeStruct((NDEV, RH, NB), BF16)],
        scratch_shapes=[pltpu.VMEM((RH, NQ * D), F32)],
        args=(h, *dq), rider=rider)


def _adamw_math(g, w, m, v):
    m2 = ADAM_B1 * m + (1.0 - ADAM_B1) * g
    v2 = ADAM_B2 * v + (1.0 - ADAM_B2) * (g * g)
    m_hat = m2 / (1.0 - ADAM_B1 ** ADAM_STEP)
    v_hat = v2 / (1.0 - ADAM_B2 ** ADAM_STEP)
    delta = -ADAM_LR * (m_hat / (jnp.sqrt(v_hat) + ADAM_EPS) + ADAM_WD * w)
    return delta, m2, v2


def _adamw(name, gs, w, m, v, TR, row0=0, into=None):
    L = len(gs)
    n, Rp, C = gs[0].shape
    R = w.shape[1]
    b0 = row0 // TR

    def body(*refs):
        g_refs = refs[:L]
        w_ref, m_ref, v_ref = refs[L:L + 3]
        go_ref, do_ref, mo_ref, vo_ref = refs[-4:]
        lay = pl.program_id(0)
        for li in range(L):
            @pl.when(lay == li)
            def _(li=li):
                g = g_refs[li][0].astype(F32)
                for s in range(1, n):
                    g = g + g_refs[li][s].astype(F32)
                delta, m2, v2 = _adamw_math(g, w_ref[...], m_ref[...], v_ref[...])
                go_ref[...] = g
                do_ref[...] = delta
                mo_ref[...] = m2
                vo_ref[...] = v2

    lrc = pl.BlockSpec((None, TR, C), lambda lay, r: (lay, r + b0, 0))
    g_specs = [pl.BlockSpec((n, TR, C), lambda lay, r, li=li: (0, jnp.where(lay == li, r, 0), 0))
               for li in range(L)]
    in_specs, args, aliases = g_specs + [lrc, lrc, lrc], [*gs, w, m, v], None
    if into is not None:
        aliases = {len(args) + k: k for k in range(4)}
        in_specs = in_specs + [pl.BlockSpec(memory_space=pl.ANY)] * 4
        args = args + list(into)
    return _pcall(
        body, name=name, grid=(L, Rp // TR), in_specs=in_specs, out_specs=[lrc] * 4,
        out_shape=[jax.ShapeDtypeStruct((L, R, C), F32)] * 4, args=args, aliases=aliases)


def _ada_adamw(c_all_t, dm, w, m, v, rider=None):
    L, _, nc = w.shape

    def body(c_ref, dm_ref, w_ref, m_ref, v_ref, go_ref, do_ref, mo_ref, vo_ref):
        cv = c_ref[...]
        ca = cv * jax.nn.sigmoid(cv)
        dmv = dm_ref[...]
        g = ca[:, 0:1] * dmv[0:1, :]
        for b in range(1, NDEV):
            g = g + ca[:, b:b + 1] * dmv[b:b + 1, :]
        delta, m2, v2 = _adamw_math(g, w_ref[...], m_ref[...], v_ref[...])
        go_ref[...] = g
        do_ref[...] = delta
        mo_ref[...] = m2
        vo_ref[...] = v2

    big = pl.BlockSpec((None, D, nc), lambda lay: (lay, 0, 0))
    return _pcall(
        body, name="ada_adamw", grid=(L,),
        in_specs=[_full((D, NDEV)), pl.BlockSpec((None, NDEV, nc), lambda lay: (lay, 0, 0)), big, big, big],
        out_specs=[big] * 4, out_shape=[jax.ShapeDtypeStruct((L, D, nc), F32)] * 4,
        args=(c_all_t, dm, w, m, v), rider=rider)


def _sum_slots(recv):
    n, R, C = recv.shape

    def body(r_ref, o_ref):
        acc = r_ref[0].astype(F32)
        for s in range(1, n):
            acc = acc + r_ref[s].astype(F32)
        o_ref[...] = acc

    return pl.pallas_call(body, name="sum_slots", out_shape=jax.ShapeDtypeStruct((R, C), F32),
                          compiler_params=_cparams())(recv)


def _pad_rows(a, rows):
    return jnp.pad(a, ((0, rows - a.shape[0]), (0, 0)))


def _pack_sharded_block(pool_w, pool_b, conv_w):
    return jnp.concatenate([pool_w.reshape(-1, PACK_C), _pad_rows(pool_b.reshape(-1, PACK_C), SUB),
                            _pad_rows(conv_w.reshape(-1, PACK_C), SUB)], axis=0)


def _unpack_sharded_block(p):
    n_pw = DEPTH * NGRP * (GD // NDEV)
    pool_w = p[:n_pw].reshape(DEPTH, NGRP, GD // NDEV, GD)
    pool_b = p[n_pw].reshape(DEPTH, NGRP, GD // NDEV)
    conv_w = p[n_pw + SUB:n_pw + SUB + DEPTH * CONV_K * (D // NDEV) // PACK_C].reshape(DEPTH, CONV_K, D // NDEV)
    return pool_w, pool_b, conv_w


def _blocks_of_full(pool_w, pool_b, conv_w):
    pw = pool_w.reshape(DEPTH, NGRP, NDEV, GD // NDEV, GD).transpose(2, 0, 1, 3, 4).reshape(NDEV, -1, PACK_C)
    pb = pool_b.reshape(DEPTH, NGRP, NDEV, GD // NDEV).transpose(2, 0, 1, 3).reshape(NDEV, -1, PACK_C)
    cw = conv_w.reshape(DEPTH, CONV_K, NDEV, D // NDEV).transpose(2, 0, 1, 3).reshape(NDEV, -1, PACK_C)
    pad = lambda a: jnp.pad(a, ((0, 0), (0, SUB - a.shape[1]), (0, 0)))
    return jnp.concatenate([pw, pad(pb), pad(cw)], axis=1)


def _full_of_blocks(p):
    n_pw = DEPTH * NGRP * (GD // NDEV)
    pool_w = p[:, :n_pw].reshape(NDEV, DEPTH, NGRP, GD // NDEV, GD).transpose(1, 2, 0, 3, 4)
    pool_b = p[:, n_pw].reshape(NDEV, DEPTH, NGRP, GD // NDEV).transpose(1, 2, 0, 3)
    n_cw = DEPTH * CONV_K * (D // NDEV) // PACK_C
    conv_w = p[:, n_pw + SUB:n_pw + SUB + n_cw].reshape(NDEV, DEPTH, CONV_K, D // NDEV).transpose(1, 2, 0, 3)
    return (pool_w.reshape(DEPTH, NGRP, GD, GD), pool_b.reshape(DEPTH, NGRP, GD),
            conv_w.reshape(DEPTH, CONV_K, D))


def _pack_replicated(t, keys, rows):
    p = jnp.concatenate([t[k].reshape(-1, PACK_C) for k in keys], axis=0)
    return _pad_rows(p, rows)


def _unpack_replicated(p, like, keys):
    out, r0 = {}, 0
    for k in keys:
        rows = like[k].size // PACK_C
        out[k] = p[r0:r0 + rows].reshape(like[k].shape)
        r0 += rows
    return out


def kernel(x, c, ada_w, ada_b, pre_norm_g, w_in, conv_w, conv_b, gate_a_w, gate_a_b, gate_x_w, gate_x_b, lru_lambda, pool_w, pool_b, pool_scale, w_out, post_norm_g, loss_target, m_ada_w, m_ada_b, m_pre_norm_g, m_w_in, m_conv_w, m_conv_b, m_gate_a_w, m_gate_a_b, m_gate_x_w, m_gate_x_b, m_lru_lambda, m_pool_w, m_pool_b, m_pool_scale, m_w_out, m_post_norm_g, v_ada_w, v_ada_b, v_pre_norm_g, v_w_in, v_conv_w, v_conv_b, v_gate_a_w, v_gate_a_b, v_gate_x_w, v_gate_x_b, v_lru_lambda, v_pool_w, v_pool_b, v_pool_scale, v_w_out, v_post_norm_g):
    W = dict(ada_w=ada_w, ada_b=ada_b, pre_norm_g=pre_norm_g, w_in=w_in, conv_w=conv_w, conv_b=conv_b,
             gate_a_w=gate_a_w, gate_a_b=gate_a_b, gate_x_w=gate_x_w, gate_x_b=gate_x_b, lru_lambda=lru_lambda,
             pool_w=pool_w, pool_b=pool_b, pool_scale=pool_scale, w_out=w_out, post_norm_g=post_norm_g)
    M = dict(ada_w=m_ada_w, ada_b=m_ada_b, pre_norm_g=m_pre_norm_g, w_in=m_w_in, conv_w=m_conv_w,
             conv_b=m_conv_b, gate_a_w=m_gate_a_w, gate_a_b=m_gate_a_b, gate_x_w=m_gate_x_w,
             gate_x_b=m_gate_x_b, lru_lambda=m_lru_lambda, pool_w=m_pool_w, pool_b=m_pool_b,
             pool_scale=m_pool_scale, w_out=m_w_out, post_norm_g=m_post_norm_g)
    V = dict(ada_w=v_ada_w, ada_b=v_ada_b, pre_norm_g=v_pre_norm_g, w_in=v_w_in, conv_w=v_conv_w,
             conv_b=v_conv_b, gate_a_w=v_gate_a_w, gate_a_b=v_gate_a_b, gate_x_w=v_gate_x_w,
             gate_x_b=v_gate_x_b, lru_lambda=v_lru_lambda, pool_w=v_pool_w, pool_b=v_pool_b,
             pool_scale=v_pool_scale, w_out=v_w_out, post_norm_g=v_post_norm_g)
    S = x.shape[1]
    me = 4 * lax.axis_index("x") + 2 * lax.axis_index("y") + lax.axis_index("c")
    xs = x.reshape(S, D)
    tgt = loss_target.reshape(S, D)
    nc = ada_w.shape[2]
    NB = w_in.shape[2]
    w_in_b, w_out_b = w_in.astype(BF16), w_out.astype(BF16)
    rows = lambda a: a.reshape(DEPTH, 1, D)
    P = dict(pre_norm_g=rows(pre_norm_g), conv_b=rows(conv_b), gate_a_b=rows(gate_a_b), gate_x_b=rows(gate_x_b),
             lru_lambda=rows(lru_lambda), pool_scale=rows(pool_scale), post_norm_g=rows(post_norm_g),
             gate_a_w=gate_a_w.astype(BF16), gate_x_w=gate_x_w.astype(BF16))

    c_slots, w_in0 = _exchange("gather_c_w_in0", _AllGather2([jnp.broadcast_to(c, (SUB, D)), w_in_b[0]]))
    c_all = c_slots[:, 0, :]
    (mod_slots,) = _exchange("gather_mod", _Direct(ag=[_mod_cols(c_all, ada_w)]))
    mod = lax.dynamic_index_in_dim(mod_slots, me, axis=1, keepdims=False)
    mod = (mod.reshape(NDEV, DEPTH, nc).transpose(1, 0, 2).reshape(DEPTH, 3 * D) + ada_b).reshape(DEPTH, 3, 1, D)

    w_in_all, w_out_all = [w_in0, None], [None, None]
    saved = []
    xl = xs
    flat = lambda w_slots: w_slots.reshape(2 * D, D)
    for l in range(DEPTH):
        rider = _AllGather2([_pack_sharded_block(pool_w, pool_b, conv_w), w_out_b[0]]) if l == 0 else None
        (h_a, h_b, proj), got = _pre_proj(xl, P, mod, w_in_all[l], l, rider=rider)
        if l == 0:
            pool_w_f, pool_b_f, conv_w_f = _full_of_blocks(got[0])
            P.update(conv_w=conv_w_f, pool_w=pool_w_f.astype(BF16), pool_b=rows(pool_b_f))
            w_out_all[0] = flat(got[1])
        rider = _AllGather2([w_in_b[1]]) if l == 0 else None
        (hs, yr, *fwd), got = _rnn_fwd(proj, P, l, rider=rider)
        if l == 0:
            w_in_all[1] = got[0]
        (yp,), _ = _pool_fwd(proj, P, l)
        if l == DEPTH - 1:
            (y, x_next, loss_acc), _ = _out_post(yr, yp, w_out_all[l], xl, mod, P, l, tgt)
        else:
            (y, x_next), got = _out_post(yr, yp, w_out_all[l], xl, mod, P, l, rider=_AllGather2([w_out_b[1]]))
            w_out_all[1] = flat(got[0])
        saved.append((xl, h_a, h_b, proj, hs, fwd, yr, yp, y))
        xl = x_next

    dxo = xl
    G = {k: [None] * DEPTH for k in WEIGHTS}
    dmod = [None] * DEPTH
    recv_in, recv_out = [[None, None] for _ in range(DEPTH)], [None] * DEPTH
    full = dict(conv_w=(CONV_K, D), pool_w=(NGRP, GD, GD), pool_b=(NGRP, GD))
    stack = lambda k: jnp.stack([g.reshape(full.get(k, W[k].shape[1:])) for g in G[k]])
    gw_bot_prev = None
    for l in reversed(range(DEPTH)):
        xin, h_a, h_b, proj, hs, fwd, yr, yp, y = saved[l]
        rider = _AllGather2([loss_acc]) if gw_bot_prev is None else _Direct(a2a=[gw_bot_prev])
        (dyr, dyp, gw_out, dgate, G['post_norm_g'][l]), got = _out_bwd(dxo, y, yr, yp, w_out_all[l], mod, P, l,
                                                                       rider=rider)
        if gw_bot_prev is None:
            loss = (0.5 / D) * jnp.sum(got[0])
        else:
            recv_in[l + 1][1] = got[0]
        ((dxr, dgr, G['conv_w'][l], G['conv_b'][l], G['gate_a_w'][l], G['gate_a_b'][l], G['gate_x_w'][l],
          G['gate_x_b'][l], G['lru_lambda'][l]), (recv_out[l],)) = _rnn_bwd(
            dyr, hs, fwd, proj, P, l,
            rider=_Direct(a2a=[gw_out.reshape(NDEV, 2 * D // NDEV, D)]))
        (dxp, dgp, G['pool_w'][l], G['pool_b'][l], G['pool_scale'][l]), _ = _pool_bwd(dyp, proj, P, l)
        dq = (dxr, dgr, dxp, dgp)
        if l > 0:
            (gw_top,), _ = _grad_w_in(h_a, dq, NB)
            (dxo, dshift, dscale, G['pre_norm_g'][l]), (recv_in[l][0],) = _in_bwd(
                dq, w_in_all[l], xin, dxo, P, mod, l, rider=_Direct(a2a=[gw_top]))
            (gw_bot_prev,), _ = _grad_w_in(h_b, dq, NB)
        else:
            Ge = {k: stack(k) for k in REP_EARLY + ['pool_w', 'pool_b', 'conv_w']}
            early = jnp.concatenate([_blocks_of_full(Ge['pool_w'], Ge['pool_b'], Ge['conv_w']),
                                     _pack_replicated(Ge, REP_EARLY, NDEV * PACK_ROWS).reshape(NDEV, PACK_ROWS, PACK_C)],
                                    axis=1).astype(BF16)
            (gw_top,), (early_recv,) = _grad_w_in(h_a, dq, NB, rider=_Direct(a2a=[early]))
            early_sum = _sum_slots(early_recv)
            (gw_bot,), (recv_in[l][0],) = _grad_w_in(h_b, dq, NB, rider=_Direct(a2a=[gw_top]))
            (dxo, dshift, dscale, G['pre_norm_g'][l]), (recv_in[l][1], early_all) = _in_bwd(
                dq, w_in_all[l], xin, dxo, P, mod, l,
                rider=_Both(_Direct(a2a=[gw_bot]), _AllGather2([early_sum[PACK_ROWS:]])))
        dmod[l] = jnp.concatenate([dshift, dscale, dgate], axis=1)
    grad_x = dxo.reshape(x.shape)

    Gl = dict(ada_b=jnp.concatenate(dmod, axis=0), pre_norm_g=stack('pre_norm_g'))
    (late_slots,) = _exchange("gather_late", _AllGather2([_pack_replicated(Gl, REP_LATE, LATE_PACK_ROWS)]))
    late_sum = _sum_slots(late_slots)
    dmod_all = late_slots[:, :DEPTH * 3 * D // PACK_C]

    out = {}
    first, _ = _adamw("adamw_w_in_a", [recv_in[l][0] for l in range(DEPTH)], w_in, M['w_in'], V['w_in'], LANE)
    out['w_in'], _ = _adamw("adamw_w_in_b", [recv_in[l][1] for l in range(DEPTH)], w_in, M['w_in'], V['w_in'], LANE,
                            row0=GW_SPLIT, into=first)
    out['w_out'], _ = _adamw("adamw_w_out", recv_out, w_out, M['w_out'], V['w_out'], 256)
    dm = lax.dynamic_slice_in_dim(dmod_all.reshape(NDEV, DEPTH, 3 * D), me * nc, nc, axis=2)
    out['ada_w'], _ = _ada_adamw(c_all.T, dm.transpose(1, 0, 2), ada_w, M['ada_w'], V['ada_w'])
    g_small = jnp.concatenate([early_sum[:PACK_ROWS], early_all.reshape(NDEV * PACK_ROWS, PACK_C), late_sum],
                              axis=0)

    def packs(T):
        return jnp.concatenate([_pack_sharded_block(T['pool_w'], T['pool_b'], T['conv_w']),
                                _pack_replicated(T, REP_EARLY, NDEV * PACK_ROWS),
                                _pack_replicated(T, REP_LATE, LATE_PACK_ROWS)], axis=0)[None]
    res_small, _ = _adamw("adamw_small", [g_small[None]], packs(W), packs(M), packs(V), g_small.shape[0] // 2)
    n_early = (1 + NDEV) * PACK_ROWS
    for idx in range(4):
        p = res_small[idx][0]
        pw_, pb_, cw_ = _unpack_sharded_block(p[:PACK_ROWS])
        rep = _unpack_replicated(p[PACK_ROWS:n_early], W, REP_EARLY)
        rep.update(_unpack_replicated(p[n_early:], W, REP_LATE))
        rep.update(pool_w=pw_, pool_b=pb_, conv_w=cw_)
        for k, a in rep.items():
            out.setdefault(k, [None] * 4)[idx] = a
    for k in ('w_in', 'w_out', 'ada_w'):
        out[k] = [a.reshape(W[k].shape) for a in out[k]]

    return (loss, grad_x, *[out[k][0] for k in WEIGHTS], *[out[k][1] for k in WEIGHTS],
            *[out[k][2] for k in WEIGHTS], *[out[k][3] for k in WEIGHTS])
```

```python
import functools

import jax
import jax.numpy as jnp
from jax import lax
from jax.experimental import pallas as pl
from jax.experimental.pallas import tpu as pltpu

F32, BF16 = jnp.float32, jnp.bfloat16
MESH = pl.DeviceIdType.MESH
HIGHEST = lax.Precision.HIGHEST

NDEV = 8
DEPTH = 2
D = 1024
NHEAD, HD = 8, 128
NGRP, GD = 4, 256
WINS = (2, 4, 8, 16)
CONV_K = 4
CONV_HALO = 8
POOL_HALO = 16
LRU_C = 8.0
NORM_EPS = 1e-6
ADAM_LR, ADAM_B1, ADAM_B2, ADAM_EPS, ADAM_WD, ADAM_STEP = 0.001, 0.9, 0.999, 1e-08, 0.01, 10
VMEM_LIMIT = 56 * 1024 * 1024
NQ = 4
SUB = 8
LANE = 128
PACK_C = 256
PACK_ROWS = 272

WEIGHTS = ['ada_w', 'ada_b', 'pre_norm_g', 'w_in', 'conv_w', 'conv_b', 'gate_a_w', 'gate_a_b', 'gate_x_w',
           'gate_x_b', 'lru_lambda', 'pool_w', 'pool_b', 'pool_scale', 'w_out', 'post_norm_g']
REP_EARLY = ['conv_b', 'gate_a_w', 'gate_a_b', 'gate_x_w', 'gate_x_b', 'lru_lambda', 'pool_scale', 'post_norm_g']
REP_LATE = ['ada_b', 'pre_norm_g']
GW_SPLIT = 512
LATE_PACK_ROWS = 32


def _cparams(*sem):
    return pltpu.CompilerParams(dimension_semantics=sem, vmem_limit_bytes=VMEM_LIMIT)


def _vec(l, k=None):
    if k is None:
        return pl.BlockSpec((None, 1, D), lambda *_: (l, 0, 0))
    return pl.BlockSpec((None, None, 1, D), lambda *_: (l, k, 0, 0))


def _layer(l, shape):
    nd = len(shape)
    return pl.BlockSpec((None,) + tuple(shape), lambda *_: (l,) + (0,) * nd)


def _full(shape):
    nd = len(shape)
    return pl.BlockSpec(shape, lambda *_: (0,) * nd)


def _rowsum8(z):
    return z.reshape(z.shape[0] // SUB, SUB, z.shape[1]).sum(axis=0)


def _sum8(acc):
    return jnp.sum(acc, axis=0, keepdims=True)


def _sigmoid(z):
    return 0.5 * jnp.tanh(0.5 * z) + 0.5


def _silu_parts(g):
    sg = _sigmoid(g)
    return g * sg, sg * (1.0 + g * (1.0 - sg))


def _one_minus_sq(a, log_a):
    z = 2.0 * log_a
    p = 1.0 / 24.0
    for k in (6.0, 2.0, 1.0):
        p = p * z + 1.0 / k
    return jnp.where(z > -0.03, -(p * z), 1.0 - a * a)


def _place():
    x, y, c = lax.axis_index("x"), lax.axis_index("y"), lax.axis_index("c")
    return x, y, c, 4 * x + 2 * y + c


class _Direct:
    def __init__(self, a2a=(), ag=()):
        self.arrays = list(a2a) + list(ag)
        self.n_a, self.n = len(a2a), len(self.arrays)
        self.out_shape = ([jax.ShapeDtypeStruct(a.shape, a.dtype) for a in a2a]
                          + [jax.ShapeDtypeStruct((NDEV,) + a.shape, a.dtype) for a in ag])
        self.scratch = [pltpu.SemaphoreType.DMA((self.n, NDEV - 1)), pltpu.SemaphoreType.DMA((self.n, NDEV - 1)),
                        pltpu.SemaphoreType.DMA((self.n,))]

    def _copies(self, ins, outs, sems):
        send_sems, recv_sems, local_sems = sems
        x, y, c, me = _place()
        local, remote = [], []
        for t in range(self.n):
            src = ins[t].at[me] if t < self.n_a else ins[t]
            local.append(pltpu.make_async_copy(src, outs[t].at[me], local_sems.at[t]))
        for r in range(1, NDEV):
            px = 1 - x if r & 4 else x
            py = 1 - y if r & 2 else y
            pc = 1 - c if r & 1 else c
            for t in range(self.n):
                src = ins[t].at[4 * px + 2 * py + pc] if t < self.n_a else ins[t]
                remote.append(pltpu.make_async_remote_copy(
                    src_ref=src, dst_ref=outs[t].at[me], send_sem=send_sems.at[t, r - 1],
                    recv_sem=recv_sems.at[t, r - 1], device_id=(px, py, pc), device_id_type=MESH))
        return local, remote

    def start(self, ins, outs, sems):
        local, remote = self._copies(ins, outs, sems)
        for cp in local + remote:
            cp.start()

    def finish(self, ins, outs, sems):
        local, remote = self._copies(ins, outs, sems)
        for cp in remote + local:
            cp.wait()


class _AllGather2:
    def __init__(self, arrays):
        self.arrays = list(arrays)
        self.n = len(self.arrays)
        self.out_shape = [jax.ShapeDtypeStruct((NDEV,) + a.shape, a.dtype) for a in self.arrays]
        self.scratch = [pltpu.SemaphoreType.DMA((self.n, NDEV - 1)), pltpu.SemaphoreType.DMA((self.n, NDEV - 1)),
                        pltpu.SemaphoreType.DMA((self.n,))]

    @staticmethod
    def _chips(x, y):
        return [(1 - x, y), (x, 1 - y), (1 - x, 1 - y)]

    def _copy(self, t, k, src, dst, to, sems):
        return pltpu.make_async_remote_copy(src_ref=src, dst_ref=dst, send_sem=sems[0].at[t, k],
                                            recv_sem=sems[1].at[t, k], device_id=to, device_id_type=MESH)

    def start(self, ins, outs, sems):
        x, y, c, me = _place()
        for t in range(self.n):
            pltpu.make_async_copy(ins[t], outs[t].at[me], sems[2].at[t]).start()
            self._copy(t, 0, ins[t], outs[t].at[me], (x, y, 1 - c), sems).start()
            for j, (px, py) in enumerate(self._chips(x, y)):
                self._copy(t, 1 + j, ins[t], outs[t].at[me], (px, py, c), sems).start()

    def finish(self, ins, outs, sems):
        x, y, c, me = _place()
        sib = (x, y, 1 - c)
        for j, (px, py) in enumerate(self._chips(x, y)):
            slot = 4 * px + 2 * py + c
            for t in range(self.n):
                self._copy(t, 1 + j, ins[t], outs[t].at[slot], sib, sems).wait_recv()
                self._copy(t, 4 + j, outs[t].at[slot], outs[t].at[slot], sib, sems).start()
        for t in range(self.n):
            for k in (0, 4, 5, 6):
                self._copy(t, k, ins[t], outs[t].at[me], sib, sems).wait_recv()
        for t in range(self.n):
            for k in range(NDEV - 1):
                self._copy(t, k, ins[t], outs[t].at[me], sib, sems).wait_send()
            pltpu.make_async_copy(ins[t], outs[t].at[me], sems[2].at[t]).wait()


class _AllGatherVia:
    def __init__(self, arrays):
        self.arrays = list(arrays)
        self.n = len(self.arrays)
        self.out_shape = [jax.ShapeDtypeStruct((NDEV,) + a.shape, a.dtype) for a in self.arrays]
        self.scratch = [pltpu.SemaphoreType.DMA((self.n, NDEV - 1)), pltpu.SemaphoreType.DMA((self.n, NDEV - 1)),
                        pltpu.SemaphoreType.DMA((self.n,))]

    def _copy(self, t, k, src, dst, to, sems):
        return pltpu.make_async_remote_copy(src_ref=src, dst_ref=dst, send_sem=sems[0].at[t, k],
                                            recv_sem=sems[1].at[t, k], device_id=to, device_id_type=MESH)

    def start(self, ins, outs, sems):
        x, y, c, me = _place()
        for t in range(self.n):
            pltpu.make_async_copy(ins[t], outs[t].at[me], sems[2].at[t]).start()
            self._copy(t, 0, ins[t], outs[t].at[me], (x, y, 1 - c), sems).start()
            self._copy(t, 1, ins[t], outs[t].at[me], (1 - x, y, c), sems).start()
            self._copy(t, 2, ins[t], outs[t].at[me], (x, 1 - y, c), sems).start()

    def finish(self, ins, outs, sems):
        x, y, c, me = _place()
        sib = (x, y, 1 - c)
        slot_x, slot_y, slot_d = 4 * (1 - x) + 2 * y + c, 4 * x + 2 * (1 - y) + c, 4 * (1 - x) + 2 * (1 - y) + c
        slot_on = c * slot_x + (1 - c) * slot_y
        to_on = (c * x + (1 - c) * (1 - x), c * (1 - y) + (1 - c) * y, c)
        for t in range(self.n):
            self._copy(t, 1, ins[t], outs[t].at[slot_x], sib, sems).wait_recv()
            self._copy(t, 2, ins[t], outs[t].at[slot_y], sib, sems).wait_recv()
            self._copy(t, 3, outs[t].at[slot_on], outs[t].at[slot_on], to_on, sems).start()
            self._copy(t, 4, outs[t].at[slot_x], outs[t].at[slot_x], sib, sems).start()
            self._copy(t, 5, outs[t].at[slot_y], outs[t].at[slot_y], sib, sems).start()
        for t in range(self.n):
            self._copy(t, 3, ins[t], outs[t].at[slot_d], sib, sems).wait_recv()
            self._copy(t, 6, outs[t].at[slot_d], outs[t].at[slot_d], sib, sems).start()
        for t in range(self.n):
            for k in (0, 4, 5, 6):
                self._copy(t, k, ins[t], outs[t].at[me], sib, sems).wait_recv()
        for t in range(self.n):
            for k in range(NDEV - 1):
                self._copy(t, k, ins[t], outs[t].at[me], sib, sems).wait_send()
            pltpu.make_async_copy(ins[t], outs[t].at[me], sems[2].at[t]).wait()


class _Both:
    def __init__(self, *riders):
        self.riders = riders
        self.arrays = [a for r in riders for a in r.arrays]
        self.n = len(self.arrays)
        self.out_shape = [o for r in riders for o in r.out_shape]
        self.scratch = [s for r in riders for s in r.scratch]

    def _parts(self, ins, outs, sems):
        p, q = 0, 0
        for r in self.riders:
            yield r, ins[p:p + r.n], outs[p:p + r.n], sems[q:q + len(r.scratch)]
            p, q = p + r.n, q + len(r.scratch)

    def start(self, ins, outs, sems):
        for r, i, o, s in self._parts(ins, outs, sems):
            r.start(i, o, s)

    def finish(self, ins, outs, sems):
        for r, i, o, s in self._parts(ins, outs, sems):
            r.finish(i, o, s)


def _exchange(name, rider):
    n = rider.n

    def body(*refs):
        rider.start(refs[:n], refs[n:2 * n], refs[2 * n:])
        rider.finish(refs[:n], refs[n:2 * n], refs[2 * n:])

    any_spec = pl.BlockSpec(memory_space=pl.ANY)
    return list(pl.pallas_call(body, name=name, out_shape=rider.out_shape, in_specs=[any_spec] * n,
                               out_specs=[any_spec] * n, scratch_shapes=rider.scratch)(*rider.arrays))


def _pcall(body, *, name, grid, in_specs, out_specs, out_shape, args, scratch_shapes=(), rider=None, aliases=None):
    params = _cparams(*(("arbitrary",) * len(grid)))
    if rider is None:
        res = pl.pallas_call(body, name=name, grid=grid, in_specs=in_specs, out_specs=out_specs,
                             out_shape=out_shape, scratch_shapes=list(scratch_shapes),
                             input_output_aliases=aliases or {}, compiler_params=params)(*args)
        return list(res), []
    n_in, n_out, n_scr, rn = len(in_specs), len(out_specs), len(scratch_shapes), rider.n

    def wrapped(*refs):
        cuts = [n_in, rn, n_out, rn, n_scr]
        parts, p = [], 0
        for n in cuts:
            parts.append(refs[p:p + n])
            p += n
        ins, r_in, outs, r_out, scr = parts
        sems = refs[p:]
        ids = [pl.program_id(a) for a in range(len(grid))]
        first = functools.reduce(jnp.logical_and, [i == 0 for i in ids])
        last = functools.reduce(jnp.logical_and, [i == g - 1 for i, g in zip(ids, grid)])

        @pl.when(first)
        def _():
            rider.start(r_in, r_out, sems)
        body(*ins, *outs, *scr)

        @pl.when(last)
        def _():
            rider.finish(r_in, r_out, sems)

    any_spec = pl.BlockSpec(memory_space=pl.ANY)
    res = pl.pallas_call(
        wrapped, name=name, grid=grid, in_specs=list(in_specs) + [any_spec] * rn,
        out_specs=list(out_specs) + [any_spec] * rn, out_shape=list(out_shape) + rider.out_shape,
        scratch_shapes=list(scratch_shapes) + rider.scratch, compiler_params=params)(*args, *rider.arrays)
    return list(res[:n_out]), list(res[n_out:])


def _mod_cols(c_all, ada_w):
    nc = ada_w.shape[2]

    def body(c_ref, w_ref, o_ref):
        cv = c_ref[...]
        ca = cv * jax.nn.sigmoid(cv)
        for l in range(DEPTH):
            o_ref[:, l * nc:(l + 1) * nc] = jnp.dot(ca, w_ref[l], precision=HIGHEST, preferred_element_type=F32)

    return pl.pallas_call(body, name="mod_cols", out_shape=jax.ShapeDtypeStruct((NDEV, DEPTH * nc), F32),
                          compiler_params=_cparams())(c_all, ada_w)


def _pre_proj(x, P, mod, w_in_l, l, rider=None):
    S = x.shape[0]
    TM = min(512, S)
    NB = w_in_l.shape[2]

    def body(x_ref, g_ref, sc_ref, sh_ref, w_ref, ha_ref, hb_ref, p_ref):
        xv = x_ref[...]
        rstd = lax.rsqrt(jnp.mean(xv * xv, axis=-1, keepdims=True) + NORM_EPS)
        h = ((xv * rstd * g_ref[...]) * (1.0 + sc_ref[...]) + sh_ref[...]).astype(BF16)
        ha_ref[...] = h[:, :GW_SPLIT]
        hb_ref[...] = h[:, GW_SPLIT:]
        for j in range(NDEV):
            p_ref[:, j * NB:(j + 1) * NB] = jnp.dot(h, w_ref[j], preferred_element_type=F32)

    row = pl.BlockSpec((TM, D), lambda i: (i, 0))
    return _pcall(
        body, name="pre_proj", grid=(S // TM,),
        in_specs=[row, _vec(l), _vec(l, 1), _vec(l, 0), _full((NDEV, D, NB))],
        out_specs=[pl.BlockSpec((TM, GW_SPLIT), lambda i: (i, 0)), pl.BlockSpec((TM, D - GW_SPLIT), lambda i: (i, 0)),
                   pl.BlockSpec((TM, NDEV * NB), lambda i: (i, 0))],
        out_shape=[jax.ShapeDtypeStruct((S, GW_SPLIT), BF16), jax.ShapeDtypeStruct((S, D - GW_SPLIT), BF16),
                   jax.ShapeDtypeStruct((S, NDEV * NB), F32)],
        args=(x, P['pre_norm_g'], mod, mod, w_in_l), rider=rider)


def _taps(E):
    return [pltpu.roll(E, CONV_K - 1 - k, axis=0)[CONV_HALO:, :] for k in range(CONV_K - 1)] + [E[CONV_HALO:, :]]


def _conv(E, cw_ref, cb_ref):
    w = cw_ref[...]
    taps = _taps(E)
    acc = cb_ref[...] + taps[0] * w[0:1, :]
    for k in range(1, CONV_K):
        acc = acc + taps[k] * w[k:k + 1, :]
    return acc


CH_R, CH_C = 16, 512


def _chunks(T, fn):
    def step(c, carry):
        rows = pl.ds(pl.multiple_of(c * CH_R, CH_R), CH_R)
        for hf in range(D // CH_C):
            fn(rows, slice(hf * CH_C, (hf + 1) * CH_C), hf)
        return carry
    lax.fori_loop(0, T // CH_R, step, 0)


def _to_scan(ref, rows, hf, val):
    for q in range(CH_C // LANE):
        ref[hf * (CH_C // LANE) + q, rows, :] = val[:, q * LANE:(q + 1) * LANE]


def _scan(sa, sv, carry_ref, out_ref, reverse):
    NC, T, _ = sa.shape
    n8 = T // SUB
    rows = range(SUB - 2, -1, -1) if reverse else range(1, SUB)
    for cb in range(NC):
        r_in = SUB - 1 if reverse else 0
        Ap = sa[cb, pl.ds(r_in, n8, stride=SUB), :]
        Vp = sv[cb, pl.ds(r_in, n8, stride=SUB), :]
        for r in rows:
            Ar = sa[cb, pl.ds(r, n8, stride=SUB), :]
            Vp = sv[cb, pl.ds(r, n8, stride=SUB), :] + Ar * Vp
            Ap = Ar * Ap
            sa[cb, pl.ds(r, n8, stride=SUB), :] = Ap
            sv[cb, pl.ds(r, n8, stride=SUB), :] = Vp
    edge = 0 if reverse else SUB - 1

    def step(k, c):
        r0 = pl.multiple_of((n8 - 1 - k if reverse else k) * SUB, SUB)
        h = jnp.concatenate([sv[cb, pl.ds(r0, SUB), :] + sa[cb, pl.ds(r0, SUB), :] * c[:, cb * LANE:(cb + 1) * LANE]
                             for cb in range(NC)], axis=1)
        out_ref[pl.ds(r0, SUB), :] = h
        return jnp.broadcast_to(h[edge:edge + 1, :], (SUB, D))

    carry_ref[...] = lax.fori_loop(0, n8, step, carry_ref[...])


def _rnn_fwd(proj, P, l, rider=None):
    S = proj.shape[0]
    TB = min(512, S)

    def body(xr_ref, g_ref, cw_ref, cb_ref, wa_ref, ba_ref, wx_ref, bx_ref, lam_ref, hs_ref, y_ref,
             u_ref, r_ref, i_ref, a_ref, m_ref, xbuf, sa, sv, hc):
        @pl.when(pl.program_id(0) == 0)
        def _():
            xbuf[0:CONV_HALO, :] = jnp.zeros((CONV_HALO, D), F32)
            hc[...] = jnp.zeros((SUB, D), F32)
        xbuf[CONV_HALO:, :] = xr_ref[...]
        u = _conv(xbuf[...], cw_ref, cb_ref)
        xbuf[0:CONV_HALO, :] = xbuf[TB:TB + CONV_HALO, :]
        ub = u.astype(BF16)
        zr = jnp.concatenate([jnp.dot(ub[:, h * HD:(h + 1) * HD], wa_ref[h], preferred_element_type=F32)
                              for h in range(NHEAD)], axis=1)
        zi = jnp.concatenate([jnp.dot(ub[:, h * HD:(h + 1) * HD], wx_ref[h], preferred_element_type=F32)
                              for h in range(NHEAD)], axis=1)
        r = _sigmoid(zr + ba_ref[...])
        ig = _sigmoid(zi + bx_ref[...])
        log_a = r * (-LRU_C * jax.nn.softplus(-lam_ref[...]))
        a = jnp.exp(log_a)
        mult = jnp.sqrt(_one_minus_sq(a, log_a))
        v = mult * (ig * u)
        u_ref[...] = u
        r_ref[...] = r
        i_ref[...] = ig
        a_ref[...] = a
        m_ref[...] = mult
        for cb in range(D // LANE):
            sa[cb] = a[:, cb * LANE:(cb + 1) * LANE]
            sv[cb] = v[:, cb * LANE:(cb + 1) * LANE]
        _scan(sa, sv, hc, hs_ref, reverse=False)
        silu, _ = _silu_parts(g_ref[...])
        y_ref[...] = (hs_ref[...] * silu).astype(BF16)

    rowb = pl.BlockSpec((TB, D), lambda i: (i, 0))
    return _pcall(
        body, name="rnn_fwd", grid=(S // TB,),
        in_specs=[rowb, pl.BlockSpec((TB, D), lambda i: (i, 1)), _layer(l, (CONV_K, D)), _vec(l),
                  _layer(l, (NHEAD, HD, HD)), _vec(l), _layer(l, (NHEAD, HD, HD)), _vec(l), _vec(l)],
        out_specs=[rowb] * 7,
        out_shape=[jax.ShapeDtypeStruct((S, D), F32), jax.ShapeDtypeStruct((S, D), BF16)]
        + [jax.ShapeDtypeStruct((S, D), F32)] * 5,
        scratch_shapes=[pltpu.VMEM((TB + CONV_HALO, D), F32), pltpu.VMEM((D // LANE, TB, LANE), F32),
                        pltpu.VMEM((D // LANE, TB, LANE), F32), pltpu.VMEM((SUB, D), F32)],
        args=(proj, proj, P['conv_w'], P['conv_b'], P['gate_a_w'], P['gate_a_b'], P['gate_x_w'], P['gate_x_b'],
              P['lru_lambda']), rider=rider)


def _pooled(ebuf, t0, TB):
    tt = t0 + lax.broadcasted_iota(jnp.int32, (TB, 1), 0)
    pooled, inv = [], []
    for g, win in enumerate(WINS):
        Eg = ebuf[:, g * GD:(g + 1) * GD]
        L = Eg
        for lev in range(g + 1):
            L = L + pltpu.roll(L, 1 << lev, axis=0)
        icnt = 1.0 / jnp.minimum(tt + 1, win).astype(F32)
        pooled.append(L[POOL_HALO:, :] * icnt - Eg[POOL_HALO:, :])
        inv.append(icnt)
    return pooled, inv


def _pool_fwd(proj, P, l, rider=None):
    S = proj.shape[0]
    TB = min(512, S)

    def body(xp_ref, g_ref, pw_ref, pb_ref, ps_ref, y_ref, ebuf):
        i = pl.program_id(0)

        @pl.when(i == 0)
        def _():
            ebuf[0:POOL_HALO, :] = jnp.zeros((POOL_HALO, D), F32)
        ebuf[POOL_HALO:, :] = xp_ref[...]
        pooled, _ = _pooled(ebuf, i * TB, TB)
        ebuf[0:POOL_HALO, :] = ebuf[TB:TB + POOL_HALO, :]
        yp = jnp.concatenate([jnp.dot(pooled[g].astype(BF16), pw_ref[g], preferred_element_type=F32)
                              for g in range(NGRP)], axis=1) + pb_ref[...]
        silu, _ = _silu_parts(g_ref[...])
        y_ref[...] = (yp * ps_ref[...] * silu).astype(BF16)

    return _pcall(
        body, name="pool_fwd", grid=(S // TB,),
        in_specs=[pl.BlockSpec((TB, D), lambda i: (i, 2)), pl.BlockSpec((TB, D), lambda i: (i, 3)),
                  _layer(l, (NGRP, GD, GD)), _vec(l), _vec(l)],
        out_specs=[pl.BlockSpec((TB, D), lambda i: (i, 0))],
        out_shape=[jax.ShapeDtypeStruct((S, D), BF16)],
        scratch_shapes=[pltpu.VMEM((TB + POOL_HALO, D), F32)],
        args=(proj, proj, P['pool_w'], P['pool_b'], P['pool_scale']), rider=rider)


def _out_post(yr, yp, w_out_l, x, mod, P, l, target=None, rider=None):
    S = x.shape[0]
    TM = min(512, S)
    last = target is not None

    def body(*refs):
        if last:
            yr_ref, yp_ref, w_ref, x_ref, gate_ref, gp_ref, t_ref, y_ref, xo_ref, loss_ref = refs
        else:
            yr_ref, yp_ref, w_ref, x_ref, gate_ref, gp_ref, y_ref, xo_ref = refs
        acc = (jnp.dot(yr_ref[...], w_ref[0:D, :], preferred_element_type=F32)
               + jnp.dot(yp_ref[...], w_ref[D:2 * D, :], preferred_element_type=F32))
        y_ref[...] = acc
        rstd = lax.rsqrt(jnp.mean(acc * acc, axis=-1, keepdims=True) + NORM_EPS)
        xn = x_ref[...] + gate_ref[...] * (acc * rstd * gp_ref[...])
        if last:
            err = xn - t_ref[...]
            xo_ref[...] = err * (1.0 / D)

            @pl.when(pl.program_id(0) == 0)
            def _():
                loss_ref[...] = jnp.zeros((SUB, D), F32)
            loss_ref[...] += _rowsum8(err * err)
        else:
            xo_ref[...] = xn

    row = pl.BlockSpec((TM, D), lambda i: (i, 0))
    in_specs = [row, row, _full((2 * D, D)), row, _vec(l, 2), _vec(l)]
    out_specs = [row, row]
    out_shape = [jax.ShapeDtypeStruct((S, D), F32), jax.ShapeDtypeStruct((S, D), F32)]
    args = [yr, yp, w_out_l, x, mod, P['post_norm_g']]
    if last:
        in_specs.append(row)
        out_specs.append(_full((SUB, D)))
        out_shape.append(jax.ShapeDtypeStruct((SUB, D), F32))
        args.append(target)
    return _pcall(body, name="out_post_loss" if last else "out_post", grid=(S // TM,), in_specs=in_specs,
                  out_specs=out_specs, out_shape=out_shape, args=args, rider=rider)


def _out_bwd(dxo, y, yr, yp, w_out_l, mod, P, l, rider=None):
    S = y.shape[0]
    TM = min(512, S)
    nsteps = S // TM

    def body(dxo_ref, y_ref, yr_ref, yp_ref, w_ref, gate_ref, gp_ref, dyr_ref, dyp_ref, gw_ref, dgate_ref,
             dgp_ref, gw_acc, vacc):
        i = pl.program_id(0)

        @pl.when(i == 0)
        def _():
            gw_acc[...] = jnp.zeros_like(gw_acc)
            vacc[...] = jnp.zeros_like(vacc)
        yv = y_ref[...]
        dxo_v = dxo_ref[...]
        rstd = lax.rsqrt(jnp.mean(yv * yv, axis=-1, keepdims=True) + NORM_EPS)
        n = yv * rstd
        gp = gp_ref[...]
        vacc[0] += _rowsum8(dxo_v * (n * gp))
        drn = dxo_v * gate_ref[...]
        vacc[1] += _rowsum8(drn * n)
        dn = drn * gp
        dy = (rstd * (dn - n * jnp.mean(dn * n, axis=-1, keepdims=True))).astype(BF16)
        dyr_ref[...] = lax.dot_general(dy, w_ref[0:D, :], (((1,), (1,)), ((), ())), preferred_element_type=F32)
        dyp_ref[...] = lax.dot_general(dy, w_ref[D:2 * D, :], (((1,), (1,)), ((), ())),
                                       preferred_element_type=F32)
        gw_acc[0:D, :] += lax.dot_general(yr_ref[...], dy, (((0,), (0,)), ((), ())), preferred_element_type=F32)
        gw_acc[D:2 * D, :] += lax.dot_general(yp_ref[...], dy, (((0,), (0,)), ((), ())),
                                              preferred_element_type=F32)

        @pl.when(i == nsteps - 1)
        def _():
            gw_ref[...] = gw_acc[...].astype(BF16)
            dgate_ref[...] = _sum8(vacc[0])
            dgp_ref[...] = _sum8(vacc[1])

    row = pl.BlockSpec((TM, D), lambda i: (i, 0))
    return _pcall(
        body, name="out_bwd", grid=(nsteps,),
        in_specs=[row, row, row, row, _full((2 * D, D)), _vec(l, 2), _vec(l)],
        out_specs=[row, row, _full((2 * D, D)), _full((1, D)), _full((1, D))],
        out_shape=[jax.ShapeDtypeStruct((S, D), F32), jax.ShapeDtypeStruct((S, D), F32),
                   jax.ShapeDtypeStruct((2 * D, D), BF16), jax.ShapeDtypeStruct((1, D), F32),
                   jax.ShapeDtypeStruct((1, D), F32)],
        scratch_shapes=[pltpu.VMEM((2 * D, D), F32), pltpu.VMEM((2, SUB, D), F32)],
        args=(dxo, y, yr, yp, w_out_l, mod, P['post_norm_g']), rider=rider)


def _rnn_bwd(dyr, hs, fwd, proj, P, l, rider=None):
    S = proj.shape[0]
    TB = min(256, S)
    nb = S // TB
    TE = TB + CONV_HALO
    A_BA, A_BX, A_LAM, A_CB, A_CW = 0, 1, 2, 3, 4

    def blk(i):
        return nb - 1 - i

    def body(dyr_ref, hs_ref, hprev_ref, u_ref, r_ref, i_ref, a_ref, m_ref, xr_ref, g_ref, cw_ref,
             wa_ref, wx_ref, lam_ref, dxr_ref, dg_ref, gcw_ref, gcb_ref, gwa_ref, gba_ref, gwx_ref, gbx_ref,
             glam_ref, hbuf, abuf, dbuf, sa, sv, dh_ref, hp_ref, dzr_ref, dzi_ref, dhc, vacc):
        i = pl.program_id(0)
        first = blk(i) == 0

        @pl.when(i == 0)
        def _():
            abuf[TB:, :] = jnp.zeros((CONV_HALO, D), F32)
            dbuf[TB:, :] = jnp.zeros((CONV_HALO, D), F32)
            dhc[...] = jnp.zeros_like(dhc)
            vacc[...] = jnp.zeros_like(vacc)
            gwa_ref[...] = jnp.zeros_like(gwa_ref)
            gwx_ref[...] = jnp.zeros_like(gwx_ref)

        hbuf[0:CONV_HALO, :] = jnp.where(first, 0.0, hprev_ref[...])
        hbuf[CONV_HALO:, :] = hs_ref[...]
        hp_ref[...] = pltpu.roll(hbuf[...], 1, axis=0)[CONV_HALO:, :]
        abuf[0:TB, :] = a_ref[...]
        b = pltpu.roll(abuf[...], TE - 1, axis=0)[0:TB, :]
        for cb in range(D // LANE):
            sa[cb] = b[:, cb * LANE:(cb + 1) * LANE]
        abuf[TB:, :] = a_ref[0:CONV_HALO, :]

        def gate_bwd(rows, cs, hf):
            silu, dsilu = _silu_parts(g_ref[rows, cs])
            dy = dyr_ref[rows, cs]
            dg_ref[rows, cs] = (dy * hs_ref[rows, cs] * dsilu).astype(BF16)
            _to_scan(sv, rows, hf, dy * silu)
        _chunks(TB, gate_bwd)

        _scan(sa, sv, dhc, dh_ref, reverse=True)

        csp = -LRU_C * jax.nn.softplus(-lam_ref[...])

        def lru_bwd(rows, cs, hf):
            dh, a, ig, u, mult, r = dh_ref[rows, cs], a_ref[rows, cs], i_ref[rows, cs], u_ref[rows, cs], \
                m_ref[rows, cs], r_ref[rows, cs]
            dlog_a = dh * hp_ref[rows, cs] * a - (dh * ig * u) * (a * a) / mult
            dzr = dlog_a * csp[:, cs] * r * (1.0 - r)
            dzi = (dh * mult * u) * ig * (1.0 - ig)
            dzr_ref[rows, cs] = dzr.astype(BF16)
            dzi_ref[rows, cs] = dzi.astype(BF16)
            dbuf[rows, cs] = dh * mult * ig
            vacc[A_LAM, :, cs] += _rowsum8(dlog_a * r)
            vacc[A_BA, :, cs] += _rowsum8(dzr)
            vacc[A_BX, :, cs] += _rowsum8(dzi)
        _chunks(TB, lru_bwd)

        ub, dzrb, dzib = u_ref[...].astype(BF16), dzr_ref[...], dzi_ref[...]
        du_g = []
        for h in range(NHEAD):
            cs = slice(h * HD, (h + 1) * HD)
            gwa_ref[h] += lax.dot_general(ub[:, cs], dzrb[:, cs], (((0,), (0,)), ((), ())),
                                          preferred_element_type=F32)
            gwx_ref[h] += lax.dot_general(ub[:, cs], dzib[:, cs], (((0,), (0,)), ((), ())),
                                          preferred_element_type=F32)
            du_g.append(lax.dot_general(dzrb[:, cs], wa_ref[h], (((1,), (1,)), ((), ())),
                                        preferred_element_type=F32)
                        + lax.dot_general(dzib[:, cs], wx_ref[h], (((1,), (1,)), ((), ())),
                                          preferred_element_type=F32))
        du = dbuf[0:TB, :] + jnp.concatenate(du_g, axis=1)
        dbuf[0:TB, :] = du

        Dd = dbuf[...]
        w = cw_ref[...]
        xv = xr_ref[...]
        dx = du * w[CONV_K - 1:CONV_K, :]
        vacc[A_CB] += _rowsum8(du)
        vacc[A_CW + CONV_K - 1] += _rowsum8(xv * du)
        for k in range(CONV_K - 1):
            ahead = pltpu.roll(Dd, TE - (CONV_K - 1 - k), axis=0)[0:TB, :]
            dx = dx + ahead * w[k:k + 1, :]
            vacc[A_CW + k] += _rowsum8(xv * ahead)
        dxr_ref[...] = dx.astype(BF16)
        dbuf[TB:, :] = du[0:CONV_HALO, :]

        @pl.when(i == nb - 1)
        def _():
            gba_ref[...] = _sum8(vacc[A_BA])
            gbx_ref[...] = _sum8(vacc[A_BX])
            glam_ref[...] = _sum8(vacc[A_LAM]) * (LRU_C * _sigmoid(-lam_ref[...]))
            gcb_ref[...] = _sum8(vacc[A_CB])
            for k in range(CONV_K):
                gcw_ref[k:k + 1, :] = _sum8(vacc[A_CW + k])

    rowb = pl.BlockSpec((TB, D), lambda i: (blk(i), 0))
    halo = pl.BlockSpec((SUB, D), lambda i: (jnp.maximum(blk(i) * (TB // SUB) - 1, 0), 0))
    wspec = _full((NHEAD, HD, HD))
    wlay = _layer(l, (NHEAD, HD, HD))
    vec1 = _full((1, D))
    return _pcall(
        body, name="rnn_bwd", grid=(nb,),
        in_specs=[rowb, rowb, halo] + [rowb] * 5 + [rowb, pl.BlockSpec((TB, D), lambda i: (blk(i), 1)),
                                                    _layer(l, (CONV_K, D)), wlay, wlay, _vec(l)],
        out_specs=[rowb, rowb, _full((CONV_K, D)), vec1, wspec, vec1, wspec, vec1, vec1],
        out_shape=[jax.ShapeDtypeStruct((S, D), BF16), jax.ShapeDtypeStruct((S, D), BF16),
                   jax.ShapeDtypeStruct((CONV_K, D), F32), jax.ShapeDtypeStruct((1, D), F32),
                   jax.ShapeDtypeStruct((NHEAD, HD, HD), F32), jax.ShapeDtypeStruct((1, D), F32),
                   jax.ShapeDtypeStruct((NHEAD, HD, HD), F32), jax.ShapeDtypeStruct((1, D), F32),
                   jax.ShapeDtypeStruct((1, D), F32)],
        scratch_shapes=[pltpu.VMEM((TE, D), F32), pltpu.VMEM((TE, D), F32), pltpu.VMEM((TE, D), F32),
                        pltpu.VMEM((D // LANE, TB, LANE), F32), pltpu.VMEM((D // LANE, TB, LANE), F32),
                        pltpu.VMEM((TB, D), F32), pltpu.VMEM((TB, D), F32), pltpu.VMEM((TB, D), BF16),
                        pltpu.VMEM((TB, D), BF16), pltpu.VMEM((SUB, D), F32),
                        pltpu.VMEM((A_CW + CONV_K, SUB, D), F32)],
        args=(dyr, hs, hs, *fwd, proj, proj, P['conv_w'], P['gate_a_w'], P['gate_x_w'], P['lru_lambda']),
        rider=rider)


def _pool_bwd(dyp, proj, P, l, rider=None):
    S = proj.shape[0]
    TB = min(512, S)
    nb = S // TB
    TE = TB + POOL_HALO

    def blk(i):
        return nb - 1 - i

    def body(dy_ref, xp_ref, xprev_ref, g_ref, pw_ref, pb_ref, ps_ref, dxp_ref, dg_ref, gpw_ref, gpb_ref,
             gps_ref, ebuf, qbuf, vacc):
        i = pl.program_id(0)
        first = blk(i) == 0

        @pl.when(i == 0)
        def _():
            qbuf[TB:, :] = jnp.zeros((POOL_HALO, D), F32)
            vacc[...] = jnp.zeros_like(vacc)
            gpw_ref[...] = jnp.zeros_like(gpw_ref)

        ebuf[0:POOL_HALO, :] = jnp.where(first, 0.0, xprev_ref[...])
        ebuf[POOL_HALO:, :] = xp_ref[...]
        pooled, inv = _pooled(ebuf, blk(i) * TB, TB)
        pooled = [p.astype(BF16) for p in pooled]
        yp = jnp.concatenate([jnp.dot(pooled[g], pw_ref[g], preferred_element_type=F32)
                              for g in range(NGRP)], axis=1) + pb_ref[...]
        silu, dsilu = _silu_parts(g_ref[...])
        dy = dy_ref[...]
        ps = ps_ref[...]
        dyp_v = dy * ps * silu
        vacc[0] += _rowsum8(dy * yp * silu)
        vacc[1] += _rowsum8(dyp_v)
        dg_ref[...] = (dy * yp * ps * dsilu).astype(BF16)
        dypb = dyp_v.astype(BF16)
        for g in range(NGRP):
            cs = slice(g * GD, (g + 1) * GD)
            gpw_ref[g] += lax.dot_general(pooled[g], dypb[:, cs], (((0,), (0,)), ((), ())),
                                          preferred_element_type=F32)
            dpool = lax.dot_general(dypb[:, cs], pw_ref[g], (((1,), (1,)), ((), ())),
                                    preferred_element_type=F32)
            qbuf[0:TB, cs] = dpool * inv[g]
            L = qbuf[:, cs]
            for lev in range(g + 1):
                L = L + pltpu.roll(L, TE - (1 << lev), axis=0)
            dxp_ref[:, cs] = (L[0:TB, :] - dpool).astype(BF16)
        qbuf[TB:, :] = qbuf[0:POOL_HALO, :]

        @pl.when(i == nb - 1)
        def _():
            gps_ref[...] = _sum8(vacc[0])
            gpb_ref[...] = _sum8(vacc[1])

    rowb = pl.BlockSpec((TB, D), lambda i: (blk(i), 0))
    return _pcall(
        body, name="pool_bwd", grid=(nb,),
        in_specs=[rowb, pl.BlockSpec((TB, D), lambda i: (blk(i), 2)),
                  pl.BlockSpec((POOL_HALO, D), lambda i: (jnp.maximum(blk(i) * (TB // POOL_HALO) - 1, 0), 2)),
                  pl.BlockSpec((TB, D), lambda i: (blk(i), 3)), _layer(l, (NGRP, GD, GD)), _vec(l), _vec(l)],
        out_specs=[rowb, rowb, _full((NGRP, GD, GD)), _full((1, D)), _full((1, D))],
        out_shape=[jax.ShapeDtypeStruct((S, D), BF16), jax.ShapeDtypeStruct((S, D), BF16),
                   jax.ShapeDtypeStruct((NGRP, GD, GD), F32), jax.ShapeDtypeStruct((1, D), F32),
                   jax.ShapeDtypeStruct((1, D), F32)],
        scratch_shapes=[pltpu.VMEM((TE, D), F32), pltpu.VMEM((TE, D), F32), pltpu.VMEM((2, SUB, D), F32)],
        args=(dyp, proj, proj, proj, P['pool_w'], P['pool_b'], P['pool_scale']), rider=rider)


def _in_bwd(dq, w_in_l, x, dxo, P, mod, l, rider=None):
    S = x.shape[0]
    TM = min(256, S)
    NB = w_in_l.shape[2]
    nsteps = S // TM
    per_q = D // NB

    def body(d0, d1, d2, d3, w_ref, x_ref, dxo_ref, g_ref, sc_ref, dx_ref, dsh_ref, dsc_ref, dg_ref, vacc):
        i = pl.program_id(0)

        @pl.when(i == 0)
        def _():
            vacc[...] = jnp.zeros_like(vacc)
        dref = (d0, d1, d2, d3)
        dh = jnp.zeros((TM, D), F32)
        for j in range(NDEV):
            c0 = (j % per_q) * NB
            dh = dh + lax.dot_general(dref[j // per_q][:, c0:c0 + NB], w_ref[j], (((1,), (1,)), ((), ())),
                                      preferred_element_type=F32)
        xv = x_ref[...]
        rstd = lax.rsqrt(jnp.mean(xv * xv, axis=-1, keepdims=True) + NORM_EPS)
        xn = xv * rstd
        g, sc = g_ref[...], 1.0 + sc_ref[...]
        vacc[0] += _rowsum8(dh)
        vacc[1] += _rowsum8(dh * (xn * g))
        vacc[2] += _rowsum8(dh * sc * xn)
        dxn = dh * sc * g
        dx_ref[...] = dxo_ref[...] + rstd * (dxn - xn * jnp.mean(dxn * xn, axis=-1, keepdims=True))

        @pl.when(i == nsteps - 1)
        def _():
            dsh_ref[...] = _sum8(vacc[0])
            dsc_ref[...] = _sum8(vacc[1])
            dg_ref[...] = _sum8(vacc[2])

    row = pl.BlockSpec((TM, D), lambda i: (i, 0))
    return _pcall(
        body, name="in_bwd", grid=(nsteps,),
        in_specs=[row, row, row, row, _full((NDEV, D, NB)), row, row, _vec(l), _vec(l, 1)],
        out_specs=[row, _full((1, D)), _full((1, D)), _full((1, D))],
        out_shape=[jax.ShapeDtypeStruct((S, D), F32)] + [jax.ShapeDtypeStruct((1, D), F32)] * 3,
        scratch_shapes=[pltpu.VMEM((3, SUB, D), F32)],
        args=(*dq, w_in_l, x, dxo, P['pre_norm_g'], mod), rider=rider)


def _grad_w_in(h, dq, NB, rider=None):
    S, RH = h.shape
    TK = min(1024, S)
    nk = S // TK

    def body(h_ref, d0, d1, d2, d3, o_ref, acc):
        k = pl.program_id(0)

        @pl.when(k == 0)
        def _():
            acc[...] = jnp.zeros_like(acc)
        hv = h_ref[...]
        for q, d_ref in enumerate((d0, d1, d2, d3)):
            acc[:, q * D:(q + 1) * D] += lax.dot_general(hv, d_ref[...], (((0,), (0,)), ((), ())),
                                                         preferred_element_type=F32)

        @pl.when(k == nk - 1)
        def _():
            for j in range(NDEV):
                o_ref[j] = acc[:, j * NB:(j + 1) * NB].astype(BF16)

    row = pl.BlockSpec((TK, D), lambda k: (k, 0))
    return _pcall(
        body, name="grad_w_in", grid=(nk,),
        in_specs=[pl.BlockSpec((TK, RH), lambda k: (k, 0)), row, row, row, row],
        out_specs=[_full((NDEV, RH, NB))],
        out_shape=[jax.ShapeDtypeStruct((NDEV, RH, NB), BF16)],
        scratch_shapes=[pltpu.VMEM((RH, NQ * D), F32)],
        args=(h, *dq), rider=rider)


def _adamw_math(g, w, m, v):
    m2 = ADAM_B1 * m + (1.0 - ADAM_B1) * g
    v2 = ADAM_B2 * v + (1.0 - ADAM_B2) * (g * g)
    m_hat = m2 / (1.0 - ADAM_B1 ** ADAM_STEP)
    v_hat = v2 / (1.0 - ADAM_B2 ** ADAM_STEP)
    delta = -ADAM_LR * (m_hat / (jnp.sqrt(v_hat) + ADAM_EPS) + ADAM_WD * w)
    return delta, m2, v2


def _adamw(name, gs, w, m, v, TR, row0=0, into=None):
    L = len(gs)
    n, Rp, C = gs[0].shape
    R = w.shape[1]
    b0 = row0 // TR

    def body(*refs):
        g_refs = refs[:L]
        w_ref, m_ref, v_ref = refs[L:L + 3]
        go_ref, do_ref, mo_ref, vo_ref = refs[-4:]
        lay = pl.program_id(0)
        for li in range(L):
            @pl.when(lay == li)
            def _(li=li):
                g = g_refs[li][0].astype(F32)
                for s in range(1, n):
                    g = g + g_refs[li][s].astype(F32)
                delta, m2, v2 = _adamw_math(g, w_ref[...], m_ref[...], v_ref[...])
                go_ref[...] = g
                do_ref[...] = delta
                mo_ref[...] = m2
                vo_ref[...] = v2

    lrc = pl.BlockSpec((None, TR, C), lambda lay, r: (lay, r + b0, 0))
    g_specs = [pl.BlockSpec((n, TR, C), lambda lay, r, li=li: (0, jnp.where(lay == li, r, 0), 0))
               for li in range(L)]
    in_specs, args, aliases = g_specs + [lrc, lrc, lrc], [*gs, w, m, v], None
    if into is not None:
        aliases = {len(args) + k: k for k in range(4)}
        in_specs = in_specs + [pl.BlockSpec(memory_space=pl.ANY)] * 4
        args = args + list(into)
    return _pcall(
        body, name=name, grid=(L, Rp // TR), in_specs=in_specs, out_specs=[lrc] * 4,
        out_shape=[jax.ShapeDtypeStruct((L, R, C), F32)] * 4, args=args, aliases=aliases)


def _ada_adamw(c_all_t, dm, w, m, v, rider=None):
    L, _, nc = w.shape

    def body(c_ref, dm_ref, w_ref, m_ref, v_ref, go_ref, do_ref, mo_ref, vo_ref):
        cv = c_ref[...]
        ca = cv * jax.nn.sigmoid(cv)
        dmv = dm_ref[...]
        g = ca[:, 0:1] * dmv[0:1, :]
        for b in range(1, NDEV):
            g = g + ca[:, b:b + 1] * dmv[b:b + 1, :]
        delta, m2, v2 = _adamw_math(g, w_ref[...], m_ref[...], v_ref[...])
        go_ref[...] = g
        do_ref[...] = delta
        mo_ref[...] = m2
        vo_ref[...] = v2

    big = pl.BlockSpec((None, D, nc), lambda lay: (lay, 0, 0))
    return _pcall(
        body, name="ada_adamw", grid=(L,),
        in_specs=[_full((D, NDEV)), pl.BlockSpec((None, NDEV, nc), lambda lay: (lay, 0, 0)), big, big, big],
        out_specs=[big] * 4, out_shape=[jax.ShapeDtypeStruct((L, D, nc), F32)] * 4,
        args=(c_all_t, dm, w, m, v), rider=rider)


def _sum_slots(recv):
    n, R, C = recv.shape

    def body(r_ref, o_ref):
        acc = r_ref[0].astype(F32)
        for s in range(1, n):
            acc = acc + r_ref[s].astype(F32)
        o_ref[...] = acc

    return pl.pallas_call(body, name="sum_slots", out_shape=jax.ShapeDtypeStruct((R, C), F32),
                          compiler_params=_cparams())(recv)


def _pad_rows(a, rows):
    return jnp.pad(a, ((0, rows - a.shape[0]), (0, 0)))


def _pack_sharded_block(pool_w, pool_b, conv_w):
    return jnp.concatenate([pool_w.reshape(-1, PACK_C), _pad_rows(pool_b.reshape(-1, PACK_C), SUB),
                            _pad_rows(conv_w.reshape(-1, PACK_C), SUB)], axis=0)


def _unpack_sharded_block(p):
    n_pw = DEPTH * NGRP * (GD // NDEV)
    pool_w = p[:n_pw].reshape(DEPTH, NGRP, GD // NDEV, GD)
    pool_b = p[n_pw].reshape(DEPTH, NGRP, GD // NDEV)
    conv_w = p[n_pw + SUB:n_pw + SUB + DEPTH * CONV_K * (D // NDEV) // PACK_C].reshape(DEPTH, CONV_K, D // NDEV)
    return pool_w, pool_b, conv_w


def _blocks_of_full(pool_w, pool_b, conv_w):
    pw = pool_w.reshape(DEPTH, NGRP, NDEV, GD // NDEV, GD).transpose(2, 0, 1, 3, 4).reshape(NDEV, -1, PACK_C)
    pb = pool_b.reshape(DEPTH, NGRP, NDEV, GD // NDEV).transpose(2, 0, 1, 3).reshape(NDEV, -1, PACK_C)
    cw = conv_w.reshape(DEPTH, CONV_K, NDEV, D // NDEV).transpose(2, 0, 1, 3).reshape(NDEV, -1, PACK_C)
    pad = lambda a: jnp.pad(a, ((0, 0), (0, SUB - a.shape[1]), (0, 0)))
    return jnp.concatenate([pw, pad(pb), pad(cw)], axis=1)


def _full_of_blocks(p):
    n_pw = DEPTH * NGRP * (GD // NDEV)
    pool_w = p[:, :n_pw].reshape(NDEV, DEPTH, NGRP, GD // NDEV, GD).transpose(1, 2, 0, 3, 4)
    pool_b = p[:, n_pw].reshape(NDEV, DEPTH, NGRP, GD // NDEV).transpose(1, 2, 0, 3)
    n_cw = DEPTH * CONV_K * (D // NDEV) // PACK_C
    conv_w = p[:, n_pw + SUB:n_pw + SUB + n_cw].reshape(NDEV, DEPTH, CONV_K, D // NDEV).transpose(1, 2, 0, 3)
    return (pool_w.reshape(DEPTH, NGRP, GD, GD), pool_b.reshape(DEPTH, NGRP, GD),
            conv_w.reshape(DEPTH, CONV_K, D))


def _pack_replicated(t, keys, rows):
    p = jnp.concatenate([t[k].reshape(-1, PACK_C) for k in keys], axis=0)
    return _pad_rows(p, rows)


def _unpack_replicated(p, like, keys):
    out, r0 = {}, 0
    for k in keys:
        rows = like[k].size // PACK_C
        out[k] = p[r0:r0 + rows].reshape(like[k].shape)
        r0 += rows
    return out


def kernel(x, c, ada_w, ada_b, pre_norm_g, w_in, conv_w, conv_b, gate_a_w, gate_a_b, gate_x_w, gate_x_b, lru_lambda, pool_w, pool_b, pool_scale, w_out, post_norm_g, loss_target, m_ada_w, m_ada_b, m_pre_norm_g, m_w_in, m_conv_w, m_conv_b, m_gate_a_w, m_gate_a_b, m_gate_x_w, m_gate_x_b, m_lru_lambda, m_pool_w, m_pool_b, m_pool_scale, m_w_out, m_post_norm_g, v_ada_w, v_ada_b, v_pre_norm_g, v_w_in, v_conv_w, v_conv_b, v_gate_a_w, v_gate_a_b, v_gate_x_w, v_gate_x_b, v_lru_lambda, v_pool_w, v_pool_b, v_pool_scale, v_w_out, v_post_norm_g):
    W = dict(ada_w=ada_w, ada_b=ada_b, pre_norm_g=pre_norm_g, w_in=w_in, conv_w=conv_w, conv_b=conv_b,
             gate_a_w=gate_a_w, gate_a_b=gate_a_b, gate_x_w=gate_x_w, gate_x_b=gate_x_b, lru_lambda=lru_lambda,
             pool_w=pool_w, pool_b=pool_b, pool_scale=pool_scale, w_out=w_out, post_norm_g=post_norm_g)
    M = dict(ada_w=m_ada_w, ada_b=m_ada_b, pre_norm_g=m_pre_norm_g, w_in=m_w_in, conv_w=m_conv_w,
             conv_b=m_conv_b, gate_a_w=m_gate_a_w, gate_a_b=m_gate_a_b, gate_x_w=m_gate_x_w,
             gate_x_b=m_gate_x_b, lru_lambda=m_lru_lambda, pool_w=m_pool_w, pool_b=m_pool_b,
             pool_scale=m_pool_scale, w_out=m_w_out, post_norm_g=m_post_norm_g)
    V = dict(ada_w=v_ada_w, ada_b=v_ada_b, pre_norm_g=v_pre_norm_g, w_in=v_w_in, conv_w=v_conv_w,
             conv_b=v_conv_b, gate_a_w=v_gate_a_w, gate_a_b=v_gate_a_b, gate_x_w=v_gate_x_w,
             gate_x_b=v_gate_x_b, lru_lambda=v_lru_lambda, pool_w=v_pool_w, pool_b=v_pool_b,
             pool_scale=v_pool_scale, w_out=v_w_out, post_norm_g=v_post_norm_g)
    S = x.shape[1]
    me = 4 * lax.axis_index("x") + 2 * lax.axis_index("y") + lax.axis_index("c")
    xs = x.reshape(S, D)
    tgt = loss_target.reshape(S, D)
    nc = ada_w.shape[2]
    NB = w_in.shape[2]
    w_in_b, w_out_b = w_in.astype(BF16), w_out.astype(BF16)
    rows = lambda a: a.reshape(DEPTH, 1, D)
    P = dict(pre_norm_g=rows(pre_norm_g), conv_b=rows(conv_b), gate_a_b=rows(gate_a_b), gate_x_b=rows(gate_x_b),
             lru_lambda=rows(lru_lambda), pool_scale=rows(pool_scale), post_norm_g=rows(post_norm_g),
             gate_a_w=gate_a_w.astype(BF16), gate_x_w=gate_x_w.astype(BF16))

    c_slots, w_in0 = _exchange("gather_c_w_in0", _AllGatherVia([jnp.broadcast_to(c, (SUB, D)), w_in_b[0]]))
    c_all = c_slots[:, 0, :]
    (mod_slots,) = _exchange("gather_mod", _Direct(ag=[_mod_cols(c_all, ada_w)]))
    mod = lax.dynamic_index_in_dim(mod_slots, me, axis=1, keepdims=False)
    mod = (mod.reshape(NDEV, DEPTH, nc).transpose(1, 0, 2).reshape(DEPTH, 3 * D) + ada_b).reshape(DEPTH, 3, 1, D)

    w_in_all, w_out_all = [w_in0, None], [None, None]
    saved = []
    xl = xs
    flat = lambda w_slots: w_slots.reshape(2 * D, D)
    for l in range(DEPTH):
        rider = _AllGather2([_pack_sharded_block(pool_w, pool_b, conv_w), w_out_b[0]]) if l == 0 else None
        (h_a, h_b, proj), got = _pre_proj(xl, P, mod, w_in_all[l], l, rider=rider)
        if l == 0:
            pool_w_f, pool_b_f, conv_w_f = _full_of_blocks(got[0])
            P.update(conv_w=conv_w_f, pool_w=pool_w_f.astype(BF16), pool_b=rows(pool_b_f))
            w_out_all[0] = flat(got[1])
        rider = _AllGather2([w_in_b[1]]) if l == 0 else None
        (hs, yr, *fwd), got = _rnn_fwd(proj, P, l, rider=rider)
        if l == 0:
            w_in_all[1] = got[0]
        (yp,), _ = _pool_fwd(proj, P, l)
        if l == DEPTH - 1:
            (y, x_next, loss_acc), _ = _out_post(yr, yp, w_out_all[l], xl, mod, P, l, tgt)
        else:
            (y, x_next), got = _out_post(yr, yp, w_out_all[l], xl, mod, P, l, rider=_AllGather2([w_out_b[1]]))
            w_out_all[1] = flat(got[0])
        saved.append((xl, h_a, h_b, proj, hs, fwd, yr, yp, y))
        xl = x_next

    dxo = xl
    G = {k: [None] * DEPTH for k in WEIGHTS}
    dmod = [None] * DEPTH
    recv_in, recv_out = [[None, None] for _ in range(DEPTH)], [None] * DEPTH
    full = dict(conv_w=(CONV_K, D), pool_w=(NGRP, GD, GD), pool_b=(NGRP, GD))
    stack = lambda k: jnp.stack([g.reshape(full.get(k, W[k].shape[1:])) for g in G[k]])
    gw_bot_prev = None
    for l in reversed(range(DEPTH)):
        xin, h_a, h_b, proj, hs, fwd, yr, yp, y = saved[l]
        rider = _AllGather2([loss_acc]) if gw_bot_prev is None else _Direct(a2a=[gw_bot_prev])
        (dyr, dyp, gw_out, dgate, G['post_norm_g'][l]), got = _out_bwd(dxo, y, yr, yp, w_out_all[l], mod, P, l,
                                                                       rider=rider)
        if gw_bot_prev is None:
            loss = (0.5 / D) * jnp.sum(got[0])
        else:
            recv_in[l + 1][1] = got[0]
        ((dxr, dgr, G['conv_w'][l], G['conv_b'][l], G['gate_a_w'][l], G['gate_a_b'][l], G['gate_x_w'][l],
          G['gate_x_b'][l], G['lru_lambda'][l]), (recv_out[l],)) = _rnn_bwd(
            dyr, hs, fwd, proj, P, l,
            rider=_Direct(a2a=[gw_out.reshape(NDEV, 2 * D // NDEV, D)]))
        (dxp, dgp, G['pool_w'][l], G['pool_b'][l], G['pool_scale'][l]), _ = _pool_bwd(dyp, proj, P, l)
        dq = (dxr, dgr, dxp, dgp)
        if l > 0:
            (gw_top,), _ = _grad_w_in(h_a, dq, NB)
            (dxo, dshift, dscale, G['pre_norm_g'][l]), (recv_in[l][0],) = _in_bwd(
                dq, w_in_all[l], xin, dxo, P, mod, l, rider=_Direct(a2a=[gw_top]))
            (gw_bot_prev,), _ = _grad_w_in(h_b, dq, NB)
        else:
            Ge = {k: stack(k) for k in REP_EARLY + ['pool_w', 'pool_b', 'conv_w']}
            early = jnp.concatenate([_blocks_of_full(Ge['pool_w'], Ge['pool_b'], Ge['conv_w']),
                                     _pack_replicated(Ge, REP_EARLY, NDEV * PACK_ROWS).reshape(NDEV, PACK_ROWS, PACK_C)],
                                    axis=1).astype(BF16)
            (gw_top,), (early_recv,) = _grad_w_in(h_a, dq, NB, rider=_Direct(a2a=[early]))
            early_sum = _sum_slots(early_recv)
            (gw_bot,), (recv_in[l][0],) = _grad_w_in(h_b, dq, NB, rider=_Direct(a2a=[gw_top]))
            (dxo, dshift, dscale, G['pre_norm_g'][l]), (recv_in[l][1], early_all) = _in_bwd(
                dq, w_in_all[l], xin, dxo, P, mod, l,
                rider=_Both(_Direct(a2a=[gw_bot]), _AllGather2([early_sum[PACK_ROWS:]])))
        dmod[l] = jnp.concatenate([dshift, dscale, dgate], axis=1)
    grad_x = dxo.reshape(x.shape)

    Gl = dict(ada_b=jnp.concatenate(dmod, axis=0), pre_norm_g=stack('pre_norm_g'))
    (late_slots,) = _exchange("gather_late", _AllGather2([_pack_replicated(Gl, REP_LATE, LATE_PACK_ROWS)]))
    late_sum = _sum_slots(late_slots)
    dmod_all = late_slots[:, :DEPTH * 3 * D // PACK_C]

    out = {}
    first, _ = _adamw("adamw_w_in_a", [recv_in[l][0] for l in range(DEPTH)], w_in, M['w_in'], V['w_in'], LANE)
    out['w_in'], _ = _adamw("adamw_w_in_b", [recv_in[l][1] for l in range(DEPTH)], w_in, M['w_in'], V['w_in'], LANE,
                            row0=GW_SPLIT, into=first)
    out['w_out'], _ = _adamw("adamw_w_out", recv_out, w_out, M['w_out'], V['w_out'], 256)
    dm = lax.dynamic_slice_in_dim(dmod_all.reshape(NDEV, DEPTH, 3 * D), me * nc, nc, axis=2)
    out['ada_w'], _ = _ada_adamw(c_all.T, dm.transpose(1, 0, 2), ada_w, M['ada_w'], V['ada_w'])
    g_small = jnp.concatenate([early_sum[:PACK_ROWS], early_all.reshape(NDEV * PACK_ROWS, PACK_C), late_sum],
                              axis=0)

    def packs(T):
        return jnp.concatenate([_pack_sharded_block(T['pool_w'], T['pool_b'], T['conv_w']),
                                _pack_replicated(T, REP_EARLY, NDEV * PACK_ROWS),
                                _pack_replicated(T, REP_LATE, LATE_PACK_ROWS)], axis=0)[None]
    res_small, _ = _adamw("adamw_small", [g_small[None]], packs(W), packs(M), packs(V), g_small.shape[0] // 2)
    n_early = (1 + NDEV) * PACK_ROWS
    for idx in range(4):
        p = res_small[idx][0]
        pw_, pb_, cw_ = _unpack_sharded_block(p[:PACK_ROWS])
        rep = _unpack_replicated(p[PACK_ROWS:n_early], W, REP_EARLY)
        rep.update(_unpack_replicated(p[n_early:], W, REP_LATE))
        rep.update(pool_w=pw_, pool_b=pb_, conv_w=cw_)
        for k, a in rep.items():
            out.setdefault(k, [None] * 4)[idx] = a
    for k in ('w_in', 'w_out', 'ada_w'):
        out[k] = [a.reshape(W[k].shape) for a in out[k]]

    return (loss, grad_x, *[out[k][0] for k in WEIGHTS], *[out[k][1] for k in WEIGHTS],
            *[out[k][2] for k in WEIGHTS], *[out[k][3] for k in WEIGHTS])
```

```python
import functools

import jax
import jax.numpy as jnp
from jax import lax
from jax.experimental import pallas as pl
from jax.experimental.pallas import tpu as pltpu

F32, BF16 = jnp.float32, jnp.bfloat16
MESH = pl.DeviceIdType.MESH
HIGHEST = lax.Precision.HIGHEST

NDEV = 8
DEPTH = 2
D = 1024
NHEAD, HD = 8, 128
NGRP, GD = 4, 256
WINS = (2, 4, 8, 16)
CONV_K = 4
CONV_HALO = 8
POOL_HALO = 16
LRU_C = 8.0
NORM_EPS = 1e-6
ADAM_LR, ADAM_B1, ADAM_B2, ADAM_EPS, ADAM_WD, ADAM_STEP = 0.001, 0.9, 0.999, 1e-08, 0.01, 10
VMEM_LIMIT = 56 * 1024 * 1024
NQ = 4
SUB = 8
LANE = 128
PACK_C = 256
PACK_ROWS = 272

WEIGHTS = ['ada_w', 'ada_b', 'pre_norm_g', 'w_in', 'conv_w', 'conv_b', 'gate_a_w', 'gate_a_b', 'gate_x_w',
           'gate_x_b', 'lru_lambda', 'pool_w', 'pool_b', 'pool_scale', 'w_out', 'post_norm_g']
REP_EARLY = ['conv_b', 'gate_a_w', 'gate_a_b', 'gate_x_w', 'gate_x_b', 'lru_lambda', 'pool_scale', 'post_norm_g']
REP_LATE = ['ada_b', 'pre_norm_g']
GW_SPLIT = 512
LATE_PACK_ROWS = 32


def _cparams(*sem):
    return pltpu.CompilerParams(dimension_semantics=sem, vmem_limit_bytes=VMEM_LIMIT)


def _vec(l, k=None):
    if k is None:
        return pl.BlockSpec((None, 1, D), lambda *_: (l, 0, 0))
    return pl.BlockSpec((None, None, 1, D), lambda *_: (l, k, 0, 0))


def _layer(l, shape):
    nd = len(shape)
    return pl.BlockSpec((None,) + tuple(shape), lambda *_: (l,) + (0,) * nd)


def _full(shape):
    nd = len(shape)
    return pl.BlockSpec(shape, lambda *_: (0,) * nd)


def _rowsum8(z):
    return z.reshape(z.shape[0] // SUB, SUB, z.shape[1]).sum(axis=0)


def _sum8(acc):
    return jnp.sum(acc, axis=0, keepdims=True)


def _sigmoid(z):
    return 0.5 * jnp.tanh(0.5 * z) + 0.5


def _silu_parts(g):
    sg = _sigmoid(g)
    return g * sg, sg * (1.0 + g * (1.0 - sg))


def _one_minus_sq(a, log_a):
    z = 2.0 * log_a
    p = 1.0 / 24.0
    for k in (6.0, 2.0, 1.0):
        p = p * z + 1.0 / k
    return jnp.where(z > -0.03, -(p * z), 1.0 - a * a)


def _place():
    x, y, c = lax.axis_index("x"), lax.axis_index("y"), lax.axis_index("c")
    return x, y, c, 4 * x + 2 * y + c


class _Direct:
    def __init__(self, a2a=(), ag=()):
        self.arrays = list(a2a) + list(ag)
        self.n_a, self.n = len(a2a), len(self.arrays)
        self.out_shape = ([jax.ShapeDtypeStruct(a.shape, a.dtype) for a in a2a]
                          + [jax.ShapeDtypeStruct((NDEV,) + a.shape, a.dtype) for a in ag])
        self.scratch = [pltpu.SemaphoreType.DMA((self.n, NDEV - 1)), pltpu.SemaphoreType.DMA((self.n, NDEV - 1)),
                        pltpu.SemaphoreType.DMA((self.n,))]

    def _copies(self, ins, outs, sems):
        send_sems, recv_sems, local_sems = sems
        x, y, c, me = _place()
        local, remote = [], []
        for t in range(self.n):
            src = ins[t].at[me] if t < self.n_a else ins[t]
            local.append(pltpu.make_async_copy(src, outs[t].at[me], local_sems.at[t]))
        for r in range(1, NDEV):
            px = 1 - x if r & 4 else x
            py = 1 - y if r & 2 else y
            pc = 1 - c if r & 1 else c
            for t in range(self.n):
                src = ins[t].at[4 * px + 2 * py + pc] if t < self.n_a else ins[t]
                remote.append(pltpu.make_async_remote_copy(
                    src_ref=src, dst_ref=outs[t].at[me], send_sem=send_sems.at[t, r - 1],
                    recv_sem=recv_sems.at[t, r - 1], device_id=(px, py, pc), device_id_type=MESH))
        return local, remote

    def start(self, ins, outs, sems):
        local, remote = self._copies(ins, outs, sems)
        for cp in local + remote:
            cp.start()

    def finish(self, ins, outs, sems):
        local, remote = self._copies(ins, outs, sems)
        for cp in remote + local:
            cp.wait()


class _AllGather2:
    def __init__(self, arrays):
        self.arrays = list(arrays)
        self.n = len(self.arrays)
        self.out_shape = [jax.ShapeDtypeStruct((NDEV,) + a.shape, a.dtype) for a in self.arrays]
        self.scratch = [pltpu.SemaphoreType.DMA((self.n, NDEV - 1)), pltpu.SemaphoreType.DMA((self.n, NDEV - 1)),
                        pltpu.SemaphoreType.DMA((self.n,))]

    @staticmethod
    def _chips(x, y):
        return [(1 - x, y), (x, 1 - y), (1 - x, 1 - y)]

    def _copy(self, t, k, src, dst, to, sems):
        return pltpu.make_async_remote_copy(src_ref=src, dst_ref=dst, send_sem=sems[0].at[t, k],
                                            recv_sem=sems[1].at[t, k], device_id=to, device_id_type=MESH)

    def start(self, ins, outs, sems):
        x, y, c, me = _place()
        for t in range(self.n):
            pltpu.make_async_copy(ins[t], outs[t].at[me], sems[2].at[t]).start()
            self._copy(t, 0, ins[t], outs[t].at[me], (x, y, 1 - c), sems).start()
            for j, (px, py) in enumerate(self._chips(x, y)):
                self._copy(t, 1 + j, ins[t], outs[t].at[me], (px, py, c), sems).start()

    def finish(self, ins, outs, sems):
        x, y, c, me = _place()
        sib = (x, y, 1 - c)
        for j, (px, py) in enumerate(self._chips(x, y)):
            slot = 4 * px + 2 * py + c
            for t in range(self.n):
                self._copy(t, 1 + j, ins[t], outs[t].at[slot], sib, sems).wait_recv()
                self._copy(t, 4 + j, outs[t].at[slot], outs[t].at[slot], sib, sems).start()
        for t in range(self.n):
            for k in (0, 4, 5, 6):
                self._copy(t, k, ins[t], outs[t].at[me], sib, sems).wait_recv()
        for t in range(self.n):
            for k in range(NDEV - 1):
                self._copy(t, k, ins[t], outs[t].at[me], sib, sems).wait_send()
            pltpu.make_async_copy(ins[t], outs[t].at[me], sems[2].at[t]).wait()


class _AllGatherVia:
    def __init__(self, arrays):
        self.arrays = list(arrays)
        self.n = len(self.arrays)
        self.out_shape = [jax.ShapeDtypeStruct((NDEV,) + a.shape, a.dtype) for a in self.arrays]
        self.scratch = [pltpu.SemaphoreType.DMA((self.n, NDEV - 1)), pltpu.SemaphoreType.DMA((self.n, NDEV - 1)),
                        pltpu.SemaphoreType.DMA((self.n,))]

    def _copy(self, t, k, src, dst, to, sems):
        return pltpu.make_async_remote_copy(src_ref=src, dst_ref=dst, send_sem=sems[0].at[t, k],
                                            recv_sem=sems[1].at[t, k], device_id=to, device_id_type=MESH)

    def start(self, ins, outs, sems):
        x, y, c, me = _place()
        for t in range(self.n):
            pltpu.make_async_copy(ins[t], outs[t].at[me], sems[2].at[t]).start()
            self._copy(t, 0, ins[t], outs[t].at[me], (x, y, 1 - c), sems).start()
            self._copy(t, 1, ins[t], outs[t].at[me], (1 - x, y, c), sems).start()
            self._copy(t, 2, ins[t], outs[t].at[me], (x, 1 - y, c), sems).start()

    def finish(self, ins, outs, sems):
        x, y, c, me = _place()
        sib = (x, y, 1 - c)
        slot_x, slot_y, slot_d = 4 * (1 - x) + 2 * y + c, 4 * x + 2 * (1 - y) + c, 4 * (1 - x) + 2 * (1 - y) + c
        slot_on = c * slot_x + (1 - c) * slot_y
        to_on = (c * x + (1 - c) * (1 - x), c * (1 - y) + (1 - c) * y, c)
        for t in range(self.n):
            self._copy(t, 1, ins[t], outs[t].at[slot_x], sib, sems).wait_recv()
            self._copy(t, 2, ins[t], outs[t].at[slot_y], sib, sems).wait_recv()
            self._copy(t, 3, outs[t].at[slot_on], outs[t].at[slot_on], to_on, sems).start()
            self._copy(t, 4, outs[t].at[slot_x], outs[t].at[slot_x], sib, sems).start()
            self._copy(t, 5, outs[t].at[slot_y], outs[t].at[slot_y], sib, sems).start()
        for t in range(self.n):
            self._copy(t, 3, ins[t], outs[t].at[slot_d], sib, sems).wait_recv()
            self._copy(t, 6, outs[t].at[slot_d], outs[t].at[slot_d], sib, sems).start()
        for t in range(self.n):
            for k in (0, 4, 5, 6):
                self._copy(t, k, ins[t], outs[t].at[me], sib, sems).wait_recv()
        for t in range(self.n):
            for k in range(NDEV - 1):
                self._copy(t, k, ins[t], outs[t].at[me], sib, sems).wait_send()
            pltpu.make_async_copy(ins[t], outs[t].at[me], sems[2].at[t]).wait()


class _Both:
    def __init__(self, *riders):
        self.riders = riders
        self.arrays = [a for r in riders for a in r.arrays]
        self.n = len(self.arrays)
        self.out_shape = [o for r in riders for o in r.out_shape]
        self.scratch = [s for r in riders for s in r.scratch]

    def _parts(self, ins, outs, sems):
        p, q = 0, 0
        for r in self.riders:
            yield r, ins[p:p + r.n], outs[p:p + r.n], sems[q:q + len(r.scratch)]
            p, q = p + r.n, q + len(r.scratch)

    def start(self, ins, outs, sems):
        for r, i, o, s in self._parts(ins, outs, sems):
            r.start(i, o, s)

    def finish(self, ins, outs, sems):
        for r, i, o, s in self._parts(ins, outs, sems):
            r.finish(i, o, s)


def _exchange(name, rider):
    n = rider.n

    def body(*refs):
        rider.start(refs[:n], refs[n:2 * n], refs[2 * n:])
        rider.finish(refs[:n], refs[n:2 * n], refs[2 * n:])

    any_spec = pl.BlockSpec(memory_space=pl.ANY)
    return list(pl.pallas_call(body, name=name, out_shape=rider.out_shape, in_specs=[any_spec] * n,
                               out_specs=[any_spec] * n, scratch_shapes=rider.scratch)(*rider.arrays))


def _pcall(body, *, name, grid, in_specs, out_specs, out_shape, args, scratch_shapes=(), rider=None, aliases=None):
    params = _cparams(*(("arbitrary",) * len(grid)))
    if rider is None:
        res = pl.pallas_call(body, name=name, grid=grid, in_specs=in_specs, out_specs=out_specs,
                             out_shape=out_shape, scratch_shapes=list(scratch_shapes),
                             input_output_aliases=aliases or {}, compiler_params=params)(*args)
        return list(res), []
    n_in, n_out, n_scr, rn = len(in_specs), len(out_specs), len(scratch_shapes), rider.n

    def wrapped(*refs):
        cuts = [n_in, rn, n_out, rn, n_scr]
        parts, p = [], 0
        for n in cuts:
            parts.append(refs[p:p + n])
            p += n
        ins, r_in, outs, r_out, scr = parts
        sems = refs[p:]
        ids = [pl.program_id(a) for a in range(len(grid))]
        first = functools.reduce(jnp.logical_and, [i == 0 for i in ids])
        last = functools.reduce(jnp.logical_and, [i == g - 1 for i, g in zip(ids, grid)])

        @pl.when(first)
        def _():
            rider.start(r_in, r_out, sems)
        body(*ins, *outs, *scr)

        @pl.when(last)
        def _():
            rider.finish(r_in, r_out, sems)

    any_spec = pl.BlockSpec(memory_space=pl.ANY)
    res = pl.pallas_call(
        wrapped, name=name, grid=grid, in_specs=list(in_specs) + [any_spec] * rn,
        out_specs=list(out_specs) + [any_spec] * rn, out_shape=list(out_shape) + rider.out_shape,
        scratch_shapes=list(scratch_shapes) + rider.scratch, compiler_params=params)(*args, *rider.arrays)
    return list(res[:n_out]), list(res[n_out:])


def _mod_cols(c_all, ada_w):
    nc = ada_w.shape[2]

    def body(c_ref, w_ref, o_ref):
        cv = c_ref[...]
        ca = cv * jax.nn.sigmoid(cv)
        for l in range(DEPTH):
            o_ref[:, l * nc:(l + 1) * nc] = jnp.dot(ca, w_ref[l], precision=HIGHEST, preferred_element_type=F32)

    return pl.pallas_call(body, name="mod_cols", out_shape=jax.ShapeDtypeStruct((NDEV, DEPTH * nc), F32),
                          compiler_params=_cparams())(c_all, ada_w)


def _pre_proj(x, P, mod, w_in_l, l, rider=None):
    S = x.shape[0]
    TM = min(512, S)
    NB = w_in_l.shape[2]

    def body(x_ref, g_ref, sc_ref, sh_ref, w_ref, ha_ref, hb_ref, p_ref):
        xv = x_ref[...]
        rstd = lax.rsqrt(jnp.mean(xv * xv, axis=-1, keepdims=True) + NORM_EPS)
        h = ((xv * rstd * g_ref[...]) * (1.0 + sc_ref[...]) + sh_ref[...]).astype(BF16)
        ha_ref[...] = h[:, :GW_SPLIT]
        hb_ref[...] = h[:, GW_SPLIT:]
        for j in range(NDEV):
            p_ref[:, j * NB:(j + 1) * NB] = jnp.dot(h, w_ref[j], preferred_element_type=F32)

    row = pl.BlockSpec((TM, D), lambda i: (i, 0))
    return _pcall(
        body, name="pre_proj", grid=(S // TM,),
        in_specs=[row, _vec(l), _vec(l, 1), _vec(l, 0), _full((NDEV, D, NB))],
        out_specs=[pl.BlockSpec((TM, GW_SPLIT), lambda i: (i, 0)), pl.BlockSpec((TM, D - GW_SPLIT), lambda i: (i, 0)),
                   pl.BlockSpec((TM, NDEV * NB), lambda i: (i, 0))],
        out_shape=[jax.ShapeDtypeStruct((S, GW_SPLIT), BF16), jax.ShapeDtypeStruct((S, D - GW_SPLIT), BF16),
                   jax.ShapeDtypeStruct((S, NDEV * NB), F32)],
        args=(x, P['pre_norm_g'], mod, mod, w_in_l), rider=rider)


def _taps(E):
    return [pltpu.roll(E, CONV_K - 1 - k, axis=0)[CONV_HALO:, :] for k in range(CONV_K - 1)] + [E[CONV_HALO:, :]]


def _conv(E, cw_ref, cb_ref):
    w = cw_ref[...]
    taps = _taps(E)
    acc = cb_ref[...] + taps[0] * w[0:1, :]
    for k in range(1, CONV_K):
        acc = acc + taps[k] * w[k:k + 1, :]
    return acc


CH_R, CH_C = 16, 512


def _chunks(T, fn):
    def step(c, carry):
        rows = pl.ds(pl.multiple_of(c * CH_R, CH_R), CH_R)
        for hf in range(D // CH_C):
            fn(rows, slice(hf * CH_C, (hf + 1) * CH_C), hf)
        return carry
    lax.fori_loop(0, T // CH_R, step, 0)


def _to_scan(ref, rows, hf, val):
    for q in range(CH_C // LANE):
        ref[hf * (CH_C // LANE) + q, rows, :] = val[:, q * LANE:(q + 1) * LANE]


def _scan(sa, sv, carry_ref, out_ref, reverse):
    NC, T, _ = sa.shape
    n8 = T // SUB
    rows = range(SUB - 2, -1, -1) if reverse else range(1, SUB)
    for cb in range(NC):
        r_in = SUB - 1 if reverse else 0
        Ap = sa[cb, pl.ds(r_in, n8, stride=SUB), :]
        Vp = sv[cb, pl.ds(r_in, n8, stride=SUB), :]
        for r in rows:
            Ar = sa[cb, pl.ds(r, n8, stride=SUB), :]
            Vp = sv[cb, pl.ds(r, n8, stride=SUB), :] + Ar * Vp
            Ap = Ar * Ap
            sa[cb, pl.ds(r, n8, stride=SUB), :] = Ap
            sv[cb, pl.ds(r, n8, stride=SUB), :] = Vp
    edge = 0 if reverse else SUB - 1

    def step(k, c):
        r0 = pl.multiple_of((n8 - 1 - k if reverse else k) * SUB, SUB)
        h = jnp.concatenate([sv[cb, pl.ds(r0, SUB), :] + sa[cb, pl.ds(r0, SUB), :] * c[:, cb * LANE:(cb + 1) * LANE]
                             for cb in range(NC)], axis=1)
        out_ref[pl.ds(r0, SUB), :] = h
        return jnp.broadcast_to(h[edge:edge + 1, :], (SUB, D))

    carry_ref[...] = lax.fori_loop(0, n8, step, carry_ref[...])


def _rnn_fwd(proj, P, l, rider=None):
    S = proj.shape[0]
    TB = min(256, S)

    def body(xr_ref, g_ref, cw_ref, cb_ref, wa_ref, ba_ref, wx_ref, bx_ref, lam_ref, hs_ref, y_ref,
             u_ref, r_ref, i_ref, a_ref, m_ref, xbuf, sa, sv, hc):
        @pl.when(pl.program_id(0) == 0)
        def _():
            xbuf[0:CONV_HALO, :] = jnp.zeros((CONV_HALO, D), F32)
            hc[...] = jnp.zeros((SUB, D), F32)
        xbuf[CONV_HALO:, :] = xr_ref[...]
        u = _conv(xbuf[...], cw_ref, cb_ref)
        xbuf[0:CONV_HALO, :] = xbuf[TB:TB + CONV_HALO, :]
        ub = u.astype(BF16)
        zr = jnp.concatenate([jnp.dot(ub[:, h * HD:(h + 1) * HD], wa_ref[h], preferred_element_type=F32)
                              for h in range(NHEAD)], axis=1)
        zi = jnp.concatenate([jnp.dot(ub[:, h * HD:(h + 1) * HD], wx_ref[h], preferred_element_type=F32)
                              for h in range(NHEAD)], axis=1)
        r = _sigmoid(zr + ba_ref[...])
        ig = _sigmoid(zi + bx_ref[...])
        log_a = r * (-LRU_C * jax.nn.softplus(-lam_ref[...]))
        a = jnp.exp(log_a)
        mult = jnp.sqrt(_one_minus_sq(a, log_a))
        v = mult * (ig * u)
        u_ref[...] = u
        r_ref[...] = r
        i_ref[...] = ig
        a_ref[...] = a
        m_ref[...] = mult
        for cb in range(D // LANE):
            sa[cb] = a[:, cb * LANE:(cb + 1) * LANE]
            sv[cb] = v[:, cb * LANE:(cb + 1) * LANE]
        _scan(sa, sv, hc, hs_ref, reverse=False)
        silu, _ = _silu_parts(g_ref[...])
        y_ref[...] = (hs_ref[...] * silu).astype(BF16)

    rowb = pl.BlockSpec((TB, D), lambda i: (i, 0))
    return _pcall(
        body, name="rnn_fwd", grid=(S // TB,),
        in_specs=[rowb, pl.BlockSpec((TB, D), lambda i: (i, 1)), _layer(l, (CONV_K, D)), _vec(l),
                  _layer(l, (NHEAD, HD, HD)), _vec(l), _layer(l, (NHEAD, HD, HD)), _vec(l), _vec(l)],
        out_specs=[rowb] * 7,
        out_shape=[jax.ShapeDtypeStruct((S, D), F32), jax.ShapeDtypeStruct((S, D), BF16)]
        + [jax.ShapeDtypeStruct((S, D), F32)] * 5,
        scratch_shapes=[pltpu.VMEM((TB + CONV_HALO, D), F32), pltpu.VMEM((D // LANE, TB, LANE), F32),
                        pltpu.VMEM((D // LANE, TB, LANE), F32), pltpu.VMEM((SUB, D), F32)],
        args=(proj, proj, P['conv_w'], P['conv_b'], P['gate_a_w'], P['gate_a_b'], P['gate_x_w'], P['gate_x_b'],
              P['lru_lambda']), rider=rider)


def _pooled(ebuf, t0, TB):
    tt = t0 + lax.broadcasted_iota(jnp.int32, (TB, 1), 0)
    pooled, inv = [], []
    for g, win in enumerate(WINS):
        Eg = ebuf[:, g * GD:(g + 1) * GD]
        L = Eg
        for lev in range(g + 1):
            L = L + pltpu.roll(L, 1 << lev, axis=0)
        icnt = 1.0 / jnp.minimum(tt + 1, win).astype(F32)
        pooled.append(L[POOL_HALO:, :] * icnt - Eg[POOL_HALO:, :])
        inv.append(icnt)
    return pooled, inv


def _pool_fwd(proj, P, l, rider=None):
    S = proj.shape[0]
    TB = min(512, S)

    def body(xp_ref, g_ref, pw_ref, pb_ref, ps_ref, y_ref, ebuf):
        i = pl.program_id(0)

        @pl.when(i == 0)
        def _():
            ebuf[0:POOL_HALO, :] = jnp.zeros((POOL_HALO, D), F32)
        ebuf[POOL_HALO:, :] = xp_ref[...]
        pooled, _ = _pooled(ebuf, i * TB, TB)
        ebuf[0:POOL_HALO, :] = ebuf[TB:TB + POOL_HALO, :]
        yp = jnp.concatenate([jnp.dot(pooled[g].astype(BF16), pw_ref[g], preferred_element_type=F32)
                              for g in range(NGRP)], axis=1) + pb_ref[...]
        silu, _ = _silu_parts(g_ref[...])
        y_ref[...] = (yp * ps_ref[...] * silu).astype(BF16)

    return _pcall(
        body, name="pool_fwd", grid=(S // TB,),
        in_specs=[pl.BlockSpec((TB, D), lambda i: (i, 2)), pl.BlockSpec((TB, D), lambda i: (i, 3)),
                  _layer(l, (NGRP, GD, GD)), _vec(l), _vec(l)],
        out_specs=[pl.BlockSpec((TB, D), lambda i: (i, 0))],
        out_shape=[jax.ShapeDtypeStruct((S, D), BF16)],
        scratch_shapes=[pltpu.VMEM((TB + POOL_HALO, D), F32)],
        args=(proj, proj, P['pool_w'], P['pool_b'], P['pool_scale']), rider=rider)


def _out_post(yr, yp, w_out_l, x, mod, P, l, target=None, rider=None):
    S = x.shape[0]
    TM = min(512, S)
    last = target is not None

    def body(*refs):
        if last:
            yr_ref, yp_ref, w_ref, x_ref, gate_ref, gp_ref, t_ref, y_ref, xo_ref, loss_ref = refs
        else:
            yr_ref, yp_ref, w_ref, x_ref, gate_ref, gp_ref, y_ref, xo_ref = refs
        acc = (jnp.dot(yr_ref[...], w_ref[0:D, :], preferred_element_type=F32)
               + jnp.dot(yp_ref[...], w_ref[D:2 * D, :], preferred_element_type=F32))
        y_ref[...] = acc
        rstd = lax.rsqrt(jnp.mean(acc * acc, axis=-1, keepdims=True) + NORM_EPS)
        xn = x_ref[...] + gate_ref[...] * (acc * rstd * gp_ref[...])
        if last:
            err = xn - t_ref[...]
            xo_ref[...] = err * (1.0 / D)

            @pl.when(pl.program_id(0) == 0)
            def _():
                loss_ref[...] = jnp.zeros((SUB, D), F32)
            loss_ref[...] += _rowsum8(err * err)
        else:
            xo_ref[...] = xn

    row = pl.BlockSpec((TM, D), lambda i: (i, 0))
    in_specs = [row, row, _full((2 * D, D)), row, _vec(l, 2), _vec(l)]
    out_specs = [row, row]
    out_shape = [jax.ShapeDtypeStruct((S, D), F32), jax.ShapeDtypeStruct((S, D), F32)]
    args = [yr, yp, w_out_l, x, mod, P['post_norm_g']]
    if last:
        in_specs.append(row)
        out_specs.append(_full((SUB, D)))
        out_shape.append(jax.ShapeDtypeStruct((SUB, D), F32))
        args.append(target)
    return _pcall(body, name="out_post_loss" if last else "out_post", grid=(S // TM,), in_specs=in_specs,
                  out_specs=out_specs, out_shape=out_shape, args=args, rider=rider)


def _out_bwd(dxo, y, yr, yp, w_out_l, mod, P, l, rider=None):
    S = y.shape[0]
    TM = min(512, S)
    nsteps = S // TM

    def body(dxo_ref, y_ref, yr_ref, yp_ref, w_ref, gate_ref, gp_ref, dyr_ref, dyp_ref, gw_ref, dgate_ref,
             dgp_ref, gw_acc, vacc):
        i = pl.program_id(0)

        @pl.when(i == 0)
        def _():
            gw_acc[...] = jnp.zeros_like(gw_acc)
            vacc[...] = jnp.zeros_like(vacc)
        yv = y_ref[...]
        dxo_v = dxo_ref[...]
        rstd = lax.rsqrt(jnp.mean(yv * yv, axis=-1, keepdims=True) + NORM_EPS)
        n = yv * rstd
        gp = gp_ref[...]
        vacc[0] += _rowsum8(dxo_v * (n * gp))
        drn = dxo_v * gate_ref[...]
        vacc[1] += _rowsum8(drn * n)
        dn = drn * gp
        dy = (rstd * (dn - n * jnp.mean(dn * n, axis=-1, keepdims=True))).astype(BF16)
        dyr_ref[...] = lax.dot_general(dy, w_ref[0:D, :], (((1,), (1,)), ((), ())), preferred_element_type=F32)
        dyp_ref[...] = lax.dot_general(dy, w_ref[D:2 * D, :], (((1,), (1,)), ((), ())),
                                       preferred_element_type=F32)
        gw_acc[0:D, :] += lax.dot_general(yr_ref[...], dy, (((0,), (0,)), ((), ())), preferred_element_type=F32)
        gw_acc[D:2 * D, :] += lax.dot_general(yp_ref[...], dy, (((0,), (0,)), ((), ())),
                                              preferred_element_type=F32)

        @pl.when(i == nsteps - 1)
        def _():
            gw_ref[...] = gw_acc[...].astype(BF16)
            dgate_ref[...] = _sum8(vacc[0])
            dgp_ref[...] = _sum8(vacc[1])

    row = pl.BlockSpec((TM, D), lambda i: (i, 0))
    return _pcall(
        body, name="out_bwd", grid=(nsteps,),
        in_specs=[row, row, row, row, _full((2 * D, D)), _vec(l, 2), _vec(l)],
        out_specs=[row, row, _full((2 * D, D)), _full((1, D)), _full((1, D))],
        out_shape=[jax.ShapeDtypeStruct((S, D), F32), jax.ShapeDtypeStruct((S, D), F32),
                   jax.ShapeDtypeStruct((2 * D, D), BF16), jax.ShapeDtypeStruct((1, D), F32),
                   jax.ShapeDtypeStruct((1, D), F32)],
        scratch_shapes=[pltpu.VMEM((2 * D, D), F32), pltpu.VMEM((2, SUB, D), F32)],
        args=(dxo, y, yr, yp, w_out_l, mod, P['post_norm_g']), rider=rider)


def _rnn_bwd(dyr, hs, fwd, proj, P, l, rider=None):
    S = proj.shape[0]
    TB = min(256, S)
    nb = S // TB
    TE = TB + CONV_HALO
    A_BA, A_BX, A_LAM, A_CB, A_CW = 0, 1, 2, 3, 4

    def blk(i):
        return nb - 1 - i

    def body(dyr_ref, hs_ref, hprev_ref, u_ref, r_ref, i_ref, a_ref, m_ref, xr_ref, g_ref, cw_ref,
             wa_ref, wx_ref, lam_ref, dxr_ref, dg_ref, gcw_ref, gcb_ref, gwa_ref, gba_ref, gwx_ref, gbx_ref,
             glam_ref, hbuf, abuf, dbuf, sa, sv, dh_ref, hp_ref, dzr_ref, dzi_ref, dhc, vacc):
        i = pl.program_id(0)
        first = blk(i) == 0

        @pl.when(i == 0)
        def _():
            abuf[TB:, :] = jnp.zeros((CONV_HALO, D), F32)
            dbuf[TB:, :] = jnp.zeros((CONV_HALO, D), F32)
            dhc[...] = jnp.zeros_like(dhc)
            vacc[...] = jnp.zeros_like(vacc)
            gwa_ref[...] = jnp.zeros_like(gwa_ref)
            gwx_ref[...] = jnp.zeros_like(gwx_ref)

        hbuf[0:CONV_HALO, :] = jnp.where(first, 0.0, hprev_ref[...])
        hbuf[CONV_HALO:, :] = hs_ref[...]
        hp_ref[...] = pltpu.roll(hbuf[...], 1, axis=0)[CONV_HALO:, :]
        abuf[0:TB, :] = a_ref[...]
        b = pltpu.roll(abuf[...], TE - 1, axis=0)[0:TB, :]
        for cb in range(D // LANE):
            sa[cb] = b[:, cb * LANE:(cb + 1) * LANE]
        abuf[TB:, :] = a_ref[0:CONV_HALO, :]

        def gate_bwd(rows, cs, hf):
            silu, dsilu = _silu_parts(g_ref[rows, cs])
            dy = dyr_ref[rows, cs]
            dg_ref[rows, cs] = (dy * hs_ref[rows, cs] * dsilu).astype(BF16)
            _to_scan(sv, rows, hf, dy * silu)
        _chunks(TB, gate_bwd)

        _scan(sa, sv, dhc, dh_ref, reverse=True)

        csp = -LRU_C * jax.nn.softplus(-lam_ref[...])

        def lru_bwd(rows, cs, hf):
            dh, a, ig, u, mult, r = dh_ref[rows, cs], a_ref[rows, cs], i_ref[rows, cs], u_ref[rows, cs], \
                m_ref[rows, cs], r_ref[rows, cs]
            dlog_a = dh * hp_ref[rows, cs] * a - (dh * ig * u) * (a * a) / mult
            dzr = dlog_a * csp[:, cs] * r * (1.0 - r)
            dzi = (dh * mult * u) * ig * (1.0 - ig)
            dzr_ref[rows, cs] = dzr.astype(BF16)
            dzi_ref[rows, cs] = dzi.astype(BF16)
            dbuf[rows, cs] = dh * mult * ig
            vacc[A_LAM, :, cs] += _rowsum8(dlog_a * r)
            vacc[A_BA, :, cs] += _rowsum8(dzr)
            vacc[A_BX, :, cs] += _rowsum8(dzi)
        _chunks(TB, lru_bwd)

        ub, dzrb, dzib = u_ref[...].astype(BF16), dzr_ref[...], dzi_ref[...]
        du_g = []
        for h in range(NHEAD):
            cs = slice(h * HD, (h + 1) * HD)
            gwa_ref[h] += lax.dot_general(ub[:, cs], dzrb[:, cs], (((0,), (0,)), ((), ())),
                                          preferred_element_type=F32)
            gwx_ref[h] += lax.dot_general(ub[:, cs], dzib[:, cs], (((0,), (0,)), ((), ())),
                                          preferred_element_type=F32)
            du_g.append(lax.dot_general(dzrb[:, cs], wa_ref[h], (((1,), (1,)), ((), ())),
                                        preferred_element_type=F32)
                        + lax.dot_general(dzib[:, cs], wx_ref[h], (((1,), (1,)), ((), ())),
                                          preferred_element_type=F32))
        du = dbuf[0:TB, :] + jnp.concatenate(du_g, axis=1)
        dbuf[0:TB, :] = du

        Dd = dbuf[...]
        w = cw_ref[...]
        xv = xr_ref[...]
        dx = du * w[CONV_K - 1:CONV_K, :]
        vacc[A_CB] += _rowsum8(du)
        vacc[A_CW + CONV_K - 1] += _rowsum8(xv * du)
        for k in range(CONV_K - 1):
            ahead = pltpu.roll(Dd, TE - (CONV_K - 1 - k), axis=0)[0:TB, :]
            dx = dx + ahead * w[k:k + 1, :]
            vacc[A_CW + k] += _rowsum8(xv * ahead)
        dxr_ref[...] = dx.astype(BF16)
        dbuf[TB:, :] = du[0:CONV_HALO, :]

        @pl.when(i == nb - 1)
        def _():
            gba_ref[...] = _sum8(vacc[A_BA])
            gbx_ref[...] = _sum8(vacc[A_BX])
            glam_ref[...] = _sum8(vacc[A_LAM]) * (LRU_C * _sigmoid(-lam_ref[...]))
            gcb_ref[...] = _sum8(vacc[A_CB])
            for k in range(CONV_K):
                gcw_ref[k:k + 1, :] = _sum8(vacc[A_CW + k])

    rowb = pl.BlockSpec((TB, D), lambda i: (blk(i), 0))
    halo = pl.BlockSpec((SUB, D), lambda i: (jnp.maximum(blk(i) * (TB // SUB) - 1, 0), 0))
    wspec = _full((NHEAD, HD, HD))
    wlay = _layer(l, (NHEAD, HD, HD))
    vec1 = _full((1, D))
    return _pcall(
        body, name="rnn_bwd", grid=(nb,),
        in_specs=[rowb, rowb, halo] + [rowb] * 5 + [rowb, pl.BlockSpec((TB, D), lambda i: (blk(i), 1)),
                                                    _layer(l, (CONV_K, D)), wlay, wlay, _vec(l)],
        out_specs=[rowb, rowb, _full((CONV_K, D)), vec1, wspec, vec1, wspec, vec1, vec1],
        out_shape=[jax.ShapeDtypeStruct((S, D), BF16), jax.ShapeDtypeStruct((S, D), BF16),
                   jax.ShapeDtypeStruct((CONV_K, D), F32), jax.ShapeDtypeStruct((1, D), F32),
                   jax.ShapeDtypeStruct((NHEAD, HD, HD), F32), jax.ShapeDtypeStruct((1, D), F32),
                   jax.ShapeDtypeStruct((NHEAD, HD, HD), F32), jax.ShapeDtypeStruct((1, D), F32),
                   jax.ShapeDtypeStruct((1, D), F32)],
        scratch_shapes=[pltpu.VMEM((TE, D), F32), pltpu.VMEM((TE, D), F32), pltpu.VMEM((TE, D), F32),
                        pltpu.VMEM((D // LANE, TB, LANE), F32), pltpu.VMEM((D // LANE, TB, LANE), F32),
                        pltpu.VMEM((TB, D), F32), pltpu.VMEM((TB, D), F32), pltpu.VMEM((TB, D), BF16),
                        pltpu.VMEM((TB, D), BF16), pltpu.VMEM((SUB, D), F32),
                        pltpu.VMEM((A_CW + CONV_K, SUB, D), F32)],
        args=(dyr, hs, hs, *fwd, proj, proj, P['conv_w'], P['gate_a_w'], P['gate_x_w'], P['lru_lambda']),
        rider=rider)


def _pool_bwd(dyp, proj, P, l, rider=None):
    S = proj.shape[0]
    TB = min(512, S)
    nb = S // TB
    TE = TB + POOL_HALO

    def blk(i):
        return nb - 1 - i

    def body(dy_ref, xp_ref, xprev_ref, g_ref, pw_ref, pb_ref, ps_ref, dxp_ref, dg_ref, gpw_ref, gpb_ref,
             gps_ref, ebuf, qbuf, vacc):
        i = pl.program_id(0)
        first = blk(i) == 0

        @pl.when(i == 0)
        def _():
            qbuf[TB:, :] = jnp.zeros((POOL_HALO, D), F32)
            vacc[...] = jnp.zeros_like(vacc)
            gpw_ref[...] = jnp.zeros_like(gpw_ref)

        ebuf[0:POOL_HALO, :] = jnp.where(first, 0.0, xprev_ref[...])
        ebuf[POOL_HALO:, :] = xp_ref[...]
        pooled, inv = _pooled(ebuf, blk(i) * TB, TB)
        pooled = [p.astype(BF16) for p in pooled]
        yp = jnp.concatenate([jnp.dot(pooled[g], pw_ref[g], preferred_element_type=F32)
                              for g in range(NGRP)], axis=1) + pb_ref[...]
        silu, dsilu = _silu_parts(g_ref[...])
        dy = dy_ref[...]
        ps = ps_ref[...]
        dyp_v = dy * ps * silu
        vacc[0] += _rowsum8(dy * yp * silu)
        vacc[1] += _rowsum8(dyp_v)
        dg_ref[...] = (dy * yp * ps * dsilu).astype(BF16)
        dypb = dyp_v.astype(BF16)
        for g in range(NGRP):
            cs = slice(g * GD, (g + 1) * GD)
            gpw_ref[g] += lax.dot_general(pooled[g], dypb[:, cs], (((0,), (0,)), ((), ())),
                                          preferred_element_type=F32)
            dpool = lax.dot_general(dypb[:, cs], pw_ref[g], (((1,), (1,)), ((), ())),
                                    preferred_element_type=F32)
            qbuf[0:TB, cs] = dpool * inv[g]
            L = qbuf[:, cs]
            for lev in range(g + 1):
                L = L + pltpu.roll(L, TE - (1 << lev), axis=0)
            dxp_ref[:, cs] = (L[0:TB, :] - dpool).astype(BF16)
        qbuf[TB:, :] = qbuf[0:POOL_HALO, :]

        @pl.when(i == nb - 1)
        def _():
            gps_ref[...] = _sum8(vacc[0])
            gpb_ref[...] = _sum8(vacc[1])

    rowb = pl.BlockSpec((TB, D), lambda i: (blk(i), 0))
    return _pcall(
        body, name="pool_bwd", grid=(nb,),
        in_specs=[rowb, pl.BlockSpec((TB, D), lambda i: (blk(i), 2)),
                  pl.BlockSpec((POOL_HALO, D), lambda i: (jnp.maximum(blk(i) * (TB // POOL_HALO) - 1, 0), 2)),
                  pl.BlockSpec((TB, D), lambda i: (blk(i), 3)), _layer(l, (NGRP, GD, GD)), _vec(l), _vec(l)],
        out_specs=[rowb, rowb, _full((NGRP, GD, GD)), _full((1, D)), _full((1, D))],
        out_shape=[jax.ShapeDtypeStruct((S, D), BF16), jax.ShapeDtypeStruct((S, D), BF16),
                   jax.ShapeDtypeStruct((NGRP, GD, GD), F32), jax.ShapeDtypeStruct((1, D), F32),
                   jax.ShapeDtypeStruct((1, D), F32)],
        scratch_shapes=[pltpu.VMEM((TE, D), F32), pltpu.VMEM((TE, D), F32), pltpu.VMEM((2, SUB, D), F32)],
        args=(dyp, proj, proj, proj, P['pool_w'], P['pool_b'], P['pool_scale']), rider=rider)


def _in_bwd(dq, w_in_l, x, dxo, P, mod, l, rider=None):
    S = x.shape[0]
    TM = min(256, S)
    NB = w_in_l.shape[2]
    nsteps = S // TM
    per_q = D // NB

    def body(d0, d1, d2, d3, w_ref, x_ref, dxo_ref, g_ref, sc_ref, dx_ref, dsh_ref, dsc_ref, dg_ref, vacc):
        i = pl.program_id(0)

        @pl.when(i == 0)
        def _():
            vacc[...] = jnp.zeros_like(vacc)
        dref = (d0, d1, d2, d3)
        dh = jnp.zeros((TM, D), F32)
        for j in range(NDEV):
            c0 = (j % per_q) * NB
            dh = dh + lax.dot_general(dref[j // per_q][:, c0:c0 + NB], w_ref[j], (((1,), (1,)), ((), ())),
                                      preferred_element_type=F32)
        xv = x_ref[...]
        rstd = lax.rsqrt(jnp.mean(xv * xv, axis=-1, keepdims=True) + NORM_EPS)
        xn = xv * rstd
        g, sc = g_ref[...], 1.0 + sc_ref[...]
        vacc[0] += _rowsum8(dh)
        vacc[1] += _rowsum8(dh * (xn * g))
        vacc[2] += _rowsum8(dh * sc * xn)
        dxn = dh * sc * g
        dx_ref[...] = dxo_ref[...] + rstd * (dxn - xn * jnp.mean(dxn * xn, axis=-1, keepdims=True))

        @pl.when(i == nsteps - 1)
        def _():
            dsh_ref[...] = _sum8(vacc[0])
            dsc_ref[...] = _sum8(vacc[1])
            dg_ref[...] = _sum8(vacc[2])

    row = pl.BlockSpec((TM, D), lambda i: (i, 0))
    return _pcall(
        body, name="in_bwd", grid=(nsteps,),
        in_specs=[row, row, row, row, _full((NDEV, D, NB)), row, row, _vec(l), _vec(l, 1)],
        out_specs=[row, _full((1, D)), _full((1, D)), _full((1, D))],
        out_shape=[jax.ShapeDtypeStruct((S, D), F32)] + [jax.ShapeDtypeStruct((1, D), F32)] * 3,
        scratch_shapes=[pltpu.VMEM((3, SUB, D), F32)],
        args=(*dq, w_in_l, x, dxo, P['pre_norm_g'], mod), rider=rider)


def _grad_w_in(h, dq, NB, rider=None):
    S, RH = h.shape
    TK = min(1024, S)
    nk = S // TK

    def body(h_ref, d0, d1, d2, d3, o_ref, acc):
        k = pl.program_id(0)

        @pl.when(k == 0)
        def _():
            acc[...] = jnp.zeros_like(acc)
        hv = h_ref[...]
        for q, d_ref in enumerate((d0, d1, d2, d3)):
            acc[:, q * D:(q + 1) * D] += lax.dot_general(hv, d_ref[...], (((0,), (0,)), ((), ())),
                                                         preferred_element_type=F32)

        @pl.when(k == nk - 1)
        def _():
            for j in range(NDEV):
                o_ref[j] = acc[:, j * NB:(j + 1) * NB].astype(BF16)

    row = pl.BlockSpec((TK, D), lambda k: (k, 0))
    return _pcall(
        body, name="grad_w_in", grid=(nk,),
        in_specs=[pl.BlockSpec((TK, RH), lambda k: (k, 0)), row, row, row, row],
        out_specs=[_full((NDEV, RH, NB))],
        out_shape=[jax.ShapeDtypeStruct((NDEV, RH, NB), BF16)],
        scratch_shapes=[pltpu.VMEM((RH, NQ * D), F32)],
        args=(h, *dq), rider=rider)


def _adamw_math(g, w, m, v):
    m2 = ADAM_B1 * m + (1.0 - ADAM_B1) * g
    v2 = ADAM_B2 * v + (1.0 - ADAM_B2) * (g * g)
    m_hat = m2 / (1.0 - ADAM_B1 ** ADAM_STEP)
    v_hat = v2 / (1.0 - ADAM_B2 ** ADAM_STEP)
    delta = -ADAM_LR * (m_hat / (jnp.sqrt(v_hat) + ADAM_EPS) + ADAM_WD * w)
    return delta, m2, v2


def _adamw(name, gs, w, m, v, TR, row0=0, into=None):
    L = len(gs)
    n, Rp, C = gs[0].shape
    R = w.shape[1]
    b0 = row0 // TR

    def body(*refs):
        g_refs = refs[:L]
        w_ref, m_ref, v_ref = refs[L:L + 3]
        go_ref, do_ref, mo_ref, vo_ref = refs[-4:]
        lay = pl.program_id(0)
        for li in range(L):
            @pl.when(lay == li)
            def _(li=li):
                g = g_refs[li][0].astype(F32)
                for s in range(1, n):
                    g = g + g_refs[li][s].astype(F32)
                delta, m2, v2 = _adamw_math(g, w_ref[...], m_ref[...], v_ref[...])
                go_ref[...] = g
                do_ref[...] = delta
                mo_ref[...] = m2
                vo_ref[...] = v2

    lrc = pl.BlockSpec((None, TR, C), lambda lay, r: (lay, r + b0, 0))
    g_specs = [pl.BlockSpec((n, TR, C), lambda lay, r, li=li: (0, jnp.where(lay == li, r, 0), 0))
               for li in range(L)]
    in_specs, args, aliases = g_specs + [lrc, lrc, lrc], [*gs, w, m, v], None
    if into is not None:
        aliases = {len(args) + k: k for k in range(4)}
        in_specs = in_specs + [pl.BlockSpec(memory_space=pl.ANY)] * 4
        args = args + list(into)
    return _pcall(
        body, name=name, grid=(L, Rp // TR), in_specs=in_specs, out_specs=[lrc] * 4,
        out_shape=[jax.ShapeDtypeStruct((L, R, C), F32)] * 4, args=args, aliases=aliases)


def _ada_adamw(c_all_t, dm, w, m, v, rider=None):
    L, _, nc = w.shape

    def body(c_ref, dm_ref, w_ref, m_ref, v_ref, go_ref, do_ref, mo_ref, vo_ref):
        cv = c_ref[...]
        ca = cv * jax.nn.sigmoid(cv)
        dmv = dm_ref[...]
        g = ca[:, 0:1] * dmv[0:1, :]
        for b in range(1, NDEV):
            g = g + ca[:, b:b + 1] * dmv[b:b + 1, :]
        delta, m2, v2 = _adamw_math(g, w_ref[...], m_ref[...], v_ref[...])
        go_ref[...] = g
        do_ref[...] = delta
        mo_ref[...] = m2
        vo_ref[...] = v2

    big = pl.BlockSpec((None, D, nc), lambda lay: (lay, 0, 0))
    return _pcall(
        body, name="ada_adamw", grid=(L,),
        in_specs=[_full((D, NDEV)), pl.BlockSpec((None, NDEV, nc), lambda lay: (lay, 0, 0)), big, big, big],
        out_specs=[big] * 4, out_shape=[jax.ShapeDtypeStruct((L, D, nc), F32)] * 4,
        args=(c_all_t, dm, w, m, v), rider=rider)


def _sum_slots(recv):
    n, R, C = recv.shape

    def body(r_ref, o_ref):
        acc = r_ref[0].astype(F32)
        for s in range(1, n):
            acc = acc + r_ref[s].astype(F32)
        o_ref[...] = acc

    return pl.pallas_call(body, name="sum_slots", out_shape=jax.ShapeDtypeStruct((R, C), F32),
                          compiler_params=_cparams())(recv)


def _pad_rows(a, rows):
    return jnp.pad(a, ((0, rows - a.shape[0]), (0, 0)))


def _pack_sharded_block(pool_w, pool_b, conv_w):
    return jnp.concatenate([pool_w.reshape(-1, PACK_C), _pad_rows(pool_b.reshape(-1, PACK_C), SUB),
                            _pad_rows(conv_w.reshape(-1, PACK_C), SUB)], axis=0)


def _unpack_sharded_block(p):
    n_pw = DEPTH * NGRP * (GD // NDEV)
    pool_w = p[:n_pw].reshape(DEPTH, NGRP, GD // NDEV, GD)
    pool_b = p[n_pw].reshape(DEPTH, NGRP, GD // NDEV)
    conv_w = p[n_pw + SUB:n_pw + SUB + DEPTH * CONV_K * (D // NDEV) // PACK_C].reshape(DEPTH, CONV_K, D // NDEV)
    return pool_w, pool_b, conv_w


def _blocks_of_full(pool_w, pool_b, conv_w):
    pw = pool_w.reshape(DEPTH, NGRP, NDEV, GD // NDEV, GD).transpose(2, 0, 1, 3, 4).reshape(NDEV, -1, PACK_C)
    pb = pool_b.reshape(DEPTH, NGRP, NDEV, GD // NDEV).transpose(2, 0, 1, 3).reshape(NDEV, -1, PACK_C)
    cw = conv_w.reshape(DEPTH, CONV_K, NDEV, D // NDEV).transpose(2, 0, 1, 3).reshape(NDEV, -1, PACK_C)
    pad = lambda a: jnp.pad(a, ((0, 0), (0, SUB - a.shape[1]), (0, 0)))
    return jnp.concatenate([pw, pad(pb), pad(cw)], axis=1)


def _full_of_blocks(p):
    n_pw = DEPTH * NGRP * (GD // NDEV)
    pool_w = p[:, :n_pw].reshape(NDEV, DEPTH, NGRP, GD // NDEV, GD).transpose(1, 2, 0, 3, 4)
    pool_b = p[:, n_pw].reshape(NDEV, DEPTH, NGRP, GD // NDEV).transpose(1, 2, 0, 3)
    n_cw = DEPTH * CONV_K * (D // NDEV) // PACK_C
    conv_w = p[:, n_pw + SUB:n_pw + SUB + n_cw].reshape(NDEV, DEPTH, CONV_K, D // NDEV).transpose(1, 2, 0, 3)
    return (pool_w.reshape(DEPTH, NGRP, GD, GD), pool_b.reshape(DEPTH, NGRP, GD),
            conv_w.reshape(DEPTH, CONV_K, D))


def _pack_replicated(t, keys, rows):
    p = jnp.concatenate([t[k].reshape(-1, PACK_C) for k in keys], axis=0)
    return _pad_rows(p, rows)


def _unpack_replicated(p, like, keys):
    out, r0 = {}, 0
    for k in keys:
        rows = like[k].size // PACK_C
        out[k] = p[r0:r0 + rows].reshape(like[k].shape)
        r0 += rows
    return out


def kernel(x, c, ada_w, ada_b, pre_norm_g, w_in, conv_w, conv_b, gate_a_w, gate_a_b, gate_x_w, gate_x_b, lru_lambda, pool_w, pool_b, pool_scale, w_out, post_norm_g, loss_target, m_ada_w, m_ada_b, m_pre_norm_g, m_w_in, m_conv_w, m_conv_b, m_gate_a_w, m_gate_a_b, m_gate_x_w, m_gate_x_b, m_lru_lambda, m_pool_w, m_pool_b, m_pool_scale, m_w_out, m_post_norm_g, v_ada_w, v_ada_b, v_pre_norm_g, v_w_in, v_conv_w, v_conv_b, v_gate_a_w, v_gate_a_b, v_gate_x_w, v_gate_x_b, v_lru_lambda, v_pool_w, v_pool_b, v_pool_scale, v_w_out, v_post_norm_g):
    W = dict(ada_w=ada_w, ada_b=ada_b, pre_norm_g=pre_norm_g, w_in=w_in, conv_w=conv_w, conv_b=conv_b,
             gate_a_w=gate_a_w, gate_a_b=gate_a_b, gate_x_w=gate_x_w, gate_x_b=gate_x_b, lru_lambda=lru_lambda,
             pool_w=pool_w, pool_b=pool_b, pool_scale=pool_scale, w_out=w_out, post_norm_g=post_norm_g)
    M = dict(ada_w=m_ada_w, ada_b=m_ada_b, pre_norm_g=m_pre_norm_g, w_in=m_w_in, conv_w=m_conv_w,
             conv_b=m_conv_b, gate_a_w=m_gate_a_w, gate_a_b=m_gate_a_b, gate_x_w=m_gate_x_w,
             gate_x_b=m_gate_x_b, lru_lambda=m_lru_lambda, pool_w=m_pool_w, pool_b=m_pool_b,
             pool_scale=m_pool_scale, w_out=m_w_out, post_norm_g=m_post_norm_g)
    V = dict(ada_w=v_ada_w, ada_b=v_ada_b, pre_norm_g=v_pre_norm_g, w_in=v_w_in, conv_w=v_conv_w,
             conv_b=v_conv_b, gate_a_w=v_gate_a_w, gate_a_b=v_gate_a_b, gate_x_w=v_gate_x_w,
             gate_x_b=v_gate_x_b, lru_lambda=v_lru_lambda, pool_w=v_pool_w, pool_b=v_pool_b,
             pool_scale=v_pool_scale, w_out=v_w_out, post_norm_g=v_post_norm_g)
    S = x.shape[1]
    me = 4 * lax.axis_index("x") + 2 * lax.axis_index("y") + lax.axis_index("c")
    xs = x.reshape(S, D)
    tgt = loss_target.reshape(S, D)
    nc = ada_w.shape[2]
    NB = w_in.shape[2]
    w_in_b, w_out_b = w_in.astype(BF16), w_out.astype(BF16)
    rows = lambda a: a.reshape(DEPTH, 1, D)
    P = dict(pre_norm_g=rows(pre_norm_g), conv_b=rows(conv_b), gate_a_b=rows(gate_a_b), gate_x_b=rows(gate_x_b),
             lru_lambda=rows(lru_lambda), pool_scale=rows(pool_scale), post_norm_g=rows(post_norm_g),
             gate_a_w=gate_a_w.astype(BF16), gate_x_w=gate_x_w.astype(BF16))

    c_slots, w_in0 = _exchange("gather_c_w_in0", _AllGatherVia([jnp.broadcast_to(c, (SUB, D)), w_in_b[0]]))
    c_all = c_slots[:, 0, :]
    (mod_slots,) = _exchange("gather_mod", _Direct(ag=[_mod_cols(c_all, ada_w)]))
    mod = lax.dynamic_index_in_dim(mod_slots, me, axis=1, keepdims=False)
    mod = (mod.reshape(NDEV, DEPTH, nc).transpose(1, 0, 2).reshape(DEPTH, 3 * D) + ada_b).reshape(DEPTH, 3, 1, D)

    w_in_all, w_out_all = [w_in0, None], [None, None]
    saved = []
    xl = xs
    flat = lambda w_slots: w_slots.reshape(2 * D, D)
    for l in range(DEPTH):
        rider = _AllGather2([_pack_sharded_block(pool_w, pool_b, conv_w), w_out_b[0]]) if l == 0 else None
        (h_a, h_b, proj), got = _pre_proj(xl, P, mod, w_in_all[l], l, rider=rider)
        if l == 0:
            pool_w_f, pool_b_f, conv_w_f = _full_of_blocks(got[0])
            P.update(conv_w=conv_w_f, pool_w=pool_w_f.astype(BF16), pool_b=rows(pool_b_f))
            w_out_all[0] = flat(got[1])
        rider = _AllGather2([w_in_b[1]]) if l == 0 else None
        (hs, yr, *fwd), got = _rnn_fwd(proj, P, l, rider=rider)
        if l == 0:
            w_in_all[1] = got[0]
        (yp,), _ = _pool_fwd(proj, P, l)
        if l == DEPTH - 1:
            (y, x_next, loss_acc), _ = _out_post(yr, yp, w_out_all[l], xl, mod, P, l, tgt)
        else:
            (y, x_next), got = _out_post(yr, yp, w_out_all[l], xl, mod, P, l, rider=_AllGather2([w_out_b[1]]))
            w_out_all[1] = flat(got[0])
        saved.append((xl, h_a, h_b, proj, hs, fwd, yr, yp, y))
        xl = x_next

    dxo = xl
    G = {k: [None] * DEPTH for k in WEIGHTS}
    dmod = [None] * DEPTH
    recv_in, recv_out = [[None, None] for _ in range(DEPTH)], [None] * DEPTH
    full = dict(conv_w=(CONV_K, D), pool_w=(NGRP, GD, GD), pool_b=(NGRP, GD))
    stack = lambda k: jnp.stack([g.reshape(full.get(k, W[k].shape[1:])) for g in G[k]])
    gw_bot_prev = None
    for l in reversed(range(DEPTH)):
        xin, h_a, h_b, proj, hs, fwd, yr, yp, y = saved[l]
        rider = _AllGather2([loss_acc]) if gw_bot_prev is None else _Direct(a2a=[gw_bot_prev])
        (dyr, dyp, gw_out, dgate, G['post_norm_g'][l]), got = _out_bwd(dxo, y, yr, yp, w_out_all[l], mod, P, l,
                                                                       rider=rider)
        if gw_bot_prev is None:
            loss = (0.5 / D) * jnp.sum(got[0])
        else:
            recv_in[l + 1][1] = got[0]
        ((dxr, dgr, G['conv_w'][l], G['conv_b'][l], G['gate_a_w'][l], G['gate_a_b'][l], G['gate_x_w'][l],
          G['gate_x_b'][l], G['lru_lambda'][l]), (recv_out[l],)) = _rnn_bwd(
            dyr, hs, fwd, proj, P, l,
            rider=_Direct(a2a=[gw_out.reshape(NDEV, 2 * D // NDEV, D)]))
        (dxp, dgp, G['pool_w'][l], G['pool_b'][l], G['pool_scale'][l]), _ = _pool_bwd(dyp, proj, P, l)
        dq = (dxr, dgr, dxp, dgp)
        if l > 0:
            (gw_top,), _ = _grad_w_in(h_a, dq, NB)
            (dxo, dshift, dscale, G['pre_norm_g'][l]), (recv_in[l][0],) = _in_bwd(
                dq, w_in_all[l], xin, dxo, P, mod, l, rider=_Direct(a2a=[gw_top]))
            (gw_bot_prev,), _ = _grad_w_in(h_b, dq, NB)
        else:
            Ge = {k: stack(k) for k in REP_EARLY + ['pool_w', 'pool_b', 'conv_w']}
            early = jnp.concatenate([_blocks_of_full(Ge['pool_w'], Ge['pool_b'], Ge['conv_w']),
                                     _pack_replicated(Ge, REP_EARLY, NDEV * PACK_ROWS).reshape(NDEV, PACK_ROWS, PACK_C)],
                                    axis=1).astype(BF16)
            (gw_top,), (early_recv,) = _grad_w_in(h_a, dq, NB, rider=_Direct(a2a=[early]))
            early_sum = _sum_slots(early_recv)
            (gw_bot,), (recv_in[l][0],) = _grad_w_in(h_b, dq, NB, rider=_Direct(a2a=[gw_top]))
            (dxo, dshift, dscale, G['pre_norm_g'][l]), (recv_in[l][1], early_all) = _in_bwd(
                dq, w_in_all[l], xin, dxo, P, mod, l,
                rider=_Both(_Direct(a2a=[gw_bot]), _AllGather2([early_sum[PACK_ROWS:]])))
        dmod[l] = jnp.concatenate([dshift, dscale, dgate], axis=1)
    grad_x = dxo.reshape(x.shape)

    Gl = dict(ada_b=jnp.concatenate(dmod, axis=0), pre_norm_g=stack('pre_norm_g'))
    (late_slots,) = _exchange("gather_late", _AllGather2([_pack_replicated(Gl, REP_LATE, LATE_PACK_ROWS)]))
    late_sum = _sum_slots(late_slots)
    dmod_all = late_slots[:, :DEPTH * 3 * D // PACK_C]

    out = {}
    first, _ = _adamw("adamw_w_in_a", [recv_in[l][0] for l in range(DEPTH)], w_in, M['w_in'], V['w_in'], LANE)
    out['w_in'], _ = _adamw("adamw_w_in_b", [recv_in[l][1] for l in range(DEPTH)], w_in, M['w_in'], V['w_in'], LANE,
                            row0=GW_SPLIT, into=first)
    out['w_out'], _ = _adamw("adamw_w_out", recv_out, w_out, M['w_out'], V['w_out'], 256)
    dm = lax.dynamic_slice_in_dim(dmod_all.reshape(NDEV, DEPTH, 3 * D), me * nc, nc, axis=2)
    out['ada_w'], _ = _ada_adamw(c_all.T, dm.transpose(1, 0, 2), ada_w, M['ada_w'], V['ada_w'])
    g_small = jnp.concatenate([early_sum[:PACK_ROWS], early_all.reshape(NDEV * PACK_ROWS, PACK_C), late_sum],
                              axis=0)

    def packs(T):
        return jnp.concatenate([_pack_sharded_block(T['pool_w'], T['pool_b'], T['conv_w']),
                                _pack_replicated(T, REP_EARLY, NDEV * PACK_ROWS),
                                _pack_replicated(T, REP_LATE, LATE_PACK_ROWS)], axis=0)[None]
    res_small, _ = _adamw("adamw_small", [g_small[None]], packs(W), packs(M), packs(V), g_small.shape[0] // 2)
    n_early = (1 + NDEV) * PACK_ROWS
    for idx in range(4):
        p = res_small[idx][0]
        pw_, pb_, cw_ = _unpack_sharded_block(p[:PACK_ROWS])
        rep = _unpack_replicated(p[PACK_ROWS:n_early], W, REP_EARLY)
        rep.update(_unpack_replicated(p[n_early:], W, REP_LATE))
        rep.update(pool_w=pw_, pool_b=pb_, conv_w=cw_)
        for k, a in rep.items():
            out.setdefault(k, [None] * 4)[idx] = a
    for k in ('w_in', 'w_out', 'ada_w'):
        out[k] = [a.reshape(W[k].shape) for a in out[k]]

    return (loss, grad_x, *[out[k][0] for k in WEIGHTS], *[out[k][1] for k in WEIGHTS],
            *[out[k][2] for k in WEIGHTS], *[out[k][3] for k in WEIGHTS])
```

```python
import functools

import jax
import jax.numpy as jnp
from jax import lax
from jax.experimental import pallas as pl
from jax.experimental.pallas import tpu as pltpu

F32, BF16 = jnp.float32, jnp.bfloat16
MESH = pl.DeviceIdType.MESH
HIGHEST = lax.Precision.HIGHEST

NDEV = 8
DEPTH = 2
D = 1024
NHEAD, HD = 8, 128
NGRP, GD = 4, 256
WINS = (2, 4, 8, 16)
CONV_K = 4
CONV_HALO = 8
POOL_HALO = 16
LRU_C = 8.0
NORM_EPS = 1e-6
ADAM_LR, ADAM_B1, ADAM_B2, ADAM_EPS, ADAM_WD, ADAM_STEP = 0.001, 0.9, 0.999, 1e-08, 0.01, 10
VMEM_LIMIT = 56 * 1024 * 1024
NQ = 4
SUB = 8
LANE = 128
PACK_C = 256
PACK_ROWS = 272

WEIGHTS = ['ada_w', 'ada_b', 'pre_norm_g', 'w_in', 'conv_w', 'conv_b', 'gate_a_w', 'gate_a_b', 'gate_x_w',
           'gate_x_b', 'lru_lambda', 'pool_w', 'pool_b', 'pool_scale', 'w_out', 'post_norm_g']
REP_EARLY = ['conv_b', 'gate_a_w', 'gate_a_b', 'gate_x_w', 'gate_x_b', 'lru_lambda', 'pool_scale', 'post_norm_g']
REP_LATE = ['ada_b', 'pre_norm_g']
GW_SPLIT = 512
LATE_PACK_ROWS = 32


def _cparams(*sem):
    return pltpu.CompilerParams(dimension_semantics=sem, vmem_limit_bytes=VMEM_LIMIT)


def _vec(l, k=None):
    if k is None:
        return pl.BlockSpec((None, 1, D), lambda *_: (l, 0, 0))
    return pl.BlockSpec((None, None, 1, D), lambda *_: (l, k, 0, 0))


def _layer(l, shape):
    nd = len(shape)
    return pl.BlockSpec((None,) + tuple(shape), lambda *_: (l,) + (0,) * nd)


def _full(shape):
    nd = len(shape)
    return pl.BlockSpec(shape, lambda *_: (0,) * nd)


def _rowsum8(z):
    return z.reshape(z.shape[0] // SUB, SUB, z.shape[1]).sum(axis=0)


def _sum8(acc):
    return jnp.sum(acc, axis=0, keepdims=True)


def _sigmoid(z):
    return 0.5 * jnp.tanh(0.5 * z) + 0.5


def _silu_parts(g):
    sg = _sigmoid(g)
    return g * sg, sg * (1.0 + g * (1.0 - sg))


def _one_minus_sq(a, log_a):
    z = 2.0 * log_a
    p = 1.0 / 24.0
    for k in (6.0, 2.0, 1.0):
        p = p * z + 1.0 / k
    return jnp.where(z > -0.03, -(p * z), 1.0 - a * a)


def _place():
    x, y, c = lax.axis_index("x"), lax.axis_index("y"), lax.axis_index("c")
    return x, y, c, 4 * x + 2 * y + c


class _Direct:
    def __init__(self, a2a=(), ag=()):
        self.arrays = list(a2a) + list(ag)
        self.n_a, self.n = len(a2a), len(self.arrays)
        self.out_shape = ([jax.ShapeDtypeStruct(a.shape, a.dtype) for a in a2a]
                          + [jax.ShapeDtypeStruct((NDEV,) + a.shape, a.dtype) for a in ag])
        self.scratch = [pltpu.SemaphoreType.DMA((self.n, NDEV - 1)), pltpu.SemaphoreType.DMA((self.n, NDEV - 1)),
                        pltpu.SemaphoreType.DMA((self.n,))]

    def _copies(self, ins, outs, sems):
        send_sems, recv_sems, local_sems = sems
        x, y, c, me = _place()
        local, remote = [], []
        for t in range(self.n):
            src = ins[t].at[me] if t < self.n_a else ins[t]
            local.append(pltpu.make_async_copy(src, outs[t].at[me], local_sems.at[t]))
        for r in range(1, NDEV):
            px = 1 - x if r & 4 else x
            py = 1 - y if r & 2 else y
            pc = 1 - c if r & 1 else c
            for t in range(self.n):
                src = ins[t].at[4 * px + 2 * py + pc] if t < self.n_a else ins[t]
                remote.append(pltpu.make_async_remote_copy(
                    src_ref=src, dst_ref=outs[t].at[me], send_sem=send_sems.at[t, r - 1],
                    recv_sem=recv_sems.at[t, r - 1], device_id=(px, py, pc), device_id_type=MESH))
        return local, remote

    def start(self, ins, outs, sems):
        local, remote = self._copies(ins, outs, sems)
        for cp in local + remote:
            cp.start()

    def finish(self, ins, outs, sems):
        local, remote = self._copies(ins, outs, sems)
        for cp in remote + local:
            cp.wait()


class _AllGather2:
    def __init__(self, arrays):
        self.arrays = list(arrays)
        self.n = len(self.arrays)
        self.out_shape = [jax.ShapeDtypeStruct((NDEV,) + a.shape, a.dtype) for a in self.arrays]
        self.scratch = [pltpu.SemaphoreType.DMA((self.n, NDEV - 1)), pltpu.SemaphoreType.DMA((self.n, NDEV - 1)),
                        pltpu.SemaphoreType.DMA((self.n,))]

    @staticmethod
    def _chips(x, y):
        return [(1 - x, y), (x, 1 - y), (1 - x, 1 - y)]

    def _copy(self, t, k, src, dst, to, sems):
        return pltpu.make_async_remote_copy(src_ref=src, dst_ref=dst, send_sem=sems[0].at[t, k],
                                            recv_sem=sems[1].at[t, k], device_id=to, device_id_type=MESH)

    def start(self, ins, outs, sems):
        x, y, c, me = _place()
        for t in range(self.n):
            pltpu.make_async_copy(ins[t], outs[t].at[me], sems[2].at[t]).start()
            self._copy(t, 0, ins[t], outs[t].at[me], (x, y, 1 - c), sems).start()
            for j, (px, py) in enumerate(self._chips(x, y)):
                self._copy(t, 1 + j, ins[t], outs[t].at[me], (px, py, c), sems).start()

    def middle(self, ins, outs, sems):
        x, y, c, me = _place()
        sib = (x, y, 1 - c)
        for j, (px, py) in enumerate(self._chips(x, y)):
            slot = 4 * px + 2 * py + c
            for t in range(self.n):
                self._copy(t, 1 + j, ins[t], outs[t].at[slot], sib, sems).wait_recv()
                self._copy(t, 4 + j, outs[t].at[slot], outs[t].at[slot], sib, sems).start()

    def finish(self, ins, outs, sems):
        x, y, c, me = _place()
        sib = (x, y, 1 - c)
        for t in range(self.n):
            for k in (0, 4, 5, 6):
                self._copy(t, k, ins[t], outs[t].at[me], sib, sems).wait_recv()
        for t in range(self.n):
            for k in range(NDEV - 1):
                self._copy(t, k, ins[t], outs[t].at[me], sib, sems).wait_send()
            pltpu.make_async_copy(ins[t], outs[t].at[me], sems[2].at[t]).wait()


class _AllGatherVia:
    def __init__(self, arrays):
        self.arrays = list(arrays)
        self.n = len(self.arrays)
        self.out_shape = [jax.ShapeDtypeStruct((NDEV,) + a.shape, a.dtype) for a in self.arrays]
        self.scratch = [pltpu.SemaphoreType.DMA((self.n, NDEV - 1)), pltpu.SemaphoreType.DMA((self.n, NDEV - 1)),
                        pltpu.SemaphoreType.DMA((self.n,))]

    def _copy(self, t, k, src, dst, to, sems):
        return pltpu.make_async_remote_copy(src_ref=src, dst_ref=dst, send_sem=sems[0].at[t, k],
                                            recv_sem=sems[1].at[t, k], device_id=to, device_id_type=MESH)

    def start(self, ins, outs, sems):
        x, y, c, me = _place()
        for t in range(self.n):
            pltpu.make_async_copy(ins[t], outs[t].at[me], sems[2].at[t]).start()
            self._copy(t, 0, ins[t], outs[t].at[me], (x, y, 1 - c), sems).start()
            self._copy(t, 1, ins[t], outs[t].at[me], (1 - x, y, c), sems).start()
            self._copy(t, 2, ins[t], outs[t].at[me], (x, 1 - y, c), sems).start()

    def finish(self, ins, outs, sems):
        x, y, c, me = _place()
        sib = (x, y, 1 - c)
        slot_x, slot_y, slot_d = 4 * (1 - x) + 2 * y + c, 4 * x + 2 * (1 - y) + c, 4 * (1 - x) + 2 * (1 - y) + c
        slot_on = c * slot_x + (1 - c) * slot_y
        to_on = (c * x + (1 - c) * (1 - x), c * (1 - y) + (1 - c) * y, c)
        for t in range(self.n):
            self._copy(t, 1, ins[t], outs[t].at[slot_x], sib, sems).wait_recv()
            self._copy(t, 2, ins[t], outs[t].at[slot_y], sib, sems).wait_recv()
            self._copy(t, 3, outs[t].at[slot_on], outs[t].at[slot_on], to_on, sems).start()
            self._copy(t, 4, outs[t].at[slot_x], outs[t].at[slot_x], sib, sems).start()
            self._copy(t, 5, outs[t].at[slot_y], outs[t].at[slot_y], sib, sems).start()
        for t in range(self.n):
            self._copy(t, 3, ins[t], outs[t].at[slot_d], sib, sems).wait_recv()
            self._copy(t, 6, outs[t].at[slot_d], outs[t].at[slot_d], sib, sems).start()
        for t in range(self.n):
            for k in (0, 4, 5, 6):
                self._copy(t, k, ins[t], outs[t].at[me], sib, sems).wait_recv()
        for t in range(self.n):
            for k in range(NDEV - 1):
                self._copy(t, k, ins[t], outs[t].at[me], sib, sems).wait_send()
            pltpu.make_async_copy(ins[t], outs[t].at[me], sems[2].at[t]).wait()


class _Both:
    def __init__(self, *riders):
        self.riders = riders
        self.arrays = [a for r in riders for a in r.arrays]
        self.n = len(self.arrays)
        self.out_shape = [o for r in riders for o in r.out_shape]
        self.scratch = [s for r in riders for s in r.scratch]

    def _parts(self, ins, outs, sems):
        p, q = 0, 0
        for r in self.riders:
            yield r, ins[p:p + r.n], outs[p:p + r.n], sems[q:q + len(r.scratch)]
            p, q = p + r.n, q + len(r.scratch)

    def start(self, ins, outs, sems):
        for r, i, o, s in self._parts(ins, outs, sems):
            r.start(i, o, s)

    def middle(self, ins, outs, sems):
        for r, i, o, s in self._parts(ins, outs, sems):
            if hasattr(r, "middle"):
                r.middle(i, o, s)

    def finish(self, ins, outs, sems):
        for r, i, o, s in self._parts(ins, outs, sems):
            r.finish(i, o, s)


def _exchange(name, rider):
    n = rider.n

    def body(*refs):
        rider.start(refs[:n], refs[n:2 * n], refs[2 * n:])
        if hasattr(rider, "middle"):
            rider.middle(refs[:n], refs[n:2 * n], refs[2 * n:])
        rider.finish(refs[:n], refs[n:2 * n], refs[2 * n:])

    any_spec = pl.BlockSpec(memory_space=pl.ANY)
    return list(pl.pallas_call(body, name=name, out_shape=rider.out_shape, in_specs=[any_spec] * n,
                               out_specs=[any_spec] * n, scratch_shapes=rider.scratch)(*rider.arrays))


def _pcall(body, *, name, grid, in_specs, out_specs, out_shape, args, scratch_shapes=(), rider=None, aliases=None):
    params = _cparams(*(("arbitrary",) * len(grid)))
    if rider is None:
        res = pl.pallas_call(body, name=name, grid=grid, in_specs=in_specs, out_specs=out_specs,
                             out_shape=out_shape, scratch_shapes=list(scratch_shapes),
                             input_output_aliases=aliases or {}, compiler_params=params)(*args)
        return list(res), []
    n_in, n_out, n_scr, rn = len(in_specs), len(out_specs), len(scratch_shapes), rider.n

    def wrapped(*refs):
        cuts = [n_in, rn, n_out, rn, n_scr]
        parts, p = [], 0
        for n in cuts:
            parts.append(refs[p:p + n])
            p += n
        ins, r_in, outs, r_out, scr = parts
        sems = refs[p:]
        ids = [pl.program_id(a) for a in range(len(grid))]
        first = functools.reduce(jnp.logical_and, [i == 0 for i in ids])
        last = functools.reduce(jnp.logical_and, [i == g - 1 for i, g in zip(ids, grid)])

        @pl.when(first)
        def _():
            rider.start(r_in, r_out, sems)
        body(*ins, *outs, *scr)
        early = hasattr(rider, "middle") and len(grid) == 1 and grid[0] >= 4
        if early:
            @pl.when(ids[0] == (3 * grid[0]) // 4)
            def _():
                rider.middle(r_in, r_out, sems)

        @pl.when(last)
        def _():
            if hasattr(rider, "middle") and not early:
                rider.middle(r_in, r_out, sems)
            rider.finish(r_in, r_out, sems)

    any_spec = pl.BlockSpec(memory_space=pl.ANY)
    res = pl.pallas_call(
        wrapped, name=name, grid=grid, in_specs=list(in_specs) + [any_spec] * rn,
        out_specs=list(out_specs) + [any_spec] * rn, out_shape=list(out_shape) + rider.out_shape,
        scratch_shapes=list(scratch_shapes) + rider.scratch, compiler_params=params)(*args, *rider.arrays)
    return list(res[:n_out]), list(res[n_out:])


def _mod_cols(c_all, ada_w):
    nc = ada_w.shape[2]

    def body(c_ref, w_ref, o_ref):
        cv = c_ref[...]
        ca = cv * jax.nn.sigmoid(cv)
        for l in range(DEPTH):
            o_ref[:, l * nc:(l + 1) * nc] = jnp.dot(ca, w_ref[l], precision=HIGHEST, preferred_element_type=F32)

    return pl.pallas_call(body, name="mod_cols", out_shape=jax.ShapeDtypeStruct((NDEV, DEPTH * nc), F32),
                          compiler_params=_cparams())(c_all, ada_w)


def _pre_proj(x, P, mod, w_in_l, l, rider=None):
    S = x.shape[0]
    TM = min(512, S)
    NB = w_in_l.shape[2]

    def body(x_ref, g_ref, sc_ref, sh_ref, w_ref, ha_ref, hb_ref, p_ref):
        xv = x_ref[...]
        rstd = lax.rsqrt(jnp.mean(xv * xv, axis=-1, keepdims=True) + NORM_EPS)
        h = ((xv * rstd * g_ref[...]) * (1.0 + sc_ref[...]) + sh_ref[...]).astype(BF16)
        ha_ref[...] = h[:, :GW_SPLIT]
        hb_ref[...] = h[:, GW_SPLIT:]
        for j in range(NDEV):
            p_ref[:, j * NB:(j + 1) * NB] = jnp.dot(h, w_ref[j], preferred_element_type=F32)

    row = pl.BlockSpec((TM, D), lambda i: (i, 0))
    return _pcall(
        body, name="pre_proj", grid=(S // TM,),
        in_specs=[row, _vec(l), _vec(l, 1), _vec(l, 0), _full((NDEV, D, NB))],
        out_specs=[pl.BlockSpec((TM, GW_SPLIT), lambda i: (i, 0)), pl.BlockSpec((TM, D - GW_SPLIT), lambda i: (i, 0)),
                   pl.BlockSpec((TM, NDEV * NB), lambda i: (i, 0))],
        out_shape=[jax.ShapeDtypeStruct((S, GW_SPLIT), BF16), jax.ShapeDtypeStruct((S, D - GW_SPLIT), BF16),
                   jax.ShapeDtypeStruct((S, NDEV * NB), F32)],
        args=(x, P['pre_norm_g'], mod, mod, w_in_l), rider=rider)


def _taps(E):
    return [pltpu.roll(E, CONV_K - 1 - k, axis=0)[CONV_HALO:, :] for k in range(CONV_K - 1)] + [E[CONV_HALO:, :]]


def _conv(E, cw_ref, cb_ref):
    w = cw_ref[...]
    taps = _taps(E)
    acc = cb_ref[...] + taps[0] * w[0:1, :]
    for k in range(1, CONV_K):
        acc = acc + taps[k] * w[k:k + 1, :]
    return acc


CH_R, CH_C = 16, 512


def _chunks(T, fn):
    def step(c, carry):
        rows = pl.ds(pl.multiple_of(c * CH_R, CH_R), CH_R)
        for hf in range(D // CH_C):
            fn(rows, slice(hf * CH_C, (hf + 1) * CH_C), hf)
        return carry
    lax.fori_loop(0, T // CH_R, step, 0)


def _to_scan(ref, rows, hf, val):
    for q in range(CH_C // LANE):
        ref[hf * (CH_C // LANE) + q, rows, :] = val[:, q * LANE:(q + 1) * LANE]


def _scan(sa, sv, carry_ref, out_ref, reverse):
    NC, T, _ = sa.shape
    n8 = T // SUB
    rows = range(SUB - 2, -1, -1) if reverse else range(1, SUB)
    for cb in range(NC):
        r_in = SUB - 1 if reverse else 0
        Ap = sa[cb, pl.ds(r_in, n8, stride=SUB), :]
        Vp = sv[cb, pl.ds(r_in, n8, stride=SUB), :]
        for r in rows:
            Ar = sa[cb, pl.ds(r, n8, stride=SUB), :]
            Vp = sv[cb, pl.ds(r, n8, stride=SUB), :] + Ar * Vp
            Ap = Ar * Ap
            sa[cb, pl.ds(r, n8, stride=SUB), :] = Ap
            sv[cb, pl.ds(r, n8, stride=SUB), :] = Vp
    edge = 0 if reverse else SUB - 1

    def step(k, c):
        r0 = pl.multiple_of((n8 - 1 - k if reverse else k) * SUB, SUB)
        h = jnp.concatenate([sv[cb, pl.ds(r0, SUB), :] + sa[cb, pl.ds(r0, SUB), :] * c[:, cb * LANE:(cb + 1) * LANE]
                             for cb in range(NC)], axis=1)
        out_ref[pl.ds(r0, SUB), :] = h
        return jnp.broadcast_to(h[edge:edge + 1, :], (SUB, D))

    carry_ref[...] = lax.fori_loop(0, n8, step, carry_ref[...])


def _rnn_fwd(proj, P, l, rider=None):
    S = proj.shape[0]
    TB = min(256, S)

    def body(xr_ref, g_ref, cw_ref, cb_ref, wa_ref, ba_ref, wx_ref, bx_ref, lam_ref, hs_ref, y_ref,
             u_ref, r_ref, i_ref, a_ref, m_ref, xbuf, sa, sv, hc):
        @pl.when(pl.program_id(0) == 0)
        def _():
            xbuf[0:CONV_HALO, :] = jnp.zeros((CONV_HALO, D), F32)
            hc[...] = jnp.zeros((SUB, D), F32)
        xbuf[CONV_HALO:, :] = xr_ref[...]
        u = _conv(xbuf[...], cw_ref, cb_ref)
        xbuf[0:CONV_HALO, :] = xbuf[TB:TB + CONV_HALO, :]
        ub = u.astype(BF16)
        zr = jnp.concatenate([jnp.dot(ub[:, h * HD:(h + 1) * HD], wa_ref[h], preferred_element_type=F32)
                              for h in range(NHEAD)], axis=1)
        zi = jnp.concatenate([jnp.dot(ub[:, h * HD:(h + 1) * HD], wx_ref[h], preferred_element_type=F32)
                              for h in range(NHEAD)], axis=1)
        r = _sigmoid(zr + ba_ref[...])
        ig = _sigmoid(zi + bx_ref[...])
        log_a = r * (-LRU_C * jax.nn.softplus(-lam_ref[...]))
        a = jnp.exp(log_a)
        mult = jnp.sqrt(_one_minus_sq(a, log_a))
        v = mult * (ig * u)
        u_ref[...] = u
        r_ref[...] = r
        i_ref[...] = ig
        a_ref[...] = a
        m_ref[...] = mult
        for cb in range(D // LANE):
            sa[cb] = a[:, cb * LANE:(cb + 1) * LANE]
            sv[cb] = v[:, cb * LANE:(cb + 1) * LANE]
        _scan(sa, sv, hc, hs_ref, reverse=False)
        silu, _ = _silu_parts(g_ref[...])
        y_ref[...] = (hs_ref[...] * silu).astype(BF16)

    rowb = pl.BlockSpec((TB, D), lambda i: (i, 0))
    return _pcall(
        body, name="rnn_fwd", grid=(S // TB,),
        in_specs=[rowb, pl.BlockSpec((TB, D), lambda i: (i, 1)), _layer(l, (CONV_K, D)), _vec(l),
                  _layer(l, (NHEAD, HD, HD)), _vec(l), _layer(l, (NHEAD, HD, HD)), _vec(l), _vec(l)],
        out_specs=[rowb] * 7,
        out_shape=[jax.ShapeDtypeStruct((S, D), F32), jax.ShapeDtypeStruct((S, D), BF16)]
        + [jax.ShapeDtypeStruct((S, D), F32)] * 5,
        scratch_shapes=[pltpu.VMEM((TB + CONV_HALO, D), F32), pltpu.VMEM((D // LANE, TB, LANE), F32),
                        pltpu.VMEM((D // LANE, TB, LANE), F32), pltpu.VMEM((SUB, D), F32)],
        args=(proj, proj, P['conv_w'], P['conv_b'], P['gate_a_w'], P['gate_a_b'], P['gate_x_w'], P['gate_x_b'],
              P['lru_lambda']), rider=rider)


def _pooled(ebuf, t0, TB):
    tt = t0 + lax.broadcasted_iota(jnp.int32, (TB, 1), 0)
    pooled, inv = [], []
    for g, win in enumerate(WINS):
        Eg = ebuf[:, g * GD:(g + 1) * GD]
        L = Eg
        for lev in range(g + 1):
            L = L + pltpu.roll(L, 1 << lev, axis=0)
        icnt = 1.0 / jnp.minimum(tt + 1, win).astype(F32)
        pooled.append(L[POOL_HALO:, :] * icnt - Eg[POOL_HALO:, :])
        inv.append(icnt)
    return pooled, inv


def _pool_fwd(proj, P, l, rider=None):
    S = proj.shape[0]
    TB = min(512, S)

    def body(xp_ref, g_ref, pw_ref, pb_ref, ps_ref, y_ref, ebuf):
        i = pl.program_id(0)

        @pl.when(i == 0)
        def _():
            ebuf[0:POOL_HALO, :] = jnp.zeros((POOL_HALO, D), F32)
        ebuf[POOL_HALO:, :] = xp_ref[...]
        pooled, _ = _pooled(ebuf, i * TB, TB)
        ebuf[0:POOL_HALO, :] = ebuf[TB:TB + POOL_HALO, :]
        yp = jnp.concatenate([jnp.dot(pooled[g].astype(BF16), pw_ref[g], preferred_element_type=F32)
                              for g in range(NGRP)], axis=1) + pb_ref[...]
        silu, _ = _silu_parts(g_ref[...])
        y_ref[...] = (yp * ps_ref[...] * silu).astype(BF16)

    return _pcall(
        body, name="pool_fwd", grid=(S // TB,),
        in_specs=[pl.BlockSpec((TB, D), lambda i: (i, 2)), pl.BlockSpec((TB, D), lambda i: (i, 3)),
                  _layer(l, (NGRP, GD, GD)), _vec(l), _vec(l)],
        out_specs=[pl.BlockSpec((TB, D), lambda i: (i, 0))],
        out_shape=[jax.ShapeDtypeStruct((S, D), BF16)],
        scratch_shapes=[pltpu.VMEM((TB + POOL_HALO, D), F32)],
        args=(proj, proj, P['pool_w'], P['pool_b'], P['pool_scale']), rider=rider)


def _out_post(yr, yp, w_out_l, x, mod, P, l, target=None, rider=None):
    S = x.shape[0]
    TM = min(512, S)
    last = target is not None

    def body(*refs):
        if last:
            yr_ref, yp_ref, w_ref, x_ref, gate_ref, gp_ref, t_ref, y_ref, xo_ref, loss_ref = refs
        else:
            yr_ref, yp_ref, w_ref, x_ref, gate_ref, gp_ref, y_ref, xo_ref = refs
        acc = (jnp.dot(yr_ref[...], w_ref[0:D, :], preferred_element_type=F32)
               + jnp.dot(yp_ref[...], w_ref[D:2 * D, :], preferred_element_type=F32))
        y_ref[...] = acc
        rstd = lax.rsqrt(jnp.mean(acc * acc, axis=-1, keepdims=True) + NORM_EPS)
        xn = x_ref[...] + gate_ref[...] * (acc * rstd * gp_ref[...])
        if last:
            err = xn - t_ref[...]
            xo_ref[...] = err * (1.0 / D)

            @pl.when(pl.program_id(0) == 0)
            def _():
                loss_ref[...] = jnp.zeros((SUB, D), F32)
            loss_ref[...] += _rowsum8(err * err)
        else:
            xo_ref[...] = xn

    row = pl.BlockSpec((TM, D), lambda i: (i, 0))
    in_specs = [row, row, _full((2 * D, D)), row, _vec(l, 2), _vec(l)]
    out_specs = [row, row]
    out_shape = [jax.ShapeDtypeStruct((S, D), F32), jax.ShapeDtypeStruct((S, D), F32)]
    args = [yr, yp, w_out_l, x, mod, P['post_norm_g']]
    if last:
        in_specs.append(row)
        out_specs.append(_full((SUB, D)))
        out_shape.append(jax.ShapeDtypeStruct((SUB, D), F32))
        args.append(target)
    return _pcall(body, name="out_post_loss" if last else "out_post", grid=(S // TM,), in_specs=in_specs,
                  out_specs=out_specs, out_shape=out_shape, args=args, rider=rider)


def _out_bwd(dxo, y, yr, yp, w_out_l, mod, P, l, rider=None):
    S = y.shape[0]
    TM = min(512, S)
    nsteps = S // TM

    def body(dxo_ref, y_ref, yr_ref, yp_ref, w_ref, gate_ref, gp_ref, dyr_ref, dyp_ref, gw_ref, dgate_ref,
             dgp_ref, gw_acc, vacc):
        i = pl.program_id(0)

        @pl.when(i == 0)
        def _():
            gw_acc[...] = jnp.zeros_like(gw_acc)
            vacc[...] = jnp.zeros_like(vacc)
        yv = y_ref[...]
        dxo_v = dxo_ref[...]
        rstd = lax.rsqrt(jnp.mean(yv * yv, axis=-1, keepdims=True) + NORM_EPS)
        n = yv * rstd
        gp = gp_ref[...]
        vacc[0] += _rowsum8(dxo_v * (n * gp))
        drn = dxo_v * gate_ref[...]
        vacc[1] += _rowsum8(drn * n)
        dn = drn * gp
        dy = (rstd * (dn - n * jnp.mean(dn * n, axis=-1, keepdims=True))).astype(BF16)
        dyr_ref[...] = lax.dot_general(dy, w_ref[0:D, :], (((1,), (1,)), ((), ())), preferred_element_type=F32)
        dyp_ref[...] = lax.dot_general(dy, w_ref[D:2 * D, :], (((1,), (1,)), ((), ())),
                                       preferred_element_type=F32)
        gw_acc[0:D, :] += lax.dot_general(yr_ref[...], dy, (((0,), (0,)), ((), ())), preferred_element_type=F32)
        gw_acc[D:2 * D, :] += lax.dot_general(yp_ref[...], dy, (((0,), (0,)), ((), ())),
                                              preferred_element_type=F32)

        @pl.when(i == nsteps - 1)
        def _():
            gw_ref[...] = gw_acc[...].astype(BF16)
            dgate_ref[...] = _sum8(vacc[0])
            dgp_ref[...] = _sum8(vacc[1])

    row = pl.BlockSpec((TM, D), lambda i: (i, 0))
    return _pcall(
        body, name="out_bwd", grid=(nsteps,),
        in_specs=[row, row, row, row, _full((2 * D, D)), _vec(l, 2), _vec(l)],
        out_specs=[row, row, _full((2 * D, D)), _full((1, D)), _full((1, D))],
        out_shape=[jax.ShapeDtypeStruct((S, D), F32), jax.ShapeDtypeStruct((S, D), F32),
                   jax.ShapeDtypeStruct((2 * D, D), BF16), jax.ShapeDtypeStruct((1, D), F32),
                   jax.ShapeDtypeStruct((1, D), F32)],
        scratch_shapes=[pltpu.VMEM((2 * D, D), F32), pltpu.VMEM((2, SUB, D), F32)],
        args=(dxo, y, yr, yp, w_out_l, mod, P['post_norm_g']), rider=rider)


def _rnn_bwd(dyr, hs, fwd, proj, P, l, rider=None):
    S = proj.shape[0]
    TB = min(256, S)
    nb = S // TB
    TE = TB + CONV_HALO
    A_BA, A_BX, A_LAM, A_CB, A_CW = 0, 1, 2, 3, 4

    def blk(i):
        return nb - 1 - i

    def body(dyr_ref, hs_ref, hprev_ref, u_ref, r_ref, i_ref, a_ref, m_ref, xr_ref, g_ref, cw_ref,
             wa_ref, wx_ref, lam_ref, dxr_ref, dg_ref, gcw_ref, gcb_ref, gwa_ref, gba_ref, gwx_ref, gbx_ref,
             glam_ref, hbuf, abuf, dbuf, sa, sv, dh_ref, hp_ref, dzr_ref, dzi_ref, dhc, vacc):
        i = pl.program_id(0)
        first = blk(i) == 0

        @pl.when(i == 0)
        def _():
            abuf[TB:, :] = jnp.zeros((CONV_HALO, D), F32)
            dbuf[TB:, :] = jnp.zeros((CONV_HALO, D), F32)
            dhc[...] = jnp.zeros_like(dhc)
            vacc[...] = jnp.zeros_like(vacc)
            gwa_ref[...] = jnp.zeros_like(gwa_ref)
            gwx_ref[...] = jnp.zeros_like(gwx_ref)

        hbuf[0:CONV_HALO, :] = jnp.where(first, 0.0, hprev_ref[...])
        hbuf[CONV_HALO:, :] = hs_ref[...]
        hp_ref[...] = pltpu.roll(hbuf[...], 1, axis=0)[CONV_HALO:, :]
        abuf[0:TB, :] = a_ref[...]
        b = pltpu.roll(abuf[...], TE - 1, axis=0)[0:TB, :]
        for cb in range(D // LANE):
            sa[cb] = b[:, cb * LANE:(cb + 1) * LANE]
        abuf[TB:, :] = a_ref[0:CONV_HALO, :]

        def gate_bwd(rows, cs, hf):
            silu, dsilu = _silu_parts(g_ref[rows, cs])
            dy = dyr_ref[rows, cs]
            dg_ref[rows, cs] = (dy * hs_ref[rows, cs] * dsilu).astype(BF16)
            _to_scan(sv, rows, hf, dy * silu)
        _chunks(TB, gate_bwd)

        _scan(sa, sv, dhc, dh_ref, reverse=True)

        csp = -LRU_C * jax.nn.softplus(-lam_ref[...])

        def lru_bwd(rows, cs, hf):
            dh, a, ig, u, mult, r = dh_ref[rows, cs], a_ref[rows, cs], i_ref[rows, cs], u_ref[rows, cs], \
                m_ref[rows, cs], r_ref[rows, cs]
            dlog_a = dh * hp_ref[rows, cs] * a - (dh * ig * u) * (a * a) / mult
            dzr = dlog_a * csp[:, cs] * r * (1.0 - r)
            dzi = (dh * mult * u) * ig * (1.0 - ig)
            dzr_ref[rows, cs] = dzr.astype(BF16)
            dzi_ref[rows, cs] = dzi.astype(BF16)
            dbuf[rows, cs] = dh * mult * ig
            vacc[A_LAM, :, cs] += _rowsum8(dlog_a * r)
            vacc[A_BA, :, cs] += _rowsum8(dzr)
            vacc[A_BX, :, cs] += _rowsum8(dzi)
        _chunks(TB, lru_bwd)

        ub, dzrb, dzib = u_ref[...].astype(BF16), dzr_ref[...], dzi_ref[...]
        du_g = []
        for h in range(NHEAD):
            cs = slice(h * HD, (h + 1) * HD)
            gwa_ref[h] += lax.dot_general(ub[:, cs], dzrb[:, cs], (((0,), (0,)), ((), ())),
                                          preferred_element_type=F32)
            gwx_ref[h] += lax.dot_general(ub[:, cs], dzib[:, cs], (((0,), (0,)), ((), ())),
                                          preferred_element_type=F32)
            du_g.append(lax.dot_general(dzrb[:, cs], wa_ref[h], (((1,), (1,)), ((), ())),
                                        preferred_element_type=F32)
                        + lax.dot_general(dzib[:, cs], wx_ref[h], (((1,), (1,)), ((), ())),
                                          preferred_element_type=F32))
        du = dbuf[0:TB, :] + jnp.concatenate(du_g, axis=1)
        dbuf[0:TB, :] = du

        Dd = dbuf[...]
        w = cw_ref[...]
        xv = xr_ref[...]
        dx = du * w[CONV_K - 1:CONV_K, :]
        vacc[A_CB] += _rowsum8(du)
        vacc[A_CW + CONV_K - 1] += _rowsum8(xv * du)
        for k in range(CONV_K - 1):
            ahead = pltpu.roll(Dd, TE - (CONV_K - 1 - k), axis=0)[0:TB, :]
            dx = dx + ahead * w[k:k + 1, :]
            vacc[A_CW + k] += _rowsum8(xv * ahead)
        dxr_ref[...] = dx.astype(BF16)
        dbuf[TB:, :] = du[0:CONV_HALO, :]

        @pl.when(i == nb - 1)
        def _():
            gba_ref[...] = _sum8(vacc[A_BA])
            gbx_ref[...] = _sum8(vacc[A_BX])
            glam_ref[...] = _sum8(vacc[A_LAM]) * (LRU_C * _sigmoid(-lam_ref[...]))
            gcb_ref[...] = _sum8(vacc[A_CB])
            for k in range(CONV_K):
                gcw_ref[k:k + 1, :] = _sum8(vacc[A_CW + k])

    rowb = pl.BlockSpec((TB, D), lambda i: (blk(i), 0))
    halo = pl.BlockSpec((SUB, D), lambda i: (jnp.maximum(blk(i) * (TB // SUB) - 1, 0), 0))
    wspec = _full((NHEAD, HD, HD))
    wlay = _layer(l, (NHEAD, HD, HD))
    vec1 = _full((1, D))
    return _pcall(
        body, name="rnn_bwd", grid=(nb,),
        in_specs=[rowb, rowb, halo] + [rowb] * 5 + [rowb, pl.BlockSpec((TB, D), lambda i: (blk(i), 1)),
                                                    _layer(l, (CONV_K, D)), wlay, wlay, _vec(l)],
        out_specs=[rowb, rowb, _full((CONV_K, D)), vec1, wspec, vec1, wspec, vec1, vec1],
        out_shape=[jax.ShapeDtypeStruct((S, D), BF16), jax.ShapeDtypeStruct((S, D), BF16),
                   jax.ShapeDtypeStruct((CONV_K, D), F32), jax.ShapeDtypeStruct((1, D), F32),
                   jax.ShapeDtypeStruct((NHEAD, HD, HD), F32), jax.ShapeDtypeStruct((1, D), F32),
                   jax.ShapeDtypeStruct((NHEAD, HD, HD), F32), jax.ShapeDtypeStruct((1, D), F32),
                   jax.ShapeDtypeStruct((1, D), F32)],
        scratch_shapes=[pltpu.VMEM((TE, D), F32), pltpu.VMEM((TE, D), F32), pltpu.VMEM((TE, D), F32),
                        pltpu.VMEM((D // LANE, TB, LANE), F32), pltpu.VMEM((D // LANE, TB, LANE), F32),
                        pltpu.VMEM((TB, D), F32), pltpu.VMEM((TB, D), F32), pltpu.VMEM((TB, D), BF16),
                        pltpu.VMEM((TB, D), BF16), pltpu.VMEM((SUB, D), F32),
                        pltpu.VMEM((A_CW + CONV_K, SUB, D), F32)],
        args=(dyr, hs, hs, *fwd, proj, proj, P['conv_w'], P['gate_a_w'], P['gate_x_w'], P['lru_lambda']),
        rider=rider)


def _pool_bwd(dyp, proj, P, l, rider=None):
    S = proj.shape[0]
    TB = min(512, S)
    nb = S // TB
    TE = TB + POOL_HALO

    def blk(i):
        return nb - 1 - i

    def body(dy_ref, xp_ref, xprev_ref, g_ref, pw_ref, pb_ref, ps_ref, dxp_ref, dg_ref, gpw_ref, gpb_ref,
             gps_ref, ebuf, qbuf, vacc):
        i = pl.program_id(0)
        first = blk(i) == 0

        @pl.when(i == 0)
        def _():
            qbuf[TB:, :] = jnp.zeros((POOL_HALO, D), F32)
            vacc[...] = jnp.zeros_like(vacc)
            gpw_ref[...] = jnp.zeros_like(gpw_ref)

        ebuf[0:POOL_HALO, :] = jnp.where(first, 0.0, xprev_ref[...])
        ebuf[POOL_HALO:, :] = xp_ref[...]
        pooled, inv = _pooled(ebuf, blk(i) * TB, TB)
        pooled = [p.astype(BF16) for p in pooled]
        yp = jnp.concatenate([jnp.dot(pooled[g], pw_ref[g], preferred_element_type=F32)
                              for g in range(NGRP)], axis=1) + pb_ref[...]
        silu, dsilu = _silu_parts(g_ref[...])
        dy = dy_ref[...]
        ps = ps_ref[...]
        dyp_v = dy * ps * silu
        vacc[0] += _rowsum8(dy * yp * silu)
        vacc[1] += _rowsum8(dyp_v)
        dg_ref[...] = (dy * yp * ps * dsilu).astype(BF16)
        dypb = dyp_v.astype(BF16)
        for g in range(NGRP):
            cs = slice(g * GD, (g + 1) * GD)
            gpw_ref[g] += lax.dot_general(pooled[g], dypb[:, cs], (((0,), (0,)), ((), ())),
                                          preferred_element_type=F32)
            dpool = lax.dot_general(dypb[:, cs], pw_ref[g], (((1,), (1,)), ((), ())),
                                    preferred_element_type=F32)
            qbuf[0:TB, cs] = dpool * inv[g]
            L = qbuf[:, cs]
            for lev in range(g + 1):
                L = L + pltpu.roll(L, TE - (1 << lev), axis=0)
            dxp_ref[:, cs] = (L[0:TB, :] - dpool).astype(BF16)
        qbuf[TB:, :] = qbuf[0:POOL_HALO, :]

        @pl.when(i == nb - 1)
        def _():
            gps_ref[...] = _sum8(vacc[0])
            gpb_ref[...] = _sum8(vacc[1])

    rowb = pl.BlockSpec((TB, D), lambda i: (blk(i), 0))
    return _pcall(
        body, name="pool_bwd", grid=(nb,),
        in_specs=[rowb, pl.BlockSpec((TB, D), lambda i: (blk(i), 2)),
                  pl.BlockSpec((POOL_HALO, D), lambda i: (jnp.maximum(blk(i) * (TB // POOL_HALO) - 1, 0), 2)),
                  pl.BlockSpec((TB, D), lambda i: (blk(i), 3)), _layer(l, (NGRP, GD, GD)), _vec(l), _vec(l)],
        out_specs=[rowb, rowb, _full((NGRP, GD, GD)), _full((1, D)), _full((1, D))],
        out_shape=[jax.ShapeDtypeStruct((S, D), BF16), jax.ShapeDtypeStruct((S, D), BF16),
                   jax.ShapeDtypeStruct((NGRP, GD, GD), F32), jax.ShapeDtypeStruct((1, D), F32),
                   jax.ShapeDtypeStruct((1, D), F32)],
        scratch_shapes=[pltpu.VMEM((TE, D), F32), pltpu.VMEM((TE, D), F32), pltpu.VMEM((2, SUB, D), F32)],
        args=(dyp, proj, proj, proj, P['pool_w'], P['pool_b'], P['pool_scale']), rider=rider)


def _in_bwd(dq, w_in_l, x, dxo, P, mod, l, rider=None):
    S = x.shape[0]
    TM = min(256, S)
    NB = w_in_l.shape[2]
    nsteps = S // TM
    per_q = D // NB

    def body(d0, d1, d2, d3, w_ref, x_ref, dxo_ref, g_ref, sc_ref, dx_ref, dsh_ref, dsc_ref, dg_ref, vacc):
        i = pl.program_id(0)

        @pl.when(i == 0)
        def _():
            vacc[...] = jnp.zeros_like(vacc)
        dref = (d0, d1, d2, d3)
        dh = jnp.zeros((TM, D), F32)
        for j in range(NDEV):
            c0 = (j % per_q) * NB
            dh = dh + lax.dot_general(dref[j // per_q][:, c0:c0 + NB], w_ref[j], (((1,), (1,)), ((), ())),
                                      preferred_element_type=F32)
        xv = x_ref[...]
        rstd = lax.rsqrt(jnp.mean(xv * xv, axis=-1, keepdims=True) + NORM_EPS)
        xn = xv * rstd
        g, sc = g_ref[...], 1.0 + sc_ref[...]
        vacc[0] += _rowsum8(dh)
        vacc[1] += _rowsum8(dh * (xn * g))
        vacc[2] += _rowsum8(dh * sc * xn)
        dxn = dh * sc * g
        dx_ref[...] = dxo_ref[...] + rstd * (dxn - xn * jnp.mean(dxn * xn, axis=-1, keepdims=True))

        @pl.when(i == nsteps - 1)
        def _():
            dsh_ref[...] = _sum8(vacc[0])
            dsc_ref[...] = _sum8(vacc[1])
            dg_ref[...] = _sum8(vacc[2])

    row = pl.BlockSpec((TM, D), lambda i: (i, 0))
    return _pcall(
        body, name="in_bwd", grid=(nsteps,),
        in_specs=[row, row, row, row, _full((NDEV, D, NB)), row, row, _vec(l), _vec(l, 1)],
        out_specs=[row, _full((1, D)), _full((1, D)), _full((1, D))],
        out_shape=[jax.ShapeDtypeStruct((S, D), F32)] + [jax.ShapeDtypeStruct((1, D), F32)] * 3,
        scratch_shapes=[pltpu.VMEM((3, SUB, D), F32)],
        args=(*dq, w_in_l, x, dxo, P['pre_norm_g'], mod), rider=rider)


def _grad_w_in(h, dq, NB, rider=None):
    S, RH = h.shape
    TK = min(1024, S)
    nk = S // TK

    def body(h_ref, d0, d1, d2, d3, o_ref, acc):
        k = pl.program_id(0)

        @pl.when(k == 0)
        def _():
            acc[...] = jnp.zeros_like(acc)
        hv = h_ref[...]
        for q, d_ref in enumerate((d0, d1, d2, d3)):
            acc[:, q * D:(q + 1) * D] += lax.dot_general(hv, d_ref[...], (((0,), (0,)), ((), ())),
                                                         preferred_element_type=F32)

        @pl.when(k == nk - 1)
        def _():
            for j in range(NDEV):
                o_ref[j] = acc[:, j * NB:(j + 1) * NB].astype(BF16)

    row = pl.BlockSpec((TK, D), lambda k: (k, 0))
    return _pcall(
        body, name="grad_w_in", grid=(nk,),
        in_specs=[pl.BlockSpec((TK, RH), lambda k: (k, 0)), row, row, row, row],
        out_specs=[_full((NDEV, RH, NB))],
        out_shape=[jax.ShapeDtypeStruct((NDEV, RH, NB), BF16)],
        scratch_shapes=[pltpu.VMEM((RH, NQ * D), F32)],
        args=(h, *dq), rider=rider)


def _adamw_math(g, w, m, v):
    m2 = ADAM_B1 * m + (1.0 - ADAM_B1) * g
    v2 = ADAM_B2 * v + (1.0 - ADAM_B2) * (g * g)
    m_hat = m2 / (1.0 - ADAM_B1 ** ADAM_STEP)
    v_hat = v2 / (1.0 - ADAM_B2 ** ADAM_STEP)
    delta = -ADAM_LR * (m_hat / (jnp.sqrt(v_hat) + ADAM_EPS) + ADAM_WD * w)
    return delta, m2, v2


def _adamw(name, gs, w, m, v, TR, row0=0, into=None):
    L = len(gs)
    n, Rp, C = gs[0].shape
    R = w.shape[1]
    b0 = row0 // TR

    def body(*refs):
        g_refs = refs[:L]
        w_ref, m_ref, v_ref = refs[L:L + 3]
        go_ref, do_ref, mo_ref, vo_ref = refs[-4:]
        lay = pl.program_id(0)
        for li in range(L):
            @pl.when(lay == li)
            def _(li=li):
                g = g_refs[li][0].astype(F32)
                for s in range(1, n):
                    g = g + g_refs[li][s].astype(F32)
                delta, m2, v2 = _adamw_math(g, w_ref[...], m_ref[...], v_ref[...])
                go_ref[...] = g
                do_ref[...] = delta
                mo_ref[...] = m2
                vo_ref[...] = v2

    lrc = pl.BlockSpec((None, TR, C), lambda lay, r: (lay, r + b0, 0))
    g_specs = [pl.BlockSpec((n, TR, C), lambda lay, r, li=li: (0, jnp.where(lay == li, r, 0), 0))
               for li in range(L)]
    in_specs, args, aliases = g_specs + [lrc, lrc, lrc], [*gs, w, m, v], None
    if into is not None:
        aliases = {len(args) + k: k for k in range(4)}
        in_specs = in_specs + [pl.BlockSpec(memory_space=pl.ANY)] * 4
        args = args + list(into)
    return _pcall(
        body, name=name, grid=(L, Rp // TR), in_specs=in_specs, out_specs=[lrc] * 4,
        out_shape=[jax.ShapeDtypeStruct((L, R, C), F32)] * 4, args=args, aliases=aliases)


def _ada_adamw(c_all_t, dm, w, m, v, rider=None):
    L, _, nc = w.shape

    def body(c_ref, dm_ref, w_ref, m_ref, v_ref, go_ref, do_ref, mo_ref, vo_ref):
        cv = c_ref[...]
        ca = cv * jax.nn.sigmoid(cv)
        dmv = dm_ref[...]
        g = ca[:, 0:1] * dmv[0:1, :]
        for b in range(1, NDEV):
            g = g + ca[:, b:b + 1] * dmv[b:b + 1, :]
        delta, m2, v2 = _adamw_math(g, w_ref[...], m_ref[...], v_ref[...])
        go_ref[...] = g
        do_ref[...] = delta
        mo_ref[...] = m2
        vo_ref[...] = v2

    big = pl.BlockSpec((None, D, nc), lambda lay: (lay, 0, 0))
    return _pcall(
        body, name="ada_adamw", grid=(L,),
        in_specs=[_full((D, NDEV)), pl.BlockSpec((None, NDEV, nc), lambda lay: (lay, 0, 0)), big, big, big],
        out_specs=[big] * 4, out_shape=[jax.ShapeDtypeStruct((L, D, nc), F32)] * 4,
        args=(c_all_t, dm, w, m, v), rider=rider)


def _sum_slots(recv):
    n, R, C = recv.shape

    def body(r_ref, o_ref):
        acc = r_ref[0].astype(F32)
        for s in range(1, n):
            acc = acc + r_ref[s].astype(F32)
        o_ref[...] = acc

    return pl.pallas_call(body, name="sum_slots", out_shape=jax.ShapeDtypeStruct((R, C), F32),
                          compiler_params=_cparams())(recv)


def _pad_rows(a, rows):
    return jnp.pad(a, ((0, rows - a.shape[0]), (0, 0)))


def _pack_sharded_block(pool_w, pool_b, conv_w):
    return jnp.concatenate([pool_w.reshape(-1, PACK_C), _pad_rows(pool_b.reshape(-1, PACK_C), SUB),
                            _pad_rows(conv_w.reshape(-1, PACK_C), SUB)], axis=0)


def _unpack_sharded_block(p):
    n_pw = DEPTH * NGRP * (GD // NDEV)
    pool_w = p[:n_pw].reshape(DEPTH, NGRP, GD // NDEV, GD)
    pool_b = p[n_pw].reshape(DEPTH, NGRP, GD // NDEV)
    conv_w = p[n_pw + SUB:n_pw + SUB + DEPTH * CONV_K * (D // NDEV) // PACK_C].reshape(DEPTH, CONV_K, D // NDEV)
    return pool_w, pool_b, conv_w


def _blocks_of_full(pool_w, pool_b, conv_w):
    pw = pool_w.reshape(DEPTH, NGRP, NDEV, GD // NDEV, GD).transpose(2, 0, 1, 3, 4).reshape(NDEV, -1, PACK_C)
    pb = pool_b.reshape(DEPTH, NGRP, NDEV, GD // NDEV).transpose(2, 0, 1, 3).reshape(NDEV, -1, PACK_C)
    cw = conv_w.reshape(DEPTH, CONV_K, NDEV, D // NDEV).transpose(2, 0, 1, 3).reshape(NDEV, -1, PACK_C)
    pad = lambda a: jnp.pad(a, ((0, 0), (0, SUB - a.shape[1]), (0, 0)))
    return jnp.concatenate([pw, pad(pb), pad(cw)], axis=1)


def _full_of_blocks(p):
    n_pw = DEPTH * NGRP * (GD // NDEV)
    pool_w = p[:, :n_pw].reshape(NDEV, DEPTH, NGRP, GD // NDEV, GD).transpose(1, 2, 0, 3, 4)
    pool_b = p[:, n_pw].reshape(NDEV, DEPTH, NGRP, GD // NDEV).transpose(1, 2, 0, 3)
    n_cw = DEPTH * CONV_K * (D // NDEV) // PACK_C
    conv_w = p[:, n_pw + SUB:n_pw + SUB + n_cw].reshape(NDEV, DEPTH, CONV_K, D // NDEV).transpose(1, 2, 0, 3)
    return (pool_w.reshape(DEPTH, NGRP, GD, GD), pool_b.reshape(DEPTH, NGRP, GD),
            conv_w.reshape(DEPTH, CONV_K, D))


def _pack_replicated(t, keys, rows):
    p = jnp.concatenate([t[k].reshape(-1, PACK_C) for k in keys], axis=0)
    return _pad_rows(p, rows)


def _unpack_replicated(p, like, keys):
    out, r0 = {}, 0
    for k in keys:
        rows = like[k].size // PACK_C
        out[k] = p[r0:r0 + rows].reshape(like[k].shape)
        r0 += rows
    return out


def kernel(x, c, ada_w, ada_b, pre_norm_g, w_in, conv_w, conv_b, gate_a_w, gate_a_b, gate_x_w, gate_x_b, lru_lambda, pool_w, pool_b, pool_scale, w_out, post_norm_g, loss_target, m_ada_w, m_ada_b, m_pre_norm_g, m_w_in, m_conv_w, m_conv_b, m_gate_a_w, m_gate_a_b, m_gate_x_w, m_gate_x_b, m_lru_lambda, m_pool_w, m_pool_b, m_pool_scale, m_w_out, m_post_norm_g, v_ada_w, v_ada_b, v_pre_norm_g, v_w_in, v_conv_w, v_conv_b, v_gate_a_w, v_gate_a_b, v_gate_x_w, v_gate_x_b, v_lru_lambda, v_pool_w, v_pool_b, v_pool_scale, v_w_out, v_post_norm_g):
    W = dict(ada_w=ada_w, ada_b=ada_b, pre_norm_g=pre_norm_g, w_in=w_in, conv_w=conv_w, conv_b=conv_b,
             gate_a_w=gate_a_w, gate_a_b=gate_a_b, gate_x_w=gate_x_w, gate_x_b=gate_x_b, lru_lambda=lru_lambda,
             pool_w=pool_w, pool_b=pool_b, pool_scale=pool_scale, w_out=w_out, post_norm_g=post_norm_g)
    M = dict(ada_w=m_ada_w, ada_b=m_ada_b, pre_norm_g=m_pre_norm_g, w_in=m_w_in, conv_w=m_conv_w,
             conv_b=m_conv_b, gate_a_w=m_gate_a_w, gate_a_b=m_gate_a_b, gate_x_w=m_gate_x_w,
             gate_x_b=m_gate_x_b, lru_lambda=m_lru_lambda, pool_w=m_pool_w, pool_b=m_pool_b,
             pool_scale=m_pool_scale, w_out=m_w_out, post_norm_g=m_post_norm_g)
    V = dict(ada_w=v_ada_w, ada_b=v_ada_b, pre_norm_g=v_pre_norm_g, w_in=v_w_in, conv_w=v_conv_w,
             conv_b=v_conv_b, gate_a_w=v_gate_a_w, gate_a_b=v_gate_a_b, gate_x_w=v_gate_x_w,
             gate_x_b=v_gate_x_b, lru_lambda=v_lru_lambda, pool_w=v_pool_w, pool_b=v_pool_b,
             pool_scale=v_pool_scale, w_out=v_w_out, post_norm_g=v_post_norm_g)
    S = x.shape[1]
    me = 4 * lax.axis_index("x") + 2 * lax.axis_index("y") + lax.axis_index("c")
    xs = x.reshape(S, D)
    tgt = loss_target.reshape(S, D)
    nc = ada_w.shape[2]
    NB = w_in.shape[2]
    w_in_b, w_out_b = w_in.astype(BF16), w_out.astype(BF16)
    rows = lambda a: a.reshape(DEPTH, 1, D)
    P = dict(pre_norm_g=rows(pre_norm_g), conv_b=rows(conv_b), gate_a_b=rows(gate_a_b), gate_x_b=rows(gate_x_b),
             lru_lambda=rows(lru_lambda), pool_scale=rows(pool_scale), post_norm_g=rows(post_norm_g),
             gate_a_w=gate_a_w.astype(BF16), gate_x_w=gate_x_w.astype(BF16))

    c_slots, w_in0 = _exchange("gather_c_w_in0", _AllGatherVia([jnp.broadcast_to(c, (SUB, D)), w_in_b[0]]))
    c_all = c_slots[:, 0, :]
    (mod_slots,) = _exchange("gather_mod", _Direct(ag=[_mod_cols(c_all, ada_w)]))
    mod = lax.dynamic_index_in_dim(mod_slots, me, axis=1, keepdims=False)
    mod = (mod.reshape(NDEV, DEPTH, nc).transpose(1, 0, 2).reshape(DEPTH, 3 * D) + ada_b).reshape(DEPTH, 3, 1, D)

    w_in_all, w_out_all = [w_in0, None], [None, None]
    saved = []
    xl = xs
    flat = lambda w_slots: w_slots.reshape(2 * D, D)
    for l in range(DEPTH):
        rider = _AllGather2([_pack_sharded_block(pool_w, pool_b, conv_w), w_out_b[0]]) if l == 0 else None
        (h_a, h_b, proj), got = _pre_proj(xl, P, mod, w_in_all[l], l, rider=rider)
        if l == 0:
            pool_w_f, pool_b_f, conv_w_f = _full_of_blocks(got[0])
            P.update(conv_w=conv_w_f, pool_w=pool_w_f.astype(BF16), pool_b=rows(pool_b_f))
            w_out_all[0] = flat(got[1])
        rider = _AllGather2([w_in_b[1]]) if l == 0 else None
        (hs, yr, *fwd), got = _rnn_fwd(proj, P, l, rider=rider)
        if l == 0:
            w_in_all[1] = got[0]
        (yp,), _ = _pool_fwd(proj, P, l)
        if l == DEPTH - 1:
            (y, x_next, loss_acc), _ = _out_post(yr, yp, w_out_all[l], xl, mod, P, l, tgt)
        else:
            (y, x_next), got = _out_post(yr, yp, w_out_all[l], xl, mod, P, l, rider=_AllGather2([w_out_b[1]]))
            w_out_all[1] = flat(got[0])
        saved.append((xl, h_a, h_b, proj, hs, fwd, yr, yp, y))
        xl = x_next

    dxo = xl
    G = {k: [None] * DEPTH for k in WEIGHTS}
    dmod = [None] * DEPTH
    recv_in, recv_out = [[None, None] for _ in range(DEPTH)], [None] * DEPTH
    full = dict(conv_w=(CONV_K, D), pool_w=(NGRP, GD, GD), pool_b=(NGRP, GD))
    stack = lambda k: jnp.stack([g.reshape(full.get(k, W[k].shape[1:])) for g in G[k]])
    gw_bot_prev = None
    for l in reversed(range(DEPTH)):
        xin, h_a, h_b, proj, hs, fwd, yr, yp, y = saved[l]
        rider = _AllGather2([loss_acc]) if gw_bot_prev is None else _Direct(a2a=[gw_bot_prev])
        (dyr, dyp, gw_out, dgate, G['post_norm_g'][l]), got = _out_bwd(dxo, y, yr, yp, w_out_all[l], mod, P, l,
                                                                       rider=rider)
        if gw_bot_prev is None:
            loss = (0.5 / D) * jnp.sum(got[0])
        else:
            recv_in[l + 1][1] = got[0]
        ((dxr, dgr, G['conv_w'][l], G['conv_b'][l], G['gate_a_w'][l], G['gate_a_b'][l], G['gate_x_w'][l],
          G['gate_x_b'][l], G['lru_lambda'][l]), (recv_out[l],)) = _rnn_bwd(
            dyr, hs, fwd, proj, P, l,
            rider=_Direct(a2a=[gw_out.reshape(NDEV, 2 * D // NDEV, D)]))
        (dxp, dgp, G['pool_w'][l], G['pool_b'][l], G['pool_scale'][l]), _ = _pool_bwd(dyp, proj, P, l)
        dq = (dxr, dgr, dxp, dgp)
        if l > 0:
            (gw_top,), _ = _grad_w_in(h_a, dq, NB)
            (dxo, dshift, dscale, G['pre_norm_g'][l]), (recv_in[l][0],) = _in_bwd(
                dq, w_in_all[l], xin, dxo, P, mod, l, rider=_Direct(a2a=[gw_top]))
            (gw_bot_prev,), _ = _grad_w_in(h_b, dq, NB)
        else:
            Ge = {k: stack(k) for k in REP_EARLY + ['pool_w', 'pool_b', 'conv_w']}
            early = jnp.concatenate([_blocks_of_full(Ge['pool_w'], Ge['pool_b'], Ge['conv_w']),
                                     _pack_replicated(Ge, REP_EARLY, NDEV * PACK_ROWS).reshape(NDEV, PACK_ROWS, PACK_C)],
                                    axis=1).astype(BF16)
            (gw_top,), (early_recv,) = _grad_w_in(h_a, dq, NB, rider=_Direct(a2a=[early]))
            early_sum = _sum_slots(early_recv)
            (gw_bot,), (recv_in[l][0],) = _grad_w_in(h_b, dq, NB, rider=_Direct(a2a=[gw_top]))
            (dxo, dshift, dscale, G['pre_norm_g'][l]), (recv_in[l][1], early_all) = _in_bwd(
                dq, w_in_all[l], xin, dxo, P, mod, l,
                rider=_Both(_Direct(a2a=[gw_bot]), _AllGather2([early_sum[PACK_ROWS:]])))
        dmod[l] = jnp.concatenate([dshift, dscale, dgate], axis=1)
    grad_x = dxo.reshape(x.shape)

    Gl = dict(ada_b=jnp.concatenate(dmod, axis=0), pre_norm_g=stack('pre_norm_g'))
    (late_slots,) = _exchange("gather_late", _AllGather2([_pack_replicated(Gl, REP_LATE, LATE_PACK_ROWS)]))
    late_sum = _sum_slots(late_slots)
    dmod_all = late_slots[:, :DEPTH * 3 * D // PACK_C]

    out = {}
    first, _ = _adamw("adamw_w_in_a", [recv_in[l][0] for l in range(DEPTH)], w_in, M['w_in'], V['w_in'], LANE)
    out['w_in'], _ = _adamw("adamw_w_in_b", [recv_in[l][1] for l in range(DEPTH)], w_in, M['w_in'], V['w_in'], LANE,
                            row0=GW_SPLIT, into=first)
    out['w_out'], _ = _adamw("adamw_w_out", recv_out, w_out, M['w_out'], V['w_out'], 256)
    dm = lax.dynamic_slice_in_dim(dmod_all.reshape(NDEV, DEPTH, 3 * D), me * nc, nc, axis=2)
    out['ada_w'], _ = _ada_adamw(c_all.T, dm.transpose(1, 0, 2), ada_w, M['ada_w'], V['ada_w'])
    g_small = jnp.concatenate([early_sum[:PACK_ROWS], early_all.reshape(NDEV * PACK_ROWS, PACK_C), late_sum],
                              axis=0)

    def packs(T):
        return jnp.concatenate([_pack_sharded_block(T['pool_w'], T['pool_b'], T['conv_w']),
                                _pack_replicated(T, REP_EARLY, NDEV * PACK_ROWS),
                                _pack_replicated(T, REP_LATE, LATE_PACK_ROWS)], axis=0)[None]
    res_small, _ = _adamw("adamw_small", [g_small[None]], packs(W), packs(M), packs(V), g_small.shape[0] // 2)
    n_early = (1 + NDEV) * PACK_ROWS
    for idx in range(4):
        p = res_small[idx][0]
        pw_, pb_, cw_ = _unpack_sharded_block(p[:PACK_ROWS])
        rep = _unpack_replicated(p[PACK_ROWS:n_early], W, REP_EARLY)
        rep.update(_unpack_replicated(p[n_early:], W, REP_LATE))
        rep.update(pool_w=pw_, pool_b=pb_, conv_w=cw_)
        for k, a in rep.items():
            out.setdefault(k, [None] * 4)[idx] = a
    for k in ('w_in', 'w_out', 'ada_w'):
        out[k] = [a.reshape(W[k].shape) for a in out[k]]

    return (loss, grad_x, *[out[k][0] for k in WEIGHTS], *[out[k][1] for k in WEIGHTS],
            *[out[k][2] for k in WEIGHTS], *[out[k][3] for k in WEIGHTS])
```

```python
import functools

import jax
import jax.numpy as jnp
from jax import lax
from jax.experimental import pallas as pl
from jax.experimental.pallas import tpu as pltpu

F32, BF16 = jnp.float32, jnp.bfloat16
MESH = pl.DeviceIdType.MESH
HIGHEST = lax.Precision.HIGHEST

NDEV = 8
DEPTH = 2
D = 1024
NHEAD, HD = 8, 128
NGRP, GD = 4, 256
WINS = (2, 4, 8, 16)
CONV_K = 4
CONV_HALO = 8
POOL_HALO = 16
LRU_C = 8.0
NORM_EPS = 1e-6
ADAM_LR, ADAM_B1, ADAM_B2, ADAM_EPS, ADAM_WD, ADAM_STEP = 0.001, 0.9, 0.999, 1e-08, 0.01, 10
VMEM_LIMIT = 56 * 1024 * 1024
NQ = 4
SUB = 8
LANE = 128
PACK_C = 256
PACK_ROWS = 272

WEIGHTS = ['ada_w', 'ada_b', 'pre_norm_g', 'w_in', 'conv_w', 'conv_b', 'gate_a_w', 'gate_a_b', 'gate_x_w',
           'gate_x_b', 'lru_lambda', 'pool_w', 'pool_b', 'pool_scale', 'w_out', 'post_norm_g']
REP_EARLY = ['conv_b', 'gate_a_w', 'gate_a_b', 'gate_x_w', 'gate_x_b', 'lru_lambda', 'pool_scale', 'post_norm_g']
REP_LATE = ['ada_b', 'pre_norm_g']
GW_SPLIT = 512
LATE_PACK_ROWS = 32


def _cparams(*sem):
    return pltpu.CompilerParams(dimension_semantics=sem, vmem_limit_bytes=VMEM_LIMIT)


def _vec(l, k=None):
    if k is None:
        return pl.BlockSpec((None, 1, D), lambda *_: (l, 0, 0))
    return pl.BlockSpec((None, None, 1, D), lambda *_: (l, k, 0, 0))


def _layer(l, shape):
    nd = len(shape)
    return pl.BlockSpec((None,) + tuple(shape), lambda *_: (l,) + (0,) * nd)


def _full(shape):
    nd = len(shape)
    return pl.BlockSpec(shape, lambda *_: (0,) * nd)


def _rowsum8(z):
    return z.reshape(z.shape[0] // SUB, SUB, z.shape[1]).sum(axis=0)


def _sum8(acc):
    return jnp.sum(acc, axis=0, keepdims=True)


def _sigmoid(z):
    return 0.5 * jnp.tanh(0.5 * z) + 0.5


def _silu_parts(g):
    sg = _sigmoid(g)
    return g * sg, sg * (1.0 + g * (1.0 - sg))


def _one_minus_sq(a, log_a):
    z = 2.0 * log_a
    p = 1.0 / 24.0
    for k in (6.0, 2.0, 1.0):
        p = p * z + 1.0 / k
    return jnp.where(z > -0.03, -(p * z), 1.0 - a * a)


def _place():
    x, y, c = lax.axis_index("x"), lax.axis_index("y"), lax.axis_index("c")
    return x, y, c, 4 * x + 2 * y + c


class _Direct:
    def __init__(self, a2a=(), ag=()):
        self.arrays = list(a2a) + list(ag)
        self.n_a, self.n = len(a2a), len(self.arrays)
        self.out_shape = ([jax.ShapeDtypeStruct(a.shape, a.dtype) for a in a2a]
                          + [jax.ShapeDtypeStruct((NDEV,) + a.shape, a.dtype) for a in ag])
        self.scratch = [pltpu.SemaphoreType.DMA((self.n, NDEV - 1)), pltpu.SemaphoreType.DMA((self.n, NDEV - 1)),
                        pltpu.SemaphoreType.DMA((self.n,))]

    def _copies(self, ins, outs, sems):
        send_sems, recv_sems, local_sems = sems
        x, y, c, me = _place()
        local, remote = [], []
        for t in range(self.n):
            src = ins[t].at[me] if t < self.n_a else ins[t]
            local.append(pltpu.make_async_copy(src, outs[t].at[me], local_sems.at[t]))
        for r in range(1, NDEV):
            px = 1 - x if r & 4 else x
            py = 1 - y if r & 2 else y
            pc = 1 - c if r & 1 else c
            for t in range(self.n):
                src = ins[t].at[4 * px + 2 * py + pc] if t < self.n_a else ins[t]
                remote.append(pltpu.make_async_remote_copy(
                    src_ref=src, dst_ref=outs[t].at[me], send_sem=send_sems.at[t, r - 1],
                    recv_sem=recv_sems.at[t, r - 1], device_id=(px, py, pc), device_id_type=MESH))
        return local, remote

    def start(self, ins, outs, sems):
        local, remote = self._copies(ins, outs, sems)
        for cp in local + remote:
            cp.start()

    def finish(self, ins, outs, sems):
        local, remote = self._copies(ins, outs, sems)
        for cp in remote + local:
            cp.wait()


class _AllGather2:
    def __init__(self, arrays):
        self.arrays = list(arrays)
        self.n = len(self.arrays)
        self.out_shape = [jax.ShapeDtypeStruct((NDEV,) + a.shape, a.dtype) for a in self.arrays]
        self.scratch = [pltpu.SemaphoreType.DMA((self.n, NDEV - 1)), pltpu.SemaphoreType.DMA((self.n, NDEV - 1)),
                        pltpu.SemaphoreType.DMA((self.n,))]

    @staticmethod
    def _chips(x, y):
        return [(1 - x, y), (x, 1 - y), (1 - x, 1 - y)]

    def _copy(self, t, k, src, dst, to, sems):
        return pltpu.make_async_remote_copy(src_ref=src, dst_ref=dst, send_sem=sems[0].at[t, k],
                                            recv_sem=sems[1].at[t, k], device_id=to, device_id_type=MESH)

    def start(self, ins, outs, sems):
        x, y, c, me = _place()
        for t in range(self.n):
            pltpu.make_async_copy(ins[t], outs[t].at[me], sems[2].at[t]).start()
            self._copy(t, 0, ins[t], outs[t].at[me], (x, y, 1 - c), sems).start()
            for j, (px, py) in enumerate(self._chips(x, y)):
                self._copy(t, 1 + j, ins[t], outs[t].at[me], (px, py, c), sems).start()

    def middle(self, ins, outs, sems):
        x, y, c, me = _place()
        sib = (x, y, 1 - c)
        for j, (px, py) in enumerate(self._chips(x, y)):
            slot = 4 * px + 2 * py + c
            for t in range(self.n):
                self._copy(t, 1 + j, ins[t], outs[t].at[slot], sib, sems).wait_recv()
                self._copy(t, 4 + j, outs[t].at[slot], outs[t].at[slot], sib, sems).start()

    def finish(self, ins, outs, sems):
        x, y, c, me = _place()
        sib = (x, y, 1 - c)
        for t in range(self.n):
            for k in (0, 4, 5, 6):
                self._copy(t, k, ins[t], outs[t].at[me], sib, sems).wait_recv()
        for t in range(self.n):
            for k in range(NDEV - 1):
                self._copy(t, k, ins[t], outs[t].at[me], sib, sems).wait_send()
            pltpu.make_async_copy(ins[t], outs[t].at[me], sems[2].at[t]).wait()


class _AllGatherVia:
    def __init__(self, arrays):
        self.arrays = list(arrays)
        self.n = len(self.arrays)
        self.out_shape = [jax.ShapeDtypeStruct((NDEV,) + a.shape, a.dtype) for a in self.arrays]
        self.scratch = [pltpu.SemaphoreType.DMA((self.n, NDEV - 1)), pltpu.SemaphoreType.DMA((self.n, NDEV - 1)),
                        pltpu.SemaphoreType.DMA((self.n,))]

    def _copy(self, t, k, src, dst, to, sems):
        return pltpu.make_async_remote_copy(src_ref=src, dst_ref=dst, send_sem=sems[0].at[t, k],
                                            recv_sem=sems[1].at[t, k], device_id=to, device_id_type=MESH)

    def start(self, ins, outs, sems):
        x, y, c, me = _place()
        for t in range(self.n):
            pltpu.make_async_copy(ins[t], outs[t].at[me], sems[2].at[t]).start()
            self._copy(t, 0, ins[t], outs[t].at[me], (x, y, 1 - c), sems).start()
            self._copy(t, 1, ins[t], outs[t].at[me], (1 - x, y, c), sems).start()
            self._copy(t, 2, ins[t], outs[t].at[me], (x, 1 - y, c), sems).start()

    def finish(self, ins, outs, sems):
        x, y, c, me = _place()
        sib = (x, y, 1 - c)
        slot_x, slot_y, slot_d = 4 * (1 - x) + 2 * y + c, 4 * x + 2 * (1 - y) + c, 4 * (1 - x) + 2 * (1 - y) + c
        slot_on = c * slot_x + (1 - c) * slot_y
        to_on = (c * x + (1 - c) * (1 - x), c * (1 - y) + (1 - c) * y, c)
        for t in range(self.n):
            self._copy(t, 1, ins[t], outs[t].at[slot_x], sib, sems).wait_recv()
            self._copy(t, 2, ins[t], outs[t].at[slot_y], sib, sems).wait_recv()
            self._copy(t, 3, outs[t].at[slot_on], outs[t].at[slot_on], to_on, sems).start()
            self._copy(t, 4, outs[t].at[slot_x], outs[t].at[slot_x], sib, sems).start()
            self._copy(t, 5, outs[t].at[slot_y], outs[t].at[slot_y], sib, sems).start()
        for t in range(self.n):
            self._copy(t, 3, ins[t], outs[t].at[slot_d], sib, sems).wait_recv()
            self._copy(t, 6, outs[t].at[slot_d], outs[t].at[slot_d], sib, sems).start()
        for t in range(self.n):
            for k in (0, 4, 5, 6):
                self._copy(t, k, ins[t], outs[t].at[me], sib, sems).wait_recv()
        for t in range(self.n):
            for k in range(NDEV - 1):
                self._copy(t, k, ins[t], outs[t].at[me], sib, sems).wait_send()
            pltpu.make_async_copy(ins[t], outs[t].at[me], sems[2].at[t]).wait()


class _Both:
    def __init__(self, *riders):
        self.riders = riders
        self.arrays = [a for r in riders for a in r.arrays]
        self.n = len(self.arrays)
        self.out_shape = [o for r in riders for o in r.out_shape]
        self.scratch = [s for r in riders for s in r.scratch]

    def _parts(self, ins, outs, sems):
        p, q = 0, 0
        for r in self.riders:
            yield r, ins[p:p + r.n], outs[p:p + r.n], sems[q:q + len(r.scratch)]
            p, q = p + r.n, q + len(r.scratch)

    def start(self, ins, outs, sems):
        for r, i, o, s in self._parts(ins, outs, sems):
            r.start(i, o, s)

    def finish(self, ins, outs, sems):
        for r, i, o, s in self._parts(ins, outs, sems):
            if hasattr(r, "middle"):
                r.middle(i, o, s)
            r.finish(i, o, s)


def _exchange(name, rider):
    n = rider.n

    def body(*refs):
        rider.start(refs[:n], refs[n:2 * n], refs[2 * n:])
        if hasattr(rider, "middle"):
            rider.middle(refs[:n], refs[n:2 * n], refs[2 * n:])
        rider.finish(refs[:n], refs[n:2 * n], refs[2 * n:])

    any_spec = pl.BlockSpec(memory_space=pl.ANY)
    return list(pl.pallas_call(body, name=name, out_shape=rider.out_shape, in_specs=[any_spec] * n,
                               out_specs=[any_spec] * n, scratch_shapes=rider.scratch)(*rider.arrays))


def _pcall(body, *, name, grid, in_specs, out_specs, out_shape, args, scratch_shapes=(), rider=None, aliases=None):
    params = _cparams(*(("arbitrary",) * len(grid)))
    if rider is None:
        res = pl.pallas_call(body, name=name, grid=grid, in_specs=in_specs, out_specs=out_specs,
                             out_shape=out_shape, scratch_shapes=list(scratch_shapes),
                             input_output_aliases=aliases or {}, compiler_params=params)(*args)
        return list(res), []
    n_in, n_out, n_scr, rn = len(in_specs), len(out_specs), len(scratch_shapes), rider.n

    def wrapped(*refs):
        cuts = [n_in, rn, n_out, rn, n_scr]
        parts, p = [], 0
        for n in cuts:
            parts.append(refs[p:p + n])
            p += n
        ins, r_in, outs, r_out, scr = parts
        sems = refs[p:]
        ids = [pl.program_id(a) for a in range(len(grid))]
        first = functools.reduce(jnp.logical_and, [i == 0 for i in ids])
        last = functools.reduce(jnp.logical_and, [i == g - 1 for i, g in zip(ids, grid)])

        @pl.when(first)
        def _():
            rider.start(r_in, r_out, sems)
        body(*ins, *outs, *scr)
        early = hasattr(rider, "middle") and len(grid) == 1 and grid[0] >= 4
        if early:
            @pl.when(ids[0] == (3 * grid[0]) // 4)
            def _():
                rider.middle(r_in, r_out, sems)

        @pl.when(last)
        def _():
            if hasattr(rider, "middle") and not early:
                rider.middle(r_in, r_out, sems)
            rider.finish(r_in, r_out, sems)

    any_spec = pl.BlockSpec(memory_space=pl.ANY)
    res = pl.pallas_call(
        wrapped, name=name, grid=grid, in_specs=list(in_specs) + [any_spec] * rn,
        out_specs=list(out_specs) + [any_spec] * rn, out_shape=list(out_shape) + rider.out_shape,
        scratch_shapes=list(scratch_shapes) + rider.scratch, compiler_params=params)(*args, *rider.arrays)
    return list(res[:n_out]), list(res[n_out:])


def _mod_cols(c_all, ada_w):
    nc = ada_w.shape[2]

    def body(c_ref, w_ref, o_ref):
        cv = c_ref[...]
        ca = cv * jax.nn.sigmoid(cv)
        for l in range(DEPTH):
            o_ref[:, l * nc:(l + 1) * nc] = jnp.dot(ca, w_ref[l], precision=HIGHEST, preferred_element_type=F32)

    return pl.pallas_call(body, name="mod_cols", out_shape=jax.ShapeDtypeStruct((NDEV, DEPTH * nc), F32),
                          compiler_params=_cparams())(c_all, ada_w)


def _pre_proj(x, P, mod, w_in_l, l, rider=None):
    S = x.shape[0]
    TM = min(512, S)
    NB = w_in_l.shape[2]

    def body(x_ref, g_ref, sc_ref, sh_ref, w_ref, ha_ref, hb_ref, p_ref):
        xv = x_ref[...]
        rstd = lax.rsqrt(jnp.mean(xv * xv, axis=-1, keepdims=True) + NORM_EPS)
        h = ((xv * rstd * g_ref[...]) * (1.0 + sc_ref[...]) + sh_ref[...]).astype(BF16)
        ha_ref[...] = h[:, :GW_SPLIT]
        hb_ref[...] = h[:, GW_SPLIT:]
        for j in range(NDEV):
            p_ref[:, j * NB:(j + 1) * NB] = jnp.dot(h, w_ref[j], preferred_element_type=F32)

    row = pl.BlockSpec((TM, D), lambda i: (i, 0))
    return _pcall(
        body, name="pre_proj", grid=(S // TM,),
        in_specs=[row, _vec(l), _vec(l, 1), _vec(l, 0), _full((NDEV, D, NB))],
        out_specs=[pl.BlockSpec((TM, GW_SPLIT), lambda i: (i, 0)), pl.BlockSpec((TM, D - GW_SPLIT), lambda i: (i, 0)),
                   pl.BlockSpec((TM, NDEV * NB), lambda i: (i, 0))],
        out_shape=[jax.ShapeDtypeStruct((S, GW_SPLIT), BF16), jax.ShapeDtypeStruct((S, D - GW_SPLIT), BF16),
                   jax.ShapeDtypeStruct((S, NDEV * NB), F32)],
        args=(x, P['pre_norm_g'], mod, mod, w_in_l), rider=rider)


def _taps(E):
    return [pltpu.roll(E, CONV_K - 1 - k, axis=0)[CONV_HALO:, :] for k in range(CONV_K - 1)] + [E[CONV_HALO:, :]]


def _conv(E, cw_ref, cb_ref):
    w = cw_ref[...]
    taps = _taps(E)
    acc = cb_ref[...] + taps[0] * w[0:1, :]
    for k in range(1, CONV_K):
        acc = acc + taps[k] * w[k:k + 1, :]
    return acc


CH_R, CH_C = 16, 512


def _chunks(T, fn):
    def step(c, carry):
        rows = pl.ds(pl.multiple_of(c * CH_R, CH_R), CH_R)
        for hf in range(D // CH_C):
            fn(rows, slice(hf * CH_C, (hf + 1) * CH_C), hf)
        return carry
    lax.fori_loop(0, T // CH_R, step, 0)


def _to_scan(ref, rows, hf, val):
    for q in range(CH_C // LANE):
        ref[hf * (CH_C // LANE) + q, rows, :] = val[:, q * LANE:(q + 1) * LANE]


def _scan(sa, sv, carry_ref, out_ref, reverse):
    NC, T, _ = sa.shape
    n8 = T // SUB
    rows = range(SUB - 2, -1, -1) if reverse else range(1, SUB)
    for cb in range(NC):
        r_in = SUB - 1 if reverse else 0
        Ap = sa[cb, pl.ds(r_in, n8, stride=SUB), :]
        Vp = sv[cb, pl.ds(r_in, n8, stride=SUB), :]
        for r in rows:
            Ar = sa[cb, pl.ds(r, n8, stride=SUB), :]
            Vp = sv[cb, pl.ds(r, n8, stride=SUB), :] + Ar * Vp
            Ap = Ar * Ap
            sa[cb, pl.ds(r, n8, stride=SUB), :] = Ap
            sv[cb, pl.ds(r, n8, stride=SUB), :] = Vp
    edge = 0 if reverse else SUB - 1

    def step(k, c):
        r0 = pl.multiple_of((n8 - 1 - k if reverse else k) * SUB, SUB)
        h = jnp.concatenate([sv[cb, pl.ds(r0, SUB), :] + sa[cb, pl.ds(r0, SUB), :] * c[:, cb * LANE:(cb + 1) * LANE]
                             for cb in range(NC)], axis=1)
        out_ref[pl.ds(r0, SUB), :] = h
        return jnp.broadcast_to(h[edge:edge + 1, :], (SUB, D))

    carry_ref[...] = lax.fori_loop(0, n8, step, carry_ref[...])


def _rnn_fwd(proj, P, l, rider=None):
    S = proj.shape[0]
    TB = min(256, S)

    def body(xr_ref, g_ref, cw_ref, cb_ref, wa_ref, ba_ref, wx_ref, bx_ref, lam_ref, hs_ref, y_ref,
             u_ref, r_ref, i_ref, a_ref, m_ref, xbuf, sa, sv, hc):
        @pl.when(pl.program_id(0) == 0)
        def _():
            xbuf[0:CONV_HALO, :] = jnp.zeros((CONV_HALO, D), F32)
            hc[...] = jnp.zeros((SUB, D), F32)
        xbuf[CONV_HALO:, :] = xr_ref[...]
        u = _conv(xbuf[...], cw_ref, cb_ref)
        xbuf[0:CONV_HALO, :] = xbuf[TB:TB + CONV_HALO, :]
        ub = u.astype(BF16)
        zr = jnp.concatenate([jnp.dot(ub[:, h * HD:(h + 1) * HD], wa_ref[h], preferred_element_type=F32)
                              for h in range(NHEAD)], axis=1)
        zi = jnp.concatenate([jnp.dot(ub[:, h * HD:(h + 1) * HD], wx_ref[h], preferred_element_type=F32)
                              for h in range(NHEAD)], axis=1)
        r = _sigmoid(zr + ba_ref[...])
        ig = _sigmoid(zi + bx_ref[...])
        log_a = r * (-LRU_C * jax.nn.softplus(-lam_ref[...]))
        a = jnp.exp(log_a)
        mult = jnp.sqrt(_one_minus_sq(a, log_a))
        v = mult * (ig * u)
        u_ref[...] = u
        r_ref[...] = r
        i_ref[...] = ig
        a_ref[...] = a
        m_ref[...] = mult
        for cb in range(D // LANE):
            sa[cb] = a[:, cb * LANE:(cb + 1) * LANE]
            sv[cb] = v[:, cb * LANE:(cb + 1) * LANE]
        _scan(sa, sv, hc, hs_ref, reverse=False)
        silu, _ = _silu_parts(g_ref[...])
        y_ref[...] = (hs_ref[...] * silu).astype(BF16)

    rowb = pl.BlockSpec((TB, D), lambda i: (i, 0))
    return _pcall(
        body, name="rnn_fwd", grid=(S // TB,),
        in_specs=[rowb, pl.BlockSpec((TB, D), lambda i: (i, 1)), _layer(l, (CONV_K, D)), _vec(l),
                  _layer(l, (NHEAD, HD, HD)), _vec(l), _layer(l, (NHEAD, HD, HD)), _vec(l), _vec(l)],
        out_specs=[rowb] * 7,
        out_shape=[jax.ShapeDtypeStruct((S, D), F32), jax.ShapeDtypeStruct((S, D), BF16)]
        + [jax.ShapeDtypeStruct((S, D), F32)] * 5,
        scratch_shapes=[pltpu.VMEM((TB + CONV_HALO, D), F32), pltpu.VMEM((D // LANE, TB, LANE), F32),
                        pltpu.VMEM((D // LANE, TB, LANE), F32), pltpu.VMEM((SUB, D), F32)],
        args=(proj, proj, P['conv_w'], P['conv_b'], P['gate_a_w'], P['gate_a_b'], P['gate_x_w'], P['gate_x_b'],
              P['lru_lambda']), rider=rider)


def _pooled(ebuf, t0, TB):
    tt = t0 + lax.broadcasted_iota(jnp.int32, (TB, 1), 0)
    pooled, inv = [], []
    for g, win in enumerate(WINS):
        Eg = ebuf[:, g * GD:(g + 1) * GD]
        L = Eg
        for lev in range(g + 1):
            L = L + pltpu.roll(L, 1 << lev, axis=0)
        icnt = 1.0 / jnp.minimum(tt + 1, win).astype(F32)
        pooled.append(L[POOL_HALO:, :] * icnt - Eg[POOL_HALO:, :])
        inv.append(icnt)
    return pooled, inv


def _pool_fwd(proj, P, l, rider=None):
    S = proj.shape[0]
    TB = min(512, S)

    def body(xp_ref, g_ref, pw_ref, pb_ref, ps_ref, y_ref, ebuf):
        i = pl.program_id(0)

        @pl.when(i == 0)
        def _():
            ebuf[0:POOL_HALO, :] = jnp.zeros((POOL_HALO, D), F32)
        ebuf[POOL_HALO:, :] = xp_ref[...]
        pooled, _ = _pooled(ebuf, i * TB, TB)
        ebuf[0:POOL_HALO, :] = ebuf[TB:TB + POOL_HALO, :]
        yp = jnp.concatenate([jnp.dot(pooled[g].astype(BF16), pw_ref[g], preferred_element_type=F32)
                              for g in range(NGRP)], axis=1) + pb_ref[...]
        silu, _ = _silu_parts(g_ref[...])
        y_ref[...] = (yp * ps_ref[...] * silu).astype(BF16)

    return _pcall(
        body, name="pool_fwd", grid=(S // TB,),
        in_specs=[pl.BlockSpec((TB, D), lambda i: (i, 2)), pl.BlockSpec((TB, D), lambda i: (i, 3)),
                  _layer(l, (NGRP, GD, GD)), _vec(l), _vec(l)],
        out_specs=[pl.BlockSpec((TB, D), lambda i: (i, 0))],
        out_shape=[jax.ShapeDtypeStruct((S, D), BF16)],
        scratch_shapes=[pltpu.VMEM((TB + POOL_HALO, D), F32)],
        args=(proj, proj, P['pool_w'], P['pool_b'], P['pool_scale']), rider=rider)


def _out_post(yr, yp, w_out_l, x, mod, P, l, target=None, rider=None):
    S = x.shape[0]
    TM = min(512, S)
    last = target is not None

    def body(*refs):
        if last:
            yr_ref, yp_ref, w_ref, x_ref, gate_ref, gp_ref, t_ref, y_ref, xo_ref, loss_ref = refs
        else:
            yr_ref, yp_ref, w_ref, x_ref, gate_ref, gp_ref, y_ref, xo_ref = refs
        acc = (jnp.dot(yr_ref[...], w_ref[0:D, :], preferred_element_type=F32)
               + jnp.dot(yp_ref[...], w_ref[D:2 * D, :], preferred_element_type=F32))
        y_ref[...] = acc
        rstd = lax.rsqrt(jnp.mean(acc * acc, axis=-1, keepdims=True) + NORM_EPS)
        xn = x_ref[...] + gate_ref[...] * (acc * rstd * gp_ref[...])
        if last:
            err = xn - t_ref[...]
            xo_ref[...] = err * (1.0 / D)

            @pl.when(pl.program_id(0) == 0)
            def _():
                loss_ref[...] = jnp.zeros((SUB, D), F32)
            loss_ref[...] += _rowsum8(err * err)
        else:
            xo_ref[...] = xn

    row = pl.BlockSpec((TM, D), lambda i: (i, 0))
    in_specs = [row, row, _full((2 * D, D)), row, _vec(l, 2), _vec(l)]
    out_specs = [row, row]
    out_shape = [jax.ShapeDtypeStruct((S, D), F32), jax.ShapeDtypeStruct((S, D), F32)]
    args = [yr, yp, w_out_l, x, mod, P['post_norm_g']]
    if last:
        in_specs.append(row)
        out_specs.append(_full((SUB, D)))
        out_shape.append(jax.ShapeDtypeStruct((SUB, D), F32))
        args.append(target)
    return _pcall(body, name="out_post_loss" if last else "out_post", grid=(S // TM,), in_specs=in_specs,
                  out_specs=out_specs, out_shape=out_shape, args=args, rider=rider)


def _out_bwd(dxo, y, yr, yp, w_out_l, mod, P, l, rider=None):
    S = y.shape[0]
    TM = min(512, S)
    nsteps = S // TM

    def body(dxo_ref, y_ref, yr_ref, yp_ref, w_ref, gate_ref, gp_ref, dyr_ref, dyp_ref, gw_ref, dgate_ref,
             dgp_ref, gw_acc, vacc):
        i = pl.program_id(0)

        @pl.when(i == 0)
        def _():
            gw_acc[...] = jnp.zeros_like(gw_acc)
            vacc[...] = jnp.zeros_like(vacc)
        yv = y_ref[...]
        dxo_v = dxo_ref[...]
        rstd = lax.rsqrt(jnp.mean(yv * yv, axis=-1, keepdims=True) + NORM_EPS)
        n = yv * rstd
        gp = gp_ref[...]
        vacc[0] += _rowsum8(dxo_v * (n * gp))
        drn = dxo_v * gate_ref[...]
        vacc[1] += _rowsum8(drn * n)
        dn = drn * gp
        dy = (rstd * (dn - n * jnp.mean(dn * n, axis=-1, keepdims=True))).astype(BF16)
        dyr_ref[...] = lax.dot_general(dy, w_ref[0:D, :], (((1,), (1,)), ((), ())), preferred_element_type=F32)
        dyp_ref[...] = lax.dot_general(dy, w_ref[D:2 * D, :], (((1,), (1,)), ((), ())),
                                       preferred_element_type=F32)
        gw_acc[0:D, :] += lax.dot_general(yr_ref[...], dy, (((0,), (0,)), ((), ())), preferred_element_type=F32)
        gw_acc[D:2 * D, :] += lax.dot_general(yp_ref[...], dy, (((0,), (0,)), ((), ())),
                                              preferred_element_type=F32)

        @pl.when(i == nsteps - 1)
        def _():
            gw_ref[...] = gw_acc[...].astype(BF16)
            dgate_ref[...] = _sum8(vacc[0])
            dgp_ref[...] = _sum8(vacc[1])

    row = pl.BlockSpec((TM, D), lambda i: (i, 0))
    return _pcall(
        body, name="out_bwd", grid=(nsteps,),
        in_specs=[row, row, row, row, _full((2 * D, D)), _vec(l, 2), _vec(l)],
        out_specs=[row, row, _full((2 * D, D)), _full((1, D)), _full((1, D))],
        out_shape=[jax.ShapeDtypeStruct((S, D), F32), jax.ShapeDtypeStruct((S, D), F32),
                   jax.ShapeDtypeStruct((2 * D, D), BF16), jax.ShapeDtypeStruct((1, D), F32),
                   jax.ShapeDtypeStruct((1, D), F32)],
        scratch_shapes=[pltpu.VMEM((2 * D, D), F32), pltpu.VMEM((2, SUB, D), F32)],
        args=(dxo, y, yr, yp, w_out_l, mod, P['post_norm_g']), rider=rider)


def _rnn_bwd(dyr, hs, fwd, proj, P, l, rider=None):
    S = proj.shape[0]
    TB = min(256, S)
    nb = S // TB
    TE = TB + CONV_HALO
    A_BA, A_BX, A_LAM, A_CB, A_CW = 0, 1, 2, 3, 4

    def blk(i):
        return nb - 1 - i

    def body(dyr_ref, hs_ref, hprev_ref, u_ref, r_ref, i_ref, a_ref, m_ref, xr_ref, g_ref, cw_ref,
             wa_ref, wx_ref, lam_ref, dxr_ref, dg_ref, gcw_ref, gcb_ref, gwa_ref, gba_ref, gwx_ref, gbx_ref,
             glam_ref, hbuf, abuf, dbuf, sa, sv, dh_ref, hp_ref, dzr_ref, dzi_ref, dhc, vacc):
        i = pl.program_id(0)
        first = blk(i) == 0

        @pl.when(i == 0)
        def _():
            abuf[TB:, :] = jnp.zeros((CONV_HALO, D), F32)
            dbuf[TB:, :] = jnp.zeros((CONV_HALO, D), F32)
            dhc[...] = jnp.zeros_like(dhc)
            vacc[...] = jnp.zeros_like(vacc)
            gwa_ref[...] = jnp.zeros_like(gwa_ref)
            gwx_ref[...] = jnp.zeros_like(gwx_ref)

        hbuf[0:CONV_HALO, :] = jnp.where(first, 0.0, hprev_ref[...])
        hbuf[CONV_HALO:, :] = hs_ref[...]
        hp_ref[...] = pltpu.roll(hbuf[...], 1, axis=0)[CONV_HALO:, :]
        abuf[0:TB, :] = a_ref[...]
        b = pltpu.roll(abuf[...], TE - 1, axis=0)[0:TB, :]
        for cb in range(D // LANE):
            sa[cb] = b[:, cb * LANE:(cb + 1) * LANE]
        abuf[TB:, :] = a_ref[0:CONV_HALO, :]

        def gate_bwd(rows, cs, hf):
            silu, dsilu = _silu_parts(g_ref[rows, cs])
            dy = dyr_ref[rows, cs]
            dg_ref[rows, cs] = (dy * hs_ref[rows, cs] * dsilu).astype(BF16)
            _to_scan(sv, rows, hf, dy * silu)
        _chunks(TB, gate_bwd)

        _scan(sa, sv, dhc, dh_ref, reverse=True)

        csp = -LRU_C * jax.nn.softplus(-lam_ref[...])

        def lru_bwd(rows, cs, hf):
            dh, a, ig, u, mult, r = dh_ref[rows, cs], a_ref[rows, cs], i_ref[rows, cs], u_ref[rows, cs], \
                m_ref[rows, cs], r_ref[rows, cs]
            dlog_a = dh * hp_ref[rows, cs] * a - (dh * ig * u) * (a * a) / mult
            dzr = dlog_a * csp[:, cs] * r * (1.0 - r)
            dzi = (dh * mult * u) * ig * (1.0 - ig)
            dzr_ref[rows, cs] = dzr.astype(BF16)
            dzi_ref[rows, cs] = dzi.astype(BF16)
            dbuf[rows, cs] = dh * mult * ig
            vacc[A_LAM, :, cs] += _rowsum8(dlog_a * r)
            vacc[A_BA, :, cs] += _rowsum8(dzr)
            vacc[A_BX, :, cs] += _rowsum8(dzi)
        _chunks(TB, lru_bwd)

        ub, dzrb, dzib = u_ref[...].astype(BF16), dzr_ref[...], dzi_ref[...]
        du_g = []
        for h in range(NHEAD):
            cs = slice(h * HD, (h + 1) * HD)
            gwa_ref[h] += lax.dot_general(ub[:, cs], dzrb[:, cs], (((0,), (0,)), ((), ())),
                                          preferred_element_type=F32)
            gwx_ref[h] += lax.dot_general(ub[:, cs], dzib[:, cs], (((0,), (0,)), ((), ())),
                                          preferred_element_type=F32)
            du_g.append(lax.dot_general(dzrb[:, cs], wa_ref[h], (((1,), (1,)), ((), ())),
                                        preferred_element_type=F32)
                        + lax.dot_general(dzib[:, cs], wx_ref[h], (((1,), (1,)), ((), ())),
                                          preferred_element_type=F32))
        du = dbuf[0:TB, :] + jnp.concatenate(du_g, axis=1)
        dbuf[0:TB, :] = du

        Dd = dbuf[...]
        w = cw_ref[...]
        xv = xr_ref[...]
        dx = du * w[CONV_K - 1:CONV_K, :]
        vacc[A_CB] += _rowsum8(du)
        vacc[A_CW + CONV_K - 1] += _rowsum8(xv * du)
        for k in range(CONV_K - 1):
            ahead = pltpu.roll(Dd, TE - (CONV_K - 1 - k), axis=0)[0:TB, :]
            dx = dx + ahead * w[k:k + 1, :]
            vacc[A_CW + k] += _rowsum8(xv * ahead)
        dxr_ref[...] = dx.astype(BF16)
        dbuf[TB:, :] = du[0:CONV_HALO, :]

        @pl.when(i == nb - 1)
        def _():
            gba_ref[...] = _sum8(vacc[A_BA])
            gbx_ref[...] = _sum8(vacc[A_BX])
            glam_ref[...] = _sum8(vacc[A_LAM]) * (LRU_C * _sigmoid(-lam_ref[...]))
            gcb_ref[...] = _sum8(vacc[A_CB])
            for k in range(CONV_K):
                gcw_ref[k:k + 1, :] = _sum8(vacc[A_CW + k])

    rowb = pl.BlockSpec((TB, D), lambda i: (blk(i), 0))
    halo = pl.BlockSpec((SUB, D), lambda i: (jnp.maximum(blk(i) * (TB // SUB) - 1, 0), 0))
    wspec = _full((NHEAD, HD, HD))
    wlay = _layer(l, (NHEAD, HD, HD))
    vec1 = _full((1, D))
    return _pcall(
        body, name="rnn_bwd", grid=(nb,),
        in_specs=[rowb, rowb, halo] + [rowb] * 5 + [rowb, pl.BlockSpec((TB, D), lambda i: (blk(i), 1)),
                                                    _layer(l, (CONV_K, D)), wlay, wlay, _vec(l)],
        out_specs=[rowb, rowb, _full((CONV_K, D)), vec1, wspec, vec1, wspec, vec1, vec1],
        out_shape=[jax.ShapeDtypeStruct((S, D), BF16), jax.ShapeDtypeStruct((S, D), BF16),
                   jax.ShapeDtypeStruct((CONV_K, D), F32), jax.ShapeDtypeStruct((1, D), F32),
                   jax.ShapeDtypeStruct((NHEAD, HD, HD), F32), jax.ShapeDtypeStruct((1, D), F32),
                   jax.ShapeDtypeStruct((NHEAD, HD, HD), F32), jax.ShapeDtypeStruct((1, D), F32),
                   jax.ShapeDtypeStruct((1, D), F32)],
        scratch_shapes=[pltpu.VMEM((TE, D), F32), pltpu.VMEM((TE, D), F32), pltpu.VMEM((TE, D), F32),
                        pltpu.VMEM((D // LANE, TB, LANE), F32), pltpu.VMEM((D // LANE, TB, LANE), F32),
                        pltpu.VMEM((TB, D), F32), pltpu.VMEM((TB, D), F32), pltpu.VMEM((TB, D), BF16),
                        pltpu.VMEM((TB, D), BF16), pltpu.VMEM((SUB, D), F32),
                        pltpu.VMEM((A_CW + CONV_K, SUB, D), F32)],
        args=(dyr, hs, hs, *fwd, proj, proj, P['conv_w'], P['gate_a_w'], P['gate_x_w'], P['lru_lambda']),
        rider=rider)


def _pool_bwd(dyp, proj, P, l, rider=None):
    S = proj.shape[0]
    TB = min(512, S)
    nb = S // TB
    TE = TB + POOL_HALO

    def blk(i):
        return nb - 1 - i

    def body(dy_ref, xp_ref, xprev_ref, g_ref, pw_ref, pb_ref, ps_ref, dxp_ref, dg_ref, gpw_ref, gpb_ref,
             gps_ref, ebuf, qbuf, vacc):
        i = pl.program_id(0)
        first = blk(i) == 0

        @pl.when(i == 0)
        def _():
            qbuf[TB:, :] = jnp.zeros((POOL_HALO, D), F32)
            vacc[...] = jnp.zeros_like(vacc)
            gpw_ref[...] = jnp.zeros_like(gpw_ref)

        ebuf[0:POOL_HALO, :] = jnp.where(first, 0.0, xprev_ref[...])
        ebuf[POOL_HALO:, :] = xp_ref[...]
        pooled, inv = _pooled(ebuf, blk(i) * TB, TB)
        pooled = [p.astype(BF16) for p in pooled]
        yp = jnp.concatenate([jnp.dot(pooled[g], pw_ref[g], preferred_element_type=F32)
                              for g in range(NGRP)], axis=1) + pb_ref[...]
        silu, dsilu = _silu_parts(g_ref[...])
        dy = dy_ref[...]
        ps = ps_ref[...]
        dyp_v = dy * ps * silu
        vacc[0] += _rowsum8(dy * yp * silu)
        vacc[1] += _rowsum8(dyp_v)
        dg_ref[...] = (dy * yp * ps * dsilu).astype(BF16)
        dypb = dyp_v.astype(BF16)
        for g in range(NGRP):
            cs = slice(g * GD, (g + 1) * GD)
            gpw_ref[g] += lax.dot_general(pooled[g], dypb[:, cs], (((0,), (0,)), ((), ())),
                                          preferred_element_type=F32)
            dpool = lax.dot_general(dypb[:, cs], pw_ref[g], (((1,), (1,)), ((), ())),
                                    preferred_element_type=F32)
            qbuf[0:TB, cs] = dpool * inv[g]
            L = qbuf[:, cs]
            for lev in range(g + 1):
                L = L + pltpu.roll(L, TE - (1 << lev), axis=0)
            dxp_ref[:, cs] = (L[0:TB, :] - dpool).astype(BF16)
        qbuf[TB:, :] = qbuf[0:POOL_HALO, :]

        @pl.when(i == nb - 1)
        def _():
            gps_ref[...] = _sum8(vacc[0])
            gpb_ref[...] = _sum8(vacc[1])

    rowb = pl.BlockSpec((TB, D), lambda i: (blk(i), 0))
    return _pcall(
        body, name="pool_bwd", grid=(nb,),
        in_specs=[rowb, pl.BlockSpec((TB, D), lambda i: (blk(i), 2)),
                  pl.BlockSpec((POOL_HALO, D), lambda i: (jnp.maximum(blk(i) * (TB // POOL_HALO) - 1, 0), 2)),
                  pl.BlockSpec((TB, D), lambda i: (blk(i), 3)), _layer(l, (NGRP, GD, GD)), _vec(l), _vec(l)],
        out_specs=[rowb, rowb, _full((NGRP, GD, GD)), _full((1, D)), _full((1, D))],
        out_shape=[jax.ShapeDtypeStruct((S, D), BF16), jax.ShapeDtypeStruct((S, D), BF16),
                   jax.ShapeDtypeStruct((NGRP, GD, GD), F32), jax.ShapeDtypeStruct((1, D), F32),
                   jax.ShapeDtypeStruct((1, D), F32)],
        scratch_shapes=[pltpu.VMEM((TE, D), F32), pltpu.VMEM((TE, D), F32), pltpu.VMEM((2, SUB, D), F32)],
        args=(dyp, proj, proj, proj, P['pool_w'], P['pool_b'], P['pool_scale']), rider=rider)


def _in_bwd(dq, w_in_l, x, dxo, P, mod, l, rider=None):
    S = x.shape[0]
    TM = min(256, S)
    NB = w_in_l.shape[2]
    nsteps = S // TM
    per_q = D // NB

    def body(d0, d1, d2, d3, w_ref, x_ref, dxo_ref, g_ref, sc_ref, dx_ref, dsh_ref, dsc_ref, dg_ref, vacc):
        i = pl.program_id(0)

        @pl.when(i == 0)
        def _():
            vacc[...] = jnp.zeros_like(vacc)
        dref = (d0, d1, d2, d3)
        dh = jnp.zeros((TM, D), F32)
        for j in range(NDEV):
            c0 = (j % per_q) * NB
            dh = dh + lax.dot_general(dref[j // per_q][:, c0:c0 + NB], w_ref[j], (((1,), (1,)), ((), ())),
                                      preferred_element_type=F32)
        xv = x_ref[...]
        rstd = lax.rsqrt(jnp.mean(xv * xv, axis=-1, keepdims=True) + NORM_EPS)
        xn = xv * rstd
        g, sc = g_ref[...], 1.0 + sc_ref[...]
        vacc[0] += _rowsum8(dh)
        vacc[1] += _rowsum8(dh * (xn * g))
        vacc[2] += _rowsum8(dh * sc * xn)
        dxn = dh * sc * g
        dx_ref[...] = dxo_ref[...] + rstd * (dxn - xn * jnp.mean(dxn * xn, axis=-1, keepdims=True))

        @pl.when(i == nsteps - 1)
        def _():
            dsh_ref[...] = _sum8(vacc[0])
            dsc_ref[...] = _sum8(vacc[1])
            dg_ref[...] = _sum8(vacc[2])

    row = pl.BlockSpec((TM, D), lambda i: (i, 0))
    return _pcall(
        body, name="in_bwd", grid=(nsteps,),
        in_specs=[row, row, row, row, _full((NDEV, D, NB)), row, row, _vec(l), _vec(l, 1)],
        out_specs=[row, _full((1, D)), _full((1, D)), _full((1, D))],
        out_shape=[jax.ShapeDtypeStruct((S, D), F32)] + [jax.ShapeDtypeStruct((1, D), F32)] * 3,
        scratch_shapes=[pltpu.VMEM((3, SUB, D), F32)],
        args=(*dq, w_in_l, x, dxo, P['pre_norm_g'], mod), rider=rider)


def _grad_w_in(h, dq, NB, rider=None):
    S, RH = h.shape
    TK = min(1024, S)
    nk = S // TK

    def body(h_ref, d0, d1, d2, d3, o_ref, acc):
        k = pl.program_id(0)

        @pl.when(k == 0)
        def _():
            acc[...] = jnp.zeros_like(acc)
        hv = h_ref[...]
        for q, d_ref in enumerate((d0, d1, d2, d3)):
            acc[:, q * D:(q + 1) * D] += lax.dot_general(hv, d_ref[...], (((0,), (0,)), ((), ())),
                                                         preferred_element_type=F32)

        @pl.when(k == nk - 1)
        def _():
            for j in range(NDEV):
                o_ref[j] = acc[:, j * NB:(j + 1) * NB].astype(BF16)

    row = pl.BlockSpec((TK, D), lambda k: (k, 0))
    return _pcall(
        body, name="grad_w_in", grid=(nk,),
        in_specs=[pl.BlockSpec((TK, RH), lambda k: (k, 0)), row, row, row, row],
        out_specs=[_full((NDEV, RH, NB))],
        out_shape=[jax.ShapeDtypeStruct((NDEV, RH, NB), BF16)],
        scratch_shapes=[pltpu.VMEM((RH, NQ * D), F32)],
        args=(h, *dq), rider=rider)


def _adamw_math(g, w, m, v):
    m2 = ADAM_B1 * m + (1.0 - ADAM_B1) * g
    v2 = ADAM_B2 * v + (1.0 - ADAM_B2) * (g * g)
    m_hat = m2 / (1.0 - ADAM_B1 ** ADAM_STEP)
    v_hat = v2 / (1.0 - ADAM_B2 ** ADAM_STEP)
    delta = -ADAM_LR * (m_hat / (jnp.sqrt(v_hat) + ADAM_EPS) + ADAM_WD * w)
    return delta, m2, v2


def _adamw(name, gs, w, m, v, TR, row0=0, into=None):
    L = len(gs)
    n, Rp, C = gs[0].shape
    R = w.shape[1]
    b0 = row0 // TR

    def body(*refs):
        g_refs = refs[:L]
        w_ref, m_ref, v_ref = refs[L:L + 3]
        go_ref, do_ref, mo_ref, vo_ref = refs[-4:]
        lay = pl.program_id(0)
        for li in range(L):
            @pl.when(lay == li)
            def _(li=li):
                g = g_refs[li][0].astype(F32)
                for s in range(1, n):
                    g = g + g_refs[li][s].astype(F32)
                delta, m2, v2 = _adamw_math(g, w_ref[...], m_ref[...], v_ref[...])
                go_ref[...] = g
                do_ref[...] = delta
                mo_ref[...] = m2
                vo_ref[...] = v2

    lrc = pl.BlockSpec((None, TR, C), lambda lay, r: (lay, r + b0, 0))
    g_specs = [pl.BlockSpec((n, TR, C), lambda lay, r, li=li: (0, jnp.where(lay == li, r, 0), 0))
               for li in range(L)]
    in_specs, args, aliases = g_specs + [lrc, lrc, lrc], [*gs, w, m, v], None
    if into is not None:
        aliases = {len(args) + k: k for k in range(4)}
        in_specs = in_specs + [pl.BlockSpec(memory_space=pl.ANY)] * 4
        args = args + list(into)
    return _pcall(
        body, name=name, grid=(L, Rp // TR), in_specs=in_specs, out_specs=[lrc] * 4,
        out_shape=[jax.ShapeDtypeStruct((L, R, C), F32)] * 4, args=args, aliases=aliases)


def _ada_adamw(c_all_t, dm, w, m, v, rider=None):
    L, _, nc = w.shape

    def body(c_ref, dm_ref, w_ref, m_ref, v_ref, go_ref, do_ref, mo_ref, vo_ref):
        cv = c_ref[...]
        ca = cv * jax.nn.sigmoid(cv)
        dmv = dm_ref[...]
        g = ca[:, 0:1] * dmv[0:1, :]
        for b in range(1, NDEV):
            g = g + ca[:, b:b + 1] * dmv[b:b + 1, :]
        delta, m2, v2 = _adamw_math(g, w_ref[...], m_ref[...], v_ref[...])
        go_ref[...] = g
        do_ref[...] = delta
        mo_ref[...] = m2
        vo_ref[...] = v2

    big = pl.BlockSpec((None, D, nc), lambda lay: (lay, 0, 0))
    return _pcall(
        body, name="ada_adamw", grid=(L,),
        in_specs=[_full((D, NDEV)), pl.BlockSpec((None, NDEV, nc), lambda lay: (lay, 0, 0)), big, big, big],
        out_specs=[big] * 4, out_shape=[jax.ShapeDtypeStruct((L, D, nc), F32)] * 4,
        args=(c_all_t, dm, w, m, v), rider=rider)


def _sum_slots(recv):
    n, R, C = recv.shape

    def body(r_ref, o_ref):
        acc = r_ref[0].astype(F32)
        for s in range(1, n):
            acc = acc + r_ref[s].astype(F32)
        o_ref[...] = acc

    return pl.pallas_call(body, name="sum_slots", out_shape=jax.ShapeDtypeStruct((R, C), F32),
                          compiler_params=_cparams())(recv)


def _pad_rows(a, rows):
    return jnp.pad(a, ((0, rows - a.shape[0]), (0, 0)))


def _pack_sharded_block(pool_w, pool_b, conv_w):
    return jnp.concatenate([pool_w.reshape(-1, PACK_C), _pad_rows(pool_b.reshape(-1, PACK_C), SUB),
                            _pad_rows(conv_w.reshape(-1, PACK_C), SUB)], axis=0)


def _unpack_sharded_block(p):
    n_pw = DEPTH * NGRP * (GD // NDEV)
    pool_w = p[:n_pw].reshape(DEPTH, NGRP, GD // NDEV, GD)
    pool_b = p[n_pw].reshape(DEPTH, NGRP, GD // NDEV)
    conv_w = p[n_pw + SUB:n_pw + SUB + DEPTH * CONV_K * (D // NDEV) // PACK_C].reshape(DEPTH, CONV_K, D // NDEV)
    return pool_w, pool_b, conv_w


def _blocks_of_full(pool_w, pool_b, conv_w):
    pw = pool_w.reshape(DEPTH, NGRP, NDEV, GD // NDEV, GD).transpose(2, 0, 1, 3, 4).reshape(NDEV, -1, PACK_C)
    pb = pool_b.reshape(DEPTH, NGRP, NDEV, GD // NDEV).transpose(2, 0, 1, 3).reshape(NDEV, -1, PACK_C)
    cw = conv_w.reshape(DEPTH, CONV_K, NDEV, D // NDEV).transpose(2, 0, 1, 3).reshape(NDEV, -1, PACK_C)
    pad = lambda a: jnp.pad(a, ((0, 0), (0, SUB - a.shape[1]), (0, 0)))
    return jnp.concatenate([pw, pad(pb), pad(cw)], axis=1)


def _full_of_blocks(p):
    n_pw = DEPTH * NGRP * (GD // NDEV)
    pool_w = p[:, :n_pw].reshape(NDEV, DEPTH, NGRP, GD // NDEV, GD).transpose(1, 2, 0, 3, 4)
    pool_b = p[:, n_pw].reshape(NDEV, DEPTH, NGRP, GD // NDEV).transpose(1, 2, 0, 3)
    n_cw = DEPTH * CONV_K * (D // NDEV) // PACK_C
    conv_w = p[:, n_pw + SUB:n_pw + SUB + n_cw].reshape(NDEV, DEPTH, CONV_K, D // NDEV).transpose(1, 2, 0, 3)
    return (pool_w.reshape(DEPTH, NGRP, GD, GD), pool_b.reshape(DEPTH, NGRP, GD),
            conv_w.reshape(DEPTH, CONV_K, D))


def _pack_replicated(t, keys, rows):
    p = jnp.concatenate([t[k].reshape(-1, PACK_C) for k in keys], axis=0)
    return _pad_rows(p, rows)


def _unpack_replicated(p, like, keys):
    out, r0 = {}, 0
    for k in keys:
        rows = like[k].size // PACK_C
        out[k] = p[r0:r0 + rows].reshape(like[k].shape)
        r0 += rows
    return out


def kernel(x, c, ada_w, ada_b, pre_norm_g, w_in, conv_w, conv_b, gate_a_w, gate_a_b, gate_x_w, gate_x_b, lru_lambda, pool_w, pool_b, pool_scale, w_out, post_norm_g, loss_target, m_ada_w, m_ada_b, m_pre_norm_g, m_w_in, m_conv_w, m_conv_b, m_gate_a_w, m_gate_a_b, m_gate_x_w, m_gate_x_b, m_lru_lambda, m_pool_w, m_pool_b, m_pool_scale, m_w_out, m_post_norm_g, v_ada_w, v_ada_b, v_pre_norm_g, v_w_in, v_conv_w, v_conv_b, v_gate_a_w, v_gate_a_b, v_gate_x_w, v_gate_x_b, v_lru_lambda, v_pool_w, v_pool_b, v_pool_scale, v_w_out, v_post_norm_g):
    W = dict(ada_w=ada_w, ada_b=ada_b, pre_norm_g=pre_norm_g, w_in=w_in, conv_w=conv_w, conv_b=conv_b,
             gate_a_w=gate_a_w, gate_a_b=gate_a_b, gate_x_w=gate_x_w, gate_x_b=gate_x_b, lru_lambda=lru_lambda,
             pool_w=pool_w, pool_b=pool_b, pool_scale=pool_scale, w_out=w_out, post_norm_g=post_norm_g)
    M = dict(ada_w=m_ada_w, ada_b=m_ada_b, pre_norm_g=m_pre_norm_g, w_in=m_w_in, conv_w=m_conv_w,
             conv_b=m_conv_b, gate_a_w=m_gate_a_w, gate_a_b=m_gate_a_b, gate_x_w=m_gate_x_w,
             gate_x_b=m_gate_x_b, lru_lambda=m_lru_lambda, pool_w=m_pool_w, pool_b=m_pool_b,
             pool_scale=m_pool_scale, w_out=m_w_out, post_norm_g=m_post_norm_g)
    V = dict(ada_w=v_ada_w, ada_b=v_ada_b, pre_norm_g=v_pre_norm_g, w_in=v_w_in, conv_w=v_conv_w,
             conv_b=v_conv_b, gate_a_w=v_gate_a_w, gate_a_b=v_gate_a_b, gate_x_w=v_gate_x_w,
             gate_x_b=v_gate_x_b, lru_lambda=v_lru_lambda, pool_w=v_pool_w, pool_b=v_pool_b,
             pool_scale=v_pool_scale, w_out=v_w_out, post_norm_g=v_post_norm_g)
    S = x.shape[1]
    me = 4 * lax.axis_index("x") + 2 * lax.axis_index("y") + lax.axis_index("c")
    xs = x.reshape(S, D)
    tgt = loss_target.reshape(S, D)
    nc = ada_w.shape[2]
    NB = w_in.shape[2]
    w_in_b, w_out_b = w_in.astype(BF16), w_out.astype(BF16)
    rows = lambda a: a.reshape(DEPTH, 1, D)
    P = dict(pre_norm_g=rows(pre_norm_g), conv_b=rows(conv_b), gate_a_b=rows(gate_a_b), gate_x_b=rows(gate_x_b),
             lru_lambda=rows(lru_lambda), pool_scale=rows(pool_scale), post_norm_g=rows(post_norm_g),
             gate_a_w=gate_a_w.astype(BF16), gate_x_w=gate_x_w.astype(BF16))

    c_slots, w_in0 = _exchange("gather_c_w_in0", _AllGatherVia([jnp.broadcast_to(c, (SUB, D)), w_in_b[0]]))
    c_all = c_slots[:, 0, :]
    (mod_slots,) = _exchange("gather_mod", _Direct(ag=[_mod_cols(c_all, ada_w)]))
    mod = lax.dynamic_index_in_dim(mod_slots, me, axis=1, keepdims=False)
    mod = (mod.reshape(NDEV, DEPTH, nc).transpose(1, 0, 2).reshape(DEPTH, 3 * D) + ada_b).reshape(DEPTH, 3, 1, D)

    w_in_all, w_out_all = [w_in0, None], [None, None]
    saved = []
    xl = xs
    flat = lambda w_slots: w_slots.reshape(2 * D, D)
    for l in range(DEPTH):
        rider = _AllGather2([_pack_sharded_block(pool_w, pool_b, conv_w), w_out_b[0]]) if l == 0 else None
        (h_a, h_b, proj), got = _pre_proj(xl, P, mod, w_in_all[l], l, rider=rider)
        if l == 0:
            pool_w_f, pool_b_f, conv_w_f = _full_of_blocks(got[0])
            P.update(conv_w=conv_w_f, pool_w=pool_w_f.astype(BF16), pool_b=rows(pool_b_f))
            w_out_all[0] = flat(got[1])
        rider = _AllGather2([w_in_b[1]]) if l == 0 else None
        (hs, yr, *fwd), got = _rnn_fwd(proj, P, l, rider=rider)
        if l == 0:
            w_in_all[1] = got[0]
        (yp,), _ = _pool_fwd(proj, P, l)
        if l == DEPTH - 1:
            (y, x_next, loss_acc), _ = _out_post(yr, yp, w_out_all[l], xl, mod, P, l, tgt)
        else:
            (y, x_next), got = _out_post(yr, yp, w_out_all[l], xl, mod, P, l, rider=_AllGather2([w_out_b[1]]))
            w_out_all[1] = flat(got[0])
        saved.append((xl, h_a, h_b, proj, hs, fwd, yr, yp, y))
        xl = x_next

    dxo = xl
    G = {k: [None] * DEPTH for k in WEIGHTS}
    dmod = [None] * DEPTH
    recv_in, recv_out = [[None, None] for _ in range(DEPTH)], [None] * DEPTH
    full = dict(conv_w=(CONV_K, D), pool_w=(NGRP, GD, GD), pool_b=(NGRP, GD))
    stack = lambda k: jnp.stack([g.reshape(full.get(k, W[k].shape[1:])) for g in G[k]])
    gw_bot_prev = None
    for l in reversed(range(DEPTH)):
        xin, h_a, h_b, proj, hs, fwd, yr, yp, y = saved[l]
        rider = _AllGather2([loss_acc]) if gw_bot_prev is None else _Direct(a2a=[gw_bot_prev])
        (dyr, dyp, gw_out, dgate, G['post_norm_g'][l]), got = _out_bwd(dxo, y, yr, yp, w_out_all[l], mod, P, l,
                                                                       rider=rider)
        if gw_bot_prev is None:
            loss = (0.5 / D) * jnp.sum(got[0])
        else:
            recv_in[l + 1][1] = got[0]
        ((dxr, dgr, G['conv_w'][l], G['conv_b'][l], G['gate_a_w'][l], G['gate_a_b'][l], G['gate_x_w'][l],
          G['gate_x_b'][l], G['lru_lambda'][l]), (recv_out[l],)) = _rnn_bwd(
            dyr, hs, fwd, proj, P, l,
            rider=_Direct(a2a=[gw_out.reshape(NDEV, 2 * D // NDEV, D)]))
        (dxp, dgp, G['pool_w'][l], G['pool_b'][l], G['pool_scale'][l]), _ = _pool_bwd(dyp, proj, P, l)
        dq = (dxr, dgr, dxp, dgp)
        if l > 0:
            (gw_top,), _ = _grad_w_in(h_a, dq, NB)
            (dxo, dshift, dscale, G['pre_norm_g'][l]), (recv_in[l][0],) = _in_bwd(
                dq, w_in_all[l], xin, dxo, P, mod, l, rider=_Direct(a2a=[gw_top]))
            (gw_bot_prev,), _ = _grad_w_in(h_b, dq, NB)
        else:
            Ge = {k: stack(k) for k in REP_EARLY + ['pool_w', 'pool_b', 'conv_w']}
            early = jnp.concatenate([_blocks_of_full(Ge['pool_w'], Ge['pool_b'], Ge['conv_w']),
                                     _pack_replicated(Ge, REP_EARLY, NDEV * PACK_ROWS).reshape(NDEV, PACK_ROWS, PACK_C)],
                                    axis=1).astype(BF16)
            (gw_top,), (early_recv,) = _grad_w_in(h_a, dq, NB, rider=_Direct(a2a=[early]))
            early_sum = _sum_slots(early_recv)
            (gw_bot,), (recv_in[l][0],) = _grad_w_in(h_b, dq, NB, rider=_Direct(a2a=[gw_top]))
            (dxo, dshift, dscale, G['pre_norm_g'][l]), (recv_in[l][1], early_all) = _in_bwd(
                dq, w_in_all[l], xin, dxo, P, mod, l,
                rider=_Both(_Direct(a2a=[gw_bot]), _AllGather2([early_sum[PACK_ROWS:]])))
        dmod[l] = jnp.concatenate([dshift, dscale, dgate], axis=1)
    grad_x = dxo.reshape(x.shape)

    Gl = dict(ada_b=jnp.concatenate(dmod, axis=0), pre_norm_g=stack('pre_norm_g'))
    (late_slots,) = _exchange("gather_late", _AllGather2([_pack_replicated(Gl, REP_LATE, LATE_PACK_ROWS)]))
    late_sum = _sum_slots(late_slots)
    dmod_all = late_slots[:, :DEPTH * 3 * D // PACK_C]

    out = {}
    first, _ = _adamw("adamw_w_in_a", [recv_in[l][0] for l in range(DEPTH)], w_in, M['w_in'], V['w_in'], LANE)
    out['w_in'], _ = _adamw("adamw_w_in_b", [recv_in[l][1] for l in range(DEPTH)], w_in, M['w_in'], V['w_in'], LANE,
                            row0=GW_SPLIT, into=first)
    out['w_out'], _ = _adamw("adamw_w_out", recv_out, w_out, M['w_out'], V['w_out'], 256)
    dm = lax.dynamic_slice_in_dim(dmod_all.reshape(NDEV, DEPTH, 3 * D), me * nc, nc, axis=2)
    out['ada_w'], _ = _ada_adamw(c_all.T, dm.transpose(1, 0, 2), ada_w, M['ada_w'], V['ada_w'])
    g_small = jnp.concatenate([early_sum[:PACK_ROWS], early_all.reshape(NDEV * PACK_ROWS, PACK_C), late_sum],
                              axis=0)

    def packs(T):
        return jnp.concatenate([_pack_sharded_block(T['pool_w'], T['pool_b'], T['conv_w']),
                                _pack_replicated(T, REP_EARLY, NDEV * PACK_ROWS),
                                _pack_replicated(T, REP_LATE, LATE_PACK_ROWS)], axis=0)[None]
    res_small, _ = _adamw("adamw_small", [g_small[None]], packs(W), packs(M), packs(V), g_small.shape[0] // 2)
    n_early = (1 + NDEV) * PACK_ROWS
    for idx in range(4):
        p = res_small[idx][0]
        pw_, pb_, cw_ = _unpack_sharded_block(p[:PACK_ROWS])
        rep = _unpack_replicated(p[PACK_ROWS:n_early], W, REP_EARLY)
        rep.update(_unpack_replicated(p[n_early:], W, REP_LATE))
        rep.update(pool_w=pw_, pool_b=pb_, conv_w=cw_)
        for k, a in rep.items():
            out.setdefault(k, [None] * 4)[idx] = a
    for k in ('w_in', 'w_out', 'ada_w'):
        out[k] = [a.reshape(W[k].shape) for a in out[k]]

    return (loss, grad_x, *[out[k][0] for k in WEIGHTS], *[out[k][1] for k in WEIGHTS],
            *[out[k][2] for k in WEIGHTS], *[out[k][3] for k in WEIGHTS])
```

```python
import functools

import jax
import jax.numpy as jnp
from jax import lax
from jax.experimental import pallas as pl
from jax.experimental.pallas import tpu as pltpu

F32, BF16 = jnp.float32, jnp.bfloat16
MESH = pl.DeviceIdType.MESH
HIGHEST = lax.Precision.HIGHEST

NDEV = 8
DEPTH = 2
D = 1024
NHEAD, HD = 8, 128
NGRP, GD = 4, 256
WINS = (2, 4, 8, 16)
CONV_K = 4
CONV_HALO = 8
POOL_HALO = 16
LRU_C = 8.0
NORM_EPS = 1e-6
ADAM_LR, ADAM_B1, ADAM_B2, ADAM_EPS, ADAM_WD, ADAM_STEP = 0.001, 0.9, 0.999, 1e-08, 0.01, 10
VMEM_LIMIT = 56 * 1024 * 1024
NQ = 4
SUB = 8
LANE = 128
PACK_C = 256
PACK_ROWS = 272

WEIGHTS = ['ada_w', 'ada_b', 'pre_norm_g', 'w_in', 'conv_w', 'conv_b', 'gate_a_w', 'gate_a_b', 'gate_x_w',
           'gate_x_b', 'lru_lambda', 'pool_w', 'pool_b', 'pool_scale', 'w_out', 'post_norm_g']
REP_EARLY = ['conv_b', 'gate_a_w', 'gate_a_b', 'gate_x_w', 'gate_x_b', 'lru_lambda', 'pool_scale', 'post_norm_g']
REP_LATE = ['ada_b', 'pre_norm_g']
GW_SPLIT = 512
LATE_PACK_ROWS = 32


def _cparams(*sem):
    return pltpu.CompilerParams(dimension_semantics=sem, vmem_limit_bytes=VMEM_LIMIT)


def _vec(l, k=None):
    if k is None:
        return pl.BlockSpec((None, 1, D), lambda *_: (l, 0, 0))
    return pl.BlockSpec((None, None, 1, D), lambda *_: (l, k, 0, 0))


def _layer(l, shape):
    nd = len(shape)
    return pl.BlockSpec((None,) + tuple(shape), lambda *_: (l,) + (0,) * nd)


def _full(shape):
    nd = len(shape)
    return pl.BlockSpec(shape, lambda *_: (0,) * nd)


def _rowsum8(z):
    return z.reshape(z.shape[0] // SUB, SUB, z.shape[1]).sum(axis=0)


def _sum8(acc):
    return jnp.sum(acc, axis=0, keepdims=True)


def _sigmoid(z):
    return 0.5 * jnp.tanh(0.5 * z) + 0.5


def _silu_parts(g):
    sg = _sigmoid(g)
    return g * sg, sg * (1.0 + g * (1.0 - sg))


def _one_minus_sq(a, log_a):
    z = 2.0 * log_a
    p = 1.0 / 24.0
    for k in (6.0, 2.0, 1.0):
        p = p * z + 1.0 / k
    return jnp.where(z > -0.03, -(p * z), 1.0 - a * a)


def _place():
    x, y, c = lax.axis_index("x"), lax.axis_index("y"), lax.axis_index("c")
    return x, y, c, 4 * x + 2 * y + c


class _Direct:
    def __init__(self, a2a=(), ag=()):
        self.arrays = list(a2a) + list(ag)
        self.n_a, self.n = len(a2a), len(self.arrays)
        self.out_shape = ([jax.ShapeDtypeStruct(a.shape, a.dtype) for a in a2a]
                          + [jax.ShapeDtypeStruct((NDEV,) + a.shape, a.dtype) for a in ag])
        self.scratch = [pltpu.SemaphoreType.DMA((self.n, NDEV - 1)), pltpu.SemaphoreType.DMA((self.n, NDEV - 1)),
                        pltpu.SemaphoreType.DMA((self.n,))]

    def _copies(self, ins, outs, sems):
        send_sems, recv_sems, local_sems = sems
        x, y, c, me = _place()
        local, remote = [], []
        for t in range(self.n):
            src = ins[t].at[me] if t < self.n_a else ins[t]
            local.append(pltpu.make_async_copy(src, outs[t].at[me], local_sems.at[t]))
        for r in range(1, NDEV):
            px = 1 - x if r & 4 else x
            py = 1 - y if r & 2 else y
            pc = 1 - c if r & 1 else c
            for t in range(self.n):
                src = ins[t].at[4 * px + 2 * py + pc] if t < self.n_a else ins[t]
                remote.append(pltpu.make_async_remote_copy(
                    src_ref=src, dst_ref=outs[t].at[me], send_sem=send_sems.at[t, r - 1],
                    recv_sem=recv_sems.at[t, r - 1], device_id=(px, py, pc), device_id_type=MESH))
        return local, remote

    def start(self, ins, outs, sems):
        local, remote = self._copies(ins, outs, sems)
        for cp in local + remote:
            cp.start()

    def finish(self, ins, outs, sems):
        local, remote = self._copies(ins, outs, sems)
        for cp in remote + local:
            cp.wait()


class _AllGather2:
    def __init__(self, arrays):
        self.arrays = list(arrays)
        self.n = len(self.arrays)
        self.out_shape = [jax.ShapeDtypeStruct((NDEV,) + a.shape, a.dtype) for a in self.arrays]
        self.scratch = [pltpu.SemaphoreType.DMA((self.n, NDEV - 1)), pltpu.SemaphoreType.DMA((self.n, NDEV - 1)),
                        pltpu.SemaphoreType.DMA((self.n,))]

    @staticmethod
    def _chips(x, y):
        return [(1 - x, y), (x, 1 - y), (1 - x, 1 - y)]

    def _copy(self, t, k, src, dst, to, sems):
        return pltpu.make_async_remote_copy(src_ref=src, dst_ref=dst, send_sem=sems[0].at[t, k],
                                            recv_sem=sems[1].at[t, k], device_id=to, device_id_type=MESH)

    def start(self, ins, outs, sems):
        x, y, c, me = _place()
        for t in range(self.n):
            pltpu.make_async_copy(ins[t], outs[t].at[me], sems[2].at[t]).start()
            self._copy(t, 0, ins[t], outs[t].at[me], (x, y, 1 - c), sems).start()
            for j, (px, py) in enumerate(self._chips(x, y)):
                self._copy(t, 1 + j, ins[t], outs[t].at[me], (px, py, c), sems).start()

    def middle(self, ins, outs, sems):
        x, y, c, me = _place()
        sib = (x, y, 1 - c)
        for j, (px, py) in enumerate(self._chips(x, y)):
            slot = 4 * px + 2 * py + c
            for t in range(self.n):
                self._copy(t, 1 + j, ins[t], outs[t].at[slot], sib, sems).wait_recv()
                self._copy(t, 4 + j, outs[t].at[slot], outs[t].at[slot], sib, sems).start()

    def finish(self, ins, outs, sems):
        x, y, c, me = _place()
        sib = (x, y, 1 - c)
        for t in range(self.n):
            for k in (0, 4, 5, 6):
                self._copy(t, k, ins[t], outs[t].at[me], sib, sems).wait_recv()
        for t in range(self.n):
            for k in range(NDEV - 1):
                self._copy(t, k, ins[t], outs[t].at[me], sib, sems).wait_send()
            pltpu.make_async_copy(ins[t], outs[t].at[me], sems[2].at[t]).wait()


class _AllGatherVia:
    def __init__(self, arrays):
        self.arrays = list(arrays)
        self.n = len(self.arrays)
        self.out_shape = [jax.ShapeDtypeStruct((NDEV,) + a.shape, a.dtype) for a in self.arrays]
        self.scratch = [pltpu.SemaphoreType.DMA((self.n, NDEV - 1)), pltpu.SemaphoreType.DMA((self.n, NDEV - 1)),
                        pltpu.SemaphoreType.DMA((self.n,))]

    def _copy(self, t, k, src, dst, to, sems):
        return pltpu.make_async_remote_copy(src_ref=src, dst_ref=dst, send_sem=sems[0].at[t, k],
                                            recv_sem=sems[1].at[t, k], device_id=to, device_id_type=MESH)

    def start(self, ins, outs, sems):
        x, y, c, me = _place()
        for t in range(self.n):
            pltpu.make_async_copy(ins[t], outs[t].at[me], sems[2].at[t]).start()
            self._copy(t, 0, ins[t], outs[t].at[me], (x, y, 1 - c), sems).start()
            self._copy(t, 1, ins[t], outs[t].at[me], (1 - x, y, c), sems).start()
            self._copy(t, 2, ins[t], outs[t].at[me], (x, 1 - y, c), sems).start()

    def finish(self, ins, outs, sems):
        x, y, c, me = _place()
        sib = (x, y, 1 - c)
        slot_x, slot_y, slot_d = 4 * (1 - x) + 2 * y + c, 4 * x + 2 * (1 - y) + c, 4 * (1 - x) + 2 * (1 - y) + c
        slot_on = c * slot_x + (1 - c) * slot_y
        to_on = (c * x + (1 - c) * (1 - x), c * (1 - y) + (1 - c) * y, c)
        for t in range(self.n):
            self._copy(t, 1, ins[t], outs[t].at[slot_x], sib, sems).wait_recv()
            self._copy(t, 2, ins[t], outs[t].at[slot_y], sib, sems).wait_recv()
            self._copy(t, 3, outs[t].at[slot_on], outs[t].at[slot_on], to_on, sems).start()
            self._copy(t, 4, outs[t].at[slot_x], outs[t].at[slot_x], sib, sems).start()
            self._copy(t, 5, outs[t].at[slot_y], outs[t].at[slot_y], sib, sems).start()
        for t in range(self.n):
            self._copy(t, 3, ins[t], outs[t].at[slot_d], sib, sems).wait_recv()
            self._copy(t, 6, outs[t].at[slot_d], outs[t].at[slot_d], sib, sems).start()
        for t in range(self.n):
            for k in (0, 4, 5, 6):
                self._copy(t, k, ins[t], outs[t].at[me], sib, sems).wait_recv()
        for t in range(self.n):
            for k in range(NDEV - 1):
                self._copy(t, k, ins[t], outs[t].at[me], sib, sems).wait_send()
            pltpu.make_async_copy(ins[t], outs[t].at[me], sems[2].at[t]).wait()


class _Both:
    def __init__(self, *riders):
        self.riders = riders
        self.arrays = [a for r in riders for a in r.arrays]
        self.n = len(self.arrays)
        self.out_shape = [o for r in riders for o in r.out_shape]
        self.scratch = [s for r in riders for s in r.scratch]

    def _parts(self, ins, outs, sems):
        p, q = 0, 0
        for r in self.riders:
            yield r, ins[p:p + r.n], outs[p:p + r.n], sems[q:q + len(r.scratch)]
            p, q = p + r.n, q + len(r.scratch)

    def start(self, ins, outs, sems):
        for r, i, o, s in self._parts(ins, outs, sems):
            r.start(i, o, s)

    def finish(self, ins, outs, sems):
        for r, i, o, s in self._parts(ins, outs, sems):
            if hasattr(r, "middle"):
                r.middle(i, o, s)
            r.finish(i, o, s)


def _exchange(name, rider):
    n = rider.n

    def body(*refs):
        rider.start(refs[:n], refs[n:2 * n], refs[2 * n:])
        if hasattr(rider, "middle"):
            rider.middle(refs[:n], refs[n:2 * n], refs[2 * n:])
        rider.finish(refs[:n], refs[n:2 * n], refs[2 * n:])

    any_spec = pl.BlockSpec(memory_space=pl.ANY)
    return list(pl.pallas_call(body, name=name, out_shape=rider.out_shape, in_specs=[any_spec] * n,
                               out_specs=[any_spec] * n, scratch_shapes=rider.scratch)(*rider.arrays))


def _pcall(body, *, name, grid, in_specs, out_specs, out_shape, args, scratch_shapes=(), rider=None, aliases=None):
    params = _cparams(*(("arbitrary",) * len(grid)))
    if rider is None:
        res = pl.pallas_call(body, name=name, grid=grid, in_specs=in_specs, out_specs=out_specs,
                             out_shape=out_shape, scratch_shapes=list(scratch_shapes),
                             input_output_aliases=aliases or {}, compiler_params=params)(*args)
        return list(res), []
    n_in, n_out, n_scr, rn = len(in_specs), len(out_specs), len(scratch_shapes), rider.n

    def wrapped(*refs):
        cuts = [n_in, rn, n_out, rn, n_scr]
        parts, p = [], 0
        for n in cuts:
            parts.append(refs[p:p + n])
            p += n
        ins, r_in, outs, r_out, scr = parts
        sems = refs[p:]
        ids = [pl.program_id(a) for a in range(len(grid))]
        first = functools.reduce(jnp.logical_and, [i == 0 for i in ids])
        last = functools.reduce(jnp.logical_and, [i == g - 1 for i, g in zip(ids, grid)])

        @pl.when(first)
        def _():
            rider.start(r_in, r_out, sems)
        body(*ins, *outs, *scr)
        early = hasattr(rider, "middle") and len(grid) == 1 and grid[0] >= 4
        if early:
            @pl.when(ids[0] == (3 * grid[0]) // 4)
            def _():
                rider.middle(r_in, r_out, sems)

        @pl.when(last)
        def _():
            if hasattr(rider, "middle") and not early:
                rider.middle(r_in, r_out, sems)
            rider.finish(r_in, r_out, sems)

    any_spec = pl.BlockSpec(memory_space=pl.ANY)
    res = pl.pallas_call(
        wrapped, name=name, grid=grid, in_specs=list(in_specs) + [any_spec] * rn,
        out_specs=list(out_specs) + [any_spec] * rn, out_shape=list(out_shape) + rider.out_shape,
        scratch_shapes=list(scratch_shapes) + rider.scratch, compiler_params=params)(*args, *rider.arrays)
    return list(res[:n_out]), list(res[n_out:])


def _mod_cols(c_all, ada_w):
    nc = ada_w.shape[2]

    def body(c_ref, w_ref, o_ref):
        cv = c_ref[...]
        ca = cv * jax.nn.sigmoid(cv)
        for l in range(DEPTH):
            o_ref[:, l * nc:(l + 1) * nc] = jnp.dot(ca, w_ref[l], precision=HIGHEST, preferred_element_type=F32)

    return pl.pallas_call(body, name="mod_cols", out_shape=jax.ShapeDtypeStruct((NDEV, DEPTH * nc), F32),
                          compiler_params=_cparams())(c_all, ada_w)


def _pre_proj(x, P, mod, w_in_l, l, rider=None):
    S = x.shape[0]
    TM = min(512, S)
    NB = w_in_l.shape[2]

    def body(x_ref, g_ref, sc_ref, sh_ref, w_ref, ha_ref, hb_ref, p_ref):
        xv = x_ref[...]
        rstd = lax.rsqrt(jnp.mean(xv * xv, axis=-1, keepdims=True) + NORM_EPS)
        h = ((xv * rstd * g_ref[...]) * (1.0 + sc_ref[...]) + sh_ref[...]).astype(BF16)
        ha_ref[...] = h[:, :GW_SPLIT]
        hb_ref[...] = h[:, GW_SPLIT:]
        for j in range(NDEV):
            p_ref[:, j * NB:(j + 1) * NB] = jnp.dot(h, w_ref[j], preferred_element_type=F32)

    row = pl.BlockSpec((TM, D), lambda i: (i, 0))
    return _pcall(
        body, name="pre_proj", grid=(S // TM,),
        in_specs=[row, _vec(l), _vec(l, 1), _vec(l, 0), _full((NDEV, D, NB))],
        out_specs=[pl.BlockSpec((TM, GW_SPLIT), lambda i: (i, 0)), pl.BlockSpec((TM, D - GW_SPLIT), lambda i: (i, 0)),
                   pl.BlockSpec((TM, NDEV * NB), lambda i: (i, 0))],
        out_shape=[jax.ShapeDtypeStruct((S, GW_SPLIT), BF16), jax.ShapeDtypeStruct((S, D - GW_SPLIT), BF16),
                   jax.ShapeDtypeStruct((S, NDEV * NB), F32)],
        args=(x, P['pre_norm_g'], mod, mod, w_in_l), rider=rider)


def _taps(E):
    return [pltpu.roll(E, CONV_K - 1 - k, axis=0)[CONV_HALO:, :] for k in range(CONV_K - 1)] + [E[CONV_HALO:, :]]


def _conv(E, cw_ref, cb_ref):
    w = cw_ref[...]
    taps = _taps(E)
    acc = cb_ref[...] + taps[0] * w[0:1, :]
    for k in range(1, CONV_K):
        acc = acc + taps[k] * w[k:k + 1, :]
    return acc


CH_R, CH_C = 16, 512


def _chunks(T, fn):
    def step(c, carry):
        rows = pl.ds(pl.multiple_of(c * CH_R, CH_R), CH_R)
        for hf in range(D // CH_C):
            fn(rows, slice(hf * CH_C, (hf + 1) * CH_C), hf)
        return carry
    lax.fori_loop(0, T // CH_R, step, 0)


def _to_scan(ref, rows, hf, val):
    for q in range(CH_C // LANE):
        ref[hf * (CH_C // LANE) + q, rows, :] = val[:, q * LANE:(q + 1) * LANE]


def _scan(sa, sv, carry_ref, out_ref, reverse):
    NC, T, _ = sa.shape
    n8 = T // SUB
    rows = range(SUB - 2, -1, -1) if reverse else range(1, SUB)
    for cb in range(NC):
        r_in = SUB - 1 if reverse else 0
        Ap = sa[cb, pl.ds(r_in, n8, stride=SUB), :]
        Vp = sv[cb, pl.ds(r_in, n8, stride=SUB), :]
        for r in rows:
            Ar = sa[cb, pl.ds(r, n8, stride=SUB), :]
            Vp = sv[cb, pl.ds(r, n8, stride=SUB), :] + Ar * Vp
            Ap = Ar * Ap
            sa[cb, pl.ds(r, n8, stride=SUB), :] = Ap
            sv[cb, pl.ds(r, n8, stride=SUB), :] = Vp
    edge = 0 if reverse else SUB - 1

    def step(k, c):
        r0 = pl.multiple_of((n8 - 1 - k if reverse else k) * SUB, SUB)
        h = jnp.concatenate([sv[cb, pl.ds(r0, SUB), :] + sa[cb, pl.ds(r0, SUB), :] * c[:, cb * LANE:(cb + 1) * LANE]
                             for cb in range(NC)], axis=1)
        out_ref[pl.ds(r0, SUB), :] = h
        return jnp.broadcast_to(h[edge:edge + 1, :], (SUB, D))

    carry_ref[...] = lax.fori_loop(0, n8, step, carry_ref[...])


def _rnn_fwd(proj, P, l, rider=None):
    S = proj.shape[0]
    TB = min(256, S)

    def body(xr_ref, g_ref, cw_ref, cb_ref, wa_ref, ba_ref, wx_ref, bx_ref, lam_ref, hs_ref, y_ref,
             u_ref, r_ref, i_ref, a_ref, m_ref, xbuf, sa, sv, hc):
        @pl.when(pl.program_id(0) == 0)
        def _():
            xbuf[0:CONV_HALO, :] = jnp.zeros((CONV_HALO, D), F32)
            hc[...] = jnp.zeros((SUB, D), F32)
        xbuf[CONV_HALO:, :] = xr_ref[...]
        u = _conv(xbuf[...], cw_ref, cb_ref)
        xbuf[0:CONV_HALO, :] = xbuf[TB:TB + CONV_HALO, :]
        ub = u.astype(BF16)
        zr = jnp.concatenate([jnp.dot(ub[:, h * HD:(h + 1) * HD], wa_ref[h], preferred_element_type=F32)
                              for h in range(NHEAD)], axis=1)
        zi = jnp.concatenate([jnp.dot(ub[:, h * HD:(h + 1) * HD], wx_ref[h], preferred_element_type=F32)
                              for h in range(NHEAD)], axis=1)
        r = _sigmoid(zr + ba_ref[...])
        ig = _sigmoid(zi + bx_ref[...])
        log_a = r * (-LRU_C * jax.nn.softplus(-lam_ref[...]))
        a = jnp.exp(log_a)
        mult = jnp.sqrt(_one_minus_sq(a, log_a))
        v = mult * (ig * u)
        u_ref[...] = u
        r_ref[...] = r
        i_ref[...] = ig
        a_ref[...] = a
        m_ref[...] = mult
        for cb in range(D // LANE):
            sa[cb] = a[:, cb * LANE:(cb + 1) * LANE]
            sv[cb] = v[:, cb * LANE:(cb + 1) * LANE]
        _scan(sa, sv, hc, hs_ref, reverse=False)
        silu, _ = _silu_parts(g_ref[...])
        y_ref[...] = (hs_ref[...] * silu).astype(BF16)

    rowb = pl.BlockSpec((TB, D), lambda i: (i, 0))
    return _pcall(
        body, name="rnn_fwd", grid=(S // TB,),
        in_specs=[rowb, pl.BlockSpec((TB, D), lambda i: (i, 1)), _layer(l, (CONV_K, D)), _vec(l),
                  _layer(l, (NHEAD, HD, HD)), _vec(l), _layer(l, (NHEAD, HD, HD)), _vec(l), _vec(l)],
        out_specs=[rowb] * 7,
        out_shape=[jax.ShapeDtypeStruct((S, D), F32), jax.ShapeDtypeStruct((S, D), BF16)]
        + [jax.ShapeDtypeStruct((S, D), F32)] * 5,
        scratch_shapes=[pltpu.VMEM((TB + CONV_HALO, D), F32), pltpu.VMEM((D // LANE, TB, LANE), F32),
                        pltpu.VMEM((D // LANE, TB, LANE), F32), pltpu.VMEM((SUB, D), F32)],
        args=(proj, proj, P['conv_w'], P['conv_b'], P['gate_a_w'], P['gate_a_b'], P['gate_x_w'], P['gate_x_b'],
              P['lru_lambda']), rider=rider)


def _pooled(ebuf, t0, TB):
    tt = t0 + lax.broadcasted_iota(jnp.int32, (TB, 1), 0)
    pooled, inv = [], []
    for g, win in enumerate(WINS):
        Eg = ebuf[:, g * GD:(g + 1) * GD]
        L = Eg
        for lev in range(g + 1):
            L = L + pltpu.roll(L, 1 << lev, axis=0)
        icnt = 1.0 / jnp.minimum(tt + 1, win).astype(F32)
        pooled.append(L[POOL_HALO:, :] * icnt - Eg[POOL_HALO:, :])
        inv.append(icnt)
    return pooled, inv


def _pool_fwd(proj, P, l, rider=None):
    S = proj.shape[0]
    TB = min(512, S)

    def body(xp_ref, g_ref, pw_ref, pb_ref, ps_ref, y_ref, ebuf):
        i = pl.program_id(0)

        @pl.when(i == 0)
        def _():
            ebuf[0:POOL_HALO, :] = jnp.zeros((POOL_HALO, D), F32)
        ebuf[POOL_HALO:, :] = xp_ref[...]
        pooled, _ = _pooled(ebuf, i * TB, TB)
        ebuf[0:POOL_HALO, :] = ebuf[TB:TB + POOL_HALO, :]
        yp = jnp.concatenate([jnp.dot(pooled[g].astype(BF16), pw_ref[g], preferred_element_type=F32)
                              for g in range(NGRP)], axis=1) + pb_ref[...]
        silu, _ = _silu_parts(g_ref[...])
        y_ref[...] = (yp * ps_ref[...] * silu).astype(BF16)

    return _pcall(
        body, name="pool_fwd", grid=(S // TB,),
        in_specs=[pl.BlockSpec((TB, D), lambda i: (i, 2)), pl.BlockSpec((TB, D), lambda i: (i, 3)),
                  _layer(l, (NGRP, GD, GD)), _vec(l), _vec(l)],
        out_specs=[pl.BlockSpec((TB, D), lambda i: (i, 0))],
        out_shape=[jax.ShapeDtypeStruct((S, D), BF16)],
        scratch_shapes=[pltpu.VMEM((TB + POOL_HALO, D), F32)],
        args=(proj, proj, P['pool_w'], P['pool_b'], P['pool_scale']), rider=rider)


def _out_post(yr, yp, w_out_l, x, mod, P, l, target=None, rider=None):
    S = x.shape[0]
    TM = min(512, S)
    last = target is not None

    def body(*refs):
        if last:
            yr_ref, yp_ref, w_ref, x_ref, gate_ref, gp_ref, t_ref, y_ref, xo_ref, loss_ref = refs
        else:
            yr_ref, yp_ref, w_ref, x_ref, gate_ref, gp_ref, y_ref, xo_ref = refs
        acc = (jnp.dot(yr_ref[...], w_ref[0:D, :], preferred_element_type=F32)
               + jnp.dot(yp_ref[...], w_ref[D:2 * D, :], preferred_element_type=F32))
        y_ref[...] = acc
        rstd = lax.rsqrt(jnp.mean(acc * acc, axis=-1, keepdims=True) + NORM_EPS)
        xn = x_ref[...] + gate_ref[...] * (acc * rstd * gp_ref[...])
        if last:
            err = xn - t_ref[...]
            xo_ref[...] = err * (1.0 / D)

            @pl.when(pl.program_id(0) == 0)
            def _():
                loss_ref[...] = jnp.zeros((SUB, D), F32)
            loss_ref[...] += _rowsum8(err * err)
        else:
            xo_ref[...] = xn

    row = pl.BlockSpec((TM, D), lambda i: (i, 0))
    in_specs = [row, row, _full((2 * D, D)), row, _vec(l, 2), _vec(l)]
    out_specs = [row, row]
    out_shape = [jax.ShapeDtypeStruct((S, D), F32), jax.ShapeDtypeStruct((S, D), F32)]
    args = [yr, yp, w_out_l, x, mod, P['post_norm_g']]
    if last:
        in_specs.append(row)
        out_specs.append(_full((SUB, D)))
        out_shape.append(jax.ShapeDtypeStruct((SUB, D), F32))
        args.append(target)
    return _pcall(body, name="out_post_loss" if last else "out_post", grid=(S // TM,), in_specs=in_specs,
                  out_specs=out_specs, out_shape=out_shape, args=args, rider=rider)


def _out_bwd(dxo, y, yr, yp, w_out_l, mod, P, l, rider=None):
    S = y.shape[0]
    TM = min(512, S)
    nsteps = S // TM

    def body(dxo_ref, y_ref, yr_ref, yp_ref, w_ref, gate_ref, gp_ref, dyr_ref, dyp_ref, gw_ref, dgate_ref,
             dgp_ref, gw_acc, vacc):
        i = pl.program_id(0)

        @pl.when(i == 0)
        def _():
            gw_acc[...] = jnp.zeros_like(gw_acc)
            vacc[...] = jnp.zeros_like(vacc)
        yv = y_ref[...]
        dxo_v = dxo_ref[...]
        rstd = lax.rsqrt(jnp.mean(yv * yv, axis=-1, keepdims=True) + NORM_EPS)
        n = yv * rstd
        gp = gp_ref[...]
        vacc[0] += _rowsum8(dxo_v * (n * gp))
        drn = dxo_v * gate_ref[...]
        vacc[1] += _rowsum8(drn * n)
        dn = drn * gp
        dy = (rstd * (dn - n * jnp.mean(dn * n, axis=-1, keepdims=True))).astype(BF16)
        dyr_ref[...] = lax.dot_general(dy, w_ref[0:D, :], (((1,), (1,)), ((), ())), preferred_element_type=F32)
        dyp_ref[...] = lax.dot_general(dy, w_ref[D:2 * D, :], (((1,), (1,)), ((), ())),
                                       preferred_element_type=F32)
        gw_acc[0:D, :] += lax.dot_general(yr_ref[...], dy, (((0,), (0,)), ((), ())), preferred_element_type=F32)
        gw_acc[D:2 * D, :] += lax.dot_general(yp_ref[...], dy, (((0,), (0,)), ((), ())),
                                              preferred_element_type=F32)

        @pl.when(i == nsteps - 1)
        def _():
            gw_ref[...] = gw_acc[...].astype(BF16)
            dgate_ref[...] = _sum8(vacc[0])
            dgp_ref[...] = _sum8(vacc[1])

    row = pl.BlockSpec((TM, D), lambda i: (i, 0))
    return _pcall(
        body, name="out_bwd", grid=(nsteps,),
        in_specs=[row, row, row, row, _full((2 * D, D)), _vec(l, 2), _vec(l)],
        out_specs=[row, row, _full((2 * D, D)), _full((1, D)), _full((1, D))],
        out_shape=[jax.ShapeDtypeStruct((S, D), F32), jax.ShapeDtypeStruct((S, D), F32),
                   jax.ShapeDtypeStruct((2 * D, D), BF16), jax.ShapeDtypeStruct((1, D), F32),
                   jax.ShapeDtypeStruct((1, D), F32)],
        scratch_shapes=[pltpu.VMEM((2 * D, D), F32), pltpu.VMEM((2, SUB, D), F32)],
        args=(dxo, y, yr, yp, w_out_l, mod, P['post_norm_g']), rider=rider)


def _rnn_bwd(dyr, hs, fwd, proj, P, l, rider=None):
    S = proj.shape[0]
    TB = min(256, S)
    nb = S // TB
    TE = TB + CONV_HALO
    A_BA, A_BX, A_LAM, A_CB, A_CW = 0, 1, 2, 3, 4

    def blk(i):
        return nb - 1 - i

    def body(dyr_ref, hs_ref, hprev_ref, u_ref, r_ref, i_ref, a_ref, m_ref, xr_ref, g_ref, cw_ref,
             wa_ref, wx_ref, lam_ref, dxr_ref, dg_ref, gcw_ref, gcb_ref, gwa_ref, gba_ref, gwx_ref, gbx_ref,
             glam_ref, hbuf, abuf, dbuf, sa, sv, dh_ref, hp_ref, dzr_ref, dzi_ref, dhc, vacc):
        i = pl.program_id(0)
        first = blk(i) == 0

        @pl.when(i == 0)
        def _():
            abuf[TB:, :] = jnp.zeros((CONV_HALO, D), F32)
            dbuf[TB:, :] = jnp.zeros((CONV_HALO, D), F32)
            dhc[...] = jnp.zeros_like(dhc)
            vacc[...] = jnp.zeros_like(vacc)
            gwa_ref[...] = jnp.zeros_like(gwa_ref)
            gwx_ref[...] = jnp.zeros_like(gwx_ref)

        hbuf[0:CONV_HALO, :] = jnp.where(first, 0.0, hprev_ref[...])
        hbuf[CONV_HALO:, :] = hs_ref[...]
        hp_ref[...] = pltpu.roll(hbuf[...], 1, axis=0)[CONV_HALO:, :]
        abuf[0:TB, :] = a_ref[...]
        b = pltpu.roll(abuf[...], TE - 1, axis=0)[0:TB, :]
        for cb in range(D // LANE):
            sa[cb] = b[:, cb * LANE:(cb + 1) * LANE]
        abuf[TB:, :] = a_ref[0:CONV_HALO, :]

        def gate_bwd(rows, cs, hf):
            silu, dsilu = _silu_parts(g_ref[rows, cs])
            dy = dyr_ref[rows, cs]
            dg_ref[rows, cs] = (dy * hs_ref[rows, cs] * dsilu).astype(BF16)
            _to_scan(sv, rows, hf, dy * silu)
        _chunks(TB, gate_bwd)

        _scan(sa, sv, dhc, dh_ref, reverse=True)

        csp = -LRU_C * jax.nn.softplus(-lam_ref[...])

        def lru_bwd(rows, cs, hf):
            dh, a, ig, u, mult, r = dh_ref[rows, cs], a_ref[rows, cs], i_ref[rows, cs], u_ref[rows, cs], \
                m_ref[rows, cs], r_ref[rows, cs]
            dlog_a = dh * hp_ref[rows, cs] * a - (dh * ig * u) * (a * a) / mult
            dzr = dlog_a * csp[:, cs] * r * (1.0 - r)
            dzi = (dh * mult * u) * ig * (1.0 - ig)
            dzr_ref[rows, cs] = dzr.astype(BF16)
            dzi_ref[rows, cs] = dzi.astype(BF16)
            dbuf[rows, cs] = dh * mult * ig
            vacc[A_LAM, :, cs] += _rowsum8(dlog_a * r)
            vacc[A_BA, :, cs] += _rowsum8(dzr)
            vacc[A_BX, :, cs] += _rowsum8(dzi)
        _chunks(TB, lru_bwd)

        ub, dzrb, dzib = u_ref[...].astype(BF16), dzr_ref[...], dzi_ref[...]
        du_g = []
        for h in range(NHEAD):
            cs = slice(h * HD, (h + 1) * HD)
            gwa_ref[h] += lax.dot_general(ub[:, cs], dzrb[:, cs], (((0,), (0,)), ((), ())),
                                          preferred_element_type=F32)
            gwx_ref[h] += lax.dot_general(ub[:, cs], dzib[:, cs], (((0,), (0,)), ((), ())),
                                          preferred_element_type=F32)
            du_g.append(lax.dot_general(dzrb[:, cs], wa_ref[h], (((1,), (1,)), ((), ())),
                                        preferred_element_type=F32)
                        + lax.dot_general(dzib[:, cs], wx_ref[h], (((1,), (1,)), ((), ())),
                                          preferred_element_type=F32))
        du = dbuf[0:TB, :] + jnp.concatenate(du_g, axis=1)
        dbuf[0:TB, :] = du

        Dd = dbuf[...]
        w = cw_ref[...]
        xv = xr_ref[...]
        dx = du * w[CONV_K - 1:CONV_K, :]
        vacc[A_CB] += _rowsum8(du)
        vacc[A_CW + CONV_K - 1] += _rowsum8(xv * du)
        for k in range(CONV_K - 1):
            ahead = pltpu.roll(Dd, TE - (CONV_K - 1 - k), axis=0)[0:TB, :]
            dx = dx + ahead * w[k:k + 1, :]
            vacc[A_CW + k] += _rowsum8(xv * ahead)
        dxr_ref[...] = dx.astype(BF16)
        dbuf[TB:, :] = du[0:CONV_HALO, :]

        @pl.when(i == nb - 1)
        def _():
            gba_ref[...] = _sum8(vacc[A_BA])
            gbx_ref[...] = _sum8(vacc[A_BX])
            glam_ref[...] = _sum8(vacc[A_LAM]) * (LRU_C * _sigmoid(-lam_ref[...]))
            gcb_ref[...] = _sum8(vacc[A_CB])
            for k in range(CONV_K):
                gcw_ref[k:k + 1, :] = _sum8(vacc[A_CW + k])

    rowb = pl.BlockSpec((TB, D), lambda i: (blk(i), 0))
    halo = pl.BlockSpec((SUB, D), lambda i: (jnp.maximum(blk(i) * (TB // SUB) - 1, 0), 0))
    wspec = _full((NHEAD, HD, HD))
    wlay = _layer(l, (NHEAD, HD, HD))
    vec1 = _full((1, D))
    return _pcall(
        body, name="rnn_bwd", grid=(nb,),
        in_specs=[rowb, rowb, halo] + [rowb] * 5 + [rowb, pl.BlockSpec((TB, D), lambda i: (blk(i), 1)),
                                                    _layer(l, (CONV_K, D)), wlay, wlay, _vec(l)],
        out_specs=[rowb, rowb, _full((CONV_K, D)), vec1, wspec, vec1, wspec, vec1, vec1],
        out_shape=[jax.ShapeDtypeStruct((S, D), BF16), jax.ShapeDtypeStruct((S, D), BF16),
                   jax.ShapeDtypeStruct((CONV_K, D), F32), jax.ShapeDtypeStruct((1, D), F32),
                   jax.ShapeDtypeStruct((NHEAD, HD, HD), F32), jax.ShapeDtypeStruct((1, D), F32),
                   jax.ShapeDtypeStruct((NHEAD, HD, HD), F32), jax.ShapeDtypeStruct((1, D), F32),
                   jax.ShapeDtypeStruct((1, D), F32)],
        scratch_shapes=[pltpu.VMEM((TE, D), F32), pltpu.VMEM((TE, D), F32), pltpu.VMEM((TE, D), F32),
                        pltpu.VMEM((D // LANE, TB, LANE), F32), pltpu.VMEM((D // LANE, TB, LANE), F32),
                        pltpu.VMEM((TB, D), F32), pltpu.VMEM((TB, D), F32), pltpu.VMEM((TB, D), BF16),
                        pltpu.VMEM((TB, D), BF16), pltpu.VMEM((SUB, D), F32),
                        pltpu.VMEM((A_CW + CONV_K, SUB, D), F32)],
        args=(dyr, hs, hs, *fwd, proj, proj, P['conv_w'], P['gate_a_w'], P['gate_x_w'], P['lru_lambda']),
        rider=rider)


def _pool_bwd(dyp, proj, P, l, rider=None):
    S = proj.shape[0]
    TB = min(512, S)
    nb = S // TB
    TE = TB + POOL_HALO

    def blk(i):
        return nb - 1 - i

    def body(dy_ref, xp_ref, xprev_ref, g_ref, pw_ref, pb_ref, ps_ref, dxp_ref, dg_ref, gpw_ref, gpb_ref,
             gps_ref, ebuf, qbuf, vacc):
        i = pl.program_id(0)
        first = blk(i) == 0

        @pl.when(i == 0)
        def _():
            qbuf[TB:, :] = jnp.zeros((POOL_HALO, D), F32)
            vacc[...] = jnp.zeros_like(vacc)
            gpw_ref[...] = jnp.zeros_like(gpw_ref)

        ebuf[0:POOL_HALO, :] = jnp.where(first, 0.0, xprev_ref[...])
        ebuf[POOL_HALO:, :] = xp_ref[...]
        pooled, inv = _pooled(ebuf, blk(i) * TB, TB)
        pooled = [p.astype(BF16) for p in pooled]
        yp = jnp.concatenate([jnp.dot(pooled[g], pw_ref[g], preferred_element_type=F32)
                              for g in range(NGRP)], axis=1) + pb_ref[...]
        silu, dsilu = _silu_parts(g_ref[...])
        dy = dy_ref[...]
        ps = ps_ref[...]
        dyp_v = dy * ps * silu
        vacc[0] += _rowsum8(dy * yp * silu)
        vacc[1] += _rowsum8(dyp_v)
        dg_ref[...] = (dy * yp * ps * dsilu).astype(BF16)
        dypb = dyp_v.astype(BF16)
        for g in range(NGRP):
            cs = slice(g * GD, (g + 1) * GD)
            gpw_ref[g] += lax.dot_general(pooled[g], dypb[:, cs], (((0,), (0,)), ((), ())),
                                          preferred_element_type=F32)
            dpool = lax.dot_general(dypb[:, cs], pw_ref[g], (((1,), (1,)), ((), ())),
                                    preferred_element_type=F32)
            qbuf[0:TB, cs] = dpool * inv[g]
            L = qbuf[:, cs]
            for lev in range(g + 1):
                L = L + pltpu.roll(L, TE - (1 << lev), axis=0)
            dxp_ref[:, cs] = (L[0:TB, :] - dpool).astype(BF16)
        qbuf[TB:, :] = qbuf[0:POOL_HALO, :]

        @pl.when(i == nb - 1)
        def _():
            gps_ref[...] = _sum8(vacc[0])
            gpb_ref[...] = _sum8(vacc[1])

    rowb = pl.BlockSpec((TB, D), lambda i: (blk(i), 0))
    return _pcall(
        body, name="pool_bwd", grid=(nb,),
        in_specs=[rowb, pl.BlockSpec((TB, D), lambda i: (blk(i), 2)),
                  pl.BlockSpec((POOL_HALO, D), lambda i: (jnp.maximum(blk(i) * (TB // POOL_HALO) - 1, 0), 2)),
                  pl.BlockSpec((TB, D), lambda i: (blk(i), 3)), _layer(l, (NGRP, GD, GD)), _vec(l), _vec(l)],
        out_specs=[rowb, rowb, _full((NGRP, GD, GD)), _full((1, D)), _full((1, D))],
        out_shape=[jax.ShapeDtypeStruct((S, D), BF16), jax.ShapeDtypeStruct((S, D), BF16),
                   jax.ShapeDtypeStruct((NGRP, GD, GD), F32), jax.ShapeDtypeStruct((1, D), F32),
                   jax.ShapeDtypeStruct((1, D), F32)],
        scratch_shapes=[pltpu.VMEM((TE, D), F32), pltpu.VMEM((TE, D), F32), pltpu.VMEM((2, SUB, D), F32)],
        args=(dyp, proj, proj, proj, P['pool_w'], P['pool_b'], P['pool_scale']), rider=rider)


def _in_bwd(dq, w_in_l, x, dxo, P, mod, l, rider=None):
    S = x.shape[0]
    TM = min(256, S)
    NB = w_in_l.shape[2]
    nsteps = S // TM
    per_q = D // NB

    def body(d0, d1, d2, d3, w_ref, x_ref, dxo_ref, g_ref, sc_ref, dx_ref, dsh_ref, dsc_ref, dg_ref, vacc):
        i = pl.program_id(0)

        @pl.when(i == 0)
        def _():
            vacc[...] = jnp.zeros_like(vacc)
        dref = (d0, d1, d2, d3)
        dh = jnp.zeros((TM, D), F32)
        for j in range(NDEV):
            c0 = (j % per_q) * NB
            dh = dh + lax.dot_general(dref[j // per_q][:, c0:c0 + NB], w_ref[j], (((1,), (1,)), ((), ())),
                                      preferred_element_type=F32)
        xv = x_ref[...]
        rstd = lax.rsqrt(jnp.mean(xv * xv, axis=-1, keepdims=True) + NORM_EPS)
        xn = xv * rstd
        g, sc = g_ref[...], 1.0 + sc_ref[...]
        vacc[0] += _rowsum8(dh)
        vacc[1] += _rowsum8(dh * (xn * g))
        vacc[2] += _rowsum8(dh * sc * xn)
        dxn = dh * sc * g
        dx_ref[...] = dxo_ref[...] + rstd * (dxn - xn * jnp.mean(dxn * xn, axis=-1, keepdims=True))

        @pl.when(i == nsteps - 1)
        def _():
            dsh_ref[...] = _sum8(vacc[0])
            dsc_ref[...] = _sum8(vacc[1])
            dg_ref[...] = _sum8(vacc[2])

    row = pl.BlockSpec((TM, D), lambda i: (i, 0))
    return _pcall(
        body, name="in_bwd", grid=(nsteps,),
        in_specs=[row, row, row, row, _full((NDEV, D, NB)), row, row, _vec(l), _vec(l, 1)],
        out_specs=[row, _full((1, D)), _full((1, D)), _full((1, D))],
        out_shape=[jax.ShapeDtypeStruct((S, D), F32)] + [jax.ShapeDtypeStruct((1, D), F32)] * 3,
        scratch_shapes=[pltpu.VMEM((3, SUB, D), F32)],
        args=(*dq, w_in_l, x, dxo, P['pre_norm_g'], mod), rider=rider)


def _grad_w_in(h, dq, NB, rider=None):
    S, RH = h.shape
    TK = min(1024, S)
    nk = S // TK

    def body(h_ref, d0, d1, d2, d3, o_ref, acc):
        k = pl.program_id(0)

        @pl.when(k == 0)
        def _():
            acc[...] = jnp.zeros_like(acc)
        hv = h_ref[...]
        for q, d_ref in enumerate((d0, d1, d2, d3)):
            acc[:, q * D:(q + 1) * D] += lax.dot_general(hv, d_ref[...], (((0,), (0,)), ((), ())),
                                                         preferred_element_type=F32)

        @pl.when(k == nk - 1)
        def _():
            for j in range(NDEV):
                o_ref[j] = acc[:, j * NB:(j + 1) * NB].astype(BF16)

    row = pl.BlockSpec((TK, D), lambda k: (k, 0))
    return _pcall(
        body, name="grad_w_in", grid=(nk,),
        in_specs=[pl.BlockSpec((TK, RH), lambda k: (k, 0)), row, row, row, row],
        out_specs=[_full((NDEV, RH, NB))],
        out_shape=[jax.ShapeDtypeStruct((NDEV, RH, NB), BF16)],
        scratch_shapes=[pltpu.VMEM((RH, NQ * D), F32)],
        args=(h, *dq), rider=rider)


def _adamw_math(g, w, m, v):
    m2 = ADAM_B1 * m + (1.0 - ADAM_B1) * g
    v2 = ADAM_B2 * v + (1.0 - ADAM_B2) * (g * g)
    m_hat = m2 / (1.0 - ADAM_B1 ** ADAM_STEP)
    v_hat = v2 / (1.0 - ADAM_B2 ** ADAM_STEP)
    delta = -ADAM_LR * (m_hat / (jnp.sqrt(v_hat) + ADAM_EPS) + ADAM_WD * w)
    return delta, m2, v2


def _adamw(name, gs, w, m, v, TR, row0=0, into=None):
    L = len(gs)
    n, Rp, C = gs[0].shape
    R = w.shape[1]
    b0 = row0 // TR

    def body(*refs):
        g_refs = refs[:L]
        w_ref, m_ref, v_ref = refs[L:L + 3]
        go_ref, do_ref, mo_ref, vo_ref = refs[-4:]
        lay = pl.program_id(0)
        for li in range(L):
            @pl.when(lay == li)
            def _(li=li):
                g = g_refs[li][0].astype(F32)
                for s in range(1, n):
                    g = g + g_refs[li][s].astype(F32)
                delta, m2, v2 = _adamw_math(g, w_ref[...], m_ref[...], v_ref[...])
                go_ref[...] = g
                do_ref[...] = delta
                mo_ref[...] = m2
                vo_ref[...] = v2

    lrc = pl.BlockSpec((None, TR, C), lambda lay, r: (lay, r + b0, 0))
    g_specs = [pl.BlockSpec((n, TR, C), lambda lay, r, li=li: (0, jnp.where(lay == li, r, 0), 0))
               for li in range(L)]
    in_specs, args, aliases = g_specs + [lrc, lrc, lrc], [*gs, w, m, v], None
    if into is not None:
        aliases = {len(args) + k: k for k in range(4)}
        in_specs = in_specs + [pl.BlockSpec(memory_space=pl.ANY)] * 4
        args = args + list(into)
    return _pcall(
        body, name=name, grid=(L, Rp // TR), in_specs=in_specs, out_specs=[lrc] * 4,
        out_shape=[jax.ShapeDtypeStruct((L, R, C), F32)] * 4, args=args, aliases=aliases)


def _ada_adamw(c_all_t, dm, w, m, v, rider=None):
    L, _, nc = w.shape

    def body(c_ref, dm_ref, w_ref, m_ref, v_ref, go_ref, do_ref, mo_ref, vo_ref):
        cv = c_ref[...]
        ca = cv * jax.nn.sigmoid(cv)
        dmv = dm_ref[...]
        g = ca[:, 0:1] * dmv[0:1, :]
        for b in range(1, NDEV):
            g = g + ca[:, b:b + 1] * dmv[b:b + 1, :]
        delta, m2, v2 = _adamw_math(g, w_ref[...], m_ref[...], v_ref[...])
        go_ref[...] = g
        do_ref[...] = delta
        mo_ref[...] = m2
        vo_ref[...] = v2

    big = pl.BlockSpec((None, D, nc), lambda lay: (lay, 0, 0))
    return _pcall(
        body, name="ada_adamw", grid=(L,),
        in_specs=[_full((D, NDEV)), pl.BlockSpec((None, NDEV, nc), lambda lay: (lay, 0, 0)), big, big, big],
        out_specs=[big] * 4, out_shape=[jax.ShapeDtypeStruct((L, D, nc), F32)] * 4,
        args=(c_all_t, dm, w, m, v), rider=rider)


def _sum_slots(recv):
    n, R, C = recv.shape

    def body(r_ref, o_ref):
        acc = r_ref[0].astype(F32)
        for s in range(1, n):
            acc = acc + r_ref[s].astype(F32)
        o_ref[...] = acc

    return pl.pallas_call(body, name="sum_slots", out_shape=jax.ShapeDtypeStruct((R, C), F32),
                          compiler_params=_cparams())(recv)


def _pad_rows(a, rows):
    return jnp.pad(a, ((0, rows - a.shape[0]), (0, 0)))


def _pack_sharded_block(pool_w, pool_b, conv_w):
    return jnp.concatenate([pool_w.reshape(-1, PACK_C), _pad_rows(pool_b.reshape(-1, PACK_C), SUB),
                            _pad_rows(conv_w.reshape(-1, PACK_C), SUB)], axis=0)


def _unpack_sharded_block(p):
    n_pw = DEPTH * NGRP * (GD // NDEV)
    pool_w = p[:n_pw].reshape(DEPTH, NGRP, GD // NDEV, GD)
    pool_b = p[n_pw].reshape(DEPTH, NGRP, GD // NDEV)
    conv_w = p[n_pw + SUB:n_pw + SUB + DEPTH * CONV_K * (D // NDEV) // PACK_C].reshape(DEPTH, CONV_K, D // NDEV)
    return pool_w, pool_b, conv_w


def _blocks_of_full(pool_w, pool_b, conv_w):
    pw = pool_w.reshape(DEPTH, NGRP, NDEV, GD // NDEV, GD).transpose(2, 0, 1, 3, 4).reshape(NDEV, -1, PACK_C)
    pb = pool_b.reshape(DEPTH, NGRP, NDEV, GD // NDEV).transpose(2, 0, 1, 3).reshape(NDEV, -1, PACK_C)
    cw = conv_w.reshape(DEPTH, CONV_K, NDEV, D // NDEV).transpose(2, 0, 1, 3).reshape(NDEV, -1, PACK_C)
    pad = lambda a: jnp.pad(a, ((0, 0), (0, SUB - a.shape[1]), (0, 0)))
    return jnp.concatenate([pw, pad(pb), pad(cw)], axis=1)


def _full_of_blocks(p):
    n_pw = DEPTH * NGRP * (GD // NDEV)
    pool_w = p[:, :n_pw].reshape(NDEV, DEPTH, NGRP, GD // NDEV, GD).transpose(1, 2, 0, 3, 4)
    pool_b = p[:, n_pw].reshape(NDEV, DEPTH, NGRP, GD // NDEV).transpose(1, 2, 0, 3)
    n_cw = DEPTH * CONV_K * (D // NDEV) // PACK_C
    conv_w = p[:, n_pw + SUB:n_pw + SUB + n_cw].reshape(NDEV, DEPTH, CONV_K, D // NDEV).transpose(1, 2, 0, 3)
    return (pool_w.reshape(DEPTH, NGRP, GD, GD), pool_b.reshape(DEPTH, NGRP, GD),
            conv_w.reshape(DEPTH, CONV_K, D))


def _pack_replicated(t, keys, rows):
    p = jnp.concatenate([t[k].reshape(-1, PACK_C) for k in keys], axis=0)
    return _pad_rows(p, rows)


def _unpack_replicated(p, like, keys):
    out, r0 = {}, 0
    for k in keys:
        rows = like[k].size // PACK_C
        out[k] = p[r0:r0 + rows].reshape(like[k].shape)
        r0 += rows
    return out


def kernel(x, c, ada_w, ada_b, pre_norm_g, w_in, conv_w, conv_b, gate_a_w, gate_a_b, gate_x_w, gate_x_b, lru_lambda, pool_w, pool_b, pool_scale, w_out, post_norm_g, loss_target, m_ada_w, m_ada_b, m_pre_norm_g, m_w_in, m_conv_w, m_conv_b, m_gate_a_w, m_gate_a_b, m_gate_x_w, m_gate_x_b, m_lru_lambda, m_pool_w, m_pool_b, m_pool_scale, m_w_out, m_post_norm_g, v_ada_w, v_ada_b, v_pre_norm_g, v_w_in, v_conv_w, v_conv_b, v_gate_a_w, v_gate_a_b, v_gate_x_w, v_gate_x_b, v_lru_lambda, v_pool_w, v_pool_b, v_pool_scale, v_w_out, v_post_norm_g):
    W = dict(ada_w=ada_w, ada_b=ada_b, pre_norm_g=pre_norm_g, w_in=w_in, conv_w=conv_w, conv_b=conv_b,
             gate_a_w=gate_a_w, gate_a_b=gate_a_b, gate_x_w=gate_x_w, gate_x_b=gate_x_b, lru_lambda=lru_lambda,
             pool_w=pool_w, pool_b=pool_b, pool_scale=pool_scale, w_out=w_out, post_norm_g=post_norm_g)
    M = dict(ada_w=m_ada_w, ada_b=m_ada_b, pre_norm_g=m_pre_norm_g, w_in=m_w_in, conv_w=m_conv_w,
             conv_b=m_conv_b, gate_a_w=m_gate_a_w, gate_a_b=m_gate_a_b, gate_x_w=m_gate_x_w,
             gate_x_b=m_gate_x_b, lru_lambda=m_lru_lambda, pool_w=m_pool_w, pool_b=m_pool_b,
             pool_scale=m_pool_scale, w_out=m_w_out, post_norm_g=m_post_norm_g)
    V = dict(ada_w=v_ada_w, ada_b=v_ada_b, pre_norm_g=v_pre_norm_g, w_in=v_w_in, conv_w=v_conv_w,
             conv_b=v_conv_b, gate_a_w=v_gate_a_w, gate_a_b=v_gate_a_b, gate_x_w=v_gate_x_w,
             gate_x_b=v_gate_x_b, lru_lambda=v_lru_lambda, pool_w=v_pool_w, pool_b=v_pool_b,
             pool_scale=v_pool_scale, w_out=v_w_out, post_norm_g=v_post_norm_g)
    S = x.shape[1]
    me = 4 * lax.axis_index("x") + 2 * lax.axis_index("y") + lax.axis_index("c")
    xs = x.reshape(S, D)
    tgt = loss_target.reshape(S, D)
    nc = ada_w.shape[2]
    NB = w_in.shape[2]
    w_in_b, w_out_b = w_in.astype(BF16), w_out.astype(BF16)
    rows = lambda a: a.reshape(DEPTH, 1, D)
    P = dict(pre_norm_g=rows(pre_norm_g), conv_b=rows(conv_b), gate_a_b=rows(gate_a_b), gate_x_b=rows(gate_x_b),
             lru_lambda=rows(lru_lambda), pool_scale=rows(pool_scale), post_norm_g=rows(post_norm_g),
             gate_a_w=gate_a_w.astype(BF16), gate_x_w=gate_x_w.astype(BF16))

    c_slots, w_in0 = _exchange("gather_c_w_in0", _AllGatherVia([jnp.broadcast_to(c, (SUB, D)), w_in_b[0]]))
    c_all = c_slots[:, 0, :]
    (mod_slots,) = _exchange("gather_mod", _Direct(ag=[_mod_cols(c_all, ada_w)]))
    mod = lax.dynamic_index_in_dim(mod_slots, me, axis=1, keepdims=False)
    mod = (mod.reshape(NDEV, DEPTH, nc).transpose(1, 0, 2).reshape(DEPTH, 3 * D) + ada_b).reshape(DEPTH, 3, 1, D)

    w_in_all, w_out_all = [w_in0, None], [None, None]
    saved = []
    xl = xs
    flat = lambda w_slots: w_slots.reshape(2 * D, D)
    for l in range(DEPTH):
        rider = _AllGather2([_pack_sharded_block(pool_w, pool_b, conv_w), w_out_b[0]]) if l == 0 else None
        (h_a, h_b, proj), got = _pre_proj(xl, P, mod, w_in_all[l], l, rider=rider)
        if l == 0:
            pool_w_f, pool_b_f, conv_w_f = _full_of_blocks(got[0])
            P.update(conv_w=conv_w_f, pool_w=pool_w_f.astype(BF16), pool_b=rows(pool_b_f))
            w_out_all[0] = flat(got[1])
        rider = _AllGather2([w_in_b[1]]) if l == 0 else None
        (hs, yr, *fwd), got = _rnn_fwd(proj, P, l, rider=rider)
        if l == 0:
            w_in_all[1] = got[0]
        (yp,), _ = _pool_fwd(proj, P, l)
        if l == DEPTH - 1:
            (y, x_next, loss_acc), _ = _out_post(yr, yp, w_out_all[l], xl, mod, P, l, tgt)
        else:
            (y, x_next), got = _out_post(yr, yp, w_out_all[l], xl, mod, P, l, rider=_AllGather2([w_out_b[1]]))
            w_out_all[1] = flat(got[0])
        saved.append((xl, h_a, h_b, proj, hs, fwd, yr, yp, y))
        xl = x_next

    dxo = xl
    G = {k: [None] * DEPTH for k in WEIGHTS}
    dmod = [None] * DEPTH
    recv_in, recv_out = [[None, None] for _ in range(DEPTH)], [None] * DEPTH
    full = dict(conv_w=(CONV_K, D), pool_w=(NGRP, GD, GD), pool_b=(NGRP, GD))
    stack = lambda k: jnp.stack([g.reshape(full.get(k, W[k].shape[1:])) for g in G[k]])
    gw_bot_prev = None
    for l in reversed(range(DEPTH)):
        xin, h_a, h_b, proj, hs, fwd, yr, yp, y = saved[l]
        last_layer = gw_bot_prev is None
        rider = None if last_layer else _Direct(a2a=[gw_bot_prev])
        (dyr, dyp, gw_out, dgate, G['post_norm_g'][l]), got = _out_bwd(dxo, y, yr, yp, w_out_all[l], mod, P, l,
                                                                       rider=rider)
        if not last_layer:
            recv_in[l + 1][1] = got[0]
        rider = _Direct(a2a=[gw_out.reshape(NDEV, 2 * D // NDEV, D)])
        if last_layer:
            rider = _Both(rider, _AllGather2([loss_acc]))
        ((dxr, dgr, G['conv_w'][l], G['conv_b'][l], G['gate_a_w'][l], G['gate_a_b'][l], G['gate_x_w'][l],
          G['gate_x_b'][l], G['lru_lambda'][l]), got) = _rnn_bwd(dyr, hs, fwd, proj, P, l, rider=rider)
        recv_out[l] = got[0]
        if last_layer:
            loss = (0.5 / D) * jnp.sum(got[1])
        (dxp, dgp, G['pool_w'][l], G['pool_b'][l], G['pool_scale'][l]), _ = _pool_bwd(dyp, proj, P, l)
        dq = (dxr, dgr, dxp, dgp)
        if l > 0:
            (gw_top,), _ = _grad_w_in(h_a, dq, NB)
            (dxo, dshift, dscale, G['pre_norm_g'][l]), (recv_in[l][0],) = _in_bwd(
                dq, w_in_all[l], xin, dxo, P, mod, l, rider=_Direct(a2a=[gw_top]))
            (gw_bot_prev,), _ = _grad_w_in(h_b, dq, NB)
        else:
            Ge = {k: stack(k) for k in REP_EARLY + ['pool_w', 'pool_b', 'conv_w']}
            early = jnp.concatenate([_blocks_of_full(Ge['pool_w'], Ge['pool_b'], Ge['conv_w']),
                                     _pack_replicated(Ge, REP_EARLY, NDEV * PACK_ROWS).reshape(NDEV, PACK_ROWS, PACK_C)],
                                    axis=1).astype(BF16)
            (gw_top,), (early_recv,) = _grad_w_in(h_a, dq, NB, rider=_Direct(a2a=[early]))
            early_sum = _sum_slots(early_recv)
            (gw_bot,), (recv_in[l][0],) = _grad_w_in(h_b, dq, NB, rider=_Direct(a2a=[gw_top]))
            (dxo, dshift, dscale, G['pre_norm_g'][l]), (recv_in[l][1], early_all) = _in_bwd(
                dq, w_in_all[l], xin, dxo, P, mod, l,
                rider=_Both(_Direct(a2a=[gw_bot]), _AllGather2([early_sum[PACK_ROWS:]])))
        dmod[l] = jnp.concatenate([dshift, dscale, dgate], axis=1)
    grad_x = dxo.reshape(x.shape)

    Gl = dict(ada_b=jnp.concatenate(dmod, axis=0), pre_norm_g=stack('pre_norm_g'))
    (late_slots,) = _exchange("gather_late", _AllGather2([_pack_replicated(Gl, REP_LATE, LATE_PACK_ROWS)]))
    late_sum = _sum_slots(late_slots)
    dmod_all = late_slots[:, :DEPTH * 3 * D // PACK_C]

    out = {}
    first, _ = _adamw("adamw_w_in_a", [recv_in[l][0] for l in range(DEPTH)], w_in, M['w_in'], V['w_in'], LANE)
    out['w_in'], _ = _adamw("adamw_w_in_b", [recv_in[l][1] for l in range(DEPTH)], w_in, M['w_in'], V['w_in'], LANE,
                            row0=GW_SPLIT, into=first)
    out['w_out'], _ = _adamw("adamw_w_out", recv_out, w_out, M['w_out'], V['w_out'], 256)
    dm = lax.dynamic_slice_in_dim(dmod_all.reshape(NDEV, DEPTH, 3 * D), me * nc, nc, axis=2)
    out['ada_w'], _ = _ada_adamw(c_all.T, dm.transpose(1, 0, 2), ada_w, M['ada_w'], V['ada_w'])
    g_small = jnp.concatenate([early_sum[:PACK_ROWS], early_all.reshape(NDEV * PACK_ROWS, PACK_C), late_sum],
                              axis=0)

    def packs(T):
        return jnp.concatenate([_pack_sharded_block(T['pool_w'], T['pool_b'], T['conv_w']),
                                _pack_replicated(T, REP_EARLY, NDEV * PACK_ROWS),
                                _pack_replicated(T, REP_LATE, LATE_PACK_ROWS)], axis=0)[None]
    res_small, _ = _adamw("adamw_small", [g_small[None]], packs(W), packs(M), packs(V), g_small.shape[0] // 2)
    n_early = (1 + NDEV) * PACK_ROWS
    for idx in range(4):
        p = res_small[idx][0]
        pw_, pb_, cw_ = _unpack_sharded_block(p[:PACK_ROWS])
        rep = _unpack_replicated(p[PACK_ROWS:n_early], W, REP_EARLY)
        rep.update(_unpack_replicated(p[n_early:], W, REP_LATE))
        rep.update(pool_w=pw_, pool_b=pb_, conv_w=cw_)
        for k, a in rep.items():
            out.setdefault(k, [None] * 4)[idx] = a
    for k in ('w_in', 'w_out', 'ada_w'):
        out[k] = [a.reshape(W[k].shape) for a in out[k]]

    return (loss, grad_x, *[out[k][0] for k in WEIGHTS], *[out[k][1] for k in WEIGHTS],
            *[out[k][2] for k in WEIGHTS], *[out[k][3] for k in WEIGHTS])
```
